```python
import jax, jax.numpy as jnp
from jax import lax
import numpy as np

D_MODEL = 1024
BATCH = 8
SEQ = 16384
DEPTH = 4

N_MIXERS = 3
EXPAND = 2
D_INNER = EXPAND * D_MODEL
CHUNK = 64

GDN_HEAD_DIM = 128
GDN_V_HEADS = D_INNER // GDN_HEAD_DIM
GDN_QK_HEADS = GDN_V_HEADS // 2
GDN_QK_DIM = GDN_QK_HEADS * GDN_HEAD_DIM
GDN_V_DIM = GDN_V_HEADS * GDN_HEAD_DIM
GDN_CONV = 4
GDN_CONV_DIM = 2 * GDN_QK_DIM + GDN_V_DIM
GDN_IN = GDN_CONV_DIM + GDN_V_DIM + 2 * GDN_V_HEADS

SC_WIDTH = D_INNER
SC_CONV = 3
SC_IN = 4 * SC_WIDTH

SSD_HEAD_DIM = 64
SSD_HEADS = D_INNER // SSD_HEAD_DIM
SSD_GROUPS = 4
SSD_STATE = 128
SSD_CONV = 4
SSD_CONV_DIM = D_INNER + 2 * SSD_GROUPS * SSD_STATE
SSD_IN = D_INNER + SSD_CONV_DIM + SSD_HEADS

DEEPNORM_ALPHA = (2 * DEPTH) ** 0.25
DEEPNORM_BETA = (8 * DEPTH) ** -0.25
RMS_EPS = 1e-6
LN_EPS = 1e-5
L2_EPS = 1e-6

kernel_name = 'hybrid_gdn_shortconv_ssd_deepnorm'


def causal_depthwise_conv(u, w):
    width, ch = w.shape
    return lax.conv_general_dilated(u, w[:, None, :], window_strides=(1,), padding=[(width - 1, 0)],
                                    dimension_numbers=('NWC', 'WIO', 'NWC'), feature_group_count=ch)


def rms_normalize(u):
    uf = u.astype(jnp.float32)
    return uf * lax.rsqrt(jnp.mean(uf * uf, axis=-1, keepdims=True) + RMS_EPS)


def l2_normalize(u):
    return u * lax.rsqrt(jnp.sum(u * u, axis=-1, keepdims=True) + L2_EPS)


def layer_norm(u, g, b):
    uf = u.astype(jnp.float32)
    mu = jnp.mean(uf, axis=-1, keepdims=True)
    var = jnp.mean(jnp.square(uf - mu), axis=-1, keepdims=True)
    return ((uf - mu) * lax.rsqrt(var + LN_EPS)).astype(u.dtype) * g + b


def chunked_gated_delta_rule(q, k, v, g, beta):
    b, l, h, dk = q.shape
    dv = v.shape[-1]
    n = l // CHUNK
    to_blocks = lambda t: t.reshape(b, n, CHUNK, h, t.shape[-1]).transpose(1, 0, 3, 2, 4)
    q, k, v = to_blocks(q), to_blocks(k), to_blocks(v)
    g = g.reshape(b, n, CHUNK, h).transpose(1, 0, 3, 2)
    beta = beta.reshape(b, n, CHUNK, h).transpose(1, 0, 3, 2)
    g_cum = jnp.cumsum(g, axis=-1)
    causal = jnp.tril(jnp.ones((CHUNK, CHUNK), dtype=bool))
    strict = jnp.tril(jnp.ones((CHUNK, CHUNK), dtype=bool), -1)
    decay = jnp.exp(jnp.where(causal, g_cum[..., :, None] - g_cum[..., None, :], -jnp.inf))
    k_beta = k * beta[..., None]
    a_strict = jnp.where(strict, jnp.einsum('nbhcd,nbhmd->nbhcm', k_beta, k) * decay, 0.0)
    rhs = jnp.concatenate([v * beta[..., None], k_beta * jnp.exp(g_cum)[..., None]], axis=-1)
    solved = lax.linalg.triangular_solve(a_strict, rhs, left_side=True, lower=True, unit_diagonal=True)
    u_vals, w_keys = solved[..., :dv], solved[..., dv:]
    attn_intra = jnp.einsum('nbhcd,nbhmd->nbhcm', q, k) * decay
    q_decay = q * jnp.exp(g_cum)[..., None]
    k_tail = k * jnp.exp(g_cum[..., -1:] - g_cum)[..., None]
    chunk_decay = jnp.exp(g_cum[..., -1])

    def step(state, inp):
        qd, wk, uv, att, kt, cd = inp
        v_new = uv - jnp.einsum('bhcd,bhde->bhce', wk, state)
        out = jnp.einsum('bhcd,bhde->bhce', qd, state) + jnp.einsum('bhcm,bhme->bhce', att, v_new)
        state = state * cd[..., None, None] + jnp.einsum('bhcd,bhce->bhde', kt, v_new)
        return state, out

    s0 = jnp.zeros((b, h, dk, dv), dtype=q.dtype)
    _, o = lax.scan(step, s0, (q_decay, w_keys, u_vals, attn_intra, k_tail, chunk_decay))
    return o.transpose(1, 0, 3, 2, 4).reshape(b, l, h, dv)


def gated_deltanet_mixer(x, w_in, conv_w, a_log, dt_bias, norm_w, w_out):
    b, l, _ = x.shape
    qkv, z, beta_raw, a_raw = jnp.split(
        x @ w_in, [GDN_CONV_DIM, GDN_CONV_DIM + GDN_V_DIM, GDN_CONV_DIM + GDN_V_DIM + GDN_V_HEADS], axis=-1)
    qkv = jax.nn.silu(causal_depthwise_conv(qkv, conv_w)).astype(jnp.float32)
    q, k, v = jnp.split(qkv, [GDN_QK_DIM, 2 * GDN_QK_DIM], axis=-1)
    q = l2_normalize(q.reshape(b, l, GDN_QK_HEADS, GDN_HEAD_DIM)) * (GDN_HEAD_DIM ** -0.5)
    k = l2_normalize(k.reshape(b, l, GDN_QK_HEADS, GDN_HEAD_DIM))
    q = jnp.repeat(q, GDN_V_HEADS // GDN_QK_HEADS, axis=2)
    k = jnp.repeat(k, GDN_V_HEADS // GDN_QK_HEADS, axis=2)
    v = v.reshape(b, l, GDN_V_HEADS, GDN_HEAD_DIM)
    beta = jax.nn.sigmoid(beta_raw.astype(jnp.float32))
    g = -jnp.exp(a_log.astype(jnp.float32)) * jax.nn.softplus(a_raw.astype(jnp.float32) + dt_bias.astype(jnp.float32))
    o = chunked_gated_delta_rule(q, k, v, g, beta)
    z = z.reshape(b, l, GDN_V_HEADS, GDN_HEAD_DIM)
    o = rms_normalize(o).astype(x.dtype) * norm_w * jax.nn.silu(z)
    return o.reshape(b, l, GDN_V_DIM) @ w_out


def short_conv_mixer(x, w_in, conv_w, w_out):
    h, b_gate, c_gate, z = jnp.split(x @ w_in, 4, axis=-1)
    y = b_gate * causal_depthwise_conv(c_gate * h, conv_w)
    return (y * jax.nn.silu(z)) @ w_out


def ssd_chunked(x, dt, a, b_mat, c_mat):
    bsz, l, h, p = x.shape
    g, s = b_mat.shape[2], b_mat.shape[3]
    e = h // g
    n = l // CHUNK
    xdt = (x * dt[..., None]).reshape(bsz, n, CHUNK, g, e, p).transpose(1, 0, 3, 4, 2, 5)
    a_cum = jnp.cumsum((dt * a).reshape(bsz, n, CHUNK, g, e).transpose(1, 0, 3, 4, 2), axis=-1)
    bm = b_mat.reshape(bsz, n, CHUNK, g, s).transpose(1, 0, 3, 2, 4)
    cm = c_mat.reshape(bsz, n, CHUNK, g, s).transpose(1, 0, 3, 2, 4)
    causal = jnp.tril(jnp.ones((CHUNK, CHUNK), dtype=bool))
    seg = jnp.exp(jnp.where(causal, a_cum[..., :, None] - a_cum[..., None, :], -jnp.inf))
    cb = jnp.einsum('nbgcs,nbgms->nbgcm', cm, bm)
    y_diag = jnp.einsum('nbgecm,nbgemp->nbgecp', seg * cb[:, :, :, None], xdt)
    states = jnp.einsum('nbgcs,nbgecp->nbgeps', bm, xdt * jnp.exp(a_cum[..., -1:] - a_cum)[..., None])
    c_decay = jnp.exp(a_cum)
    chunk_decay = jnp.exp(a_cum[..., -1])

    def step(state, inp):
        st, c_n, cdec, tot = inp
        y_off = jnp.einsum('bgcs,bgeps,bgec->bgecp', c_n, state, cdec)
        state = state * tot[..., None, None] + st
        return state, y_off

    s0 = jnp.zeros((bsz, g, e, p, s), dtype=x.dtype)
    _, y_off = lax.scan(step, s0, (states, cm, c_decay, chunk_decay))
    y = y_diag + y_off
    return y.transpose(1, 0, 4, 2, 3, 5).reshape(bsz, l, h, p)


def mamba2_mixer(x, w_in, conv_w, conv_b, a_log, dt_bias, d_skip, norm_w, w_out):
    b, l, _ = x.shape
    z, xbc, dt = jnp.split(x @ w_in, [D_INNER, D_INNER + SSD_CONV_DIM], axis=-1)
    xbc = jax.nn.silu(causal_depthwise_conv(xbc, conv_w) + conv_b).astype(jnp.float32)
    xs, bm, cm = jnp.split(xbc, [D_INNER, D_INNER + SSD_GROUPS * SSD_STATE], axis=-1)
    xs = xs.reshape(b, l, SSD_HEADS, SSD_HEAD_DIM)
    dt = jax.nn.softplus(dt.astype(jnp.float32) + dt_bias.astype(jnp.float32))
    a = -jnp.exp(a_log.astype(jnp.float32))
    y = ssd_chunked(xs, dt, a, bm.reshape(b, l, SSD_GROUPS, SSD_STATE), cm.reshape(b, l, SSD_GROUPS, SSD_STATE))
    y = y + d_skip.astype(jnp.float32)[:, None] * xs
    y = y.reshape(b, l, D_INNER) * jax.nn.silu(z.astype(jnp.float32))
    y = rms_normalize(y.reshape(b, l, SSD_GROUPS, D_INNER // SSD_GROUPS)).reshape(b, l, D_INNER)
    return (y.astype(x.dtype) * norm_w) @ w_out


def _fwd_setup_inputs(seed: int = 0) -> dict:
    key = jax.random.key(seed)
    ks = jax.random.split(key, 24)
    na = (DEPTH + 2) // N_MIXERS
    nb = (DEPTH + 1) // N_MIXERS
    nc = DEPTH // N_MIXERS
    nrm = lambda k, shape, scale: scale * jax.random.normal(k, shape, dtype=jnp.float32)

    def dt_bias_init(k, shape):
        dt = jnp.exp(jax.random.uniform(k, shape, minval=np.log(1e-3), maxval=np.log(1e-1)))
        return dt + jnp.log(-jnp.expm1(-dt))

    def a_log_init(k, shape):
        return jnp.log(jax.random.uniform(k, shape, minval=1.0, maxval=16.0))

    out_scale = DEEPNORM_BETA * D_INNER ** -0.5
    return {
        'x': nrm(ks[0], (BATCH, SEQ, D_MODEL), 1.0),
        'gdn_w_in': nrm(ks[1], (na, D_MODEL, GDN_IN), D_MODEL ** -0.5),
        'gdn_conv_w': nrm(ks[2], (na, GDN_CONV, GDN_CONV_DIM), GDN_CONV ** -0.5),
        'gdn_a_log': a_log_init(ks[3], (na, GDN_V_HEADS)),
        'gdn_dt_bias': dt_bias_init(ks[4], (na, GDN_V_HEADS)),
        'gdn_norm_w': 1.0 + nrm(ks[5], (na, GDN_HEAD_DIM), 0.02),
        'gdn_w_out': nrm(ks[6], (na, GDN_V_DIM, D_MODEL), out_scale),
        'sc_w_in': nrm(ks[7], (nb, D_MODEL, SC_IN), D_MODEL ** -0.5),
        'sc_conv_w': nrm(ks[8], (nb, SC_CONV, SC_WIDTH), SC_CONV ** -0.5),
        'sc_w_out': nrm(ks[9], (nb, SC_WIDTH, D_MODEL), out_scale),
        'ssd_w_in': nrm(ks[10], (nc, D_MODEL, SSD_IN), D_MODEL ** -0.5),
        'ssd_conv_w': nrm(ks[11], (nc, SSD_CONV, SSD_CONV_DIM), SSD_CONV ** -0.5),
        'ssd_conv_b': nrm(ks[12], (nc, SSD_CONV_DIM), 0.02),
        'ssd_a_log': a_log_init(ks[13], (nc, SSD_HEADS)),
        'ssd_dt_bias': dt_bias_init(ks[14], (nc, SSD_HEADS)),
        'ssd_d_skip': 1.0 + nrm(ks[15], (nc, SSD_HEADS), 0.1),
        'ssd_norm_w': 1.0 + nrm(ks[16], (nc, D_INNER), 0.02),
        'ssd_w_out': nrm(ks[17], (nc, D_INNER, D_MODEL), out_scale),
        'ln_g': 1.0 + nrm(ks[18], (DEPTH, D_MODEL), 0.02),
        'ln_b': nrm(ks[19], (DEPTH, D_MODEL), 0.02),
    }


def _fwd_reference(x, gdn_w_in, gdn_conv_w, gdn_a_log, gdn_dt_bias, gdn_norm_w, gdn_w_out,
              sc_w_in, sc_conv_w, sc_w_out,
              ssd_w_in, ssd_conv_w, ssd_conv_b, ssd_a_log, ssd_dt_bias, ssd_d_skip, ssd_norm_w, ssd_w_out,
              ln_g, ln_b):
    for i in range(DEPTH):
        j = i // N_MIXERS
        kind = i % N_MIXERS
        if kind == 0:
            y = gated_deltanet_mixer(x, gdn_w_in[j], gdn_conv_w[j], gdn_a_log[j], gdn_dt_bias[j],
                                     gdn_norm_w[j], gdn_w_out[j])
        elif kind == 1:
            y = short_conv_mixer(x, sc_w_in[j], sc_conv_w[j], sc_w_out[j])
        else:
            y = mamba2_mixer(x, ssd_w_in[j], ssd_conv_w[j], ssd_conv_b[j], ssd_a_log[j], ssd_dt_bias[j],
                             ssd_d_skip[j], ssd_norm_w[j], ssd_w_out[j])
        x = layer_norm(DEEPNORM_ALPHA * x + y, ln_g[i], ln_b[i])
    return x


import jax as _jax
import jax.numpy as _jnp

TWIN_FORMAT = 'train_step'
FWD_PARAMS = ['x', 'gdn_w_in', 'gdn_conv_w', 'gdn_a_log', 'gdn_dt_bias', 'gdn_norm_w', 'gdn_w_out', 'sc_w_in', 'sc_conv_w', 'sc_w_out', 'ssd_w_in', 'ssd_conv_w', 'ssd_conv_b', 'ssd_a_log', 'ssd_dt_bias', 'ssd_d_skip', 'ssd_norm_w', 'ssd_w_out', 'ln_g', 'ln_b']
TWIN_WEIGHTS = ['gdn_w_in', 'gdn_conv_w', 'gdn_a_log', 'gdn_dt_bias', 'gdn_norm_w', 'gdn_w_out', 'sc_w_in', 'sc_conv_w', 'sc_w_out', 'ssd_w_in', 'ssd_conv_w', 'ssd_conv_b', 'ssd_a_log', 'ssd_dt_bias', 'ssd_d_skip', 'ssd_norm_w', 'ssd_w_out', 'ln_g', 'ln_b']
TWIN_DIFF_INPUT = 'x'
TWIN_INPUTS = ['x', 'gdn_w_in', 'gdn_conv_w', 'gdn_a_log', 'gdn_dt_bias', 'gdn_norm_w', 'gdn_w_out', 'sc_w_in', 'sc_conv_w', 'sc_w_out', 'ssd_w_in', 'ssd_conv_w', 'ssd_conv_b', 'ssd_a_log', 'ssd_dt_bias', 'ssd_d_skip', 'ssd_norm_w', 'ssd_w_out', 'ln_g', 'ln_b', 'loss_target', 'm_gdn_w_in', 'm_gdn_conv_w', 'm_gdn_a_log', 'm_gdn_dt_bias', 'm_gdn_norm_w', 'm_gdn_w_out', 'm_sc_w_in', 'm_sc_conv_w', 'm_sc_w_out', 'm_ssd_w_in', 'm_ssd_conv_w', 'm_ssd_conv_b', 'm_ssd_a_log', 'm_ssd_dt_bias', 'm_ssd_d_skip', 'm_ssd_norm_w', 'm_ssd_w_out', 'm_ln_g', 'm_ln_b', 'v_gdn_w_in', 'v_gdn_conv_w', 'v_gdn_a_log', 'v_gdn_dt_bias', 'v_gdn_norm_w', 'v_gdn_w_out', 'v_sc_w_in', 'v_sc_conv_w', 'v_sc_w_out', 'v_ssd_w_in', 'v_ssd_conv_w', 'v_ssd_conv_b', 'v_ssd_a_log', 'v_ssd_dt_bias', 'v_ssd_d_skip', 'v_ssd_norm_w', 'v_ssd_w_out', 'v_ln_g', 'v_ln_b']
TWIN_OUTPUTS = ['loss', 'grad_x', 'grad_gdn_w_in', 'grad_gdn_conv_w', 'grad_gdn_a_log', 'grad_gdn_dt_bias', 'grad_gdn_norm_w', 'grad_gdn_w_out', 'grad_sc_w_in', 'grad_sc_conv_w', 'grad_sc_w_out', 'grad_ssd_w_in', 'grad_ssd_conv_w', 'grad_ssd_conv_b', 'grad_ssd_a_log', 'grad_ssd_dt_bias', 'grad_ssd_d_skip', 'grad_ssd_norm_w', 'grad_ssd_w_out', 'grad_ln_g', 'grad_ln_b', 'delta_gdn_w_in', 'delta_gdn_conv_w', 'delta_gdn_a_log', 'delta_gdn_dt_bias', 'delta_gdn_norm_w', 'delta_gdn_w_out', 'delta_sc_w_in', 'delta_sc_conv_w', 'delta_sc_w_out', 'delta_ssd_w_in', 'delta_ssd_conv_w', 'delta_ssd_conv_b', 'delta_ssd_a_log', 'delta_ssd_dt_bias', 'delta_ssd_d_skip', 'delta_ssd_norm_w', 'delta_ssd_w_out', 'delta_ln_g', 'delta_ln_b', 'new_m_gdn_w_in', 'new_m_gdn_conv_w', 'new_m_gdn_a_log', 'new_m_gdn_dt_bias', 'new_m_gdn_norm_w', 'new_m_gdn_w_out', 'new_m_sc_w_in', 'new_m_sc_conv_w', 'new_m_sc_w_out', 'new_m_ssd_w_in', 'new_m_ssd_conv_w', 'new_m_ssd_conv_b', 'new_m_ssd_a_log', 'new_m_ssd_dt_bias', 'new_m_ssd_d_skip', 'new_m_ssd_norm_w', 'new_m_ssd_w_out', 'new_m_ln_g', 'new_m_ln_b', 'new_v_gdn_w_in', 'new_v_gdn_conv_w', 'new_v_gdn_a_log', 'new_v_gdn_dt_bias', 'new_v_gdn_norm_w', 'new_v_gdn_w_out', 'new_v_sc_w_in', 'new_v_sc_conv_w', 'new_v_sc_w_out', 'new_v_ssd_w_in', 'new_v_ssd_conv_w', 'new_v_ssd_conv_b', 'new_v_ssd_a_log', 'new_v_ssd_dt_bias', 'new_v_ssd_d_skip', 'new_v_ssd_norm_w', 'new_v_ssd_w_out', 'new_v_ln_g', 'new_v_ln_b']
TWIN_LEAF_KINDS = {'loss': 'loss', 'grad_x': 'grad_x', 'grad_gdn_w_in': 'grad_w', 'grad_gdn_conv_w': 'grad_w', 'grad_gdn_a_log': 'grad_w', 'grad_gdn_dt_bias': 'grad_w', 'grad_gdn_norm_w': 'grad_w', 'grad_gdn_w_out': 'grad_w', 'grad_sc_w_in': 'grad_w', 'grad_sc_conv_w': 'grad_w', 'grad_sc_w_out': 'grad_w', 'grad_ssd_w_in': 'grad_w', 'grad_ssd_conv_w': 'grad_w', 'grad_ssd_conv_b': 'grad_w', 'grad_ssd_a_log': 'grad_w', 'grad_ssd_dt_bias': 'grad_w', 'grad_ssd_d_skip': 'grad_w', 'grad_ssd_norm_w': 'grad_w', 'grad_ssd_w_out': 'grad_w', 'grad_ln_g': 'grad_w', 'grad_ln_b': 'grad_w', 'delta_gdn_w_in': 'delta_w', 'delta_gdn_conv_w': 'delta_w', 'delta_gdn_a_log': 'delta_w', 'delta_gdn_dt_bias': 'delta_w', 'delta_gdn_norm_w': 'delta_w', 'delta_gdn_w_out': 'delta_w', 'delta_sc_w_in': 'delta_w', 'delta_sc_conv_w': 'delta_w', 'delta_sc_w_out': 'delta_w', 'delta_ssd_w_in': 'delta_w', 'delta_ssd_conv_w': 'delta_w', 'delta_ssd_conv_b': 'delta_w', 'delta_ssd_a_log': 'delta_w', 'delta_ssd_dt_bias': 'delta_w', 'delta_ssd_d_skip': 'delta_w', 'delta_ssd_norm_w': 'delta_w', 'delta_ssd_w_out': 'delta_w', 'delta_ln_g': 'delta_w', 'delta_ln_b': 'delta_w', 'new_m_gdn_w_in': 'new_m', 'new_m_gdn_conv_w': 'new_m', 'new_m_gdn_a_log': 'new_m', 'new_m_gdn_dt_bias': 'new_m', 'new_m_gdn_norm_w': 'new_m', 'new_m_gdn_w_out': 'new_m', 'new_m_sc_w_in': 'new_m', 'new_m_sc_conv_w': 'new_m', 'new_m_sc_w_out': 'new_m', 'new_m_ssd_w_in': 'new_m', 'new_m_ssd_conv_w': 'new_m', 'new_m_ssd_conv_b': 'new_m', 'new_m_ssd_a_log': 'new_m', 'new_m_ssd_dt_bias': 'new_m', 'new_m_ssd_d_skip': 'new_m', 'new_m_ssd_norm_w': 'new_m', 'new_m_ssd_w_out': 'new_m', 'new_m_ln_g': 'new_m', 'new_m_ln_b': 'new_m', 'new_v_gdn_w_in': 'new_v', 'new_v_gdn_conv_w': 'new_v', 'new_v_gdn_a_log': 'new_v', 'new_v_gdn_dt_bias': 'new_v', 'new_v_gdn_norm_w': 'new_v', 'new_v_gdn_w_out': 'new_v', 'new_v_sc_w_in': 'new_v', 'new_v_sc_conv_w': 'new_v', 'new_v_sc_w_out': 'new_v', 'new_v_ssd_w_in': 'new_v', 'new_v_ssd_conv_w': 'new_v', 'new_v_ssd_conv_b': 'new_v', 'new_v_ssd_a_log': 'new_v', 'new_v_ssd_dt_bias': 'new_v', 'new_v_ssd_d_skip': 'new_v', 'new_v_ssd_norm_w': 'new_v', 'new_v_ssd_w_out': 'new_v', 'new_v_ln_g': 'new_v', 'new_v_ln_b': 'new_v'}


def _forward(args):
    return _fwd_reference(*[args[k] for k in FWD_PARAMS])


def _output_shape():
    def fwd():
        inp = _fwd_setup_inputs(0)
        return _fwd_reference(*[inp[k] for k in FWD_PARAMS])
    out = _jax.eval_shape(fwd)
    return out.shape, out.dtype

N_MICROBATCH = 1
ADAM_LR = 0.001
ADAM_B1 = 0.9
ADAM_B2 = 0.999
ADAM_EPS = 1e-08
ADAM_WD = 0.01
ADAM_STEP = 10
PER_EXAMPLE_BATCH_AXIS = {'x': 0, 'loss_target': 0}
SHARED_INPUTS = []
_WEIGHT_DTYPES = {'gdn_w_in': _jnp.float32, 'gdn_conv_w': _jnp.float32, 'gdn_a_log': _jnp.float32, 'gdn_dt_bias': _jnp.float32, 'gdn_norm_w': _jnp.float32, 'gdn_w_out': _jnp.float32, 'sc_w_in': _jnp.float32, 'sc_conv_w': _jnp.float32, 'sc_w_out': _jnp.float32, 'ssd_w_in': _jnp.float32, 'ssd_conv_w': _jnp.float32, 'ssd_conv_b': _jnp.float32, 'ssd_a_log': _jnp.float32, 'ssd_dt_bias': _jnp.float32, 'ssd_d_skip': _jnp.float32, 'ssd_norm_w': _jnp.float32, 'ssd_w_out': _jnp.float32, 'ln_g': _jnp.float32, 'ln_b': _jnp.float32}
MOMENT_SCALE = {'gdn_w_in': 3.695659e-02, 'gdn_conv_w': 3.815715e-02, 'gdn_a_log': 1.415224e-01, 'gdn_dt_bias': 1.374009e-01, 'gdn_norm_w': 1.940316e-01, 'gdn_w_out': 1.533584e-01, 'sc_w_in': 3.902287e-02, 'sc_conv_w': 4.024462e-02, 'sc_w_out': 1.303510e-01, 'ssd_w_in': 5.873745e-02, 'ssd_conv_w': 5.562881e-02, 'ssd_conv_b': 9.924920e-02, 'ssd_a_log': 1.746802e-01, 'ssd_dt_bias': 2.425570e-01, 'ssd_d_skip': 3.860008e-01, 'ssd_norm_w': 6.971619e-02, 'ssd_w_out': 2.427102e-01, 'ln_g': 6.414341e+01, 'ln_b': 4.447718e+00}


def _to_microbatches(a, axis):
    t = _jnp.moveaxis(a, axis, 0)
    t = t.reshape((N_MICROBATCH, t.shape[0] // N_MICROBATCH) + t.shape[1:])
    return _jnp.moveaxis(t, 1, axis + 1)


def setup_inputs(seed: int = 0) -> dict:
    inp = _fwd_setup_inputs(seed)
    key = _jax.random.fold_in(_jax.random.key(seed), 7919)
    shape, _ = _output_shape()
    out = dict(inp)
    out["loss_target"] = _jax.random.normal(_jax.random.fold_in(key, 0), shape, _jnp.float32)
    for i, name in enumerate(TWIN_WEIGHTS):
        w = inp[name].astype(_jnp.float32)
        if MOMENT_SCALE is None:
            s = _jnp.sqrt(_jnp.mean(_jnp.square(w)) + 1e-30)
        else:
            s = MOMENT_SCALE[name]
        km, kv = _jax.random.split(_jax.random.fold_in(key, i + 1))
        out[name] = w
        out["m_" + name] = s * _jax.random.normal(km, w.shape, _jnp.float32)
        out["v_" + name] = (s * s) * _jax.random.uniform(kv, w.shape, _jnp.float32, 0.5, 1.5)
    if N_MICROBATCH > 1:
        for name, axis in PER_EXAMPLE_BATCH_AXIS.items():
            out[name] = _to_microbatches(out[name], axis)
    return {'x': out['x'], 'gdn_w_in': out['gdn_w_in'], 'gdn_conv_w': out['gdn_conv_w'], 'gdn_a_log': out['gdn_a_log'], 'gdn_dt_bias': out['gdn_dt_bias'], 'gdn_norm_w': out['gdn_norm_w'], 'gdn_w_out': out['gdn_w_out'], 'sc_w_in': out['sc_w_in'], 'sc_conv_w': out['sc_conv_w'], 'sc_w_out': out['sc_w_out'], 'ssd_w_in': out['ssd_w_in'], 'ssd_conv_w': out['ssd_conv_w'], 'ssd_conv_b': out['ssd_conv_b'], 'ssd_a_log': out['ssd_a_log'], 'ssd_dt_bias': out['ssd_dt_bias'], 'ssd_d_skip': out['ssd_d_skip'], 'ssd_norm_w': out['ssd_norm_w'], 'ssd_w_out': out['ssd_w_out'], 'ln_g': out['ln_g'], 'ln_b': out['ln_b'], 'loss_target': out['loss_target'], 'm_gdn_w_in': out['m_gdn_w_in'], 'm_gdn_conv_w': out['m_gdn_conv_w'], 'm_gdn_a_log': out['m_gdn_a_log'], 'm_gdn_dt_bias': out['m_gdn_dt_bias'], 'm_gdn_norm_w': out['m_gdn_norm_w'], 'm_gdn_w_out': out['m_gdn_w_out'], 'm_sc_w_in': out['m_sc_w_in'], 'm_sc_conv_w': out['m_sc_conv_w'], 'm_sc_w_out': out['m_sc_w_out'], 'm_ssd_w_in': out['m_ssd_w_in'], 'm_ssd_conv_w': out['m_ssd_conv_w'], 'm_ssd_conv_b': out['m_ssd_conv_b'], 'm_ssd_a_log': out['m_ssd_a_log'], 'm_ssd_dt_bias': out['m_ssd_dt_bias'], 'm_ssd_d_skip': out['m_ssd_d_skip'], 'm_ssd_norm_w': out['m_ssd_norm_w'], 'm_ssd_w_out': out['m_ssd_w_out'], 'm_ln_g': out['m_ln_g'], 'm_ln_b': out['m_ln_b'], 'v_gdn_w_in': out['v_gdn_w_in'], 'v_gdn_conv_w': out['v_gdn_conv_w'], 'v_gdn_a_log': out['v_gdn_a_log'], 'v_gdn_dt_bias': out['v_gdn_dt_bias'], 'v_gdn_norm_w': out['v_gdn_norm_w'], 'v_gdn_w_out': out['v_gdn_w_out'], 'v_sc_w_in': out['v_sc_w_in'], 'v_sc_conv_w': out['v_sc_conv_w'], 'v_sc_w_out': out['v_sc_w_out'], 'v_ssd_w_in': out['v_ssd_w_in'], 'v_ssd_conv_w': out['v_ssd_conv_w'], 'v_ssd_conv_b': out['v_ssd_conv_b'], 'v_ssd_a_log': out['v_ssd_a_log'], 'v_ssd_dt_bias': out['v_ssd_dt_bias'], 'v_ssd_d_skip': out['v_ssd_d_skip'], 'v_ssd_norm_w': out['v_ssd_norm_w'], 'v_ssd_w_out': out['v_ssd_w_out'], 'v_ln_g': out['v_ln_g'], 'v_ln_b': out['v_ln_b']}


def _loss(weights, diff, rest, loss_target):
    with _jax.named_scope("forward"):
        args = {**rest, TWIN_DIFF_INPUT: diff, **{k: w.astype(_WEIGHT_DTYPES[k]) for k, w in weights.items()}}
        y = _forward(args)
    with _jax.named_scope("loss_head"):
        err = _jnp.square(y.astype(_jnp.float32) - loss_target)
        return 0.5 * _jnp.sum(_jnp.mean(err, axis=-1)) if err.ndim else 0.5 * err


def _adamw(w, g, m, v):
    m = ADAM_B1 * m + (1.0 - ADAM_B1) * g
    v = ADAM_B2 * v + (1.0 - ADAM_B2) * _jnp.square(g)
    m_hat = m / (1.0 - ADAM_B1 ** ADAM_STEP)
    v_hat = v / (1.0 - ADAM_B2 ** ADAM_STEP)
    delta = -ADAM_LR * (m_hat / (_jnp.sqrt(v_hat) + ADAM_EPS) + ADAM_WD * w)
    return delta, m, v


def reference(x, gdn_w_in, gdn_conv_w, gdn_a_log, gdn_dt_bias, gdn_norm_w, gdn_w_out, sc_w_in, sc_conv_w, sc_w_out, ssd_w_in, ssd_conv_w, ssd_conv_b, ssd_a_log, ssd_dt_bias, ssd_d_skip, ssd_norm_w, ssd_w_out, ln_g, ln_b, loss_target, m_gdn_w_in, m_gdn_conv_w, m_gdn_a_log, m_gdn_dt_bias, m_gdn_norm_w, m_gdn_w_out, m_sc_w_in, m_sc_conv_w, m_sc_w_out, m_ssd_w_in, m_ssd_conv_w, m_ssd_conv_b, m_ssd_a_log, m_ssd_dt_bias, m_ssd_d_skip, m_ssd_norm_w, m_ssd_w_out, m_ln_g, m_ln_b, v_gdn_w_in, v_gdn_conv_w, v_gdn_a_log, v_gdn_dt_bias, v_gdn_norm_w, v_gdn_w_out, v_sc_w_in, v_sc_conv_w, v_sc_w_out, v_ssd_w_in, v_ssd_conv_w, v_ssd_conv_b, v_ssd_a_log, v_ssd_dt_bias, v_ssd_d_skip, v_ssd_norm_w, v_ssd_w_out, v_ln_g, v_ln_b):
    given = dict(x=x, gdn_w_in=gdn_w_in, gdn_conv_w=gdn_conv_w, gdn_a_log=gdn_a_log, gdn_dt_bias=gdn_dt_bias, gdn_norm_w=gdn_norm_w, gdn_w_out=gdn_w_out, sc_w_in=sc_w_in, sc_conv_w=sc_conv_w, sc_w_out=sc_w_out, ssd_w_in=ssd_w_in, ssd_conv_w=ssd_conv_w, ssd_conv_b=ssd_conv_b, ssd_a_log=ssd_a_log, ssd_dt_bias=ssd_dt_bias, ssd_d_skip=ssd_d_skip, ssd_norm_w=ssd_norm_w, ssd_w_out=ssd_w_out, ln_g=ln_g, ln_b=ln_b, loss_target=loss_target, m_gdn_w_in=m_gdn_w_in, m_gdn_conv_w=m_gdn_conv_w, m_gdn_a_log=m_gdn_a_log, m_gdn_dt_bias=m_gdn_dt_bias, m_gdn_norm_w=m_gdn_norm_w, m_gdn_w_out=m_gdn_w_out, m_sc_w_in=m_sc_w_in, m_sc_conv_w=m_sc_conv_w, m_sc_w_out=m_sc_w_out, m_ssd_w_in=m_ssd_w_in, m_ssd_conv_w=m_ssd_conv_w, m_ssd_conv_b=m_ssd_conv_b, m_ssd_a_log=m_ssd_a_log, m_ssd_dt_bias=m_ssd_dt_bias, m_ssd_d_skip=m_ssd_d_skip, m_ssd_norm_w=m_ssd_norm_w, m_ssd_w_out=m_ssd_w_out, m_ln_g=m_ln_g, m_ln_b=m_ln_b, v_gdn_w_in=v_gdn_w_in, v_gdn_conv_w=v_gdn_conv_w, v_gdn_a_log=v_gdn_a_log, v_gdn_dt_bias=v_gdn_dt_bias, v_gdn_norm_w=v_gdn_norm_w, v_gdn_w_out=v_gdn_w_out, v_sc_w_in=v_sc_w_in, v_sc_conv_w=v_sc_conv_w, v_sc_w_out=v_sc_w_out, v_ssd_w_in=v_ssd_w_in, v_ssd_conv_w=v_ssd_conv_w, v_ssd_conv_b=v_ssd_conv_b, v_ssd_a_log=v_ssd_a_log, v_ssd_dt_bias=v_ssd_dt_bias, v_ssd_d_skip=v_ssd_d_skip, v_ssd_norm_w=v_ssd_norm_w, v_ssd_w_out=v_ssd_w_out, v_ln_g=v_ln_g, v_ln_b=v_ln_b)
    weights = {n: given[n] for n in TWIN_WEIGHTS}
    shared = {n: given[n] for n in SHARED_INPUTS}
    per_example = {n: given[n] for n in ['x']}
    grad_fn = _jax.value_and_grad(_loss, argnums=(0, 1))

    def one_microbatch(ex, loss_target):
        ex = dict(ex)
        diff = ex.pop(TWIN_DIFF_INPUT)
        return grad_fn(weights, diff, {**shared, **ex}, loss_target)

    if N_MICROBATCH == 1:
        loss, (grad_w, grad_x) = one_microbatch(per_example, given["loss_target"])
    else:
        def body(carry, xs):
            loss_sum, grad_sum = carry
            l_k, (gw_k, gx_k) = one_microbatch(xs[0], xs[1])
            with _jax.named_scope("update"):
                return (loss_sum + l_k, _jax.tree.map(_jnp.add, grad_sum, gw_k)), gx_k

        init = (_jnp.zeros((), _jnp.float32), _jax.tree.map(_jnp.zeros_like, weights))
        (loss, grad_w), grad_x = _jax.lax.scan(body, init, (per_example, given["loss_target"]))
    with _jax.named_scope("update"):
        delta_w, new_m, new_v = {}, {}, {}
        for n in TWIN_WEIGHTS:
            delta_w[n], new_m[n], new_v[n] = _adamw(weights[n], grad_w[n], given["m_" + n], given["v_" + n])
    return (loss, grad_x, *[grad_w[n] for n in TWIN_WEIGHTS], *[delta_w[n] for n in TWIN_WEIGHTS],
            *[new_m[n] for n in TWIN_WEIGHTS], *[new_v[n] for n in TWIN_WEIGHTS])
```

```python
import functools

import jax
import jax.numpy as jnp
from jax import lax
from jax.experimental import pallas as pl
from jax.experimental.pallas import tpu as pltpu

F32 = jnp.float32
MXU_DTYPE = jnp.bfloat16
HIGHEST = lax.Precision.HIGHEST

D_MODEL = 1024
DEPTH = 4
D_INNER = 2048
CHUNK = 64
LANES = 128
SUBLANES = 8
VMEM_LIMIT = 56 * 1024 * 1024

GDN_HEAD = 128
GDN_VH = 16
GDN_QKH = 8
GDN_QK = 1024
GDN_V = 2048
GDN_CONV = 4
GDN_CONV_DIM = 4096
GDN_IN = 6176
GDN_IN_PAD = 6272

SC_W = 2048
SC_CONV = 3
SC_IN = 8192

SSD_P = 64
SSD_H = 32
SSD_G = 4
SSD_S = 128
SSD_CONV = 4
SSD_CONV_DIM = 3072
SSD_IN = 5152
SSD_IN_PAD = 5376

ALPHA = (2 * DEPTH) ** 0.25
RMS_EPS = 1e-6
LN_EPS = 1e-5
L2_EPS = 1e-6

ADAM_LR = 0.001
ADAM_B1 = 0.9
ADAM_B2 = 0.999
ADAM_EPS = 1e-08
ADAM_WD = 0.01
ADAM_STEP = 10

MESH_AXES = ("x", "y", "c")


def _cparams(sem):
    return pltpu.CompilerParams(dimension_semantics=sem, vmem_limit_bytes=VMEM_LIMIT)


def _pick(n, prefs):
    for p in prefs:
        if n % p == 0:
            return p
    return n


def _dot(a, b, dims=(((1,), (0,)), ((), ()))):
    return lax.dot_general(a.astype(MXU_DTYPE), b.astype(MXU_DTYPE), dims, preferred_element_type=F32)


def _dot_nt(a, b):
    return _dot(a, b, (((1,), (1,)), ((), ())))


def _dot_tn(a, b):
    return _dot(a, b, (((0,), (0,)), ((), ())))


def _dot_hi(a, b, dims=(((1,), (0,)), ((), ()))):
    return lax.dot_general(a, b, dims, preferred_element_type=F32, precision=HIGHEST)


def _dot_nt_hi(a, b):
    return _dot_hi(a, b, (((1,), (1,)), ((), ())))


def _iota(shape, dim):
    return lax.broadcasted_iota(jnp.int32, shape, dim)


def _sigmoid(x):
    return 1.0 / (1.0 + jnp.exp(-x))


def _silu(x):
    return x * _sigmoid(x)


def _dsilu(x):
    s = _sigmoid(x)
    return s * (1.0 + x * (1.0 - s))


def _softplus(x):
    return jnp.maximum(x, 0.0) + jnp.log(1.0 + jnp.exp(-jnp.abs(x)))


def matmul(a, b, *, ta=False, tb=False, add=None, add_scale=1.0, name):
    if ta:
        kdim, m = a.shape
    else:
        m, kdim = a.shape
    n = b.shape[0] if tb else b.shape[1]
    assert (b.shape[1] if tb else b.shape[0]) == kdim
    tm = _pick(m, (1024, 896, 768, 512)) if ta else _pick(m, (512, 256, 128))
    tn = _pick(n, (1024, 896, 768, 512, 256, 128))
    tk = _pick(kdim, (512, 256)) if ta else _pick(kdim, (1024, 896, 768, 512))
    nk = kdim // tk
    dims = (((0 if ta else 1,), (1 if tb else 0,)), ((), ()))

    def body(a_ref, b_ref, *rest):
        if add is None:
            o_ref, acc_ref = rest
        else:
            add_ref, o_ref, acc_ref = rest
        k = pl.program_id(2)

        @pl.when(k == 0)
        def _():
            acc_ref[...] = jnp.zeros_like(acc_ref)

        acc_ref[...] += _dot(a_ref[...], b_ref[...], dims)

        @pl.when(k == nk - 1)
        def _():
            if add is None:
                o_ref[...] = acc_ref[...]
            else:
                o_ref[...] = acc_ref[...] + add_scale * add_ref[...]

    a_spec = pl.BlockSpec((tk, tm), lambda i, j, k: (k, i)) if ta else pl.BlockSpec((tm, tk), lambda i, j, k: (i, k))
    b_spec = pl.BlockSpec((tn, tk), lambda i, j, k: (j, k)) if tb else pl.BlockSpec((tk, tn), lambda i, j, k: (k, j))
    o_spec = pl.BlockSpec((tm, tn), lambda i, j, k: (i, j))
    in_specs = [a_spec, b_spec] + ([] if add is None else [o_spec])
    args = (a, b) + (() if add is None else (add,))
    return pl.pallas_call(
        body, name=name, grid=(m // tm, n // tn, nk), in_specs=in_specs, out_specs=o_spec,
        out_shape=jax.ShapeDtypeStruct((m, n), F32), scratch_shapes=[pltpu.VMEM((tm, tn), F32)],
        compiler_params=_cparams(("parallel", "parallel", "arbitrary")),
    )(*args)


LN_ROWS = 512


def _ln_stats(x, y):
    r = ALPHA * x + y
    mu = jnp.mean(r, axis=-1, keepdims=True)
    rc = r - mu
    var = jnp.mean(rc * rc, axis=-1, keepdims=True)
    rstd = lax.rsqrt(var + LN_EPS)
    return rc * rstd, rstd


def ln_fwd(x, y, g, b, *, name):
    rows, d = x.shape

    def body(x_ref, y_ref, g_ref, b_ref, o_ref):
        xhat, _ = _ln_stats(x_ref[...], y_ref[...])
        o_ref[...] = xhat * g_ref[...] + b_ref[...]

    blk = pl.BlockSpec((LN_ROWS, d), lambda i: (i, 0))
    vec = pl.BlockSpec((1, d), lambda i: (0, 0))
    return pl.pallas_call(
        body, name=name, grid=(rows // LN_ROWS,), in_specs=[blk, blk, vec, vec], out_specs=blk,
        out_shape=jax.ShapeDtypeStruct((rows, d), F32), compiler_params=_cparams(("parallel",)),
    )(x, y, g.reshape(1, d), b.reshape(1, d))


def ln_bwd(x, y, g, dxn=None, *, b=None, target=None, name):
    rows, d = x.shape
    final = target is not None

    def body(x_ref, y_ref, g_ref, *rest):
        if final:
            b_ref, t_ref, dr_ref, dg_ref, db_ref, loss_ref = rest
        else:
            dxn_ref, dr_ref, dg_ref, db_ref = rest
        i = pl.program_id(0)
        xhat, rstd = _ln_stats(x_ref[...], y_ref[...])
        gv = g_ref[...]
        if final:
            err = xhat * gv + b_ref[...] - t_ref[...]
            dxn_v = err * (1.0 / d)
            part = 0.5 * jnp.sum(jnp.mean(err * err, axis=-1, keepdims=True), axis=0, keepdims=True)
        else:
            dxn_v = dxn_ref[...]
        dxh = dxn_v * gv
        m1 = jnp.mean(dxh, axis=-1, keepdims=True)
        m2 = jnp.mean(dxh * xhat, axis=-1, keepdims=True)
        dr_ref[...] = rstd * (dxh - m1 - xhat * m2)

        @pl.when(i == 0)
        def _():
            dg_ref[...] = jnp.zeros_like(dg_ref)
            db_ref[...] = jnp.zeros_like(db_ref)
            if final:
                loss_ref[...] = jnp.zeros_like(loss_ref)

        dg_ref[...] += jnp.sum(dxn_v * xhat, axis=0, keepdims=True)
        db_ref[...] += jnp.sum(dxn_v, axis=0, keepdims=True)
        if final:
            loss_ref[...] += jnp.broadcast_to(part, loss_ref.shape)

    blk = pl.BlockSpec((LN_ROWS, d), lambda i: (i, 0))
    vec = pl.BlockSpec((1, d), lambda i: (0, 0))
    lvec = pl.BlockSpec((1, LANES), lambda i: (0, 0))
    out_shape = [jax.ShapeDtypeStruct((rows, d), F32), jax.ShapeDtypeStruct((1, d), F32), jax.ShapeDtypeStruct((1, d), F32)]
    out_specs = [blk, vec, vec]
    if final:
        in_specs = [blk, blk, vec, vec, blk]
        args = (x, y, g.reshape(1, d), b.reshape(1, d), target)
        out_shape.append(jax.ShapeDtypeStruct((1, LANES), F32))
        out_specs.append(lvec)
    else:
        in_specs = [blk, blk, vec, blk]
        args = (x, y, g.reshape(1, d), dxn)
    return pl.pallas_call(
        body, name=name, grid=(rows // LN_ROWS,), in_specs=in_specs, out_specs=out_specs, out_shape=out_shape,
        compiler_params=_cparams(("arbitrary",)),
    )(*args)


def _conv_from_ext(ext_ref, w, width, rows):
    out = None
    for j in range(width):
        off = SUBLANES - (width - 1) + j
        term = ext_ref[off:off + rows, :] * w[j:j + 1, :]
        out = term if out is None else out + term
    return out


def _conv_wgrad_from_ext(ext_ref, dout, width, rows):
    res = []
    for j in range(width):
        off = SUBLANES - (width - 1) + j
        res.append(jnp.sum(ext_ref[off:off + rows, :] * dout, axis=0, keepdims=True))
    return res


def _conv_dgrad_from_ext(dext_ref, w, width, rows):
    out = None
    for j in range(width):
        off = (width - 1) - j
        term = dext_ref[off:off + rows, :] * w[j:j + 1, :]
        out = term if out is None else out + term
    return out


SC_ROWS = 128


def sc_fwd(proj, conv_w, *, name):
    rows = proj.shape[0]
    nb = rows // SC_ROWS
    hb = SC_ROWS // SUBLANES

    def body(p_ref, halo_ref, w_ref, o_ref, ext_ref):
        i = pl.program_id(0)
        h = p_ref[:, 0:SC_W]
        bg = p_ref[:, SC_W:2 * SC_W]
        cg = p_ref[:, 2 * SC_W:3 * SC_W]
        z = p_ref[:, 3 * SC_W:4 * SC_W]
        uh = halo_ref[:, 2 * SC_W:3 * SC_W] * halo_ref[:, 0:SC_W]
        ext_ref[0:SUBLANES, :] = jnp.where(i == 0, 0.0, uh)
        ext_ref[SUBLANES:, :] = cg * h
        cv = _conv_from_ext(ext_ref, w_ref[...], SC_CONV, SC_ROWS)
        o_ref[...] = bg * cv * _silu(z)

    return pl.pallas_call(
        body, name=name, grid=(nb,),
        in_specs=[pl.BlockSpec((SC_ROWS, SC_IN), lambda i: (i, 0)),
                  pl.BlockSpec((SUBLANES, SC_IN), lambda i: (jnp.maximum(i * hb - 1, 0), 0)),
                  pl.BlockSpec((SUBLANES, SC_W), lambda i: (0, 0))],
        out_specs=pl.BlockSpec((SC_ROWS, SC_W), lambda i: (i, 0)),
        out_shape=jax.ShapeDtypeStruct((rows, SC_W), F32),
        scratch_shapes=[pltpu.VMEM((SC_ROWS + SUBLANES, SC_W), F32)],
        compiler_params=_cparams(("parallel",)),
    )(proj, proj, _pad_rows(conv_w))


def sc_bwd(proj, conv_w, dmix, *, name):
    rows = proj.shape[0]
    nb = rows // SC_ROWS
    hb = SC_ROWS // SUBLANES

    def body(p_ref, halo_ref, w_ref, dm_ref, dp_ref, dw_ref, ext_ref, dext_ref):
        i = pl.program_id(0)
        blk = nb - 1 - i
        w = w_ref[...]
        h = p_ref[:, 0:SC_W]
        bg = p_ref[:, SC_W:2 * SC_W]
        cg = p_ref[:, 2 * SC_W:3 * SC_W]
        z = p_ref[:, 3 * SC_W:4 * SC_W]
        uh = halo_ref[:, 2 * SC_W:3 * SC_W] * halo_ref[:, 0:SC_W]
        ext_ref[0:SUBLANES, :] = jnp.where(blk == 0, 0.0, uh)
        ext_ref[SUBLANES:, :] = cg * h
        cv = _conv_from_ext(ext_ref, w, SC_CONV, SC_ROWS)
        dm = dm_ref[...]
        sz = _silu(z)
        dy = dm * sz
        dp_ref[:, 3 * SC_W:4 * SC_W] = dm * bg * cv * _dsilu(z)
        dp_ref[:, SC_W:2 * SC_W] = dy * cv
        dcv = dy * bg

        @pl.when(i == 0)
        def _():
            dext_ref[SC_ROWS:, :] = jnp.zeros((SUBLANES, SC_W), F32)
            dw_ref[...] = jnp.zeros_like(dw_ref)

        dext_ref[0:SC_ROWS, :] = dcv
        du = _conv_dgrad_from_ext(dext_ref, w, SC_CONV, SC_ROWS)
        dp_ref[:, 0:SC_W] = du * cg
        dp_ref[:, 2 * SC_W:3 * SC_W] = du * h
        dws = _conv_wgrad_from_ext(ext_ref, dcv, SC_CONV, SC_ROWS)
        for j in range(SC_CONV):
            dw_ref[j:j + 1, :] += dws[j]
        dext_ref[SC_ROWS:, :] = dcv[0:SUBLANES, :]

    return pl.pallas_call(
        body, name=name, grid=(nb,),
        in_specs=[pl.BlockSpec((SC_ROWS, SC_IN), lambda i: (nb - 1 - i, 0)),
                  pl.BlockSpec((SUBLANES, SC_IN), lambda i: (jnp.maximum((nb - 1 - i) * hb - 1, 0), 0)),
                  pl.BlockSpec((SUBLANES, SC_W), lambda i: (0, 0)),
                  pl.BlockSpec((SC_ROWS, SC_W), lambda i: (nb - 1 - i, 0))],
        out_specs=[pl.BlockSpec((SC_ROWS, SC_IN), lambda i: (nb - 1 - i, 0)),
                   pl.BlockSpec((SUBLANES, SC_W), lambda i: (0, 0))],
        out_shape=[jax.ShapeDtypeStruct((rows, SC_IN), F32), jax.ShapeDtypeStruct((SUBLANES, SC_W), F32)],
        scratch_shapes=[pltpu.VMEM((SC_ROWS + SUBLANES, SC_W), F32), pltpu.VMEM((SC_ROWS + SUBLANES, SC_W), F32)],
        compiler_params=_cparams(("arbitrary",)),
    )(proj, proj, _pad_rows(conv_w), dmix)


def _pad_rows(w, rows=SUBLANES):
    return jnp.pad(w, ((0, rows - w.shape[0]), (0, 0)))


def _pad_lanes(v, lanes=LANES):
    v = v.reshape(1, -1)
    return jnp.pad(v, ((0, 0), (0, lanes - v.shape[1])))


def _tri(n, lower=True):
    r, c = _iota((n, n), 0), _iota((n, n), 1)
    return jnp.where((c <= r) if lower else (c >= r), 1.0, 0.0)


def _eye(n):
    return jnp.where(_iota((n, n), 0) == _iota((n, n), 1), 1.0, 0.0)


def _head_expand(n, width):
    return jnp.where(_iota((LANES, n), 1) // width == _iota((LANES, n), 0), 1.0, 0.0)


def _col(v, h):
    return jnp.sum(jnp.where(_iota(v.shape, 1) == h, v, 0.0), axis=1, keepdims=True)


def _row(v, r):
    return jnp.sum(jnp.where(_iota(v.shape, 0) == r, v, 0.0), axis=0, keepdims=True)


def _expand_row(v, e):
    return jnp.max(_dot_hi(jnp.broadcast_to(v, (SUBLANES, LANES)), e), axis=0, keepdims=True)


SSD_ROWS = 128
SSD_X0 = D_INNER
SSD_DT0 = D_INNER + SSD_CONV_DIM
SSD_B0 = D_INNER
SSD_C0 = D_INNER + SSD_G * SSD_S
SSD_GW = D_INNER // SSD_G
SSD_HG = SSD_H // SSD_G


def _ssd_prologue(blk, p_ref, halo_ref, cw_ref, cb_ref, dtb_ref, ext_ref, xbc_ref, dt_ref):
    ext_ref[0:SUBLANES, :] = jnp.where(blk == 0, 0.0, halo_ref[:, SSD_X0:SSD_DT0])
    ext_ref[SUBLANES:, :] = p_ref[:, SSD_X0:SSD_DT0]
    pre = _conv_from_ext(ext_ref, cw_ref[...], SSD_CONV, SSD_ROWS) + cb_ref[...]
    xbc_ref[...] = _silu(pre)
    dt_ref[...] = _softplus(p_ref[:, SSD_DT0:SSD_DT0 + LANES] + dtb_ref[...])
    return pre


def _ssd_chunk_decays(dt_c, a_row, ltri, eye_l, act_ref):
    da = dt_c * a_row
    ac = _dot_hi(ltri, da)
    act_ref[...] = _dot_nt_hi(eye_l, ac)
    ac_last = _row(ac, CHUNK - 1)
    return ac, jnp.exp(ac_last - ac), jnp.exp(ac), jnp.exp(ac_last)


def _ssd_seg(ac, act_ref, h, causal):
    return jnp.where(causal, jnp.exp(jnp.minimum(_col(ac, h) - act_ref[pl.ds(h, 1), :], 0.0)), 0.0)


def _ssd_group_fwd(g, xbc_ref, rows, dt_exp, tail_exp, cdec_exp, ac, act_ref, s_g, causal):
    gl = slice(g * SSD_GW, (g + 1) * SSD_GW)
    bg = xbc_ref[rows, SSD_B0 + g * SSD_S:SSD_B0 + (g + 1) * SSD_S]
    cg = xbc_ref[rows, SSD_C0 + g * SSD_S:SSD_C0 + (g + 1) * SSD_S]
    xdt = xbc_ref[rows, gl] * dt_exp[:, gl]
    cb = _dot_nt(cg, bg)
    cs = _dot(cg, s_g)
    lane_head = _iota((CHUNK, SSD_GW), 1) // SSD_P

    def head(e, yd):
        seg = _ssd_seg(ac, act_ref, g * SSD_HG + e, causal)
        return yd + _dot(seg * cb, jnp.where(lane_head == e, xdt, 0.0))

    yd = lax.fori_loop(0, SSD_HG, head, jnp.zeros((CHUNK, SSD_GW), F32))
    st = _dot_tn(bg, xdt * tail_exp[:, gl])
    return yd + cs * cdec_exp[:, gl], st, bg, cg, cb, xdt, cs


def ssd_fwd(proj, conv_w, conv_b, a_log, dt_bias, d_skip, norm_w, *, name):
    rows_total = proj.shape[0]
    nb = rows_total // SSD_ROWS
    hb = SSD_ROWS // SUBLANES
    cpb = SSD_ROWS // CHUNK

    def body(p_ref, halo_ref, cw_ref, cb_ref, alog_ref, dtb_ref, dsk_ref, nw_ref, mix_ref, st_ref,
             ext_ref, xbc_ref, dt_ref, s_ref, act_ref):
        i = pl.program_id(0)

        @pl.when(i == 0)
        def _():
            s_ref[...] = jnp.zeros_like(s_ref)

        _ssd_prologue(i, p_ref, halo_ref, cw_ref, cb_ref, dtb_ref, ext_ref, xbc_ref, dt_ref)
        a_row = -jnp.exp(alog_ref[...])
        expand = _head_expand(D_INNER, SSD_P)
        dsk_exp = _expand_row(dsk_ref[...], expand)
        ltri, eye_l = _tri(CHUNK), _eye(LANES)
        causal = _iota((CHUNK, CHUNK), 1) <= _iota((CHUNK, CHUNK), 0)

        def chunk(c, carry):
            rows = pl.ds(pl.multiple_of(c * CHUNK, CHUNK), CHUNK)
            dt_c = dt_ref[rows, :]
            ac, tail, cdec, tot = _ssd_chunk_decays(dt_c, a_row, ltri, eye_l, act_ref)
            dt_exp = _dot_hi(dt_c, expand)
            tail_exp = _dot_hi(tail, expand)
            cdec_exp = _dot_hi(cdec, expand)
            tot_exp = _expand_row(tot, expand)
            for g in range(SSD_G):
                gl = slice(g * SSD_GW, (g + 1) * SSD_GW)
                s_g = s_ref[g]
                st_ref[c, g] = s_g
                y, st = _ssd_group_fwd(g, xbc_ref, rows, dt_exp, tail_exp, cdec_exp, ac, act_ref, s_g, causal)[:2]
                s_ref[g] = s_g * tot_exp[:, gl] + st
                y = (y + dsk_exp[:, gl] * xbc_ref[rows, gl]) * _silu(p_ref[rows, gl])
                r = lax.rsqrt(jnp.mean(y * y, axis=-1, keepdims=True) + RMS_EPS)
                mix_ref[rows, gl] = y * r * nw_ref[:, gl]
            return carry

        lax.fori_loop(0, cpb, chunk, 0)

    vec = lambda n: pl.BlockSpec((1, n), lambda i: (0, 0))
    return pl.pallas_call(
        body, name=name, grid=(nb,),
        in_specs=[pl.BlockSpec((SSD_ROWS, SSD_IN_PAD), lambda i: (i, 0)),
                  pl.BlockSpec((SUBLANES, SSD_IN_PAD), lambda i: (jnp.maximum(i * hb - 1, 0), 0)),
                  pl.BlockSpec((SUBLANES, SSD_CONV_DIM), lambda i: (0, 0)),
                  vec(SSD_CONV_DIM), vec(LANES), vec(LANES), vec(LANES), vec(D_INNER)],
        out_specs=[pl.BlockSpec((SSD_ROWS, D_INNER), lambda i: (i, 0)),
                   pl.BlockSpec((cpb, SSD_G, SSD_S, SSD_GW), lambda i: (i, 0, 0, 0))],
        out_shape=[jax.ShapeDtypeStruct((rows_total, D_INNER), F32),
                   jax.ShapeDtypeStruct((rows_total // CHUNK, SSD_G, SSD_S, SSD_GW), F32)],
        scratch_shapes=[pltpu.VMEM((SSD_ROWS + SUBLANES, SSD_CONV_DIM), F32),
                        pltpu.VMEM((SSD_ROWS, SSD_CONV_DIM), F32),
                        pltpu.VMEM((SSD_ROWS, LANES), F32),
                        pltpu.VMEM((SSD_G, SSD_S, SSD_GW), F32),
                        pltpu.VMEM((LANES, CHUNK), F32)],
        compiler_params=_cparams(("arbitrary",)),
    )(proj, proj, _pad_rows(conv_w), conv_b.reshape(1, -1), _pad_lanes(a_log), _pad_lanes(dt_bias),
      _pad_lanes(d_skip), norm_w.reshape(1, -1))


def ssd_bwd(proj, conv_w, conv_b, a_log, dt_bias, d_skip, norm_w, states, dmix, *, name):
    rows_total = proj.shape[0]
    nb = rows_total // SSD_ROWS
    hb = SSD_ROWS // SUBLANES
    cpb = SSD_ROWS // CHUNK

    def body(p_ref, halo_ref, cw_ref, cb_ref, alog_ref, dtb_ref, dsk_ref, nw_ref, st_ref, dm_ref,
             dp_ref, dcw_ref, dcb_ref, dalog_ref, ddtb_ref, ddsk_ref, dnw_ref,
             ext_ref, xbc_ref, dt_ref, ds_ref, act_ref, dext_ref, dxdt_ref, dcbm_ref, dac_ref, dact_ref, ddskw_ref):
        i = pl.program_id(0)
        blk = nb - 1 - i

        @pl.when(i == 0)
        def _():
            ds_ref[...] = jnp.zeros_like(ds_ref)
            dext_ref[SSD_ROWS:, :] = jnp.zeros((SUBLANES, SSD_CONV_DIM), F32)
            ddskw_ref[...] = jnp.zeros_like(ddskw_ref)
            for r in (dcw_ref, dcb_ref, dalog_ref, ddtb_ref, ddsk_ref, dnw_ref):
                r[...] = jnp.zeros_like(r)

        _ssd_prologue(blk, p_ref, halo_ref, cw_ref, cb_ref, dtb_ref, ext_ref, xbc_ref, dt_ref)
        a_row = -jnp.exp(alog_ref[...])
        expand = _head_expand(D_INNER, SSD_P)
        dsk_exp = _expand_row(dsk_ref[...], expand)
        ltri, utri, eye_l, eye_c = _tri(CHUNK), _tri(CHUNK, lower=False), _eye(LANES), _eye(CHUNK)
        causal = _iota((CHUNK, CHUNK), 1) <= _iota((CHUNK, CHUNK), 0)
        lane_head = _iota((CHUNK, SSD_GW), 1) // SSD_P
        dp_ref[:, SSD_DT0 + LANES:] = jnp.zeros((SSD_ROWS, SSD_IN_PAD - SSD_DT0 - LANES), F32)

        def chunk(cc, carry):
            c = cpb - 1 - cc
            rows = pl.ds(pl.multiple_of(c * CHUNK, CHUNK), CHUNK)
            dt_c = dt_ref[rows, :]
            ac, tail, cdec, tot = _ssd_chunk_decays(dt_c, a_row, ltri, eye_l, act_ref)
            dt_exp = _dot_hi(dt_c, expand)
            tail_exp = _dot_hi(tail, expand)
            cdec_exp = _dot_hi(cdec, expand)
            tot_exp = _expand_row(tot, expand)
            dac_ref[...] = jnp.zeros_like(dac_ref)
            dact_ref[...] = jnp.zeros_like(dact_ref)
            d_cdec = jnp.zeros((CHUNK, LANES), F32)
            d_tail = jnp.zeros((CHUNK, LANES), F32)
            d_dt = jnp.zeros((CHUNK, LANES), F32)
            d_tot = jnp.zeros((1, LANES), F32)
            for g in range(SSD_G):
                gl = slice(g * SSD_GW, (g + 1) * SSD_GW)
                ex_g = expand[:, gl]
                s_g = st_ref[c, g]
                y, _, bg, cg, cb, xdt, cs = _ssd_group_fwd(g, xbc_ref, rows, dt_exp, tail_exp, cdec_exp, ac, act_ref, s_g, causal)
                xs = xbc_ref[rows, gl]
                z = p_ref[rows, gl]
                sz = _silu(z)
                y2 = y + dsk_exp[:, gl] * xs
                yg = y2 * sz
                r = lax.rsqrt(jnp.mean(yg * yg, axis=-1, keepdims=True) + RMS_EPS)
                yn = yg * r
                dm = dm_ref[rows, gl]
                dnw_ref[:, gl] += jnp.sum(dm * yn, axis=0, keepdims=True)
                dyn = dm * nw_ref[:, gl]
                dyg = r * (dyn - yn * jnp.mean(dyn * yn, axis=-1, keepdims=True))
                dp_ref[rows, gl] = dyg * y2 * _dsilu(z)
                dy = dyg * sz
                ddskw_ref[:, gl] += jnp.sum(dy * xs, axis=0, keepdims=True)
                ds_g = ds_ref[g]
                dyc = dy * cdec_exp[:, gl]
                ds_ref[g] = ds_g * tot_exp[:, gl] + _dot_tn(cg, dyc)
                sds = jnp.broadcast_to(jnp.sum(s_g * ds_g, axis=0, keepdims=True), (SUBLANES, SSD_GW))
                d_tot = d_tot + jnp.max(_dot_nt_hi(sds, ex_g), axis=0, keepdims=True)
                dcg = _dot_nt(dyc, s_g)
                d_cdec = d_cdec + _dot_nt_hi(dy * cs, ex_g)
                xdtd = xdt * tail_exp[:, gl]
                d_xdtd = _dot(bg, ds_g)
                dbg = _dot_nt(xdtd, ds_g)
                d_tail = d_tail + _dot_nt_hi(d_xdtd * xdt, ex_g)
                dxdt_ref[...] = d_xdtd * tail_exp[:, gl]
                dcbm_ref[...] = jnp.zeros_like(dcbm_ref)

                def head(e, hc):
                    h = g * SSD_HG + e
                    seg = _ssd_seg(ac, act_ref, h, causal)
                    gm = seg * cb
                    dy_h = jnp.where(lane_head == e, dy, 0.0)
                    dxdt_ref[...] += _dot_tn(gm, dy_h)
                    dg_m = jnp.where(causal, _dot_nt(dy_h, xdt), 0.0)
                    dcbm_ref[...] += dg_m * seg
                    em = dg_m * gm
                    dac_ref[...] += jnp.where(_iota((CHUNK, LANES), 1) == h, jnp.sum(em, axis=1, keepdims=True), 0.0)
                    dact_ref[pl.ds(h, 1), :] = jnp.sum(em, axis=0, keepdims=True)
                    return hc

                lax.fori_loop(0, SSD_HG, head, 0)
                d_cb = dcbm_ref[...]
                dcg = dcg + _dot(d_cb, bg)
                dbg = dbg + _dot_tn(d_cb, cg)
                d_xdt = dxdt_ref[...]
                d_dt = d_dt + _dot_nt_hi(d_xdt * xs, ex_g)
                dext_ref[rows, gl] = d_xdt * dt_exp[:, gl] + dy * dsk_exp[:, gl]
                dext_ref[rows, SSD_B0 + g * SSD_S:SSD_B0 + (g + 1) * SSD_S] = dbg
                dext_ref[rows, SSD_C0 + g * SSD_S:SSD_C0 + (g + 1) * SSD_S] = dcg
            d_ac = dac_ref[...] - _dot_nt_hi(eye_c, dact_ref[...]) + d_cdec * cdec - d_tail * tail
            d_last = jnp.sum(d_tail * tail, axis=0, keepdims=True) + d_tot * tot
            d_ac = jnp.where(_iota((CHUNK, LANES), 0) == CHUNK - 1, d_ac + d_last, d_ac)
            d_da = _dot_hi(utri, d_ac)
            d_dt = d_dt + d_da * a_row
            dalog_ref[...] += jnp.sum(d_da * dt_c, axis=0, keepdims=True) * a_row
            d_raw = d_dt * _sigmoid(p_ref[rows, SSD_DT0:SSD_DT0 + LANES] + dtb_ref[...])
            d_raw = jnp.where(_iota((CHUNK, LANES), 1) < SSD_H, d_raw, 0.0)
            ddtb_ref[...] += jnp.sum(d_raw, axis=0, keepdims=True)
            dp_ref[rows, SSD_DT0:SSD_DT0 + LANES] = d_raw
            return carry

        lax.fori_loop(0, cpb, chunk, 0)
        pre = _conv_from_ext(ext_ref, cw_ref[...], SSD_CONV, SSD_ROWS) + cb_ref[...]
        d_pre = dext_ref[0:SSD_ROWS, :] * _dsilu(pre)
        dext_ref[0:SSD_ROWS, :] = d_pre
        dcb_ref[...] += jnp.sum(d_pre, axis=0, keepdims=True)
        dp_ref[:, SSD_X0:SSD_DT0] = _conv_dgrad_from_ext(dext_ref, cw_ref[...], SSD_CONV, SSD_ROWS)
        dws = _conv_wgrad_from_ext(ext_ref, d_pre, SSD_CONV, SSD_ROWS)
        for j in range(SSD_CONV):
            dcw_ref[j:j + 1, :] += dws[j]
        dext_ref[SSD_ROWS:, :] = d_pre[0:SUBLANES, :]

        @pl.when(i == nb - 1)
        def _():
            ddsk_ref[...] = jnp.max(_dot_nt_hi(jnp.broadcast_to(ddskw_ref[...], (SUBLANES, D_INNER)), expand), axis=0, keepdims=True)

    vec = lambda n: pl.BlockSpec((1, n), lambda i: (0, 0))
    outs = pl.pallas_call(
        body, name=name, grid=(nb,),
        in_specs=[pl.BlockSpec((SSD_ROWS, SSD_IN_PAD), lambda i: (nb - 1 - i, 0)),
                  pl.BlockSpec((SUBLANES, SSD_IN_PAD), lambda i: (jnp.maximum((nb - 1 - i) * hb - 1, 0), 0)),
                  pl.BlockSpec((SUBLANES, SSD_CONV_DIM), lambda i: (0, 0)),
                  vec(SSD_CONV_DIM), vec(LANES), vec(LANES), vec(LANES), vec(D_INNER),
                  pl.BlockSpec((cpb, SSD_G, SSD_S, SSD_GW), lambda i: (nb - 1 - i, 0, 0, 0)),
                  pl.BlockSpec((SSD_ROWS, D_INNER), lambda i: (nb - 1 - i, 0))],
        out_specs=[pl.BlockSpec((SSD_ROWS, SSD_IN_PAD), lambda i: (nb - 1 - i, 0)),
                   pl.BlockSpec((SUBLANES, SSD_CONV_DIM), lambda i: (0, 0)),
                   vec(SSD_CONV_DIM), vec(LANES), vec(LANES), vec(LANES), vec(D_INNER)],
        out_shape=[jax.ShapeDtypeStruct((rows_total, SSD_IN_PAD), F32),
                   jax.ShapeDtypeStruct((SUBLANES, SSD_CONV_DIM), F32),
                   jax.ShapeDtypeStruct((1, SSD_CONV_DIM), F32), jax.ShapeDtypeStruct((1, LANES), F32),
                   jax.ShapeDtypeStruct((1, LANES), F32), jax.ShapeDtypeStruct((1, LANES), F32),
                   jax.ShapeDtypeStruct((1, D_INNER), F32)],
        scratch_shapes=[pltpu.VMEM((SSD_ROWS + SUBLANES, SSD_CONV_DIM), F32),
                        pltpu.VMEM((SSD_ROWS, SSD_CONV_DIM), F32),
                        pltpu.VMEM((SSD_ROWS, LANES), F32),
                        pltpu.VMEM((SSD_G, SSD_S, SSD_GW), F32),
                        pltpu.VMEM((LANES, CHUNK), F32),
                        pltpu.VMEM((SSD_ROWS + SUBLANES, SSD_CONV_DIM), F32),
                        pltpu.VMEM((CHUNK, SSD_GW), F32),
                        pltpu.VMEM((CHUNK, CHUNK), F32),
                        pltpu.VMEM((CHUNK, LANES), F32),
                        pltpu.VMEM((LANES, CHUNK), F32),
                        pltpu.VMEM((1, D_INNER), F32)],
        compiler_params=_cparams(("arbitrary",)),
    )(proj, proj, _pad_rows(conv_w), conv_b.reshape(1, -1), _pad_lanes(a_log), _pad_lanes(dt_bias),
      _pad_lanes(d_skip), norm_w.reshape(1, -1), states, dmix)
    dproj, dcw, dcb, dalog, ddtb, ddsk, dnw = outs
    return dproj, [dcw[:SSD_CONV], dcb[0], dalog[0, :SSD_H], ddtb[0, :SSD_H], ddsk[0, :SSD_H], dnw[0]]


GDN_ROWS = 128
GDN_K0 = GDN_QK
GDN_V0 = 2 * GDN_QK
GDN_Z0 = GDN_CONV_DIM
GDN_BA0 = GDN_CONV_DIM + GDN_V
GDN_GL = GDN_VH
GDN_SCALE = GDN_HEAD ** -0.5


def _gdn_lane_params(v):
    return jnp.pad(v.reshape(1, GDN_VH), ((0, 0), (GDN_GL, LANES - GDN_GL - GDN_VH)))


def _inv_unit_lower(a, eye_c):
    x = eye_c - a
    p = a
    n = 2
    while n < CHUNK:
        p = _dot_hi(p, p)
        x = x + _dot_hi(x, p)
        n *= 2
    return x


def _gdn_prologue(blk, p_ref, halo_ref, cw_ref, alog_ref, dtb_ref, ext_ref, qkv_ref, beta_ref, g_ref):
    ext_ref[0:SUBLANES, :] = jnp.where(blk == 0, 0.0, halo_ref[:, 0:GDN_CONV_DIM])
    ext_ref[SUBLANES:, :] = p_ref[:, 0:GDN_CONV_DIM]
    w = cw_ref[...]
    for hq in range(2 * GDN_QKH):
        cols = slice(hq * GDN_HEAD, (hq + 1) * GDN_HEAD)
        pre = None
        for j in range(GDN_CONV):
            off = SUBLANES - (GDN_CONV - 1) + j
            term = ext_ref[off:off + GDN_ROWS, cols] * w[j:j + 1, cols]
            pre = term if pre is None else pre + term
        a = _silu(pre)
        r = lax.rsqrt(jnp.sum(a * a, axis=-1, keepdims=True) + L2_EPS)
        qkv_ref[:, cols] = a * (r * (GDN_SCALE if hq < GDN_QKH else 1.0))
    vcols = slice(GDN_V0, GDN_CONV_DIM)
    pre = None
    for j in range(GDN_CONV):
        off = SUBLANES - (GDN_CONV - 1) + j
        term = ext_ref[off:off + GDN_ROWS, vcols] * w[j:j + 1, vcols]
        pre = term if pre is None else pre + term
    qkv_ref[:, vcols] = _silu(pre)
    ba = p_ref[:, GDN_BA0:GDN_BA0 + LANES]
    beta_ref[...] = _sigmoid(ba)
    g_ref[...] = -jnp.exp(alog_ref[...]) * _softplus(ba + dtb_ref[...])


def _gdn_head_fwd(q, k, v, kk, qk, gcol, grow, glast, bcol, s, causal, strict, eye_c):
    decay = jnp.where(causal, jnp.exp(jnp.minimum(gcol - grow, 0.0)), 0.0)
    egc = jnp.exp(gcol)
    etail = jnp.exp(glast - gcol)
    cd = jnp.exp(glast)
    a = jnp.where(strict, bcol * kk * decay, 0.0)
    t = _inv_unit_lower(a, eye_c)
    kb = k * bcol
    rhs_w = kb * egc
    u = _dot_hi(t, v * bcol)
    w = _dot_hi(t, rhs_w)
    attn = qk * decay
    v_new = u - _dot(w, s)
    qd = q * egc
    kt = k * etail
    out = _dot(qd, s) + _dot(attn, v_new)
    s_new = s * cd + _dot_tn(kt, v_new)
    return dict(decay=decay, egc=egc, etail=etail, cd=cd, a=a, t=t, kb=kb, rhs_w=rhs_w, u=u, w=w, attn=attn,
                v_new=v_new, qd=qd, kt=kt, out=out, s_new=s_new)


def gdn_fwd(proj, conv_w, a_log, dt_bias, norm_w, *, name):
    rows_total = proj.shape[0]
    nb = rows_total // GDN_ROWS
    hb = GDN_ROWS // SUBLANES
    cpb = GDN_ROWS // CHUNK

    def body(p_ref, halo_ref, cw_ref, alog_ref, dtb_ref, nw_ref, mix_ref, st_ref,
             ext_ref, qkv_ref, beta_ref, g_ref, s_ref, gct_ref):
        i = pl.program_id(0)

        @pl.when(i == 0)
        def _():
            s_ref[...] = jnp.zeros_like(s_ref)

        _gdn_prologue(i, p_ref, halo_ref, cw_ref, alog_ref, dtb_ref, ext_ref, qkv_ref, beta_ref, g_ref)
        ltri, eye_l, eye_c = _tri(CHUNK), _eye(LANES), _eye(CHUNK)
        causal = _iota((CHUNK, CHUNK), 1) <= _iota((CHUNK, CHUNK), 0)
        strict = _iota((CHUNK, CHUNK), 1) < _iota((CHUNK, CHUNK), 0)
        nw = nw_ref[...]

        def chunk(c, carry):
            rows = pl.ds(pl.multiple_of(c * CHUNK, CHUNK), CHUNK)
            gc = _dot_hi(ltri, g_ref[rows, :])
            gct_ref[...] = _dot_nt_hi(eye_l, gc)
            glast_row = _row(gc, CHUNK - 1)
            beta_c = beta_ref[rows, :]
            for hq in range(GDN_QKH):
                q = qkv_ref[rows, hq * GDN_HEAD:(hq + 1) * GDN_HEAD]
                k = qkv_ref[rows, GDN_K0 + hq * GDN_HEAD:GDN_K0 + (hq + 1) * GDN_HEAD]
                kk = _dot_nt(k, k)
                qk = _dot_nt(q, k)
                for h in (2 * hq, 2 * hq + 1):
                    hc = slice(h * GDN_HEAD, (h + 1) * GDN_HEAD)
                    v = qkv_ref[rows, GDN_V0 + h * GDN_HEAD:GDN_V0 + (h + 1) * GDN_HEAD]
                    s = s_ref[h]
                    st_ref[c, h] = s
                    f = _gdn_head_fwd(q, k, v, kk, qk, _col(gc, GDN_GL + h), gct_ref[GDN_GL + h:GDN_GL + h + 1, :],
                                      _col(glast_row, GDN_GL + h), _col(beta_c, h), s, causal, strict, eye_c)
                    s_ref[h] = f["s_new"]
                    o = f["out"]
                    r = lax.rsqrt(jnp.mean(o * o, axis=-1, keepdims=True) + RMS_EPS)
                    mix_ref[rows, hc] = o * r * nw * _silu(p_ref[rows, GDN_Z0 + h * GDN_HEAD:GDN_Z0 + (h + 1) * GDN_HEAD])
            return carry

        lax.fori_loop(0, cpb, chunk, 0)

    vec = lambda n: pl.BlockSpec((1, n), lambda i: (0, 0))
    return pl.pallas_call(
        body, name=name, grid=(nb,),
        in_specs=[pl.BlockSpec((GDN_ROWS, GDN_IN_PAD), lambda i: (i, 0)),
                  pl.BlockSpec((SUBLANES, GDN_IN_PAD), lambda i: (jnp.maximum(i * hb - 1, 0), 0)),
                  pl.BlockSpec((SUBLANES, GDN_CONV_DIM), lambda i: (0, 0)),
                  vec(LANES), vec(LANES), vec(GDN_HEAD)],
        out_specs=[pl.BlockSpec((GDN_ROWS, GDN_V), lambda i: (i, 0)),
                   pl.BlockSpec((cpb, GDN_VH, GDN_HEAD, GDN_HEAD), lambda i: (i, 0, 0, 0))],
        out_shape=[jax.ShapeDtypeStruct((rows_total, GDN_V), F32),
                   jax.ShapeDtypeStruct((rows_total // CHUNK, GDN_VH, GDN_HEAD, GDN_HEAD), F32)],
        scratch_shapes=[pltpu.VMEM((GDN_ROWS + SUBLANES, GDN_CONV_DIM), F32),
                        pltpu.VMEM((GDN_ROWS, GDN_CONV_DIM), F32),
                        pltpu.VMEM((GDN_ROWS, LANES), F32),
                        pltpu.VMEM((GDN_ROWS, LANES), F32),
                        pltpu.VMEM((GDN_VH, GDN_HEAD, GDN_HEAD), F32),
                        pltpu.VMEM((LANES, CHUNK), F32)],
        compiler_params=_cparams(("arbitrary",)),
    )(proj, proj, _pad_rows(conv_w), _gdn_lane_params(a_log), _gdn_lane_params(dt_bias), norm_w.reshape(1, -1))


def gdn_bwd(proj, conv_w, a_log, dt_bias, norm_w, states, dmix, *, name):
    rows_total = proj.shape[0]
    nb = rows_total // GDN_ROWS
    hb = GDN_ROWS // SUBLANES
    cpb = GDN_ROWS // CHUNK

    def body(p_ref, halo_ref, cw_ref, alog_ref, dtb_ref, nw_ref, st_ref, dm_ref,
             dp_ref, dcw_ref, dalog_ref, ddtb_ref, dnw_ref,
             ext_ref, qkv_ref, beta_ref, g_ref, ds_ref, gct_ref, dext_ref, dgc_ref, dgct_ref, dbeta_ref):
        i = pl.program_id(0)
        blk = nb - 1 - i

        @pl.when(i == 0)
        def _():
            ds_ref[...] = jnp.zeros_like(ds_ref)
            dext_ref[GDN_ROWS:, :] = jnp.zeros((SUBLANES, GDN_CONV_DIM), F32)
            for r in (dcw_ref, dalog_ref, ddtb_ref, dnw_ref):
                r[...] = jnp.zeros_like(r)

        _gdn_prologue(blk, p_ref, halo_ref, cw_ref, alog_ref, dtb_ref, ext_ref, qkv_ref, beta_ref, g_ref)
        ltri, utri, eye_l, eye_c = _tri(CHUNK), _tri(CHUNK, lower=False), _eye(LANES), _eye(CHUNK)
        causal = _iota((CHUNK, CHUNK), 1) <= _iota((CHUNK, CHUNK), 0)
        strict = _iota((CHUNK, CHUNK), 1) < _iota((CHUNK, CHUNK), 0)
        lane = _iota((CHUNK, LANES), 1)
        is_last = _iota((CHUNK, 1), 0) == CHUNK - 1
        nw = nw_ref[...]

        def chunk(cc, carry):
            c = cpb - 1 - cc
            rows = pl.ds(pl.multiple_of(c * CHUNK, CHUNK), CHUNK)
            g_c = g_ref[rows, :]
            gc = _dot_hi(ltri, g_c)
            gct_ref[...] = _dot_nt_hi(eye_l, gc)
            glast_row = _row(gc, CHUNK - 1)
            beta_c = beta_ref[rows, :]
            dgc_ref[...] = jnp.zeros_like(dgc_ref)
            dgct_ref[...] = jnp.zeros_like(dgct_ref)
            dbeta_ref[...] = jnp.zeros_like(dbeta_ref)
            for hq in range(GDN_QKH):
                qc = slice(hq * GDN_HEAD, (hq + 1) * GDN_HEAD)
                kc = slice(GDN_K0 + hq * GDN_HEAD, GDN_K0 + (hq + 1) * GDN_HEAD)
                q = qkv_ref[rows, qc]
                k = qkv_ref[rows, kc]
                kk = _dot_nt(k, k)
                qk = _dot_nt(q, k)
                dq_acc = jnp.zeros((CHUNK, GDN_HEAD), F32)
                dk_acc = jnp.zeros((CHUNK, GDN_HEAD), F32)
                for h in (2 * hq, 2 * hq + 1):
                    hc = slice(h * GDN_HEAD, (h + 1) * GDN_HEAD)
                    vc = slice(GDN_V0 + h * GDN_HEAD, GDN_V0 + (h + 1) * GDN_HEAD)
                    zc = slice(GDN_Z0 + h * GDN_HEAD, GDN_Z0 + (h + 1) * GDN_HEAD)
                    v = qkv_ref[rows, vc]
                    s = st_ref[c, h]
                    bcol = _col(beta_c, h)
                    f = _gdn_head_fwd(q, k, v, kk, qk, _col(gc, GDN_GL + h), gct_ref[GDN_GL + h:GDN_GL + h + 1, :],
                                      _col(glast_row, GDN_GL + h), bcol, s, causal, strict, eye_c)
                    o = f["out"]
                    z = p_ref[rows, zc]
                    sz = _silu(z)
                    r = lax.rsqrt(jnp.mean(o * o, axis=-1, keepdims=True) + RMS_EPS)
                    on = o * r
                    dm = dm_ref[rows, hc]
                    dnw_ref[...] += jnp.sum(dm * on * sz, axis=0, keepdims=True)
                    d_on = dm * nw * sz
                    dp_ref[rows, zc] = dm * on * nw * _dsilu(z)
                    do = r * (d_on - on * jnp.mean(d_on * on, axis=-1, keepdims=True))
                    ds_n = ds_ref[h]
                    d_vnew = _dot_tn(f["attn"], do) + _dot(f["kt"], ds_n)
                    d_attn = jnp.where(causal, _dot_nt(do, f["v_new"]), 0.0)
                    d_qd = _dot_nt(do, s)
                    ds_ref[h] = _dot_tn(f["qd"], do) + f["cd"] * ds_n - _dot_tn(f["w"], d_vnew)
                    d_cd = jnp.sum(jnp.sum(s * ds_n, axis=1, keepdims=True), axis=0, keepdims=True)
                    d_kt = _dot_nt(f["v_new"], ds_n)
                    d_w = -_dot_nt(d_vnew, s)
                    tt = (((0,), (0,)), ((), ()))
                    d_rhs_u = _dot_hi(f["t"], d_vnew, tt)
                    d_rhs_w = _dot_hi(f["t"], d_w, tt)
                    da = -jnp.where(strict, _dot_nt(d_rhs_u, f["u"]) + _dot_nt(d_rhs_w, f["w"]), 0.0)
                    dmm = da * f["decay"]
                    em = da * f["a"] + d_attn * f["attn"]
                    d_kb = _dot(dmm, k) + d_rhs_w * f["egc"]
                    dk = _dot_tn(dmm, f["kb"])
                    dpm = d_attn * f["decay"]
                    dq = _dot(dpm, k) + d_qd * f["egc"]
                    dk = dk + _dot_tn(dpm, q) + d_kb * bcol + d_kt * f["etail"]
                    tmp = jnp.sum(d_kt * f["kt"], axis=1, keepdims=True)
                    d_gcol = (jnp.sum(em, axis=1, keepdims=True) + jnp.sum(d_rhs_w * f["rhs_w"], axis=1, keepdims=True)
                              + jnp.sum(d_qd * f["qd"], axis=1, keepdims=True) - tmp)
                    d_glast = jnp.sum(tmp, axis=0, keepdims=True) + d_cd * f["cd"]
                    d_gcol = jnp.where(is_last, d_gcol + d_glast, d_gcol)
                    d_beta = jnp.sum(d_rhs_u * v, axis=1, keepdims=True) + jnp.sum(d_kb * k, axis=1, keepdims=True)
                    dgc_ref[...] += jnp.where(lane == GDN_GL + h, d_gcol, 0.0)
                    dgct_ref[GDN_GL + h:GDN_GL + h + 1, :] = jnp.sum(em, axis=0, keepdims=True)
                    dbeta_ref[...] += jnp.where(lane == h, d_beta, 0.0)
                    dext_ref[rows, vc] = d_rhs_u * bcol
                    dq_acc = dq_acc + dq
                    dk_acc = dk_acc + dk
                dext_ref[rows, qc] = dq_acc
                dext_ref[rows, kc] = dk_acc
            d_gc = dgc_ref[...] - _dot_nt_hi(eye_c, dgct_ref[...])
            dg = _dot_hi(utri, d_gc)
            ba = p_ref[rows, GDN_BA0:GDN_BA0 + LANES]
            d_sp = dg * -jnp.exp(alog_ref[...])
            d_araw = d_sp * _sigmoid(ba + dtb_ref[...])
            d_araw = jnp.where((lane >= GDN_GL) & (lane < GDN_GL + GDN_VH), d_araw, 0.0)
            dalog_ref[...] += jnp.sum(dg * g_c, axis=0, keepdims=True)
            ddtb_ref[...] += jnp.sum(d_araw, axis=0, keepdims=True)
            d_braw = jnp.where(lane < GDN_VH, dbeta_ref[...] * beta_c * (1.0 - beta_c), 0.0)
            dp_ref[rows, GDN_BA0:GDN_BA0 + LANES] = d_braw + d_araw
            return carry

        lax.fori_loop(0, cpb, chunk, 0)
        w = cw_ref[...]
        for hh in range(GDN_CONV_DIM // GDN_HEAD):
            cols = slice(hh * GDN_HEAD, (hh + 1) * GDN_HEAD)
            pre = None
            for j in range(GDN_CONV):
                off = SUBLANES - (GDN_CONV - 1) + j
                term = ext_ref[off:off + GDN_ROWS, cols] * w[j:j + 1, cols]
                pre = term if pre is None else pre + term
            d_act = dext_ref[0:GDN_ROWS, cols]
            if hh < 2 * GDN_QKH:
                a = _silu(pre)
                r = lax.rsqrt(jnp.sum(a * a, axis=-1, keepdims=True) + L2_EPS)
                ah = a * r
                if hh < GDN_QKH:
                    d_act = d_act * GDN_SCALE
                d_act = r * (d_act - ah * jnp.sum(d_act * ah, axis=-1, keepdims=True))
            d_pre = d_act * _dsilu(pre)
            dext_ref[0:GDN_ROWS, cols] = d_pre
            for j in range(GDN_CONV):
                off = SUBLANES - (GDN_CONV - 1) + j
                dcw_ref[j:j + 1, cols] += jnp.sum(ext_ref[off:off + GDN_ROWS, cols] * d_pre, axis=0, keepdims=True)
        dp_ref[:, 0:GDN_CONV_DIM] = _conv_dgrad_from_ext(dext_ref, w, GDN_CONV, GDN_ROWS)
        dext_ref[GDN_ROWS:, :] = dext_ref[0:SUBLANES, :]

    vec = lambda n: pl.BlockSpec((1, n), lambda i: (0, 0))
    outs = pl.pallas_call(
        body, name=name, grid=(nb,),
        in_specs=[pl.BlockSpec((GDN_ROWS, GDN_IN_PAD), lambda i: (nb - 1 - i, 0)),
                  pl.BlockSpec((SUBLANES, GDN_IN_PAD), lambda i: (jnp.maximum((nb - 1 - i) * hb - 1, 0), 0)),
                  pl.BlockSpec((SUBLANES, GDN_CONV_DIM), lambda i: (0, 0)),
                  vec(LANES), vec(LANES), vec(GDN_HEAD),
                  pl.BlockSpec((cpb, GDN_VH, GDN_HEAD, GDN_HEAD), lambda i: (nb - 1 - i, 0, 0, 0)),
                  pl.BlockSpec((GDN_ROWS, GDN_V), lambda i: (nb - 1 - i, 0))],
        out_specs=[pl.BlockSpec((GDN_ROWS, GDN_IN_PAD), lambda i: (nb - 1 - i, 0)),
                   pl.BlockSpec((SUBLANES, GDN_CONV_DIM), lambda i: (0, 0)),
                   vec(LANES), vec(LANES), vec(GDN_HEAD)],
        out_shape=[jax.ShapeDtypeStruct((rows_total, GDN_IN_PAD), F32),
                   jax.ShapeDtypeStruct((SUBLANES, GDN_CONV_DIM), F32),
                   jax.ShapeDtypeStruct((1, LANES), F32), jax.ShapeDtypeStruct((1, LANES), F32),
                   jax.ShapeDtypeStruct((1, GDN_HEAD), F32)],
        scratch_shapes=[pltpu.VMEM((GDN_ROWS + SUBLANES, GDN_CONV_DIM), F32),
                        pltpu.VMEM((GDN_ROWS, GDN_CONV_DIM), F32),
                        pltpu.VMEM((GDN_ROWS, LANES), F32),
                        pltpu.VMEM((GDN_ROWS, LANES), F32),
                        pltpu.VMEM((GDN_VH, GDN_HEAD, GDN_HEAD), F32),
                        pltpu.VMEM((LANES, CHUNK), F32),
                        pltpu.VMEM((GDN_ROWS + SUBLANES, GDN_CONV_DIM), F32),
                        pltpu.VMEM((CHUNK, LANES), F32),
                        pltpu.VMEM((LANES, CHUNK), F32),
                        pltpu.VMEM((CHUNK, LANES), F32)],
        compiler_params=_cparams(("arbitrary",)),
    )(proj, proj, _pad_rows(conv_w), _gdn_lane_params(a_log), _gdn_lane_params(dt_bias), norm_w.reshape(1, -1),
      states, dmix)
    dproj, dcw, dalog, ddtb, dnw = outs
    return dproj, [dcw[:GDN_CONV], dalog[0, GDN_GL:GDN_GL + GDN_VH], ddtb[0, GDN_GL:GDN_GL + GDN_VH], dnw[0]]


def exchange(src, axes, *, gather, name):
    n = 2 ** len(axes)
    piece_shape = src.shape if gather else src.shape[1:]

    def body(src_ref, out_ref, send_sems, recv_sems, local_sem):
        coords = {a: lax.axis_index(a) for a in MESH_AXES}
        me = 0
        for a in axes:
            me = me * 2 + coords[a]

        def piece(j):
            return src_ref if gather else src_ref.at[j]

        local = pltpu.make_async_copy(piece(me), out_ref.at[me], local_sem)
        local.start()
        copies = []
        for k in range(1, n):
            peer = dict(coords)
            for bit, a in enumerate(reversed(axes)):
                if (k >> bit) & 1:
                    peer[a] = 1 - peer[a]
            cp = pltpu.make_async_remote_copy(
                src_ref=piece(jnp.bitwise_xor(me, k)), dst_ref=out_ref.at[me],
                send_sem=send_sems.at[k - 1], recv_sem=recv_sems.at[k - 1],
                device_id=tuple(peer[a] for a in MESH_AXES), device_id_type=pl.DeviceIdType.MESH)
            cp.start()
            copies.append(cp)
        for cp in copies:
            cp.wait()
        local.wait()

    hbm = pl.BlockSpec(memory_space=pl.ANY)
    return pl.pallas_call(
        body, name=name, in_specs=[hbm], out_specs=hbm,
        out_shape=jax.ShapeDtypeStruct((n,) + tuple(piece_shape), src.dtype),
        scratch_shapes=[pltpu.SemaphoreType.DMA((n - 1,)), pltpu.SemaphoreType.DMA((n - 1,)), pltpu.SemaphoreType.DMA],
    )(src)


def sum_slots(buf, *, name):
    n, rows, cols = buf.shape
    tr = _pick(rows, (512, 256, 128))

    def body(b_ref, o_ref):
        acc = b_ref[0]
        for j in range(1, n):
            acc = acc + b_ref[j]
        o_ref[...] = acc

    return pl.pallas_call(
        body, name=name, grid=(rows // tr,), in_specs=[pl.BlockSpec((n, tr, cols), lambda i: (0, i, 0))],
        out_specs=pl.BlockSpec((tr, cols), lambda i: (i, 0)), out_shape=jax.ShapeDtypeStruct((rows, cols), F32),
        compiler_params=_cparams(("parallel",)),
    )(buf)


def adamw(w, g, m, v, *, name):
    rows, cols = w.shape
    tr = _pick(rows, (256, 128))

    def body(w_ref, g_ref, m_ref, v_ref, d_ref, mo_ref, vo_ref):
        gv = g_ref[...]
        mn = ADAM_B1 * m_ref[...] + (1.0 - ADAM_B1) * gv
        vn = ADAM_B2 * v_ref[...] + (1.0 - ADAM_B2) * (gv * gv)
        m_hat = mn / (1.0 - ADAM_B1 ** ADAM_STEP)
        v_hat = vn / (1.0 - ADAM_B2 ** ADAM_STEP)
        d_ref[...] = -ADAM_LR * (m_hat / (jnp.sqrt(v_hat) + ADAM_EPS) + ADAM_WD * w_ref[...])
        mo_ref[...] = mn
        vo_ref[...] = vn

    blk = pl.BlockSpec((tr, cols), lambda i: (i, 0))
    shp = jax.ShapeDtypeStruct((rows, cols), F32)
    return pl.pallas_call(
        body, name=name, grid=(rows // tr,), in_specs=[blk] * 4, out_specs=[blk] * 3, out_shape=[shp] * 3,
        compiler_params=_cparams(("parallel",)),
    )(w, g, m, v)


N_SHARDS = 4
FLAT_COLS = 1024
W_SPECS = (
    ("gdn_w_in", (2, 1024, 6176), 2), ("gdn_conv_w", (2, 4, 4096), 2), ("gdn_a_log", (2, 16), None),
    ("gdn_dt_bias", (2, 16), None), ("gdn_norm_w", (2, 128), None), ("gdn_w_out", (2, 2048, 1024), 1),
    ("sc_w_in", (1, 1024, 8192), 2), ("sc_conv_w", (1, 3, 2048), 2), ("sc_w_out", (1, 2048, 1024), 1),
    ("ssd_w_in", (1, 1024, 5152), 2), ("ssd_conv_w", (1, 4, 3072), 2), ("ssd_conv_b", (1, 3072), 1),
    ("ssd_a_log", (1, 32), None), ("ssd_dt_bias", (1, 32), None), ("ssd_d_skip", (1, 32), None),
    ("ssd_norm_w", (1, 2048), 1), ("ssd_w_out", (1, 2048, 1024), 1), ("ln_g", (4, 1024), None), ("ln_b", (4, 1024), None),
)


def _local_shape(shape, axis):
    return shape if axis is None else tuple(d // N_SHARDS if i == axis else d for i, d in enumerate(shape))


def _size(shape):
    n = 1
    for d in shape:
        n *= d
    return n


FLAT_USED = sum(_size(_local_shape(s, a)) for _, s, a in W_SPECS)
FLAT_ROWS = -(-FLAT_USED // (FLAT_COLS * 512)) * 512
FLAT_HALF = FLAT_ROWS // 2


def _pack(pieces):
    flat = jnp.concatenate([p.reshape(-1) for p in pieces] + [jnp.zeros((FLAT_ROWS * FLAT_COLS - FLAT_USED,), F32)])
    return flat.reshape(FLAT_ROWS, FLAT_COLS)


def _unpack(flat):
    flat = flat.reshape(-1)
    out, off = [], 0
    for _, shape, axis in W_SPECS:
        ls = _local_shape(shape, axis)
        out.append(flat[off:off + _size(ls)].reshape(ls))
        off += _size(ls)
    return out


def _shard_of(full, axis, s):
    if axis is None:
        return full
    n = full.shape[axis] // N_SHARDS
    return lax.slice_in_dim(full, s * n, (s + 1) * n, axis=axis)


def _adamw_all(weights, grads_flat, moms, vels):
    delta, new_m, new_v = adamw(_pack(weights), grads_flat, _pack(moms), _pack(vels), name="adamw")
    return _unpack(grads_flat), _unpack(delta), _unpack(new_m), _unpack(new_v)


def _reduce_scatter(full_grads):
    by_shard = jnp.stack([_pack([_shard_of(g, a, s) for g, (_, _, a) in zip(full_grads, W_SPECS)])
                          for s in range(N_SHARDS)])
    by_half = by_shard.reshape(N_SHARDS, 2, FLAT_HALF, FLAT_COLS).transpose(1, 0, 2, 3)
    pair = exchange(by_half, ("c",), gather=False, name="rs_pair")
    pair_sum = sum_slots(pair.reshape(2, N_SHARDS * FLAT_HALF, FLAT_COLS), name="rs_pair_sum")
    chips = exchange(pair_sum.reshape(N_SHARDS, FLAT_HALF, FLAT_COLS), ("x", "y"), gather=False, name="rs_chips")
    half = sum_slots(chips, name="rs_chip_sum")
    return exchange(half, ("c",), gather=True, name="rs_halves").reshape(FLAT_ROWS, FLAT_COLS)


def _gather_weights(local_weights):
    gathered = exchange(_pack(local_weights), ("x", "y"), gather=True, name="gather_weights")
    per_shard = [_unpack(gathered[s]) for s in range(N_SHARDS)]
    full = []
    for i, (_, _, axis) in enumerate(W_SPECS):
        if axis is None:
            full.append(local_weights[i])
        else:
            full.append(jnp.concatenate([per_shard[s][i] for s in range(N_SHARDS)], axis=axis))
    return full


def _mxu_weight(w, pad_to=None, axis=1):
    if pad_to is not None:
        pads = [(0, 0), (0, 0)]
        pads[axis] = (0, pad_to - w.shape[axis])
        w = jnp.pad(w, pads)
    return w.astype(MXU_DTYPE)


def kernel(x, gdn_w_in, gdn_conv_w, gdn_a_log, gdn_dt_bias, gdn_norm_w, gdn_w_out, sc_w_in, sc_conv_w, sc_w_out, ssd_w_in, ssd_conv_w, ssd_conv_b, ssd_a_log, ssd_dt_bias, ssd_d_skip, ssd_norm_w, ssd_w_out, ln_g, ln_b, loss_target, m_gdn_w_in, m_gdn_conv_w, m_gdn_a_log, m_gdn_dt_bias, m_gdn_norm_w, m_gdn_w_out, m_sc_w_in, m_sc_conv_w, m_sc_w_out, m_ssd_w_in, m_ssd_conv_w, m_ssd_conv_b, m_ssd_a_log, m_ssd_dt_bias, m_ssd_d_skip, m_ssd_norm_w, m_ssd_w_out, m_ln_g, m_ln_b, v_gdn_w_in, v_gdn_conv_w, v_gdn_a_log, v_gdn_dt_bias, v_gdn_norm_w, v_gdn_w_out, v_sc_w_in, v_sc_conv_w, v_sc_w_out, v_ssd_w_in, v_ssd_conv_w, v_ssd_conv_b, v_ssd_a_log, v_ssd_dt_bias, v_ssd_d_skip, v_ssd_norm_w, v_ssd_w_out, v_ln_g, v_ln_b):
    weights = [gdn_w_in, gdn_conv_w, gdn_a_log, gdn_dt_bias, gdn_norm_w, gdn_w_out, sc_w_in, sc_conv_w, sc_w_out,
               ssd_w_in, ssd_conv_w, ssd_conv_b, ssd_a_log, ssd_dt_bias, ssd_d_skip, ssd_norm_w, ssd_w_out, ln_g, ln_b]
    moms = [m_gdn_w_in, m_gdn_conv_w, m_gdn_a_log, m_gdn_dt_bias, m_gdn_norm_w, m_gdn_w_out, m_sc_w_in, m_sc_conv_w,
            m_sc_w_out, m_ssd_w_in, m_ssd_conv_w, m_ssd_conv_b, m_ssd_a_log, m_ssd_dt_bias, m_ssd_d_skip, m_ssd_norm_w,
            m_ssd_w_out, m_ln_g, m_ln_b]
    vels = [v_gdn_w_in, v_gdn_conv_w, v_gdn_a_log, v_gdn_dt_bias, v_gdn_norm_w, v_gdn_w_out, v_sc_w_in, v_sc_conv_w,
            v_sc_w_out, v_ssd_w_in, v_ssd_conv_w, v_ssd_conv_b, v_ssd_a_log, v_ssd_dt_bias, v_ssd_d_skip, v_ssd_norm_w,
            v_ssd_w_out, v_ln_g, v_ln_b]
    full = dict(zip([n for n, _, _ in W_SPECS], _gather_weights(weights)))
    x0 = x[0]
    target = loss_target[0]

    layers = (("gdn", 0, GDN_IN_PAD, GDN_IN), ("sc", 0, SC_IN, SC_IN), ("ssd", 0, SSD_IN_PAD, SSD_IN), ("gdn", 1, GDN_IN_PAD, GDN_IN))

    def params(kind, j):
        if kind == "gdn":
            return [full["gdn_conv_w"][j], full["gdn_a_log"][j], full["gdn_dt_bias"][j], full["gdn_norm_w"][j]]
        if kind == "sc":
            return [full["sc_conv_w"][j]]
        return [full["ssd_conv_w"][j], full["ssd_conv_b"][j], full["ssd_a_log"][j], full["ssd_dt_bias"][j],
                full["ssd_d_skip"][j], full["ssd_norm_w"][j]]

    xs, saved = [x0], []
    for i, (kind, j, n_pad, _) in enumerate(layers):
        w_in = _mxu_weight(full[kind + "_w_in"][j], n_pad)
        w_out = _mxu_weight(full[kind + "_w_out"][j])
        proj = matmul(xs[i], w_in, name=kind + "_proj")
        if kind == "gdn":
            mix, states = gdn_fwd(proj, *params(kind, j), name="gdn_fwd")
        elif kind == "sc":
            mix, states = sc_fwd(proj, *params(kind, j), name="sc_fwd"), None
        else:
            mix, states = ssd_fwd(proj, *params(kind, j), name="ssd_fwd")
        y = matmul(mix, w_out, name=kind + "_out")
        saved.append((w_in, w_out, proj, mix, states, y))
        if i + 1 < DEPTH:
            xs.append(ln_fwd(xs[i], y, full["ln_g"][i], full["ln_b"][i], name="ln_fwd"))

    grads = {n: [None] * s[0] for n, s, _ in W_SPECS}
    dr, dg, db, loss_rows = ln_bwd(xs[DEPTH - 1], saved[DEPTH - 1][5], full["ln_g"][DEPTH - 1], b=full["ln_b"][DEPTH - 1],
                                   target=target, name="ln_bwd_loss")
    dx = None
    for i in reversed(range(DEPTH)):
        kind, j, _, n_in = layers[i]
        w_in, w_out, proj, mix, states, _ = saved[i]
        grads["ln_g"][i], grads["ln_b"][i] = dg[0], db[0]
        dmix = matmul(dr, w_out, tb=True, name=kind + "_dmix")
        grads[kind + "_w_out"][j] = matmul(mix, dr, ta=True, name=kind + "_dw_out")
        if kind == "gdn":
            dproj, (dcw, dalog, ddtb, dnw) = gdn_bwd(proj, *params(kind, j), states, dmix, name="gdn_bwd")
            grads["gdn_conv_w"][j], grads["gdn_a_log"][j], grads["gdn_dt_bias"][j], grads["gdn_norm_w"][j] = dcw, dalog, ddtb, dnw
        elif kind == "sc":
            dproj, dcw = sc_bwd(proj, *params(kind, j), dmix, name="sc_bwd")
            grads["sc_conv_w"][j] = dcw[:SC_CONV]
        else:
            dproj, (dcw, dcb, dalog, ddtb, ddsk, dnw) = ssd_bwd(proj, *params(kind, j), states, dmix, name="ssd_bwd")
            grads["ssd_conv_w"][j], grads["ssd_conv_b"][j], grads["ssd_a_log"][j] = dcw, dcb, dalog
            grads["ssd_dt_bias"][j], grads["ssd_d_skip"][j], grads["ssd_norm_w"][j] = ddtb, ddsk, dnw
        grads[kind + "_w_in"][j] = matmul(xs[i], dproj, ta=True, name=kind + "_dw_in")[:, :n_in]
        dx = matmul(dproj, w_in, tb=True, add=dr, add_scale=ALPHA, name=kind + "_dx")
        if i > 0:
            dr, dg, db = ln_bwd(xs[i - 1], saved[i - 1][5], full["ln_g"][i - 1], dx, name="ln_bwd")

    full_grads = [jnp.stack(grads[n]) for n, _, _ in W_SPECS]
    grads_flat = _reduce_scatter(full_grads)
    g_out, d_out, m_out, v_out = _adamw_all(weights, grads_flat, moms, vels)
    loss = lax.psum(loss_rows[0, 0], MESH_AXES)
    return (loss, dx[None], *g_out, *d_out, *m_out, *v_out)
```

```python
import functools

import jax
import jax.numpy as jnp
from jax import lax
from jax.experimental import pallas as pl
from jax.experimental.pallas import tpu as pltpu

F32 = jnp.float32
MXU_DTYPE = jnp.bfloat16

D_MODEL = 1024
DEPTH = 4
D_INNER = 2048
CHUNK = 64
LANES = 128
SUBLANES = 8
VMEM_LIMIT = 56 * 1024 * 1024

GDN_HEAD = 128
GDN_VH = 16
GDN_QKH = 8
GDN_QK = 1024
GDN_V = 2048
GDN_CONV = 4
GDN_CONV_DIM = 4096
GDN_IN = 6176
GDN_IN_PAD = 6272

SC_W = 2048
SC_CONV = 3
SC_IN = 8192

SSD_P = 64
SSD_H = 32
SSD_G = 4
SSD_S = 128
SSD_CONV = 4
SSD_CONV_DIM = 3072
SSD_IN = 5152
SSD_IN_PAD = 5376

ALPHA = (2 * DEPTH) ** 0.25
RMS_EPS = 1e-6
LN_EPS = 1e-5
L2_EPS = 1e-6

ADAM_LR = 0.001
ADAM_B1 = 0.9
ADAM_B2 = 0.999
ADAM_EPS = 1e-08
ADAM_WD = 0.01
ADAM_STEP = 10

MESH_AXES = ("x", "y", "c")


def _cparams(sem):
    return pltpu.CompilerParams(dimension_semantics=sem, vmem_limit_bytes=VMEM_LIMIT)


def _pick(n, prefs):
    for p in prefs:
        if n % p == 0:
            return p
    return n


def _dot(a, b, dims=(((1,), (0,)), ((), ()))):
    return lax.dot_general(a.astype(MXU_DTYPE), b.astype(MXU_DTYPE), dims, preferred_element_type=F32)


def _dot_nt(a, b):
    return _dot(a, b, (((1,), (1,)), ((), ())))


def _dot_tn(a, b):
    return _dot(a, b, (((0,), (0,)), ((), ())))


NN = (((1,), (0,)), ((), ()))
NT = (((1,), (1,)), ((), ()))
TN = (((0,), (0,)), ((), ()))


def _mxu(a, b, dims):
    return lax.dot_general(a, b, dims, preferred_element_type=F32)


def _split(x, pieces):
    out, r = [], x
    for i in range(pieces):
        p = r.astype(jnp.bfloat16)
        out.append(p)
        if i + 1 < pieces:
            r = r - p.astype(F32)
    return out


def _dot_x3(a, b, dims=NN):
    (ah, al), (bh, bl) = _split(a, 2), _split(b, 2)
    return (_mxu(al, bh, dims) + _mxu(ah, bl, dims)) + _mxu(ah, bh, dims)


def _sel(m, x, dims=NN):
    mb = m.astype(jnp.bfloat16)
    x1, x2, x3 = _split(x, 3)
    return (_mxu(mb, x3, dims) + _mxu(mb, x2, dims)) + _mxu(mb, x1, dims)


def _sel_nt(m, x):
    return _sel(m, x, NT)


def _xsel(x, m, dims=NN):
    mb = m.astype(jnp.bfloat16)
    x1, x2, x3 = _split(x, 3)
    return (_mxu(x3, mb, dims) + _mxu(x2, mb, dims)) + _mxu(x1, mb, dims)


def _xsel_nt(x, m):
    return _xsel(x, m, NT)


def _iota(shape, dim):
    return lax.broadcasted_iota(jnp.int32, shape, dim)


def _sigmoid(x):
    return 1.0 / (1.0 + jnp.exp(-x))


def _silu(x):
    return x * _sigmoid(x)


def _dsilu(x):
    s = _sigmoid(x)
    return s * (1.0 + x * (1.0 - s))


def _softplus(x):
    return jnp.maximum(x, 0.0) + jnp.log(1.0 + jnp.exp(-jnp.abs(x)))


def matmul(a, b, *, ta=False, tb=False, add=None, add_scale=1.0, name):
    if ta:
        kdim, m = a.shape
    else:
        m, kdim = a.shape
    n = b.shape[0] if tb else b.shape[1]
    assert (b.shape[1] if tb else b.shape[0]) == kdim
    tm = _pick(m, (1024, 896, 768, 512)) if ta else _pick(m, (512, 256, 128))
    tn = _pick(n, (1024, 896, 768, 512, 256, 128))
    tk = _pick(kdim, (512, 256)) if ta else _pick(kdim, (1024, 896, 768, 512))
    nk = kdim // tk
    dims = (((0 if ta else 1,), (1 if tb else 0,)), ((), ()))

    def body(a_ref, b_ref, *rest):
        if add is None:
            o_ref, acc_ref = rest
        else:
            add_ref, o_ref, acc_ref = rest
        k = pl.program_id(2)

        @pl.when(k == 0)
        def _():
            acc_ref[...] = jnp.zeros_like(acc_ref)

        acc_ref[...] += _dot(a_ref[...], b_ref[...], dims)

        @pl.when(k == nk - 1)
        def _():
            if add is None:
                o_ref[...] = acc_ref[...]
            else:
                o_ref[...] = acc_ref[...] + add_scale * add_ref[...]

    a_spec = pl.BlockSpec((tk, tm), lambda i, j, k: (k, i)) if ta else pl.BlockSpec((tm, tk), lambda i, j, k: (i, k))
    b_spec = pl.BlockSpec((tn, tk), lambda i, j, k: (j, k)) if tb else pl.BlockSpec((tk, tn), lambda i, j, k: (k, j))
    o_spec = pl.BlockSpec((tm, tn), lambda i, j, k: (i, j))
    in_specs = [a_spec, b_spec] + ([] if add is None else [o_spec])
    args = (a, b) + (() if add is None else (add,))
    return pl.pallas_call(
        body, name=name, grid=(m // tm, n // tn, nk), in_specs=in_specs, out_specs=o_spec,
        out_shape=jax.ShapeDtypeStruct((m, n), F32), scratch_shapes=[pltpu.VMEM((tm, tn), F32)],
        compiler_params=_cparams(("parallel", "parallel", "arbitrary")),
    )(*args)


LN_ROWS = 512


def _ln_stats(x, y):
    r = ALPHA * x + y
    mu = jnp.mean(r, axis=-1, keepdims=True)
    rc = r - mu
    var = jnp.mean(rc * rc, axis=-1, keepdims=True)
    rstd = lax.rsqrt(var + LN_EPS)
    return rc * rstd, rstd


def ln_fwd(x, y, g, b, *, name):
    rows, d = x.shape

    def body(x_ref, y_ref, g_ref, b_ref, o_ref):
        xhat, _ = _ln_stats(x_ref[...], y_ref[...])
        o_ref[...] = xhat * g_ref[...] + b_ref[...]

    blk = pl.BlockSpec((LN_ROWS, d), lambda i: (i, 0))
    vec = pl.BlockSpec((1, d), lambda i: (0, 0))
    return pl.pallas_call(
        body, name=name, grid=(rows // LN_ROWS,), in_specs=[blk, blk, vec, vec], out_specs=blk,
        out_shape=jax.ShapeDtypeStruct((rows, d), F32), compiler_params=_cparams(("parallel",)),
    )(x, y, g.reshape(1, d), b.reshape(1, d))


def ln_bwd(x, y, g, dxn=None, *, b=None, target=None, name):
    rows, d = x.shape
    final = target is not None

    def body(x_ref, y_ref, g_ref, *rest):
        if final:
            b_ref, t_ref, dr_ref, dg_ref, db_ref, loss_ref = rest
        else:
            dxn_ref, dr_ref, dg_ref, db_ref = rest
        i = pl.program_id(0)
        xhat, rstd = _ln_stats(x_ref[...], y_ref[...])
        gv = g_ref[...]
        if final:
            err = xhat * gv + b_ref[...] - t_ref[...]
            dxn_v = err * (1.0 / d)
            part = 0.5 * jnp.sum(jnp.mean(err * err, axis=-1, keepdims=True), axis=0, keepdims=True)
        else:
            dxn_v = dxn_ref[...]
        dxh = dxn_v * gv
        m1 = jnp.mean(dxh, axis=-1, keepdims=True)
        m2 = jnp.mean(dxh * xhat, axis=-1, keepdims=True)
        dr_ref[...] = rstd * (dxh - m1 - xhat * m2)

        @pl.when(i == 0)
        def _():
            dg_ref[...] = jnp.zeros_like(dg_ref)
            db_ref[...] = jnp.zeros_like(db_ref)
            if final:
                loss_ref[...] = jnp.zeros_like(loss_ref)

        dg_ref[...] += jnp.sum(dxn_v * xhat, axis=0, keepdims=True)
        db_ref[...] += jnp.sum(dxn_v, axis=0, keepdims=True)
        if final:
            loss_ref[...] += jnp.broadcast_to(part, loss_ref.shape)

    blk = pl.BlockSpec((LN_ROWS, d), lambda i: (i, 0))
    vec = pl.BlockSpec((1, d), lambda i: (0, 0))
    lvec = pl.BlockSpec((1, LANES), lambda i: (0, 0))
    out_shape = [jax.ShapeDtypeStruct((rows, d), F32), jax.ShapeDtypeStruct((1, d), F32), jax.ShapeDtypeStruct((1, d), F32)]
    out_specs = [blk, vec, vec]
    if final:
        in_specs = [blk, blk, vec, vec, blk]
        args = (x, y, g.reshape(1, d), b.reshape(1, d), target)
        out_shape.append(jax.ShapeDtypeStruct((1, LANES), F32))
        out_specs.append(lvec)
    else:
        in_specs = [blk, blk, vec, blk]
        args = (x, y, g.reshape(1, d), dxn)
    return pl.pallas_call(
        body, name=name, grid=(rows // LN_ROWS,), in_specs=in_specs, out_specs=out_specs, out_shape=out_shape,
        compiler_params=_cparams(("arbitrary",)),
    )(*args)


def _conv_from_ext(ext_ref, w, width, rows):
    out = None
    for j in range(width):
        off = SUBLANES - (width - 1) + j
        term = ext_ref[off:off + rows, :] * w[j:j + 1, :]
        out = term if out is None else out + term
    return out


def _conv_wgrad_from_ext(ext_ref, dout, width, rows):
    res = []
    for j in range(width):
        off = SUBLANES - (width - 1) + j
        res.append(jnp.sum(ext_ref[off:off + rows, :] * dout, axis=0, keepdims=True))
    return res


def _conv_dgrad_from_ext(dext_ref, w, width, rows):
    out = None
    for j in range(width):
        off = (width - 1) - j
        term = dext_ref[off:off + rows, :] * w[j:j + 1, :]
        out = term if out is None else out + term
    return out


SC_ROWS = 128


def sc_fwd(proj, conv_w, *, name):
    rows = proj.shape[0]
    nb = rows // SC_ROWS
    hb = SC_ROWS // SUBLANES

    def body(p_ref, halo_ref, w_ref, o_ref, ext_ref):
        i = pl.program_id(0)
        h = p_ref[:, 0:SC_W]
        bg = p_ref[:, SC_W:2 * SC_W]
        cg = p_ref[:, 2 * SC_W:3 * SC_W]
        z = p_ref[:, 3 * SC_W:4 * SC_W]
        uh = halo_ref[:, 2 * SC_W:3 * SC_W] * halo_ref[:, 0:SC_W]
        ext_ref[0:SUBLANES, :] = jnp.where(i == 0, 0.0, uh)
        ext_ref[SUBLANES:, :] = cg * h
        cv = _conv_from_ext(ext_ref, w_ref[...], SC_CONV, SC_ROWS)
        o_ref[...] = bg * cv * _silu(z)

    return pl.pallas_call(
        body, name=name, grid=(nb,),
        in_specs=[pl.BlockSpec((SC_ROWS, SC_IN), lambda i: (i, 0)),
                  pl.BlockSpec((SUBLANES, SC_IN), lambda i: (jnp.maximum(i * hb - 1, 0), 0)),
                  pl.BlockSpec((SUBLANES, SC_W), lambda i: (0, 0))],
        out_specs=pl.BlockSpec((SC_ROWS, SC_W), lambda i: (i, 0)),
        out_shape=jax.ShapeDtypeStruct((rows, SC_W), F32),
        scratch_shapes=[pltpu.VMEM((SC_ROWS + SUBLANES, SC_W), F32)],
        compiler_params=_cparams(("parallel",)),
    )(proj, proj, _pad_rows(conv_w))


def sc_bwd(proj, conv_w, dmix, *, name):
    rows = proj.shape[0]
    nb = rows // SC_ROWS
    hb = SC_ROWS // SUBLANES

    def body(p_ref, halo_ref, w_ref, dm_ref, dp_ref, dw_ref, ext_ref, dext_ref):
        i = pl.program_id(0)
        blk = nb - 1 - i
        w = w_ref[...]
        h = p_ref[:, 0:SC_W]
        bg = p_ref[:, SC_W:2 * SC_W]
        cg = p_ref[:, 2 * SC_W:3 * SC_W]
        z = p_ref[:, 3 * SC_W:4 * SC_W]
        uh = halo_ref[:, 2 * SC_W:3 * SC_W] * halo_ref[:, 0:SC_W]
        ext_ref[0:SUBLANES, :] = jnp.where(blk == 0, 0.0, uh)
        ext_ref[SUBLANES:, :] = cg * h
        cv = _conv_from_ext(ext_ref, w, SC_CONV, SC_ROWS)
        dm = dm_ref[...]
        sz = _silu(z)
        dy = dm * sz
        dp_ref[:, 3 * SC_W:4 * SC_W] = dm * bg * cv * _dsilu(z)
        dp_ref[:, SC_W:2 * SC_W] = dy * cv
        dcv = dy * bg

        @pl.when(i == 0)
        def _():
            dext_ref[SC_ROWS:, :] = jnp.zeros((SUBLANES, SC_W), F32)
            dw_ref[...] = jnp.zeros_like(dw_ref)

        dext_ref[0:SC_ROWS, :] = dcv
        du = _conv_dgrad_from_ext(dext_ref, w, SC_CONV, SC_ROWS)
        dp_ref[:, 0:SC_W] = du * cg
        dp_ref[:, 2 * SC_W:3 * SC_W] = du * h
        dws = _conv_wgrad_from_ext(ext_ref, dcv, SC_CONV, SC_ROWS)
        for j in range(SC_CONV):
            dw_ref[j:j + 1, :] += dws[j]
        dext_ref[SC_ROWS:, :] = dcv[0:SUBLANES, :]

    return pl.pallas_call(
        body, name=name, grid=(nb,),
        in_specs=[pl.BlockSpec((SC_ROWS, SC_IN), lambda i: (nb - 1 - i, 0)),
                  pl.BlockSpec((SUBLANES, SC_IN), lambda i: (jnp.maximum((nb - 1 - i) * hb - 1, 0), 0)),
                  pl.BlockSpec((SUBLANES, SC_W), lambda i: (0, 0)),
                  pl.BlockSpec((SC_ROWS, SC_W), lambda i: (nb - 1 - i, 0))],
        out_specs=[pl.BlockSpec((SC_ROWS, SC_IN), lambda i: (nb - 1 - i, 0)),
                   pl.BlockSpec((SUBLANES, SC_W), lambda i: (0, 0))],
        out_shape=[jax.ShapeDtypeStruct((rows, SC_IN), F32), jax.ShapeDtypeStruct((SUBLANES, SC_W), F32)],
        scratch_shapes=[pltpu.VMEM((SC_ROWS + SUBLANES, SC_W), F32), pltpu.VMEM((SC_ROWS + SUBLANES, SC_W), F32)],
        compiler_params=_cparams(("arbitrary",)),
    )(proj, proj, _pad_rows(conv_w), dmix)


def _pad_rows(w, rows=SUBLANES):
    return jnp.pad(w, ((0, rows - w.shape[0]), (0, 0)))


def _pad_lanes(v, lanes=LANES):
    v = v.reshape(1, -1)
    return jnp.pad(v, ((0, 0), (0, lanes - v.shape[1])))


def _tri(n, lower=True):
    r, c = _iota((n, n), 0), _iota((n, n), 1)
    return jnp.where((c <= r) if lower else (c >= r), 1.0, 0.0)


def _eye(n):
    return jnp.where(_iota((n, n), 0) == _iota((n, n), 1), 1.0, 0.0)


def _head_expand(n, width):
    return jnp.where(_iota((LANES, n), 1) // width == _iota((LANES, n), 0), 1.0, 0.0)


def _col(v, h):
    return jnp.sum(jnp.where(_iota(v.shape, 1) == h, v, 0.0), axis=1, keepdims=True)


def _row(v, r):
    return jnp.sum(jnp.where(_iota(v.shape, 0) == r, v, 0.0), axis=0, keepdims=True)


def _expand_row(v, e):
    return jnp.max(_xsel(jnp.broadcast_to(v, (SUBLANES, LANES)), e), axis=0, keepdims=True)


SSD_ROWS = 128
SSD_X0 = D_INNER
SSD_DT0 = D_INNER + SSD_CONV_DIM
SSD_B0 = D_INNER
SSD_C0 = D_INNER + SSD_G * SSD_S
SSD_GW = D_INNER // SSD_G
SSD_HG = SSD_H // SSD_G


def _ssd_prologue(blk, p_ref, halo_ref, cw_ref, cb_ref, dtb_ref, ext_ref, xbc_ref, dt_ref):
    ext_ref[0:SUBLANES, :] = jnp.where(blk == 0, 0.0, halo_ref[:, SSD_X0:SSD_DT0])
    ext_ref[SUBLANES:, :] = p_ref[:, SSD_X0:SSD_DT0]
    pre = _conv_from_ext(ext_ref, cw_ref[...], SSD_CONV, SSD_ROWS) + cb_ref[...]
    xbc_ref[...] = _silu(pre)
    dt_ref[...] = _softplus(p_ref[:, SSD_DT0:SSD_DT0 + LANES] + dtb_ref[...])
    return pre


def _ssd_chunk_decays(dt_c, a_row, ltri, eye_l, act_ref):
    da = dt_c * a_row
    ac = _sel(ltri,da)
    act_ref[...] = _sel_nt(eye_l,ac)
    ac_last = _row(ac, CHUNK - 1)
    return ac, jnp.exp(ac_last - ac), jnp.exp(ac), jnp.exp(ac_last)


def _ssd_seg(ac, act_ref, h, causal):
    return jnp.where(causal, jnp.exp(jnp.minimum(_col(ac, h) - act_ref[pl.ds(h, 1), :], 0.0)), 0.0)


def _ssd_half(pair, e):
    upper = _iota(pair.shape, 1) >= SSD_P
    return jnp.where(upper if e % 2 else jnp.logical_not(upper), pair, 0.0)


def _ssd_group_fwd(g, xbc_ref, rows, dt_exp, tail_exp, cdec_exp, ac, act_ref, s_g, causal):
    gl = slice(g * SSD_GW, (g + 1) * SSD_GW)
    bg = xbc_ref[rows, SSD_B0 + g * SSD_S:SSD_B0 + (g + 1) * SSD_S]
    cg = xbc_ref[rows, SSD_C0 + g * SSD_S:SSD_C0 + (g + 1) * SSD_S]
    xdt = xbc_ref[rows, gl] * dt_exp[:, gl]
    cb = _dot_nt(cg, bg)
    cs = _dot(cg, s_g)
    segs = [_ssd_seg(ac, act_ref, g * SSD_HG + e, causal) for e in range(SSD_HG)]
    gms = [seg * cb for seg in segs]
    xps = [xdt[:, p * LANES:(p + 1) * LANES] for p in range(SSD_HG // 2)]
    parts = [_dot(gms[e], _ssd_half(xps[e // 2], e)) for e in range(SSD_HG)]
    yd = jnp.concatenate([parts[2 * p] + parts[2 * p + 1] for p in range(SSD_HG // 2)], axis=1)
    st = _dot_tn(bg, xdt * tail_exp[:, gl])
    return yd + cs * cdec_exp[:, gl], st, bg, cg, cb, xdt, cs, segs, gms


def ssd_fwd(proj, conv_w, conv_b, a_log, dt_bias, d_skip, norm_w, *, name):
    rows_total = proj.shape[0]
    nb = rows_total // SSD_ROWS
    hb = SSD_ROWS // SUBLANES
    cpb = SSD_ROWS // CHUNK

    def body(p_ref, halo_ref, cw_ref, cb_ref, alog_ref, dtb_ref, dsk_ref, nw_ref, mix_ref, st_ref,
             ext_ref, xbc_ref, dt_ref, s_ref, act_ref):
        i = pl.program_id(0)

        @pl.when(i == 0)
        def _():
            s_ref[...] = jnp.zeros_like(s_ref)

        _ssd_prologue(i, p_ref, halo_ref, cw_ref, cb_ref, dtb_ref, ext_ref, xbc_ref, dt_ref)
        a_row = -jnp.exp(alog_ref[...])
        expand = _head_expand(D_INNER, SSD_P)
        dsk_exp = _expand_row(dsk_ref[...], expand)
        ltri, eye_l = _tri(CHUNK), _eye(LANES)
        causal = _iota((CHUNK, CHUNK), 1) <= _iota((CHUNK, CHUNK), 0)

        def chunk(c, carry):
            rows = pl.ds(pl.multiple_of(c * CHUNK, CHUNK), CHUNK)
            dt_c = dt_ref[rows, :]
            ac, tail, cdec, tot = _ssd_chunk_decays(dt_c, a_row, ltri, eye_l, act_ref)
            dt_exp = _xsel(dt_c, expand)
            tail_exp = _xsel(tail, expand)
            cdec_exp = _xsel(cdec, expand)
            tot_exp = _expand_row(tot, expand)
            for g in range(SSD_G):
                gl = slice(g * SSD_GW, (g + 1) * SSD_GW)
                s_g = s_ref[g]
                st_ref[c, g] = s_g
                y, st = _ssd_group_fwd(g, xbc_ref, rows, dt_exp, tail_exp, cdec_exp, ac, act_ref, s_g, causal)[:2]
                s_ref[g] = s_g * tot_exp[:, gl] + st
                y = (y + dsk_exp[:, gl] * xbc_ref[rows, gl]) * _silu(p_ref[rows, gl])
                r = lax.rsqrt(jnp.mean(y * y, axis=-1, keepdims=True) + RMS_EPS)
                mix_ref[rows, gl] = y * r * nw_ref[:, gl]
            return carry

        lax.fori_loop(0, cpb, chunk, 0)

    vec = lambda n: pl.BlockSpec((1, n), lambda i: (0, 0))
    return pl.pallas_call(
        body, name=name, grid=(nb,),
        in_specs=[pl.BlockSpec((SSD_ROWS, SSD_IN_PAD), lambda i: (i, 0)),
                  pl.BlockSpec((SUBLANES, SSD_IN_PAD), lambda i: (jnp.maximum(i * hb - 1, 0), 0)),
                  pl.BlockSpec((SUBLANES, SSD_CONV_DIM), lambda i: (0, 0)),
                  vec(SSD_CONV_DIM), vec(LANES), vec(LANES), vec(LANES), vec(D_INNER)],
        out_specs=[pl.BlockSpec((SSD_ROWS, D_INNER), lambda i: (i, 0)),
                   pl.BlockSpec((cpb, SSD_G, SSD_S, SSD_GW), lambda i: (i, 0, 0, 0))],
        out_shape=[jax.ShapeDtypeStruct((rows_total, D_INNER), F32),
                   jax.ShapeDtypeStruct((rows_total // CHUNK, SSD_G, SSD_S, SSD_GW), F32)],
        scratch_shapes=[pltpu.VMEM((SSD_ROWS + SUBLANES, SSD_CONV_DIM), F32),
                        pltpu.VMEM((SSD_ROWS, SSD_CONV_DIM), F32),
                        pltpu.VMEM((SSD_ROWS, LANES), F32),
                        pltpu.VMEM((SSD_G, SSD_S, SSD_GW), F32),
                        pltpu.VMEM((LANES, CHUNK), F32)],
        compiler_params=_cparams(("arbitrary",)),
    )(proj, proj, _pad_rows(conv_w), conv_b.reshape(1, -1), _pad_lanes(a_log), _pad_lanes(dt_bias),
      _pad_lanes(d_skip), norm_w.reshape(1, -1))


def ssd_bwd(proj, conv_w, conv_b, a_log, dt_bias, d_skip, norm_w, states, dmix, *, name):
    rows_total = proj.shape[0]
    nb = rows_total // SSD_ROWS
    hb = SSD_ROWS // SUBLANES
    cpb = SSD_ROWS // CHUNK

    def body(p_ref, halo_ref, cw_ref, cb_ref, alog_ref, dtb_ref, dsk_ref, nw_ref, st_ref, dm_ref,
             dp_ref, dcw_ref, dcb_ref, dalog_ref, ddtb_ref, ddsk_ref, dnw_ref,
             ext_ref, xbc_ref, dt_ref, ds_ref, act_ref, dext_ref, dac_ref, dact_ref, ddskw_ref):
        i = pl.program_id(0)
        blk = nb - 1 - i

        @pl.when(i == 0)
        def _():
            ds_ref[...] = jnp.zeros_like(ds_ref)
            dext_ref[SSD_ROWS:, :] = jnp.zeros((SUBLANES, SSD_CONV_DIM), F32)
            ddskw_ref[...] = jnp.zeros_like(ddskw_ref)
            for r in (dcw_ref, dcb_ref, dalog_ref, ddtb_ref, ddsk_ref, dnw_ref):
                r[...] = jnp.zeros_like(r)

        _ssd_prologue(blk, p_ref, halo_ref, cw_ref, cb_ref, dtb_ref, ext_ref, xbc_ref, dt_ref)
        a_row = -jnp.exp(alog_ref[...])
        expand = _head_expand(D_INNER, SSD_P)
        dsk_exp = _expand_row(dsk_ref[...], expand)
        ltri, utri, eye_l, eye_c = _tri(CHUNK), _tri(CHUNK, lower=False), _eye(LANES), _eye(CHUNK)
        causal = _iota((CHUNK, CHUNK), 1) <= _iota((CHUNK, CHUNK), 0)
        dp_ref[:, SSD_DT0 + LANES:] = jnp.zeros((SSD_ROWS, SSD_IN_PAD - SSD_DT0 - LANES), F32)

        def chunk(cc, carry):
            c = cpb - 1 - cc
            rows = pl.ds(pl.multiple_of(c * CHUNK, CHUNK), CHUNK)
            dt_c = dt_ref[rows, :]
            ac, tail, cdec, tot = _ssd_chunk_decays(dt_c, a_row, ltri, eye_l, act_ref)
            dt_exp = _xsel(dt_c, expand)
            tail_exp = _xsel(tail, expand)
            cdec_exp = _xsel(cdec, expand)
            tot_exp = _expand_row(tot, expand)
            dac_ref[...] = jnp.zeros_like(dac_ref)
            dact_ref[...] = jnp.zeros_like(dact_ref)
            d_cdec = jnp.zeros((CHUNK, LANES), F32)
            d_tail = jnp.zeros((CHUNK, LANES), F32)
            d_dt = jnp.zeros((CHUNK, LANES), F32)
            d_tot = jnp.zeros((1, LANES), F32)
            for g in range(SSD_G):
                gl = slice(g * SSD_GW, (g + 1) * SSD_GW)
                ex_g = expand[:, gl]
                s_g = st_ref[c, g]
                y, _, bg, cg, cb, xdt, cs, segs, gms = _ssd_group_fwd(g, xbc_ref, rows, dt_exp, tail_exp, cdec_exp, ac, act_ref, s_g, causal)
                xs = xbc_ref[rows, gl]
                z = p_ref[rows, gl]
                sz = _silu(z)
                y2 = y + dsk_exp[:, gl] * xs
                yg = y2 * sz
                r = lax.rsqrt(jnp.mean(yg * yg, axis=-1, keepdims=True) + RMS_EPS)
                yn = yg * r
                dm = dm_ref[rows, gl]
                dnw_ref[:, gl] += jnp.sum(dm * yn, axis=0, keepdims=True)
                dyn = dm * nw_ref[:, gl]
                dyg = r * (dyn - yn * jnp.mean(dyn * yn, axis=-1, keepdims=True))
                dp_ref[rows, gl] = dyg * y2 * _dsilu(z)
                dy = dyg * sz
                ddskw_ref[:, gl] += jnp.sum(dy * xs, axis=0, keepdims=True)
                ds_g = ds_ref[g]
                dyc = dy * cdec_exp[:, gl]
                ds_ref[g] = ds_g * tot_exp[:, gl] + _dot_tn(cg, dyc)
                sds = jnp.broadcast_to(jnp.sum(s_g * ds_g, axis=0, keepdims=True), (SUBLANES, SSD_GW))
                d_tot = d_tot + jnp.max(_xsel_nt(sds, ex_g), axis=0, keepdims=True)
                dcg = _dot_nt(dyc, s_g)
                d_cdec = d_cdec + _xsel_nt(dy * cs, ex_g)
                xdtd = xdt * tail_exp[:, gl]
                d_xdtd = _dot(bg, ds_g)
                dbg = _dot_nt(xdtd, ds_g)
                d_tail = d_tail + _xsel_nt(d_xdtd * xdt, ex_g)
                heads = range(SSD_HG)
                dy_h = [_ssd_half(dy[:, (e // 2) * LANES:(e // 2 + 1) * LANES], e) for e in heads]
                back = [_dot_tn(gms[e], dy_h[e]) for e in heads]
                dg_m = [jnp.where(causal, _dot_nt(dy_h[e], xdt[:, (e // 2) * LANES:(e // 2 + 1) * LANES]), 0.0) for e in heads]
                d_cb = None
                for e in heads:
                    h = g * SSD_HG + e
                    term = dg_m[e] * segs[e]
                    d_cb = term if d_cb is None else d_cb + term
                    em = dg_m[e] * gms[e]
                    dac_ref[...] += jnp.where(_iota((CHUNK, LANES), 1) == h, jnp.sum(em, axis=1, keepdims=True), 0.0)
                    dact_ref[h:h + 1, :] = jnp.sum(em, axis=0, keepdims=True)
                dcg = dcg + _dot(d_cb, bg)
                dbg = dbg + _dot_tn(d_cb, cg)
                d_xdt = d_xdtd * tail_exp[:, gl] + jnp.concatenate(
                    [back[2 * p] + back[2 * p + 1] for p in range(SSD_HG // 2)], axis=1)
                d_dt = d_dt + _xsel_nt(d_xdt * xs, ex_g)
                dext_ref[rows, gl] = d_xdt * dt_exp[:, gl] + dy * dsk_exp[:, gl]
                dext_ref[rows, SSD_B0 + g * SSD_S:SSD_B0 + (g + 1) * SSD_S] = dbg
                dext_ref[rows, SSD_C0 + g * SSD_S:SSD_C0 + (g + 1) * SSD_S] = dcg
            d_ac = dac_ref[...] - _sel_nt(eye_c,dact_ref[...]) + d_cdec * cdec - d_tail * tail
            d_last = jnp.sum(d_tail * tail, axis=0, keepdims=True) + d_tot * tot
            d_ac = jnp.where(_iota((CHUNK, LANES), 0) == CHUNK - 1, d_ac + d_last, d_ac)
            d_da = _sel(utri,d_ac)
            d_dt = d_dt + d_da * a_row
            dalog_ref[...] += jnp.sum(d_da * dt_c, axis=0, keepdims=True) * a_row
            d_raw = d_dt * _sigmoid(p_ref[rows, SSD_DT0:SSD_DT0 + LANES] + dtb_ref[...])
            d_raw = jnp.where(_iota((CHUNK, LANES), 1) < SSD_H, d_raw, 0.0)
            ddtb_ref[...] += jnp.sum(d_raw, axis=0, keepdims=True)
            dp_ref[rows, SSD_DT0:SSD_DT0 + LANES] = d_raw
            return carry

        lax.fori_loop(0, cpb, chunk, 0)
        pre = _conv_from_ext(ext_ref, cw_ref[...], SSD_CONV, SSD_ROWS) + cb_ref[...]
        d_pre = dext_ref[0:SSD_ROWS, :] * _dsilu(pre)
        dext_ref[0:SSD_ROWS, :] = d_pre
        dcb_ref[...] += jnp.sum(d_pre, axis=0, keepdims=True)
        dp_ref[:, SSD_X0:SSD_DT0] = _conv_dgrad_from_ext(dext_ref, cw_ref[...], SSD_CONV, SSD_ROWS)
        dws = _conv_wgrad_from_ext(ext_ref, d_pre, SSD_CONV, SSD_ROWS)
        for j in range(SSD_CONV):
            dcw_ref[j:j + 1, :] += dws[j]
        dext_ref[SSD_ROWS:, :] = d_pre[0:SUBLANES, :]

        @pl.when(i == nb - 1)
        def _():
            ddsk_ref[...] = jnp.max(_xsel_nt(jnp.broadcast_to(ddskw_ref[...], (SUBLANES, D_INNER)), expand), axis=0, keepdims=True)

    vec = lambda n: pl.BlockSpec((1, n), lambda i: (0, 0))
    outs = pl.pallas_call(
        body, name=name, grid=(nb,),
        in_specs=[pl.BlockSpec((SSD_ROWS, SSD_IN_PAD), lambda i: (nb - 1 - i, 0)),
                  pl.BlockSpec((SUBLANES, SSD_IN_PAD), lambda i: (jnp.maximum((nb - 1 - i) * hb - 1, 0), 0)),
                  pl.BlockSpec((SUBLANES, SSD_CONV_DIM), lambda i: (0, 0)),
                  vec(SSD_CONV_DIM), vec(LANES), vec(LANES), vec(LANES), vec(D_INNER),
                  pl.BlockSpec((cpb, SSD_G, SSD_S, SSD_GW), lambda i: (nb - 1 - i, 0, 0, 0)),
                  pl.BlockSpec((SSD_ROWS, D_INNER), lambda i: (nb - 1 - i, 0))],
        out_specs=[pl.BlockSpec((SSD_ROWS, SSD_IN_PAD), lambda i: (nb - 1 - i, 0)),
                   pl.BlockSpec((SUBLANES, SSD_CONV_DIM), lambda i: (0, 0)),
                   vec(SSD_CONV_DIM), vec(LANES), vec(LANES), vec(LANES), vec(D_INNER)],
        out_shape=[jax.ShapeDtypeStruct((rows_total, SSD_IN_PAD), F32),
                   jax.ShapeDtypeStruct((SUBLANES, SSD_CONV_DIM), F32),
                   jax.ShapeDtypeStruct((1, SSD_CONV_DIM), F32), jax.ShapeDtypeStruct((1, LANES), F32),
                   jax.ShapeDtypeStruct((1, LANES), F32), jax.ShapeDtypeStruct((1, LANES), F32),
                   jax.ShapeDtypeStruct((1, D_INNER), F32)],
        scratch_shapes=[pltpu.VMEM((SSD_ROWS + SUBLANES, SSD_CONV_DIM), F32),
                        pltpu.VMEM((SSD_ROWS, SSD_CONV_DIM), F32),
                        pltpu.VMEM((SSD_ROWS, LANES), F32),
                        pltpu.VMEM((SSD_G, SSD_S, SSD_GW), F32),
                        pltpu.VMEM((LANES, CHUNK), F32),
                        pltpu.VMEM((SSD_ROWS + SUBLANES, SSD_CONV_DIM), F32),
                        pltpu.VMEM((CHUNK, LANES), F32),
                        pltpu.VMEM((LANES, CHUNK), F32),
                        pltpu.VMEM((1, D_INNER), F32)],
        compiler_params=_cparams(("arbitrary",)),
    )(proj, proj, _pad_rows(conv_w), conv_b.reshape(1, -1), _pad_lanes(a_log), _pad_lanes(dt_bias),
      _pad_lanes(d_skip), norm_w.reshape(1, -1), states, dmix)
    dproj, dcw, dcb, dalog, ddtb, ddsk, dnw = outs
    return dproj, [dcw[:SSD_CONV], dcb[0], dalog[0, :SSD_H], ddtb[0, :SSD_H], ddsk[0, :SSD_H], dnw[0]]


GDN_ROWS = 128
GDN_K0 = GDN_QK
GDN_V0 = 2 * GDN_QK
GDN_Z0 = GDN_CONV_DIM
GDN_BA0 = GDN_CONV_DIM + GDN_V
GDN_GL = GDN_VH
GDN_SCALE = GDN_HEAD ** -0.5
GDN_GROUP = 8


def _gdn_lane_params(v):
    return jnp.pad(v.reshape(1, GDN_VH), ((0, 0), (GDN_GL, LANES - GDN_GL - GDN_VH)))


def _inv_unit_lower(a, eye_c):
    x = eye_c - a
    ph, pl_ = _split(a, 2)
    n = 2
    while n < CHUNK:
        p = (_mxu(pl_, ph, NN) + _mxu(ph, pl_, NN)) + _mxu(ph, ph, NN)
        ph, pl_ = _split(p, 2)
        xh, xl = _split(x, 2)
        x = x + ((_mxu(xl, ph, NN) + _mxu(xh, pl_, NN)) + _mxu(xh, ph, NN))
        n *= 2
    return x


def _gdn_prologue(blk, p_ref, halo_ref, cw_ref, alog_ref, dtb_ref, ext_ref, qkv_ref, beta_ref, g_ref):
    ext_ref[0:SUBLANES, :] = jnp.where(blk == 0, 0.0, halo_ref[:, 0:GDN_CONV_DIM])
    ext_ref[SUBLANES:, :] = p_ref[:, 0:GDN_CONV_DIM]
    w = cw_ref[...]
    for hq in range(2 * GDN_QKH):
        cols = slice(hq * GDN_HEAD, (hq + 1) * GDN_HEAD)
        pre = None
        for j in range(GDN_CONV):
            off = SUBLANES - (GDN_CONV - 1) + j
            term = ext_ref[off:off + GDN_ROWS, cols] * w[j:j + 1, cols]
            pre = term if pre is None else pre + term
        a = _silu(pre)
        r = lax.rsqrt(jnp.sum(a * a, axis=-1, keepdims=True) + L2_EPS)
        qkv_ref[:, cols] = a * (r * (GDN_SCALE if hq < GDN_QKH else 1.0))
    vcols = slice(GDN_V0, GDN_CONV_DIM)
    pre = None
    for j in range(GDN_CONV):
        off = SUBLANES - (GDN_CONV - 1) + j
        term = ext_ref[off:off + GDN_ROWS, vcols] * w[j:j + 1, vcols]
        pre = term if pre is None else pre + term
    qkv_ref[:, vcols] = _silu(pre)
    ba = p_ref[:, GDN_BA0:GDN_BA0 + LANES]
    beta_ref[...] = _sigmoid(ba)
    g_ref[...] = -jnp.exp(alog_ref[...]) * _softplus(ba + dtb_ref[...])


def _each(f, *lists):
    return [f(*z) for z in zip(*lists)]


def _inv_unit_lower_each(a_list, eye_c):
    xs = [eye_c - a for a in a_list]
    ps = [_split(a, 2) for a in a_list]
    n = 2
    while n < CHUNK:
        ps = [_split((_mxu(pl_, ph, NN) + _mxu(ph, pl_, NN)) + _mxu(ph, ph, NN), 2) for ph, pl_ in ps]
        xs_split = [_split(x, 2) for x in xs]
        xs = [x + ((_mxu(xl, ph, NN) + _mxu(xh, pl_, NN)) + _mxu(xh, ph, NN))
              for x, (xh, xl), (ph, pl_) in zip(xs, xs_split, ps)]
        n *= 2
    return xs


def _gdn_heads_fwd(q, k, v, kk, qk, gcol, grow, glast, bcol, s, causal, strict, eye_c, t=None):
    decay = _each(lambda gc_, gr_: jnp.where(causal, jnp.exp(jnp.minimum(gc_ - gr_, 0.0)), 0.0), gcol, grow)
    egc = _each(jnp.exp, gcol)
    etail = _each(lambda gl_, gc_: jnp.exp(gl_ - gc_), glast, gcol)
    cd = _each(jnp.exp, glast)
    a = _each(lambda b_, kk_, d_: jnp.where(strict, b_ * kk_ * d_, 0.0), bcol, kk, decay)
    if t is None:
        t = _inv_unit_lower_each(a, eye_c)
    kb = _each(lambda k_, b_: k_ * b_, k, bcol)
    rhs_w = _each(lambda kb_, e_: kb_ * e_, kb, egc)
    u = _each(lambda t_, v_, b_: _dot_x3(t_, v_ * b_), t, v, bcol)
    w = _each(_dot_x3, t, rhs_w)
    attn = _each(lambda qk_, d_: qk_ * d_, qk, decay)
    ws = _each(_dot, w, s)
    v_new = _each(lambda u_, ws_: u_ - ws_, u, ws)
    qd = _each(lambda q_, e_: q_ * e_, q, egc)
    kt = _each(lambda k_, e_: k_ * e_, k, etail)
    o1 = _each(_dot, qd, s)
    o2 = _each(_dot, attn, v_new)
    out = _each(lambda a_, b_: a_ + b_, o1, o2)
    upd = _each(_dot_tn, kt, v_new)
    s_new = _each(lambda s_, c_, u_: s_ * c_ + u_, s, cd, upd)
    return dict(decay=decay, egc=egc, etail=etail, cd=cd, a=a, t=t, kb=kb, rhs_w=rhs_w, u=u, w=w, attn=attn,
                v_new=v_new, qd=qd, kt=kt, out=out, s_new=s_new)


def gdn_fwd(proj, conv_w, a_log, dt_bias, norm_w, *, name):
    rows_total = proj.shape[0]
    nb = rows_total // GDN_ROWS
    hb = GDN_ROWS // SUBLANES
    cpb = GDN_ROWS // CHUNK

    def body(p_ref, halo_ref, cw_ref, alog_ref, dtb_ref, nw_ref, mix_ref, st_ref, tm_ref,
             ext_ref, qkv_ref, beta_ref, g_ref, s_ref, gct_ref):
        i = pl.program_id(0)

        @pl.when(i == 0)
        def _():
            s_ref[...] = jnp.zeros_like(s_ref)

        _gdn_prologue(i, p_ref, halo_ref, cw_ref, alog_ref, dtb_ref, ext_ref, qkv_ref, beta_ref, g_ref)
        ltri, eye_l, eye_c = _tri(CHUNK), _eye(LANES), _eye(CHUNK)
        causal = _iota((CHUNK, CHUNK), 1) <= _iota((CHUNK, CHUNK), 0)
        strict = _iota((CHUNK, CHUNK), 1) < _iota((CHUNK, CHUNK), 0)
        nw = nw_ref[...]

        def chunk(c, carry):
            rows = pl.ds(pl.multiple_of(c * CHUNK, CHUNK), CHUNK)
            gc = _sel(ltri,g_ref[rows, :])
            gct_ref[...] = _sel_nt(eye_l,gc)
            glast_row = _row(gc, CHUNK - 1)
            beta_c = beta_ref[rows, :]
            for h0 in range(0, GDN_VH, GDN_GROUP):
                hs = list(range(h0, h0 + GDN_GROUP))
                qs = {hq: qkv_ref[rows, hq * GDN_HEAD:(hq + 1) * GDN_HEAD] for hq in range(h0 // 2, (h0 + GDN_GROUP) // 2)}
                ks = {hq: qkv_ref[rows, GDN_K0 + hq * GDN_HEAD:GDN_K0 + (hq + 1) * GDN_HEAD] for hq in qs}
                kks = {hq: _dot_nt(ks[hq], ks[hq]) for hq in qs}
                qks = {hq: _dot_nt(qs[hq], ks[hq]) for hq in qs}
                ss = [s_ref[h] for h in hs]
                for h, s in zip(hs, ss):
                    st_ref[c, h] = s
                f = _gdn_heads_fwd(
                    [qs[h // 2] for h in hs], [ks[h // 2] for h in hs],
                    [qkv_ref[rows, GDN_V0 + h * GDN_HEAD:GDN_V0 + (h + 1) * GDN_HEAD] for h in hs],
                    [kks[h // 2] for h in hs], [qks[h // 2] for h in hs],
                    [_col(gc, GDN_GL + h) for h in hs], [gct_ref[GDN_GL + h:GDN_GL + h + 1, :] for h in hs],
                    [_col(glast_row, GDN_GL + h) for h in hs], [_col(beta_c, h) for h in hs], ss, causal, strict, eye_c)
                for i_h, h in enumerate(hs):
                    hc = slice(h * GDN_HEAD, (h + 1) * GDN_HEAD)
                    s_ref[h] = f["s_new"][i_h]
                    tm_ref[c, h] = f["t"][i_h]
                    o = f["out"][i_h]
                    r = lax.rsqrt(jnp.mean(o * o, axis=-1, keepdims=True) + RMS_EPS)
                    mix_ref[rows, hc] = o * r * nw * _silu(p_ref[rows, GDN_Z0 + h * GDN_HEAD:GDN_Z0 + (h + 1) * GDN_HEAD])
            return carry

        lax.fori_loop(0, cpb, chunk, 0)

    vec = lambda n: pl.BlockSpec((1, n), lambda i: (0, 0))
    mix, states, tmats = pl.pallas_call(
        body, name=name, grid=(nb,),
        in_specs=[pl.BlockSpec((GDN_ROWS, GDN_IN_PAD), lambda i: (i, 0)),
                  pl.BlockSpec((SUBLANES, GDN_IN_PAD), lambda i: (jnp.maximum(i * hb - 1, 0), 0)),
                  pl.BlockSpec((SUBLANES, GDN_CONV_DIM), lambda i: (0, 0)),
                  vec(LANES), vec(LANES), vec(GDN_HEAD)],
        out_specs=[pl.BlockSpec((GDN_ROWS, GDN_V), lambda i: (i, 0)),
                   pl.BlockSpec((cpb, GDN_VH, GDN_HEAD, GDN_HEAD), lambda i: (i, 0, 0, 0)),
                   pl.BlockSpec((cpb, GDN_VH, CHUNK, CHUNK), lambda i: (i, 0, 0, 0))],
        out_shape=[jax.ShapeDtypeStruct((rows_total, GDN_V), F32),
                   jax.ShapeDtypeStruct((rows_total // CHUNK, GDN_VH, GDN_HEAD, GDN_HEAD), F32),
                   jax.ShapeDtypeStruct((rows_total // CHUNK, GDN_VH, CHUNK, CHUNK), F32)],
        scratch_shapes=[pltpu.VMEM((GDN_ROWS + SUBLANES, GDN_CONV_DIM), F32),
                        pltpu.VMEM((GDN_ROWS, GDN_CONV_DIM), F32),
                        pltpu.VMEM((GDN_ROWS, LANES), F32),
                        pltpu.VMEM((GDN_ROWS, LANES), F32),
                        pltpu.VMEM((GDN_VH, GDN_HEAD, GDN_HEAD), F32),
                        pltpu.VMEM((LANES, CHUNK), F32)],
        compiler_params=_cparams(("arbitrary",)),
    )(proj, proj, _pad_rows(conv_w), _gdn_lane_params(a_log), _gdn_lane_params(dt_bias), norm_w.reshape(1, -1))
    return mix, (states, tmats)


def gdn_bwd(proj, conv_w, a_log, dt_bias, norm_w, saved, dmix, *, name):
    states, tmats = saved
    rows_total = proj.shape[0]
    nb = rows_total // GDN_ROWS
    hb = GDN_ROWS // SUBLANES
    cpb = GDN_ROWS // CHUNK

    def body(p_ref, halo_ref, cw_ref, alog_ref, dtb_ref, nw_ref, st_ref, tm_ref, dm_ref,
             dp_ref, dcw_ref, dalog_ref, ddtb_ref, dnw_ref,
             ext_ref, qkv_ref, beta_ref, g_ref, ds_ref, gct_ref, dext_ref, dgc_ref, dgct_ref, dbeta_ref):
        i = pl.program_id(0)
        blk = nb - 1 - i

        @pl.when(i == 0)
        def _():
            ds_ref[...] = jnp.zeros_like(ds_ref)
            dext_ref[GDN_ROWS:, :] = jnp.zeros((SUBLANES, GDN_CONV_DIM), F32)
            for r in (dcw_ref, dalog_ref, ddtb_ref, dnw_ref):
                r[...] = jnp.zeros_like(r)

        _gdn_prologue(blk, p_ref, halo_ref, cw_ref, alog_ref, dtb_ref, ext_ref, qkv_ref, beta_ref, g_ref)
        ltri, utri, eye_l, eye_c = _tri(CHUNK), _tri(CHUNK, lower=False), _eye(LANES), _eye(CHUNK)
        causal = _iota((CHUNK, CHUNK), 1) <= _iota((CHUNK, CHUNK), 0)
        strict = _iota((CHUNK, CHUNK), 1) < _iota((CHUNK, CHUNK), 0)
        lane = _iota((CHUNK, LANES), 1)
        is_last = _iota((CHUNK, 1), 0) == CHUNK - 1
        nw = nw_ref[...]

        def chunk(cc, carry):
            c = cpb - 1 - cc
            rows = pl.ds(pl.multiple_of(c * CHUNK, CHUNK), CHUNK)
            g_c = g_ref[rows, :]
            gc = _sel(ltri,g_c)
            gct_ref[...] = _sel_nt(eye_l,gc)
            glast_row = _row(gc, CHUNK - 1)
            beta_c = beta_ref[rows, :]
            dgc_ref[...] = jnp.zeros_like(dgc_ref)
            dgct_ref[...] = jnp.zeros_like(dgct_ref)
            dbeta_ref[...] = jnp.zeros_like(dbeta_ref)
            for h0 in range(0, GDN_VH, GDN_GROUP):
                hs = list(range(h0, h0 + GDN_GROUP))
                hqs = list(range(h0 // 2, (h0 + GDN_GROUP) // 2))
                qs = {hq: qkv_ref[rows, hq * GDN_HEAD:(hq + 1) * GDN_HEAD] for hq in hqs}
                ks = {hq: qkv_ref[rows, GDN_K0 + hq * GDN_HEAD:GDN_K0 + (hq + 1) * GDN_HEAD] for hq in hqs}
                kks = {hq: _dot_nt(ks[hq], ks[hq]) for hq in hqs}
                qks = {hq: _dot_nt(qs[hq], ks[hq]) for hq in hqs}
                q = [qs[h // 2] for h in hs]
                k = [ks[h // 2] for h in hs]
                v = [qkv_ref[rows, GDN_V0 + h * GDN_HEAD:GDN_V0 + (h + 1) * GDN_HEAD] for h in hs]
                s = [st_ref[c, h] for h in hs]
                bcol = [_col(beta_c, h) for h in hs]
                f = _gdn_heads_fwd(q, k, v, [kks[h // 2] for h in hs], [qks[h // 2] for h in hs],
                                   [_col(gc, GDN_GL + h) for h in hs], [gct_ref[GDN_GL + h:GDN_GL + h + 1, :] for h in hs],
                                   [_col(glast_row, GDN_GL + h) for h in hs], bcol, s, causal, strict, eye_c,
                                   t=[tm_ref[c, h] for h in hs])
                do = []
                for i_h, h in enumerate(hs):
                    zc = slice(GDN_Z0 + h * GDN_HEAD, GDN_Z0 + (h + 1) * GDN_HEAD)
                    o = f["out"][i_h]
                    z = p_ref[rows, zc]
                    sz = _silu(z)
                    r = lax.rsqrt(jnp.mean(o * o, axis=-1, keepdims=True) + RMS_EPS)
                    on = o * r
                    dm = dm_ref[rows, h * GDN_HEAD:(h + 1) * GDN_HEAD]
                    dnw_ref[...] += jnp.sum(dm * on * sz, axis=0, keepdims=True)
                    d_on = dm * nw * sz
                    dp_ref[rows, zc] = dm * on * nw * _dsilu(z)
                    do.append(r * (d_on - on * jnp.mean(d_on * on, axis=-1, keepdims=True)))
                ds_n = [ds_ref[h] for h in hs]
                dv1 = _each(_dot_tn, f["attn"], do)
                dv2 = _each(_dot, f["kt"], ds_n)
                d_vnew = _each(lambda a_, b_: a_ + b_, dv1, dv2)
                d_attn = _each(lambda do_, vn_: jnp.where(causal, _dot_nt(do_, vn_), 0.0), do, f["v_new"])
                d_qd = _each(_dot_nt, do, s)
                t1 = _each(_dot_tn, f["qd"], do)
                t2 = _each(_dot_tn, f["w"], d_vnew)
                for h, a_, cd_, dsn_, b_ in zip(hs, t1, f["cd"], ds_n, t2):
                    ds_ref[h] = a_ + cd_ * dsn_ - b_
                d_cd = _each(lambda s_, dsn_: jnp.sum(jnp.sum(s_ * dsn_, axis=1, keepdims=True), axis=0, keepdims=True), s, ds_n)
                d_kt = _each(_dot_nt, f["v_new"], ds_n)
                d_w = _each(lambda dv_, s_: -_dot_nt(dv_, s_), d_vnew, s)
                d_rhs_u = _each(lambda t_, d_: _dot_x3(t_, d_, TN), f["t"], d_vnew)
                d_rhs_w = _each(lambda t_, d_: _dot_x3(t_, d_, TN), f["t"], d_w)
                m1 = _each(_dot_nt, d_rhs_u, f["u"])
                m2 = _each(_dot_nt, d_rhs_w, f["w"])
                da = _each(lambda a_, b_: -jnp.where(strict, a_ + b_, 0.0), m1, m2)
                dmm = _each(lambda a_, b_: a_ * b_, da, f["decay"])
                em = _each(lambda da_, a_, dat_, at_: da_ * a_ + dat_ * at_, da, f["a"], d_attn, f["attn"])
                x1 = _each(_dot, dmm, k)
                d_kb = _each(lambda x_, drw_, e_: x_ + drw_ * e_, x1, d_rhs_w, f["egc"])
                dk1 = _each(_dot_tn, dmm, f["kb"])
                dpm = _each(lambda a_, b_: a_ * b_, d_attn, f["decay"])
                dq1 = _each(_dot, dpm, k)
                dq = _each(lambda x_, dqd_, e_: x_ + dqd_ * e_, dq1, d_qd, f["egc"])
                dk2 = _each(_dot_tn, dpm, q)
                dk = _each(lambda a_, b_, dkb_, bc_, dkt_, et_: a_ + b_ + dkb_ * bc_ + dkt_ * et_,
                           dk1, dk2, d_kb, bcol, d_kt, f["etail"])
                for i_h, h in enumerate(hs):
                    tmp = jnp.sum(d_kt[i_h] * f["kt"][i_h], axis=1, keepdims=True)
                    d_gcol = (jnp.sum(em[i_h], axis=1, keepdims=True)
                              + jnp.sum(d_rhs_w[i_h] * f["rhs_w"][i_h], axis=1, keepdims=True)
                              + jnp.sum(d_qd[i_h] * f["qd"][i_h], axis=1, keepdims=True) - tmp)
                    d_glast = jnp.sum(tmp, axis=0, keepdims=True) + d_cd[i_h] * f["cd"][i_h]
                    d_gcol = jnp.where(is_last, d_gcol + d_glast, d_gcol)
                    d_beta = (jnp.sum(d_rhs_u[i_h] * v[i_h], axis=1, keepdims=True)
                              + jnp.sum(d_kb[i_h] * k[i_h], axis=1, keepdims=True))
                    dgc_ref[...] += jnp.where(lane == GDN_GL + h, d_gcol, 0.0)
                    dgct_ref[GDN_GL + h:GDN_GL + h + 1, :] = jnp.sum(em[i_h], axis=0, keepdims=True)
                    dbeta_ref[...] += jnp.where(lane == h, d_beta, 0.0)
                    dext_ref[rows, GDN_V0 + h * GDN_HEAD:GDN_V0 + (h + 1) * GDN_HEAD] = d_rhs_u[i_h] * bcol[i_h]
                for hq in hqs:
                    i0 = 2 * hq - h0
                    dext_ref[rows, hq * GDN_HEAD:(hq + 1) * GDN_HEAD] = dq[i0] + dq[i0 + 1]
                    dext_ref[rows, GDN_K0 + hq * GDN_HEAD:GDN_K0 + (hq + 1) * GDN_HEAD] = dk[i0] + dk[i0 + 1]
            d_gc = dgc_ref[...] - _sel_nt(eye_c,dgct_ref[...])
            dg = _sel(utri,d_gc)
            ba = p_ref[rows, GDN_BA0:GDN_BA0 + LANES]
            d_sp = dg * -jnp.exp(alog_ref[...])
            d_araw = d_sp * _sigmoid(ba + dtb_ref[...])
            d_araw = jnp.where((lane >= GDN_GL) & (lane < GDN_GL + GDN_VH), d_araw, 0.0)
            dalog_ref[...] += jnp.sum(dg * g_c, axis=0, keepdims=True)
            ddtb_ref[...] += jnp.sum(d_araw, axis=0, keepdims=True)
            d_braw = jnp.where(lane < GDN_VH, dbeta_ref[...] * beta_c * (1.0 - beta_c), 0.0)
            dp_ref[rows, GDN_BA0:GDN_BA0 + LANES] = d_braw + d_araw
            return carry

        lax.fori_loop(0, cpb, chunk, 0)
        w = cw_ref[...]
        for hh in range(GDN_CONV_DIM // GDN_HEAD):
            cols = slice(hh * GDN_HEAD, (hh + 1) * GDN_HEAD)
            pre = None
            for j in range(GDN_CONV):
                off = SUBLANES - (GDN_CONV - 1) + j
                term = ext_ref[off:off + GDN_ROWS, cols] * w[j:j + 1, cols]
                pre = term if pre is None else pre + term
            d_act = dext_ref[0:GDN_ROWS, cols]
            if hh < 2 * GDN_QKH:
                a = _silu(pre)
                r = lax.rsqrt(jnp.sum(a * a, axis=-1, keepdims=True) + L2_EPS)
                ah = a * r
                if hh < GDN_QKH:
                    d_act = d_act * GDN_SCALE
                d_act = r * (d_act - ah * jnp.sum(d_act * ah, axis=-1, keepdims=True))
            d_pre = d_act * _dsilu(pre)
            dext_ref[0:GDN_ROWS, cols] = d_pre
            for j in range(GDN_CONV):
                off = SUBLANES - (GDN_CONV - 1) + j
                dcw_ref[j:j + 1, cols] += jnp.sum(ext_ref[off:off + GDN_ROWS, cols] * d_pre, axis=0, keepdims=True)
        dp_ref[:, 0:GDN_CONV_DIM] = _conv_dgrad_from_ext(dext_ref, w, GDN_CONV, GDN_ROWS)
        dext_ref[GDN_ROWS:, :] = dext_ref[0:SUBLANES, :]

    vec = lambda n: pl.BlockSpec((1, n), lambda i: (0, 0))
    outs = pl.pallas_call(
        body, name=name, grid=(nb,),
        in_specs=[pl.BlockSpec((GDN_ROWS, GDN_IN_PAD), lambda i: (nb - 1 - i, 0)),
                  pl.BlockSpec((SUBLANES, GDN_IN_PAD), lambda i: (jnp.maximum((nb - 1 - i) * hb - 1, 0), 0)),
                  pl.BlockSpec((SUBLANES, GDN_CONV_DIM), lambda i: (0, 0)),
                  vec(LANES), vec(LANES), vec(GDN_HEAD),
                  pl.BlockSpec((cpb, GDN_VH, GDN_HEAD, GDN_HEAD), lambda i: (nb - 1 - i, 0, 0, 0)),
                  pl.BlockSpec((cpb, GDN_VH, CHUNK, CHUNK), lambda i: (nb - 1 - i, 0, 0, 0)),
                  pl.BlockSpec((GDN_ROWS, GDN_V), lambda i: (nb - 1 - i, 0))],
        out_specs=[pl.BlockSpec((GDN_ROWS, GDN_IN_PAD), lambda i: (nb - 1 - i, 0)),
                   pl.BlockSpec((SUBLANES, GDN_CONV_DIM), lambda i: (0, 0)),
                   vec(LANES), vec(LANES), vec(GDN_HEAD)],
        out_shape=[jax.ShapeDtypeStruct((rows_total, GDN_IN_PAD), F32),
                   jax.ShapeDtypeStruct((SUBLANES, GDN_CONV_DIM), F32),
                   jax.ShapeDtypeStruct((1, LANES), F32), jax.ShapeDtypeStruct((1, LANES), F32),
                   jax.ShapeDtypeStruct((1, GDN_HEAD), F32)],
        scratch_shapes=[pltpu.VMEM((GDN_ROWS + SUBLANES, GDN_CONV_DIM), F32),
                        pltpu.VMEM((GDN_ROWS, GDN_CONV_DIM), F32),
                        pltpu.VMEM((GDN_ROWS, LANES), F32),
                        pltpu.VMEM((GDN_ROWS, LANES), F32),
                        pltpu.VMEM((GDN_VH, GDN_HEAD, GDN_HEAD), F32),
                        pltpu.VMEM((LANES, CHUNK), F32),
                        pltpu.VMEM((GDN_ROWS + SUBLANES, GDN_CONV_DIM), F32),
                        pltpu.VMEM((CHUNK, LANES), F32),
                        pltpu.VMEM((LANES, CHUNK), F32),
                        pltpu.VMEM((CHUNK, LANES), F32)],
        compiler_params=_cparams(("arbitrary",)),
    )(proj, proj, _pad_rows(conv_w), _gdn_lane_params(a_log), _gdn_lane_params(dt_bias), norm_w.reshape(1, -1),
      states, tmats, dmix)
    dproj, dcw, dalog, ddtb, dnw = outs
    return dproj, [dcw[:GDN_CONV], dalog[0, GDN_GL:GDN_GL + GDN_VH], ddtb[0, GDN_GL:GDN_GL + GDN_VH], dnw[0]]


def exchange(src, axes, *, gather, name, splits=1):
    n = 2 ** len(axes)
    piece_shape = src.shape if gather else src.shape[1:]
    rows = piece_shape[0] // splits
    assert rows * splits == piece_shape[0]

    def body(src_ref, out_ref, send_sems, recv_sems, local_sem):
        coords = {a: lax.axis_index(a) for a in MESH_AXES}
        me = 0
        for a in axes:
            me = me * 2 + coords[a]

        def piece(j):
            return src_ref if gather else src_ref.at[j]

        local = pltpu.make_async_copy(piece(me), out_ref.at[me], local_sem)
        local.start()
        copies = []
        for k in range(1, n):
            peer = dict(coords)
            for bit, a in enumerate(reversed(axes)):
                if (k >> bit) & 1:
                    peer[a] = 1 - peer[a]
            for part in range(splits):
                sem = (k - 1) * splits + part
                sl = pl.ds(part * rows, rows)
                cp = pltpu.make_async_remote_copy(
                    src_ref=piece(jnp.bitwise_xor(me, k)).at[sl], dst_ref=out_ref.at[me].at[sl],
                    send_sem=send_sems.at[sem], recv_sem=recv_sems.at[sem],
                    device_id=tuple(peer[a] for a in MESH_AXES), device_id_type=pl.DeviceIdType.MESH)
                cp.start()
                copies.append(cp)
        for cp in copies:
            cp.wait()
        local.wait()

    hbm = pl.BlockSpec(memory_space=pl.ANY)
    n_sems = (n - 1) * splits
    return pl.pallas_call(
        body, name=name, in_specs=[hbm], out_specs=hbm,
        out_shape=jax.ShapeDtypeStruct((n,) + tuple(piece_shape), src.dtype),
        scratch_shapes=[pltpu.SemaphoreType.DMA((n_sems,)), pltpu.SemaphoreType.DMA((n_sems,)), pltpu.SemaphoreType.DMA],
    )(src)


def sum_slots(buf, *, name):
    n, rows, cols = buf.shape
    tr = _pick(rows, (512, 256, 128))

    def body(b_ref, o_ref):
        acc = b_ref[0]
        for j in range(1, n):
            acc = acc + b_ref[j]
        o_ref[...] = acc

    return pl.pallas_call(
        body, name=name, grid=(rows // tr,), in_specs=[pl.BlockSpec((n, tr, cols), lambda i: (0, i, 0))],
        out_specs=pl.BlockSpec((tr, cols), lambda i: (i, 0)), out_shape=jax.ShapeDtypeStruct((rows, cols), F32),
        compiler_params=_cparams(("parallel",)),
    )(buf)


def adamw(w, g, m, v, *, name):
    rows, cols = w.shape
    tr = _pick(rows, (256, 128))

    def body(w_ref, g_ref, m_ref, v_ref, d_ref, mo_ref, vo_ref):
        gv = g_ref[...]
        mn = ADAM_B1 * m_ref[...] + (1.0 - ADAM_B1) * gv
        vn = ADAM_B2 * v_ref[...] + (1.0 - ADAM_B2) * (gv * gv)
        m_hat = mn / (1.0 - ADAM_B1 ** ADAM_STEP)
        v_hat = vn / (1.0 - ADAM_B2 ** ADAM_STEP)
        d_ref[...] = -ADAM_LR * (m_hat / (jnp.sqrt(v_hat) + ADAM_EPS) + ADAM_WD * w_ref[...])
        mo_ref[...] = mn
        vo_ref[...] = vn

    blk = pl.BlockSpec((tr, cols), lambda i: (i, 0))
    shp = jax.ShapeDtypeStruct((rows, cols), F32)
    return pl.pallas_call(
        body, name=name, grid=(rows // tr,), in_specs=[blk] * 4, out_specs=[blk] * 3, out_shape=[shp] * 3,
        compiler_params=_cparams(("parallel",)),
    )(w, g, m, v)


N_SHARDS = 4
FLAT_COLS = 1024
W_SPECS = (
    ("gdn_w_in", (2, 1024, 6176), 2), ("gdn_conv_w", (2, 4, 4096), 2), ("gdn_a_log", (2, 16), None),
    ("gdn_dt_bias", (2, 16), None), ("gdn_norm_w", (2, 128), None), ("gdn_w_out", (2, 2048, 1024), 1),
    ("sc_w_in", (1, 1024, 8192), 2), ("sc_conv_w", (1, 3, 2048), 2), ("sc_w_out", (1, 2048, 1024), 1),
    ("ssd_w_in", (1, 1024, 5152), 2), ("ssd_conv_w", (1, 4, 3072), 2), ("ssd_conv_b", (1, 3072), 1),
    ("ssd_a_log", (1, 32), None), ("ssd_dt_bias", (1, 32), None), ("ssd_d_skip", (1, 32), None),
    ("ssd_norm_w", (1, 2048), 1), ("ssd_w_out", (1, 2048, 1024), 1), ("ln_g", (4, 1024), None), ("ln_b", (4, 1024), None),
)


def _local_shape(shape, axis):
    return shape if axis is None else tuple(d // N_SHARDS if i == axis else d for i, d in enumerate(shape))


def _size(shape):
    n = 1
    for d in shape:
        n *= d
    return n


FLAT_USED = sum(_size(_local_shape(s, a)) for _, s, a in W_SPECS)
FLAT_ROWS = -(-FLAT_USED // (FLAT_COLS * 512)) * 512
FLAT_HALF = FLAT_ROWS // 2
D2D_SPLITS = 8


def _pack(pieces):
    flat = jnp.concatenate([p.reshape(-1) for p in pieces] + [jnp.zeros((FLAT_ROWS * FLAT_COLS - FLAT_USED,), F32)])
    return flat.reshape(FLAT_ROWS, FLAT_COLS)


def _unpack(flat):
    flat = flat.reshape(-1)
    out, off = [], 0
    for _, shape, axis in W_SPECS:
        ls = _local_shape(shape, axis)
        out.append(flat[off:off + _size(ls)].reshape(ls))
        off += _size(ls)
    return out


def _shard_of(full, axis, s):
    if axis is None:
        return full
    n = full.shape[axis] // N_SHARDS
    return lax.slice_in_dim(full, s * n, (s + 1) * n, axis=axis)


def _adamw_all(weights, grads_flat, moms, vels):
    delta, new_m, new_v = adamw(_pack(weights), grads_flat, _pack(moms), _pack(vels), name="adamw")
    return _unpack(grads_flat), _unpack(delta), _unpack(new_m), _unpack(new_v)


def _reduce_scatter(full_grads):
    by_shard = jnp.stack([_pack([_shard_of(g, a, s) for g, (_, _, a) in zip(full_grads, W_SPECS)])
                          for s in range(N_SHARDS)])
    by_half = by_shard.reshape(N_SHARDS, 2, FLAT_HALF, FLAT_COLS).transpose(1, 0, 2, 3)
    by_half = by_half.reshape(2, N_SHARDS * FLAT_HALF, FLAT_COLS)
    pair = exchange(by_half, ("c",), gather=False, name="rs_pair", splits=D2D_SPLITS)
    pair_sum = sum_slots(pair, name="rs_pair_sum")
    chips = exchange(pair_sum.reshape(N_SHARDS, FLAT_HALF, FLAT_COLS), ("x", "y"), gather=False, name="rs_chips")
    half = sum_slots(chips, name="rs_chip_sum")
    return exchange(half, ("c",), gather=True, name="rs_halves", splits=D2D_SPLITS).reshape(FLAT_ROWS, FLAT_COLS)


def _gather_weights(local_weights):
    gathered = exchange(_pack(local_weights), ("x", "y"), gather=True, name="gather_weights")
    per_shard = [_unpack(gathered[s]) for s in range(N_SHARDS)]
    full = []
    for i, (_, _, axis) in enumerate(W_SPECS):
        if axis is None:
            full.append(local_weights[i])
        else:
            full.append(jnp.concatenate([per_shard[s][i] for s in range(N_SHARDS)], axis=axis))
    return full


def _mxu_weight(w, pad_to=None, axis=1):
    if pad_to is not None:
        pads = [(0, 0), (0, 0)]
        pads[axis] = (0, pad_to - w.shape[axis])
        w = jnp.pad(w, pads)
    return w.astype(MXU_DTYPE)


def kernel(x, gdn_w_in, gdn_conv_w, gdn_a_log, gdn_dt_bias, gdn_norm_w, gdn_w_out, sc_w_in, sc_conv_w, sc_w_out, ssd_w_in, ssd_conv_w, ssd_conv_b, ssd_a_log, ssd_dt_bias, ssd_d_skip, ssd_norm_w, ssd_w_out, ln_g, ln_b, loss_target, m_gdn_w_in, m_gdn_conv_w, m_gdn_a_log, m_gdn_dt_bias, m_gdn_norm_w, m_gdn_w_out, m_sc_w_in, m_sc_conv_w, m_sc_w_out, m_ssd_w_in, m_ssd_conv_w, m_ssd_conv_b, m_ssd_a_log, m_ssd_dt_bias, m_ssd_d_skip, m_ssd_norm_w, m_ssd_w_out, m_ln_g, m_ln_b, v_gdn_w_in, v_gdn_conv_w, v_gdn_a_log, v_gdn_dt_bias, v_gdn_norm_w, v_gdn_w_out, v_sc_w_in, v_sc_conv_w, v_sc_w_out, v_ssd_w_in, v_ssd_conv_w, v_ssd_conv_b, v_ssd_a_log, v_ssd_dt_bias, v_ssd_d_skip, v_ssd_norm_w, v_ssd_w_out, v_ln_g, v_ln_b):
    weights = [gdn_w_in, gdn_conv_w, gdn_a_log, gdn_dt_bias, gdn_norm_w, gdn_w_out, sc_w_in, sc_conv_w, sc_w_out,
               ssd_w_in, ssd_conv_w, ssd_conv_b, ssd_a_log, ssd_dt_bias, ssd_d_skip, ssd_norm_w, ssd_w_out, ln_g, ln_b]
    moms = [m_gdn_w_in, m_gdn_conv_w, m_gdn_a_log, m_gdn_dt_bias, m_gdn_norm_w, m_gdn_w_out, m_sc_w_in, m_sc_conv_w,
            m_sc_w_out, m_ssd_w_in, m_ssd_conv_w, m_ssd_conv_b, m_ssd_a_log, m_ssd_dt_bias, m_ssd_d_skip, m_ssd_norm_w,
            m_ssd_w_out, m_ln_g, m_ln_b]
    vels = [v_gdn_w_in, v_gdn_conv_w, v_gdn_a_log, v_gdn_dt_bias, v_gdn_norm_w, v_gdn_w_out, v_sc_w_in, v_sc_conv_w,
            v_sc_w_out, v_ssd_w_in, v_ssd_conv_w, v_ssd_conv_b, v_ssd_a_log, v_ssd_dt_bias, v_ssd_d_skip, v_ssd_norm_w,
            v_ssd_w_out, v_ln_g, v_ln_b]
    full = dict(zip([n for n, _, _ in W_SPECS], _gather_weights(weights)))
    x0 = x[0]
    target = loss_target[0]

    layers = (("gdn", 0, GDN_IN_PAD, GDN_IN), ("sc", 0, SC_IN, SC_IN), ("ssd", 0, SSD_IN_PAD, SSD_IN), ("gdn", 1, GDN_IN_PAD, GDN_IN))

    def params(kind, j):
        if kind == "gdn":
            return [full["gdn_conv_w"][j], full["gdn_a_log"][j], full["gdn_dt_bias"][j], full["gdn_norm_w"][j]]
        if kind == "sc":
            return [full["sc_conv_w"][j]]
        return [full["ssd_conv_w"][j], full["ssd_conv_b"][j], full["ssd_a_log"][j], full["ssd_dt_bias"][j],
                full["ssd_d_skip"][j], full["ssd_norm_w"][j]]

    xs, saved = [x0], []
    for i, (kind, j, n_pad, _) in enumerate(layers):
        w_in = _mxu_weight(full[kind + "_w_in"][j], n_pad)
        w_out = _mxu_weight(full[kind + "_w_out"][j])
        proj = matmul(xs[i], w_in, name=kind + "_proj")
        if kind == "gdn":
            mix, states = gdn_fwd(proj, *params(kind, j), name="gdn_fwd")
        elif kind == "sc":
            mix, states = sc_fwd(proj, *params(kind, j), name="sc_fwd"), None
        else:
            mix, states = ssd_fwd(proj, *params(kind, j), name="ssd_fwd")
        y = matmul(mix, w_out, name=kind + "_out")
        saved.append((w_in, w_out, proj, mix, states, y))
        if i + 1 < DEPTH:
            xs.append(ln_fwd(xs[i], y, full["ln_g"][i], full["ln_b"][i], name="ln_fwd"))

    grads = {n: [None] * s[0] for n, s, _ in W_SPECS}
    dr, dg, db, loss_rows = ln_bwd(xs[DEPTH - 1], saved[DEPTH - 1][5], full["ln_g"][DEPTH - 1], b=full["ln_b"][DEPTH - 1],
                                   target=target, name="ln_bwd_loss")
    dx = None
    for i in reversed(range(DEPTH)):
        kind, j, _, n_in = layers[i]
        w_in, w_out, proj, mix, states, _ = saved[i]
        grads["ln_g"][i], grads["ln_b"][i] = dg[0], db[0]
        dmix = matmul(dr, w_out, tb=True, name=kind + "_dmix")
        grads[kind + "_w_out"][j] = matmul(mix, dr, ta=True, name=kind + "_dw_out")
        if kind == "gdn":
            dproj, (dcw, dalog, ddtb, dnw) = gdn_bwd(proj, *params(kind, j), states, dmix, name="gdn_bwd")
            grads["gdn_conv_w"][j], grads["gdn_a_log"][j], grads["gdn_dt_bias"][j], grads["gdn_norm_w"][j] = dcw, dalog, ddtb, dnw
        elif kind == "sc":
            dproj, dcw = sc_bwd(proj, *params(kind, j), dmix, name="sc_bwd")
            grads["sc_conv_w"][j] = dcw[:SC_CONV]
        else:
            dproj, (dcw, dcb, dalog, ddtb, ddsk, dnw) = ssd_bwd(proj, *params(kind, j), states, dmix, name="ssd_bwd")
            grads["ssd_conv_w"][j], grads["ssd_conv_b"][j], grads["ssd_a_log"][j] = dcw, dcb, dalog
            grads["ssd_dt_bias"][j], grads["ssd_d_skip"][j], grads["ssd_norm_w"][j] = ddtb, ddsk, dnw
        grads[kind + "_w_in"][j] = matmul(xs[i], dproj, ta=True, name=kind + "_dw_in")[:, :n_in]
        dx = matmul(dproj, w_in, tb=True, add=dr, add_scale=ALPHA, name=kind + "_dx")
        if i > 0:
            dr, dg, db = ln_bwd(xs[i - 1], saved[i - 1][5], full["ln_g"][i - 1], dx, name="ln_bwd")

    full_grads = [jnp.stack(grads[n]) for n, _, _ in W_SPECS]
    grads_flat = _reduce_scatter(full_grads)
    g_out, d_out, m_out, v_out = _adamw_all(weights, grads_flat, moms, vels)
    loss = lax.psum(loss_rows[0, 0], MESH_AXES)
    return (loss, dx[None], *g_out, *d_out, *m_out, *v_out)
```

```python
import functools

import jax
import jax.numpy as jnp
from jax import lax
from jax.experimental import pallas as pl
from jax.experimental.pallas import tpu as pltpu

F32 = jnp.float32
MXU_DTYPE = jnp.bfloat16

D_MODEL = 1024
DEPTH = 4
D_INNER = 2048
CHUNK = 64
LANES = 128
SUBLANES = 8
VMEM_LIMIT = 56 * 1024 * 1024

GDN_HEAD = 128
GDN_VH = 16
GDN_QKH = 8
GDN_QK = 1024
GDN_V = 2048
GDN_CONV = 4
GDN_CONV_DIM = 4096
GDN_IN = 6176
GDN_IN_PAD = 6272

SC_W = 2048
SC_CONV = 3
SC_IN = 8192

SSD_P = 64
SSD_H = 32
SSD_G = 4
SSD_S = 128
SSD_CONV = 4
SSD_CONV_DIM = 3072
SSD_IN = 5152
SSD_IN_PAD = 5376

ALPHA = (2 * DEPTH) ** 0.25
RMS_EPS = 1e-6
LN_EPS = 1e-5
L2_EPS = 1e-6

ADAM_LR = 0.001
ADAM_B1 = 0.9
ADAM_B2 = 0.999
ADAM_EPS = 1e-08
ADAM_WD = 0.01
ADAM_STEP = 10

MESH_AXES = ("x", "y", "c")


def _cparams(sem):
    return pltpu.CompilerParams(dimension_semantics=sem, vmem_limit_bytes=VMEM_LIMIT)


def _pick(n, prefs):
    for p in prefs:
        if n % p == 0:
            return p
    return n


def _dot(a, b, dims=(((1,), (0,)), ((), ()))):
    return lax.dot_general(a.astype(MXU_DTYPE), b.astype(MXU_DTYPE), dims, preferred_element_type=F32)


def _dot_nt(a, b):
    return _dot(a, b, (((1,), (1,)), ((), ())))


def _dot_tn(a, b):
    return _dot(a, b, (((0,), (0,)), ((), ())))


NN = (((1,), (0,)), ((), ()))
NT = (((1,), (1,)), ((), ()))
TN = (((0,), (0,)), ((), ()))


def _mxu(a, b, dims):
    return lax.dot_general(a, b, dims, preferred_element_type=F32)


def _split(x, pieces):
    out, r = [], x
    for i in range(pieces):
        p = r.astype(jnp.bfloat16)
        out.append(p)
        if i + 1 < pieces:
            r = r - p.astype(F32)
    return out


def _dot_x3(a, b, dims=NN):
    (ah, al), (bh, bl) = _split(a, 2), _split(b, 2)
    return (_mxu(al, bh, dims) + _mxu(ah, bl, dims)) + _mxu(ah, bh, dims)


def _sel(m, x, dims=NN):
    mb = m.astype(jnp.bfloat16)
    x1, x2, x3 = _split(x, 3)
    return (_mxu(mb, x3, dims) + _mxu(mb, x2, dims)) + _mxu(mb, x1, dims)


def _sel_nt(m, x):
    return _sel(m, x, NT)


def _xsel(x, m, dims=NN):
    mb = m.astype(jnp.bfloat16)
    x1, x2, x3 = _split(x, 3)
    return (_mxu(x3, mb, dims) + _mxu(x2, mb, dims)) + _mxu(x1, mb, dims)


def _xsel_nt(x, m):
    return _xsel(x, m, NT)


def _iota(shape, dim):
    return lax.broadcasted_iota(jnp.int32, shape, dim)


def _sigmoid(x):
    return 1.0 / (1.0 + jnp.exp(-x))


def _silu(x):
    return x * _sigmoid(x)


def _dsilu(x):
    s = _sigmoid(x)
    return s * (1.0 + x * (1.0 - s))


def _softplus(x):
    return jnp.maximum(x, 0.0) + jnp.log(1.0 + jnp.exp(-jnp.abs(x)))


def matmul(a, b, *, ta=False, tb=False, add=None, add_scale=1.0, name):
    if ta:
        kdim, m = a.shape
    else:
        m, kdim = a.shape
    n = b.shape[0] if tb else b.shape[1]
    assert (b.shape[1] if tb else b.shape[0]) == kdim
    tm = _pick(m, (1024, 896, 768, 512)) if ta else _pick(m, (1024, 512, 256, 128))
    tn = _pick(n, (1024, 896, 768, 512, 256, 128))
    tk = _pick(kdim, (1024, 512, 256)) if ta else _pick(kdim, (1024, 896, 768, 512))
    nk = kdim // tk
    dims = (((0 if ta else 1,), (1 if tb else 0,)), ((), ()))

    def body(a_ref, b_ref, *rest):
        o_ref = rest[-1]
        k = pl.program_id(2)
        part = _dot(a_ref[...], b_ref[...], dims)

        @pl.when(k == 0)
        def _():
            o_ref[...] = part if add is None else part + add_scale * rest[0][...]

        @pl.when(k > 0)
        def _():
            o_ref[...] += part

    a_spec = pl.BlockSpec((tk, tm), lambda i, j, k: (k, i)) if ta else pl.BlockSpec((tm, tk), lambda i, j, k: (i, k))
    b_spec = pl.BlockSpec((tn, tk), lambda i, j, k: (j, k)) if tb else pl.BlockSpec((tk, tn), lambda i, j, k: (k, j))
    o_spec = pl.BlockSpec((tm, tn), lambda i, j, k: (i, j))
    in_specs = [a_spec, b_spec] + ([] if add is None else [o_spec])
    args = (a, b) + (() if add is None else (add,))
    return pl.pallas_call(
        body, name=name, grid=(m // tm, n // tn, nk), in_specs=in_specs, out_specs=o_spec,
        out_shape=jax.ShapeDtypeStruct((m, n), F32),
        compiler_params=_cparams(("parallel", "parallel", "arbitrary")),
    )(*args)


LN_ROWS = 512


def _ln_stats(x, y):
    r = ALPHA * x + y
    mu = jnp.mean(r, axis=-1, keepdims=True)
    rc = r - mu
    var = jnp.mean(rc * rc, axis=-1, keepdims=True)
    rstd = lax.rsqrt(var + LN_EPS)
    return rc * rstd, rstd


def ln_fwd(x, y, g, b, *, name):
    rows, d = x.shape

    def body(x_ref, y_ref, g_ref, b_ref, o_ref):
        xhat, _ = _ln_stats(x_ref[...], y_ref[...])
        o_ref[...] = xhat * g_ref[...] + b_ref[...]

    blk = pl.BlockSpec((LN_ROWS, d), lambda i: (i, 0))
    vec = pl.BlockSpec((1, d), lambda i: (0, 0))
    return pl.pallas_call(
        body, name=name, grid=(rows // LN_ROWS,), in_specs=[blk, blk, vec, vec], out_specs=blk,
        out_shape=jax.ShapeDtypeStruct((rows, d), F32), compiler_params=_cparams(("parallel",)),
    )(x, y, g.reshape(1, d), b.reshape(1, d))


def ln_bwd(x, y, g, dxn=None, *, b=None, target=None, name):
    rows, d = x.shape
    final = target is not None

    def body(x_ref, y_ref, g_ref, *rest):
        if final:
            b_ref, t_ref, dr_ref, dg_ref, db_ref, loss_ref = rest
        else:
            dxn_ref, dr_ref, dg_ref, db_ref = rest
        i = pl.program_id(0)
        xhat, rstd = _ln_stats(x_ref[...], y_ref[...])
        gv = g_ref[...]
        if final:
            err = xhat * gv + b_ref[...] - t_ref[...]
            dxn_v = err * (1.0 / d)
            part = 0.5 * jnp.sum(jnp.mean(err * err, axis=-1, keepdims=True), axis=0, keepdims=True)
        else:
            dxn_v = dxn_ref[...]
        dxh = dxn_v * gv
        m1 = jnp.mean(dxh, axis=-1, keepdims=True)
        m2 = jnp.mean(dxh * xhat, axis=-1, keepdims=True)
        dr_ref[...] = rstd * (dxh - m1 - xhat * m2)

        @pl.when(i == 0)
        def _():
            dg_ref[...] = jnp.zeros_like(dg_ref)
            db_ref[...] = jnp.zeros_like(db_ref)
            if final:
                loss_ref[...] = jnp.zeros_like(loss_ref)

        dg_ref[...] += jnp.sum(dxn_v * xhat, axis=0, keepdims=True)
        db_ref[...] += jnp.sum(dxn_v, axis=0, keepdims=True)
        if final:
            loss_ref[...] += jnp.broadcast_to(part, loss_ref.shape)

    blk = pl.BlockSpec((LN_ROWS, d), lambda i: (i, 0))
    vec = pl.BlockSpec((1, d), lambda i: (0, 0))
    lvec = pl.BlockSpec((1, LANES), lambda i: (0, 0))
    out_shape = [jax.ShapeDtypeStruct((rows, d), F32), jax.ShapeDtypeStruct((1, d), F32), jax.ShapeDtypeStruct((1, d), F32)]
    out_specs = [blk, vec, vec]
    if final:
        in_specs = [blk, blk, vec, vec, blk]
        args = (x, y, g.reshape(1, d), b.reshape(1, d), target)
        out_shape.append(jax.ShapeDtypeStruct((1, LANES), F32))
        out_specs.append(lvec)
    else:
        in_specs = [blk, blk, vec, blk]
        args = (x, y, g.reshape(1, d), dxn)
    return pl.pallas_call(
        body, name=name, grid=(rows // LN_ROWS,), in_specs=in_specs, out_specs=out_specs, out_shape=out_shape,
        compiler_params=_cparams(("arbitrary",)),
    )(*args)


def _conv_from_ext(ext_ref, w, width, rows):
    out = None
    for j in range(width):
        off = SUBLANES - (width - 1) + j
        term = ext_ref[off:off + rows, :] * w[j:j + 1, :]
        out = term if out is None else out + term
    return out


def _conv_wgrad_from_ext(ext_ref, dout, width, rows):
    res = []
    for j in range(width):
        off = SUBLANES - (width - 1) + j
        res.append(jnp.sum(ext_ref[off:off + rows, :] * dout, axis=0, keepdims=True))
    return res


def _conv_dgrad_from_ext(dext_ref, w, width, rows):
    out = None
    for j in range(width):
        off = (width - 1) - j
        term = dext_ref[off:off + rows, :] * w[j:j + 1, :]
        out = term if out is None else out + term
    return out


SC_ROWS = 128


def sc_fwd(proj, conv_w, *, name):
    rows = proj.shape[0]
    nb = rows // SC_ROWS
    hb = SC_ROWS // SUBLANES

    def body(p_ref, halo_ref, w_ref, o_ref, ext_ref):
        i = pl.program_id(0)
        h = p_ref[:, 0:SC_W]
        bg = p_ref[:, SC_W:2 * SC_W]
        cg = p_ref[:, 2 * SC_W:3 * SC_W]
        z = p_ref[:, 3 * SC_W:4 * SC_W]
        uh = halo_ref[:, 2 * SC_W:3 * SC_W] * halo_ref[:, 0:SC_W]
        ext_ref[0:SUBLANES, :] = jnp.where(i == 0, 0.0, uh)
        ext_ref[SUBLANES:, :] = cg * h
        cv = _conv_from_ext(ext_ref, w_ref[...], SC_CONV, SC_ROWS)
        o_ref[...] = bg * cv * _silu(z)

    return pl.pallas_call(
        body, name=name, grid=(nb,),
        in_specs=[pl.BlockSpec((SC_ROWS, SC_IN), lambda i: (i, 0)),
                  pl.BlockSpec((SUBLANES, SC_IN), lambda i: (jnp.maximum(i * hb - 1, 0), 0)),
                  pl.BlockSpec((SUBLANES, SC_W), lambda i: (0, 0))],
        out_specs=pl.BlockSpec((SC_ROWS, SC_W), lambda i: (i, 0)),
        out_shape=jax.ShapeDtypeStruct((rows, SC_W), F32),
        scratch_shapes=[pltpu.VMEM((SC_ROWS + SUBLANES, SC_W), F32)],
        compiler_params=_cparams(("parallel",)),
    )(proj, proj, _pad_rows(conv_w))


def sc_bwd(proj, conv_w, dmix, *, name):
    rows = proj.shape[0]
    nb = rows // SC_ROWS
    hb = SC_ROWS // SUBLANES

    def body(p_ref, halo_ref, w_ref, dm_ref, dp_ref, dw_ref, ext_ref, dext_ref):
        i = pl.program_id(0)
        blk = nb - 1 - i
        w = w_ref[...]
        h = p_ref[:, 0:SC_W]
        bg = p_ref[:, SC_W:2 * SC_W]
        cg = p_ref[:, 2 * SC_W:3 * SC_W]
        z = p_ref[:, 3 * SC_W:4 * SC_W]
        uh = halo_ref[:, 2 * SC_W:3 * SC_W] * halo_ref[:, 0:SC_W]
        ext_ref[0:SUBLANES, :] = jnp.where(blk == 0, 0.0, uh)
        ext_ref[SUBLANES:, :] = cg * h
        cv = _conv_from_ext(ext_ref, w, SC_CONV, SC_ROWS)
        dm = dm_ref[...]
        sz = _silu(z)
        dy = dm * sz
        dp_ref[:, 3 * SC_W:4 * SC_W] = dm * bg * cv * _dsilu(z)
        dp_ref[:, SC_W:2 * SC_W] = dy * cv
        dcv = dy * bg

        @pl.when(i == 0)
        def _():
            dext_ref[SC_ROWS:, :] = jnp.zeros((SUBLANES, SC_W), F32)
            dw_ref[...] = jnp.zeros_like(dw_ref)

        dext_ref[0:SC_ROWS, :] = dcv
        du = _conv_dgrad_from_ext(dext_ref, w, SC_CONV, SC_ROWS)
        dp_ref[:, 0:SC_W] = du * cg
        dp_ref[:, 2 * SC_W:3 * SC_W] = du * h
        dws = _conv_wgrad_from_ext(ext_ref, dcv, SC_CONV, SC_ROWS)
        for j in range(SC_CONV):
            dw_ref[j:j + 1, :] += dws[j]
        dext_ref[SC_ROWS:, :] = dcv[0:SUBLANES, :]

    return pl.pallas_call(
        body, name=name, grid=(nb,),
        in_specs=[pl.BlockSpec((SC_ROWS, SC_IN), lambda i: (nb - 1 - i, 0)),
                  pl.BlockSpec((SUBLANES, SC_IN), lambda i: (jnp.maximum((nb - 1 - i) * hb - 1, 0), 0)),
                  pl.BlockSpec((SUBLANES, SC_W), lambda i: (0, 0)),
                  pl.BlockSpec((SC_ROWS, SC_W), lambda i: (nb - 1 - i, 0))],
        out_specs=[pl.BlockSpec((SC_ROWS, SC_IN), lambda i: (nb - 1 - i, 0)),
                   pl.BlockSpec((SUBLANES, SC_W), lambda i: (0, 0))],
        out_shape=[jax.ShapeDtypeStruct((rows, SC_IN), F32), jax.ShapeDtypeStruct((SUBLANES, SC_W), F32)],
        scratch_shapes=[pltpu.VMEM((SC_ROWS + SUBLANES, SC_W), F32), pltpu.VMEM((SC_ROWS + SUBLANES, SC_W), F32)],
        compiler_params=_cparams(("arbitrary",)),
    )(proj, proj, _pad_rows(conv_w), dmix)


def _pad_rows(w, rows=SUBLANES):
    return jnp.pad(w, ((0, rows - w.shape[0]), (0, 0)))


def _pad_lanes(v, lanes=LANES):
    v = v.reshape(1, -1)
    return jnp.pad(v, ((0, 0), (0, lanes - v.shape[1])))


def _tri(n, lower=True):
    r, c = _iota((n, n), 0), _iota((n, n), 1)
    return jnp.where((c <= r) if lower else (c >= r), 1.0, 0.0)


def _eye(n):
    return jnp.where(_iota((n, n), 0) == _iota((n, n), 1), 1.0, 0.0)


def _head_expand(n, width):
    return jnp.where(_iota((LANES, n), 1) // width == _iota((LANES, n), 0), 1.0, 0.0)


def _col(v, h):
    return jnp.sum(jnp.where(_iota(v.shape, 1) == h, v, 0.0), axis=1, keepdims=True)


def _row(v, r):
    return jnp.sum(jnp.where(_iota(v.shape, 0) == r, v, 0.0), axis=0, keepdims=True)


def _expand_row(v, e):
    return jnp.max(_xsel(jnp.broadcast_to(v, (SUBLANES, LANES)), e), axis=0, keepdims=True)


SSD_ROWS = 128
SSD_X0 = D_INNER
SSD_DT0 = D_INNER + SSD_CONV_DIM
SSD_B0 = D_INNER
SSD_C0 = D_INNER + SSD_G * SSD_S
SSD_GW = D_INNER // SSD_G
SSD_HG = SSD_H // SSD_G


def _ssd_prologue(blk, p_ref, halo_ref, cw_ref, cb_ref, dtb_ref, ext_ref, xbc_ref, dt_ref):
    ext_ref[0:SUBLANES, :] = jnp.where(blk == 0, 0.0, halo_ref[:, SSD_X0:SSD_DT0])
    ext_ref[SUBLANES:, :] = p_ref[:, SSD_X0:SSD_DT0]
    pre = _conv_from_ext(ext_ref, cw_ref[...], SSD_CONV, SSD_ROWS) + cb_ref[...]
    xbc_ref[...] = _silu(pre)
    dt_ref[...] = _softplus(p_ref[:, SSD_DT0:SSD_DT0 + LANES] + dtb_ref[...])
    return pre


def _ssd_chunk_decays(dt_c, a_row, ltri, eye_l, act_ref):
    da = dt_c * a_row
    ac = _sel(ltri,da)
    act_ref[...] = _sel_nt(eye_l,ac)
    ac_last = _row(ac, CHUNK - 1)
    return ac, jnp.exp(ac_last - ac), jnp.exp(ac), jnp.exp(ac_last)


def _ssd_seg(ac, act_ref, h, causal):
    return jnp.where(causal, jnp.exp(jnp.minimum(_col(ac, h) - act_ref[pl.ds(h, 1), :], 0.0)), 0.0)


def _ssd_half(pair, e):
    upper = _iota(pair.shape, 1) >= SSD_P
    return jnp.where(upper if e % 2 else jnp.logical_not(upper), pair, 0.0)


def _ssd_group_fwd(g, xbc_ref, rows, dt_exp, tail_exp, cdec_exp, ac, act_ref, s_g, causal):
    gl = slice(g * SSD_GW, (g + 1) * SSD_GW)
    bg = xbc_ref[rows, SSD_B0 + g * SSD_S:SSD_B0 + (g + 1) * SSD_S]
    cg = xbc_ref[rows, SSD_C0 + g * SSD_S:SSD_C0 + (g + 1) * SSD_S]
    xdt = xbc_ref[rows, gl] * dt_exp[:, gl]
    cb = _dot_nt(cg, bg)
    cs = _dot(cg, s_g)
    segs = [_ssd_seg(ac, act_ref, g * SSD_HG + e, causal) for e in range(SSD_HG)]
    gms = [seg * cb for seg in segs]
    xps = [xdt[:, p * LANES:(p + 1) * LANES] for p in range(SSD_HG // 2)]
    parts = [_dot(gms[e], _ssd_half(xps[e // 2], e)) for e in range(SSD_HG)]
    yd = jnp.concatenate([parts[2 * p] + parts[2 * p + 1] for p in range(SSD_HG // 2)], axis=1)
    st = _dot_tn(bg, xdt * tail_exp[:, gl])
    return yd + cs * cdec_exp[:, gl], st, bg, cg, cb, xdt, cs, segs, gms


def ssd_fwd(proj, conv_w, conv_b, a_log, dt_bias, d_skip, norm_w, *, name):
    rows_total = proj.shape[0]
    nb = rows_total // SSD_ROWS
    hb = SSD_ROWS // SUBLANES
    cpb = SSD_ROWS // CHUNK

    def body(p_ref, halo_ref, cw_ref, cb_ref, alog_ref, dtb_ref, dsk_ref, nw_ref, mix_ref, st_ref,
             ext_ref, xbc_ref, dt_ref, s_ref, act_ref):
        i = pl.program_id(0)

        @pl.when(i == 0)
        def _():
            s_ref[...] = jnp.zeros_like(s_ref)

        _ssd_prologue(i, p_ref, halo_ref, cw_ref, cb_ref, dtb_ref, ext_ref, xbc_ref, dt_ref)
        a_row = -jnp.exp(alog_ref[...])
        expand = _head_expand(D_INNER, SSD_P)
        dsk_exp = _expand_row(dsk_ref[...], expand)
        ltri, eye_l = _tri(CHUNK), _eye(LANES)
        causal = _iota((CHUNK, CHUNK), 1) <= _iota((CHUNK, CHUNK), 0)

        def chunk(c, carry):
            rows = pl.ds(pl.multiple_of(c * CHUNK, CHUNK), CHUNK)
            dt_c = dt_ref[rows, :]
            ac, tail, cdec, tot = _ssd_chunk_decays(dt_c, a_row, ltri, eye_l, act_ref)
            dt_exp = _xsel(dt_c, expand)
            tail_exp = _xsel(tail, expand)
            cdec_exp = _xsel(cdec, expand)
            tot_exp = _expand_row(tot, expand)
            for g in range(SSD_G):
                gl = slice(g * SSD_GW, (g + 1) * SSD_GW)
                s_g = s_ref[g]
                st_ref[c, g] = s_g
                y, st = _ssd_group_fwd(g, xbc_ref, rows, dt_exp, tail_exp, cdec_exp, ac, act_ref, s_g, causal)[:2]
                s_ref[g] = s_g * tot_exp[:, gl] + st
                y = (y + dsk_exp[:, gl] * xbc_ref[rows, gl]) * _silu(p_ref[rows, gl])
                r = lax.rsqrt(jnp.mean(y * y, axis=-1, keepdims=True) + RMS_EPS)
                mix_ref[rows, gl] = y * r * nw_ref[:, gl]
            return carry

        lax.fori_loop(0, cpb, chunk, 0)

    vec = lambda n: pl.BlockSpec((1, n), lambda i: (0, 0))
    return pl.pallas_call(
        body, name=name, grid=(nb,),
        in_specs=[pl.BlockSpec((SSD_ROWS, SSD_IN_PAD), lambda i: (i, 0)),
                  pl.BlockSpec((SUBLANES, SSD_IN_PAD), lambda i: (jnp.maximum(i * hb - 1, 0), 0)),
                  pl.BlockSpec((SUBLANES, SSD_CONV_DIM), lambda i: (0, 0)),
                  vec(SSD_CONV_DIM), vec(LANES), vec(LANES), vec(LANES), vec(D_INNER)],
        out_specs=[pl.BlockSpec((SSD_ROWS, D_INNER), lambda i: (i, 0)),
                   pl.BlockSpec((cpb, SSD_G, SSD_S, SSD_GW), lambda i: (i, 0, 0, 0))],
        out_shape=[jax.ShapeDtypeStruct((rows_total, D_INNER), F32),
                   jax.ShapeDtypeStruct((rows_total // CHUNK, SSD_G, SSD_S, SSD_GW), F32)],
        scratch_shapes=[pltpu.VMEM((SSD_ROWS + SUBLANES, SSD_CONV_DIM), F32),
                        pltpu.VMEM((SSD_ROWS, SSD_CONV_DIM), F32),
                        pltpu.VMEM((SSD_ROWS, LANES), F32),
                        pltpu.VMEM((SSD_G, SSD_S, SSD_GW), F32),
                        pltpu.VMEM((LANES, CHUNK), F32)],
        compiler_params=_cparams(("arbitrary",)),
    )(proj, proj, _pad_rows(conv_w), conv_b.reshape(1, -1), _pad_lanes(a_log), _pad_lanes(dt_bias),
      _pad_lanes(d_skip), norm_w.reshape(1, -1))


def ssd_bwd(proj, conv_w, conv_b, a_log, dt_bias, d_skip, norm_w, states, dmix, *, name):
    rows_total = proj.shape[0]
    nb = rows_total // SSD_ROWS
    hb = SSD_ROWS // SUBLANES
    cpb = SSD_ROWS // CHUNK

    def body(p_ref, halo_ref, cw_ref, cb_ref, alog_ref, dtb_ref, dsk_ref, nw_ref, st_ref, dm_ref,
             dp_ref, dcw_ref, dcb_ref, dalog_ref, ddtb_ref, ddsk_ref, dnw_ref,
             ext_ref, xbc_ref, dt_ref, ds_ref, act_ref, dext_ref, dac_ref, dact_ref, ddskw_ref):
        i = pl.program_id(0)
        blk = nb - 1 - i

        @pl.when(i == 0)
        def _():
            ds_ref[...] = jnp.zeros_like(ds_ref)
            dext_ref[SSD_ROWS:, :] = jnp.zeros((SUBLANES, SSD_CONV_DIM), F32)
            ddskw_ref[...] = jnp.zeros_like(ddskw_ref)
            for r in (dcw_ref, dcb_ref, dalog_ref, ddtb_ref, ddsk_ref, dnw_ref):
                r[...] = jnp.zeros_like(r)

        _ssd_prologue(blk, p_ref, halo_ref, cw_ref, cb_ref, dtb_ref, ext_ref, xbc_ref, dt_ref)
        a_row = -jnp.exp(alog_ref[...])
        expand = _head_expand(D_INNER, SSD_P)
        dsk_exp = _expand_row(dsk_ref[...], expand)
        ltri, utri, eye_l, eye_c = _tri(CHUNK), _tri(CHUNK, lower=False), _eye(LANES), _eye(CHUNK)
        causal = _iota((CHUNK, CHUNK), 1) <= _iota((CHUNK, CHUNK), 0)
        dp_ref[:, SSD_DT0 + LANES:] = jnp.zeros((SSD_ROWS, SSD_IN_PAD - SSD_DT0 - LANES), F32)

        def chunk(cc, carry):
            c = cpb - 1 - cc
            rows = pl.ds(pl.multiple_of(c * CHUNK, CHUNK), CHUNK)
            dt_c = dt_ref[rows, :]
            ac, tail, cdec, tot = _ssd_chunk_decays(dt_c, a_row, ltri, eye_l, act_ref)
            dt_exp = _xsel(dt_c, expand)
            tail_exp = _xsel(tail, expand)
            cdec_exp = _xsel(cdec, expand)
            tot_exp = _expand_row(tot, expand)
            dac_ref[...] = jnp.zeros_like(dac_ref)
            dact_ref[...] = jnp.zeros_like(dact_ref)
            d_cdec = jnp.zeros((CHUNK, LANES), F32)
            d_tail = jnp.zeros((CHUNK, LANES), F32)
            d_dt = jnp.zeros((CHUNK, LANES), F32)
            d_tot = jnp.zeros((1, LANES), F32)
            for g in range(SSD_G):
                gl = slice(g * SSD_GW, (g + 1) * SSD_GW)
                ex_g = expand[:, gl]
                s_g = st_ref[c, g]
                y, _, bg, cg, cb, xdt, cs, segs, gms = _ssd_group_fwd(g, xbc_ref, rows, dt_exp, tail_exp, cdec_exp, ac, act_ref, s_g, causal)
                xs = xbc_ref[rows, gl]
                z = p_ref[rows, gl]
                sz = _silu(z)
                y2 = y + dsk_exp[:, gl] * xs
                yg = y2 * sz
                r = lax.rsqrt(jnp.mean(yg * yg, axis=-1, keepdims=True) + RMS_EPS)
                yn = yg * r
                dm = dm_ref[rows, gl]
                dnw_ref[:, gl] += jnp.sum(dm * yn, axis=0, keepdims=True)
                dyn = dm * nw_ref[:, gl]
                dyg = r * (dyn - yn * jnp.mean(dyn * yn, axis=-1, keepdims=True))
                dp_ref[rows, gl] = dyg * y2 * _dsilu(z)
                dy = dyg * sz
                ddskw_ref[:, gl] += jnp.sum(dy * xs, axis=0, keepdims=True)
                ds_g = ds_ref[g]
                dyc = dy * cdec_exp[:, gl]
                ds_ref[g] = ds_g * tot_exp[:, gl] + _dot_tn(cg, dyc)
                sds = jnp.broadcast_to(jnp.sum(s_g * ds_g, axis=0, keepdims=True), (SUBLANES, SSD_GW))
                d_tot = d_tot + jnp.max(_xsel_nt(sds, ex_g), axis=0, keepdims=True)
                dcg = _dot_nt(dyc, s_g)
                d_cdec = d_cdec + _xsel_nt(dy * cs, ex_g)
                xdtd = xdt * tail_exp[:, gl]
                d_xdtd = _dot(bg, ds_g)
                dbg = _dot_nt(xdtd, ds_g)
                d_tail = d_tail + _xsel_nt(d_xdtd * xdt, ex_g)
                heads = range(SSD_HG)
                dy_h = [_ssd_half(dy[:, (e // 2) * LANES:(e // 2 + 1) * LANES], e) for e in heads]
                back = [_dot_tn(gms[e], dy_h[e]) for e in heads]
                dg_m = [jnp.where(causal, _dot_nt(dy_h[e], xdt[:, (e // 2) * LANES:(e // 2 + 1) * LANES]), 0.0) for e in heads]
                d_cb = None
                for e in heads:
                    h = g * SSD_HG + e
                    term = dg_m[e] * segs[e]
                    d_cb = term if d_cb is None else d_cb + term
                    em = dg_m[e] * gms[e]
                    dac_ref[...] += jnp.where(_iota((CHUNK, LANES), 1) == h, jnp.sum(em, axis=1, keepdims=True), 0.0)
                    dact_ref[h:h + 1, :] = jnp.sum(em, axis=0, keepdims=True)
                dcg = dcg + _dot(d_cb, bg)
                dbg = dbg + _dot_tn(d_cb, cg)
                d_xdt = d_xdtd * tail_exp[:, gl] + jnp.concatenate(
                    [back[2 * p] + back[2 * p + 1] for p in range(SSD_HG // 2)], axis=1)
                d_dt = d_dt + _xsel_nt(d_xdt * xs, ex_g)
                dext_ref[rows, gl] = d_xdt * dt_exp[:, gl] + dy * dsk_exp[:, gl]
                dext_ref[rows, SSD_B0 + g * SSD_S:SSD_B0 + (g + 1) * SSD_S] = dbg
                dext_ref[rows, SSD_C0 + g * SSD_S:SSD_C0 + (g + 1) * SSD_S] = dcg
            d_ac = dac_ref[...] - _sel_nt(eye_c,dact_ref[...]) + d_cdec * cdec - d_tail * tail
            d_last = jnp.sum(d_tail * tail, axis=0, keepdims=True) + d_tot * tot
            d_ac = jnp.where(_iota((CHUNK, LANES), 0) == CHUNK - 1, d_ac + d_last, d_ac)
            d_da = _sel(utri,d_ac)
            d_dt = d_dt + d_da * a_row
            dalog_ref[...] += jnp.sum(d_da * dt_c, axis=0, keepdims=True) * a_row
            d_raw = d_dt * _sigmoid(p_ref[rows, SSD_DT0:SSD_DT0 + LANES] + dtb_ref[...])
            d_raw = jnp.where(_iota((CHUNK, LANES), 1) < SSD_H, d_raw, 0.0)
            ddtb_ref[...] += jnp.sum(d_raw, axis=0, keepdims=True)
            dp_ref[rows, SSD_DT0:SSD_DT0 + LANES] = d_raw
            return carry

        lax.fori_loop(0, cpb, chunk, 0)
        pre = _conv_from_ext(ext_ref, cw_ref[...], SSD_CONV, SSD_ROWS) + cb_ref[...]
        d_pre = dext_ref[0:SSD_ROWS, :] * _dsilu(pre)
        dext_ref[0:SSD_ROWS, :] = d_pre
        dcb_ref[...] += jnp.sum(d_pre, axis=0, keepdims=True)
        dp_ref[:, SSD_X0:SSD_DT0] = _conv_dgrad_from_ext(dext_ref, cw_ref[...], SSD_CONV, SSD_ROWS)
        dws = _conv_wgrad_from_ext(ext_ref, d_pre, SSD_CONV, SSD_ROWS)
        for j in range(SSD_CONV):
            dcw_ref[j:j + 1, :] += dws[j]
        dext_ref[SSD_ROWS:, :] = d_pre[0:SUBLANES, :]

        @pl.when(i == nb - 1)
        def _():
            ddsk_ref[...] = jnp.max(_xsel_nt(jnp.broadcast_to(ddskw_ref[...], (SUBLANES, D_INNER)), expand), axis=0, keepdims=True)

    vec = lambda n: pl.BlockSpec((1, n), lambda i: (0, 0))
    outs = pl.pallas_call(
        body, name=name, grid=(nb,),
        in_specs=[pl.BlockSpec((SSD_ROWS, SSD_IN_PAD), lambda i: (nb - 1 - i, 0)),
                  pl.BlockSpec((SUBLANES, SSD_IN_PAD), lambda i: (jnp.maximum((nb - 1 - i) * hb - 1, 0), 0)),
                  pl.BlockSpec((SUBLANES, SSD_CONV_DIM), lambda i: (0, 0)),
                  vec(SSD_CONV_DIM), vec(LANES), vec(LANES), vec(LANES), vec(D_INNER),
                  pl.BlockSpec((cpb, SSD_G, SSD_S, SSD_GW), lambda i: (nb - 1 - i, 0, 0, 0)),
                  pl.BlockSpec((SSD_ROWS, D_INNER), lambda i: (nb - 1 - i, 0))],
        out_specs=[pl.BlockSpec((SSD_ROWS, SSD_IN_PAD), lambda i: (nb - 1 - i, 0)),
                   pl.BlockSpec((SUBLANES, SSD_CONV_DIM), lambda i: (0, 0)),
                   vec(SSD_CONV_DIM), vec(LANES), vec(LANES), vec(LANES), vec(D_INNER)],
        out_shape=[jax.ShapeDtypeStruct((rows_total, SSD_IN_PAD), F32),
                   jax.ShapeDtypeStruct((SUBLANES, SSD_CONV_DIM), F32),
                   jax.ShapeDtypeStruct((1, SSD_CONV_DIM), F32), jax.ShapeDtypeStruct((1, LANES), F32),
                   jax.ShapeDtypeStruct((1, LANES), F32), jax.ShapeDtypeStruct((1, LANES), F32),
                   jax.ShapeDtypeStruct((1, D_INNER), F32)],
        scratch_shapes=[pltpu.VMEM((SSD_ROWS + SUBLANES, SSD_CONV_DIM), F32),
                        pltpu.VMEM((SSD_ROWS, SSD_CONV_DIM), F32),
                        pltpu.VMEM((SSD_ROWS, LANES), F32),
                        pltpu.VMEM((SSD_G, SSD_S, SSD_GW), F32),
                        pltpu.VMEM((LANES, CHUNK), F32),
                        pltpu.VMEM((SSD_ROWS + SUBLANES, SSD_CONV_DIM), F32),
                        pltpu.VMEM((CHUNK, LANES), F32),
                        pltpu.VMEM((LANES, CHUNK), F32),
                        pltpu.VMEM((1, D_INNER), F32)],
        compiler_params=_cparams(("arbitrary",)),
    )(proj, proj, _pad_rows(conv_w), conv_b.reshape(1, -1), _pad_lanes(a_log), _pad_lanes(dt_bias),
      _pad_lanes(d_skip), norm_w.reshape(1, -1), states, dmix)
    dproj, dcw, dcb, dalog, ddtb, ddsk, dnw = outs
    return dproj, [dcw[:SSD_CONV], dcb[0], dalog[0, :SSD_H], ddtb[0, :SSD_H], ddsk[0, :SSD_H], dnw[0]]


GDN_ROWS = 128
GDN_K0 = GDN_QK
GDN_V0 = 2 * GDN_QK
GDN_Z0 = GDN_CONV_DIM
GDN_BA0 = GDN_CONV_DIM + GDN_V
GDN_GL = GDN_VH
GDN_SCALE = GDN_HEAD ** -0.5
GDN_GROUP = 8


def _gdn_lane_params(v):
    return jnp.pad(v.reshape(1, GDN_VH), ((0, 0), (GDN_GL, LANES - GDN_GL - GDN_VH)))


def _inv_unit_lower(a, eye_c):
    x = eye_c - a
    ph, pl_ = _split(a, 2)
    n = 2
    while n < CHUNK:
        p = (_mxu(pl_, ph, NN) + _mxu(ph, pl_, NN)) + _mxu(ph, ph, NN)
        ph, pl_ = _split(p, 2)
        xh, xl = _split(x, 2)
        x = x + ((_mxu(xl, ph, NN) + _mxu(xh, pl_, NN)) + _mxu(xh, ph, NN))
        n *= 2
    return x


def _gdn_prologue(blk, p_ref, halo_ref, cw_ref, alog_ref, dtb_ref, ext_ref, qkv_ref, beta_ref, g_ref):
    ext_ref[0:SUBLANES, :] = jnp.where(blk == 0, 0.0, halo_ref[:, 0:GDN_CONV_DIM])
    ext_ref[SUBLANES:, :] = p_ref[:, 0:GDN_CONV_DIM]
    w = cw_ref[...]
    for hq in range(2 * GDN_QKH):
        cols = slice(hq * GDN_HEAD, (hq + 1) * GDN_HEAD)
        pre = None
        for j in range(GDN_CONV):
            off = SUBLANES - (GDN_CONV - 1) + j
            term = ext_ref[off:off + GDN_ROWS, cols] * w[j:j + 1, cols]
            pre = term if pre is None else pre + term
        a = _silu(pre)
        r = lax.rsqrt(jnp.sum(a * a, axis=-1, keepdims=True) + L2_EPS)
        qkv_ref[:, cols] = a * (r * (GDN_SCALE if hq < GDN_QKH else 1.0))
    vcols = slice(GDN_V0, GDN_CONV_DIM)
    pre = None
    for j in range(GDN_CONV):
        off = SUBLANES - (GDN_CONV - 1) + j
        term = ext_ref[off:off + GDN_ROWS, vcols] * w[j:j + 1, vcols]
        pre = term if pre is None else pre + term
    qkv_ref[:, vcols] = _silu(pre)
    ba = p_ref[:, GDN_BA0:GDN_BA0 + LANES]
    beta_ref[...] = _sigmoid(ba)
    g_ref[...] = -jnp.exp(alog_ref[...]) * _softplus(ba + dtb_ref[...])


def _each(f, *lists):
    return [f(*z) for z in zip(*lists)]


def _inv_unit_lower_each(a_list, eye_c):
    xs = [eye_c - a for a in a_list]
    ps = [_split(a, 2) for a in a_list]
    n = 2
    while n < CHUNK:
        ps = [_split((_mxu(pl_, ph, NN) + _mxu(ph, pl_, NN)) + _mxu(ph, ph, NN), 2) for ph, pl_ in ps]
        xs_split = [_split(x, 2) for x in xs]
        xs = [x + ((_mxu(xl, ph, NN) + _mxu(xh, pl_, NN)) + _mxu(xh, ph, NN))
              for x, (xh, xl), (ph, pl_) in zip(xs, xs_split, ps)]
        n *= 2
    return xs


def _gdn_heads_fwd(q, k, v, kk, qk, gcol, grow, glast, bcol, s, causal, strict, eye_c, t=None):
    decay = _each(lambda gc_, gr_: jnp.where(causal, jnp.exp(jnp.minimum(gc_ - gr_, 0.0)), 0.0), gcol, grow)
    egc = _each(jnp.exp, gcol)
    etail = _each(lambda gl_, gc_: jnp.exp(gl_ - gc_), glast, gcol)
    cd = _each(jnp.exp, glast)
    a = _each(lambda b_, kk_, d_: jnp.where(strict, b_ * kk_ * d_, 0.0), bcol, kk, decay)
    if t is None:
        t = _inv_unit_lower_each(a, eye_c)
    kb = _each(lambda k_, b_: k_ * b_, k, bcol)
    rhs_w = _each(lambda kb_, e_: kb_ * e_, kb, egc)
    u = _each(lambda t_, v_, b_: _dot_x3(t_, v_ * b_), t, v, bcol)
    w = _each(_dot_x3, t, rhs_w)
    attn = _each(lambda qk_, d_: qk_ * d_, qk, decay)
    ws = _each(_dot, w, s)
    v_new = _each(lambda u_, ws_: u_ - ws_, u, ws)
    qd = _each(lambda q_, e_: q_ * e_, q, egc)
    kt = _each(lambda k_, e_: k_ * e_, k, etail)
    o1 = _each(_dot, qd, s)
    o2 = _each(_dot, attn, v_new)
    out = _each(lambda a_, b_: a_ + b_, o1, o2)
    upd = _each(_dot_tn, kt, v_new)
    s_new = _each(lambda s_, c_, u_: s_ * c_ + u_, s, cd, upd)
    return dict(decay=decay, egc=egc, etail=etail, cd=cd, a=a, t=t, kb=kb, rhs_w=rhs_w, u=u, w=w, attn=attn,
                v_new=v_new, qd=qd, kt=kt, out=out, s_new=s_new)


def gdn_fwd(proj, conv_w, a_log, dt_bias, norm_w, *, name):
    rows_total = proj.shape[0]
    nb = rows_total // GDN_ROWS
    hb = GDN_ROWS // SUBLANES
    cpb = GDN_ROWS // CHUNK

    def body(p_ref, halo_ref, cw_ref, alog_ref, dtb_ref, nw_ref, mix_ref, st_ref, tm_ref,
             ext_ref, qkv_ref, beta_ref, g_ref, s_ref, gct_ref):
        i = pl.program_id(0)

        @pl.when(i == 0)
        def _():
            s_ref[...] = jnp.zeros_like(s_ref)

        _gdn_prologue(i, p_ref, halo_ref, cw_ref, alog_ref, dtb_ref, ext_ref, qkv_ref, beta_ref, g_ref)
        ltri, eye_l, eye_c = _tri(CHUNK), _eye(LANES), _eye(CHUNK)
        causal = _iota((CHUNK, CHUNK), 1) <= _iota((CHUNK, CHUNK), 0)
        strict = _iota((CHUNK, CHUNK), 1) < _iota((CHUNK, CHUNK), 0)
        nw = nw_ref[...]

        def chunk(c, carry):
            rows = pl.ds(pl.multiple_of(c * CHUNK, CHUNK), CHUNK)
            gc = _sel(ltri,g_ref[rows, :])
            gct_ref[...] = _sel_nt(eye_l,gc)
            glast_row = _row(gc, CHUNK - 1)
            beta_c = beta_ref[rows, :]
            for h0 in range(0, GDN_VH, GDN_GROUP):
                hs = list(range(h0, h0 + GDN_GROUP))
                qs = {hq: qkv_ref[rows, hq * GDN_HEAD:(hq + 1) * GDN_HEAD] for hq in range(h0 // 2, (h0 + GDN_GROUP) // 2)}
                ks = {hq: qkv_ref[rows, GDN_K0 + hq * GDN_HEAD:GDN_K0 + (hq + 1) * GDN_HEAD] for hq in qs}
                kks = {hq: _dot_nt(ks[hq], ks[hq]) for hq in qs}
                qks = {hq: _dot_nt(qs[hq], ks[hq]) for hq in qs}
                ss = [s_ref[h] for h in hs]
                for h, s in zip(hs, ss):
                    st_ref[c, h] = s
                f = _gdn_heads_fwd(
                    [qs[h // 2] for h in hs], [ks[h // 2] for h in hs],
                    [qkv_ref[rows, GDN_V0 + h * GDN_HEAD:GDN_V0 + (h + 1) * GDN_HEAD] for h in hs],
                    [kks[h // 2] for h in hs], [qks[h // 2] for h in hs],
                    [_col(gc, GDN_GL + h) for h in hs], [gct_ref[GDN_GL + h:GDN_GL + h + 1, :] for h in hs],
                    [_col(glast_row, GDN_GL + h) for h in hs], [_col(beta_c, h) for h in hs], ss, causal, strict, eye_c)
                for i_h, h in enumerate(hs):
                    hc = slice(h * GDN_HEAD, (h + 1) * GDN_HEAD)
                    s_ref[h] = f["s_new"][i_h]
                    tm_ref[c, h] = f["t"][i_h]
                    o = f["out"][i_h]
                    r = lax.rsqrt(jnp.mean(o * o, axis=-1, keepdims=True) + RMS_EPS)
                    mix_ref[rows, hc] = o * r * nw * _silu(p_ref[rows, GDN_Z0 + h * GDN_HEAD:GDN_Z0 + (h + 1) * GDN_HEAD])
            return carry

        lax.fori_loop(0, cpb, chunk, 0)

    vec = lambda n: pl.BlockSpec((1, n), lambda i: (0, 0))
    mix, states, tmats = pl.pallas_call(
        body, name=name, grid=(nb,),
        in_specs=[pl.BlockSpec((GDN_ROWS, GDN_IN_PAD), lambda i: (i, 0)),
                  pl.BlockSpec((SUBLANES, GDN_IN_PAD), lambda i: (jnp.maximum(i * hb - 1, 0), 0)),
                  pl.BlockSpec((SUBLANES, GDN_CONV_DIM), lambda i: (0, 0)),
                  vec(LANES), vec(LANES), vec(GDN_HEAD)],
        out_specs=[pl.BlockSpec((GDN_ROWS, GDN_V), lambda i: (i, 0)),
                   pl.BlockSpec((cpb, GDN_VH, GDN_HEAD, GDN_HEAD), lambda i: (i, 0, 0, 0)),
                   pl.BlockSpec((cpb, GDN_VH, CHUNK, CHUNK), lambda i: (i, 0, 0, 0))],
        out_shape=[jax.ShapeDtypeStruct((rows_total, GDN_V), F32),
                   jax.ShapeDtypeStruct((rows_total // CHUNK, GDN_VH, GDN_HEAD, GDN_HEAD), F32),
                   jax.ShapeDtypeStruct((rows_total // CHUNK, GDN_VH, CHUNK, CHUNK), F32)],
        scratch_shapes=[pltpu.VMEM((GDN_ROWS + SUBLANES, GDN_CONV_DIM), F32),
                        pltpu.VMEM((GDN_ROWS, GDN_CONV_DIM), F32),
                        pltpu.VMEM((GDN_ROWS, LANES), F32),
                        pltpu.VMEM((GDN_ROWS, LANES), F32),
                        pltpu.VMEM((GDN_VH, GDN_HEAD, GDN_HEAD), F32),
                        pltpu.VMEM((LANES, CHUNK), F32)],
        compiler_params=_cparams(("arbitrary",)),
    )(proj, proj, _pad_rows(conv_w), _gdn_lane_params(a_log), _gdn_lane_params(dt_bias), norm_w.reshape(1, -1))
    return mix, (states, tmats)


def gdn_bwd(proj, conv_w, a_log, dt_bias, norm_w, saved, dmix, *, name):
    states, tmats = saved
    rows_total = proj.shape[0]
    nb = rows_total // GDN_ROWS
    hb = GDN_ROWS // SUBLANES
    cpb = GDN_ROWS // CHUNK

    def body(p_ref, halo_ref, cw_ref, alog_ref, dtb_ref, nw_ref, st_ref, tm_ref, dm_ref,
             dp_ref, dcw_ref, dalog_ref, ddtb_ref, dnw_ref,
             ext_ref, qkv_ref, beta_ref, g_ref, ds_ref, gct_ref, dext_ref, dgc_ref, dgct_ref, dbeta_ref):
        i = pl.program_id(0)
        blk = nb - 1 - i

        @pl.when(i == 0)
        def _():
            ds_ref[...] = jnp.zeros_like(ds_ref)
            dext_ref[GDN_ROWS:, :] = jnp.zeros((SUBLANES, GDN_CONV_DIM), F32)
            for r in (dcw_ref, dalog_ref, ddtb_ref, dnw_ref):
                r[...] = jnp.zeros_like(r)

        _gdn_prologue(blk, p_ref, halo_ref, cw_ref, alog_ref, dtb_ref, ext_ref, qkv_ref, beta_ref, g_ref)
        ltri, utri, eye_l, eye_c = _tri(CHUNK), _tri(CHUNK, lower=False), _eye(LANES), _eye(CHUNK)
        causal = _iota((CHUNK, CHUNK), 1) <= _iota((CHUNK, CHUNK), 0)
        strict = _iota((CHUNK, CHUNK), 1) < _iota((CHUNK, CHUNK), 0)
        lane = _iota((CHUNK, LANES), 1)
        is_last = _iota((CHUNK, 1), 0) == CHUNK - 1
        nw = nw_ref[...]

        def chunk(cc, carry):
            c = cpb - 1 - cc
            rows = pl.ds(pl.multiple_of(c * CHUNK, CHUNK), CHUNK)
            g_c = g_ref[rows, :]
            gc = _sel(ltri,g_c)
            gct_ref[...] = _sel_nt(eye_l,gc)
            glast_row = _row(gc, CHUNK - 1)
            beta_c = beta_ref[rows, :]
            dgc_ref[...] = jnp.zeros_like(dgc_ref)
            dgct_ref[...] = jnp.zeros_like(dgct_ref)
            dbeta_ref[...] = jnp.zeros_like(dbeta_ref)
            for h0 in range(0, GDN_VH, GDN_GROUP):
                hs = list(range(h0, h0 + GDN_GROUP))
                hqs = list(range(h0 // 2, (h0 + GDN_GROUP) // 2))
                qs = {hq: qkv_ref[rows, hq * GDN_HEAD:(hq + 1) * GDN_HEAD] for hq in hqs}
                ks = {hq: qkv_ref[rows, GDN_K0 + hq * GDN_HEAD:GDN_K0 + (hq + 1) * GDN_HEAD] for hq in hqs}
                kks = {hq: _dot_nt(ks[hq], ks[hq]) for hq in hqs}
                qks = {hq: _dot_nt(qs[hq], ks[hq]) for hq in hqs}
                q = [qs[h // 2] for h in hs]
                k = [ks[h // 2] for h in hs]
                v = [qkv_ref[rows, GDN_V0 + h * GDN_HEAD:GDN_V0 + (h + 1) * GDN_HEAD] for h in hs]
                s = [st_ref[c, h] for h in hs]
                bcol = [_col(beta_c, h) for h in hs]
                f = _gdn_heads_fwd(q, k, v, [kks[h // 2] for h in hs], [qks[h // 2] for h in hs],
                                   [_col(gc, GDN_GL + h) for h in hs], [gct_ref[GDN_GL + h:GDN_GL + h + 1, :] for h in hs],
                                   [_col(glast_row, GDN_GL + h) for h in hs], bcol, s, causal, strict, eye_c,
                                   t=[tm_ref[c, h] for h in hs])
                do = []
                for i_h, h in enumerate(hs):
                    zc = slice(GDN_Z0 + h * GDN_HEAD, GDN_Z0 + (h + 1) * GDN_HEAD)
                    o = f["out"][i_h]
                    z = p_ref[rows, zc]
                    sz = _silu(z)
                    r = lax.rsqrt(jnp.mean(o * o, axis=-1, keepdims=True) + RMS_EPS)
                    on = o * r
                    dm = dm_ref[rows, h * GDN_HEAD:(h + 1) * GDN_HEAD]
                    dnw_ref[...] += jnp.sum(dm * on * sz, axis=0, keepdims=True)
                    d_on = dm * nw * sz
                    dp_ref[rows, zc] = dm * on * nw * _dsilu(z)
                    do.append(r * (d_on - on * jnp.mean(d_on * on, axis=-1, keepdims=True)))
                ds_n = [ds_ref[h] for h in hs]
                dv1 = _each(_dot_tn, f["attn"], do)
                dv2 = _each(_dot, f["kt"], ds_n)
                d_vnew = _each(lambda a_, b_: a_ + b_, dv1, dv2)
                d_attn = _each(lambda do_, vn_: jnp.where(causal, _dot_nt(do_, vn_), 0.0), do, f["v_new"])
                d_qd = _each(_dot_nt, do, s)
                t1 = _each(_dot_tn, f["qd"], do)
                t2 = _each(_dot_tn, f["w"], d_vnew)
                for h, a_, cd_, dsn_, b_ in zip(hs, t1, f["cd"], ds_n, t2):
                    ds_ref[h] = a_ + cd_ * dsn_ - b_
                d_cd = _each(lambda s_, dsn_: jnp.sum(jnp.sum(s_ * dsn_, axis=1, keepdims=True), axis=0, keepdims=True), s, ds_n)
                d_kt = _each(_dot_nt, f["v_new"], ds_n)
                d_w = _each(lambda dv_, s_: -_dot_nt(dv_, s_), d_vnew, s)
                d_rhs_u = _each(lambda t_, d_: _dot_x3(t_, d_, TN), f["t"], d_vnew)
                d_rhs_w = _each(lambda t_, d_: _dot_x3(t_, d_, TN), f["t"], d_w)
                m1 = _each(_dot_nt, d_rhs_u, f["u"])
                m2 = _each(_dot_nt, d_rhs_w, f["w"])
                da = _each(lambda a_, b_: -jnp.where(strict, a_ + b_, 0.0), m1, m2)
                dmm = _each(lambda a_, b_: a_ * b_, da, f["decay"])
                em = _each(lambda da_, a_, dat_, at_: da_ * a_ + dat_ * at_, da, f["a"], d_attn, f["attn"])
                x1 = _each(_dot, dmm, k)
                d_kb = _each(lambda x_, drw_, e_: x_ + drw_ * e_, x1, d_rhs_w, f["egc"])
                dk1 = _each(_dot_tn, dmm, f["kb"])
                dpm = _each(lambda a_, b_: a_ * b_, d_attn, f["decay"])
                dq1 = _each(_dot, dpm, k)
                dq = _each(lambda x_, dqd_, e_: x_ + dqd_ * e_, dq1, d_qd, f["egc"])
                dk2 = _each(_dot_tn, dpm, q)
                dk = _each(lambda a_, b_, dkb_, bc_, dkt_, et_: a_ + b_ + dkb_ * bc_ + dkt_ * et_,
                           dk1, dk2, d_kb, bcol, d_kt, f["etail"])
                for i_h, h in enumerate(hs):
                    tmp = jnp.sum(d_kt[i_h] * f["kt"][i_h], axis=1, keepdims=True)
                    d_gcol = (jnp.sum(em[i_h], axis=1, keepdims=True)
                              + jnp.sum(d_rhs_w[i_h] * f["rhs_w"][i_h], axis=1, keepdims=True)
                              + jnp.sum(d_qd[i_h] * f["qd"][i_h], axis=1, keepdims=True) - tmp)
                    d_glast = jnp.sum(tmp, axis=0, keepdims=True) + d_cd[i_h] * f["cd"][i_h]
                    d_gcol = jnp.where(is_last, d_gcol + d_glast, d_gcol)
                    d_beta = (jnp.sum(d_rhs_u[i_h] * v[i_h], axis=1, keepdims=True)
                              + jnp.sum(d_kb[i_h] * k[i_h], axis=1, keepdims=True))
                    dgc_ref[...] += jnp.where(lane == GDN_GL + h, d_gcol, 0.0)
                    dgct_ref[GDN_GL + h:GDN_GL + h + 1, :] = jnp.sum(em[i_h], axis=0, keepdims=True)
                    dbeta_ref[...] += jnp.where(lane == h, d_beta, 0.0)
                    dext_ref[rows, GDN_V0 + h * GDN_HEAD:GDN_V0 + (h + 1) * GDN_HEAD] = d_rhs_u[i_h] * bcol[i_h]
                for hq in hqs:
                    i0 = 2 * hq - h0
                    dext_ref[rows, hq * GDN_HEAD:(hq + 1) * GDN_HEAD] = dq[i0] + dq[i0 + 1]
                    dext_ref[rows, GDN_K0 + hq * GDN_HEAD:GDN_K0 + (hq + 1) * GDN_HEAD] = dk[i0] + dk[i0 + 1]
            d_gc = dgc_ref[...] - _sel_nt(eye_c,dgct_ref[...])
            dg = _sel(utri,d_gc)
            ba = p_ref[rows, GDN_BA0:GDN_BA0 + LANES]
            d_sp = dg * -jnp.exp(alog_ref[...])
            d_araw = d_sp * _sigmoid(ba + dtb_ref[...])
            d_araw = jnp.where((lane >= GDN_GL) & (lane < GDN_GL + GDN_VH), d_araw, 0.0)
            dalog_ref[...] += jnp.sum(dg * g_c, axis=0, keepdims=True)
            ddtb_ref[...] += jnp.sum(d_araw, axis=0, keepdims=True)
            d_braw = jnp.where(lane < GDN_VH, dbeta_ref[...] * beta_c * (1.0 - beta_c), 0.0)
            dp_ref[rows, GDN_BA0:GDN_BA0 + LANES] = d_braw + d_araw
            return carry

        lax.fori_loop(0, cpb, chunk, 0)
        w = cw_ref[...]
        for hh in range(GDN_CONV_DIM // GDN_HEAD):
            cols = slice(hh * GDN_HEAD, (hh + 1) * GDN_HEAD)
            pre = None
            for j in range(GDN_CONV):
                off = SUBLANES - (GDN_CONV - 1) + j
                term = ext_ref[off:off + GDN_ROWS, cols] * w[j:j + 1, cols]
                pre = term if pre is None else pre + term
            d_act = dext_ref[0:GDN_ROWS, cols]
            if hh < 2 * GDN_QKH:
                a = _silu(pre)
                r = lax.rsqrt(jnp.sum(a * a, axis=-1, keepdims=True) + L2_EPS)
                ah = a * r
                if hh < GDN_QKH:
                    d_act = d_act * GDN_SCALE
                d_act = r * (d_act - ah * jnp.sum(d_act * ah, axis=-1, keepdims=True))
            d_pre = d_act * _dsilu(pre)
            dext_ref[0:GDN_ROWS, cols] = d_pre
            for j in range(GDN_CONV):
                off = SUBLANES - (GDN_CONV - 1) + j
                dcw_ref[j:j + 1, cols] += jnp.sum(ext_ref[off:off + GDN_ROWS, cols] * d_pre, axis=0, keepdims=True)
        dp_ref[:, 0:GDN_CONV_DIM] = _conv_dgrad_from_ext(dext_ref, w, GDN_CONV, GDN_ROWS)
        dext_ref[GDN_ROWS:, :] = dext_ref[0:SUBLANES, :]

    vec = lambda n: pl.BlockSpec((1, n), lambda i: (0, 0))
    outs = pl.pallas_call(
        body, name=name, grid=(nb,),
        in_specs=[pl.BlockSpec((GDN_ROWS, GDN_IN_PAD), lambda i: (nb - 1 - i, 0)),
                  pl.BlockSpec((SUBLANES, GDN_IN_PAD), lambda i: (jnp.maximum((nb - 1 - i) * hb - 1, 0), 0)),
                  pl.BlockSpec((SUBLANES, GDN_CONV_DIM), lambda i: (0, 0)),
                  vec(LANES), vec(LANES), vec(GDN_HEAD),
                  pl.BlockSpec((cpb, GDN_VH, GDN_HEAD, GDN_HEAD), lambda i: (nb - 1 - i, 0, 0, 0)),
                  pl.BlockSpec((cpb, GDN_VH, CHUNK, CHUNK), lambda i: (nb - 1 - i, 0, 0, 0)),
                  pl.BlockSpec((GDN_ROWS, GDN_V), lambda i: (nb - 1 - i, 0))],
        out_specs=[pl.BlockSpec((GDN_ROWS, GDN_IN_PAD), lambda i: (nb - 1 - i, 0)),
                   pl.BlockSpec((SUBLANES, GDN_CONV_DIM), lambda i: (0, 0)),
                   vec(LANES), vec(LANES), vec(GDN_HEAD)],
        out_shape=[jax.ShapeDtypeStruct((rows_total, GDN_IN_PAD), F32),
                   jax.ShapeDtypeStruct((SUBLANES, GDN_CONV_DIM), F32),
                   jax.ShapeDtypeStruct((1, LANES), F32), jax.ShapeDtypeStruct((1, LANES), F32),
                   jax.ShapeDtypeStruct((1, GDN_HEAD), F32)],
        scratch_shapes=[pltpu.VMEM((GDN_ROWS + SUBLANES, GDN_CONV_DIM), F32),
                        pltpu.VMEM((GDN_ROWS, GDN_CONV_DIM), F32),
                        pltpu.VMEM((GDN_ROWS, LANES), F32),
                        pltpu.VMEM((GDN_ROWS, LANES), F32),
                        pltpu.VMEM((GDN_VH, GDN_HEAD, GDN_HEAD), F32),
                        pltpu.VMEM((LANES, CHUNK), F32),
                        pltpu.VMEM((GDN_ROWS + SUBLANES, GDN_CONV_DIM), F32),
                        pltpu.VMEM((CHUNK, LANES), F32),
                        pltpu.VMEM((LANES, CHUNK), F32),
                        pltpu.VMEM((CHUNK, LANES), F32)],
        compiler_params=_cparams(("arbitrary",)),
    )(proj, proj, _pad_rows(conv_w), _gdn_lane_params(a_log), _gdn_lane_params(dt_bias), norm_w.reshape(1, -1),
      states, tmats, dmix)
    dproj, dcw, dalog, ddtb, dnw = outs
    return dproj, [dcw[:GDN_CONV], dalog[0, GDN_GL:GDN_GL + GDN_VH], ddtb[0, GDN_GL:GDN_GL + GDN_VH], dnw[0]]


LOCAL_SPLITS = 16


def exchange(src, axes, *, gather, name, splits=1):
    n = 2 ** len(axes)
    piece_shape = src.shape if gather else src.shape[1:]
    rows = piece_shape[0] // splits
    assert rows * splits == piece_shape[0]

    local_rows = piece_shape[0] // LOCAL_SPLITS
    assert local_rows * LOCAL_SPLITS == piece_shape[0]

    def body(src_ref, out_ref, send_sems, recv_sems, local_sems):
        coords = {a: lax.axis_index(a) for a in MESH_AXES}
        me = 0
        for a in axes:
            me = me * 2 + coords[a]

        def piece(j):
            return src_ref if gather else src_ref.at[j]

        locals_ = []
        for part in range(LOCAL_SPLITS):
            sl = pl.ds(part * local_rows, local_rows)
            cp = pltpu.make_async_copy(piece(me).at[sl], out_ref.at[me].at[sl], local_sems.at[part])
            cp.start()
            locals_.append(cp)
        copies = []
        for k in range(1, n):
            peer = dict(coords)
            for bit, a in enumerate(reversed(axes)):
                if (k >> bit) & 1:
                    peer[a] = 1 - peer[a]
            for part in range(splits):
                sem = (k - 1) * splits + part
                sl = pl.ds(part * rows, rows)
                cp = pltpu.make_async_remote_copy(
                    src_ref=piece(jnp.bitwise_xor(me, k)).at[sl], dst_ref=out_ref.at[me].at[sl],
                    send_sem=send_sems.at[sem], recv_sem=recv_sems.at[sem],
                    device_id=tuple(peer[a] for a in MESH_AXES), device_id_type=pl.DeviceIdType.MESH)
                cp.start()
                copies.append(cp)
        for cp in copies:
            cp.wait()
        for cp in locals_:
            cp.wait()

    hbm = pl.BlockSpec(memory_space=pl.ANY)
    n_sems = (n - 1) * splits
    return pl.pallas_call(
        body, name=name, in_specs=[hbm], out_specs=hbm,
        out_shape=jax.ShapeDtypeStruct((n,) + tuple(piece_shape), src.dtype),
        scratch_shapes=[pltpu.SemaphoreType.DMA((n_sems,)), pltpu.SemaphoreType.DMA((n_sems,)),
                        pltpu.SemaphoreType.DMA((LOCAL_SPLITS,))],
    )(src)


def sum_slots(buf, *, name):
    n, rows, cols = buf.shape
    tr = _pick(rows, (512, 256, 128))

    def body(b_ref, o_ref):
        acc = b_ref[0]
        for j in range(1, n):
            acc = acc + b_ref[j]
        o_ref[...] = acc

    return pl.pallas_call(
        body, name=name, grid=(rows // tr,), in_specs=[pl.BlockSpec((n, tr, cols), lambda i: (0, i, 0))],
        out_specs=pl.BlockSpec((tr, cols), lambda i: (i, 0)), out_shape=jax.ShapeDtypeStruct((rows, cols), F32),
        compiler_params=_cparams(("parallel",)),
    )(buf)


def adamw(w, g, m, v, *, name):
    shape = w.shape
    cols = shape[-1]
    rows = _size(shape) // cols
    w, g, m, v = (t.reshape(rows, cols) for t in (w, g, m, v))
    tr = 256 if rows % 256 == 0 else rows

    def body(w_ref, g_ref, m_ref, v_ref, d_ref, mo_ref, vo_ref):
        gv = g_ref[...]
        mn = ADAM_B1 * m_ref[...] + (1.0 - ADAM_B1) * gv
        vn = ADAM_B2 * v_ref[...] + (1.0 - ADAM_B2) * (gv * gv)
        m_hat = mn / (1.0 - ADAM_B1 ** ADAM_STEP)
        v_hat = vn / (1.0 - ADAM_B2 ** ADAM_STEP)
        d_ref[...] = -ADAM_LR * (m_hat / (jnp.sqrt(v_hat) + ADAM_EPS) + ADAM_WD * w_ref[...])
        mo_ref[...] = mn
        vo_ref[...] = vn

    blk = pl.BlockSpec((tr, cols), lambda i: (i, 0))
    shp = jax.ShapeDtypeStruct((rows, cols), F32)
    outs = pl.pallas_call(
        body, name=name, grid=(rows // tr,), in_specs=[blk] * 4, out_specs=[blk] * 3, out_shape=[shp] * 3,
        compiler_params=_cparams(("parallel",)),
    )(w, g, m, v)
    return [o.reshape(shape) for o in outs]


N_SHARDS = 4
FLAT_COLS = 1024
W_SPECS = (
    ("gdn_w_in", (2, 1024, 6176), 2), ("gdn_conv_w", (2, 4, 4096), 2), ("gdn_a_log", (2, 16), None),
    ("gdn_dt_bias", (2, 16), None), ("gdn_norm_w", (2, 128), None), ("gdn_w_out", (2, 2048, 1024), 1),
    ("sc_w_in", (1, 1024, 8192), 2), ("sc_conv_w", (1, 3, 2048), 2), ("sc_w_out", (1, 2048, 1024), 1),
    ("ssd_w_in", (1, 1024, 5152), 2), ("ssd_conv_w", (1, 4, 3072), 2), ("ssd_conv_b", (1, 3072), 1),
    ("ssd_a_log", (1, 32), None), ("ssd_dt_bias", (1, 32), None), ("ssd_d_skip", (1, 32), None),
    ("ssd_norm_w", (1, 2048), 1), ("ssd_w_out", (1, 2048, 1024), 1), ("ln_g", (4, 1024), None), ("ln_b", (4, 1024), None),
)


def _local_shape(shape, axis):
    return shape if axis is None else tuple(d // N_SHARDS if i == axis else d for i, d in enumerate(shape))


def _size(shape):
    n = 1
    for d in shape:
        n *= d
    return n


FLAT_USED = sum(_size(_local_shape(s, a)) for _, s, a in W_SPECS)
FLAT_ROWS = -(-FLAT_USED // (FLAT_COLS * 512)) * 512
FLAT_HALF = FLAT_ROWS // 2
D2D_SPLITS = 8


def _pack(pieces):
    flat = jnp.concatenate([p.reshape(-1) for p in pieces] + [jnp.zeros((FLAT_ROWS * FLAT_COLS - FLAT_USED,), F32)])
    return flat.reshape(FLAT_ROWS, FLAT_COLS)


def _unpack(flat):
    flat = flat.reshape(-1)
    out, off = [], 0
    for _, shape, axis in W_SPECS:
        ls = _local_shape(shape, axis)
        out.append(flat[off:off + _size(ls)].reshape(ls))
        off += _size(ls)
    return out


def _shard_of(full, axis, s):
    if axis is None:
        return full
    n = full.shape[axis] // N_SHARDS
    return lax.slice_in_dim(full, s * n, (s + 1) * n, axis=axis)


def _adamw_all(weights, grads_flat, moms, vels):
    grads = _unpack(grads_flat)
    steps = [adamw(w, g, m, v, name="adamw") for w, g, m, v in zip(weights, grads, moms, vels)]
    return grads, [s[0] for s in steps], [s[1] for s in steps], [s[2] for s in steps]


SPLIT_ROWS = 128


def shard_split(w, n_real, *, name):
    rows, n_pad = w.shape
    ns = n_real // N_SHARDS

    def body(w_ref, o_ref):
        for s in range(N_SHARDS):
            o_ref[s] = w_ref[:, s * ns:(s + 1) * ns]

    return pl.pallas_call(
        body, name=name, grid=(rows // SPLIT_ROWS,),
        in_specs=[pl.BlockSpec((SPLIT_ROWS, n_pad), lambda i: (i, 0))],
        out_specs=pl.BlockSpec((N_SHARDS, SPLIT_ROWS, ns), lambda i: (0, i, 0)),
        out_shape=jax.ShapeDtypeStruct((N_SHARDS, rows, ns), F32), compiler_params=_cparams(("parallel",)),
    )(w)


def shard_merge(pieces, n_pad, *, name):
    _, rows, ns = pieces.shape
    n_real = ns * N_SHARDS

    def body(p_ref, o_ref):
        for s in range(N_SHARDS):
            o_ref[:, s * ns:(s + 1) * ns] = p_ref[s].astype(o_ref.dtype)
        if n_pad > n_real:
            o_ref[:, n_real:] = jnp.zeros((SPLIT_ROWS, n_pad - n_real), o_ref.dtype)

    return pl.pallas_call(
        body, name=name, grid=(rows // SPLIT_ROWS,),
        in_specs=[pl.BlockSpec((N_SHARDS, SPLIT_ROWS, ns), lambda i: (0, i, 0))],
        out_specs=pl.BlockSpec((SPLIT_ROWS, n_pad), lambda i: (i, 0)),
        out_shape=jax.ShapeDtypeStruct((rows, n_pad), MXU_DTYPE), compiler_params=_cparams(("parallel",)),
    )(pieces)


def _reduce_scatter(full_grads):
    def shard(g, spec, s):
        _, shape, axis = spec
        return g[:, s] if g.ndim == len(shape) + 1 else _shard_of(g, axis, s)

    by_shard = jnp.stack([_pack([shard(g, spec, s) for g, spec in zip(full_grads, W_SPECS)])
                          for s in range(N_SHARDS)])
    by_half = by_shard.reshape(N_SHARDS, 2, FLAT_HALF, FLAT_COLS).transpose(1, 0, 2, 3)
    by_half = by_half.reshape(2, N_SHARDS * FLAT_HALF, FLAT_COLS)
    pair = exchange(by_half, ("c",), gather=False, name="rs_pair", splits=D2D_SPLITS)
    pair_sum = sum_slots(pair, name="rs_pair_sum")
    chips = exchange(pair_sum.reshape(N_SHARDS, FLAT_HALF, FLAT_COLS), ("x", "y"), gather=False, name="rs_chips")
    half = sum_slots(chips, name="rs_chip_sum")
    return exchange(half, ("c",), gather=True, name="rs_halves", splits=D2D_SPLITS).reshape(FLAT_ROWS, FLAT_COLS)


def _gather_weights(local_weights):
    gathered = exchange(_pack(local_weights), ("x", "y"), gather=True, name="gather_weights")
    per_shard = [_unpack(gathered[s]) for s in range(N_SHARDS)]
    full = []
    for i, (wname, shape, axis) in enumerate(W_SPECS):
        if axis is None:
            full.append(local_weights[i])
        elif wname in W_IN_PAD:
            pieces = jnp.stack([per_shard[s][i] for s in range(N_SHARDS)], axis=1)
            full.append([shard_merge(pieces[j], W_IN_PAD[wname], name="merge_" + wname) for j in range(shape[0])])
        else:
            full.append(jnp.concatenate([per_shard[s][i] for s in range(N_SHARDS)], axis=axis))
    return full


W_IN_PAD = {"gdn_w_in": GDN_IN_PAD, "sc_w_in": SC_IN, "ssd_w_in": SSD_IN_PAD}


def kernel(x, gdn_w_in, gdn_conv_w, gdn_a_log, gdn_dt_bias, gdn_norm_w, gdn_w_out, sc_w_in, sc_conv_w, sc_w_out, ssd_w_in, ssd_conv_w, ssd_conv_b, ssd_a_log, ssd_dt_bias, ssd_d_skip, ssd_norm_w, ssd_w_out, ln_g, ln_b, loss_target, m_gdn_w_in, m_gdn_conv_w, m_gdn_a_log, m_gdn_dt_bias, m_gdn_norm_w, m_gdn_w_out, m_sc_w_in, m_sc_conv_w, m_sc_w_out, m_ssd_w_in, m_ssd_conv_w, m_ssd_conv_b, m_ssd_a_log, m_ssd_dt_bias, m_ssd_d_skip, m_ssd_norm_w, m_ssd_w_out, m_ln_g, m_ln_b, v_gdn_w_in, v_gdn_conv_w, v_gdn_a_log, v_gdn_dt_bias, v_gdn_norm_w, v_gdn_w_out, v_sc_w_in, v_sc_conv_w, v_sc_w_out, v_ssd_w_in, v_ssd_conv_w, v_ssd_conv_b, v_ssd_a_log, v_ssd_dt_bias, v_ssd_d_skip, v_ssd_norm_w, v_ssd_w_out, v_ln_g, v_ln_b):
    weights = [gdn_w_in, gdn_conv_w, gdn_a_log, gdn_dt_bias, gdn_norm_w, gdn_w_out, sc_w_in, sc_conv_w, sc_w_out,
               ssd_w_in, ssd_conv_w, ssd_conv_b, ssd_a_log, ssd_dt_bias, ssd_d_skip, ssd_norm_w, ssd_w_out, ln_g, ln_b]
    moms = [m_gdn_w_in, m_gdn_conv_w, m_gdn_a_log, m_gdn_dt_bias, m_gdn_norm_w, m_gdn_w_out, m_sc_w_in, m_sc_conv_w,
            m_sc_w_out, m_ssd_w_in, m_ssd_conv_w, m_ssd_conv_b, m_ssd_a_log, m_ssd_dt_bias, m_ssd_d_skip, m_ssd_norm_w,
            m_ssd_w_out, m_ln_g, m_ln_b]
    vels = [v_gdn_w_in, v_gdn_conv_w, v_gdn_a_log, v_gdn_dt_bias, v_gdn_norm_w, v_gdn_w_out, v_sc_w_in, v_sc_conv_w,
            v_sc_w_out, v_ssd_w_in, v_ssd_conv_w, v_ssd_conv_b, v_ssd_a_log, v_ssd_dt_bias, v_ssd_d_skip, v_ssd_norm_w,
            v_ssd_w_out, v_ln_g, v_ln_b]
    full = dict(zip([n for n, _, _ in W_SPECS], _gather_weights(weights)))
    x0 = x[0]
    target = loss_target[0]

    layers = (("gdn", 0, GDN_IN_PAD, GDN_IN), ("sc", 0, SC_IN, SC_IN), ("ssd", 0, SSD_IN_PAD, SSD_IN), ("gdn", 1, GDN_IN_PAD, GDN_IN))

    def params(kind, j):
        if kind == "gdn":
            return [full["gdn_conv_w"][j], full["gdn_a_log"][j], full["gdn_dt_bias"][j], full["gdn_norm_w"][j]]
        if kind == "sc":
            return [full["sc_conv_w"][j]]
        return [full["ssd_conv_w"][j], full["ssd_conv_b"][j], full["ssd_a_log"][j], full["ssd_dt_bias"][j],
                full["ssd_d_skip"][j], full["ssd_norm_w"][j]]

    xs, saved = [x0], []
    for i, (kind, j, n_pad, _) in enumerate(layers):
        w_in = full[kind + "_w_in"][j]
        w_out = full[kind + "_w_out"][j].astype(MXU_DTYPE)
        proj = matmul(xs[i], w_in, name=kind + "_proj")
        if kind == "gdn":
            mix, states = gdn_fwd(proj, *params(kind, j), name="gdn_fwd")
        elif kind == "sc":
            mix, states = sc_fwd(proj, *params(kind, j), name="sc_fwd"), None
        else:
            mix, states = ssd_fwd(proj, *params(kind, j), name="ssd_fwd")
        y = matmul(mix, w_out, name=kind + "_out")
        saved.append((w_in, w_out, proj, mix, states, y))
        if i + 1 < DEPTH:
            xs.append(ln_fwd(xs[i], y, full["ln_g"][i], full["ln_b"][i], name="ln_fwd"))

    grads = {n: [None] * s[0] for n, s, _ in W_SPECS}
    dr, dg, db, loss_rows = ln_bwd(xs[DEPTH - 1], saved[DEPTH - 1][5], full["ln_g"][DEPTH - 1], b=full["ln_b"][DEPTH - 1],
                                   target=target, name="ln_bwd_loss")
    dx = None
    for i in reversed(range(DEPTH)):
        kind, j, _, n_in = layers[i]
        w_in, w_out, proj, mix, states, _ = saved[i]
        grads["ln_g"][i], grads["ln_b"][i] = dg[0], db[0]
        dmix = matmul(dr, w_out, tb=True, name=kind + "_dmix")
        grads[kind + "_w_out"][j] = matmul(mix, dr, ta=True, name=kind + "_dw_out")
        if kind == "gdn":
            dproj, (dcw, dalog, ddtb, dnw) = gdn_bwd(proj, *params(kind, j), states, dmix, name="gdn_bwd")
            grads["gdn_conv_w"][j], grads["gdn_a_log"][j], grads["gdn_dt_bias"][j], grads["gdn_norm_w"][j] = dcw, dalog, ddtb, dnw
        elif kind == "sc":
            dproj, dcw = sc_bwd(proj, *params(kind, j), dmix, name="sc_bwd")
            grads["sc_conv_w"][j] = dcw[:SC_CONV]
        else:
            dproj, (dcw, dcb, dalog, ddtb, ddsk, dnw) = ssd_bwd(proj, *params(kind, j), states, dmix, name="ssd_bwd")
            grads["ssd_conv_w"][j], grads["ssd_conv_b"][j], grads["ssd_a_log"][j] = dcw, dcb, dalog
            grads["ssd_dt_bias"][j], grads["ssd_d_skip"][j], grads["ssd_norm_w"][j] = ddtb, ddsk, dnw
        grads[kind + "_w_in"][j] = shard_split(matmul(xs[i], dproj, ta=True, name=kind + "_dw_in"), n_in, name="split_" + kind)
        dx = matmul(dproj, w_in, tb=True, add=dr, add_scale=ALPHA, name=kind + "_dx")
        if i > 0:
            dr, dg, db = ln_bwd(xs[i - 1], saved[i - 1][5], full["ln_g"][i - 1], dx, name="ln_bwd")

    full_grads = [jnp.stack(grads[n]) for n, _, _ in W_SPECS]
    grads_flat = _reduce_scatter(full_grads)
    g_out, d_out, m_out, v_out = _adamw_all(weights, grads_flat, moms, vels)
    loss = lax.psum(loss_rows[0, 0], MESH_AXES)
    return (loss, dx[None], *g_out, *d_out, *m_out, *v_out)
```

```python
import functools

import jax
import jax.numpy as jnp
from jax import lax
from jax.experimental import pallas as pl
from jax.experimental.pallas import tpu as pltpu

F32 = jnp.float32
MXU_DTYPE = jnp.bfloat16

D_MODEL = 1024
DEPTH = 4
D_INNER = 2048
CHUNK = 64
LANES = 128
SUBLANES = 8
VMEM_LIMIT = 56 * 1024 * 1024

GDN_HEAD = 128
GDN_VH = 16
GDN_QKH = 8
GDN_QK = 1024
GDN_V = 2048
GDN_CONV = 4
GDN_CONV_DIM = 4096
GDN_IN = 6176
GDN_IN_PAD = 6272

SC_W = 2048
SC_CONV = 3
SC_IN = 8192

SSD_P = 64
SSD_H = 32
SSD_G = 4
SSD_S = 128
SSD_CONV = 4
SSD_CONV_DIM = 3072
SSD_IN = 5152
SSD_IN_PAD = 5376

ALPHA = (2 * DEPTH) ** 0.25
RMS_EPS = 1e-6
LN_EPS = 1e-5
L2_EPS = 1e-6

ADAM_LR = 0.001
ADAM_B1 = 0.9
ADAM_B2 = 0.999
ADAM_EPS = 1e-08
ADAM_WD = 0.01
ADAM_STEP = 10

MESH_AXES = ("x", "y", "c")


def _cparams(sem):
    return pltpu.CompilerParams(dimension_semantics=sem, vmem_limit_bytes=VMEM_LIMIT)


def _pick(n, prefs):
    for p in prefs:
        if n % p == 0:
            return p
    return n


def _dot(a, b, dims=(((1,), (0,)), ((), ()))):
    return lax.dot_general(a.astype(MXU_DTYPE), b.astype(MXU_DTYPE), dims, preferred_element_type=F32)


def _dot_nt(a, b):
    return _dot(a, b, (((1,), (1,)), ((), ())))


def _dot_tn(a, b):
    return _dot(a, b, (((0,), (0,)), ((), ())))


NN = (((1,), (0,)), ((), ()))
NT = (((1,), (1,)), ((), ()))
TN = (((0,), (0,)), ((), ()))


def _mxu(a, b, dims):
    return lax.dot_general(a, b, dims, preferred_element_type=F32)


def _split(x, pieces):
    out, r = [], x
    for i in range(pieces):
        p = r.astype(jnp.bfloat16)
        out.append(p)
        if i + 1 < pieces:
            r = r - p.astype(F32)
    return out


def _dot_x3(a, b, dims=NN):
    (ah, al), (bh, bl) = _split(a, 2), _split(b, 2)
    return (_mxu(al, bh, dims) + _mxu(ah, bl, dims)) + _mxu(ah, bh, dims)


def _sel(m, x, dims=NN):
    mb = m.astype(jnp.bfloat16)
    x1, x2, x3 = _split(x, 3)
    return (_mxu(mb, x3, dims) + _mxu(mb, x2, dims)) + _mxu(mb, x1, dims)


def _sel_nt(m, x):
    return _sel(m, x, NT)


def _xsel(x, m, dims=NN):
    mb = m.astype(jnp.bfloat16)
    x1, x2, x3 = _split(x, 3)
    return (_mxu(x3, mb, dims) + _mxu(x2, mb, dims)) + _mxu(x1, mb, dims)


def _xsel_nt(x, m):
    return _xsel(x, m, NT)


def _iota(shape, dim):
    return lax.broadcasted_iota(jnp.int32, shape, dim)


def _sigmoid(x):
    return 1.0 / (1.0 + jnp.exp(-x))


def _silu(x):
    return x * _sigmoid(x)


def _dsilu(x):
    s = _sigmoid(x)
    return s * (1.0 + x * (1.0 - s))


def _softplus(x):
    return jnp.maximum(x, 0.0) + jnp.log(1.0 + jnp.exp(-jnp.abs(x)))


def matmul(a, b, *, ta=False, tb=False, add=None, add_scale=1.0, name):
    if ta:
        kdim, m = a.shape
    else:
        m, kdim = a.shape
    n = b.shape[0] if tb else b.shape[1]
    assert (b.shape[1] if tb else b.shape[0]) == kdim
    tm = _pick(m, (1024, 896, 768, 512)) if ta else _pick(m, (1024, 512, 256, 128))
    tn = _pick(n, (1024, 896, 768, 512, 256, 128))
    tk = _pick(kdim, (1024, 512, 256)) if ta else _pick(kdim, (1024, 896, 768, 512))
    nk = kdim // tk
    dims = (((0 if ta else 1,), (1 if tb else 0,)), ((), ()))

    def body(a_ref, b_ref, *rest):
        o_ref = rest[-1]
        k = pl.program_id(2)
        part = _dot(a_ref[...], b_ref[...], dims)

        @pl.when(k == 0)
        def _():
            o_ref[...] = part if add is None else part + add_scale * rest[0][...]

        @pl.when(k > 0)
        def _():
            o_ref[...] += part

    a_spec = pl.BlockSpec((tk, tm), lambda i, j, k: (k, i)) if ta else pl.BlockSpec((tm, tk), lambda i, j, k: (i, k))
    b_spec = pl.BlockSpec((tn, tk), lambda i, j, k: (j, k)) if tb else pl.BlockSpec((tk, tn), lambda i, j, k: (k, j))
    o_spec = pl.BlockSpec((tm, tn), lambda i, j, k: (i, j))
    in_specs = [a_spec, b_spec] + ([] if add is None else [o_spec])
    args = (a, b) + (() if add is None else (add,))
    return pl.pallas_call(
        body, name=name, grid=(m // tm, n // tn, nk), in_specs=in_specs, out_specs=o_spec,
        out_shape=jax.ShapeDtypeStruct((m, n), F32),
        compiler_params=_cparams(("parallel", "parallel", "arbitrary")),
    )(*args)


LN_ROWS = 512


def _ln_stats(x, y):
    r = ALPHA * x + y
    mu = jnp.mean(r, axis=-1, keepdims=True)
    rc = r - mu
    var = jnp.mean(rc * rc, axis=-1, keepdims=True)
    rstd = lax.rsqrt(var + LN_EPS)
    return rc * rstd, rstd


def ln_fwd(x, y, g, b, *, name):
    rows, d = x.shape

    def body(x_ref, y_ref, g_ref, b_ref, o_ref):
        xhat, _ = _ln_stats(x_ref[...], y_ref[...])
        o_ref[...] = xhat * g_ref[...] + b_ref[...]

    blk = pl.BlockSpec((LN_ROWS, d), lambda i: (i, 0))
    vec = pl.BlockSpec((1, d), lambda i: (0, 0))
    return pl.pallas_call(
        body, name=name, grid=(rows // LN_ROWS,), in_specs=[blk, blk, vec, vec], out_specs=blk,
        out_shape=jax.ShapeDtypeStruct((rows, d), F32), compiler_params=_cparams(("parallel",)),
    )(x, y, g.reshape(1, d), b.reshape(1, d))


def ln_bwd(x, y, g, dxn=None, *, b=None, target=None, name):
    rows, d = x.shape
    final = target is not None

    def body(x_ref, y_ref, g_ref, *rest):
        if final:
            b_ref, t_ref, dr_ref, dg_ref, db_ref, loss_ref = rest
        else:
            dxn_ref, dr_ref, dg_ref, db_ref = rest
        i = pl.program_id(0)
        xhat, rstd = _ln_stats(x_ref[...], y_ref[...])
        gv = g_ref[...]
        if final:
            err = xhat * gv + b_ref[...] - t_ref[...]
            dxn_v = err * (1.0 / d)
            part = 0.5 * jnp.sum(jnp.mean(err * err, axis=-1, keepdims=True), axis=0, keepdims=True)
        else:
            dxn_v = dxn_ref[...]
        dxh = dxn_v * gv
        m1 = jnp.mean(dxh, axis=-1, keepdims=True)
        m2 = jnp.mean(dxh * xhat, axis=-1, keepdims=True)
        dr_ref[...] = rstd * (dxh - m1 - xhat * m2)

        @pl.when(i == 0)
        def _():
            dg_ref[...] = jnp.zeros_like(dg_ref)
            db_ref[...] = jnp.zeros_like(db_ref)
            if final:
                loss_ref[...] = jnp.zeros_like(loss_ref)

        dg_ref[...] += jnp.sum(dxn_v * xhat, axis=0, keepdims=True)
        db_ref[...] += jnp.sum(dxn_v, axis=0, keepdims=True)
        if final:
            loss_ref[...] += jnp.broadcast_to(part, loss_ref.shape)

    blk = pl.BlockSpec((LN_ROWS, d), lambda i: (i, 0))
    vec = pl.BlockSpec((1, d), lambda i: (0, 0))
    lvec = pl.BlockSpec((1, LANES), lambda i: (0, 0))
    out_shape = [jax.ShapeDtypeStruct((rows, d), F32), jax.ShapeDtypeStruct((1, d), F32), jax.ShapeDtypeStruct((1, d), F32)]
    out_specs = [blk, vec, vec]
    if final:
        in_specs = [blk, blk, vec, vec, blk]
        args = (x, y, g.reshape(1, d), b.reshape(1, d), target)
        out_shape.append(jax.ShapeDtypeStruct((1, LANES), F32))
        out_specs.append(lvec)
    else:
        in_specs = [blk, blk, vec, blk]
        args = (x, y, g.reshape(1, d), dxn)
    return pl.pallas_call(
        body, name=name, grid=(rows // LN_ROWS,), in_specs=in_specs, out_specs=out_specs, out_shape=out_shape,
        compiler_params=_cparams(("arbitrary",)),
    )(*args)


def _conv_from_ext(ext_ref, w, width, rows):
    out = None
    for j in range(width):
        off = SUBLANES - (width - 1) + j
        term = ext_ref[off:off + rows, :] * w[j:j + 1, :]
        out = term if out is None else out + term
    return out


def _conv_wgrad_from_ext(ext_ref, dout, width, rows):
    res = []
    for j in range(width):
        off = SUBLANES - (width - 1) + j
        res.append(jnp.sum(ext_ref[off:off + rows, :] * dout, axis=0, keepdims=True))
    return res


def _conv_dgrad_from_ext(dext_ref, w, width, rows):
    out = None
    for j in range(width):
        off = (width - 1) - j
        term = dext_ref[off:off + rows, :] * w[j:j + 1, :]
        out = term if out is None else out + term
    return out


SC_ROWS = 128


def sc_fwd(proj, conv_w, *, name):
    rows = proj.shape[0]
    nb = rows // SC_ROWS
    hb = SC_ROWS // SUBLANES

    def body(p_ref, halo_ref, w_ref, o_ref, ext_ref):
        i = pl.program_id(0)
        h = p_ref[:, 0:SC_W]
        bg = p_ref[:, SC_W:2 * SC_W]
        cg = p_ref[:, 2 * SC_W:3 * SC_W]
        z = p_ref[:, 3 * SC_W:4 * SC_W]
        uh = halo_ref[:, 2 * SC_W:3 * SC_W] * halo_ref[:, 0:SC_W]
        ext_ref[0:SUBLANES, :] = jnp.where(i == 0, 0.0, uh)
        ext_ref[SUBLANES:, :] = cg * h
        cv = _conv_from_ext(ext_ref, w_ref[...], SC_CONV, SC_ROWS)
        o_ref[...] = bg * cv * _silu(z)

    return pl.pallas_call(
        body, name=name, grid=(nb,),
        in_specs=[pl.BlockSpec((SC_ROWS, SC_IN), lambda i: (i, 0)),
                  pl.BlockSpec((SUBLANES, SC_IN), lambda i: (jnp.maximum(i * hb - 1, 0), 0)),
                  pl.BlockSpec((SUBLANES, SC_W), lambda i: (0, 0))],
        out_specs=pl.BlockSpec((SC_ROWS, SC_W), lambda i: (i, 0)),
        out_shape=jax.ShapeDtypeStruct((rows, SC_W), F32),
        scratch_shapes=[pltpu.VMEM((SC_ROWS + SUBLANES, SC_W), F32)],
        compiler_params=_cparams(("parallel",)),
    )(proj, proj, _pad_rows(conv_w))


def sc_bwd(proj, conv_w, dmix, *, name):
    rows = proj.shape[0]
    nb = rows // SC_ROWS
    hb = SC_ROWS // SUBLANES

    def body(p_ref, halo_ref, w_ref, dm_ref, dp_ref, dw_ref, ext_ref, dext_ref):
        i = pl.program_id(0)
        blk = nb - 1 - i
        w = w_ref[...]
        h = p_ref[:, 0:SC_W]
        bg = p_ref[:, SC_W:2 * SC_W]
        cg = p_ref[:, 2 * SC_W:3 * SC_W]
        z = p_ref[:, 3 * SC_W:4 * SC_W]
        uh = halo_ref[:, 2 * SC_W:3 * SC_W] * halo_ref[:, 0:SC_W]
        ext_ref[0:SUBLANES, :] = jnp.where(blk == 0, 0.0, uh)
        ext_ref[SUBLANES:, :] = cg * h
        cv = _conv_from_ext(ext_ref, w, SC_CONV, SC_ROWS)
        dm = dm_ref[...]
        sz = _silu(z)
        dy = dm * sz
        dp_ref[:, 3 * SC_W:4 * SC_W] = dm * bg * cv * _dsilu(z)
        dp_ref[:, SC_W:2 * SC_W] = dy * cv
        dcv = dy * bg

        @pl.when(i == 0)
        def _():
            dext_ref[SC_ROWS:, :] = jnp.zeros((SUBLANES, SC_W), F32)
            dw_ref[...] = jnp.zeros_like(dw_ref)

        dext_ref[0:SC_ROWS, :] = dcv
        du = _conv_dgrad_from_ext(dext_ref, w, SC_CONV, SC_ROWS)
        dp_ref[:, 0:SC_W] = du * cg
        dp_ref[:, 2 * SC_W:3 * SC_W] = du * h
        dws = _conv_wgrad_from_ext(ext_ref, dcv, SC_CONV, SC_ROWS)
        for j in range(SC_CONV):
            dw_ref[j:j + 1, :] += dws[j]
        dext_ref[SC_ROWS:, :] = dcv[0:SUBLANES, :]

    return pl.pallas_call(
        body, name=name, grid=(nb,),
        in_specs=[pl.BlockSpec((SC_ROWS, SC_IN), lambda i: (nb - 1 - i, 0)),
                  pl.BlockSpec((SUBLANES, SC_IN), lambda i: (jnp.maximum((nb - 1 - i) * hb - 1, 0), 0)),
                  pl.BlockSpec((SUBLANES, SC_W), lambda i: (0, 0)),
                  pl.BlockSpec((SC_ROWS, SC_W), lambda i: (nb - 1 - i, 0))],
        out_specs=[pl.BlockSpec((SC_ROWS, SC_IN), lambda i: (nb - 1 - i, 0)),
                   pl.BlockSpec((SUBLANES, SC_W), lambda i: (0, 0))],
        out_shape=[jax.ShapeDtypeStruct((rows, SC_IN), F32), jax.ShapeDtypeStruct((SUBLANES, SC_W), F32)],
        scratch_shapes=[pltpu.VMEM((SC_ROWS + SUBLANES, SC_W), F32), pltpu.VMEM((SC_ROWS + SUBLANES, SC_W), F32)],
        compiler_params=_cparams(("arbitrary",)),
    )(proj, proj, _pad_rows(conv_w), dmix)


def _pad_rows(w, rows=SUBLANES):
    return jnp.pad(w, ((0, rows - w.shape[0]), (0, 0)))


def _pad_lanes(v, lanes=LANES):
    v = v.reshape(1, -1)
    return jnp.pad(v, ((0, 0), (0, lanes - v.shape[1])))


def _tri(n, lower=True):
    r, c = _iota((n, n), 0), _iota((n, n), 1)
    return jnp.where((c <= r) if lower else (c >= r), 1.0, 0.0)


def _eye(n):
    return jnp.where(_iota((n, n), 0) == _iota((n, n), 1), 1.0, 0.0)


def _head_expand(n, width):
    return jnp.where(_iota((LANES, n), 1) // width == _iota((LANES, n), 0), 1.0, 0.0)


def _col(v, h):
    return jnp.sum(jnp.where(_iota(v.shape, 1) == h, v, 0.0), axis=1, keepdims=True)


def _row(v, r):
    return jnp.sum(jnp.where(_iota(v.shape, 0) == r, v, 0.0), axis=0, keepdims=True)


def _expand_row(v, e):
    return jnp.max(_xsel(jnp.broadcast_to(v, (SUBLANES, LANES)), e), axis=0, keepdims=True)


SSD_ROWS = 128
SSD_X0 = D_INNER
SSD_DT0 = D_INNER + SSD_CONV_DIM
SSD_B0 = D_INNER
SSD_C0 = D_INNER + SSD_G * SSD_S
SSD_GW = D_INNER // SSD_G
SSD_HG = SSD_H // SSD_G


def _ssd_prologue(blk, p_ref, halo_ref, cw_ref, cb_ref, dtb_ref, ext_ref, xbc_ref, dt_ref):
    ext_ref[0:SUBLANES, :] = jnp.where(blk == 0, 0.0, halo_ref[:, SSD_X0:SSD_DT0])
    ext_ref[SUBLANES:, :] = p_ref[:, SSD_X0:SSD_DT0]
    pre = _conv_from_ext(ext_ref, cw_ref[...], SSD_CONV, SSD_ROWS) + cb_ref[...]
    xbc_ref[...] = _silu(pre)
    dt_ref[...] = _softplus(p_ref[:, SSD_DT0:SSD_DT0 + LANES] + dtb_ref[...])
    return pre


def _ssd_chunk_decays(dt_c, a_row, ltri, eye_l, act_ref):
    da = dt_c * a_row
    ac = _sel(ltri,da)
    act_ref[...] = _sel_nt(eye_l,ac)
    ac_last = _row(ac, CHUNK - 1)
    return ac, jnp.exp(ac_last - ac), jnp.exp(ac), jnp.exp(ac_last)


def _ssd_seg(ac, act_ref, h, causal):
    return jnp.where(causal, jnp.exp(jnp.minimum(_col(ac, h) - act_ref[pl.ds(h, 1), :], 0.0)), 0.0)


def _ssd_half(pair, e):
    upper = _iota(pair.shape, 1) >= SSD_P
    return jnp.where(upper if e % 2 else jnp.logical_not(upper), pair, 0.0)


def _ssd_group_fwd(g, xbc_ref, rows, dt_exp, tail_exp, cdec_exp, ac, act_ref, s_g, causal):
    gl = slice(g * SSD_GW, (g + 1) * SSD_GW)
    bg = xbc_ref[rows, SSD_B0 + g * SSD_S:SSD_B0 + (g + 1) * SSD_S]
    cg = xbc_ref[rows, SSD_C0 + g * SSD_S:SSD_C0 + (g + 1) * SSD_S]
    xdt = xbc_ref[rows, gl] * dt_exp[:, gl]
    cb = _dot_nt(cg, bg)
    cs = _dot(cg, s_g)
    segs = [_ssd_seg(ac, act_ref, g * SSD_HG + e, causal) for e in range(SSD_HG)]
    gms = [seg * cb for seg in segs]
    xps = [xdt[:, p * LANES:(p + 1) * LANES] for p in range(SSD_HG // 2)]
    parts = [_dot(gms[e], _ssd_half(xps[e // 2], e)) for e in range(SSD_HG)]
    yd = jnp.concatenate([parts[2 * p] + parts[2 * p + 1] for p in range(SSD_HG // 2)], axis=1)
    st = _dot_tn(bg, xdt * tail_exp[:, gl])
    return yd + cs * cdec_exp[:, gl], st, bg, cg, cb, xdt, cs, segs, gms


def ssd_fwd(proj, conv_w, conv_b, a_log, dt_bias, d_skip, norm_w, *, name):
    rows_total = proj.shape[0]
    nb = rows_total // SSD_ROWS
    hb = SSD_ROWS // SUBLANES
    cpb = SSD_ROWS // CHUNK

    def body(p_ref, halo_ref, cw_ref, cb_ref, alog_ref, dtb_ref, dsk_ref, nw_ref, mix_ref, st_ref,
             ext_ref, xbc_ref, dt_ref, s_ref, act_ref):
        i = pl.program_id(0)

        @pl.when(i == 0)
        def _():
            s_ref[...] = jnp.zeros_like(s_ref)

        _ssd_prologue(i, p_ref, halo_ref, cw_ref, cb_ref, dtb_ref, ext_ref, xbc_ref, dt_ref)
        a_row = -jnp.exp(alog_ref[...])
        expand = _head_expand(D_INNER, SSD_P)
        dsk_exp = _expand_row(dsk_ref[...], expand)
        ltri, eye_l = _tri(CHUNK), _eye(LANES)
        causal = _iota((CHUNK, CHUNK), 1) <= _iota((CHUNK, CHUNK), 0)

        def chunk(c, carry):
            rows = pl.ds(pl.multiple_of(c * CHUNK, CHUNK), CHUNK)
            dt_c = dt_ref[rows, :]
            ac, tail, cdec, tot = _ssd_chunk_decays(dt_c, a_row, ltri, eye_l, act_ref)
            dt_exp = _xsel(dt_c, expand)
            tail_exp = _xsel(tail, expand)
            cdec_exp = _xsel(cdec, expand)
            tot_exp = _expand_row(tot, expand)
            for g in range(SSD_G):
                gl = slice(g * SSD_GW, (g + 1) * SSD_GW)
                s_g = s_ref[g]
                st_ref[c, g] = s_g
                y, st = _ssd_group_fwd(g, xbc_ref, rows, dt_exp, tail_exp, cdec_exp, ac, act_ref, s_g, causal)[:2]
                s_ref[g] = s_g * tot_exp[:, gl] + st
                y = (y + dsk_exp[:, gl] * xbc_ref[rows, gl]) * _silu(p_ref[rows, gl])
                r = lax.rsqrt(jnp.mean(y * y, axis=-1, keepdims=True) + RMS_EPS)
                mix_ref[rows, gl] = y * r * nw_ref[:, gl]
            return carry

        lax.fori_loop(0, cpb, chunk, 0)

    vec = lambda n: pl.BlockSpec((1, n), lambda i: (0, 0))
    return pl.pallas_call(
        body, name=name, grid=(nb,),
        in_specs=[pl.BlockSpec((SSD_ROWS, SSD_IN_PAD), lambda i: (i, 0)),
                  pl.BlockSpec((SUBLANES, SSD_IN_PAD), lambda i: (jnp.maximum(i * hb - 1, 0), 0)),
                  pl.BlockSpec((SUBLANES, SSD_CONV_DIM), lambda i: (0, 0)),
                  vec(SSD_CONV_DIM), vec(LANES), vec(LANES), vec(LANES), vec(D_INNER)],
        out_specs=[pl.BlockSpec((SSD_ROWS, D_INNER), lambda i: (i, 0)),
                   pl.BlockSpec((cpb, SSD_G, SSD_S, SSD_GW), lambda i: (i, 0, 0, 0))],
        out_shape=[jax.ShapeDtypeStruct((rows_total, D_INNER), F32),
                   jax.ShapeDtypeStruct((rows_total // CHUNK, SSD_G, SSD_S, SSD_GW), F32)],
        scratch_shapes=[pltpu.VMEM((SSD_ROWS + SUBLANES, SSD_CONV_DIM), F32),
                        pltpu.VMEM((SSD_ROWS, SSD_CONV_DIM), F32),
                        pltpu.VMEM((SSD_ROWS, LANES), F32),
                        pltpu.VMEM((SSD_G, SSD_S, SSD_GW), F32),
                        pltpu.VMEM((LANES, CHUNK), F32)],
        compiler_params=_cparams(("arbitrary",)),
    )(proj, proj, _pad_rows(conv_w), conv_b.reshape(1, -1), _pad_lanes(a_log), _pad_lanes(dt_bias),
      _pad_lanes(d_skip), norm_w.reshape(1, -1))


def ssd_bwd(proj, conv_w, conv_b, a_log, dt_bias, d_skip, norm_w, states, dmix, *, name):
    rows_total = proj.shape[0]
    nb = rows_total // SSD_ROWS
    hb = SSD_ROWS // SUBLANES
    cpb = SSD_ROWS // CHUNK

    def body(p_ref, halo_ref, cw_ref, cb_ref, alog_ref, dtb_ref, dsk_ref, nw_ref, st_ref, dm_ref,
             dp_ref, dcw_ref, dcb_ref, dalog_ref, ddtb_ref, ddsk_ref, dnw_ref,
             ext_ref, xbc_ref, dt_ref, ds_ref, act_ref, dext_ref, dac_ref, dact_ref, ddskw_ref):
        i = pl.program_id(0)
        blk = nb - 1 - i

        @pl.when(i == 0)
        def _():
            ds_ref[...] = jnp.zeros_like(ds_ref)
            dext_ref[SSD_ROWS:, :] = jnp.zeros((SUBLANES, SSD_CONV_DIM), F32)
            ddskw_ref[...] = jnp.zeros_like(ddskw_ref)
            for r in (dcw_ref, dcb_ref, dalog_ref, ddtb_ref, ddsk_ref, dnw_ref):
                r[...] = jnp.zeros_like(r)

        _ssd_prologue(blk, p_ref, halo_ref, cw_ref, cb_ref, dtb_ref, ext_ref, xbc_ref, dt_ref)
        a_row = -jnp.exp(alog_ref[...])
        expand = _head_expand(D_INNER, SSD_P)
        dsk_exp = _expand_row(dsk_ref[...], expand)
        ltri, utri, eye_l, eye_c = _tri(CHUNK), _tri(CHUNK, lower=False), _eye(LANES), _eye(CHUNK)
        causal = _iota((CHUNK, CHUNK), 1) <= _iota((CHUNK, CHUNK), 0)
        dp_ref[:, SSD_DT0 + LANES:] = jnp.zeros((SSD_ROWS, SSD_IN_PAD - SSD_DT0 - LANES), F32)

        def chunk(cc, carry):
            c = cpb - 1 - cc
            rows = pl.ds(pl.multiple_of(c * CHUNK, CHUNK), CHUNK)
            dt_c = dt_ref[rows, :]
            ac, tail, cdec, tot = _ssd_chunk_decays(dt_c, a_row, ltri, eye_l, act_ref)
            dt_exp = _xsel(dt_c, expand)
            tail_exp = _xsel(tail, expand)
            cdec_exp = _xsel(cdec, expand)
            tot_exp = _expand_row(tot, expand)
            dac_ref[...] = jnp.zeros_like(dac_ref)
            dact_ref[...] = jnp.zeros_like(dact_ref)
            d_cdec = jnp.zeros((CHUNK, LANES), F32)
            d_tail = jnp.zeros((CHUNK, LANES), F32)
            d_dt = jnp.zeros((CHUNK, LANES), F32)
            d_tot = jnp.zeros((1, LANES), F32)
            for g in range(SSD_G):
                gl = slice(g * SSD_GW, (g + 1) * SSD_GW)
                ex_g = expand[:, gl]
                s_g = st_ref[c, g]
                y, _, bg, cg, cb, xdt, cs, segs, gms = _ssd_group_fwd(g, xbc_ref, rows, dt_exp, tail_exp, cdec_exp, ac, act_ref, s_g, causal)
                xs = xbc_ref[rows, gl]
                z = p_ref[rows, gl]
                sz = _silu(z)
                y2 = y + dsk_exp[:, gl] * xs
                yg = y2 * sz
                r = lax.rsqrt(jnp.mean(yg * yg, axis=-1, keepdims=True) + RMS_EPS)
                yn = yg * r
                dm = dm_ref[rows, gl]
                dnw_ref[:, gl] += jnp.sum(dm * yn, axis=0, keepdims=True)
                dyn = dm * nw_ref[:, gl]
                dyg = r * (dyn - yn * jnp.mean(dyn * yn, axis=-1, keepdims=True))
                dp_ref[rows, gl] = dyg * y2 * _dsilu(z)
                dy = dyg * sz
                ddskw_ref[:, gl] += jnp.sum(dy * xs, axis=0, keepdims=True)
                ds_g = ds_ref[g]
                dyc = dy * cdec_exp[:, gl]
                ds_ref[g] = ds_g * tot_exp[:, gl] + _dot_tn(cg, dyc)
                sds = jnp.broadcast_to(jnp.sum(s_g * ds_g, axis=0, keepdims=True), (SUBLANES, SSD_GW))
                d_tot = d_tot + jnp.max(_xsel_nt(sds, ex_g), axis=0, keepdims=True)
                dcg = _dot_nt(dyc, s_g)
                d_cdec = d_cdec + _xsel_nt(dy * cs, ex_g)
                xdtd = xdt * tail_exp[:, gl]
                d_xdtd = _dot(bg, ds_g)
                dbg = _dot_nt(xdtd, ds_g)
                d_tail = d_tail + _xsel_nt(d_xdtd * xdt, ex_g)
                heads = range(SSD_HG)
                dy_h = [_ssd_half(dy[:, (e // 2) * LANES:(e // 2 + 1) * LANES], e) for e in heads]
                back = [_dot_tn(gms[e], dy_h[e]) for e in heads]
                dg_m = [jnp.where(causal, _dot_nt(dy_h[e], xdt[:, (e // 2) * LANES:(e // 2 + 1) * LANES]), 0.0) for e in heads]
                d_cb = None
                for e in heads:
                    h = g * SSD_HG + e
                    term = dg_m[e] * segs[e]
                    d_cb = term if d_cb is None else d_cb + term
                    em = dg_m[e] * gms[e]
                    dac_ref[...] += jnp.where(_iota((CHUNK, LANES), 1) == h, jnp.sum(em, axis=1, keepdims=True), 0.0)
                    dact_ref[h:h + 1, :] = jnp.sum(em, axis=0, keepdims=True)
                dcg = dcg + _dot(d_cb, bg)
                dbg = dbg + _dot_tn(d_cb, cg)
                d_xdt = d_xdtd * tail_exp[:, gl] + jnp.concatenate(
                    [back[2 * p] + back[2 * p + 1] for p in range(SSD_HG // 2)], axis=1)
                d_dt = d_dt + _xsel_nt(d_xdt * xs, ex_g)
                dext_ref[rows, gl] = d_xdt * dt_exp[:, gl] + dy * dsk_exp[:, gl]
                dext_ref[rows, SSD_B0 + g * SSD_S:SSD_B0 + (g + 1) * SSD_S] = dbg
                dext_ref[rows, SSD_C0 + g * SSD_S:SSD_C0 + (g + 1) * SSD_S] = dcg
            d_ac = dac_ref[...] - _sel_nt(eye_c,dact_ref[...]) + d_cdec * cdec - d_tail * tail
            d_last = jnp.sum(d_tail * tail, axis=0, keepdims=True) + d_tot * tot
            d_ac = jnp.where(_iota((CHUNK, LANES), 0) == CHUNK - 1, d_ac + d_last, d_ac)
            d_da = _sel(utri,d_ac)
            d_dt = d_dt + d_da * a_row
            dalog_ref[...] += jnp.sum(d_da * dt_c, axis=0, keepdims=True) * a_row
            d_raw = d_dt * _sigmoid(p_ref[rows, SSD_DT0:SSD_DT0 + LANES] + dtb_ref[...])
            d_raw = jnp.where(_iota((CHUNK, LANES), 1) < SSD_H, d_raw, 0.0)
            ddtb_ref[...] += jnp.sum(d_raw, axis=0, keepdims=True)
            dp_ref[rows, SSD_DT0:SSD_DT0 + LANES] = d_raw
            return carry

        lax.fori_loop(0, cpb, chunk, 0)
        pre = _conv_from_ext(ext_ref, cw_ref[...], SSD_CONV, SSD_ROWS) + cb_ref[...]
        d_pre = dext_ref[0:SSD_ROWS, :] * _dsilu(pre)
        dext_ref[0:SSD_ROWS, :] = d_pre
        dcb_ref[...] += jnp.sum(d_pre, axis=0, keepdims=True)
        dp_ref[:, SSD_X0:SSD_DT0] = _conv_dgrad_from_ext(dext_ref, cw_ref[...], SSD_CONV, SSD_ROWS)
        dws = _conv_wgrad_from_ext(ext_ref, d_pre, SSD_CONV, SSD_ROWS)
        for j in range(SSD_CONV):
            dcw_ref[j:j + 1, :] += dws[j]
        dext_ref[SSD_ROWS:, :] = d_pre[0:SUBLANES, :]

        @pl.when(i == nb - 1)
        def _():
            ddsk_ref[...] = jnp.max(_xsel_nt(jnp.broadcast_to(ddskw_ref[...], (SUBLANES, D_INNER)), expand), axis=0, keepdims=True)

    vec = lambda n: pl.BlockSpec((1, n), lambda i: (0, 0))
    outs = pl.pallas_call(
        body, name=name, grid=(nb,),
        in_specs=[pl.BlockSpec((SSD_ROWS, SSD_IN_PAD), lambda i: (nb - 1 - i, 0)),
                  pl.BlockSpec((SUBLANES, SSD_IN_PAD), lambda i: (jnp.maximum((nb - 1 - i) * hb - 1, 0), 0)),
                  pl.BlockSpec((SUBLANES, SSD_CONV_DIM), lambda i: (0, 0)),
                  vec(SSD_CONV_DIM), vec(LANES), vec(LANES), vec(LANES), vec(D_INNER),
                  pl.BlockSpec((cpb, SSD_G, SSD_S, SSD_GW), lambda i: (nb - 1 - i, 0, 0, 0)),
                  pl.BlockSpec((SSD_ROWS, D_INNER), lambda i: (nb - 1 - i, 0))],
        out_specs=[pl.BlockSpec((SSD_ROWS, SSD_IN_PAD), lambda i: (nb - 1 - i, 0)),
                   pl.BlockSpec((SUBLANES, SSD_CONV_DIM), lambda i: (0, 0)),
                   vec(SSD_CONV_DIM), vec(LANES), vec(LANES), vec(LANES), vec(D_INNER)],
        out_shape=[jax.ShapeDtypeStruct((rows_total, SSD_IN_PAD), F32),
                   jax.ShapeDtypeStruct((SUBLANES, SSD_CONV_DIM), F32),
                   jax.ShapeDtypeStruct((1, SSD_CONV_DIM), F32), jax.ShapeDtypeStruct((1, LANES), F32),
                   jax.ShapeDtypeStruct((1, LANES), F32), jax.ShapeDtypeStruct((1, LANES), F32),
                   jax.ShapeDtypeStruct((1, D_INNER), F32)],
        scratch_shapes=[pltpu.VMEM((SSD_ROWS + SUBLANES, SSD_CONV_DIM), F32),
                        pltpu.VMEM((SSD_ROWS, SSD_CONV_DIM), F32),
                        pltpu.VMEM((SSD_ROWS, LANES), F32),
                        pltpu.VMEM((SSD_G, SSD_S, SSD_GW), F32),
                        pltpu.VMEM((LANES, CHUNK), F32),
                        pltpu.VMEM((SSD_ROWS + SUBLANES, SSD_CONV_DIM), F32),
                        pltpu.VMEM((CHUNK, LANES), F32),
                        pltpu.VMEM((LANES, CHUNK), F32),
                        pltpu.VMEM((1, D_INNER), F32)],
        compiler_params=_cparams(("arbitrary",)),
    )(proj, proj, _pad_rows(conv_w), conv_b.reshape(1, -1), _pad_lanes(a_log), _pad_lanes(dt_bias),
      _pad_lanes(d_skip), norm_w.reshape(1, -1), states, dmix)
    dproj, dcw, dcb, dalog, ddtb, ddsk, dnw = outs
    return dproj, [dcw[:SSD_CONV], dcb[0], dalog[0, :SSD_H], ddtb[0, :SSD_H], ddsk[0, :SSD_H], dnw[0]]


GDN_ROWS = 128
GDN_K0 = GDN_QK
GDN_V0 = 2 * GDN_QK
GDN_Z0 = GDN_CONV_DIM
GDN_BA0 = GDN_CONV_DIM + GDN_V
GDN_GL = GDN_VH
GDN_SCALE = GDN_HEAD ** -0.5
GDN_GROUP = 8


def _gdn_lane_params(v):
    return jnp.pad(v.reshape(1, GDN_VH), ((0, 0), (GDN_GL, LANES - GDN_GL - GDN_VH)))


def _inv_unit_lower(a, eye_c):
    x = eye_c - a
    ph, pl_ = _split(a, 2)
    n = 2
    while n < CHUNK:
        p = (_mxu(pl_, ph, NN) + _mxu(ph, pl_, NN)) + _mxu(ph, ph, NN)
        ph, pl_ = _split(p, 2)
        xh, xl = _split(x, 2)
        x = x + ((_mxu(xl, ph, NN) + _mxu(xh, pl_, NN)) + _mxu(xh, ph, NN))
        n *= 2
    return x


def _gdn_prologue(blk, p_ref, halo_ref, cw_ref, alog_ref, dtb_ref, ext_ref, qkv_ref, beta_ref, g_ref):
    ext_ref[0:SUBLANES, :] = jnp.where(blk == 0, 0.0, halo_ref[:, 0:GDN_CONV_DIM])
    ext_ref[SUBLANES:, :] = p_ref[:, 0:GDN_CONV_DIM]
    w = cw_ref[...]
    for hq in range(2 * GDN_QKH):
        cols = slice(hq * GDN_HEAD, (hq + 1) * GDN_HEAD)
        pre = None
        for j in range(GDN_CONV):
            off = SUBLANES - (GDN_CONV - 1) + j
            term = ext_ref[off:off + GDN_ROWS, cols] * w[j:j + 1, cols]
            pre = term if pre is None else pre + term
        a = _silu(pre)
        r = lax.rsqrt(jnp.sum(a * a, axis=-1, keepdims=True) + L2_EPS)
        qkv_ref[:, cols] = a * (r * (GDN_SCALE if hq < GDN_QKH else 1.0))
    vcols = slice(GDN_V0, GDN_CONV_DIM)
    pre = None
    for j in range(GDN_CONV):
        off = SUBLANES - (GDN_CONV - 1) + j
        term = ext_ref[off:off + GDN_ROWS, vcols] * w[j:j + 1, vcols]
        pre = term if pre is None else pre + term
    qkv_ref[:, vcols] = _silu(pre)
    ba = p_ref[:, GDN_BA0:GDN_BA0 + LANES]
    beta_ref[...] = _sigmoid(ba)
    g_ref[...] = -jnp.exp(alog_ref[...]) * _softplus(ba + dtb_ref[...])


def _each(f, *lists):
    return [f(*z) for z in zip(*lists)]


def _inv_unit_lower_each(a_list, eye_c):
    xs = [eye_c - a for a in a_list]
    ps = [_split(a, 2) for a in a_list]
    n = 2
    while n < CHUNK:
        ps = [_split((_mxu(pl_, ph, NN) + _mxu(ph, pl_, NN)) + _mxu(ph, ph, NN), 2) for ph, pl_ in ps]
        xs_split = [_split(x, 2) for x in xs]
        xs = [x + ((_mxu(xl, ph, NN) + _mxu(xh, pl_, NN)) + _mxu(xh, ph, NN))
              for x, (xh, xl), (ph, pl_) in zip(xs, xs_split, ps)]
        n *= 2
    return xs


def _gdn_heads_fwd(q, k, v, kk, qk, gcol, grow, glast, bcol, s, causal, strict, eye_c, t=None):
    decay = _each(lambda gc_, gr_: jnp.where(causal, jnp.exp(jnp.minimum(gc_ - gr_, 0.0)), 0.0), gcol, grow)
    egc = _each(jnp.exp, gcol)
    etail = _each(lambda gl_, gc_: jnp.exp(gl_ - gc_), glast, gcol)
    cd = _each(jnp.exp, glast)
    a = _each(lambda b_, kk_, d_: jnp.where(strict, b_ * kk_ * d_, 0.0), bcol, kk, decay)
    if t is None:
        t = _inv_unit_lower_each(a, eye_c)
    kb = _each(lambda k_, b_: k_ * b_, k, bcol)
    rhs_w = _each(lambda kb_, e_: kb_ * e_, kb, egc)
    u = _each(lambda t_, v_, b_: _dot_x3(t_, v_ * b_), t, v, bcol)
    w = _each(_dot_x3, t, rhs_w)
    attn = _each(lambda qk_, d_: qk_ * d_, qk, decay)
    ws = _each(_dot, w, s)
    v_new = _each(lambda u_, ws_: u_ - ws_, u, ws)
    qd = _each(lambda q_, e_: q_ * e_, q, egc)
    kt = _each(lambda k_, e_: k_ * e_, k, etail)
    o1 = _each(_dot, qd, s)
    o2 = _each(_dot, attn, v_new)
    out = _each(lambda a_, b_: a_ + b_, o1, o2)
    upd = _each(_dot_tn, kt, v_new)
    s_new = _each(lambda s_, c_, u_: s_ * c_ + u_, s, cd, upd)
    return dict(decay=decay, egc=egc, etail=etail, cd=cd, a=a, t=t, kb=kb, rhs_w=rhs_w, u=u, w=w, attn=attn,
                v_new=v_new, qd=qd, kt=kt, out=out, s_new=s_new)


def gdn_fwd(proj, conv_w, a_log, dt_bias, norm_w, *, name):
    rows_total = proj.shape[0]
    nb = rows_total // GDN_ROWS
    hb = GDN_ROWS // SUBLANES
    cpb = GDN_ROWS // CHUNK

    def body(p_ref, halo_ref, cw_ref, alog_ref, dtb_ref, nw_ref, mix_ref, st_ref, tm_ref,
             ext_ref, qkv_ref, beta_ref, g_ref, s_ref, gct_ref):
        i = pl.program_id(0)

        @pl.when(i == 0)
        def _():
            s_ref[...] = jnp.zeros_like(s_ref)

        _gdn_prologue(i, p_ref, halo_ref, cw_ref, alog_ref, dtb_ref, ext_ref, qkv_ref, beta_ref, g_ref)
        ltri, eye_l, eye_c = _tri(CHUNK), _eye(LANES), _eye(CHUNK)
        causal = _iota((CHUNK, CHUNK), 1) <= _iota((CHUNK, CHUNK), 0)
        strict = _iota((CHUNK, CHUNK), 1) < _iota((CHUNK, CHUNK), 0)
        nw = nw_ref[...]

        def chunk(c, carry):
            rows = pl.ds(pl.multiple_of(c * CHUNK, CHUNK), CHUNK)
            gc = _sel(ltri,g_ref[rows, :])
            gct_ref[...] = _sel_nt(eye_l,gc)
            glast_row = _row(gc, CHUNK - 1)
            beta_c = beta_ref[rows, :]
            for h0 in range(0, GDN_VH, GDN_GROUP):
                hs = list(range(h0, h0 + GDN_GROUP))
                qs = {hq: qkv_ref[rows, hq * GDN_HEAD:(hq + 1) * GDN_HEAD] for hq in range(h0 // 2, (h0 + GDN_GROUP) // 2)}
                ks = {hq: qkv_ref[rows, GDN_K0 + hq * GDN_HEAD:GDN_K0 + (hq + 1) * GDN_HEAD] for hq in qs}
                kks = {hq: _dot_nt(ks[hq], ks[hq]) for hq in qs}
                qks = {hq: _dot_nt(qs[hq], ks[hq]) for hq in qs}
                ss = [s_ref[h] for h in hs]
                for h, s in zip(hs, ss):
                    st_ref[c, h] = s
                f = _gdn_heads_fwd(
                    [qs[h // 2] for h in hs], [ks[h // 2] for h in hs],
                    [qkv_ref[rows, GDN_V0 + h * GDN_HEAD:GDN_V0 + (h + 1) * GDN_HEAD] for h in hs],
                    [kks[h // 2] for h in hs], [qks[h // 2] for h in hs],
                    [_col(gc, GDN_GL + h) for h in hs], [gct_ref[GDN_GL + h:GDN_GL + h + 1, :] for h in hs],
                    [_col(glast_row, GDN_GL + h) for h in hs], [_col(beta_c, h) for h in hs], ss, causal, strict, eye_c)
                for i_h, h in enumerate(hs):
                    hc = slice(h * GDN_HEAD, (h + 1) * GDN_HEAD)
                    s_ref[h] = f["s_new"][i_h]
                    tm_ref[c, h] = f["t"][i_h]
                    o = f["out"][i_h]
                    r = lax.rsqrt(jnp.mean(o * o, axis=-1, keepdims=True) + RMS_EPS)
                    mix_ref[rows, hc] = o * r * nw * _silu(p_ref[rows, GDN_Z0 + h * GDN_HEAD:GDN_Z0 + (h + 1) * GDN_HEAD])
            return carry

        lax.fori_loop(0, cpb, chunk, 0)

    vec = lambda n: pl.BlockSpec((1, n), lambda i: (0, 0))
    mix, states, tmats = pl.pallas_call(
        body, name=name, grid=(nb,),
        in_specs=[pl.BlockSpec((GDN_ROWS, GDN_IN_PAD), lambda i: (i, 0)),
                  pl.BlockSpec((SUBLANES, GDN_IN_PAD), lambda i: (jnp.maximum(i * hb - 1, 0), 0)),
                  pl.BlockSpec((SUBLANES, GDN_CONV_DIM), lambda i: (0, 0)),
                  vec(LANES), vec(LANES), vec(GDN_HEAD)],
        out_specs=[pl.BlockSpec((GDN_ROWS, GDN_V), lambda i: (i, 0)),
                   pl.BlockSpec((cpb, GDN_VH, GDN_HEAD, GDN_HEAD), lambda i: (i, 0, 0, 0)),
                   pl.BlockSpec((cpb, GDN_VH, CHUNK, CHUNK), lambda i: (i, 0, 0, 0))],
        out_shape=[jax.ShapeDtypeStruct((rows_total, GDN_V), F32),
                   jax.ShapeDtypeStruct((rows_total // CHUNK, GDN_VH, GDN_HEAD, GDN_HEAD), F32),
                   jax.ShapeDtypeStruct((rows_total // CHUNK, GDN_VH, CHUNK, CHUNK), F32)],
        scratch_shapes=[pltpu.VMEM((GDN_ROWS + SUBLANES, GDN_CONV_DIM), F32),
                        pltpu.VMEM((GDN_ROWS, GDN_CONV_DIM), F32),
                        pltpu.VMEM((GDN_ROWS, LANES), F32),
                        pltpu.VMEM((GDN_ROWS, LANES), F32),
                        pltpu.VMEM((GDN_VH, GDN_HEAD, GDN_HEAD), F32),
                        pltpu.VMEM((LANES, CHUNK), F32)],
        compiler_params=_cparams(("arbitrary",)),
    )(proj, proj, _pad_rows(conv_w), _gdn_lane_params(a_log), _gdn_lane_params(dt_bias), norm_w.reshape(1, -1))
    return mix, (states, tmats)


def gdn_bwd(proj, conv_w, a_log, dt_bias, norm_w, saved, dmix, *, name):
    states, tmats = saved
    rows_total = proj.shape[0]
    nb = rows_total // GDN_ROWS
    hb = GDN_ROWS // SUBLANES
    cpb = GDN_ROWS // CHUNK

    def body(p_ref, halo_ref, cw_ref, alog_ref, dtb_ref, nw_ref, st_ref, tm_ref, dm_ref,
             dp_ref, dcw_ref, dalog_ref, ddtb_ref, dnw_ref,
             ext_ref, qkv_ref, beta_ref, g_ref, ds_ref, gct_ref, dext_ref, dgc_ref, dgct_ref, dbeta_ref):
        i = pl.program_id(0)
        blk = nb - 1 - i

        @pl.when(i == 0)
        def _():
            ds_ref[...] = jnp.zeros_like(ds_ref)
            dext_ref[GDN_ROWS:, :] = jnp.zeros((SUBLANES, GDN_CONV_DIM), F32)
            for r in (dcw_ref, dalog_ref, ddtb_ref, dnw_ref):
                r[...] = jnp.zeros_like(r)

        _gdn_prologue(blk, p_ref, halo_ref, cw_ref, alog_ref, dtb_ref, ext_ref, qkv_ref, beta_ref, g_ref)
        ltri, utri, eye_l, eye_c = _tri(CHUNK), _tri(CHUNK, lower=False), _eye(LANES), _eye(CHUNK)
        causal = _iota((CHUNK, CHUNK), 1) <= _iota((CHUNK, CHUNK), 0)
        strict = _iota((CHUNK, CHUNK), 1) < _iota((CHUNK, CHUNK), 0)
        lane = _iota((CHUNK, LANES), 1)
        is_last = _iota((CHUNK, 1), 0) == CHUNK - 1
        nw = nw_ref[...]

        def chunk(cc, carry):
            c = cpb - 1 - cc
            rows = pl.ds(pl.multiple_of(c * CHUNK, CHUNK), CHUNK)
            g_c = g_ref[rows, :]
            gc = _sel(ltri,g_c)
            gct_ref[...] = _sel_nt(eye_l,gc)
            glast_row = _row(gc, CHUNK - 1)
            beta_c = beta_ref[rows, :]
            dgc_ref[...] = jnp.zeros_like(dgc_ref)
            dgct_ref[...] = jnp.zeros_like(dgct_ref)
            dbeta_ref[...] = jnp.zeros_like(dbeta_ref)
            for h0 in range(0, GDN_VH, GDN_GROUP):
                hs = list(range(h0, h0 + GDN_GROUP))
                hqs = list(range(h0 // 2, (h0 + GDN_GROUP) // 2))
                qs = {hq: qkv_ref[rows, hq * GDN_HEAD:(hq + 1) * GDN_HEAD] for hq in hqs}
                ks = {hq: qkv_ref[rows, GDN_K0 + hq * GDN_HEAD:GDN_K0 + (hq + 1) * GDN_HEAD] for hq in hqs}
                kks = {hq: _dot_nt(ks[hq], ks[hq]) for hq in hqs}
                qks = {hq: _dot_nt(qs[hq], ks[hq]) for hq in hqs}
                q = [qs[h // 2] for h in hs]
                k = [ks[h // 2] for h in hs]
                v = [qkv_ref[rows, GDN_V0 + h * GDN_HEAD:GDN_V0 + (h + 1) * GDN_HEAD] for h in hs]
                s = [st_ref[c, h] for h in hs]
                bcol = [_col(beta_c, h) for h in hs]
                f = _gdn_heads_fwd(q, k, v, [kks[h // 2] for h in hs], [qks[h // 2] for h in hs],
                                   [_col(gc, GDN_GL + h) for h in hs], [gct_ref[GDN_GL + h:GDN_GL + h + 1, :] for h in hs],
                                   [_col(glast_row, GDN_GL + h) for h in hs], bcol, s, causal, strict, eye_c,
                                   t=[tm_ref[c, h] for h in hs])
                do = []
                for i_h, h in enumerate(hs):
                    zc = slice(GDN_Z0 + h * GDN_HEAD, GDN_Z0 + (h + 1) * GDN_HEAD)
                    o = f["out"][i_h]
                    z = p_ref[rows, zc]
                    sz = _silu(z)
                    r = lax.rsqrt(jnp.mean(o * o, axis=-1, keepdims=True) + RMS_EPS)
                    on = o * r
                    dm = dm_ref[rows, h * GDN_HEAD:(h + 1) * GDN_HEAD]
                    dnw_ref[...] += jnp.sum(dm * on * sz, axis=0, keepdims=True)
                    d_on = dm * nw * sz
                    dp_ref[rows, zc] = dm * on * nw * _dsilu(z)
                    do.append(r * (d_on - on * jnp.mean(d_on * on, axis=-1, keepdims=True)))
                ds_n = [ds_ref[h] for h in hs]
                dv1 = _each(_dot_tn, f["attn"], do)
                dv2 = _each(_dot, f["kt"], ds_n)
                d_vnew = _each(lambda a_, b_: a_ + b_, dv1, dv2)
                d_attn = _each(lambda do_, vn_: jnp.where(causal, _dot_nt(do_, vn_), 0.0), do, f["v_new"])
                d_qd = _each(_dot_nt, do, s)
                t1 = _each(_dot_tn, f["qd"], do)
                t2 = _each(_dot_tn, f["w"], d_vnew)
                for h, a_, cd_, dsn_, b_ in zip(hs, t1, f["cd"], ds_n, t2):
                    ds_ref[h] = a_ + cd_ * dsn_ - b_
                d_cd = _each(lambda s_, dsn_: jnp.sum(jnp.sum(s_ * dsn_, axis=1, keepdims=True), axis=0, keepdims=True), s, ds_n)
                d_kt = _each(_dot_nt, f["v_new"], ds_n)
                d_w = _each(lambda dv_, s_: -_dot_nt(dv_, s_), d_vnew, s)
                d_rhs_u = _each(lambda t_, d_: _dot_x3(t_, d_, TN), f["t"], d_vnew)
                d_rhs_w = _each(lambda t_, d_: _dot_x3(t_, d_, TN), f["t"], d_w)
                m1 = _each(_dot_nt, d_rhs_u, f["u"])
                m2 = _each(_dot_nt, d_rhs_w, f["w"])
                da = _each(lambda a_, b_: -jnp.where(strict, a_ + b_, 0.0), m1, m2)
                dmm = _each(lambda a_, b_: a_ * b_, da, f["decay"])
                em = _each(lambda da_, a_, dat_, at_: da_ * a_ + dat_ * at_, da, f["a"], d_attn, f["attn"])
                x1 = _each(_dot, dmm, k)
                d_kb = _each(lambda x_, drw_, e_: x_ + drw_ * e_, x1, d_rhs_w, f["egc"])
                dk1 = _each(_dot_tn, dmm, f["kb"])
                dpm = _each(lambda a_, b_: a_ * b_, d_attn, f["decay"])
                dq1 = _each(_dot, dpm, k)
                dq = _each(lambda x_, dqd_, e_: x_ + dqd_ * e_, dq1, d_qd, f["egc"])
                dk2 = _each(_dot_tn, dpm, q)
                dk = _each(lambda a_, b_, dkb_, bc_, dkt_, et_: a_ + b_ + dkb_ * bc_ + dkt_ * et_,
                           dk1, dk2, d_kb, bcol, d_kt, f["etail"])
                for i_h, h in enumerate(hs):
                    tmp = jnp.sum(d_kt[i_h] * f["kt"][i_h], axis=1, keepdims=True)
                    d_gcol = (jnp.sum(em[i_h], axis=1, keepdims=True)
                              + jnp.sum(d_rhs_w[i_h] * f["rhs_w"][i_h], axis=1, keepdims=True)
                              + jnp.sum(d_qd[i_h] * f["qd"][i_h], axis=1, keepdims=True) - tmp)
                    d_glast = jnp.sum(tmp, axis=0, keepdims=True) + d_cd[i_h] * f["cd"][i_h]
                    d_gcol = jnp.where(is_last, d_gcol + d_glast, d_gcol)
                    d_beta = (jnp.sum(d_rhs_u[i_h] * v[i_h], axis=1, keepdims=True)
                              + jnp.sum(d_kb[i_h] * k[i_h], axis=1, keepdims=True))
                    dgc_ref[...] += jnp.where(lane == GDN_GL + h, d_gcol, 0.0)
                    dgct_ref[GDN_GL + h:GDN_GL + h + 1, :] = jnp.sum(em[i_h], axis=0, keepdims=True)
                    dbeta_ref[...] += jnp.where(lane == h, d_beta, 0.0)
                    dext_ref[rows, GDN_V0 + h * GDN_HEAD:GDN_V0 + (h + 1) * GDN_HEAD] = d_rhs_u[i_h] * bcol[i_h]
                for hq in hqs:
                    i0 = 2 * hq - h0
                    dext_ref[rows, hq * GDN_HEAD:(hq + 1) * GDN_HEAD] = dq[i0] + dq[i0 + 1]
                    dext_ref[rows, GDN_K0 + hq * GDN_HEAD:GDN_K0 + (hq + 1) * GDN_HEAD] = dk[i0] + dk[i0 + 1]
            d_gc = dgc_ref[...] - _sel_nt(eye_c,dgct_ref[...])
            dg = _sel(utri,d_gc)
            ba = p_ref[rows, GDN_BA0:GDN_BA0 + LANES]
            d_sp = dg * -jnp.exp(alog_ref[...])
            d_araw = d_sp * _sigmoid(ba + dtb_ref[...])
            d_araw = jnp.where((lane >= GDN_GL) & (lane < GDN_GL + GDN_VH), d_araw, 0.0)
            dalog_ref[...] += jnp.sum(dg * g_c, axis=0, keepdims=True)
            ddtb_ref[...] += jnp.sum(d_araw, axis=0, keepdims=True)
            d_braw = jnp.where(lane < GDN_VH, dbeta_ref[...] * beta_c * (1.0 - beta_c), 0.0)
            dp_ref[rows, GDN_BA0:GDN_BA0 + LANES] = d_braw + d_araw
            return carry

        lax.fori_loop(0, cpb, chunk, 0)
        w = cw_ref[...]
        for hh in range(GDN_CONV_DIM // GDN_HEAD):
            cols = slice(hh * GDN_HEAD, (hh + 1) * GDN_HEAD)
            pre = None
            for j in range(GDN_CONV):
                off = SUBLANES - (GDN_CONV - 1) + j
                term = ext_ref[off:off + GDN_ROWS, cols] * w[j:j + 1, cols]
                pre = term if pre is None else pre + term
            d_act = dext_ref[0:GDN_ROWS, cols]
            if hh < 2 * GDN_QKH:
                a = _silu(pre)
                r = lax.rsqrt(jnp.sum(a * a, axis=-1, keepdims=True) + L2_EPS)
                ah = a * r
                if hh < GDN_QKH:
                    d_act = d_act * GDN_SCALE
                d_act = r * (d_act - ah * jnp.sum(d_act * ah, axis=-1, keepdims=True))
            d_pre = d_act * _dsilu(pre)
            dext_ref[0:GDN_ROWS, cols] = d_pre
            for j in range(GDN_CONV):
                off = SUBLANES - (GDN_CONV - 1) + j
                dcw_ref[j:j + 1, cols] += jnp.sum(ext_ref[off:off + GDN_ROWS, cols] * d_pre, axis=0, keepdims=True)
        dp_ref[:, 0:GDN_CONV_DIM] = _conv_dgrad_from_ext(dext_ref, w, GDN_CONV, GDN_ROWS)
        dext_ref[GDN_ROWS:, :] = dext_ref[0:SUBLANES, :]

    vec = lambda n: pl.BlockSpec((1, n), lambda i: (0, 0))
    outs = pl.pallas_call(
        body, name=name, grid=(nb,),
        in_specs=[pl.BlockSpec((GDN_ROWS, GDN_IN_PAD), lambda i: (nb - 1 - i, 0)),
                  pl.BlockSpec((SUBLANES, GDN_IN_PAD), lambda i: (jnp.maximum((nb - 1 - i) * hb - 1, 0), 0)),
                  pl.BlockSpec((SUBLANES, GDN_CONV_DIM), lambda i: (0, 0)),
                  vec(LANES), vec(LANES), vec(GDN_HEAD),
                  pl.BlockSpec((cpb, GDN_VH, GDN_HEAD, GDN_HEAD), lambda i: (nb - 1 - i, 0, 0, 0)),
                  pl.BlockSpec((cpb, GDN_VH, CHUNK, CHUNK), lambda i: (nb - 1 - i, 0, 0, 0)),
                  pl.BlockSpec((GDN_ROWS, GDN_V), lambda i: (nb - 1 - i, 0))],
        out_specs=[pl.BlockSpec((GDN_ROWS, GDN_IN_PAD), lambda i: (nb - 1 - i, 0)),
                   pl.BlockSpec((SUBLANES, GDN_CONV_DIM), lambda i: (0, 0)),
                   vec(LANES), vec(LANES), vec(GDN_HEAD)],
        out_shape=[jax.ShapeDtypeStruct((rows_total, GDN_IN_PAD), F32),
                   jax.ShapeDtypeStruct((SUBLANES, GDN_CONV_DIM), F32),
                   jax.ShapeDtypeStruct((1, LANES), F32), jax.ShapeDtypeStruct((1, LANES), F32),
                   jax.ShapeDtypeStruct((1, GDN_HEAD), F32)],
        scratch_shapes=[pltpu.VMEM((GDN_ROWS + SUBLANES, GDN_CONV_DIM), F32),
                        pltpu.VMEM((GDN_ROWS, GDN_CONV_DIM), F32),
                        pltpu.VMEM((GDN_ROWS, LANES), F32),
                        pltpu.VMEM((GDN_ROWS, LANES), F32),
                        pltpu.VMEM((GDN_VH, GDN_HEAD, GDN_HEAD), F32),
                        pltpu.VMEM((LANES, CHUNK), F32),
                        pltpu.VMEM((GDN_ROWS + SUBLANES, GDN_CONV_DIM), F32),
                        pltpu.VMEM((CHUNK, LANES), F32),
                        pltpu.VMEM((LANES, CHUNK), F32),
                        pltpu.VMEM((CHUNK, LANES), F32)],
        compiler_params=_cparams(("arbitrary",)),
    )(proj, proj, _pad_rows(conv_w), _gdn_lane_params(a_log), _gdn_lane_params(dt_bias), norm_w.reshape(1, -1),
      states, tmats, dmix)
    dproj, dcw, dalog, ddtb, dnw = outs
    return dproj, [dcw[:GDN_CONV], dalog[0, GDN_GL:GDN_GL + GDN_VH], ddtb[0, GDN_GL:GDN_GL + GDN_VH], dnw[0]]


def chip_exchange(src, *, scatter, name):
    piece_shape = src.shape[1:]

    def body(src_ref, out_ref, send_sems, recv_sems, local_sem):
        x, y, c = (lax.axis_index(a) for a in MESH_AXES)
        me = 2 * x + y

        def piece(j):
            return src_ref.at[j] if scatter else src_ref.at[c]

        local = pltpu.make_async_copy(piece(me), out_ref.at[me], local_sem)
        local.start()
        copies = []
        for k in range(1, N_SHARDS):
            px = 1 - x if k & 2 else x
            py = 1 - y if k & 1 else y
            cp = pltpu.make_async_remote_copy(
                src_ref=piece(2 * px + py), dst_ref=out_ref.at[me], send_sem=send_sems.at[k - 1],
                recv_sem=recv_sems.at[k - 1], device_id=(px, py, c), device_id_type=pl.DeviceIdType.MESH)
            cp.start()
            copies.append(cp)
        for cp in copies:
            cp.wait()
        local.wait()

    hbm = pl.BlockSpec(memory_space=pl.ANY)
    return pl.pallas_call(
        body, name=name, in_specs=[hbm], out_specs=hbm,
        out_shape=jax.ShapeDtypeStruct((N_SHARDS,) + tuple(piece_shape), src.dtype),
        scratch_shapes=[pltpu.SemaphoreType.DMA((N_SHARDS - 1,)), pltpu.SemaphoreType.DMA((N_SHARDS - 1,)),
                        pltpu.SemaphoreType.DMA],
    )(src)


def pair_exchange(src, *, add, name):
    lead, rows, cols = src.shape
    tr = _pick(rows, (512, 256))
    nblk = rows // tr
    n_steps = nblk if add else lead * nblk

    def body(c_ref, *refs):
        if add:
            mine_ref, send_ref, o_ref, recv_ref, send_sems, recv_sems, credit = refs
        else:
            send_ref, o_ref, recv_ref, send_sems, recv_sems, credit = refs
        step = pl.program_id(0) * nblk + pl.program_id(1)
        slot = step % 2
        sibling = (lax.axis_index("x"), lax.axis_index("y"), 1 - lax.axis_index("c"))

        @pl.when(step >= 2)
        def _():
            pl.semaphore_wait(credit, 1)

        cp = pltpu.make_async_remote_copy(
            src_ref=send_ref, dst_ref=recv_ref.at[slot], send_sem=send_sems.at[slot], recv_sem=recv_sems.at[slot],
            device_id=sibling, device_id_type=pl.DeviceIdType.MESH)
        cp.start()
        cp.wait_recv()
        if add:
            o_ref[...] = mine_ref[...] + recv_ref[slot]
        else:
            o_ref[c_ref[0]] = send_ref[...]
            o_ref[1 - c_ref[0]] = recv_ref[slot]
        cp.wait_send()

        @pl.when(step + 2 < n_steps)
        def _():
            pl.semaphore_signal(credit, 1, device_id=sibling, device_id_type=pl.DeviceIdType.MESH)

    flat = src.reshape(lead * rows, cols)
    if add:
        in_specs = [pl.BlockSpec((tr, cols), lambda s, i, c_ref: (c_ref[0] * nblk + i, 0)),
                    pl.BlockSpec((tr, cols), lambda s, i, c_ref: ((1 - c_ref[0]) * nblk + i, 0))]
        out_specs = pl.BlockSpec((tr, cols), lambda s, i, c_ref: (i, 0))
        out_shape = jax.ShapeDtypeStruct((rows, cols), src.dtype)
        grid, args = (1, nblk), (flat, flat)
    else:
        in_specs = [pl.BlockSpec((tr, cols), lambda s, i, c_ref: (s * nblk + i, 0))]
        out_specs = pl.BlockSpec((2, tr, cols), lambda s, i, c_ref: (s, i, 0))
        out_shape = jax.ShapeDtypeStruct((lead * 2, rows, cols), src.dtype)
        grid, args = (lead, nblk), (flat,)
    out = pl.pallas_call(
        body, name=name, out_shape=out_shape,
        grid_spec=pltpu.PrefetchScalarGridSpec(
            num_scalar_prefetch=1, grid=grid, in_specs=in_specs, out_specs=out_specs,
            scratch_shapes=[pltpu.VMEM((2, tr, cols), src.dtype), pltpu.SemaphoreType.DMA((2,)),
                            pltpu.SemaphoreType.DMA((2,)), pltpu.SemaphoreType.REGULAR]),
        compiler_params=_cparams(("arbitrary", "arbitrary")),
    )(lax.axis_index("c").astype(jnp.int32).reshape(1), *args)
    return out if add else out.reshape(lead, 2, rows, cols)


def sum_slots(buf, *, name):
    n, rows, cols = buf.shape
    tr = _pick(rows, (512, 256, 128))

    def body(b_ref, o_ref):
        acc = b_ref[0]
        for j in range(1, n):
            acc = acc + b_ref[j]
        o_ref[...] = acc

    return pl.pallas_call(
        body, name=name, grid=(rows // tr,), in_specs=[pl.BlockSpec((n, tr, cols), lambda i: (0, i, 0))],
        out_specs=pl.BlockSpec((tr, cols), lambda i: (i, 0)), out_shape=jax.ShapeDtypeStruct((rows, cols), F32),
        compiler_params=_cparams(("parallel",)),
    )(buf)


def adamw(w, g, m, v, *, name):
    shape = w.shape
    cols = shape[-1]
    rows = _size(shape) // cols
    w, g, m, v = (t.reshape(rows, cols) for t in (w, g, m, v))
    tr = 256 if rows % 256 == 0 else rows

    def body(w_ref, g_ref, m_ref, v_ref, d_ref, mo_ref, vo_ref):
        gv = g_ref[...]
        mn = ADAM_B1 * m_ref[...] + (1.0 - ADAM_B1) * gv
        vn = ADAM_B2 * v_ref[...] + (1.0 - ADAM_B2) * (gv * gv)
        m_hat = mn / (1.0 - ADAM_B1 ** ADAM_STEP)
        v_hat = vn / (1.0 - ADAM_B2 ** ADAM_STEP)
        d_ref[...] = -ADAM_LR * (m_hat / (jnp.sqrt(v_hat) + ADAM_EPS) + ADAM_WD * w_ref[...])
        mo_ref[...] = mn
        vo_ref[...] = vn

    blk = pl.BlockSpec((tr, cols), lambda i: (i, 0))
    shp = jax.ShapeDtypeStruct((rows, cols), F32)
    outs = pl.pallas_call(
        body, name=name, grid=(rows // tr,), in_specs=[blk] * 4, out_specs=[blk] * 3, out_shape=[shp] * 3,
        compiler_params=_cparams(("parallel",)),
    )(w, g, m, v)
    return [o.reshape(shape) for o in outs]


N_SHARDS = 4
FLAT_COLS = 1024
W_SPECS = (
    ("gdn_w_in", (2, 1024, 6176), 2), ("gdn_conv_w", (2, 4, 4096), 2), ("gdn_a_log", (2, 16), None),
    ("gdn_dt_bias", (2, 16), None), ("gdn_norm_w", (2, 128), None), ("gdn_w_out", (2, 2048, 1024), 1),
    ("sc_w_in", (1, 1024, 8192), 2), ("sc_conv_w", (1, 3, 2048), 2), ("sc_w_out", (1, 2048, 1024), 1),
    ("ssd_w_in", (1, 1024, 5152), 2), ("ssd_conv_w", (1, 4, 3072), 2), ("ssd_conv_b", (1, 3072), 1),
    ("ssd_a_log", (1, 32), None), ("ssd_dt_bias", (1, 32), None), ("ssd_d_skip", (1, 32), None),
    ("ssd_norm_w", (1, 2048), 1), ("ssd_w_out", (1, 2048, 1024), 1), ("ln_g", (4, 1024), None), ("ln_b", (4, 1024), None),
)


def _local_shape(shape, axis):
    return shape if axis is None else tuple(d // N_SHARDS if i == axis else d for i, d in enumerate(shape))


def _size(shape):
    n = 1
    for d in shape:
        n *= d
    return n


FLAT_USED = sum(_size(_local_shape(s, a)) for _, s, a in W_SPECS)
FLAT_ROWS = -(-FLAT_USED // (FLAT_COLS * 512)) * 512
FLAT_HALF = FLAT_ROWS // 2


def _pack(pieces):
    flat = jnp.concatenate([p.reshape(-1) for p in pieces] + [jnp.zeros((FLAT_ROWS * FLAT_COLS - FLAT_USED,), F32)])
    return flat.reshape(FLAT_ROWS, FLAT_COLS)


def _unpack(flat):
    flat = flat.reshape(-1)
    out, off = [], 0
    for _, shape, axis in W_SPECS:
        ls = _local_shape(shape, axis)
        out.append(flat[off:off + _size(ls)].reshape(ls))
        off += _size(ls)
    return out


def _shard_of(full, axis, s):
    if axis is None:
        return full
    n = full.shape[axis] // N_SHARDS
    return lax.slice_in_dim(full, s * n, (s + 1) * n, axis=axis)


def _adamw_all(weights, grads_flat, moms, vels):
    grads = _unpack(grads_flat)
    steps = [adamw(w, g, m, v, name="adamw") for w, g, m, v in zip(weights, grads, moms, vels)]
    return grads, [s[0] for s in steps], [s[1] for s in steps], [s[2] for s in steps]


SPLIT_ROWS = 128


def shard_split(w, n_real, *, name):
    rows, n_pad = w.shape
    ns = n_real // N_SHARDS

    def body(w_ref, o_ref):
        for s in range(N_SHARDS):
            o_ref[s] = w_ref[:, s * ns:(s + 1) * ns]

    return pl.pallas_call(
        body, name=name, grid=(rows // SPLIT_ROWS,),
        in_specs=[pl.BlockSpec((SPLIT_ROWS, n_pad), lambda i: (i, 0))],
        out_specs=pl.BlockSpec((N_SHARDS, SPLIT_ROWS, ns), lambda i: (0, i, 0)),
        out_shape=jax.ShapeDtypeStruct((N_SHARDS, rows, ns), F32), compiler_params=_cparams(("parallel",)),
    )(w)


def shard_merge(pieces, n_pad, *, name):
    _, rows, ns = pieces.shape
    n_real = ns * N_SHARDS

    def body(p_ref, o_ref):
        for s in range(N_SHARDS):
            o_ref[:, s * ns:(s + 1) * ns] = p_ref[s].astype(o_ref.dtype)
        if n_pad > n_real:
            o_ref[:, n_real:] = jnp.zeros((SPLIT_ROWS, n_pad - n_real), o_ref.dtype)

    return pl.pallas_call(
        body, name=name, grid=(rows // SPLIT_ROWS,),
        in_specs=[pl.BlockSpec((N_SHARDS, SPLIT_ROWS, ns), lambda i: (0, i, 0))],
        out_specs=pl.BlockSpec((SPLIT_ROWS, n_pad), lambda i: (i, 0)),
        out_shape=jax.ShapeDtypeStruct((rows, n_pad), MXU_DTYPE), compiler_params=_cparams(("parallel",)),
    )(pieces)


def _reduce_scatter(full_grads):
    def shard(g, spec, s):
        _, shape, axis = spec
        return g[:, s] if g.ndim == len(shape) + 1 else _shard_of(g, axis, s)

    by_shard = jnp.stack([_pack([shard(g, spec, s) for g, spec in zip(full_grads, W_SPECS)])
                          for s in range(N_SHARDS)])
    by_half = by_shard.reshape(N_SHARDS, 2, FLAT_HALF, FLAT_COLS).transpose(1, 0, 2, 3)
    by_half = by_half.reshape(2, N_SHARDS * FLAT_HALF, FLAT_COLS)
    pair_sum = pair_exchange(by_half, add=True, name="rs_pair")
    chips = chip_exchange(pair_sum.reshape(N_SHARDS, FLAT_HALF, FLAT_COLS), scatter=True, name="rs_chips")
    half = sum_slots(chips, name="rs_chip_sum")
    return pair_exchange(half[None], add=False, name="rs_halves").reshape(FLAT_ROWS, FLAT_COLS)


def _gather_weights(local_weights):
    halves = chip_exchange(_pack(local_weights).reshape(2, FLAT_HALF, FLAT_COLS), scatter=False, name="gather_chips")
    gathered = pair_exchange(halves, add=False, name="gather_pair").reshape(N_SHARDS, FLAT_ROWS, FLAT_COLS)
    per_shard = [_unpack(gathered[s]) for s in range(N_SHARDS)]
    full = []
    for i, (wname, shape, axis) in enumerate(W_SPECS):
        if axis is None:
            full.append(local_weights[i])
        elif wname in W_IN_PAD:
            pieces = jnp.stack([per_shard[s][i] for s in range(N_SHARDS)], axis=1)
            full.append([shard_merge(pieces[j], W_IN_PAD[wname], name="merge_" + wname) for j in range(shape[0])])
        else:
            full.append(jnp.concatenate([per_shard[s][i] for s in range(N_SHARDS)], axis=axis))
    return full


W_IN_PAD = {"gdn_w_in": GDN_IN_PAD, "sc_w_in": SC_IN, "ssd_w_in": SSD_IN_PAD}


def kernel(x, gdn_w_in, gdn_conv_w, gdn_a_log, gdn_dt_bias, gdn_norm_w, gdn_w_out, sc_w_in, sc_conv_w, sc_w_out, ssd_w_in, ssd_conv_w, ssd_conv_b, ssd_a_log, ssd_dt_bias, ssd_d_skip, ssd_norm_w, ssd_w_out, ln_g, ln_b, loss_target, m_gdn_w_in, m_gdn_conv_w, m_gdn_a_log, m_gdn_dt_bias, m_gdn_norm_w, m_gdn_w_out, m_sc_w_in, m_sc_conv_w, m_sc_w_out, m_ssd_w_in, m_ssd_conv_w, m_ssd_conv_b, m_ssd_a_log, m_ssd_dt_bias, m_ssd_d_skip, m_ssd_norm_w, m_ssd_w_out, m_ln_g, m_ln_b, v_gdn_w_in, v_gdn_conv_w, v_gdn_a_log, v_gdn_dt_bias, v_gdn_norm_w, v_gdn_w_out, v_sc_w_in, v_sc_conv_w, v_sc_w_out, v_ssd_w_in, v_ssd_conv_w, v_ssd_conv_b, v_ssd_a_log, v_ssd_dt_bias, v_ssd_d_skip, v_ssd_norm_w, v_ssd_w_out, v_ln_g, v_ln_b):
    weights = [gdn_w_in, gdn_conv_w, gdn_a_log, gdn_dt_bias, gdn_norm_w, gdn_w_out, sc_w_in, sc_conv_w, sc_w_out,
               ssd_w_in, ssd_conv_w, ssd_conv_b, ssd_a_log, ssd_dt_bias, ssd_d_skip, ssd_norm_w, ssd_w_out, ln_g, ln_b]
    moms = [m_gdn_w_in, m_gdn_conv_w, m_gdn_a_log, m_gdn_dt_bias, m_gdn_norm_w, m_gdn_w_out, m_sc_w_in, m_sc_conv_w,
            m_sc_w_out, m_ssd_w_in, m_ssd_conv_w, m_ssd_conv_b, m_ssd_a_log, m_ssd_dt_bias, m_ssd_d_skip, m_ssd_norm_w,
            m_ssd_w_out, m_ln_g, m_ln_b]
    vels = [v_gdn_w_in, v_gdn_conv_w, v_gdn_a_log, v_gdn_dt_bias, v_gdn_norm_w, v_gdn_w_out, v_sc_w_in, v_sc_conv_w,
            v_sc_w_out, v_ssd_w_in, v_ssd_conv_w, v_ssd_conv_b, v_ssd_a_log, v_ssd_dt_bias, v_ssd_d_skip, v_ssd_norm_w,
            v_ssd_w_out, v_ln_g, v_ln_b]
    full = dict(zip([n for n, _, _ in W_SPECS], _gather_weights(weights)))
    x0 = x[0]
    target = loss_target[0]

    layers = (("gdn", 0, GDN_IN_PAD, GDN_IN), ("sc", 0, SC_IN, SC_IN), ("ssd", 0, SSD_IN_PAD, SSD_IN), ("gdn", 1, GDN_IN_PAD, GDN_IN))

    def params(kind, j):
        if kind == "gdn":
            return [full["gdn_conv_w"][j], full["gdn_a_log"][j], full["gdn_dt_bias"][j], full["gdn_norm_w"][j]]
        if kind == "sc":
            return [full["sc_conv_w"][j]]
        return [full["ssd_conv_w"][j], full["ssd_conv_b"][j], full["ssd_a_log"][j], full["ssd_dt_bias"][j],
                full["ssd_d_skip"][j], full["ssd_norm_w"][j]]

    xs, saved = [x0], []
    for i, (kind, j, n_pad, _) in enumerate(layers):
        w_in = full[kind + "_w_in"][j]
        w_out = full[kind + "_w_out"][j].astype(MXU_DTYPE)
        proj = matmul(xs[i], w_in, name=kind + "_proj")
        if kind == "gdn":
            mix, states = gdn_fwd(proj, *params(kind, j), name="gdn_fwd")
        elif kind == "sc":
            mix, states = sc_fwd(proj, *params(kind, j), name="sc_fwd"), None
        else:
            mix, states = ssd_fwd(proj, *params(kind, j), name="ssd_fwd")
        y = matmul(mix, w_out, name=kind + "_out")
        saved.append((w_in, w_out, proj, mix, states, y))
        if i + 1 < DEPTH:
            xs.append(ln_fwd(xs[i], y, full["ln_g"][i], full["ln_b"][i], name="ln_fwd"))

    grads = {n: [None] * s[0] for n, s, _ in W_SPECS}
    dr, dg, db, loss_rows = ln_bwd(xs[DEPTH - 1], saved[DEPTH - 1][5], full["ln_g"][DEPTH - 1], b=full["ln_b"][DEPTH - 1],
                                   target=target, name="ln_bwd_loss")
    dx = None
    for i in reversed(range(DEPTH)):
        kind, j, _, n_in = layers[i]
        w_in, w_out, proj, mix, states, _ = saved[i]
        grads["ln_g"][i], grads["ln_b"][i] = dg[0], db[0]
        dmix = matmul(dr, w_out, tb=True, name=kind + "_dmix")
        grads[kind + "_w_out"][j] = matmul(mix, dr, ta=True, name=kind + "_dw_out")
        if kind == "gdn":
            dproj, (dcw, dalog, ddtb, dnw) = gdn_bwd(proj, *params(kind, j), states, dmix, name="gdn_bwd")
            grads["gdn_conv_w"][j], grads["gdn_a_log"][j], grads["gdn_dt_bias"][j], grads["gdn_norm_w"][j] = dcw, dalog, ddtb, dnw
        elif kind == "sc":
            dproj, dcw = sc_bwd(proj, *params(kind, j), dmix, name="sc_bwd")
            grads["sc_conv_w"][j] = dcw[:SC_CONV]
        else:
            dproj, (dcw, dcb, dalog, ddtb, ddsk, dnw) = ssd_bwd(proj, *params(kind, j), states, dmix, name="ssd_bwd")
            grads["ssd_conv_w"][j], grads["ssd_conv_b"][j], grads["ssd_a_log"][j] = dcw, dcb, dalog
            grads["ssd_dt_bias"][j], grads["ssd_d_skip"][j], grads["ssd_norm_w"][j] = ddtb, ddsk, dnw
        grads[kind + "_w_in"][j] = shard_split(matmul(xs[i], dproj, ta=True, name=kind + "_dw_in"), n_in, name="split_" + kind)
        dx = matmul(dproj, w_in, tb=True, add=dr, add_scale=ALPHA, name=kind + "_dx")
        if i > 0:
            dr, dg, db = ln_bwd(xs[i - 1], saved[i - 1][5], full["ln_g"][i - 1], dx, name="ln_bwd")

    full_grads = [jnp.stack(grads[n]) for n, _, _ in W_SPECS]
    grads_flat = _reduce_scatter(full_grads)
    g_out, d_out, m_out, v_out = _adamw_all(weights, grads_flat, moms, vels)
    loss = lax.psum(loss_rows[0, 0], MESH_AXES)
    return (loss, dx[None], *g_out, *d_out, *m_out, *v_out)
```

```python
import functools

import jax
import jax.numpy as jnp
from jax import lax
from jax.experimental import pallas as pl
from jax.experimental.pallas import tpu as pltpu

F32 = jnp.float32
MXU_DTYPE = jnp.bfloat16

D_MODEL = 1024
DEPTH = 4
D_INNER = 2048
CHUNK = 64
LANES = 128
SUBLANES = 8
VMEM_LIMIT = 56 * 1024 * 1024

GDN_HEAD = 128
GDN_VH = 16
GDN_QKH = 8
GDN_QK = 1024
GDN_V = 2048
GDN_CONV = 4
GDN_CONV_DIM = 4096
GDN_IN = 6176
GDN_IN_PAD = 6272

SC_W = 2048
SC_CONV = 3
SC_IN = 8192

SSD_P = 64
SSD_H = 32
SSD_G = 4
SSD_S = 128
SSD_CONV = 4
SSD_CONV_DIM = 3072
SSD_IN = 5152
SSD_IN_PAD = 5376

ALPHA = (2 * DEPTH) ** 0.25
RMS_EPS = 1e-6
LN_EPS = 1e-5
L2_EPS = 1e-6

ADAM_LR = 0.001
ADAM_B1 = 0.9
ADAM_B2 = 0.999
ADAM_EPS = 1e-08
ADAM_WD = 0.01
ADAM_STEP = 10

MESH_AXES = ("x", "y", "c")


def _cparams(sem):
    return pltpu.CompilerParams(dimension_semantics=sem, vmem_limit_bytes=VMEM_LIMIT)


def _pick(n, prefs):
    for p in prefs:
        if n % p == 0:
            return p
    return n


def _dot(a, b, dims=(((1,), (0,)), ((), ()))):
    return lax.dot_general(a.astype(MXU_DTYPE), b.astype(MXU_DTYPE), dims, preferred_element_type=F32)


def _dot_nt(a, b):
    return _dot(a, b, (((1,), (1,)), ((), ())))


def _dot_tn(a, b):
    return _dot(a, b, (((0,), (0,)), ((), ())))


NN = (((1,), (0,)), ((), ()))
NT = (((1,), (1,)), ((), ()))
TN = (((0,), (0,)), ((), ()))


def _mxu(a, b, dims):
    return lax.dot_general(a, b, dims, preferred_element_type=F32)


def _split(x, pieces):
    out, r = [], x
    for i in range(pieces):
        p = r.astype(jnp.bfloat16)
        out.append(p)
        if i + 1 < pieces:
            r = r - p.astype(F32)
    return out


def _dot_x3(a, b, dims=NN):
    (ah, al), (bh, bl) = _split(a, 2), _split(b, 2)
    return (_mxu(al, bh, dims) + _mxu(ah, bl, dims)) + _mxu(ah, bh, dims)


def _sel(m, x, dims=NN):
    mb = m.astype(jnp.bfloat16)
    x1, x2, x3 = _split(x, 3)
    return (_mxu(mb, x3, dims) + _mxu(mb, x2, dims)) + _mxu(mb, x1, dims)


def _sel_nt(m, x):
    return _sel(m, x, NT)


def _xsel(x, m, dims=NN):
    mb = m.astype(jnp.bfloat16)
    x1, x2, x3 = _split(x, 3)
    return (_mxu(x3, mb, dims) + _mxu(x2, mb, dims)) + _mxu(x1, mb, dims)


def _xsel_nt(x, m):
    return _xsel(x, m, NT)


def _iota(shape, dim):
    return lax.broadcasted_iota(jnp.int32, shape, dim)


def _sigmoid(x):
    return 0.5 * jnp.tanh(0.5 * x) + 0.5


def _silu(x):
    return x * _sigmoid(x)


def _dsilu(x):
    s = _sigmoid(x)
    return s * (1.0 + x * (1.0 - s))


def _softplus(x):
    return jnp.maximum(x, 0.0) + jnp.log(1.0 + jnp.exp(-jnp.abs(x)))


def matmul(a, b, *, ta=False, tb=False, add=None, add_scale=1.0, name):
    if ta:
        kdim, m = a.shape
    else:
        m, kdim = a.shape
    n = b.shape[0] if tb else b.shape[1]
    assert (b.shape[1] if tb else b.shape[0]) == kdim
    tm = _pick(m, (1024, 896, 768, 512)) if ta else _pick(m, (1024, 512, 256, 128))
    tn = _pick(n, (1024, 896, 768, 512, 256, 128))
    tk = _pick(kdim, (1024, 512, 256)) if ta else _pick(kdim, (1024, 896, 768, 512))
    nk = kdim // tk
    dims = (((0 if ta else 1,), (1 if tb else 0,)), ((), ()))

    def body(a_ref, b_ref, *rest):
        o_ref = rest[-1]
        k = pl.program_id(2)
        part = _dot(a_ref[...], b_ref[...], dims)

        @pl.when(k == 0)
        def _():
            o_ref[...] = part if add is None else part + add_scale * rest[0][...]

        @pl.when(k > 0)
        def _():
            o_ref[...] += part

    a_spec = pl.BlockSpec((tk, tm), lambda i, j, k: (k, i)) if ta else pl.BlockSpec((tm, tk), lambda i, j, k: (i, k))
    b_spec = pl.BlockSpec((tn, tk), lambda i, j, k: (j, k)) if tb else pl.BlockSpec((tk, tn), lambda i, j, k: (k, j))
    o_spec = pl.BlockSpec((tm, tn), lambda i, j, k: (i, j))
    in_specs = [a_spec, b_spec] + ([] if add is None else [o_spec])
    args = (a, b) + (() if add is None else (add,))
    return pl.pallas_call(
        body, name=name, grid=(m // tm, n // tn, nk), in_specs=in_specs, out_specs=o_spec,
        out_shape=jax.ShapeDtypeStruct((m, n), F32),
        compiler_params=_cparams(("parallel", "parallel", "arbitrary")),
    )(*args)


LN_ROWS = 512


def _ln_stats(x, y):
    r = ALPHA * x + y
    mu = jnp.mean(r, axis=-1, keepdims=True)
    rc = r - mu
    var = jnp.mean(rc * rc, axis=-1, keepdims=True)
    rstd = lax.rsqrt(var + LN_EPS)
    return rc * rstd, rstd


def ln_fwd(x, y, g, b, *, name):
    rows, d = x.shape

    def body(x_ref, y_ref, g_ref, b_ref, o_ref):
        xhat, _ = _ln_stats(x_ref[...], y_ref[...])
        o_ref[...] = xhat * g_ref[...] + b_ref[...]

    blk = pl.BlockSpec((LN_ROWS, d), lambda i: (i, 0))
    vec = pl.BlockSpec((1, d), lambda i: (0, 0))
    return pl.pallas_call(
        body, name=name, grid=(rows // LN_ROWS,), in_specs=[blk, blk, vec, vec], out_specs=blk,
        out_shape=jax.ShapeDtypeStruct((rows, d), F32), compiler_params=_cparams(("parallel",)),
    )(x, y, g.reshape(1, d), b.reshape(1, d))


def ln_bwd(x, y, g, dxn=None, *, b=None, target=None, name):
    rows, d = x.shape
    final = target is not None

    def body(x_ref, y_ref, g_ref, *rest):
        if final:
            b_ref, t_ref, dr_ref, dg_ref, db_ref, loss_ref = rest
        else:
            dxn_ref, dr_ref, dg_ref, db_ref = rest
        i = pl.program_id(0)
        xhat, rstd = _ln_stats(x_ref[...], y_ref[...])
        gv = g_ref[...]
        if final:
            err = xhat * gv + b_ref[...] - t_ref[...]
            dxn_v = err * (1.0 / d)
            part = 0.5 * jnp.sum(jnp.mean(err * err, axis=-1, keepdims=True), axis=0, keepdims=True)
        else:
            dxn_v = dxn_ref[...]
        dxh = dxn_v * gv
        m1 = jnp.mean(dxh, axis=-1, keepdims=True)
        m2 = jnp.mean(dxh * xhat, axis=-1, keepdims=True)
        dr_ref[...] = rstd * (dxh - m1 - xhat * m2)

        @pl.when(i == 0)
        def _():
            dg_ref[...] = jnp.zeros_like(dg_ref)
            db_ref[...] = jnp.zeros_like(db_ref)
            if final:
                loss_ref[...] = jnp.zeros_like(loss_ref)

        dg_ref[...] += jnp.sum(dxn_v * xhat, axis=0, keepdims=True)
        db_ref[...] += jnp.sum(dxn_v, axis=0, keepdims=True)
        if final:
            loss_ref[...] += jnp.broadcast_to(part, loss_ref.shape)

    blk = pl.BlockSpec((LN_ROWS, d), lambda i: (i, 0))
    vec = pl.BlockSpec((1, d), lambda i: (0, 0))
    lvec = pl.BlockSpec((1, LANES), lambda i: (0, 0))
    out_shape = [jax.ShapeDtypeStruct((rows, d), F32), jax.ShapeDtypeStruct((1, d), F32), jax.ShapeDtypeStruct((1, d), F32)]
    out_specs = [blk, vec, vec]
    if final:
        in_specs = [blk, blk, vec, vec, blk]
        args = (x, y, g.reshape(1, d), b.reshape(1, d), target)
        out_shape.append(jax.ShapeDtypeStruct((1, LANES), F32))
        out_specs.append(lvec)
    else:
        in_specs = [blk, blk, vec, blk]
        args = (x, y, g.reshape(1, d), dxn)
    return pl.pallas_call(
        body, name=name, grid=(rows // LN_ROWS,), in_specs=in_specs, out_specs=out_specs, out_shape=out_shape,
        compiler_params=_cparams(("arbitrary",)),
    )(*args)


def _rows_from(ref, off, rows, cols=slice(None)):
    r = off % SUBLANES
    if r == 0:
        return ref[off:off + rows, cols]
    window = ref[off - r:off - r + rows + SUBLANES, cols]
    return pltpu.roll(window, rows + SUBLANES - r, axis=0)[:rows]


def _conv_from_ext(ext_ref, w, width, rows, cols=slice(None)):
    out = None
    for j in range(width):
        term = _rows_from(ext_ref, SUBLANES - (width - 1) + j, rows, cols) * w[j:j + 1, cols]
        out = term if out is None else out + term
    return out


def _conv_dgrad_from_ext(dext_ref, w, width, rows, cols):
    out = None
    for j in range(width):
        term = _rows_from(dext_ref, (width - 1) - j, rows, cols) * w[j:j + 1, cols]
        out = term if out is None else out + term
    return out


CONV_COLS = 256


SC_ROWS = 128


def sc_fwd(proj, conv_w, *, name):
    rows = proj.shape[0]
    nb = rows // SC_ROWS
    hb = SC_ROWS // SUBLANES

    def body(p_ref, halo_ref, w_ref, o_ref, ext_ref):
        i = pl.program_id(0)
        w = w_ref[...]
        for c0 in range(0, SC_W, CONV_COLS):
            cols, bc, cc, zc = (slice(k * SC_W + c0, k * SC_W + c0 + CONV_COLS) for k in range(4))
            ext_ref[0:SUBLANES, cols] = jnp.where(i == 0, 0.0, halo_ref[:, cc] * halo_ref[:, cols])
            ext_ref[SUBLANES:, cols] = p_ref[:, cc] * p_ref[:, cols]
            cv = _conv_from_ext(ext_ref, w, SC_CONV, SC_ROWS, cols)
            o_ref[:, cols] = p_ref[:, bc] * cv * _silu(p_ref[:, zc])

    return pl.pallas_call(
        body, name=name, grid=(nb,),
        in_specs=[pl.BlockSpec((SC_ROWS, SC_IN), lambda i: (i, 0)),
                  pl.BlockSpec((SUBLANES, SC_IN), lambda i: (jnp.maximum(i * hb - 1, 0), 0)),
                  pl.BlockSpec((SUBLANES, SC_W), lambda i: (0, 0))],
        out_specs=pl.BlockSpec((SC_ROWS, SC_W), lambda i: (i, 0)),
        out_shape=jax.ShapeDtypeStruct((rows, SC_W), F32),
        scratch_shapes=[pltpu.VMEM((SC_ROWS + SUBLANES, SC_W), F32)],
        compiler_params=_cparams(("parallel",)),
    )(proj, proj, _pad_rows(conv_w))


def sc_bwd(proj, conv_w, dmix, *, name):
    rows = proj.shape[0]
    nb = rows // SC_ROWS
    hb = SC_ROWS // SUBLANES

    def body(p_ref, halo_ref, w_ref, dm_ref, dp_ref, dw_ref, ext_ref, dext_ref):
        i = pl.program_id(0)
        blk = nb - 1 - i
        w = w_ref[...]

        @pl.when(i == 0)
        def _():
            dext_ref[SC_ROWS:, :] = jnp.zeros((SUBLANES, SC_W), F32)
            dw_ref[...] = jnp.zeros_like(dw_ref)

        for c0 in range(0, SC_W, CONV_COLS):
            cols, bc, cc, zc = (slice(k * SC_W + c0, k * SC_W + c0 + CONV_COLS) for k in range(4))
            h, bg, cg, z = p_ref[:, cols], p_ref[:, bc], p_ref[:, cc], p_ref[:, zc]
            ext_ref[0:SUBLANES, cols] = jnp.where(blk == 0, 0.0, halo_ref[:, cc] * halo_ref[:, cols])
            ext_ref[SUBLANES:, cols] = cg * h
            taps = [_rows_from(ext_ref, SUBLANES - (SC_CONV - 1) + j, SC_ROWS, cols) for j in range(SC_CONV)]
            cv = None
            for j in range(SC_CONV):
                term = taps[j] * w[j:j + 1, cols]
                cv = term if cv is None else cv + term
            dm = dm_ref[:, cols]
            dy = dm * _silu(z)
            dp_ref[:, zc] = dm * bg * cv * _dsilu(z)
            dp_ref[:, bc] = dy * cv
            dcv = dy * bg
            dext_ref[0:SC_ROWS, cols] = dcv
            du = _conv_dgrad_from_ext(dext_ref, w, SC_CONV, SC_ROWS, cols)
            dp_ref[:, cols] = du * cg
            dp_ref[:, cc] = du * h
            for j in range(SC_CONV):
                dw_ref[j:j + 1, cols] += jnp.sum(taps[j] * dcv, axis=0, keepdims=True)
            dext_ref[SC_ROWS:, cols] = dcv[0:SUBLANES, :]

    return pl.pallas_call(
        body, name=name, grid=(nb,),
        in_specs=[pl.BlockSpec((SC_ROWS, SC_IN), lambda i: (nb - 1 - i, 0)),
                  pl.BlockSpec((SUBLANES, SC_IN), lambda i: (jnp.maximum((nb - 1 - i) * hb - 1, 0), 0)),
                  pl.BlockSpec((SUBLANES, SC_W), lambda i: (0, 0)),
                  pl.BlockSpec((SC_ROWS, SC_W), lambda i: (nb - 1 - i, 0))],
        out_specs=[pl.BlockSpec((SC_ROWS, SC_IN), lambda i: (nb - 1 - i, 0)),
                   pl.BlockSpec((SUBLANES, SC_W), lambda i: (0, 0))],
        out_shape=[jax.ShapeDtypeStruct((rows, SC_IN), F32), jax.ShapeDtypeStruct((SUBLANES, SC_W), F32)],
        scratch_shapes=[pltpu.VMEM((SC_ROWS + SUBLANES, SC_W), F32), pltpu.VMEM((SC_ROWS + SUBLANES, SC_W), F32)],
        compiler_params=_cparams(("arbitrary",)),
    )(proj, proj, _pad_rows(conv_w), dmix)


def _pad_rows(w, rows=SUBLANES):
    return jnp.pad(w, ((0, rows - w.shape[0]), (0, 0)))


def _pad_lanes(v, lanes=LANES):
    v = v.reshape(1, -1)
    return jnp.pad(v, ((0, 0), (0, lanes - v.shape[1])))


def _tri(n, lower=True):
    r, c = _iota((n, n), 0), _iota((n, n), 1)
    return jnp.where((c <= r) if lower else (c >= r), 1.0, 0.0)


def _eye(n):
    return jnp.where(_iota((n, n), 0) == _iota((n, n), 1), 1.0, 0.0)


def _head_expand(n, width):
    return jnp.where(_iota((LANES, n), 1) // width == _iota((LANES, n), 0), 1.0, 0.0)


def _col(v, h):
    return jnp.sum(jnp.where(_iota(v.shape, 1) == h, v, 0.0), axis=1, keepdims=True)


def _row(v, r):
    return jnp.sum(jnp.where(_iota(v.shape, 0) == r, v, 0.0), axis=0, keepdims=True)


def _expand_row(v, e):
    return jnp.max(_xsel(jnp.broadcast_to(v, (SUBLANES, LANES)), e), axis=0, keepdims=True)


SSD_ROWS = 128
SSD_X0 = D_INNER
SSD_DT0 = D_INNER + SSD_CONV_DIM
SSD_B0 = D_INNER
SSD_C0 = D_INNER + SSD_G * SSD_S
SSD_GW = D_INNER // SSD_G
SSD_HG = SSD_H // SSD_G


def _ssd_prologue(blk, p_ref, halo_ref, cw_ref, cb_ref, dtb_ref, ext_ref, xbc_ref, dt_ref):
    ext_ref[0:SUBLANES, :] = jnp.where(blk == 0, 0.0, halo_ref[:, SSD_X0:SSD_DT0])
    ext_ref[SUBLANES:, :] = p_ref[:, SSD_X0:SSD_DT0]
    w = cw_ref[...]
    for c0 in range(0, SSD_CONV_DIM, CONV_COLS):
        cols = slice(c0, c0 + CONV_COLS)
        xbc_ref[:, cols] = _silu(_conv_from_ext(ext_ref, w, SSD_CONV, SSD_ROWS, cols) + cb_ref[:, cols])
    dt_ref[...] = _softplus(p_ref[:, SSD_DT0:SSD_DT0 + LANES] + dtb_ref[...])


def _ssd_chunk_decays(dt_c, a_row, ltri, eye_l, act_ref):
    da = dt_c * a_row
    ac = _sel(ltri,da)
    act_ref[...] = _sel_nt(eye_l,ac)
    ac_last = _row(ac, CHUNK - 1)
    return ac, jnp.exp(ac_last - ac), jnp.exp(ac), jnp.exp(ac_last)


def _ssd_seg(ac, act_ref, h, causal):
    return jnp.where(causal, jnp.exp(jnp.minimum(_col(ac, h) - act_ref[pl.ds(h, 1), :], 0.0)), 0.0)


def _ssd_half(pair, e):
    upper = _iota(pair.shape, 1) >= SSD_P
    return jnp.where(upper if e % 2 else jnp.logical_not(upper), pair, 0.0)


def _ssd_group_fwd(g, xbc_ref, rows, dt_exp, tail_exp, cdec_exp, ac, act_ref, s_g, causal):
    gl = slice(g * SSD_GW, (g + 1) * SSD_GW)
    bg = xbc_ref[rows, SSD_B0 + g * SSD_S:SSD_B0 + (g + 1) * SSD_S]
    cg = xbc_ref[rows, SSD_C0 + g * SSD_S:SSD_C0 + (g + 1) * SSD_S]
    xdt = xbc_ref[rows, gl] * dt_exp[:, gl]
    cb = _dot_nt(cg, bg)
    cs = _dot(cg, s_g)
    segs = [_ssd_seg(ac, act_ref, g * SSD_HG + e, causal) for e in range(SSD_HG)]
    gms = [seg * cb for seg in segs]
    xps = [xdt[:, p * LANES:(p + 1) * LANES] for p in range(SSD_HG // 2)]
    parts = [_dot(gms[e], _ssd_half(xps[e // 2], e)) for e in range(SSD_HG)]
    yd = jnp.concatenate([parts[2 * p] + parts[2 * p + 1] for p in range(SSD_HG // 2)], axis=1)
    st = _dot_tn(bg, xdt * tail_exp[:, gl])
    return yd + cs * cdec_exp[:, gl], st, bg, cg, cb, xdt, cs, segs, gms


def ssd_fwd(proj, conv_w, conv_b, a_log, dt_bias, d_skip, norm_w, *, name):
    rows_total = proj.shape[0]
    nb = rows_total // SSD_ROWS
    hb = SSD_ROWS // SUBLANES
    cpb = SSD_ROWS // CHUNK

    def body(p_ref, halo_ref, cw_ref, cb_ref, alog_ref, dtb_ref, dsk_ref, nw_ref, mix_ref, st_ref,
             ext_ref, xbc_ref, dt_ref, s_ref, act_ref):
        i = pl.program_id(0)

        @pl.when(i == 0)
        def _():
            s_ref[...] = jnp.zeros_like(s_ref)

        _ssd_prologue(i, p_ref, halo_ref, cw_ref, cb_ref, dtb_ref, ext_ref, xbc_ref, dt_ref)
        a_row = -jnp.exp(alog_ref[...])
        expand = _head_expand(D_INNER, SSD_P)
        dsk_exp = _expand_row(dsk_ref[...], expand)
        ltri, eye_l = _tri(CHUNK), _eye(LANES)
        causal = _iota((CHUNK, CHUNK), 1) <= _iota((CHUNK, CHUNK), 0)

        def chunk(c, carry):
            rows = pl.ds(pl.multiple_of(c * CHUNK, CHUNK), CHUNK)
            dt_c = dt_ref[rows, :]
            ac, tail, cdec, tot = _ssd_chunk_decays(dt_c, a_row, ltri, eye_l, act_ref)
            dt_exp = _xsel(dt_c, expand)
            tail_exp = _xsel(tail, expand)
            cdec_exp = _xsel(cdec, expand)
            tot_exp = _expand_row(tot, expand)
            for g in range(SSD_G):
                gl = slice(g * SSD_GW, (g + 1) * SSD_GW)
                s_g = s_ref[g]
                st_ref[c, g] = s_g
                y, st = _ssd_group_fwd(g, xbc_ref, rows, dt_exp, tail_exp, cdec_exp, ac, act_ref, s_g, causal)[:2]
                s_ref[g] = s_g * tot_exp[:, gl] + st
                y = (y + dsk_exp[:, gl] * xbc_ref[rows, gl]) * _silu(p_ref[rows, gl])
                r = lax.rsqrt(jnp.mean(y * y, axis=-1, keepdims=True) + RMS_EPS)
                mix_ref[rows, gl] = y * r * nw_ref[:, gl]
            return carry

        lax.fori_loop(0, cpb, chunk, 0)

    vec = lambda n: pl.BlockSpec((1, n), lambda i: (0, 0))
    return pl.pallas_call(
        body, name=name, grid=(nb,),
        in_specs=[pl.BlockSpec((SSD_ROWS, SSD_IN_PAD), lambda i: (i, 0)),
                  pl.BlockSpec((SUBLANES, SSD_IN_PAD), lambda i: (jnp.maximum(i * hb - 1, 0), 0)),
                  pl.BlockSpec((SUBLANES, SSD_CONV_DIM), lambda i: (0, 0)),
                  vec(SSD_CONV_DIM), vec(LANES), vec(LANES), vec(LANES), vec(D_INNER)],
        out_specs=[pl.BlockSpec((SSD_ROWS, D_INNER), lambda i: (i, 0)),
                   pl.BlockSpec((cpb, SSD_G, SSD_S, SSD_GW), lambda i: (i, 0, 0, 0))],
        out_shape=[jax.ShapeDtypeStruct((rows_total, D_INNER), F32),
                   jax.ShapeDtypeStruct((rows_total // CHUNK, SSD_G, SSD_S, SSD_GW), F32)],
        scratch_shapes=[pltpu.VMEM((SSD_ROWS + SUBLANES, SSD_CONV_DIM), F32),
                        pltpu.VMEM((SSD_ROWS, SSD_CONV_DIM), F32),
                        pltpu.VMEM((SSD_ROWS, LANES), F32),
                        pltpu.VMEM((SSD_G, SSD_S, SSD_GW), F32),
                        pltpu.VMEM((LANES, CHUNK), F32)],
        compiler_params=_cparams(("arbitrary",)),
    )(proj, proj, _pad_rows(conv_w), conv_b.reshape(1, -1), _pad_lanes(a_log), _pad_lanes(dt_bias),
      _pad_lanes(d_skip), norm_w.reshape(1, -1))


def ssd_bwd(proj, conv_w, conv_b, a_log, dt_bias, d_skip, norm_w, states, dmix, *, name):
    rows_total = proj.shape[0]
    nb = rows_total // SSD_ROWS
    hb = SSD_ROWS // SUBLANES
    cpb = SSD_ROWS // CHUNK

    def body(p_ref, halo_ref, cw_ref, cb_ref, alog_ref, dtb_ref, dsk_ref, nw_ref, st_ref, dm_ref,
             dp_ref, dcw_ref, dcb_ref, dalog_ref, ddtb_ref, ddsk_ref, dnw_ref,
             ext_ref, xbc_ref, dt_ref, ds_ref, act_ref, dext_ref, dac_ref, dact_ref, ddskw_ref):
        i = pl.program_id(0)
        blk = nb - 1 - i

        @pl.when(i == 0)
        def _():
            ds_ref[...] = jnp.zeros_like(ds_ref)
            dext_ref[SSD_ROWS:, :] = jnp.zeros((SUBLANES, SSD_CONV_DIM), F32)
            ddskw_ref[...] = jnp.zeros_like(ddskw_ref)
            for r in (dcw_ref, dcb_ref, dalog_ref, ddtb_ref, ddsk_ref, dnw_ref):
                r[...] = jnp.zeros_like(r)

        _ssd_prologue(blk, p_ref, halo_ref, cw_ref, cb_ref, dtb_ref, ext_ref, xbc_ref, dt_ref)
        a_row = -jnp.exp(alog_ref[...])
        expand = _head_expand(D_INNER, SSD_P)
        dsk_exp = _expand_row(dsk_ref[...], expand)
        ltri, utri, eye_l, eye_c = _tri(CHUNK), _tri(CHUNK, lower=False), _eye(LANES), _eye(CHUNK)
        causal = _iota((CHUNK, CHUNK), 1) <= _iota((CHUNK, CHUNK), 0)
        dp_ref[:, SSD_DT0 + LANES:] = jnp.zeros((SSD_ROWS, SSD_IN_PAD - SSD_DT0 - LANES), F32)

        def chunk(cc, carry):
            c = cpb - 1 - cc
            rows = pl.ds(pl.multiple_of(c * CHUNK, CHUNK), CHUNK)
            dt_c = dt_ref[rows, :]
            ac, tail, cdec, tot = _ssd_chunk_decays(dt_c, a_row, ltri, eye_l, act_ref)
            dt_exp = _xsel(dt_c, expand)
            tail_exp = _xsel(tail, expand)
            cdec_exp = _xsel(cdec, expand)
            tot_exp = _expand_row(tot, expand)
            dac_ref[...] = jnp.zeros_like(dac_ref)
            dact_ref[...] = jnp.zeros_like(dact_ref)
            d_cdec = jnp.zeros((CHUNK, LANES), F32)
            d_tail = jnp.zeros((CHUNK, LANES), F32)
            d_dt = jnp.zeros((CHUNK, LANES), F32)
            d_tot = jnp.zeros((1, LANES), F32)
            for g in range(SSD_G):
                gl = slice(g * SSD_GW, (g + 1) * SSD_GW)
                ex_g = expand[:, gl]
                s_g = st_ref[c, g]
                y, _, bg, cg, cb, xdt, cs, segs, gms = _ssd_group_fwd(g, xbc_ref, rows, dt_exp, tail_exp, cdec_exp, ac, act_ref, s_g, causal)
                xs = xbc_ref[rows, gl]
                z = p_ref[rows, gl]
                sz = _silu(z)
                y2 = y + dsk_exp[:, gl] * xs
                yg = y2 * sz
                r = lax.rsqrt(jnp.mean(yg * yg, axis=-1, keepdims=True) + RMS_EPS)
                yn = yg * r
                dm = dm_ref[rows, gl]
                dnw_ref[:, gl] += jnp.sum(dm * yn, axis=0, keepdims=True)
                dyn = dm * nw_ref[:, gl]
                dyg = r * (dyn - yn * jnp.mean(dyn * yn, axis=-1, keepdims=True))
                dp_ref[rows, gl] = dyg * y2 * _dsilu(z)
                dy = dyg * sz
                ddskw_ref[:, gl] += jnp.sum(dy * xs, axis=0, keepdims=True)
                ds_g = ds_ref[g]
                dyc = dy * cdec_exp[:, gl]
                ds_ref[g] = ds_g * tot_exp[:, gl] + _dot_tn(cg, dyc)
                sds = jnp.broadcast_to(jnp.sum(s_g * ds_g, axis=0, keepdims=True), (SUBLANES, SSD_GW))
                d_tot = d_tot + jnp.max(_xsel_nt(sds, ex_g), axis=0, keepdims=True)
                dcg = _dot_nt(dyc, s_g)
                d_cdec = d_cdec + _xsel_nt(dy * cs, ex_g)
                xdtd = xdt * tail_exp[:, gl]
                d_xdtd = _dot(bg, ds_g)
                dbg = _dot_nt(xdtd, ds_g)
                d_tail = d_tail + _xsel_nt(d_xdtd * xdt, ex_g)
                heads = range(SSD_HG)
                dy_h = [_ssd_half(dy[:, (e // 2) * LANES:(e // 2 + 1) * LANES], e) for e in heads]
                back = [_dot_tn(gms[e], dy_h[e]) for e in heads]
                dg_m = [jnp.where(causal, _dot_nt(dy_h[e], xdt[:, (e // 2) * LANES:(e // 2 + 1) * LANES]), 0.0) for e in heads]
                d_cb = None
                for e in heads:
                    h = g * SSD_HG + e
                    term = dg_m[e] * segs[e]
                    d_cb = term if d_cb is None else d_cb + term
                    em = dg_m[e] * gms[e]
                    dac_ref[...] += jnp.where(_iota((CHUNK, LANES), 1) == h, jnp.sum(em, axis=1, keepdims=True), 0.0)
                    dact_ref[h:h + 1, :] = jnp.sum(em, axis=0, keepdims=True)
                dcg = dcg + _dot(d_cb, bg)
                dbg = dbg + _dot_tn(d_cb, cg)
                d_xdt = d_xdtd * tail_exp[:, gl] + jnp.concatenate(
                    [back[2 * p] + back[2 * p + 1] for p in range(SSD_HG // 2)], axis=1)
                d_dt = d_dt + _xsel_nt(d_xdt * xs, ex_g)
                dext_ref[rows, gl] = d_xdt * dt_exp[:, gl] + dy * dsk_exp[:, gl]
                dext_ref[rows, SSD_B0 + g * SSD_S:SSD_B0 + (g + 1) * SSD_S] = dbg
                dext_ref[rows, SSD_C0 + g * SSD_S:SSD_C0 + (g + 1) * SSD_S] = dcg
            d_ac = dac_ref[...] - _sel_nt(eye_c,dact_ref[...]) + d_cdec * cdec - d_tail * tail
            d_last = jnp.sum(d_tail * tail, axis=0, keepdims=True) + d_tot * tot
            d_ac = jnp.where(_iota((CHUNK, LANES), 0) == CHUNK - 1, d_ac + d_last, d_ac)
            d_da = _sel(utri,d_ac)
            d_dt = d_dt + d_da * a_row
            dalog_ref[...] += jnp.sum(d_da * dt_c, axis=0, keepdims=True) * a_row
            d_raw = d_dt * _sigmoid(p_ref[rows, SSD_DT0:SSD_DT0 + LANES] + dtb_ref[...])
            d_raw = jnp.where(_iota((CHUNK, LANES), 1) < SSD_H, d_raw, 0.0)
            ddtb_ref[...] += jnp.sum(d_raw, axis=0, keepdims=True)
            dp_ref[rows, SSD_DT0:SSD_DT0 + LANES] = d_raw
            return carry

        lax.fori_loop(0, cpb, chunk, 0)
        w = cw_ref[...]
        for c0 in range(0, SSD_CONV_DIM, CONV_COLS):
            cols = slice(c0, c0 + CONV_COLS)
            taps = [_rows_from(ext_ref, SUBLANES - (SSD_CONV - 1) + j, SSD_ROWS, cols) for j in range(SSD_CONV)]
            pre = cb_ref[:, cols]
            for j in range(SSD_CONV):
                pre = pre + taps[j] * w[j:j + 1, cols]
            d_pre = dext_ref[0:SSD_ROWS, cols] * _dsilu(pre)
            dext_ref[0:SSD_ROWS, cols] = d_pre
            dcb_ref[:, cols] += jnp.sum(d_pre, axis=0, keepdims=True)
            for j in range(SSD_CONV):
                dcw_ref[j:j + 1, cols] += jnp.sum(taps[j] * d_pre, axis=0, keepdims=True)
            dp_ref[:, SSD_X0 + c0:SSD_X0 + c0 + CONV_COLS] = _conv_dgrad_from_ext(dext_ref, w, SSD_CONV, SSD_ROWS, cols)
            dext_ref[SSD_ROWS:, cols] = d_pre[0:SUBLANES, :]

        @pl.when(i == nb - 1)
        def _():
            ddsk_ref[...] = jnp.max(_xsel_nt(jnp.broadcast_to(ddskw_ref[...], (SUBLANES, D_INNER)), expand), axis=0, keepdims=True)

    vec = lambda n: pl.BlockSpec((1, n), lambda i: (0, 0))
    outs = pl.pallas_call(
        body, name=name, grid=(nb,),
        in_specs=[pl.BlockSpec((SSD_ROWS, SSD_IN_PAD), lambda i: (nb - 1 - i, 0)),
                  pl.BlockSpec((SUBLANES, SSD_IN_PAD), lambda i: (jnp.maximum((nb - 1 - i) * hb - 1, 0), 0)),
                  pl.BlockSpec((SUBLANES, SSD_CONV_DIM), lambda i: (0, 0)),
                  vec(SSD_CONV_DIM), vec(LANES), vec(LANES), vec(LANES), vec(D_INNER),
                  pl.BlockSpec((cpb, SSD_G, SSD_S, SSD_GW), lambda i: (nb - 1 - i, 0, 0, 0)),
                  pl.BlockSpec((SSD_ROWS, D_INNER), lambda i: (nb - 1 - i, 0))],
        out_specs=[pl.BlockSpec((SSD_ROWS, SSD_IN_PAD), lambda i: (nb - 1 - i, 0)),
                   pl.BlockSpec((SUBLANES, SSD_CONV_DIM), lambda i: (0, 0)),
                   vec(SSD_CONV_DIM), vec(LANES), vec(LANES), vec(LANES), vec(D_INNER)],
        out_shape=[jax.ShapeDtypeStruct((rows_total, SSD_IN_PAD), F32),
                   jax.ShapeDtypeStruct((SUBLANES, SSD_CONV_DIM), F32),
                   jax.ShapeDtypeStruct((1, SSD_CONV_DIM), F32), jax.ShapeDtypeStruct((1, LANES), F32),
                   jax.ShapeDtypeStruct((1, LANES), F32), jax.ShapeDtypeStruct((1, LANES), F32),
                   jax.ShapeDtypeStruct((1, D_INNER), F32)],
        scratch_shapes=[pltpu.VMEM((SSD_ROWS + SUBLANES, SSD_CONV_DIM), F32),
                        pltpu.VMEM((SSD_ROWS, SSD_CONV_DIM), F32),
                        pltpu.VMEM((SSD_ROWS, LANES), F32),
                        pltpu.VMEM((SSD_G, SSD_S, SSD_GW), F32),
                        pltpu.VMEM((LANES, CHUNK), F32),
                        pltpu.VMEM((SSD_ROWS + SUBLANES, SSD_CONV_DIM), F32),
                        pltpu.VMEM((CHUNK, LANES), F32),
                        pltpu.VMEM((LANES, CHUNK), F32),
                        pltpu.VMEM((1, D_INNER), F32)],
        compiler_params=_cparams(("arbitrary",)),
    )(proj, proj, _pad_rows(conv_w), conv_b.reshape(1, -1), _pad_lanes(a_log), _pad_lanes(dt_bias),
      _pad_lanes(d_skip), norm_w.reshape(1, -1), states, dmix)
    dproj, dcw, dcb, dalog, ddtb, ddsk, dnw = outs
    return dproj, [dcw[:SSD_CONV], dcb[0], dalog[0, :SSD_H], ddtb[0, :SSD_H], ddsk[0, :SSD_H], dnw[0]]


GDN_ROWS = 128
GDN_K0 = GDN_QK
GDN_V0 = 2 * GDN_QK
GDN_Z0 = GDN_CONV_DIM
GDN_BA0 = GDN_CONV_DIM + GDN_V
GDN_GL = GDN_VH
GDN_SCALE = GDN_HEAD ** -0.5
GDN_GROUP = 8


def _gdn_lane_params(v):
    return jnp.pad(v.reshape(1, GDN_VH), ((0, 0), (GDN_GL, LANES - GDN_GL - GDN_VH)))


def _inv_unit_lower(a, eye_c):
    x = eye_c - a
    ph, pl_ = _split(a, 2)
    n = 2
    while n < CHUNK:
        p = (_mxu(pl_, ph, NN) + _mxu(ph, pl_, NN)) + _mxu(ph, ph, NN)
        ph, pl_ = _split(p, 2)
        xh, xl = _split(x, 2)
        x = x + ((_mxu(xl, ph, NN) + _mxu(xh, pl_, NN)) + _mxu(xh, ph, NN))
        n *= 2
    return x


def _gdn_prologue(blk, p_ref, halo_ref, cw_ref, alog_ref, dtb_ref, ext_ref, qkv_ref, beta_ref, g_ref):
    ext_ref[0:SUBLANES, :] = jnp.where(blk == 0, 0.0, halo_ref[:, 0:GDN_CONV_DIM])
    ext_ref[SUBLANES:, :] = p_ref[:, 0:GDN_CONV_DIM]
    w = cw_ref[...]
    for hq in range(2 * GDN_QKH):
        cols = slice(hq * GDN_HEAD, (hq + 1) * GDN_HEAD)
        a = _silu(_conv_from_ext(ext_ref, w, GDN_CONV, GDN_ROWS, cols))
        r = lax.rsqrt(jnp.sum(a * a, axis=-1, keepdims=True) + L2_EPS)
        qkv_ref[:, cols] = a * (r * (GDN_SCALE if hq < GDN_QKH else 1.0))
    for h in range(GDN_VH):
        cols = slice(GDN_V0 + h * GDN_HEAD, GDN_V0 + (h + 1) * GDN_HEAD)
        qkv_ref[:, cols] = _silu(_conv_from_ext(ext_ref, w, GDN_CONV, GDN_ROWS, cols))
    ba = p_ref[:, GDN_BA0:GDN_BA0 + LANES]
    beta_ref[...] = _sigmoid(ba)
    g_ref[...] = -jnp.exp(alog_ref[...]) * _softplus(ba + dtb_ref[...])


def _each(f, *lists):
    return [f(*z) for z in zip(*lists)]


def _inv_unit_lower_each(a_list, eye_c):
    xs = [eye_c - a for a in a_list]
    ps = [_split(a, 2) for a in a_list]
    n = 2
    while n < CHUNK:
        ps = [_split((_mxu(pl_, ph, NN) + _mxu(ph, pl_, NN)) + _mxu(ph, ph, NN), 2) for ph, pl_ in ps]
        xs_split = [_split(x, 2) for x in xs]
        xs = [x + ((_mxu(xl, ph, NN) + _mxu(xh, pl_, NN)) + _mxu(xh, ph, NN))
              for x, (xh, xl), (ph, pl_) in zip(xs, xs_split, ps)]
        n *= 2
    return xs


def _gdn_heads_fwd(q, k, v, kk, qk, gcol, grow, glast, bcol, s, causal, strict, eye_c, t=None):
    decay = _each(lambda gc_, gr_: jnp.where(causal, jnp.exp(jnp.minimum(gc_ - gr_, 0.0)), 0.0), gcol, grow)
    egc = _each(jnp.exp, gcol)
    etail = _each(lambda gl_, gc_: jnp.exp(gl_ - gc_), glast, gcol)
    cd = _each(jnp.exp, glast)
    a = _each(lambda b_, kk_, d_: jnp.where(strict, b_ * kk_ * d_, 0.0), bcol, kk, decay)
    if t is None:
        t = _inv_unit_lower_each(a, eye_c)
    kb = _each(lambda k_, b_: k_ * b_, k, bcol)
    rhs_w = _each(lambda kb_, e_: kb_ * e_, kb, egc)
    u = _each(lambda t_, v_, b_: _dot_x3(t_, v_ * b_), t, v, bcol)
    w = _each(_dot_x3, t, rhs_w)
    attn = _each(lambda qk_, d_: qk_ * d_, qk, decay)
    ws = _each(_dot, w, s)
    v_new = _each(lambda u_, ws_: u_ - ws_, u, ws)
    qd = _each(lambda q_, e_: q_ * e_, q, egc)
    kt = _each(lambda k_, e_: k_ * e_, k, etail)
    o1 = _each(_dot, qd, s)
    o2 = _each(_dot, attn, v_new)
    out = _each(lambda a_, b_: a_ + b_, o1, o2)
    upd = _each(_dot_tn, kt, v_new)
    s_new = _each(lambda s_, c_, u_: s_ * c_ + u_, s, cd, upd)
    return dict(decay=decay, egc=egc, etail=etail, cd=cd, a=a, t=t, kb=kb, rhs_w=rhs_w, u=u, w=w, attn=attn,
                v_new=v_new, qd=qd, kt=kt, out=out, s_new=s_new)


def gdn_fwd(proj, conv_w, a_log, dt_bias, norm_w, *, name):
    rows_total = proj.shape[0]
    nb = rows_total // GDN_ROWS
    hb = GDN_ROWS // SUBLANES
    cpb = GDN_ROWS // CHUNK

    def body(p_ref, halo_ref, cw_ref, alog_ref, dtb_ref, nw_ref, mix_ref, st_ref, tm_ref,
             ext_ref, qkv_ref, beta_ref, g_ref, s_ref, gct_ref):
        i = pl.program_id(0)

        @pl.when(i == 0)
        def _():
            s_ref[...] = jnp.zeros_like(s_ref)

        _gdn_prologue(i, p_ref, halo_ref, cw_ref, alog_ref, dtb_ref, ext_ref, qkv_ref, beta_ref, g_ref)
        ltri, eye_l, eye_c = _tri(CHUNK), _eye(LANES), _eye(CHUNK)
        causal = _iota((CHUNK, CHUNK), 1) <= _iota((CHUNK, CHUNK), 0)
        strict = _iota((CHUNK, CHUNK), 1) < _iota((CHUNK, CHUNK), 0)
        nw = nw_ref[...]

        def chunk(c, carry):
            rows = pl.ds(pl.multiple_of(c * CHUNK, CHUNK), CHUNK)
            gc = _sel(ltri,g_ref[rows, :])
            gct_ref[...] = _sel_nt(eye_l,gc)
            glast_row = _row(gc, CHUNK - 1)
            beta_c = beta_ref[rows, :]
            for h0 in range(0, GDN_VH, GDN_GROUP):
                hs = list(range(h0, h0 + GDN_GROUP))
                qs = {hq: qkv_ref[rows, hq * GDN_HEAD:(hq + 1) * GDN_HEAD] for hq in range(h0 // 2, (h0 + GDN_GROUP) // 2)}
                ks = {hq: qkv_ref[rows, GDN_K0 + hq * GDN_HEAD:GDN_K0 + (hq + 1) * GDN_HEAD] for hq in qs}
                kks = {hq: _dot_nt(ks[hq], ks[hq]) for hq in qs}
                qks = {hq: _dot_nt(qs[hq], ks[hq]) for hq in qs}
                ss = [s_ref[h] for h in hs]
                for h, s in zip(hs, ss):
                    st_ref[c, h] = s
                f = _gdn_heads_fwd(
                    [qs[h // 2] for h in hs], [ks[h // 2] for h in hs],
                    [qkv_ref[rows, GDN_V0 + h * GDN_HEAD:GDN_V0 + (h + 1) * GDN_HEAD] for h in hs],
                    [kks[h // 2] for h in hs], [qks[h // 2] for h in hs],
                    [_col(gc, GDN_GL + h) for h in hs], [gct_ref[GDN_GL + h:GDN_GL + h + 1, :] for h in hs],
                    [_col(glast_row, GDN_GL + h) for h in hs], [_col(beta_c, h) for h in hs], ss, causal, strict, eye_c)
                for i_h, h in enumerate(hs):
                    hc = slice(h * GDN_HEAD, (h + 1) * GDN_HEAD)
                    s_ref[h] = f["s_new"][i_h]
                    tm_ref[c, h] = f["t"][i_h]
                    o = f["out"][i_h]
                    r = lax.rsqrt(jnp.mean(o * o, axis=-1, keepdims=True) + RMS_EPS)
                    mix_ref[rows, hc] = o * r * nw * _silu(p_ref[rows, GDN_Z0 + h * GDN_HEAD:GDN_Z0 + (h + 1) * GDN_HEAD])
            return carry

        lax.fori_loop(0, cpb, chunk, 0)

    vec = lambda n: pl.BlockSpec((1, n), lambda i: (0, 0))
    mix, states, tmats = pl.pallas_call(
        body, name=name, grid=(nb,),
        in_specs=[pl.BlockSpec((GDN_ROWS, GDN_IN_PAD), lambda i: (i, 0)),
                  pl.BlockSpec((SUBLANES, GDN_IN_PAD), lambda i: (jnp.maximum(i * hb - 1, 0), 0)),
                  pl.BlockSpec((SUBLANES, GDN_CONV_DIM), lambda i: (0, 0)),
                  vec(LANES), vec(LANES), vec(GDN_HEAD)],
        out_specs=[pl.BlockSpec((GDN_ROWS, GDN_V), lambda i: (i, 0)),
                   pl.BlockSpec((cpb, GDN_VH, GDN_HEAD, GDN_HEAD), lambda i: (i, 0, 0, 0)),
                   pl.BlockSpec((cpb, GDN_VH, CHUNK, CHUNK), lambda i: (i, 0, 0, 0))],
        out_shape=[jax.ShapeDtypeStruct((rows_total, GDN_V), F32),
                   jax.ShapeDtypeStruct((rows_total // CHUNK, GDN_VH, GDN_HEAD, GDN_HEAD), F32),
                   jax.ShapeDtypeStruct((rows_total // CHUNK, GDN_VH, CHUNK, CHUNK), F32)],
        scratch_shapes=[pltpu.VMEM((GDN_ROWS + SUBLANES, GDN_CONV_DIM), F32),
                        pltpu.VMEM((GDN_ROWS, GDN_CONV_DIM), F32),
                        pltpu.VMEM((GDN_ROWS, LANES), F32),
                        pltpu.VMEM((GDN_ROWS, LANES), F32),
                        pltpu.VMEM((GDN_VH, GDN_HEAD, GDN_HEAD), F32),
                        pltpu.VMEM((LANES, CHUNK), F32)],
        compiler_params=_cparams(("arbitrary",)),
    )(proj, proj, _pad_rows(conv_w), _gdn_lane_params(a_log), _gdn_lane_params(dt_bias), norm_w.reshape(1, -1))
    return mix, (states, tmats)


def gdn_bwd(proj, conv_w, a_log, dt_bias, norm_w, saved, dmix, *, name):
    states, tmats = saved
    rows_total = proj.shape[0]
    nb = rows_total // GDN_ROWS
    hb = GDN_ROWS // SUBLANES
    cpb = GDN_ROWS // CHUNK

    def body(p_ref, halo_ref, cw_ref, alog_ref, dtb_ref, nw_ref, st_ref, tm_ref, dm_ref,
             dp_ref, dcw_ref, dalog_ref, ddtb_ref, dnw_ref,
             ext_ref, qkv_ref, beta_ref, g_ref, ds_ref, gct_ref, dext_ref, dgc_ref, dgct_ref, dbeta_ref):
        i = pl.program_id(0)
        blk = nb - 1 - i

        @pl.when(i == 0)
        def _():
            ds_ref[...] = jnp.zeros_like(ds_ref)
            dext_ref[GDN_ROWS:, :] = jnp.zeros((SUBLANES, GDN_CONV_DIM), F32)
            for r in (dcw_ref, dalog_ref, ddtb_ref, dnw_ref):
                r[...] = jnp.zeros_like(r)

        _gdn_prologue(blk, p_ref, halo_ref, cw_ref, alog_ref, dtb_ref, ext_ref, qkv_ref, beta_ref, g_ref)
        ltri, utri, eye_l, eye_c = _tri(CHUNK), _tri(CHUNK, lower=False), _eye(LANES), _eye(CHUNK)
        causal = _iota((CHUNK, CHUNK), 1) <= _iota((CHUNK, CHUNK), 0)
        strict = _iota((CHUNK, CHUNK), 1) < _iota((CHUNK, CHUNK), 0)
        lane = _iota((CHUNK, LANES), 1)
        is_last = _iota((CHUNK, 1), 0) == CHUNK - 1
        nw = nw_ref[...]

        def chunk(cc, carry):
            c = cpb - 1 - cc
            rows = pl.ds(pl.multiple_of(c * CHUNK, CHUNK), CHUNK)
            g_c = g_ref[rows, :]
            gc = _sel(ltri,g_c)
            gct_ref[...] = _sel_nt(eye_l,gc)
            glast_row = _row(gc, CHUNK - 1)
            beta_c = beta_ref[rows, :]
            dgc_ref[...] = jnp.zeros_like(dgc_ref)
            dgct_ref[...] = jnp.zeros_like(dgct_ref)
            dbeta_ref[...] = jnp.zeros_like(dbeta_ref)
            for h0 in range(0, GDN_VH, GDN_GROUP):
                hs = list(range(h0, h0 + GDN_GROUP))
                hqs = list(range(h0 // 2, (h0 + GDN_GROUP) // 2))
                qs = {hq: qkv_ref[rows, hq * GDN_HEAD:(hq + 1) * GDN_HEAD] for hq in hqs}
                ks = {hq: qkv_ref[rows, GDN_K0 + hq * GDN_HEAD:GDN_K0 + (hq + 1) * GDN_HEAD] for hq in hqs}
                kks = {hq: _dot_nt(ks[hq], ks[hq]) for hq in hqs}
                qks = {hq: _dot_nt(qs[hq], ks[hq]) for hq in hqs}
                q = [qs[h // 2] for h in hs]
                k = [ks[h // 2] for h in hs]
                v = [qkv_ref[rows, GDN_V0 + h * GDN_HEAD:GDN_V0 + (h + 1) * GDN_HEAD] for h in hs]
                s = [st_ref[c, h] for h in hs]
                bcol = [_col(beta_c, h) for h in hs]
                f = _gdn_heads_fwd(q, k, v, [kks[h // 2] for h in hs], [qks[h // 2] for h in hs],
                                   [_col(gc, GDN_GL + h) for h in hs], [gct_ref[GDN_GL + h:GDN_GL + h + 1, :] for h in hs],
                                   [_col(glast_row, GDN_GL + h) for h in hs], bcol, s, causal, strict, eye_c,
                                   t=[tm_ref[c, h] for h in hs])
                do = []
                for i_h, h in enumerate(hs):
                    zc = slice(GDN_Z0 + h * GDN_HEAD, GDN_Z0 + (h + 1) * GDN_HEAD)
                    o = f["out"][i_h]
                    z = p_ref[rows, zc]
                    sz = _silu(z)
                    r = lax.rsqrt(jnp.mean(o * o, axis=-1, keepdims=True) + RMS_EPS)
                    on = o * r
                    dm = dm_ref[rows, h * GDN_HEAD:(h + 1) * GDN_HEAD]
                    dnw_ref[...] += jnp.sum(dm * on * sz, axis=0, keepdims=True)
                    d_on = dm * nw * sz
                    dp_ref[rows, zc] = dm * on * nw * _dsilu(z)
                    do.append(r * (d_on - on * jnp.mean(d_on * on, axis=-1, keepdims=True)))
                ds_n = [ds_ref[h] for h in hs]
                dv1 = _each(_dot_tn, f["attn"], do)
                dv2 = _each(_dot, f["kt"], ds_n)
                d_vnew = _each(lambda a_, b_: a_ + b_, dv1, dv2)
                d_attn = _each(lambda do_, vn_: jnp.where(causal, _dot_nt(do_, vn_), 0.0), do, f["v_new"])
                d_qd = _each(_dot_nt, do, s)
                t1 = _each(_dot_tn, f["qd"], do)
                t2 = _each(_dot_tn, f["w"], d_vnew)
                for h, a_, cd_, dsn_, b_ in zip(hs, t1, f["cd"], ds_n, t2):
                    ds_ref[h] = a_ + cd_ * dsn_ - b_
                d_cd = _each(lambda s_, dsn_: jnp.sum(jnp.sum(s_ * dsn_, axis=1, keepdims=True), axis=0, keepdims=True), s, ds_n)
                d_kt = _each(_dot_nt, f["v_new"], ds_n)
                d_w = _each(lambda dv_, s_: -_dot_nt(dv_, s_), d_vnew, s)
                d_rhs_u = _each(lambda t_, d_: _dot_x3(t_, d_, TN), f["t"], d_vnew)
                d_rhs_w = _each(lambda t_, d_: _dot_x3(t_, d_, TN), f["t"], d_w)
                m1 = _each(_dot_nt, d_rhs_u, f["u"])
                m2 = _each(_dot_nt, d_rhs_w, f["w"])
                da = _each(lambda a_, b_: -jnp.where(strict, a_ + b_, 0.0), m1, m2)
                dmm = _each(lambda a_, b_: a_ * b_, da, f["decay"])
                em = _each(lambda da_, a_, dat_, at_: da_ * a_ + dat_ * at_, da, f["a"], d_attn, f["attn"])
                x1 = _each(_dot, dmm, k)
                d_kb = _each(lambda x_, drw_, e_: x_ + drw_ * e_, x1, d_rhs_w, f["egc"])
                dk1 = _each(_dot_tn, dmm, f["kb"])
                dpm = _each(lambda a_, b_: a_ * b_, d_attn, f["decay"])
                dq1 = _each(_dot, dpm, k)
                dq = _each(lambda x_, dqd_, e_: x_ + dqd_ * e_, dq1, d_qd, f["egc"])
                dk2 = _each(_dot_tn, dpm, q)
                dk = _each(lambda a_, b_, dkb_, bc_, dkt_, et_: a_ + b_ + dkb_ * bc_ + dkt_ * et_,
                           dk1, dk2, d_kb, bcol, d_kt, f["etail"])
                for i_h, h in enumerate(hs):
                    tmp = jnp.sum(d_kt[i_h] * f["kt"][i_h], axis=1, keepdims=True)
                    d_gcol = (jnp.sum(em[i_h], axis=1, keepdims=True)
                              + jnp.sum(d_rhs_w[i_h] * f["rhs_w"][i_h], axis=1, keepdims=True)
                              + jnp.sum(d_qd[i_h] * f["qd"][i_h], axis=1, keepdims=True) - tmp)
                    d_glast = jnp.sum(tmp, axis=0, keepdims=True) + d_cd[i_h] * f["cd"][i_h]
                    d_gcol = jnp.where(is_last, d_gcol + d_glast, d_gcol)
                    d_beta = (jnp.sum(d_rhs_u[i_h] * v[i_h], axis=1, keepdims=True)
                              + jnp.sum(d_kb[i_h] * k[i_h], axis=1, keepdims=True))
                    dgc_ref[...] += jnp.where(lane == GDN_GL + h, d_gcol, 0.0)
                    dgct_ref[GDN_GL + h:GDN_GL + h + 1, :] = jnp.sum(em[i_h], axis=0, keepdims=True)
                    dbeta_ref[...] += jnp.where(lane == h, d_beta, 0.0)
                    dext_ref[rows, GDN_V0 + h * GDN_HEAD:GDN_V0 + (h + 1) * GDN_HEAD] = d_rhs_u[i_h] * bcol[i_h]
                for hq in hqs:
                    i0 = 2 * hq - h0
                    dext_ref[rows, hq * GDN_HEAD:(hq + 1) * GDN_HEAD] = dq[i0] + dq[i0 + 1]
                    dext_ref[rows, GDN_K0 + hq * GDN_HEAD:GDN_K0 + (hq + 1) * GDN_HEAD] = dk[i0] + dk[i0 + 1]
            d_gc = dgc_ref[...] - _sel_nt(eye_c,dgct_ref[...])
            dg = _sel(utri,d_gc)
            ba = p_ref[rows, GDN_BA0:GDN_BA0 + LANES]
            d_sp = dg * -jnp.exp(alog_ref[...])
            d_araw = d_sp * _sigmoid(ba + dtb_ref[...])
            d_araw = jnp.where((lane >= GDN_GL) & (lane < GDN_GL + GDN_VH), d_araw, 0.0)
            dalog_ref[...] += jnp.sum(dg * g_c, axis=0, keepdims=True)
            ddtb_ref[...] += jnp.sum(d_araw, axis=0, keepdims=True)
            d_braw = jnp.where(lane < GDN_VH, dbeta_ref[...] * beta_c * (1.0 - beta_c), 0.0)
            dp_ref[rows, GDN_BA0:GDN_BA0 + LANES] = d_braw + d_araw
            return carry

        lax.fori_loop(0, cpb, chunk, 0)
        w = cw_ref[...]
        for hh in range(GDN_CONV_DIM // GDN_HEAD):
            cols = slice(hh * GDN_HEAD, (hh + 1) * GDN_HEAD)
            taps = [_rows_from(ext_ref, SUBLANES - (GDN_CONV - 1) + j, GDN_ROWS, cols) for j in range(GDN_CONV)]
            pre = None
            for j in range(GDN_CONV):
                term = taps[j] * w[j:j + 1, cols]
                pre = term if pre is None else pre + term
            d_act = dext_ref[0:GDN_ROWS, cols]
            if hh < 2 * GDN_QKH:
                a = _silu(pre)
                r = lax.rsqrt(jnp.sum(a * a, axis=-1, keepdims=True) + L2_EPS)
                ah = a * r
                if hh < GDN_QKH:
                    d_act = d_act * GDN_SCALE
                d_act = r * (d_act - ah * jnp.sum(d_act * ah, axis=-1, keepdims=True))
            d_pre = d_act * _dsilu(pre)
            dext_ref[0:GDN_ROWS, cols] = d_pre
            for j in range(GDN_CONV):
                dcw_ref[j:j + 1, cols] += jnp.sum(taps[j] * d_pre, axis=0, keepdims=True)
            dp_ref[:, cols] = _conv_dgrad_from_ext(dext_ref, w, GDN_CONV, GDN_ROWS, cols)
            dext_ref[GDN_ROWS:, cols] = d_pre[0:SUBLANES, :]

    vec = lambda n: pl.BlockSpec((1, n), lambda i: (0, 0))
    outs = pl.pallas_call(
        body, name=name, grid=(nb,),
        in_specs=[pl.BlockSpec((GDN_ROWS, GDN_IN_PAD), lambda i: (nb - 1 - i, 0)),
                  pl.BlockSpec((SUBLANES, GDN_IN_PAD), lambda i: (jnp.maximum((nb - 1 - i) * hb - 1, 0), 0)),
                  pl.BlockSpec((SUBLANES, GDN_CONV_DIM), lambda i: (0, 0)),
                  vec(LANES), vec(LANES), vec(GDN_HEAD),
                  pl.BlockSpec((cpb, GDN_VH, GDN_HEAD, GDN_HEAD), lambda i: (nb - 1 - i, 0, 0, 0)),
                  pl.BlockSpec((cpb, GDN_VH, CHUNK, CHUNK), lambda i: (nb - 1 - i, 0, 0, 0)),
                  pl.BlockSpec((GDN_ROWS, GDN_V), lambda i: (nb - 1 - i, 0))],
        out_specs=[pl.BlockSpec((GDN_ROWS, GDN_IN_PAD), lambda i: (nb - 1 - i, 0)),
                   pl.BlockSpec((SUBLANES, GDN_CONV_DIM), lambda i: (0, 0)),
                   vec(LANES), vec(LANES), vec(GDN_HEAD)],
        out_shape=[jax.ShapeDtypeStruct((rows_total, GDN_IN_PAD), F32),
                   jax.ShapeDtypeStruct((SUBLANES, GDN_CONV_DIM), F32),
                   jax.ShapeDtypeStruct((1, LANES), F32), jax.ShapeDtypeStruct((1, LANES), F32),
                   jax.ShapeDtypeStruct((1, GDN_HEAD), F32)],
        scratch_shapes=[pltpu.VMEM((GDN_ROWS + SUBLANES, GDN_CONV_DIM), F32),
                        pltpu.VMEM((GDN_ROWS, GDN_CONV_DIM), F32),
                        pltpu.VMEM((GDN_ROWS, LANES), F32),
                        pltpu.VMEM((GDN_ROWS, LANES), F32),
                        pltpu.VMEM((GDN_VH, GDN_HEAD, GDN_HEAD), F32),
                        pltpu.VMEM((LANES, CHUNK), F32),
                        pltpu.VMEM((GDN_ROWS + SUBLANES, GDN_CONV_DIM), F32),
                        pltpu.VMEM((CHUNK, LANES), F32),
                        pltpu.VMEM((LANES, CHUNK), F32),
                        pltpu.VMEM((CHUNK, LANES), F32)],
        compiler_params=_cparams(("arbitrary",)),
    )(proj, proj, _pad_rows(conv_w), _gdn_lane_params(a_log), _gdn_lane_params(dt_bias), norm_w.reshape(1, -1),
      states, tmats, dmix)
    dproj, dcw, dalog, ddtb, dnw = outs
    return dproj, [dcw[:GDN_CONV], dalog[0, GDN_GL:GDN_GL + GDN_VH], ddtb[0, GDN_GL:GDN_GL + GDN_VH], dnw[0]]


def chip_exchange(src, *, scatter, name):
    piece_shape = src.shape[1:]

    def body(src_ref, out_ref, send_sems, recv_sems, local_sem):
        x, y, c = (lax.axis_index(a) for a in MESH_AXES)
        me = 2 * x + y

        def piece(j):
            return src_ref.at[j] if scatter else src_ref.at[c]

        local = pltpu.make_async_copy(piece(me), out_ref.at[me], local_sem)
        local.start()
        copies = []
        for k in range(1, N_SHARDS):
            px = 1 - x if k & 2 else x
            py = 1 - y if k & 1 else y
            cp = pltpu.make_async_remote_copy(
                src_ref=piece(2 * px + py), dst_ref=out_ref.at[me], send_sem=send_sems.at[k - 1],
                recv_sem=recv_sems.at[k - 1], device_id=(px, py, c), device_id_type=pl.DeviceIdType.MESH)
            cp.start()
            copies.append(cp)
        for cp in copies:
            cp.wait()
        local.wait()

    hbm = pl.BlockSpec(memory_space=pl.ANY)
    return pl.pallas_call(
        body, name=name, in_specs=[hbm], out_specs=hbm,
        out_shape=jax.ShapeDtypeStruct((N_SHARDS,) + tuple(piece_shape), src.dtype),
        scratch_shapes=[pltpu.SemaphoreType.DMA((N_SHARDS - 1,)), pltpu.SemaphoreType.DMA((N_SHARDS - 1,)),
                        pltpu.SemaphoreType.DMA],
    )(src)


def pair_exchange(src, *, add, name):
    lead, rows, cols = src.shape
    tr = _pick(rows, (512, 256))
    nblk = rows // tr
    n_steps = nblk if add else lead * nblk

    def body(c_ref, *refs):
        if add:
            mine_ref, send_ref, o_ref, recv_ref, send_sems, recv_sems, credit = refs
        else:
            send_ref, o_ref, recv_ref, send_sems, recv_sems, credit = refs
        step = pl.program_id(0) * nblk + pl.program_id(1)
        slot = step % 2
        sibling = (lax.axis_index("x"), lax.axis_index("y"), 1 - lax.axis_index("c"))

        @pl.when(step >= 2)
        def _():
            pl.semaphore_wait(credit, 1)

        cp = pltpu.make_async_remote_copy(
            src_ref=send_ref, dst_ref=recv_ref.at[slot], send_sem=send_sems.at[slot], recv_sem=recv_sems.at[slot],
            device_id=sibling, device_id_type=pl.DeviceIdType.MESH)
        cp.start()
        cp.wait_recv()
        if add:
            o_ref[...] = mine_ref[...] + recv_ref[slot]
        else:
            o_ref[c_ref[0]] = send_ref[...]
            o_ref[1 - c_ref[0]] = recv_ref[slot]
        cp.wait_send()

        @pl.when(step + 2 < n_steps)
        def _():
            pl.semaphore_signal(credit, 1, device_id=sibling, device_id_type=pl.DeviceIdType.MESH)

    flat = src.reshape(lead * rows, cols)
    if add:
        in_specs = [pl.BlockSpec((tr, cols), lambda s, i, c_ref: (c_ref[0] * nblk + i, 0)),
                    pl.BlockSpec((tr, cols), lambda s, i, c_ref: ((1 - c_ref[0]) * nblk + i, 0))]
        out_specs = pl.BlockSpec((tr, cols), lambda s, i, c_ref: (i, 0))
        out_shape = jax.ShapeDtypeStruct((rows, cols), src.dtype)
        grid, args = (1, nblk), (flat, flat)
    else:
        in_specs = [pl.BlockSpec((tr, cols), lambda s, i, c_ref: (s * nblk + i, 0))]
        out_specs = pl.BlockSpec((2, tr, cols), lambda s, i, c_ref: (s, i, 0))
        out_shape = jax.ShapeDtypeStruct((lead * 2, rows, cols), src.dtype)
        grid, args = (lead, nblk), (flat,)
    out = pl.pallas_call(
        body, name=name, out_shape=out_shape,
        grid_spec=pltpu.PrefetchScalarGridSpec(
            num_scalar_prefetch=1, grid=grid, in_specs=in_specs, out_specs=out_specs,
            scratch_shapes=[pltpu.VMEM((2, tr, cols), src.dtype), pltpu.SemaphoreType.DMA((2,)),
                            pltpu.SemaphoreType.DMA((2,)), pltpu.SemaphoreType.REGULAR]),
        compiler_params=_cparams(("arbitrary", "arbitrary")),
    )(lax.axis_index("c").astype(jnp.int32).reshape(1), *args)
    return out if add else out.reshape(lead, 2, rows, cols)


def sum_slots(buf, *, name):
    n, rows, cols = buf.shape
    tr = _pick(rows, (512, 256, 128))

    def body(b_ref, o_ref):
        acc = b_ref[0]
        for j in range(1, n):
            acc = acc + b_ref[j]
        o_ref[...] = acc

    return pl.pallas_call(
        body, name=name, grid=(rows // tr,), in_specs=[pl.BlockSpec((n, tr, cols), lambda i: (0, i, 0))],
        out_specs=pl.BlockSpec((tr, cols), lambda i: (i, 0)), out_shape=jax.ShapeDtypeStruct((rows, cols), F32),
        compiler_params=_cparams(("parallel",)),
    )(buf)


def adamw(w, g, m, v, *, name):
    shape = w.shape
    cols = shape[-1]
    rows = _size(shape) // cols
    w, g, m, v = (t.reshape(rows, cols) for t in (w, g, m, v))
    tr = 256 if rows % 256 == 0 else rows

    def body(w_ref, g_ref, m_ref, v_ref, d_ref, mo_ref, vo_ref):
        gv = g_ref[...]
        mn = ADAM_B1 * m_ref[...] + (1.0 - ADAM_B1) * gv
        vn = ADAM_B2 * v_ref[...] + (1.0 - ADAM_B2) * (gv * gv)
        m_hat = mn / (1.0 - ADAM_B1 ** ADAM_STEP)
        v_hat = vn / (1.0 - ADAM_B2 ** ADAM_STEP)
        d_ref[...] = -ADAM_LR * (m_hat / (jnp.sqrt(v_hat) + ADAM_EPS) + ADAM_WD * w_ref[...])
        mo_ref[...] = mn
        vo_ref[...] = vn

    blk = pl.BlockSpec((tr, cols), lambda i: (i, 0))
    shp = jax.ShapeDtypeStruct((rows, cols), F32)
    outs = pl.pallas_call(
        body, name=name, grid=(rows // tr,), in_specs=[blk] * 4, out_specs=[blk] * 3, out_shape=[shp] * 3,
        compiler_params=_cparams(("parallel",)),
    )(w, g, m, v)
    return [o.reshape(shape) for o in outs]


N_SHARDS = 4
FLAT_COLS = 1024
W_SPECS = (
    ("gdn_w_in", (2, 1024, 6176), 2), ("gdn_conv_w", (2, 4, 4096), 2), ("gdn_a_log", (2, 16), None),
    ("gdn_dt_bias", (2, 16), None), ("gdn_norm_w", (2, 128), None), ("gdn_w_out", (2, 2048, 1024), 1),
    ("sc_w_in", (1, 1024, 8192), 2), ("sc_conv_w", (1, 3, 2048), 2), ("sc_w_out", (1, 2048, 1024), 1),
    ("ssd_w_in", (1, 1024, 5152), 2), ("ssd_conv_w", (1, 4, 3072), 2), ("ssd_conv_b", (1, 3072), 1),
    ("ssd_a_log", (1, 32), None), ("ssd_dt_bias", (1, 32), None), ("ssd_d_skip", (1, 32), None),
    ("ssd_norm_w", (1, 2048), 1), ("ssd_w_out", (1, 2048, 1024), 1), ("ln_g", (4, 1024), None), ("ln_b", (4, 1024), None),
)


def _local_shape(shape, axis):
    return shape if axis is None else tuple(d // N_SHARDS if i == axis else d for i, d in enumerate(shape))


def _size(shape):
    n = 1
    for d in shape:
        n *= d
    return n


FLAT_USED = sum(_size(_local_shape(s, a)) for _, s, a in W_SPECS)
FLAT_ROWS = -(-FLAT_USED // (FLAT_COLS * 512)) * 512
FLAT_HALF = FLAT_ROWS // 2


def _pack(pieces):
    flat = jnp.concatenate([p.reshape(-1) for p in pieces] + [jnp.zeros((FLAT_ROWS * FLAT_COLS - FLAT_USED,), F32)])
    return flat.reshape(FLAT_ROWS, FLAT_COLS)


def _unpack(flat):
    flat = flat.reshape(-1)
    out, off = [], 0
    for _, shape, axis in W_SPECS:
        ls = _local_shape(shape, axis)
        out.append(flat[off:off + _size(ls)].reshape(ls))
        off += _size(ls)
    return out


def _shard_of(full, axis, s):
    if axis is None:
        return full
    n = full.shape[axis] // N_SHARDS
    return lax.slice_in_dim(full, s * n, (s + 1) * n, axis=axis)


def _adamw_all(weights, grads_flat, moms, vels):
    grads = _unpack(grads_flat)
    steps = [adamw(w, g, m, v, name="adamw") for w, g, m, v in zip(weights, grads, moms, vels)]
    return grads, [s[0] for s in steps], [s[1] for s in steps], [s[2] for s in steps]


SPLIT_ROWS = 128


def shard_split(w, n_real, *, name):
    rows, n_pad = w.shape
    ns = n_real // N_SHARDS

    def body(w_ref, o_ref):
        for s in range(N_SHARDS):
            o_ref[s] = w_ref[:, s * ns:(s + 1) * ns]

    return pl.pallas_call(
        body, name=name, grid=(rows // SPLIT_ROWS,),
        in_specs=[pl.BlockSpec((SPLIT_ROWS, n_pad), lambda i: (i, 0))],
        out_specs=pl.BlockSpec((N_SHARDS, SPLIT_ROWS, ns), lambda i: (0, i, 0)),
        out_shape=jax.ShapeDtypeStruct((N_SHARDS, rows, ns), F32), compiler_params=_cparams(("parallel",)),
    )(w)


def shard_merge(pieces, n_pad, *, name):
    _, rows, ns = pieces.shape
    n_real = ns * N_SHARDS

    def body(p_ref, o_ref):
        for s in range(N_SHARDS):
            o_ref[:, s * ns:(s + 1) * ns] = p_ref[s].astype(o_ref.dtype)
        if n_pad > n_real:
            o_ref[:, n_real:] = jnp.zeros((SPLIT_ROWS, n_pad - n_real), o_ref.dtype)

    return pl.pallas_call(
        body, name=name, grid=(rows // SPLIT_ROWS,),
        in_specs=[pl.BlockSpec((N_SHARDS, SPLIT_ROWS, ns), lambda i: (0, i, 0))],
        out_specs=pl.BlockSpec((SPLIT_ROWS, n_pad), lambda i: (i, 0)),
        out_shape=jax.ShapeDtypeStruct((rows, n_pad), MXU_DTYPE), compiler_params=_cparams(("parallel",)),
    )(pieces)


def _reduce_scatter(full_grads):
    def shard(g, spec, s):
        _, shape, axis = spec
        return g[:, s] if g.ndim == len(shape) + 1 else _shard_of(g, axis, s)

    by_shard = jnp.stack([_pack([shard(g, spec, s) for g, spec in zip(full_grads, W_SPECS)])
                          for s in range(N_SHARDS)])
    by_half = by_shard.reshape(N_SHARDS, 2, FLAT_HALF, FLAT_COLS).transpose(1, 0, 2, 3)
    by_half = by_half.reshape(2, N_SHARDS * FLAT_HALF, FLAT_COLS)
    pair_sum = pair_exchange(by_half, add=True, name="rs_pair")
    chips = chip_exchange(pair_sum.reshape(N_SHARDS, FLAT_HALF, FLAT_COLS), scatter=True, name="rs_chips")
    half = sum_slots(chips, name="rs_chip_sum")
    return pair_exchange(half[None], add=False, name="rs_halves").reshape(FLAT_ROWS, FLAT_COLS)


def _gather_weights(local_weights):
    halves = chip_exchange(_pack(local_weights).reshape(2, FLAT_HALF, FLAT_COLS), scatter=False, name="gather_chips")
    gathered = pair_exchange(halves, add=False, name="gather_pair").reshape(N_SHARDS, FLAT_ROWS, FLAT_COLS)
    per_shard = [_unpack(gathered[s]) for s in range(N_SHARDS)]
    full = []
    for i, (wname, shape, axis) in enumerate(W_SPECS):
        if axis is None:
            full.append(local_weights[i])
        elif wname in W_IN_PAD:
            pieces = jnp.stack([per_shard[s][i] for s in range(N_SHARDS)], axis=1)
            full.append([shard_merge(pieces[j], W_IN_PAD[wname], name="merge_" + wname) for j in range(shape[0])])
        else:
            full.append(jnp.concatenate([per_shard[s][i] for s in range(N_SHARDS)], axis=axis))
    return full


W_IN_PAD = {"gdn_w_in": GDN_IN_PAD, "sc_w_in": SC_IN, "ssd_w_in": SSD_IN_PAD}


def kernel(x, gdn_w_in, gdn_conv_w, gdn_a_log, gdn_dt_bias, gdn_norm_w, gdn_w_out, sc_w_in, sc_conv_w, sc_w_out, ssd_w_in, ssd_conv_w, ssd_conv_b, ssd_a_log, ssd_dt_bias, ssd_d_skip, ssd_norm_w, ssd_w_out, ln_g, ln_b, loss_target, m_gdn_w_in, m_gdn_conv_w, m_gdn_a_log, m_gdn_dt_bias, m_gdn_norm_w, m_gdn_w_out, m_sc_w_in, m_sc_conv_w, m_sc_w_out, m_ssd_w_in, m_ssd_conv_w, m_ssd_conv_b, m_ssd_a_log, m_ssd_dt_bias, m_ssd_d_skip, m_ssd_norm_w, m_ssd_w_out, m_ln_g, m_ln_b, v_gdn_w_in, v_gdn_conv_w, v_gdn_a_log, v_gdn_dt_bias, v_gdn_norm_w, v_gdn_w_out, v_sc_w_in, v_sc_conv_w, v_sc_w_out, v_ssd_w_in, v_ssd_conv_w, v_ssd_conv_b, v_ssd_a_log, v_ssd_dt_bias, v_ssd_d_skip, v_ssd_norm_w, v_ssd_w_out, v_ln_g, v_ln_b):
    weights = [gdn_w_in, gdn_conv_w, gdn_a_log, gdn_dt_bias, gdn_norm_w, gdn_w_out, sc_w_in, sc_conv_w, sc_w_out,
               ssd_w_in, ssd_conv_w, ssd_conv_b, ssd_a_log, ssd_dt_bias, ssd_d_skip, ssd_norm_w, ssd_w_out, ln_g, ln_b]
    moms = [m_gdn_w_in, m_gdn_conv_w, m_gdn_a_log, m_gdn_dt_bias, m_gdn_norm_w, m_gdn_w_out, m_sc_w_in, m_sc_conv_w,
            m_sc_w_out, m_ssd_w_in, m_ssd_conv_w, m_ssd_conv_b, m_ssd_a_log, m_ssd_dt_bias, m_ssd_d_skip, m_ssd_norm_w,
            m_ssd_w_out, m_ln_g, m_ln_b]
    vels = [v_gdn_w_in, v_gdn_conv_w, v_gdn_a_log, v_gdn_dt_bias, v_gdn_norm_w, v_gdn_w_out, v_sc_w_in, v_sc_conv_w,
            v_sc_w_out, v_ssd_w_in, v_ssd_conv_w, v_ssd_conv_b, v_ssd_a_log, v_ssd_dt_bias, v_ssd_d_skip, v_ssd_norm_w,
            v_ssd_w_out, v_ln_g, v_ln_b]
    full = dict(zip([n for n, _, _ in W_SPECS], _gather_weights(weights)))
    x0 = x[0]
    target = loss_target[0]

    layers = (("gdn", 0, GDN_IN_PAD, GDN_IN), ("sc", 0, SC_IN, SC_IN), ("ssd", 0, SSD_IN_PAD, SSD_IN), ("gdn", 1, GDN_IN_PAD, GDN_IN))

    def params(kind, j):
        if kind == "gdn":
            return [full["gdn_conv_w"][j], full["gdn_a_log"][j], full["gdn_dt_bias"][j], full["gdn_norm_w"][j]]
        if kind == "sc":
            return [full["sc_conv_w"][j]]
        return [full["ssd_conv_w"][j], full["ssd_conv_b"][j], full["ssd_a_log"][j], full["ssd_dt_bias"][j],
                full["ssd_d_skip"][j], full["ssd_norm_w"][j]]

    xs, saved = [x0], []
    for i, (kind, j, n_pad, _) in enumerate(layers):
        w_in = full[kind + "_w_in"][j]
        w_out = full[kind + "_w_out"][j].astype(MXU_DTYPE)
        proj = matmul(xs[i], w_in, name=kind + "_proj")
        if kind == "gdn":
            mix, states = gdn_fwd(proj, *params(kind, j), name="gdn_fwd")
        elif kind == "sc":
            mix, states = sc_fwd(proj, *params(kind, j), name="sc_fwd"), None
        else:
            mix, states = ssd_fwd(proj, *params(kind, j), name="ssd_fwd")
        y = matmul(mix, w_out, name=kind + "_out")
        saved.append((w_in, w_out, proj, mix, states, y))
        if i + 1 < DEPTH:
            xs.append(ln_fwd(xs[i], y, full["ln_g"][i], full["ln_b"][i], name="ln_fwd"))

    grads = {n: [None] * s[0] for n, s, _ in W_SPECS}
    dr, dg, db, loss_rows = ln_bwd(xs[DEPTH - 1], saved[DEPTH - 1][5], full["ln_g"][DEPTH - 1], b=full["ln_b"][DEPTH - 1],
                                   target=target, name="ln_bwd_loss")
    dx = None
    for i in reversed(range(DEPTH)):
        kind, j, _, n_in = layers[i]
        w_in, w_out, proj, mix, states, _ = saved[i]
        grads["ln_g"][i], grads["ln_b"][i] = dg[0], db[0]
        dmix = matmul(dr, w_out, tb=True, name=kind + "_dmix")
        grads[kind + "_w_out"][j] = matmul(mix, dr, ta=True, name=kind + "_dw_out")
        if kind == "gdn":
            dproj, (dcw, dalog, ddtb, dnw) = gdn_bwd(proj, *params(kind, j), states, dmix, name="gdn_bwd")
            grads["gdn_conv_w"][j], grads["gdn_a_log"][j], grads["gdn_dt_bias"][j], grads["gdn_norm_w"][j] = dcw, dalog, ddtb, dnw
        elif kind == "sc":
            dproj, dcw = sc_bwd(proj, *params(kind, j), dmix, name="sc_bwd")
            grads["sc_conv_w"][j] = dcw[:SC_CONV]
        else:
            dproj, (dcw, dcb, dalog, ddtb, ddsk, dnw) = ssd_bwd(proj, *params(kind, j), states, dmix, name="ssd_bwd")
            grads["ssd_conv_w"][j], grads["ssd_conv_b"][j], grads["ssd_a_log"][j] = dcw, dcb, dalog
            grads["ssd_dt_bias"][j], grads["ssd_d_skip"][j], grads["ssd_norm_w"][j] = ddtb, ddsk, dnw
        grads[kind + "_w_in"][j] = shard_split(matmul(xs[i], dproj, ta=True, name=kind + "_dw_in"), n_in, name="split_" + kind)
        dx = matmul(dproj, w_in, tb=True, add=dr, add_scale=ALPHA, name=kind + "_dx")
        if i > 0:
            dr, dg, db = ln_bwd(xs[i - 1], saved[i - 1][5], full["ln_g"][i - 1], dx, name="ln_bwd")

    full_grads = [jnp.stack(grads[n]) for n, _, _ in W_SPECS]
    grads_flat = _reduce_scatter(full_grads)
    g_out, d_out, m_out, v_out = _adamw_all(weights, grads_flat, moms, vels)
    loss = lax.psum(loss_rows[0, 0], MESH_AXES)
    return (loss, dx[None], *g_out, *d_out, *m_out, *v_out)
```

```python
import functools

import jax
import jax.numpy as jnp
from jax import lax
from jax.experimental import pallas as pl
from jax.experimental.pallas import tpu as pltpu

F32 = jnp.float32
MXU_DTYPE = jnp.bfloat16

D_MODEL = 1024
DEPTH = 4
D_INNER = 2048
CHUNK = 64
LANES = 128
SUBLANES = 8
VMEM_LIMIT = 56 * 1024 * 1024

GDN_HEAD = 128
GDN_VH = 16
GDN_QKH = 8
GDN_QK = 1024
GDN_V = 2048
GDN_CONV = 4
GDN_CONV_DIM = 4096
GDN_IN = 6176
GDN_IN_PAD = 6272

SC_W = 2048
SC_CONV = 3
SC_IN = 8192

SSD_P = 64
SSD_H = 32
SSD_G = 4
SSD_S = 128
SSD_CONV = 4
SSD_CONV_DIM = 3072
SSD_IN = 5152
SSD_IN_PAD = 5376

ALPHA = (2 * DEPTH) ** 0.25
RMS_EPS = 1e-6
LN_EPS = 1e-5
L2_EPS = 1e-6

ADAM_LR = 0.001
ADAM_B1 = 0.9
ADAM_B2 = 0.999
ADAM_EPS = 1e-08
ADAM_WD = 0.01
ADAM_STEP = 10

MESH_AXES = ("x", "y", "c")


def _cparams(sem):
    return pltpu.CompilerParams(dimension_semantics=sem, vmem_limit_bytes=VMEM_LIMIT)


def _pick(n, prefs):
    for p in prefs:
        if n % p == 0:
            return p
    return n


def _dot(a, b, dims=(((1,), (0,)), ((), ()))):
    return lax.dot_general(a.astype(MXU_DTYPE), b.astype(MXU_DTYPE), dims, preferred_element_type=F32)


def _dot_nt(a, b):
    return _dot(a, b, (((1,), (1,)), ((), ())))


def _dot_tn(a, b):
    return _dot(a, b, (((0,), (0,)), ((), ())))


NN = (((1,), (0,)), ((), ()))
NT = (((1,), (1,)), ((), ()))
TN = (((0,), (0,)), ((), ()))


def _mxu(a, b, dims):
    return lax.dot_general(a, b, dims, preferred_element_type=F32)


def _split(x, pieces):
    out, r = [], x
    for i in range(pieces):
        p = r.astype(jnp.bfloat16)
        out.append(p)
        if i + 1 < pieces:
            r = r - p.astype(F32)
    return out


def _dot_x3(a, b, dims=NN):
    (ah, al), (bh, bl) = _split(a, 2), _split(b, 2)
    return (_mxu(al, bh, dims) + _mxu(ah, bl, dims)) + _mxu(ah, bh, dims)


def _sel(m, x, dims=NN):
    mb = m.astype(jnp.bfloat16)
    x1, x2, x3 = _split(x, 3)
    return (_mxu(mb, x3, dims) + _mxu(mb, x2, dims)) + _mxu(mb, x1, dims)


def _sel_nt(m, x):
    return _sel(m, x, NT)


def _xsel(x, m, dims=NN):
    mb = m.astype(jnp.bfloat16)
    x1, x2, x3 = _split(x, 3)
    return (_mxu(x3, mb, dims) + _mxu(x2, mb, dims)) + _mxu(x1, mb, dims)


def _xsel_nt(x, m):
    return _xsel(x, m, NT)


def _iota(shape, dim):
    return lax.broadcasted_iota(jnp.int32, shape, dim)


def _sigmoid(x):
    return 0.5 * jnp.tanh(0.5 * x) + 0.5


def _silu(x):
    return x * _sigmoid(x)


def _dsilu(x):
    s = _sigmoid(x)
    return s * (1.0 + x * (1.0 - s))


def _softplus(x):
    return jnp.maximum(x, 0.0) + jnp.log(1.0 + jnp.exp(-jnp.abs(x)))


def matmul(a, b, *, ta=False, tb=False, add=None, add_scale=1.0, name):
    if ta:
        kdim, m = a.shape
    else:
        m, kdim = a.shape
    n = b.shape[0] if tb else b.shape[1]
    assert (b.shape[1] if tb else b.shape[0]) == kdim
    tm = _pick(m, (1024, 896, 768, 512)) if ta else _pick(m, (2048, 1024, 512, 256, 128))
    tn = _pick(n, (1024, 896, 768, 512, 256, 128))
    tk = _pick(kdim, (1024, 512, 256)) if ta else _pick(kdim, (1024, 896, 768, 512))
    nk = kdim // tk
    dims = (((0 if ta else 1,), (1 if tb else 0,)), ((), ()))

    def body(a_ref, b_ref, *rest):
        o_ref = rest[-1]
        k = pl.program_id(2)
        part = _dot(a_ref[...], b_ref[...], dims)

        @pl.when(k == 0)
        def _():
            o_ref[...] = part if add is None else part + add_scale * rest[0][...]

        @pl.when(k > 0)
        def _():
            o_ref[...] += part

    a_spec = pl.BlockSpec((tk, tm), lambda i, j, k: (k, i)) if ta else pl.BlockSpec((tm, tk), lambda i, j, k: (i, k))
    b_spec = pl.BlockSpec((tn, tk), lambda i, j, k: (j, k)) if tb else pl.BlockSpec((tk, tn), lambda i, j, k: (k, j))
    o_spec = pl.BlockSpec((tm, tn), lambda i, j, k: (i, j))
    in_specs = [a_spec, b_spec] + ([] if add is None else [o_spec])
    args = (a, b) + (() if add is None else (add,))
    return pl.pallas_call(
        body, name=name, grid=(m // tm, n // tn, nk), in_specs=in_specs, out_specs=o_spec,
        out_shape=jax.ShapeDtypeStruct((m, n), F32),
        compiler_params=_cparams(("parallel", "parallel", "arbitrary")),
    )(*args)


LN_ROWS = 512


def _ln_stats(x, y):
    r = ALPHA * x + y
    mu = jnp.mean(r, axis=-1, keepdims=True)
    rc = r - mu
    var = jnp.mean(rc * rc, axis=-1, keepdims=True)
    rstd = lax.rsqrt(var + LN_EPS)
    return rc * rstd, rstd


def ln_fwd(x, y, g, b, *, name):
    rows, d = x.shape

    def body(x_ref, y_ref, g_ref, b_ref, o_ref):
        xhat, _ = _ln_stats(x_ref[...], y_ref[...])
        o_ref[...] = xhat * g_ref[...] + b_ref[...]

    blk = pl.BlockSpec((LN_ROWS, d), lambda i: (i, 0))
    vec = pl.BlockSpec((1, d), lambda i: (0, 0))
    return pl.pallas_call(
        body, name=name, grid=(rows // LN_ROWS,), in_specs=[blk, blk, vec, vec], out_specs=blk,
        out_shape=jax.ShapeDtypeStruct((rows, d), F32), compiler_params=_cparams(("parallel",)),
    )(x, y, g.reshape(1, d), b.reshape(1, d))


def ln_bwd(x, y, g, dxn=None, *, b=None, target=None, name):
    rows, d = x.shape
    final = target is not None

    def body(x_ref, y_ref, g_ref, *rest):
        if final:
            b_ref, t_ref, dr_ref, dg_ref, db_ref, loss_ref = rest
        else:
            dxn_ref, dr_ref, dg_ref, db_ref = rest
        i = pl.program_id(0)
        xhat, rstd = _ln_stats(x_ref[...], y_ref[...])
        gv = g_ref[...]
        if final:
            err = xhat * gv + b_ref[...] - t_ref[...]
            dxn_v = err * (1.0 / d)
            part = 0.5 * jnp.sum(jnp.mean(err * err, axis=-1, keepdims=True), axis=0, keepdims=True)
        else:
            dxn_v = dxn_ref[...]
        dxh = dxn_v * gv
        m1 = jnp.mean(dxh, axis=-1, keepdims=True)
        m2 = jnp.mean(dxh * xhat, axis=-1, keepdims=True)
        dr_ref[...] = rstd * (dxh - m1 - xhat * m2)

        @pl.when(i == 0)
        def _():
            dg_ref[...] = jnp.zeros_like(dg_ref)
            db_ref[...] = jnp.zeros_like(db_ref)
            if final:
                loss_ref[...] = jnp.zeros_like(loss_ref)

        dg_ref[...] += jnp.sum(dxn_v * xhat, axis=0, keepdims=True)
        db_ref[...] += jnp.sum(dxn_v, axis=0, keepdims=True)
        if final:
            loss_ref[...] += jnp.broadcast_to(part, loss_ref.shape)

    blk = pl.BlockSpec((LN_ROWS, d), lambda i: (i, 0))
    vec = pl.BlockSpec((1, d), lambda i: (0, 0))
    lvec = pl.BlockSpec((1, LANES), lambda i: (0, 0))
    out_shape = [jax.ShapeDtypeStruct((rows, d), F32), jax.ShapeDtypeStruct((1, d), F32), jax.ShapeDtypeStruct((1, d), F32)]
    out_specs = [blk, vec, vec]
    if final:
        in_specs = [blk, blk, vec, vec, blk]
        args = (x, y, g.reshape(1, d), b.reshape(1, d), target)
        out_shape.append(jax.ShapeDtypeStruct((1, LANES), F32))
        out_specs.append(lvec)
    else:
        in_specs = [blk, blk, vec, blk]
        args = (x, y, g.reshape(1, d), dxn)
    return pl.pallas_call(
        body, name=name, grid=(rows // LN_ROWS,), in_specs=in_specs, out_specs=out_specs, out_shape=out_shape,
        compiler_params=_cparams(("arbitrary",)),
    )(*args)


def _rows_from(ref, off, rows, cols=slice(None)):
    r = off % SUBLANES
    if r == 0:
        return ref[off:off + rows, cols]
    window = ref[off - r:off - r + rows + SUBLANES, cols]
    return pltpu.roll(window, rows + SUBLANES - r, axis=0)[:rows]


def _conv_from_ext(ext_ref, w, width, rows, cols=slice(None)):
    out = None
    for j in range(width):
        term = _rows_from(ext_ref, SUBLANES - (width - 1) + j, rows, cols) * w[j:j + 1, cols]
        out = term if out is None else out + term
    return out


def _conv_dgrad_from_ext(dext_ref, w, width, rows, cols):
    out = None
    for j in range(width):
        term = _rows_from(dext_ref, (width - 1) - j, rows, cols) * w[j:j + 1, cols]
        out = term if out is None else out + term
    return out


def _conv_bwd_from_ext(dext_ref, ext_ref, w, width, rows, cols):
    u = ext_ref[SUBLANES:, cols]
    du, dws = None, []
    for j in range(width):
        shifted = _rows_from(dext_ref, (width - 1) - j, rows, cols)
        term = shifted * w[j:j + 1, cols]
        du = term if du is None else du + term
        dws.append(jnp.sum(shifted * u, axis=0, keepdims=True))
    return du, dws


CONV_COLS = 256


SC_ROWS = 128


def sc_fwd(proj, conv_w, *, name):
    rows = proj.shape[0]
    nb = rows // SC_ROWS
    hb = SC_ROWS // SUBLANES

    def body(p_ref, halo_ref, w_ref, o_ref, ext_ref):
        i = pl.program_id(0)
        w = w_ref[...]
        for c0 in range(0, SC_W, CONV_COLS):
            cols, bc, cc, zc = (slice(k * SC_W + c0, k * SC_W + c0 + CONV_COLS) for k in range(4))
            ext_ref[0:SUBLANES, cols] = jnp.where(i == 0, 0.0, halo_ref[:, cc] * halo_ref[:, cols])
            ext_ref[SUBLANES:, cols] = p_ref[:, cc] * p_ref[:, cols]
            cv = _conv_from_ext(ext_ref, w, SC_CONV, SC_ROWS, cols)
            o_ref[:, cols] = (p_ref[:, bc] * cv * _silu(p_ref[:, zc])).astype(o_ref.dtype)

    return pl.pallas_call(
        body, name=name, grid=(nb,),
        in_specs=[pl.BlockSpec((SC_ROWS, SC_IN), lambda i: (i, 0)),
                  pl.BlockSpec((SUBLANES, SC_IN), lambda i: (jnp.maximum(i * hb - 1, 0), 0)),
                  pl.BlockSpec((SUBLANES, SC_W), lambda i: (0, 0))],
        out_specs=pl.BlockSpec((SC_ROWS, SC_W), lambda i: (i, 0)),
        out_shape=jax.ShapeDtypeStruct((rows, SC_W), MXU_DTYPE),
        scratch_shapes=[pltpu.VMEM((SC_ROWS + SUBLANES, SC_W), F32)],
        compiler_params=_cparams(("parallel",)),
    )(proj, proj, _pad_rows(conv_w))


def sc_bwd(proj, conv_w, dmix, *, name):
    rows = proj.shape[0]
    nb = rows // SC_ROWS
    hb = SC_ROWS // SUBLANES

    def body(p_ref, halo_ref, w_ref, dm_ref, dp_ref, dw_ref, ext_ref, dext_ref):
        i = pl.program_id(0)
        blk = nb - 1 - i
        w = w_ref[...]

        @pl.when(i == 0)
        def _():
            dext_ref[SC_ROWS:, :] = jnp.zeros((SUBLANES, SC_W), F32)
            dw_ref[...] = jnp.zeros_like(dw_ref)

        for c0 in range(0, SC_W, CONV_COLS):
            cols, bc, cc, zc = (slice(k * SC_W + c0, k * SC_W + c0 + CONV_COLS) for k in range(4))
            h, bg, cg, z = p_ref[:, cols], p_ref[:, bc], p_ref[:, cc], p_ref[:, zc]
            ext_ref[0:SUBLANES, cols] = jnp.where(blk == 0, 0.0, halo_ref[:, cc] * halo_ref[:, cols])
            ext_ref[SUBLANES:, cols] = cg * h
            taps = [_rows_from(ext_ref, SUBLANES - (SC_CONV - 1) + j, SC_ROWS, cols) for j in range(SC_CONV)]
            cv = None
            for j in range(SC_CONV):
                term = taps[j] * w[j:j + 1, cols]
                cv = term if cv is None else cv + term
            dm = dm_ref[:, cols]
            dy = dm * _silu(z)
            dp_ref[:, zc] = (dm * bg * cv * _dsilu(z)).astype(dp_ref.dtype)
            dp_ref[:, bc] = (dy * cv).astype(dp_ref.dtype)
            dcv = dy * bg
            dext_ref[0:SC_ROWS, cols] = dcv
            du = _conv_dgrad_from_ext(dext_ref, w, SC_CONV, SC_ROWS, cols)
            dp_ref[:, cols] = (du * cg).astype(dp_ref.dtype)
            dp_ref[:, cc] = (du * h).astype(dp_ref.dtype)
            for j in range(SC_CONV):
                dw_ref[j:j + 1, cols] += jnp.sum(taps[j] * dcv, axis=0, keepdims=True)
            dext_ref[SC_ROWS:, cols] = dcv[0:SUBLANES, :]

    return pl.pallas_call(
        body, name=name, grid=(nb,),
        in_specs=[pl.BlockSpec((SC_ROWS, SC_IN), lambda i: (nb - 1 - i, 0)),
                  pl.BlockSpec((SUBLANES, SC_IN), lambda i: (jnp.maximum((nb - 1 - i) * hb - 1, 0), 0)),
                  pl.BlockSpec((SUBLANES, SC_W), lambda i: (0, 0)),
                  pl.BlockSpec((SC_ROWS, SC_W), lambda i: (nb - 1 - i, 0))],
        out_specs=[pl.BlockSpec((SC_ROWS, SC_IN), lambda i: (nb - 1 - i, 0)),
                   pl.BlockSpec((SUBLANES, SC_W), lambda i: (0, 0))],
        out_shape=[jax.ShapeDtypeStruct((rows, SC_IN), MXU_DTYPE), jax.ShapeDtypeStruct((SUBLANES, SC_W), F32)],
        scratch_shapes=[pltpu.VMEM((SC_ROWS + SUBLANES, SC_W), F32), pltpu.VMEM((SC_ROWS + SUBLANES, SC_W), F32)],
        compiler_params=_cparams(("arbitrary",)),
    )(proj, proj, _pad_rows(conv_w), dmix)


def _pad_rows(w, rows=SUBLANES):
    return jnp.pad(w, ((0, rows - w.shape[0]), (0, 0)))


def _pad_lanes(v, lanes=LANES):
    v = v.reshape(1, -1)
    return jnp.pad(v, ((0, 0), (0, lanes - v.shape[1])))


def _tri(n, lower=True):
    r, c = _iota((n, n), 0), _iota((n, n), 1)
    return jnp.where((c <= r) if lower else (c >= r), 1.0, 0.0)


def _eye(n):
    return jnp.where(_iota((n, n), 0) == _iota((n, n), 1), 1.0, 0.0)


def _head_expand(n, width):
    return jnp.where(_iota((LANES, n), 1) // width == _iota((LANES, n), 0), 1.0, 0.0)


def _col(v, h):
    return jnp.sum(jnp.where(_iota(v.shape, 1) == h, v, 0.0), axis=1, keepdims=True)


def _row(v, r):
    return jnp.sum(jnp.where(_iota(v.shape, 0) == r, v, 0.0), axis=0, keepdims=True)


def _expand_row(v, e):
    return jnp.max(_xsel(jnp.broadcast_to(v, (SUBLANES, LANES)), e), axis=0, keepdims=True)


SSD_ROWS = 128
SSD_X0 = D_INNER
SSD_DT0 = D_INNER + SSD_CONV_DIM
SSD_B0 = D_INNER
SSD_C0 = D_INNER + SSD_G * SSD_S
SSD_GW = D_INNER // SSD_G
SSD_HG = SSD_H // SSD_G


def _ssd_prologue(blk, p_ref, halo_ref, cw_ref, cb_ref, dtb_ref, ext_ref, xbc_ref, dt_ref, pre_ref=None):
    ext_ref[0:SUBLANES, :] = jnp.where(blk == 0, 0.0, halo_ref[:, SSD_X0:SSD_DT0])
    ext_ref[SUBLANES:, :] = p_ref[:, SSD_X0:SSD_DT0]
    w = cw_ref[...]
    for c0 in range(0, SSD_CONV_DIM, CONV_COLS):
        cols = slice(c0, c0 + CONV_COLS)
        pre = _conv_from_ext(ext_ref, w, SSD_CONV, SSD_ROWS, cols) + cb_ref[:, cols]
        if pre_ref is not None:
            pre_ref[:, cols] = pre
        xbc_ref[:, cols] = _silu(pre)
    dt_ref[...] = _softplus(p_ref[:, SSD_DT0:SSD_DT0 + LANES] + dtb_ref[...])


def _ssd_chunk_decays(dt_c, a_row, ltri, eye_l, act_ref):
    da = dt_c * a_row
    ac = _sel(ltri,da)
    act_ref[...] = _sel_nt(eye_l,ac)
    ac_last = _row(ac, CHUNK - 1)
    return ac, jnp.exp(ac_last - ac), jnp.exp(ac), jnp.exp(ac_last)


def _ssd_seg(ac, act_ref, h, causal):
    return jnp.where(causal, jnp.exp(jnp.minimum(_col(ac, h) - act_ref[pl.ds(h, 1), :], 0.0)), 0.0)


def _ssd_half(pair, e):
    upper = _iota(pair.shape, 1) >= SSD_P
    return jnp.where(upper if e % 2 else jnp.logical_not(upper), pair, 0.0)


def _ssd_group_fwd(g, xbc_ref, rows, dt_exp, tail_exp, cdec_exp, ac, act_ref, s_g, causal):
    gl = slice(g * SSD_GW, (g + 1) * SSD_GW)
    bg = xbc_ref[rows, SSD_B0 + g * SSD_S:SSD_B0 + (g + 1) * SSD_S]
    cg = xbc_ref[rows, SSD_C0 + g * SSD_S:SSD_C0 + (g + 1) * SSD_S]
    xdt = xbc_ref[rows, gl] * dt_exp[:, gl]
    cb = _dot_nt(cg, bg)
    cs = _dot(cg, s_g)
    segs = [_ssd_seg(ac, act_ref, g * SSD_HG + e, causal) for e in range(SSD_HG)]
    gms = [seg * cb for seg in segs]
    xps = [xdt[:, p * LANES:(p + 1) * LANES] for p in range(SSD_HG // 2)]
    parts = [_dot(gms[e], _ssd_half(xps[e // 2], e)) for e in range(SSD_HG)]
    yd = jnp.concatenate([parts[2 * p] + parts[2 * p + 1] for p in range(SSD_HG // 2)], axis=1)
    st = _dot_tn(bg, xdt * tail_exp[:, gl])
    return yd + cs * cdec_exp[:, gl], st, bg, cg, cb, xdt, cs, segs, gms


def ssd_fwd(proj, conv_w, conv_b, a_log, dt_bias, d_skip, norm_w, *, name):
    rows_total = proj.shape[0]
    nb = rows_total // SSD_ROWS
    hb = SSD_ROWS // SUBLANES
    cpb = SSD_ROWS // CHUNK

    def body(p_ref, halo_ref, cw_ref, cb_ref, alog_ref, dtb_ref, dsk_ref, nw_ref, mix_ref, st_ref,
             ext_ref, xbc_ref, dt_ref, s_ref, act_ref):
        i = pl.program_id(0)

        @pl.when(i == 0)
        def _():
            s_ref[...] = jnp.zeros_like(s_ref)

        _ssd_prologue(i, p_ref, halo_ref, cw_ref, cb_ref, dtb_ref, ext_ref, xbc_ref, dt_ref)
        a_row = -jnp.exp(alog_ref[...])
        expand = _head_expand(D_INNER, SSD_P)
        dsk_exp = _expand_row(dsk_ref[...], expand)
        ltri, eye_l = _tri(CHUNK), _eye(LANES)
        causal = _iota((CHUNK, CHUNK), 1) <= _iota((CHUNK, CHUNK), 0)

        def chunk(c, carry):
            rows = pl.ds(pl.multiple_of(c * CHUNK, CHUNK), CHUNK)
            dt_c = dt_ref[rows, :]
            ac, tail, cdec, tot = _ssd_chunk_decays(dt_c, a_row, ltri, eye_l, act_ref)
            dt_exp = _xsel(dt_c, expand)
            tail_exp = _xsel(tail, expand)
            cdec_exp = _xsel(cdec, expand)
            tot_exp = _expand_row(tot, expand)
            for g in range(SSD_G):
                gl = slice(g * SSD_GW, (g + 1) * SSD_GW)
                s_g = s_ref[g]
                st_ref[c, g] = s_g
                y, st = _ssd_group_fwd(g, xbc_ref, rows, dt_exp, tail_exp, cdec_exp, ac, act_ref, s_g, causal)[:2]
                s_ref[g] = s_g * tot_exp[:, gl] + st
                y = (y + dsk_exp[:, gl] * xbc_ref[rows, gl]) * _silu(p_ref[rows, gl])
                r = lax.rsqrt(jnp.mean(y * y, axis=-1, keepdims=True) + RMS_EPS)
                mix_ref[rows, gl] = (y * r * nw_ref[:, gl]).astype(mix_ref.dtype)
            return carry

        lax.fori_loop(0, cpb, chunk, 0)

    vec = lambda n: pl.BlockSpec((1, n), lambda i: (0, 0))
    return pl.pallas_call(
        body, name=name, grid=(nb,),
        in_specs=[pl.BlockSpec((SSD_ROWS, SSD_IN_PAD), lambda i: (i, 0)),
                  pl.BlockSpec((SUBLANES, SSD_IN_PAD), lambda i: (jnp.maximum(i * hb - 1, 0), 0)),
                  pl.BlockSpec((SUBLANES, SSD_CONV_DIM), lambda i: (0, 0)),
                  vec(SSD_CONV_DIM), vec(LANES), vec(LANES), vec(LANES), vec(D_INNER)],
        out_specs=[pl.BlockSpec((SSD_ROWS, D_INNER), lambda i: (i, 0)),
                   pl.BlockSpec((cpb, SSD_G, SSD_S, SSD_GW), lambda i: (i, 0, 0, 0))],
        out_shape=[jax.ShapeDtypeStruct((rows_total, D_INNER), MXU_DTYPE),
                   jax.ShapeDtypeStruct((rows_total // CHUNK, SSD_G, SSD_S, SSD_GW), F32)],
        scratch_shapes=[pltpu.VMEM((SSD_ROWS + SUBLANES, SSD_CONV_DIM), F32),
                        pltpu.VMEM((SSD_ROWS, SSD_CONV_DIM), F32),
                        pltpu.VMEM((SSD_ROWS, LANES), F32),
                        pltpu.VMEM((SSD_G, SSD_S, SSD_GW), F32),
                        pltpu.VMEM((LANES, CHUNK), F32)],
        compiler_params=_cparams(("arbitrary",)),
    )(proj, proj, _pad_rows(conv_w), conv_b.reshape(1, -1), _pad_lanes(a_log), _pad_lanes(dt_bias),
      _pad_lanes(d_skip), norm_w.reshape(1, -1))


def ssd_bwd(proj, conv_w, conv_b, a_log, dt_bias, d_skip, norm_w, states, dmix, *, name):
    rows_total = proj.shape[0]
    nb = rows_total // SSD_ROWS
    hb = SSD_ROWS // SUBLANES
    cpb = SSD_ROWS // CHUNK

    def body(p_ref, halo_ref, cw_ref, cb_ref, alog_ref, dtb_ref, dsk_ref, nw_ref, st_ref, dm_ref,
             dp_ref, dcw_ref, dcb_ref, dalog_ref, ddtb_ref, ddsk_ref, dnw_ref,
             ext_ref, xbc_ref, dt_ref, ds_ref, act_ref, dext_ref, dac_ref, dact_ref, ddskw_ref, pre_ref):
        i = pl.program_id(0)
        blk = nb - 1 - i

        @pl.when(i == 0)
        def _():
            ds_ref[...] = jnp.zeros_like(ds_ref)
            dext_ref[SSD_ROWS:, :] = jnp.zeros((SUBLANES, SSD_CONV_DIM), F32)
            ddskw_ref[...] = jnp.zeros_like(ddskw_ref)
            for r in (dcw_ref, dcb_ref, dalog_ref, ddtb_ref, ddsk_ref, dnw_ref):
                r[...] = jnp.zeros_like(r)

        _ssd_prologue(blk, p_ref, halo_ref, cw_ref, cb_ref, dtb_ref, ext_ref, xbc_ref, dt_ref, pre_ref)
        a_row = -jnp.exp(alog_ref[...])
        expand = _head_expand(D_INNER, SSD_P)
        dsk_exp = _expand_row(dsk_ref[...], expand)
        ltri, utri, eye_l, eye_c = _tri(CHUNK), _tri(CHUNK, lower=False), _eye(LANES), _eye(CHUNK)
        causal = _iota((CHUNK, CHUNK), 1) <= _iota((CHUNK, CHUNK), 0)
        dp_ref[:, SSD_DT0 + LANES:] = jnp.zeros((SSD_ROWS, SSD_IN_PAD - SSD_DT0 - LANES), dp_ref.dtype)

        def chunk(cc, carry):
            c = cpb - 1 - cc
            rows = pl.ds(pl.multiple_of(c * CHUNK, CHUNK), CHUNK)
            dt_c = dt_ref[rows, :]
            ac, tail, cdec, tot = _ssd_chunk_decays(dt_c, a_row, ltri, eye_l, act_ref)
            dt_exp = _xsel(dt_c, expand)
            tail_exp = _xsel(tail, expand)
            cdec_exp = _xsel(cdec, expand)
            tot_exp = _expand_row(tot, expand)
            dac_ref[...] = jnp.zeros_like(dac_ref)
            dact_ref[...] = jnp.zeros_like(dact_ref)
            d_cdec = jnp.zeros((CHUNK, LANES), F32)
            d_tail = jnp.zeros((CHUNK, LANES), F32)
            d_dt = jnp.zeros((CHUNK, LANES), F32)
            d_tot = jnp.zeros((1, LANES), F32)
            for g in range(SSD_G):
                gl = slice(g * SSD_GW, (g + 1) * SSD_GW)
                ex_g = expand[:, gl]
                s_g = st_ref[c, g]
                y, _, bg, cg, cb, xdt, cs, segs, gms = _ssd_group_fwd(g, xbc_ref, rows, dt_exp, tail_exp, cdec_exp, ac, act_ref, s_g, causal)
                xs = xbc_ref[rows, gl]
                z = p_ref[rows, gl]
                sz = _silu(z)
                y2 = y + dsk_exp[:, gl] * xs
                yg = y2 * sz
                r = lax.rsqrt(jnp.mean(yg * yg, axis=-1, keepdims=True) + RMS_EPS)
                yn = yg * r
                dm = dm_ref[rows, gl]
                dnw_ref[:, gl] += jnp.sum(dm * yn, axis=0, keepdims=True)
                dyn = dm * nw_ref[:, gl]
                dyg = r * (dyn - yn * jnp.mean(dyn * yn, axis=-1, keepdims=True))
                dp_ref[rows, gl] = (dyg * y2 * _dsilu(z)).astype(dp_ref.dtype)
                dy = dyg * sz
                ddskw_ref[:, gl] += jnp.sum(dy * xs, axis=0, keepdims=True)
                ds_g = ds_ref[g]
                dyc = dy * cdec_exp[:, gl]
                ds_ref[g] = ds_g * tot_exp[:, gl] + _dot_tn(cg, dyc)
                sds = jnp.broadcast_to(jnp.sum(s_g * ds_g, axis=0, keepdims=True), (SUBLANES, SSD_GW))
                d_tot = d_tot + jnp.max(_xsel_nt(sds, ex_g), axis=0, keepdims=True)
                dcg = _dot_nt(dyc, s_g)
                d_cdec = d_cdec + _xsel_nt(dy * cs, ex_g)
                xdtd = xdt * tail_exp[:, gl]
                d_xdtd = _dot(bg, ds_g)
                dbg = _dot_nt(xdtd, ds_g)
                d_tail = d_tail + _xsel_nt(d_xdtd * xdt, ex_g)
                heads = range(SSD_HG)
                dy_h = [_ssd_half(dy[:, (e // 2) * LANES:(e // 2 + 1) * LANES], e) for e in heads]
                back = [_dot_tn(gms[e], dy_h[e]) for e in heads]
                dg_m = [jnp.where(causal, _dot_nt(dy_h[e], xdt[:, (e // 2) * LANES:(e // 2 + 1) * LANES]), 0.0) for e in heads]
                d_cb = None
                for e in heads:
                    h = g * SSD_HG + e
                    term = dg_m[e] * segs[e]
                    d_cb = term if d_cb is None else d_cb + term
                    em = dg_m[e] * gms[e]
                    dac_ref[...] += jnp.where(_iota((CHUNK, LANES), 1) == h, jnp.sum(em, axis=1, keepdims=True), 0.0)
                    dact_ref[h:h + 1, :] = jnp.sum(em, axis=0, keepdims=True)
                dcg = dcg + _dot(d_cb, bg)
                dbg = dbg + _dot_tn(d_cb, cg)
                d_xdt = d_xdtd * tail_exp[:, gl] + jnp.concatenate(
                    [back[2 * p] + back[2 * p + 1] for p in range(SSD_HG // 2)], axis=1)
                d_dt = d_dt + _xsel_nt(d_xdt * xs, ex_g)
                dext_ref[rows, gl] = d_xdt * dt_exp[:, gl] + dy * dsk_exp[:, gl]
                dext_ref[rows, SSD_B0 + g * SSD_S:SSD_B0 + (g + 1) * SSD_S] = dbg
                dext_ref[rows, SSD_C0 + g * SSD_S:SSD_C0 + (g + 1) * SSD_S] = dcg
            d_ac = dac_ref[...] - _sel_nt(eye_c,dact_ref[...]) + d_cdec * cdec - d_tail * tail
            d_last = jnp.sum(d_tail * tail, axis=0, keepdims=True) + d_tot * tot
            d_ac = jnp.where(_iota((CHUNK, LANES), 0) == CHUNK - 1, d_ac + d_last, d_ac)
            d_da = _sel(utri,d_ac)
            d_dt = d_dt + d_da * a_row
            dalog_ref[...] += jnp.sum(d_da * dt_c, axis=0, keepdims=True) * a_row
            d_raw = d_dt * _sigmoid(p_ref[rows, SSD_DT0:SSD_DT0 + LANES] + dtb_ref[...])
            d_raw = jnp.where(_iota((CHUNK, LANES), 1) < SSD_H, d_raw, 0.0)
            ddtb_ref[...] += jnp.sum(d_raw, axis=0, keepdims=True)
            dp_ref[rows, SSD_DT0:SSD_DT0 + LANES] = d_raw.astype(dp_ref.dtype)
            return carry

        lax.fori_loop(0, cpb, chunk, 0)
        w = cw_ref[...]
        for c0 in range(0, SSD_CONV_DIM, CONV_COLS):
            cols = slice(c0, c0 + CONV_COLS)
            d_pre = dext_ref[0:SSD_ROWS, cols] * _dsilu(pre_ref[:, cols])
            dext_ref[0:SSD_ROWS, cols] = d_pre
            dcb_ref[:, cols] += jnp.sum(d_pre, axis=0, keepdims=True)
            du, dws = _conv_bwd_from_ext(dext_ref, ext_ref, w, SSD_CONV, SSD_ROWS, cols)
            for j in range(SSD_CONV):
                dcw_ref[j:j + 1, cols] += dws[j]
            dp_ref[:, SSD_X0 + c0:SSD_X0 + c0 + CONV_COLS] = du.astype(dp_ref.dtype)
            dext_ref[SSD_ROWS:, cols] = d_pre[0:SUBLANES, :]

        @pl.when(i == nb - 1)
        def _():
            ddsk_ref[...] = jnp.max(_xsel_nt(jnp.broadcast_to(ddskw_ref[...], (SUBLANES, D_INNER)), expand), axis=0, keepdims=True)

    vec = lambda n: pl.BlockSpec((1, n), lambda i: (0, 0))
    outs = pl.pallas_call(
        body, name=name, grid=(nb,),
        in_specs=[pl.BlockSpec((SSD_ROWS, SSD_IN_PAD), lambda i: (nb - 1 - i, 0)),
                  pl.BlockSpec((SUBLANES, SSD_IN_PAD), lambda i: (jnp.maximum((nb - 1 - i) * hb - 1, 0), 0)),
                  pl.BlockSpec((SUBLANES, SSD_CONV_DIM), lambda i: (0, 0)),
                  vec(SSD_CONV_DIM), vec(LANES), vec(LANES), vec(LANES), vec(D_INNER),
                  pl.BlockSpec((cpb, SSD_G, SSD_S, SSD_GW), lambda i: (nb - 1 - i, 0, 0, 0)),
                  pl.BlockSpec((SSD_ROWS, D_INNER), lambda i: (nb - 1 - i, 0))],
        out_specs=[pl.BlockSpec((SSD_ROWS, SSD_IN_PAD), lambda i: (nb - 1 - i, 0)),
                   pl.BlockSpec((SUBLANES, SSD_CONV_DIM), lambda i: (0, 0)),
                   vec(SSD_CONV_DIM), vec(LANES), vec(LANES), vec(LANES), vec(D_INNER)],
        out_shape=[jax.ShapeDtypeStruct((rows_total, SSD_IN_PAD), MXU_DTYPE),
                   jax.ShapeDtypeStruct((SUBLANES, SSD_CONV_DIM), F32),
                   jax.ShapeDtypeStruct((1, SSD_CONV_DIM), F32), jax.ShapeDtypeStruct((1, LANES), F32),
                   jax.ShapeDtypeStruct((1, LANES), F32), jax.ShapeDtypeStruct((1, LANES), F32),
                   jax.ShapeDtypeStruct((1, D_INNER), F32)],
        scratch_shapes=[pltpu.VMEM((SSD_ROWS + SUBLANES, SSD_CONV_DIM), F32),
                        pltpu.VMEM((SSD_ROWS, SSD_CONV_DIM), F32),
                        pltpu.VMEM((SSD_ROWS, LANES), F32),
                        pltpu.VMEM((SSD_G, SSD_S, SSD_GW), F32),
                        pltpu.VMEM((LANES, CHUNK), F32),
                        pltpu.VMEM((SSD_ROWS + SUBLANES, SSD_CONV_DIM), F32),
                        pltpu.VMEM((CHUNK, LANES), F32),
                        pltpu.VMEM((LANES, CHUNK), F32),
                        pltpu.VMEM((1, D_INNER), F32),
                        pltpu.VMEM((SSD_ROWS, SSD_CONV_DIM), F32)],
        compiler_params=_cparams(("arbitrary",)),
    )(proj, proj, _pad_rows(conv_w), conv_b.reshape(1, -1), _pad_lanes(a_log), _pad_lanes(dt_bias),
      _pad_lanes(d_skip), norm_w.reshape(1, -1), states, dmix)
    dproj, dcw, dcb, dalog, ddtb, ddsk, dnw = outs
    return dproj, [dcw[:SSD_CONV], dcb[0], dalog[0, :SSD_H], ddtb[0, :SSD_H], ddsk[0, :SSD_H], dnw[0]]


GDN_ROWS = 128
GDN_K0 = GDN_QK
GDN_V0 = 2 * GDN_QK
GDN_Z0 = GDN_CONV_DIM
GDN_BA0 = GDN_CONV_DIM + GDN_V
GDN_GL = GDN_VH
GDN_SCALE = GDN_HEAD ** -0.5
GDN_GROUP = 8


def _gdn_lane_params(v):
    return jnp.pad(v.reshape(1, GDN_VH), ((0, 0), (GDN_GL, LANES - GDN_GL - GDN_VH)))


def _inv_unit_lower(a, eye_c):
    x = eye_c - a
    ph, pl_ = _split(a, 2)
    n = 2
    while n < CHUNK:
        p = (_mxu(pl_, ph, NN) + _mxu(ph, pl_, NN)) + _mxu(ph, ph, NN)
        ph, pl_ = _split(p, 2)
        xh, xl = _split(x, 2)
        x = x + ((_mxu(xl, ph, NN) + _mxu(xh, pl_, NN)) + _mxu(xh, ph, NN))
        n *= 2
    return x


def _gdn_prologue(blk, p_ref, halo_ref, cw_ref, alog_ref, dtb_ref, ext_ref, qkv_ref, beta_ref, g_ref, pre_ref=None):
    ext_ref[0:SUBLANES, :] = jnp.where(blk == 0, 0.0, halo_ref[:, 0:GDN_CONV_DIM])
    ext_ref[SUBLANES:, :] = p_ref[:, 0:GDN_CONV_DIM]
    w = cw_ref[...]
    for hh in range(GDN_CONV_DIM // GDN_HEAD):
        cols = slice(hh * GDN_HEAD, (hh + 1) * GDN_HEAD)
        pre = _conv_from_ext(ext_ref, w, GDN_CONV, GDN_ROWS, cols)
        if pre_ref is not None:
            pre_ref[:, cols] = pre
        a = _silu(pre)
        if hh < 2 * GDN_QKH:
            r = lax.rsqrt(jnp.sum(a * a, axis=-1, keepdims=True) + L2_EPS)
            a = a * (r * (GDN_SCALE if hh < GDN_QKH else 1.0))
        qkv_ref[:, cols] = a
    ba = p_ref[:, GDN_BA0:GDN_BA0 + LANES]
    beta_ref[...] = _sigmoid(ba)
    g_ref[...] = -jnp.exp(alog_ref[...]) * _softplus(ba + dtb_ref[...])


def _each(f, *lists):
    return [f(*z) for z in zip(*lists)]


def _inv_unit_lower_each(a_list, eye_c):
    xs = [eye_c - a for a in a_list]
    ps = [_split(a, 2) for a in a_list]
    n = 2
    while n < CHUNK:
        ps = [_split((_mxu(pl_, ph, NN) + _mxu(ph, pl_, NN)) + _mxu(ph, ph, NN), 2) for ph, pl_ in ps]
        xs_split = [_split(x, 2) for x in xs]
        xs = [x + ((_mxu(xl, ph, NN) + _mxu(xh, pl_, NN)) + _mxu(xh, ph, NN))
              for x, (xh, xl), (ph, pl_) in zip(xs, xs_split, ps)]
        n *= 2
    return xs


def _gdn_heads_fwd(q, k, v, kk, qk, gcol, grow, glast, bcol, s, causal, strict, eye_c, t=None):
    decay = _each(lambda gc_, gr_: jnp.where(causal, jnp.exp(jnp.minimum(gc_ - gr_, 0.0)), 0.0), gcol, grow)
    egc = _each(jnp.exp, gcol)
    etail = _each(lambda gl_, gc_: jnp.exp(gl_ - gc_), glast, gcol)
    cd = _each(jnp.exp, glast)
    a = _each(lambda b_, kk_, d_: jnp.where(strict, b_ * kk_ * d_, 0.0), bcol, kk, decay)
    if t is None:
        t = _inv_unit_lower_each(a, eye_c)
    kb = _each(lambda k_, b_: k_ * b_, k, bcol)
    rhs_w = _each(lambda kb_, e_: kb_ * e_, kb, egc)
    u = _each(lambda t_, v_, b_: _dot_x3(t_, v_ * b_), t, v, bcol)
    w = _each(_dot_x3, t, rhs_w)
    attn = _each(lambda qk_, d_: qk_ * d_, qk, decay)
    ws = _each(_dot, w, s)
    v_new = _each(lambda u_, ws_: u_ - ws_, u, ws)
    qd = _each(lambda q_, e_: q_ * e_, q, egc)
    kt = _each(lambda k_, e_: k_ * e_, k, etail)
    o1 = _each(_dot, qd, s)
    o2 = _each(_dot, attn, v_new)
    out = _each(lambda a_, b_: a_ + b_, o1, o2)
    upd = _each(_dot_tn, kt, v_new)
    s_new = _each(lambda s_, c_, u_: s_ * c_ + u_, s, cd, upd)
    return dict(decay=decay, egc=egc, etail=etail, cd=cd, a=a, t=t, kb=kb, rhs_w=rhs_w, u=u, w=w, attn=attn,
                v_new=v_new, qd=qd, kt=kt, out=out, s_new=s_new)


def gdn_fwd(proj, conv_w, a_log, dt_bias, norm_w, *, name):
    rows_total = proj.shape[0]
    nb = rows_total // GDN_ROWS
    hb = GDN_ROWS // SUBLANES
    cpb = GDN_ROWS // CHUNK

    def body(p_ref, halo_ref, cw_ref, alog_ref, dtb_ref, nw_ref, mix_ref, st_ref, tm_ref,
             ext_ref, qkv_ref, beta_ref, g_ref, s_ref, gct_ref):
        i = pl.program_id(0)

        @pl.when(i == 0)
        def _():
            s_ref[...] = jnp.zeros_like(s_ref)

        _gdn_prologue(i, p_ref, halo_ref, cw_ref, alog_ref, dtb_ref, ext_ref, qkv_ref, beta_ref, g_ref)
        ltri, eye_l, eye_c = _tri(CHUNK), _eye(LANES), _eye(CHUNK)
        causal = _iota((CHUNK, CHUNK), 1) <= _iota((CHUNK, CHUNK), 0)
        strict = _iota((CHUNK, CHUNK), 1) < _iota((CHUNK, CHUNK), 0)
        nw = nw_ref[...]

        def chunk(c, carry):
            rows = pl.ds(pl.multiple_of(c * CHUNK, CHUNK), CHUNK)
            gc = _sel(ltri,g_ref[rows, :])
            gct_ref[...] = _sel_nt(eye_l,gc)
            glast_row = _row(gc, CHUNK - 1)
            beta_c = beta_ref[rows, :]
            for h0 in range(0, GDN_VH, GDN_GROUP):
                hs = list(range(h0, h0 + GDN_GROUP))
                qs = {hq: qkv_ref[rows, hq * GDN_HEAD:(hq + 1) * GDN_HEAD] for hq in range(h0 // 2, (h0 + GDN_GROUP) // 2)}
                ks = {hq: qkv_ref[rows, GDN_K0 + hq * GDN_HEAD:GDN_K0 + (hq + 1) * GDN_HEAD] for hq in qs}
                kks = {hq: _dot_nt(ks[hq], ks[hq]) for hq in qs}
                qks = {hq: _dot_nt(qs[hq], ks[hq]) for hq in qs}
                ss = [s_ref[h] for h in hs]
                for h, s in zip(hs, ss):
                    st_ref[c, h] = s
                f = _gdn_heads_fwd(
                    [qs[h // 2] for h in hs], [ks[h // 2] for h in hs],
                    [qkv_ref[rows, GDN_V0 + h * GDN_HEAD:GDN_V0 + (h + 1) * GDN_HEAD] for h in hs],
                    [kks[h // 2] for h in hs], [qks[h // 2] for h in hs],
                    [_col(gc, GDN_GL + h) for h in hs], [gct_ref[GDN_GL + h:GDN_GL + h + 1, :] for h in hs],
                    [_col(glast_row, GDN_GL + h) for h in hs], [_col(beta_c, h) for h in hs], ss, causal, strict, eye_c)
                for i_h, h in enumerate(hs):
                    hc = slice(h * GDN_HEAD, (h + 1) * GDN_HEAD)
                    s_ref[h] = f["s_new"][i_h]
                    tm_ref[c, h] = f["t"][i_h]
                    o = f["out"][i_h]
                    r = lax.rsqrt(jnp.mean(o * o, axis=-1, keepdims=True) + RMS_EPS)
                    z = p_ref[rows, GDN_Z0 + h * GDN_HEAD:GDN_Z0 + (h + 1) * GDN_HEAD]
                    mix_ref[rows, hc] = (o * r * nw * _silu(z)).astype(mix_ref.dtype)
            return carry

        lax.fori_loop(0, cpb, chunk, 0)

    vec = lambda n: pl.BlockSpec((1, n), lambda i: (0, 0))
    mix, states, tmats = pl.pallas_call(
        body, name=name, grid=(nb,),
        in_specs=[pl.BlockSpec((GDN_ROWS, GDN_IN_PAD), lambda i: (i, 0)),
                  pl.BlockSpec((SUBLANES, GDN_IN_PAD), lambda i: (jnp.maximum(i * hb - 1, 0), 0)),
                  pl.BlockSpec((SUBLANES, GDN_CONV_DIM), lambda i: (0, 0)),
                  vec(LANES), vec(LANES), vec(GDN_HEAD)],
        out_specs=[pl.BlockSpec((GDN_ROWS, GDN_V), lambda i: (i, 0)),
                   pl.BlockSpec((cpb, GDN_VH, GDN_HEAD, GDN_HEAD), lambda i: (i, 0, 0, 0)),
                   pl.BlockSpec((cpb, GDN_VH, CHUNK, CHUNK), lambda i: (i, 0, 0, 0))],
        out_shape=[jax.ShapeDtypeStruct((rows_total, GDN_V), MXU_DTYPE),
                   jax.ShapeDtypeStruct((rows_total // CHUNK, GDN_VH, GDN_HEAD, GDN_HEAD), F32),
                   jax.ShapeDtypeStruct((rows_total // CHUNK, GDN_VH, CHUNK, CHUNK), F32)],
        scratch_shapes=[pltpu.VMEM((GDN_ROWS + SUBLANES, GDN_CONV_DIM), F32),
                        pltpu.VMEM((GDN_ROWS, GDN_CONV_DIM), F32),
                        pltpu.VMEM((GDN_ROWS, LANES), F32),
                        pltpu.VMEM((GDN_ROWS, LANES), F32),
                        pltpu.VMEM((GDN_VH, GDN_HEAD, GDN_HEAD), F32),
                        pltpu.VMEM((LANES, CHUNK), F32)],
        compiler_params=_cparams(("arbitrary",)),
    )(proj, proj, _pad_rows(conv_w), _gdn_lane_params(a_log), _gdn_lane_params(dt_bias), norm_w.reshape(1, -1))
    return mix, (states, tmats)


def gdn_bwd(proj, conv_w, a_log, dt_bias, norm_w, saved, dmix, *, name):
    states, tmats = saved
    rows_total = proj.shape[0]
    nb = rows_total // GDN_ROWS
    hb = GDN_ROWS // SUBLANES
    cpb = GDN_ROWS // CHUNK

    def body(p_ref, halo_ref, cw_ref, alog_ref, dtb_ref, nw_ref, st_ref, tm_ref, dm_ref,
             dp_ref, dcw_ref, dalog_ref, ddtb_ref, dnw_ref,
             ext_ref, qkv_ref, beta_ref, g_ref, ds_ref, gct_ref, dext_ref, dgc_ref, dgct_ref, dbeta_ref, pre_ref):
        i = pl.program_id(0)
        blk = nb - 1 - i

        @pl.when(i == 0)
        def _():
            ds_ref[...] = jnp.zeros_like(ds_ref)
            dext_ref[GDN_ROWS:, :] = jnp.zeros((SUBLANES, GDN_CONV_DIM), F32)
            for r in (dcw_ref, dalog_ref, ddtb_ref, dnw_ref):
                r[...] = jnp.zeros_like(r)

        _gdn_prologue(blk, p_ref, halo_ref, cw_ref, alog_ref, dtb_ref, ext_ref, qkv_ref, beta_ref, g_ref, pre_ref)
        ltri, utri, eye_l, eye_c = _tri(CHUNK), _tri(CHUNK, lower=False), _eye(LANES), _eye(CHUNK)
        causal = _iota((CHUNK, CHUNK), 1) <= _iota((CHUNK, CHUNK), 0)
        strict = _iota((CHUNK, CHUNK), 1) < _iota((CHUNK, CHUNK), 0)
        lane = _iota((CHUNK, LANES), 1)
        is_last = _iota((CHUNK, 1), 0) == CHUNK - 1
        nw = nw_ref[...]

        def chunk(cc, carry):
            c = cpb - 1 - cc
            rows = pl.ds(pl.multiple_of(c * CHUNK, CHUNK), CHUNK)
            g_c = g_ref[rows, :]
            gc = _sel(ltri,g_c)
            gct_ref[...] = _sel_nt(eye_l,gc)
            glast_row = _row(gc, CHUNK - 1)
            beta_c = beta_ref[rows, :]
            dgc_ref[...] = jnp.zeros_like(dgc_ref)
            dgct_ref[...] = jnp.zeros_like(dgct_ref)
            dbeta_ref[...] = jnp.zeros_like(dbeta_ref)
            for h0 in range(0, GDN_VH, GDN_GROUP):
                hs = list(range(h0, h0 + GDN_GROUP))
                hqs = list(range(h0 // 2, (h0 + GDN_GROUP) // 2))
                qs = {hq: qkv_ref[rows, hq * GDN_HEAD:(hq + 1) * GDN_HEAD] for hq in hqs}
                ks = {hq: qkv_ref[rows, GDN_K0 + hq * GDN_HEAD:GDN_K0 + (hq + 1) * GDN_HEAD] for hq in hqs}
                kks = {hq: _dot_nt(ks[hq], ks[hq]) for hq in hqs}
                qks = {hq: _dot_nt(qs[hq], ks[hq]) for hq in hqs}
                q = [qs[h // 2] for h in hs]
                k = [ks[h // 2] for h in hs]
                v = [qkv_ref[rows, GDN_V0 + h * GDN_HEAD:GDN_V0 + (h + 1) * GDN_HEAD] for h in hs]
                s = [st_ref[c, h] for h in hs]
                bcol = [_col(beta_c, h) for h in hs]
                f = _gdn_heads_fwd(q, k, v, [kks[h // 2] for h in hs], [qks[h // 2] for h in hs],
                                   [_col(gc, GDN_GL + h) for h in hs], [gct_ref[GDN_GL + h:GDN_GL + h + 1, :] for h in hs],
                                   [_col(glast_row, GDN_GL + h) for h in hs], bcol, s, causal, strict, eye_c,
                                   t=[tm_ref[c, h] for h in hs])
                do = []
                for i_h, h in enumerate(hs):
                    zc = slice(GDN_Z0 + h * GDN_HEAD, GDN_Z0 + (h + 1) * GDN_HEAD)
                    o = f["out"][i_h]
                    z = p_ref[rows, zc]
                    sz = _silu(z)
                    r = lax.rsqrt(jnp.mean(o * o, axis=-1, keepdims=True) + RMS_EPS)
                    on = o * r
                    dm = dm_ref[rows, h * GDN_HEAD:(h + 1) * GDN_HEAD]
                    dnw_ref[...] += jnp.sum(dm * on * sz, axis=0, keepdims=True)
                    d_on = dm * nw * sz
                    dp_ref[rows, zc] = (dm * on * nw * _dsilu(z)).astype(dp_ref.dtype)
                    do.append(r * (d_on - on * jnp.mean(d_on * on, axis=-1, keepdims=True)))
                ds_n = [ds_ref[h] for h in hs]
                dv1 = _each(_dot_tn, f["attn"], do)
                dv2 = _each(_dot, f["kt"], ds_n)
                d_vnew = _each(lambda a_, b_: a_ + b_, dv1, dv2)
                d_attn = _each(lambda do_, vn_: jnp.where(causal, _dot_nt(do_, vn_), 0.0), do, f["v_new"])
                d_qd = _each(_dot_nt, do, s)
                t1 = _each(_dot_tn, f["qd"], do)
                t2 = _each(_dot_tn, f["w"], d_vnew)
                for h, a_, cd_, dsn_, b_ in zip(hs, t1, f["cd"], ds_n, t2):
                    ds_ref[h] = a_ + cd_ * dsn_ - b_
                d_cd = _each(lambda s_, dsn_: jnp.sum(jnp.sum(s_ * dsn_, axis=1, keepdims=True), axis=0, keepdims=True), s, ds_n)
                d_kt = _each(_dot_nt, f["v_new"], ds_n)
                d_w = _each(lambda dv_, s_: -_dot_nt(dv_, s_), d_vnew, s)
                d_rhs_u = _each(lambda t_, d_: _dot_x3(t_, d_, TN), f["t"], d_vnew)
                d_rhs_w = _each(lambda t_, d_: _dot_x3(t_, d_, TN), f["t"], d_w)
                m1 = _each(_dot_nt, d_rhs_u, f["u"])
                m2 = _each(_dot_nt, d_rhs_w, f["w"])
                da = _each(lambda a_, b_: -jnp.where(strict, a_ + b_, 0.0), m1, m2)
                dmm = _each(lambda a_, b_: a_ * b_, da, f["decay"])
                em = _each(lambda da_, a_, dat_, at_: da_ * a_ + dat_ * at_, da, f["a"], d_attn, f["attn"])
                x1 = _each(_dot, dmm, k)
                d_kb = _each(lambda x_, drw_, e_: x_ + drw_ * e_, x1, d_rhs_w, f["egc"])
                dk1 = _each(_dot_tn, dmm, f["kb"])
                dpm = _each(lambda a_, b_: a_ * b_, d_attn, f["decay"])
                dq1 = _each(_dot, dpm, k)
                dq = _each(lambda x_, dqd_, e_: x_ + dqd_ * e_, dq1, d_qd, f["egc"])
                dk2 = _each(_dot_tn, dpm, q)
                dk = _each(lambda a_, b_, dkb_, bc_, dkt_, et_: a_ + b_ + dkb_ * bc_ + dkt_ * et_,
                           dk1, dk2, d_kb, bcol, d_kt, f["etail"])
                for i_h, h in enumerate(hs):
                    tmp = jnp.sum(d_kt[i_h] * f["kt"][i_h], axis=1, keepdims=True)
                    d_gcol = (jnp.sum(em[i_h], axis=1, keepdims=True)
                              + jnp.sum(d_rhs_w[i_h] * f["rhs_w"][i_h], axis=1, keepdims=True)
                              + jnp.sum(d_qd[i_h] * f["qd"][i_h], axis=1, keepdims=True) - tmp)
                    d_glast = jnp.sum(tmp, axis=0, keepdims=True) + d_cd[i_h] * f["cd"][i_h]
                    d_gcol = jnp.where(is_last, d_gcol + d_glast, d_gcol)
                    d_beta = (jnp.sum(d_rhs_u[i_h] * v[i_h], axis=1, keepdims=True)
                              + jnp.sum(d_kb[i_h] * k[i_h], axis=1, keepdims=True))
                    dgc_ref[...] += jnp.where(lane == GDN_GL + h, d_gcol, 0.0)
                    dgct_ref[GDN_GL + h:GDN_GL + h + 1, :] = jnp.sum(em[i_h], axis=0, keepdims=True)
                    dbeta_ref[...] += jnp.where(lane == h, d_beta, 0.0)
                    dext_ref[rows, GDN_V0 + h * GDN_HEAD:GDN_V0 + (h + 1) * GDN_HEAD] = d_rhs_u[i_h] * bcol[i_h]
                for hq in hqs:
                    i0 = 2 * hq - h0
                    dext_ref[rows, hq * GDN_HEAD:(hq + 1) * GDN_HEAD] = dq[i0] + dq[i0 + 1]
                    dext_ref[rows, GDN_K0 + hq * GDN_HEAD:GDN_K0 + (hq + 1) * GDN_HEAD] = dk[i0] + dk[i0 + 1]
            d_gc = dgc_ref[...] - _sel_nt(eye_c,dgct_ref[...])
            dg = _sel(utri,d_gc)
            ba = p_ref[rows, GDN_BA0:GDN_BA0 + LANES]
            d_sp = dg * -jnp.exp(alog_ref[...])
            d_araw = d_sp * _sigmoid(ba + dtb_ref[...])
            d_araw = jnp.where((lane >= GDN_GL) & (lane < GDN_GL + GDN_VH), d_araw, 0.0)
            dalog_ref[...] += jnp.sum(dg * g_c, axis=0, keepdims=True)
            ddtb_ref[...] += jnp.sum(d_araw, axis=0, keepdims=True)
            d_braw = jnp.where(lane < GDN_VH, dbeta_ref[...] * beta_c * (1.0 - beta_c), 0.0)
            dp_ref[rows, GDN_BA0:GDN_BA0 + LANES] = (d_braw + d_araw).astype(dp_ref.dtype)
            return carry

        lax.fori_loop(0, cpb, chunk, 0)
        w = cw_ref[...]
        for hh in range(GDN_CONV_DIM // GDN_HEAD):
            cols = slice(hh * GDN_HEAD, (hh + 1) * GDN_HEAD)
            pre = pre_ref[:, cols]
            d_act = dext_ref[0:GDN_ROWS, cols]
            if hh < 2 * GDN_QKH:
                a = _silu(pre)
                r = lax.rsqrt(jnp.sum(a * a, axis=-1, keepdims=True) + L2_EPS)
                ah = a * r
                if hh < GDN_QKH:
                    d_act = d_act * GDN_SCALE
                d_act = r * (d_act - ah * jnp.sum(d_act * ah, axis=-1, keepdims=True))
            d_pre = d_act * _dsilu(pre)
            dext_ref[0:GDN_ROWS, cols] = d_pre
            du, dws = _conv_bwd_from_ext(dext_ref, ext_ref, w, GDN_CONV, GDN_ROWS, cols)
            for j in range(GDN_CONV):
                dcw_ref[j:j + 1, cols] += dws[j]
            dp_ref[:, cols] = du.astype(dp_ref.dtype)
            dext_ref[GDN_ROWS:, cols] = d_pre[0:SUBLANES, :]

    vec = lambda n: pl.BlockSpec((1, n), lambda i: (0, 0))
    outs = pl.pallas_call(
        body, name=name, grid=(nb,),
        in_specs=[pl.BlockSpec((GDN_ROWS, GDN_IN_PAD), lambda i: (nb - 1 - i, 0)),
                  pl.BlockSpec((SUBLANES, GDN_IN_PAD), lambda i: (jnp.maximum((nb - 1 - i) * hb - 1, 0), 0)),
                  pl.BlockSpec((SUBLANES, GDN_CONV_DIM), lambda i: (0, 0)),
                  vec(LANES), vec(LANES), vec(GDN_HEAD),
                  pl.BlockSpec((cpb, GDN_VH, GDN_HEAD, GDN_HEAD), lambda i: (nb - 1 - i, 0, 0, 0)),
                  pl.BlockSpec((cpb, GDN_VH, CHUNK, CHUNK), lambda i: (nb - 1 - i, 0, 0, 0)),
                  pl.BlockSpec((GDN_ROWS, GDN_V), lambda i: (nb - 1 - i, 0))],
        out_specs=[pl.BlockSpec((GDN_ROWS, GDN_IN_PAD), lambda i: (nb - 1 - i, 0)),
                   pl.BlockSpec((SUBLANES, GDN_CONV_DIM), lambda i: (0, 0)),
                   vec(LANES), vec(LANES), vec(GDN_HEAD)],
        out_shape=[jax.ShapeDtypeStruct((rows_total, GDN_IN_PAD), MXU_DTYPE),
                   jax.ShapeDtypeStruct((SUBLANES, GDN_CONV_DIM), F32),
                   jax.ShapeDtypeStruct((1, LANES), F32), jax.ShapeDtypeStruct((1, LANES), F32),
                   jax.ShapeDtypeStruct((1, GDN_HEAD), F32)],
        scratch_shapes=[pltpu.VMEM((GDN_ROWS + SUBLANES, GDN_CONV_DIM), F32),
                        pltpu.VMEM((GDN_ROWS, GDN_CONV_DIM), F32),
                        pltpu.VMEM((GDN_ROWS, LANES), F32),
                        pltpu.VMEM((GDN_ROWS, LANES), F32),
                        pltpu.VMEM((GDN_VH, GDN_HEAD, GDN_HEAD), F32),
                        pltpu.VMEM((LANES, CHUNK), F32),
                        pltpu.VMEM((GDN_ROWS + SUBLANES, GDN_CONV_DIM), F32),
                        pltpu.VMEM((CHUNK, LANES), F32),
                        pltpu.VMEM((LANES, CHUNK), F32),
                        pltpu.VMEM((CHUNK, LANES), F32),
                        pltpu.VMEM((GDN_ROWS, GDN_CONV_DIM), F32)],
        compiler_params=_cparams(("arbitrary",)),
    )(proj, proj, _pad_rows(conv_w), _gdn_lane_params(a_log), _gdn_lane_params(dt_bias), norm_w.reshape(1, -1),
      states, tmats, dmix)
    dproj, dcw, dalog, ddtb, dnw = outs
    return dproj, [dcw[:GDN_CONV], dalog[0, GDN_GL:GDN_GL + GDN_VH], ddtb[0, GDN_GL:GDN_GL + GDN_VH], dnw[0]]


def chip_exchange(src, *, scatter, name):
    piece_shape = src.shape[1:]

    def body(src_ref, out_ref, send_sems, recv_sems, local_sem):
        x, y, c = (lax.axis_index(a) for a in MESH_AXES)
        me = 2 * x + y

        def piece(j):
            return src_ref.at[j] if scatter else src_ref.at[c]

        local = pltpu.make_async_copy(piece(me), out_ref.at[me], local_sem)
        local.start()
        copies = []
        for k in range(1, N_SHARDS):
            px = 1 - x if k & 2 else x
            py = 1 - y if k & 1 else y
            cp = pltpu.make_async_remote_copy(
                src_ref=piece(2 * px + py), dst_ref=out_ref.at[me], send_sem=send_sems.at[k - 1],
                recv_sem=recv_sems.at[k - 1], device_id=(px, py, c), device_id_type=pl.DeviceIdType.MESH)
            cp.start()
            copies.append(cp)
        for cp in copies:
            cp.wait()
        local.wait()

    hbm = pl.BlockSpec(memory_space=pl.ANY)
    return pl.pallas_call(
        body, name=name, in_specs=[hbm], out_specs=hbm,
        out_shape=jax.ShapeDtypeStruct((N_SHARDS,) + tuple(piece_shape), src.dtype),
        scratch_shapes=[pltpu.SemaphoreType.DMA((N_SHARDS - 1,)), pltpu.SemaphoreType.DMA((N_SHARDS - 1,)),
                        pltpu.SemaphoreType.DMA],
    )(src)


def pair_exchange(src, *, add, name):
    lead, rows, cols = src.shape
    tr = _pick(rows, (512, 256))
    nblk = rows // tr
    n_steps = nblk if add else lead * nblk

    def body(c_ref, *refs):
        if add:
            mine_ref, send_ref, o_ref, recv_ref, send_sems, recv_sems, credit = refs
        else:
            send_ref, o_ref, recv_ref, send_sems, recv_sems, credit = refs
        step = pl.program_id(0) * nblk + pl.program_id(1)
        slot = step % 2
        sibling = (lax.axis_index("x"), lax.axis_index("y"), 1 - lax.axis_index("c"))

        @pl.when(step >= 2)
        def _():
            pl.semaphore_wait(credit, 1)

        cp = pltpu.make_async_remote_copy(
            src_ref=send_ref, dst_ref=recv_ref.at[slot], send_sem=send_sems.at[slot], recv_sem=recv_sems.at[slot],
            device_id=sibling, device_id_type=pl.DeviceIdType.MESH)
        cp.start()
        cp.wait_recv()
        if add:
            o_ref[...] = mine_ref[...] + recv_ref[slot]
        else:
            o_ref[c_ref[0]] = send_ref[...]
            o_ref[1 - c_ref[0]] = recv_ref[slot]
        cp.wait_send()

        @pl.when(step + 2 < n_steps)
        def _():
            pl.semaphore_signal(credit, 1, device_id=sibling, device_id_type=pl.DeviceIdType.MESH)

    flat = src.reshape(lead * rows, cols)
    if add:
        in_specs = [pl.BlockSpec((tr, cols), lambda s, i, c_ref: (c_ref[0] * nblk + i, 0)),
                    pl.BlockSpec((tr, cols), lambda s, i, c_ref: ((1 - c_ref[0]) * nblk + i, 0))]
        out_specs = pl.BlockSpec((tr, cols), lambda s, i, c_ref: (i, 0))
        out_shape = jax.ShapeDtypeStruct((rows, cols), src.dtype)
        grid, args = (1, nblk), (flat, flat)
    else:
        in_specs = [pl.BlockSpec((tr, cols), lambda s, i, c_ref: (s * nblk + i, 0))]
        out_specs = pl.BlockSpec((2, tr, cols), lambda s, i, c_ref: (s, i, 0))
        out_shape = jax.ShapeDtypeStruct((lead * 2, rows, cols), src.dtype)
        grid, args = (lead, nblk), (flat,)
    out = pl.pallas_call(
        body, name=name, out_shape=out_shape,
        grid_spec=pltpu.PrefetchScalarGridSpec(
            num_scalar_prefetch=1, grid=grid, in_specs=in_specs, out_specs=out_specs,
            scratch_shapes=[pltpu.VMEM((2, tr, cols), src.dtype), pltpu.SemaphoreType.DMA((2,)),
                            pltpu.SemaphoreType.DMA((2,)), pltpu.SemaphoreType.REGULAR]),
        compiler_params=_cparams(("arbitrary", "arbitrary")),
    )(lax.axis_index("c").astype(jnp.int32).reshape(1), *args)
    return out if add else out.reshape(lead, 2, rows, cols)


def sum_slots(buf, *, name):
    n, rows, cols = buf.shape
    tr = _pick(rows, (512, 256, 128))

    def body(b_ref, o_ref):
        acc = b_ref[0]
        for j in range(1, n):
            acc = acc + b_ref[j]
        o_ref[...] = acc

    return pl.pallas_call(
        body, name=name, grid=(rows // tr,), in_specs=[pl.BlockSpec((n, tr, cols), lambda i: (0, i, 0))],
        out_specs=pl.BlockSpec((tr, cols), lambda i: (i, 0)), out_shape=jax.ShapeDtypeStruct((rows, cols), F32),
        compiler_params=_cparams(("parallel",)),
    )(buf)


def adamw(w, g, m, v, *, name):
    shape = w.shape
    cols = shape[-1]
    rows = _size(shape) // cols
    w, g, m, v = (t.reshape(rows, cols) for t in (w, g, m, v))
    tr = 256 if rows % 256 == 0 else rows

    def body(w_ref, g_ref, m_ref, v_ref, d_ref, mo_ref, vo_ref):
        gv = g_ref[...]
        mn = ADAM_B1 * m_ref[...] + (1.0 - ADAM_B1) * gv
        vn = ADAM_B2 * v_ref[...] + (1.0 - ADAM_B2) * (gv * gv)
        m_hat = mn / (1.0 - ADAM_B1 ** ADAM_STEP)
        v_hat = vn / (1.0 - ADAM_B2 ** ADAM_STEP)
        d_ref[...] = -ADAM_LR * (m_hat / (jnp.sqrt(v_hat) + ADAM_EPS) + ADAM_WD * w_ref[...])
        mo_ref[...] = mn
        vo_ref[...] = vn

    blk = pl.BlockSpec((tr, cols), lambda i: (i, 0))
    shp = jax.ShapeDtypeStruct((rows, cols), F32)
    outs = pl.pallas_call(
        body, name=name, grid=(rows // tr,), in_specs=[blk] * 4, out_specs=[blk] * 3, out_shape=[shp] * 3,
        compiler_params=_cparams(("parallel",)),
    )(w, g, m, v)
    return [o.reshape(shape) for o in outs]


N_SHARDS = 4
FLAT_COLS = 1024
W_SPECS = (
    ("gdn_w_in", (2, 1024, 6176), 2), ("gdn_conv_w", (2, 4, 4096), 2), ("gdn_a_log", (2, 16), None),
    ("gdn_dt_bias", (2, 16), None), ("gdn_norm_w", (2, 128), None), ("gdn_w_out", (2, 2048, 1024), 1),
    ("sc_w_in", (1, 1024, 8192), 2), ("sc_conv_w", (1, 3, 2048), 2), ("sc_w_out", (1, 2048, 1024), 1),
    ("ssd_w_in", (1, 1024, 5152), 2), ("ssd_conv_w", (1, 4, 3072), 2), ("ssd_conv_b", (1, 3072), 1),
    ("ssd_a_log", (1, 32), None), ("ssd_dt_bias", (1, 32), None), ("ssd_d_skip", (1, 32), None),
    ("ssd_norm_w", (1, 2048), 1), ("ssd_w_out", (1, 2048, 1024), 1), ("ln_g", (4, 1024), None), ("ln_b", (4, 1024), None),
)


def _local_shape(shape, axis):
    return shape if axis is None else tuple(d // N_SHARDS if i == axis else d for i, d in enumerate(shape))


def _size(shape):
    n = 1
    for d in shape:
        n *= d
    return n


def _piece_rows(shape, axis):
    return -(-_size(_local_shape(shape, axis)) // (FLAT_COLS * SUBLANES)) * SUBLANES


FLAT_USED_ROWS = sum(_piece_rows(s, a) for _, s, a in W_SPECS)
FLAT_ROWS = -(-FLAT_USED_ROWS // 512) * 512
FLAT_HALF = FLAT_ROWS // 2


def _pack(pieces):
    blocks = []
    for p, (_, shape, axis) in zip(pieces, W_SPECS):
        rows = _piece_rows(shape, axis)
        flat = p.reshape(-1)
        if flat.shape[0] < rows * FLAT_COLS:
            flat = jnp.pad(flat, (0, rows * FLAT_COLS - flat.shape[0]))
        blocks.append(flat.reshape(rows, FLAT_COLS))
    blocks.append(jnp.zeros((FLAT_ROWS - FLAT_USED_ROWS, FLAT_COLS), F32))
    return jnp.concatenate(blocks, axis=0)


def _unpack(flat):
    out, off = [], 0
    for _, shape, axis in W_SPECS:
        ls = _local_shape(shape, axis)
        rows = _piece_rows(shape, axis)
        out.append(flat[off:off + rows].reshape(-1)[:_size(ls)].reshape(ls))
        off += rows
    return out


def _shard_of(full, axis, s):
    if axis is None:
        return full
    n = full.shape[axis] // N_SHARDS
    return lax.slice_in_dim(full, s * n, (s + 1) * n, axis=axis)


def _adamw_all(weights, grads_flat, moms, vels):
    grads = _unpack(grads_flat)
    steps = [adamw(w, g, m, v, name="adamw") for w, g, m, v in zip(weights, grads, moms, vels)]
    return grads, [s[0] for s in steps], [s[1] for s in steps], [s[2] for s in steps]


SPLIT_ROWS = 128


def shard_split(w, n_real, *, name):
    rows, n_pad = w.shape
    ns = n_real // N_SHARDS

    def body(w_ref, o_ref):
        for s in range(N_SHARDS):
            o_ref[s] = w_ref[:, s * ns:(s + 1) * ns]

    return pl.pallas_call(
        body, name=name, grid=(rows // SPLIT_ROWS,),
        in_specs=[pl.BlockSpec((SPLIT_ROWS, n_pad), lambda i: (i, 0))],
        out_specs=pl.BlockSpec((N_SHARDS, SPLIT_ROWS, ns), lambda i: (0, i, 0)),
        out_shape=jax.ShapeDtypeStruct((N_SHARDS, rows, ns), F32), compiler_params=_cparams(("parallel",)),
    )(w)


def shard_merge(pieces, n_pad, *, name):
    _, rows, ns = pieces.shape
    n_real = ns * N_SHARDS

    def body(p_ref, o_ref):
        for s in range(N_SHARDS):
            o_ref[:, s * ns:(s + 1) * ns] = p_ref[s].astype(o_ref.dtype)
        if n_pad > n_real:
            o_ref[:, n_real:] = jnp.zeros((SPLIT_ROWS, n_pad - n_real), o_ref.dtype)

    return pl.pallas_call(
        body, name=name, grid=(rows // SPLIT_ROWS,),
        in_specs=[pl.BlockSpec((N_SHARDS, SPLIT_ROWS, ns), lambda i: (0, i, 0))],
        out_specs=pl.BlockSpec((SPLIT_ROWS, n_pad), lambda i: (i, 0)),
        out_shape=jax.ShapeDtypeStruct((rows, n_pad), MXU_DTYPE), compiler_params=_cparams(("parallel",)),
    )(pieces)


def _reduce_scatter(full_grads):
    def shard(g, spec, s):
        _, shape, axis = spec
        return g[:, s] if g.ndim == len(shape) + 1 else _shard_of(g, axis, s)

    by_shard = jnp.stack([_pack([shard(g, spec, s) for g, spec in zip(full_grads, W_SPECS)])
                          for s in range(N_SHARDS)])
    by_half = by_shard.reshape(N_SHARDS, 2, FLAT_HALF, FLAT_COLS).transpose(1, 0, 2, 3)
    by_half = by_half.reshape(2, N_SHARDS * FLAT_HALF, FLAT_COLS)
    pair_sum = pair_exchange(by_half, add=True, name="rs_pair")
    chips = chip_exchange(pair_sum.reshape(N_SHARDS, FLAT_HALF, FLAT_COLS), scatter=True, name="rs_chips")
    half = sum_slots(chips, name="rs_chip_sum")
    return pair_exchange(half[None], add=False, name="rs_halves").reshape(FLAT_ROWS, FLAT_COLS)


def _gather_weights(local_weights):
    halves = chip_exchange(_pack(local_weights).reshape(2, FLAT_HALF, FLAT_COLS), scatter=False, name="gather_chips")
    gathered = pair_exchange(halves, add=False, name="gather_pair").reshape(N_SHARDS, FLAT_ROWS, FLAT_COLS)
    per_shard = [_unpack(gathered[s]) for s in range(N_SHARDS)]
    full = []
    for i, (wname, shape, axis) in enumerate(W_SPECS):
        if axis is None:
            full.append(local_weights[i])
        elif wname in W_IN_PAD:
            pieces = jnp.stack([per_shard[s][i] for s in range(N_SHARDS)], axis=1)
            full.append([shard_merge(pieces[j], W_IN_PAD[wname], name="merge_" + wname) for j in range(shape[0])])
        else:
            full.append(jnp.concatenate([per_shard[s][i] for s in range(N_SHARDS)], axis=axis))
    return full


W_IN_PAD = {"gdn_w_in": GDN_IN_PAD, "sc_w_in": SC_IN, "ssd_w_in": SSD_IN_PAD}


def kernel(x, gdn_w_in, gdn_conv_w, gdn_a_log, gdn_dt_bias, gdn_norm_w, gdn_w_out, sc_w_in, sc_conv_w, sc_w_out, ssd_w_in, ssd_conv_w, ssd_conv_b, ssd_a_log, ssd_dt_bias, ssd_d_skip, ssd_norm_w, ssd_w_out, ln_g, ln_b, loss_target, m_gdn_w_in, m_gdn_conv_w, m_gdn_a_log, m_gdn_dt_bias, m_gdn_norm_w, m_gdn_w_out, m_sc_w_in, m_sc_conv_w, m_sc_w_out, m_ssd_w_in, m_ssd_conv_w, m_ssd_conv_b, m_ssd_a_log, m_ssd_dt_bias, m_ssd_d_skip, m_ssd_norm_w, m_ssd_w_out, m_ln_g, m_ln_b, v_gdn_w_in, v_gdn_conv_w, v_gdn_a_log, v_gdn_dt_bias, v_gdn_norm_w, v_gdn_w_out, v_sc_w_in, v_sc_conv_w, v_sc_w_out, v_ssd_w_in, v_ssd_conv_w, v_ssd_conv_b, v_ssd_a_log, v_ssd_dt_bias, v_ssd_d_skip, v_ssd_norm_w, v_ssd_w_out, v_ln_g, v_ln_b):
    weights = [gdn_w_in, gdn_conv_w, gdn_a_log, gdn_dt_bias, gdn_norm_w, gdn_w_out, sc_w_in, sc_conv_w, sc_w_out,
               ssd_w_in, ssd_conv_w, ssd_conv_b, ssd_a_log, ssd_dt_bias, ssd_d_skip, ssd_norm_w, ssd_w_out, ln_g, ln_b]
    moms = [m_gdn_w_in, m_gdn_conv_w, m_gdn_a_log, m_gdn_dt_bias, m_gdn_norm_w, m_gdn_w_out, m_sc_w_in, m_sc_conv_w,
            m_sc_w_out, m_ssd_w_in, m_ssd_conv_w, m_ssd_conv_b, m_ssd_a_log, m_ssd_dt_bias, m_ssd_d_skip, m_ssd_norm_w,
            m_ssd_w_out, m_ln_g, m_ln_b]
    vels = [v_gdn_w_in, v_gdn_conv_w, v_gdn_a_log, v_gdn_dt_bias, v_gdn_norm_w, v_gdn_w_out, v_sc_w_in, v_sc_conv_w,
            v_sc_w_out, v_ssd_w_in, v_ssd_conv_w, v_ssd_conv_b, v_ssd_a_log, v_ssd_dt_bias, v_ssd_d_skip, v_ssd_norm_w,
            v_ssd_w_out, v_ln_g, v_ln_b]
    full = dict(zip([n for n, _, _ in W_SPECS], _gather_weights(weights)))
    x0 = x[0]
    target = loss_target[0]

    layers = (("gdn", 0, GDN_IN_PAD, GDN_IN), ("sc", 0, SC_IN, SC_IN), ("ssd", 0, SSD_IN_PAD, SSD_IN), ("gdn", 1, GDN_IN_PAD, GDN_IN))

    def params(kind, j):
        if kind == "gdn":
            return [full["gdn_conv_w"][j], full["gdn_a_log"][j], full["gdn_dt_bias"][j], full["gdn_norm_w"][j]]
        if kind == "sc":
            return [full["sc_conv_w"][j]]
        return [full["ssd_conv_w"][j], full["ssd_conv_b"][j], full["ssd_a_log"][j], full["ssd_dt_bias"][j],
                full["ssd_d_skip"][j], full["ssd_norm_w"][j]]

    xs, saved = [x0], []
    for i, (kind, j, n_pad, _) in enumerate(layers):
        w_in = full[kind + "_w_in"][j]
        w_out = full[kind + "_w_out"][j].astype(MXU_DTYPE)
        proj = matmul(xs[i], w_in, name=kind + "_proj")
        if kind == "gdn":
            mix, states = gdn_fwd(proj, *params(kind, j), name="gdn_fwd")
        elif kind == "sc":
            mix, states = sc_fwd(proj, *params(kind, j), name="sc_fwd"), None
        else:
            mix, states = ssd_fwd(proj, *params(kind, j), name="ssd_fwd")
        y = matmul(mix, w_out, name=kind + "_out")
        saved.append((w_in, w_out, proj, mix, states, y))
        if i + 1 < DEPTH:
            xs.append(ln_fwd(xs[i], y, full["ln_g"][i], full["ln_b"][i], name="ln_fwd"))

    grads = {n: [None] * s[0] for n, s, _ in W_SPECS}
    dr, dg, db, loss_rows = ln_bwd(xs[DEPTH - 1], saved[DEPTH - 1][5], full["ln_g"][DEPTH - 1], b=full["ln_b"][DEPTH - 1],
                                   target=target, name="ln_bwd_loss")
    dx = None
    for i in reversed(range(DEPTH)):
        kind, j, _, n_in = layers[i]
        w_in, w_out, proj, mix, states, _ = saved[i]
        grads["ln_g"][i], grads["ln_b"][i] = dg[0], db[0]
        dmix = matmul(dr, w_out, tb=True, name=kind + "_dmix")
        grads[kind + "_w_out"][j] = matmul(mix, dr, ta=True, name=kind + "_dw_out")
        if kind == "gdn":
            dproj, (dcw, dalog, ddtb, dnw) = gdn_bwd(proj, *params(kind, j), states, dmix, name="gdn_bwd")
            grads["gdn_conv_w"][j], grads["gdn_a_log"][j], grads["gdn_dt_bias"][j], grads["gdn_norm_w"][j] = dcw, dalog, ddtb, dnw
        elif kind == "sc":
            dproj, dcw = sc_bwd(proj, *params(kind, j), dmix, name="sc_bwd")
            grads["sc_conv_w"][j] = dcw[:SC_CONV]
        else:
            dproj, (dcw, dcb, dalog, ddtb, ddsk, dnw) = ssd_bwd(proj, *params(kind, j), states, dmix, name="ssd_bwd")
            grads["ssd_conv_w"][j], grads["ssd_conv_b"][j], grads["ssd_a_log"][j] = dcw, dcb, dalog
            grads["ssd_dt_bias"][j], grads["ssd_d_skip"][j], grads["ssd_norm_w"][j] = ddtb, ddsk, dnw
        grads[kind + "_w_in"][j] = shard_split(matmul(xs[i], dproj, ta=True, name=kind + "_dw_in"), n_in, name="split_" + kind)
        dx = matmul(dproj, w_in, tb=True, add=dr, add_scale=ALPHA, name=kind + "_dx")
        if i > 0:
            dr, dg, db = ln_bwd(xs[i - 1], saved[i - 1][5], full["ln_g"][i - 1], dx, name="ln_bwd")

    full_grads = [jnp.stack(grads[n]) for n, _, _ in W_SPECS]
    grads_flat = _reduce_scatter(full_grads)
    g_out, d_out, m_out, v_out = _adamw_all(weights, grads_flat, moms, vels)
    loss = lax.psum(loss_rows[0, 0], MESH_AXES)
    return (loss, dx[None], *g_out, *d_out, *m_out, *v_out)
```

```python
import functools

import jax
import jax.numpy as jnp
from jax import lax
from jax.experimental import pallas as pl
from jax.experimental.pallas import tpu as pltpu

F32 = jnp.float32
MXU_DTYPE = jnp.bfloat16

D_MODEL = 1024
DEPTH = 4
D_INNER = 2048
CHUNK = 64
LANES = 128
SUBLANES = 8
VMEM_LIMIT = 56 * 1024 * 1024

GDN_HEAD = 128
GDN_VH = 16
GDN_QKH = 8
GDN_QK = 1024
GDN_V = 2048
GDN_CONV = 4
GDN_CONV_DIM = 4096
GDN_IN = 6176
GDN_IN_PAD = 6272

SC_W = 2048
SC_CONV = 3
SC_IN = 8192

SSD_P = 64
SSD_H = 32
SSD_G = 4
SSD_S = 128
SSD_CONV = 4
SSD_CONV_DIM = 3072
SSD_IN = 5152
SSD_IN_PAD = 5376

ALPHA = (2 * DEPTH) ** 0.25
RMS_EPS = 1e-6
LN_EPS = 1e-5
L2_EPS = 1e-6

ADAM_LR = 0.001
ADAM_B1 = 0.9
ADAM_B2 = 0.999
ADAM_EPS = 1e-08
ADAM_WD = 0.01
ADAM_STEP = 10

MESH_AXES = ("x", "y", "c")


def _cparams(sem):
    return pltpu.CompilerParams(dimension_semantics=sem, vmem_limit_bytes=VMEM_LIMIT)


def _pick(n, prefs):
    for p in prefs:
        if n % p == 0:
            return p
    return n


def _dot(a, b, dims=(((1,), (0,)), ((), ()))):
    return lax.dot_general(a.astype(MXU_DTYPE), b.astype(MXU_DTYPE), dims, preferred_element_type=F32)


def _dot_nt(a, b):
    return _dot(a, b, (((1,), (1,)), ((), ())))


def _dot_tn(a, b):
    return _dot(a, b, (((0,), (0,)), ((), ())))


NN = (((1,), (0,)), ((), ()))
NT = (((1,), (1,)), ((), ()))
TN = (((0,), (0,)), ((), ()))


def _mxu(a, b, dims):
    return lax.dot_general(a, b, dims, preferred_element_type=F32)


def _split(x, pieces):
    out, r = [], x
    for i in range(pieces):
        p = r.astype(jnp.bfloat16)
        out.append(p)
        if i + 1 < pieces:
            r = r - p.astype(F32)
    return out


def _sel(m, x, dims=NN):
    mb = m.astype(jnp.bfloat16)
    x1, x2, x3 = _split(x, 3)
    return (_mxu(mb, x3, dims) + _mxu(mb, x2, dims)) + _mxu(mb, x1, dims)


def _sel_nt(m, x):
    return _sel(m, x, NT)


def _xsel(x, m, dims=NN):
    mb = m.astype(jnp.bfloat16)
    x1, x2, x3 = _split(x, 3)
    return (_mxu(x3, mb, dims) + _mxu(x2, mb, dims)) + _mxu(x1, mb, dims)


def _xsel_nt(x, m):
    return _xsel(x, m, NT)


def _iota(shape, dim):
    return lax.broadcasted_iota(jnp.int32, shape, dim)


def _sigmoid(x):
    return 0.5 * jnp.tanh(0.5 * x) + 0.5


def _silu(x):
    return x * _sigmoid(x)


def _dsilu(x):
    s = _sigmoid(x)
    return s * (1.0 + x * (1.0 - s))


def _softplus(x):
    return jnp.maximum(x, 0.0) + jnp.log(1.0 + jnp.exp(-jnp.abs(x)))


def matmul(a, b, *, ta=False, tb=False, add=None, add_scale=1.0, name):
    if ta:
        kdim, m = a.shape
    else:
        m, kdim = a.shape
    n = b.shape[0] if tb else b.shape[1]
    assert (b.shape[1] if tb else b.shape[0]) == kdim
    tm = _pick(m, (1024, 896, 768, 512)) if ta else _pick(m, (2048, 1024, 512, 256, 128))
    tn = _pick(n, (1024, 896, 768, 512, 256, 128))
    tk = _pick(kdim, (1024, 512, 256)) if ta else _pick(kdim, (1024, 896, 768, 512))
    nk = kdim // tk
    dims = (((0 if ta else 1,), (1 if tb else 0,)), ((), ()))

    def body(a_ref, b_ref, *rest):
        o_ref = rest[-1]
        k = pl.program_id(2)
        part = _dot(a_ref[...], b_ref[...], dims)

        @pl.when(k == 0)
        def _():
            o_ref[...] = part if add is None else part + add_scale * rest[0][...]

        @pl.when(k > 0)
        def _():
            o_ref[...] += part

    a_spec = pl.BlockSpec((tk, tm), lambda i, j, k: (k, i)) if ta else pl.BlockSpec((tm, tk), lambda i, j, k: (i, k))
    b_spec = pl.BlockSpec((tn, tk), lambda i, j, k: (j, k)) if tb else pl.BlockSpec((tk, tn), lambda i, j, k: (k, j))
    o_spec = pl.BlockSpec((tm, tn), lambda i, j, k: (i, j))
    in_specs = [a_spec, b_spec] + ([] if add is None else [o_spec])
    args = (a, b) + (() if add is None else (add,))
    return pl.pallas_call(
        body, name=name, grid=(m // tm, n // tn, nk), in_specs=in_specs, out_specs=o_spec,
        out_shape=jax.ShapeDtypeStruct((m, n), F32),
        compiler_params=_cparams(("parallel", "parallel", "arbitrary")),
    )(*args)


LN_ROWS = 512


def _ln_stats(x, y):
    r = ALPHA * x + y
    mu = jnp.mean(r, axis=-1, keepdims=True)
    rc = r - mu
    var = jnp.mean(rc * rc, axis=-1, keepdims=True)
    rstd = lax.rsqrt(var + LN_EPS)
    return rc * rstd, rstd


def ln_fwd(x, y, g, b, *, name):
    rows, d = x.shape

    def body(x_ref, y_ref, g_ref, b_ref, o_ref):
        xhat, _ = _ln_stats(x_ref[...], y_ref[...])
        o_ref[...] = xhat * g_ref[...] + b_ref[...]

    blk = pl.BlockSpec((LN_ROWS, d), lambda i: (i, 0))
    vec = pl.BlockSpec((1, d), lambda i: (0, 0))
    return pl.pallas_call(
        body, name=name, grid=(rows // LN_ROWS,), in_specs=[blk, blk, vec, vec], out_specs=blk,
        out_shape=jax.ShapeDtypeStruct((rows, d), F32), compiler_params=_cparams(("parallel",)),
    )(x, y, g.reshape(1, d), b.reshape(1, d))


def ln_bwd(x, y, g, dxn=None, *, b=None, target=None, name):
    rows, d = x.shape
    final = target is not None

    def body(x_ref, y_ref, g_ref, *rest):
        if final:
            b_ref, t_ref, dr_ref, dg_ref, db_ref, loss_ref = rest
        else:
            dxn_ref, dr_ref, dg_ref, db_ref = rest
        i = pl.program_id(0)
        xhat, rstd = _ln_stats(x_ref[...], y_ref[...])
        gv = g_ref[...]
        if final:
            err = xhat * gv + b_ref[...] - t_ref[...]
            dxn_v = err * (1.0 / d)
            part = 0.5 * jnp.sum(jnp.mean(err * err, axis=-1, keepdims=True), axis=0, keepdims=True)
        else:
            dxn_v = dxn_ref[...]
        dxh = dxn_v * gv
        m1 = jnp.mean(dxh, axis=-1, keepdims=True)
        m2 = jnp.mean(dxh * xhat, axis=-1, keepdims=True)
        dr_ref[...] = rstd * (dxh - m1 - xhat * m2)

        @pl.when(i == 0)
        def _():
            dg_ref[...] = jnp.zeros_like(dg_ref)
            db_ref[...] = jnp.zeros_like(db_ref)
            if final:
                loss_ref[...] = jnp.zeros_like(loss_ref)

        dg_ref[...] += jnp.sum(dxn_v * xhat, axis=0, keepdims=True)
        db_ref[...] += jnp.sum(dxn_v, axis=0, keepdims=True)
        if final:
            loss_ref[...] += jnp.broadcast_to(part, loss_ref.shape)

    blk = pl.BlockSpec((LN_ROWS, d), lambda i: (i, 0))
    vec = pl.BlockSpec((1, d), lambda i: (0, 0))
    lvec = pl.BlockSpec((1, LANES), lambda i: (0, 0))
    out_shape = [jax.ShapeDtypeStruct((rows, d), F32), jax.ShapeDtypeStruct((1, d), F32), jax.ShapeDtypeStruct((1, d), F32)]
    out_specs = [blk, vec, vec]
    if final:
        in_specs = [blk, blk, vec, vec, blk]
        args = (x, y, g.reshape(1, d), b.reshape(1, d), target)
        out_shape.append(jax.ShapeDtypeStruct((1, LANES), F32))
        out_specs.append(lvec)
    else:
        in_specs = [blk, blk, vec, blk]
        args = (x, y, g.reshape(1, d), dxn)
    return pl.pallas_call(
        body, name=name, grid=(rows // LN_ROWS,), in_specs=in_specs, out_specs=out_specs, out_shape=out_shape,
        compiler_params=_cparams(("arbitrary",)),
    )(*args)


def _rows_from(ref, off, rows, cols=slice(None)):
    r = off % SUBLANES
    if r == 0:
        return ref[off:off + rows, cols]
    window = ref[off - r:off - r + rows + SUBLANES, cols]
    return pltpu.roll(window, rows + SUBLANES - r, axis=0)[:rows]


def _conv_from_ext(ext_ref, w, width, rows, cols=slice(None)):
    out = None
    for j in range(width):
        term = _rows_from(ext_ref, SUBLANES - (width - 1) + j, rows, cols) * w[j:j + 1, cols]
        out = term if out is None else out + term
    return out


def _conv_dgrad_from_ext(dext_ref, w, width, rows, cols):
    out = None
    for j in range(width):
        term = _rows_from(dext_ref, (width - 1) - j, rows, cols) * w[j:j + 1, cols]
        out = term if out is None else out + term
    return out


def _conv_bwd_from_ext(dext_ref, ext_ref, w, width, rows, cols):
    u = ext_ref[SUBLANES:, cols]
    du, dws = None, []
    for j in range(width):
        shifted = _rows_from(dext_ref, (width - 1) - j, rows, cols)
        term = shifted * w[j:j + 1, cols]
        du = term if du is None else du + term
        dws.append(jnp.sum(shifted * u, axis=0, keepdims=True))
    return du, dws


CONV_COLS = 256


SC_ROWS = 128


def sc_fwd(proj, conv_w, *, name):
    rows = proj.shape[0]
    nb = rows // SC_ROWS
    hb = SC_ROWS // SUBLANES

    def body(p_ref, halo_ref, w_ref, o_ref, ext_ref):
        i = pl.program_id(0)
        w = w_ref[...]
        for c0 in range(0, SC_W, CONV_COLS):
            cols, bc, cc, zc = (slice(k * SC_W + c0, k * SC_W + c0 + CONV_COLS) for k in range(4))
            ext_ref[0:SUBLANES, cols] = jnp.where(i == 0, 0.0, halo_ref[:, cc] * halo_ref[:, cols])
            ext_ref[SUBLANES:, cols] = p_ref[:, cc] * p_ref[:, cols]
            cv = _conv_from_ext(ext_ref, w, SC_CONV, SC_ROWS, cols)
            o_ref[:, cols] = (p_ref[:, bc] * cv * _silu(p_ref[:, zc])).astype(o_ref.dtype)

    return pl.pallas_call(
        body, name=name, grid=(nb,),
        in_specs=[pl.BlockSpec((SC_ROWS, SC_IN), lambda i: (i, 0)),
                  pl.BlockSpec((SUBLANES, SC_IN), lambda i: (jnp.maximum(i * hb - 1, 0), 0)),
                  pl.BlockSpec((SUBLANES, SC_W), lambda i: (0, 0))],
        out_specs=pl.BlockSpec((SC_ROWS, SC_W), lambda i: (i, 0)),
        out_shape=jax.ShapeDtypeStruct((rows, SC_W), MXU_DTYPE),
        scratch_shapes=[pltpu.VMEM((SC_ROWS + SUBLANES, SC_W), F32)],
        compiler_params=_cparams(("parallel",)),
    )(proj, proj, _pad_rows(conv_w))


def sc_bwd(proj, conv_w, dmix, *, name):
    rows = proj.shape[0]
    nb = rows // SC_ROWS
    hb = SC_ROWS // SUBLANES

    def body(p_ref, halo_ref, w_ref, dm_ref, dp_ref, dw_ref, ext_ref, dext_ref):
        i = pl.program_id(0)
        blk = nb - 1 - i
        w = w_ref[...]

        @pl.when(i == 0)
        def _():
            dext_ref[SC_ROWS:, :] = jnp.zeros((SUBLANES, SC_W), F32)
            dw_ref[...] = jnp.zeros_like(dw_ref)

        for c0 in range(0, SC_W, CONV_COLS):
            cols, bc, cc, zc = (slice(k * SC_W + c0, k * SC_W + c0 + CONV_COLS) for k in range(4))
            h, bg, cg, z = p_ref[:, cols], p_ref[:, bc], p_ref[:, cc], p_ref[:, zc]
            ext_ref[0:SUBLANES, cols] = jnp.where(blk == 0, 0.0, halo_ref[:, cc] * halo_ref[:, cols])
            ext_ref[SUBLANES:, cols] = cg * h
            taps = [_rows_from(ext_ref, SUBLANES - (SC_CONV - 1) + j, SC_ROWS, cols) for j in range(SC_CONV)]
            cv = None
            for j in range(SC_CONV):
                term = taps[j] * w[j:j + 1, cols]
                cv = term if cv is None else cv + term
            dm = dm_ref[:, cols]
            dy = dm * _silu(z)
            dp_ref[:, zc] = (dm * bg * cv * _dsilu(z)).astype(dp_ref.dtype)
            dp_ref[:, bc] = (dy * cv).astype(dp_ref.dtype)
            dcv = dy * bg
            dext_ref[0:SC_ROWS, cols] = dcv
            du = _conv_dgrad_from_ext(dext_ref, w, SC_CONV, SC_ROWS, cols)
            dp_ref[:, cols] = (du * cg).astype(dp_ref.dtype)
            dp_ref[:, cc] = (du * h).astype(dp_ref.dtype)
            for j in range(SC_CONV):
                dw_ref[j:j + 1, cols] += jnp.sum(taps[j] * dcv, axis=0, keepdims=True)
            dext_ref[SC_ROWS:, cols] = dcv[0:SUBLANES, :]

    return pl.pallas_call(
        body, name=name, grid=(nb,),
        in_specs=[pl.BlockSpec((SC_ROWS, SC_IN), lambda i: (nb - 1 - i, 0)),
                  pl.BlockSpec((SUBLANES, SC_IN), lambda i: (jnp.maximum((nb - 1 - i) * hb - 1, 0), 0)),
                  pl.BlockSpec((SUBLANES, SC_W), lambda i: (0, 0)),
                  pl.BlockSpec((SC_ROWS, SC_W), lambda i: (nb - 1 - i, 0))],
        out_specs=[pl.BlockSpec((SC_ROWS, SC_IN), lambda i: (nb - 1 - i, 0)),
                   pl.BlockSpec((SUBLANES, SC_W), lambda i: (0, 0))],
        out_shape=[jax.ShapeDtypeStruct((rows, SC_IN), MXU_DTYPE), jax.ShapeDtypeStruct((SUBLANES, SC_W), F32)],
        scratch_shapes=[pltpu.VMEM((SC_ROWS + SUBLANES, SC_W), F32), pltpu.VMEM((SC_ROWS + SUBLANES, SC_W), F32)],
        compiler_params=_cparams(("arbitrary",)),
    )(proj, proj, _pad_rows(conv_w), dmix)


def _pad_rows(w, rows=SUBLANES):
    return jnp.pad(w, ((0, rows - w.shape[0]), (0, 0)))


def _pad_lanes(v, lanes=LANES):
    v = v.reshape(1, -1)
    return jnp.pad(v, ((0, 0), (0, lanes - v.shape[1])))


def _tri(n, lower=True):
    r, c = _iota((n, n), 0), _iota((n, n), 1)
    return jnp.where((c <= r) if lower else (c >= r), 1.0, 0.0)


def _eye(n):
    return jnp.where(_iota((n, n), 0) == _iota((n, n), 1), 1.0, 0.0)


def _head_expand(n, width):
    return jnp.where(_iota((LANES, n), 1) // width == _iota((LANES, n), 0), 1.0, 0.0)


def _col(v, h):
    return jnp.sum(jnp.where(_iota(v.shape, 1) == h, v, 0.0), axis=1, keepdims=True)


def _row(v, r):
    return jnp.sum(jnp.where(_iota(v.shape, 0) == r, v, 0.0), axis=0, keepdims=True)


def _expand_row(v, e):
    return jnp.max(_xsel(jnp.broadcast_to(v, (SUBLANES, LANES)), e), axis=0, keepdims=True)


SSD_ROWS = 128
SSD_X0 = D_INNER
SSD_DT0 = D_INNER + SSD_CONV_DIM
SSD_B0 = D_INNER
SSD_C0 = D_INNER + SSD_G * SSD_S
SSD_GW = D_INNER // SSD_G
SSD_HG = SSD_H // SSD_G


def _ssd_prologue(blk, p_ref, halo_ref, cw_ref, cb_ref, dtb_ref, ext_ref, xbc_ref, dt_ref, pre_ref=None):
    ext_ref[0:SUBLANES, :] = jnp.where(blk == 0, 0.0, halo_ref[:, SSD_X0:SSD_DT0])
    ext_ref[SUBLANES:, :] = p_ref[:, SSD_X0:SSD_DT0]
    w = cw_ref[...]
    for c0 in range(0, SSD_CONV_DIM, CONV_COLS):
        cols = slice(c0, c0 + CONV_COLS)
        pre = _conv_from_ext(ext_ref, w, SSD_CONV, SSD_ROWS, cols) + cb_ref[:, cols]
        if pre_ref is not None:
            pre_ref[:, cols] = pre
        xbc_ref[:, cols] = _silu(pre)
    dt_ref[...] = _softplus(p_ref[:, SSD_DT0:SSD_DT0 + LANES] + dtb_ref[...])


def _ssd_chunk_decays(dt_c, a_row, ltri, eye_l, act_ref):
    da = dt_c * a_row
    ac = _sel(ltri,da)
    act_ref[...] = _sel_nt(eye_l,ac)
    ac_last = _row(ac, CHUNK - 1)
    return ac, jnp.exp(ac_last - ac), jnp.exp(ac), jnp.exp(ac_last)


def _ssd_seg(ac, act_ref, h, causal):
    return jnp.where(causal, jnp.exp(jnp.minimum(_col(ac, h) - act_ref[pl.ds(h, 1), :], 0.0)), 0.0)


def _ssd_half(pair, e):
    upper = _iota(pair.shape, 1) >= SSD_P
    return jnp.where(upper if e % 2 else jnp.logical_not(upper), pair, 0.0)


def _ssd_group_fwd(g, xbc_ref, rows, dt_exp, tail_exp, cdec_exp, ac, act_ref, s_g, causal):
    gl = slice(g * SSD_GW, (g + 1) * SSD_GW)
    bg = xbc_ref[rows, SSD_B0 + g * SSD_S:SSD_B0 + (g + 1) * SSD_S]
    cg = xbc_ref[rows, SSD_C0 + g * SSD_S:SSD_C0 + (g + 1) * SSD_S]
    xdt = xbc_ref[rows, gl] * dt_exp[:, gl]
    cb = _dot_nt(cg, bg)
    cs = _dot(cg, s_g)
    segs = [_ssd_seg(ac, act_ref, g * SSD_HG + e, causal) for e in range(SSD_HG)]
    gms = [seg * cb for seg in segs]
    xps = [xdt[:, p * LANES:(p + 1) * LANES] for p in range(SSD_HG // 2)]
    parts = [_dot(gms[e], _ssd_half(xps[e // 2], e)) for e in range(SSD_HG)]
    yd = jnp.concatenate([parts[2 * p] + parts[2 * p + 1] for p in range(SSD_HG // 2)], axis=1)
    st = _dot_tn(bg, xdt * tail_exp[:, gl])
    return yd + cs * cdec_exp[:, gl], st, bg, cg, cb, xdt, cs, segs, gms


def ssd_fwd(proj, conv_w, conv_b, a_log, dt_bias, d_skip, norm_w, *, name):
    rows_total = proj.shape[0]
    nb = rows_total // SSD_ROWS
    hb = SSD_ROWS // SUBLANES
    cpb = SSD_ROWS // CHUNK

    def body(p_ref, halo_ref, cw_ref, cb_ref, alog_ref, dtb_ref, dsk_ref, nw_ref, mix_ref, st_ref,
             ext_ref, xbc_ref, dt_ref, s_ref, act_ref):
        i = pl.program_id(0)

        @pl.when(i == 0)
        def _():
            s_ref[...] = jnp.zeros_like(s_ref)

        _ssd_prologue(i, p_ref, halo_ref, cw_ref, cb_ref, dtb_ref, ext_ref, xbc_ref, dt_ref)
        a_row = -jnp.exp(alog_ref[...])
        expand = _head_expand(D_INNER, SSD_P)
        dsk_exp = _expand_row(dsk_ref[...], expand)
        ltri, eye_l = _tri(CHUNK), _eye(LANES)
        causal = _iota((CHUNK, CHUNK), 1) <= _iota((CHUNK, CHUNK), 0)

        def chunk(c, carry):
            rows = pl.ds(pl.multiple_of(c * CHUNK, CHUNK), CHUNK)
            dt_c = dt_ref[rows, :]
            ac, tail, cdec, tot = _ssd_chunk_decays(dt_c, a_row, ltri, eye_l, act_ref)
            dt_exp = _xsel(dt_c, expand)
            tail_exp = _xsel(tail, expand)
            cdec_exp = _xsel(cdec, expand)
            tot_exp = _expand_row(tot, expand)
            for g in range(SSD_G):
                gl = slice(g * SSD_GW, (g + 1) * SSD_GW)
                s_g = s_ref[g]
                st_ref[c, g] = s_g
                y, st = _ssd_group_fwd(g, xbc_ref, rows, dt_exp, tail_exp, cdec_exp, ac, act_ref, s_g, causal)[:2]
                s_ref[g] = s_g * tot_exp[:, gl] + st
                y = (y + dsk_exp[:, gl] * xbc_ref[rows, gl]) * _silu(p_ref[rows, gl])
                r = lax.rsqrt(jnp.mean(y * y, axis=-1, keepdims=True) + RMS_EPS)
                mix_ref[rows, gl] = (y * r * nw_ref[:, gl]).astype(mix_ref.dtype)
            return carry

        lax.fori_loop(0, cpb, chunk, 0)

    vec = lambda n: pl.BlockSpec((1, n), lambda i: (0, 0))
    return pl.pallas_call(
        body, name=name, grid=(nb,),
        in_specs=[pl.BlockSpec((SSD_ROWS, SSD_IN_PAD), lambda i: (i, 0)),
                  pl.BlockSpec((SUBLANES, SSD_IN_PAD), lambda i: (jnp.maximum(i * hb - 1, 0), 0)),
                  pl.BlockSpec((SUBLANES, SSD_CONV_DIM), lambda i: (0, 0)),
                  vec(SSD_CONV_DIM), vec(LANES), vec(LANES), vec(LANES), vec(D_INNER)],
        out_specs=[pl.BlockSpec((SSD_ROWS, D_INNER), lambda i: (i, 0)),
                   pl.BlockSpec((cpb, SSD_G, SSD_S, SSD_GW), lambda i: (i, 0, 0, 0))],
        out_shape=[jax.ShapeDtypeStruct((rows_total, D_INNER), MXU_DTYPE),
                   jax.ShapeDtypeStruct((rows_total // CHUNK, SSD_G, SSD_S, SSD_GW), F32)],
        scratch_shapes=[pltpu.VMEM((SSD_ROWS + SUBLANES, SSD_CONV_DIM), F32),
                        pltpu.VMEM((SSD_ROWS, SSD_CONV_DIM), F32),
                        pltpu.VMEM((SSD_ROWS, LANES), F32),
                        pltpu.VMEM((SSD_G, SSD_S, SSD_GW), F32),
                        pltpu.VMEM((LANES, CHUNK), F32)],
        compiler_params=_cparams(("arbitrary",)),
    )(proj, proj, _pad_rows(conv_w), conv_b.reshape(1, -1), _pad_lanes(a_log), _pad_lanes(dt_bias),
      _pad_lanes(d_skip), norm_w.reshape(1, -1))


def ssd_bwd(proj, conv_w, conv_b, a_log, dt_bias, d_skip, norm_w, states, dmix, *, name):
    rows_total = proj.shape[0]
    nb = rows_total // SSD_ROWS
    hb = SSD_ROWS // SUBLANES
    cpb = SSD_ROWS // CHUNK

    def body(p_ref, halo_ref, cw_ref, cb_ref, alog_ref, dtb_ref, dsk_ref, nw_ref, st_ref, dm_ref,
             dp_ref, dcw_ref, dcb_ref, dalog_ref, ddtb_ref, ddsk_ref, dnw_ref,
             ext_ref, xbc_ref, dt_ref, ds_ref, act_ref, dext_ref, dac_ref, dact_ref, ddskw_ref, pre_ref):
        i = pl.program_id(0)
        blk = nb - 1 - i

        @pl.when(i == 0)
        def _():
            ds_ref[...] = jnp.zeros_like(ds_ref)
            dext_ref[SSD_ROWS:, :] = jnp.zeros((SUBLANES, SSD_CONV_DIM), F32)
            ddskw_ref[...] = jnp.zeros_like(ddskw_ref)
            for r in (dcw_ref, dcb_ref, dalog_ref, ddtb_ref, ddsk_ref, dnw_ref):
                r[...] = jnp.zeros_like(r)

        _ssd_prologue(blk, p_ref, halo_ref, cw_ref, cb_ref, dtb_ref, ext_ref, xbc_ref, dt_ref, pre_ref)
        a_row = -jnp.exp(alog_ref[...])
        expand = _head_expand(D_INNER, SSD_P)
        dsk_exp = _expand_row(dsk_ref[...], expand)
        ltri, utri, eye_l, eye_c = _tri(CHUNK), _tri(CHUNK, lower=False), _eye(LANES), _eye(CHUNK)
        causal = _iota((CHUNK, CHUNK), 1) <= _iota((CHUNK, CHUNK), 0)
        dp_ref[:, SSD_DT0 + LANES:] = jnp.zeros((SSD_ROWS, SSD_IN_PAD - SSD_DT0 - LANES), dp_ref.dtype)

        def chunk(cc, carry):
            c = cpb - 1 - cc
            rows = pl.ds(pl.multiple_of(c * CHUNK, CHUNK), CHUNK)
            dt_c = dt_ref[rows, :]
            ac, tail, cdec, tot = _ssd_chunk_decays(dt_c, a_row, ltri, eye_l, act_ref)
            dt_exp = _xsel(dt_c, expand)
            tail_exp = _xsel(tail, expand)
            cdec_exp = _xsel(cdec, expand)
            tot_exp = _expand_row(tot, expand)
            dac_ref[...] = jnp.zeros_like(dac_ref)
            dact_ref[...] = jnp.zeros_like(dact_ref)
            d_cdec = jnp.zeros((CHUNK, LANES), F32)
            d_tail = jnp.zeros((CHUNK, LANES), F32)
            d_dt = jnp.zeros((CHUNK, LANES), F32)
            d_tot = jnp.zeros((1, LANES), F32)
            for g in range(SSD_G):
                gl = slice(g * SSD_GW, (g + 1) * SSD_GW)
                ex_g = expand[:, gl]
                s_g = st_ref[c, g]
                y, _, bg, cg, cb, xdt, cs, segs, gms = _ssd_group_fwd(g, xbc_ref, rows, dt_exp, tail_exp, cdec_exp, ac, act_ref, s_g, causal)
                xs = xbc_ref[rows, gl]
                z = p_ref[rows, gl]
                sz = _silu(z)
                y2 = y + dsk_exp[:, gl] * xs
                yg = y2 * sz
                r = lax.rsqrt(jnp.mean(yg * yg, axis=-1, keepdims=True) + RMS_EPS)
                yn = yg * r
                dm = dm_ref[rows, gl]
                dnw_ref[:, gl] += jnp.sum(dm * yn, axis=0, keepdims=True)
                dyn = dm * nw_ref[:, gl]
                dyg = r * (dyn - yn * jnp.mean(dyn * yn, axis=-1, keepdims=True))
                dp_ref[rows, gl] = (dyg * y2 * _dsilu(z)).astype(dp_ref.dtype)
                dy = dyg * sz
                ddskw_ref[:, gl] += jnp.sum(dy * xs, axis=0, keepdims=True)
                ds_g = ds_ref[g]
                dyc = dy * cdec_exp[:, gl]
                ds_ref[g] = ds_g * tot_exp[:, gl] + _dot_tn(cg, dyc)
                sds = jnp.broadcast_to(jnp.sum(s_g * ds_g, axis=0, keepdims=True), (SUBLANES, SSD_GW))
                d_tot = d_tot + jnp.max(_xsel_nt(sds, ex_g), axis=0, keepdims=True)
                dcg = _dot_nt(dyc, s_g)
                d_cdec = d_cdec + _xsel_nt(dy * cs, ex_g)
                xdtd = xdt * tail_exp[:, gl]
                d_xdtd = _dot(bg, ds_g)
                dbg = _dot_nt(xdtd, ds_g)
                d_tail = d_tail + _xsel_nt(d_xdtd * xdt, ex_g)
                heads = range(SSD_HG)
                dy_h = [_ssd_half(dy[:, (e // 2) * LANES:(e // 2 + 1) * LANES], e) for e in heads]
                back = [_dot_tn(gms[e], dy_h[e]) for e in heads]
                dg_m = [jnp.where(causal, _dot_nt(dy_h[e], xdt[:, (e // 2) * LANES:(e // 2 + 1) * LANES]), 0.0) for e in heads]
                d_cb = None
                for e in heads:
                    h = g * SSD_HG + e
                    term = dg_m[e] * segs[e]
                    d_cb = term if d_cb is None else d_cb + term
                    em = dg_m[e] * gms[e]
                    dac_ref[...] += jnp.where(_iota((CHUNK, LANES), 1) == h, jnp.sum(em, axis=1, keepdims=True), 0.0)
                    dact_ref[h:h + 1, :] = jnp.sum(em, axis=0, keepdims=True)
                dcg = dcg + _dot(d_cb, bg)
                dbg = dbg + _dot_tn(d_cb, cg)
                d_xdt = d_xdtd * tail_exp[:, gl] + jnp.concatenate(
                    [back[2 * p] + back[2 * p + 1] for p in range(SSD_HG // 2)], axis=1)
                d_dt = d_dt + _xsel_nt(d_xdt * xs, ex_g)
                dext_ref[rows, gl] = d_xdt * dt_exp[:, gl] + dy * dsk_exp[:, gl]
                dext_ref[rows, SSD_B0 + g * SSD_S:SSD_B0 + (g + 1) * SSD_S] = dbg
                dext_ref[rows, SSD_C0 + g * SSD_S:SSD_C0 + (g + 1) * SSD_S] = dcg
            d_ac = dac_ref[...] - _sel_nt(eye_c,dact_ref[...]) + d_cdec * cdec - d_tail * tail
            d_last = jnp.sum(d_tail * tail, axis=0, keepdims=True) + d_tot * tot
            d_ac = jnp.where(_iota((CHUNK, LANES), 0) == CHUNK - 1, d_ac + d_last, d_ac)
            d_da = _sel(utri,d_ac)
            d_dt = d_dt + d_da * a_row
            dalog_ref[...] += jnp.sum(d_da * dt_c, axis=0, keepdims=True) * a_row
            d_raw = d_dt * _sigmoid(p_ref[rows, SSD_DT0:SSD_DT0 + LANES] + dtb_ref[...])
            d_raw = jnp.where(_iota((CHUNK, LANES), 1) < SSD_H, d_raw, 0.0)
            ddtb_ref[...] += jnp.sum(d_raw, axis=0, keepdims=True)
            dp_ref[rows, SSD_DT0:SSD_DT0 + LANES] = d_raw.astype(dp_ref.dtype)
            return carry

        lax.fori_loop(0, cpb, chunk, 0)
        w = cw_ref[...]
        for c0 in range(0, SSD_CONV_DIM, CONV_COLS):
            cols = slice(c0, c0 + CONV_COLS)
            d_pre = dext_ref[0:SSD_ROWS, cols] * _dsilu(pre_ref[:, cols])
            dext_ref[0:SSD_ROWS, cols] = d_pre
            dcb_ref[:, cols] += jnp.sum(d_pre, axis=0, keepdims=True)
            du, dws = _conv_bwd_from_ext(dext_ref, ext_ref, w, SSD_CONV, SSD_ROWS, cols)
            for j in range(SSD_CONV):
                dcw_ref[j:j + 1, cols] += dws[j]
            dp_ref[:, SSD_X0 + c0:SSD_X0 + c0 + CONV_COLS] = du.astype(dp_ref.dtype)
            dext_ref[SSD_ROWS:, cols] = d_pre[0:SUBLANES, :]

        @pl.when(i == nb - 1)
        def _():
            ddsk_ref[...] = jnp.max(_xsel_nt(jnp.broadcast_to(ddskw_ref[...], (SUBLANES, D_INNER)), expand), axis=0, keepdims=True)

    vec = lambda n: pl.BlockSpec((1, n), lambda i: (0, 0))
    outs = pl.pallas_call(
        body, name=name, grid=(nb,),
        in_specs=[pl.BlockSpec((SSD_ROWS, SSD_IN_PAD), lambda i: (nb - 1 - i, 0)),
                  pl.BlockSpec((SUBLANES, SSD_IN_PAD), lambda i: (jnp.maximum((nb - 1 - i) * hb - 1, 0), 0)),
                  pl.BlockSpec((SUBLANES, SSD_CONV_DIM), lambda i: (0, 0)),
                  vec(SSD_CONV_DIM), vec(LANES), vec(LANES), vec(LANES), vec(D_INNER),
                  pl.BlockSpec((cpb, SSD_G, SSD_S, SSD_GW), lambda i: (nb - 1 - i, 0, 0, 0)),
                  pl.BlockSpec((SSD_ROWS, D_INNER), lambda i: (nb - 1 - i, 0))],
        out_specs=[pl.BlockSpec((SSD_ROWS, SSD_IN_PAD), lambda i: (nb - 1 - i, 0)),
                   pl.BlockSpec((SUBLANES, SSD_CONV_DIM), lambda i: (0, 0)),
                   vec(SSD_CONV_DIM), vec(LANES), vec(LANES), vec(LANES), vec(D_INNER)],
        out_shape=[jax.ShapeDtypeStruct((rows_total, SSD_IN_PAD), MXU_DTYPE),
                   jax.ShapeDtypeStruct((SUBLANES, SSD_CONV_DIM), F32),
                   jax.ShapeDtypeStruct((1, SSD_CONV_DIM), F32), jax.ShapeDtypeStruct((1, LANES), F32),
                   jax.ShapeDtypeStruct((1, LANES), F32), jax.ShapeDtypeStruct((1, LANES), F32),
                   jax.ShapeDtypeStruct((1, D_INNER), F32)],
        scratch_shapes=[pltpu.VMEM((SSD_ROWS + SUBLANES, SSD_CONV_DIM), F32),
                        pltpu.VMEM((SSD_ROWS, SSD_CONV_DIM), F32),
                        pltpu.VMEM((SSD_ROWS, LANES), F32),
                        pltpu.VMEM((SSD_G, SSD_S, SSD_GW), F32),
                        pltpu.VMEM((LANES, CHUNK), F32),
                        pltpu.VMEM((SSD_ROWS + SUBLANES, SSD_CONV_DIM), F32),
                        pltpu.VMEM((CHUNK, LANES), F32),
                        pltpu.VMEM((LANES, CHUNK), F32),
                        pltpu.VMEM((1, D_INNER), F32),
                        pltpu.VMEM((SSD_ROWS, SSD_CONV_DIM), F32)],
        compiler_params=_cparams(("arbitrary",)),
    )(proj, proj, _pad_rows(conv_w), conv_b.reshape(1, -1), _pad_lanes(a_log), _pad_lanes(dt_bias),
      _pad_lanes(d_skip), norm_w.reshape(1, -1), states, dmix)
    dproj, dcw, dcb, dalog, ddtb, ddsk, dnw = outs
    return dproj, [dcw[:SSD_CONV], dcb[0], dalog[0, :SSD_H], ddtb[0, :SSD_H], ddsk[0, :SSD_H], dnw[0]]


GDN_ROWS = 128
GDN_K0 = GDN_QK
GDN_V0 = 2 * GDN_QK
GDN_Z0 = GDN_CONV_DIM
GDN_BA0 = GDN_CONV_DIM + GDN_V
GDN_GL = GDN_VH
GDN_SCALE = GDN_HEAD ** -0.5
GDN_GROUP = 8


def _gdn_lane_params(v):
    return jnp.pad(v.reshape(1, GDN_VH), ((0, 0), (GDN_GL, LANES - GDN_GL - GDN_VH)))


def _gdn_prologue(blk, p_ref, halo_ref, cw_ref, alog_ref, dtb_ref, ext_ref, qkv_ref, beta_ref, g_ref, pre_ref=None):
    ext_ref[0:SUBLANES, :] = jnp.where(blk == 0, 0.0, halo_ref[:, 0:GDN_CONV_DIM])
    ext_ref[SUBLANES:, :] = p_ref[:, 0:GDN_CONV_DIM]
    w = cw_ref[...]
    for hh in range(GDN_CONV_DIM // GDN_HEAD):
        cols = slice(hh * GDN_HEAD, (hh + 1) * GDN_HEAD)
        pre = _conv_from_ext(ext_ref, w, GDN_CONV, GDN_ROWS, cols)
        if pre_ref is not None:
            pre_ref[:, cols] = pre
        a = _silu(pre)
        if hh < 2 * GDN_QKH:
            r = lax.rsqrt(jnp.sum(a * a, axis=-1, keepdims=True) + L2_EPS)
            a = a * (r * (GDN_SCALE if hh < GDN_QKH else 1.0))
        qkv_ref[:, cols] = a
    ba = p_ref[:, GDN_BA0:GDN_BA0 + LANES]
    beta_ref[...] = _sigmoid(ba)
    g_ref[...] = -jnp.exp(alog_ref[...]) * _softplus(ba + dtb_ref[...])


def _each(f, *lists):
    return [f(*z) for z in zip(*lists)]


def _inv_unit_lower_each(a_list, eye_c):
    xs = [eye_c - a for a in a_list]
    ps = a_list
    n = 2
    while n < CHUNK:
        ps = [_dot(p, p) for p in ps]
        xs = [x + _dot(x, p) for x, p in zip(xs, ps)]
        n *= 2
    return xs


def _gdn_heads_fwd(q, k, v, kk, qk, gcol, grow, glast, bcol, s, causal, strict, eye_c, t=None):
    decay = _each(lambda gc_, gr_: jnp.where(causal, jnp.exp(jnp.minimum(gc_ - gr_, 0.0)), 0.0), gcol, grow)
    egc = _each(jnp.exp, gcol)
    etail = _each(lambda gl_, gc_: jnp.exp(gl_ - gc_), glast, gcol)
    cd = _each(jnp.exp, glast)
    a = _each(lambda b_, kk_, d_: jnp.where(strict, b_ * kk_ * d_, 0.0), bcol, kk, decay)
    if t is None:
        t = _inv_unit_lower_each(a, eye_c)
    kb = _each(lambda k_, b_: k_ * b_, k, bcol)
    rhs_w = _each(lambda kb_, e_: kb_ * e_, kb, egc)
    u = _each(lambda t_, v_, b_: _dot(t_, v_ * b_), t, v, bcol)
    w = _each(_dot, t, rhs_w)
    attn = _each(lambda qk_, d_: qk_ * d_, qk, decay)
    ws = _each(_dot, w, s)
    v_new = _each(lambda u_, ws_: u_ - ws_, u, ws)
    qd = _each(lambda q_, e_: q_ * e_, q, egc)
    kt = _each(lambda k_, e_: k_ * e_, k, etail)
    o1 = _each(_dot, qd, s)
    o2 = _each(_dot, attn, v_new)
    out = _each(lambda a_, b_: a_ + b_, o1, o2)
    upd = _each(_dot_tn, kt, v_new)
    s_new = _each(lambda s_, c_, u_: s_ * c_ + u_, s, cd, upd)
    return dict(decay=decay, egc=egc, etail=etail, cd=cd, a=a, t=t, kb=kb, rhs_w=rhs_w, u=u, w=w, attn=attn,
                v_new=v_new, qd=qd, kt=kt, out=out, s_new=s_new)


def gdn_fwd(proj, conv_w, a_log, dt_bias, norm_w, *, name):
    rows_total = proj.shape[0]
    nb = rows_total // GDN_ROWS
    hb = GDN_ROWS // SUBLANES
    cpb = GDN_ROWS // CHUNK

    def body(p_ref, halo_ref, cw_ref, alog_ref, dtb_ref, nw_ref, mix_ref, st_ref, tm_ref,
             ext_ref, qkv_ref, beta_ref, g_ref, s_ref, gct_ref):
        i = pl.program_id(0)

        @pl.when(i == 0)
        def _():
            s_ref[...] = jnp.zeros_like(s_ref)

        _gdn_prologue(i, p_ref, halo_ref, cw_ref, alog_ref, dtb_ref, ext_ref, qkv_ref, beta_ref, g_ref)
        ltri, eye_l, eye_c = _tri(CHUNK), _eye(LANES), _eye(CHUNK)
        causal = _iota((CHUNK, CHUNK), 1) <= _iota((CHUNK, CHUNK), 0)
        strict = _iota((CHUNK, CHUNK), 1) < _iota((CHUNK, CHUNK), 0)
        nw = nw_ref[...]

        def chunk(c, carry):
            rows = pl.ds(pl.multiple_of(c * CHUNK, CHUNK), CHUNK)
            gc = _sel(ltri,g_ref[rows, :])
            gct_ref[...] = _sel_nt(eye_l,gc)
            glast_row = _row(gc, CHUNK - 1)
            beta_c = beta_ref[rows, :]
            for h0 in range(0, GDN_VH, GDN_GROUP):
                hs = list(range(h0, h0 + GDN_GROUP))
                qs = {hq: qkv_ref[rows, hq * GDN_HEAD:(hq + 1) * GDN_HEAD] for hq in range(h0 // 2, (h0 + GDN_GROUP) // 2)}
                ks = {hq: qkv_ref[rows, GDN_K0 + hq * GDN_HEAD:GDN_K0 + (hq + 1) * GDN_HEAD] for hq in qs}
                kks = {hq: _dot_nt(ks[hq], ks[hq]) for hq in qs}
                qks = {hq: _dot_nt(qs[hq], ks[hq]) for hq in qs}
                ss = [s_ref[h] for h in hs]
                for h, s in zip(hs, ss):
                    st_ref[c, h] = s
                f = _gdn_heads_fwd(
                    [qs[h // 2] for h in hs], [ks[h // 2] for h in hs],
                    [qkv_ref[rows, GDN_V0 + h * GDN_HEAD:GDN_V0 + (h + 1) * GDN_HEAD] for h in hs],
                    [kks[h // 2] for h in hs], [qks[h // 2] for h in hs],
                    [_col(gc, GDN_GL + h) for h in hs], [gct_ref[GDN_GL + h:GDN_GL + h + 1, :] for h in hs],
                    [_col(glast_row, GDN_GL + h) for h in hs], [_col(beta_c, h) for h in hs], ss, causal, strict, eye_c)
                for i_h, h in enumerate(hs):
                    hc = slice(h * GDN_HEAD, (h + 1) * GDN_HEAD)
                    s_ref[h] = f["s_new"][i_h]
                    tm_ref[c, h] = f["t"][i_h].astype(tm_ref.dtype)
                    o = f["out"][i_h]
                    r = lax.rsqrt(jnp.mean(o * o, axis=-1, keepdims=True) + RMS_EPS)
                    z = p_ref[rows, GDN_Z0 + h * GDN_HEAD:GDN_Z0 + (h + 1) * GDN_HEAD]
                    mix_ref[rows, hc] = (o * r * nw * _silu(z)).astype(mix_ref.dtype)
            return carry

        lax.fori_loop(0, cpb, chunk, 0)

    vec = lambda n: pl.BlockSpec((1, n), lambda i: (0, 0))
    mix, states, tmats = pl.pallas_call(
        body, name=name, grid=(nb,),
        in_specs=[pl.BlockSpec((GDN_ROWS, GDN_IN_PAD), lambda i: (i, 0)),
                  pl.BlockSpec((SUBLANES, GDN_IN_PAD), lambda i: (jnp.maximum(i * hb - 1, 0), 0)),
                  pl.BlockSpec((SUBLANES, GDN_CONV_DIM), lambda i: (0, 0)),
                  vec(LANES), vec(LANES), vec(GDN_HEAD)],
        out_specs=[pl.BlockSpec((GDN_ROWS, GDN_V), lambda i: (i, 0)),
                   pl.BlockSpec((cpb, GDN_VH, GDN_HEAD, GDN_HEAD), lambda i: (i, 0, 0, 0)),
                   pl.BlockSpec((cpb, GDN_VH, CHUNK, CHUNK), lambda i: (i, 0, 0, 0))],
        out_shape=[jax.ShapeDtypeStruct((rows_total, GDN_V), MXU_DTYPE),
                   jax.ShapeDtypeStruct((rows_total // CHUNK, GDN_VH, GDN_HEAD, GDN_HEAD), F32),
                   jax.ShapeDtypeStruct((rows_total // CHUNK, GDN_VH, CHUNK, CHUNK), MXU_DTYPE)],
        scratch_shapes=[pltpu.VMEM((GDN_ROWS + SUBLANES, GDN_CONV_DIM), F32),
                        pltpu.VMEM((GDN_ROWS, GDN_CONV_DIM), F32),
                        pltpu.VMEM((GDN_ROWS, LANES), F32),
                        pltpu.VMEM((GDN_ROWS, LANES), F32),
                        pltpu.VMEM((GDN_VH, GDN_HEAD, GDN_HEAD), F32),
                        pltpu.VMEM((LANES, CHUNK), F32)],
        compiler_params=_cparams(("arbitrary",)),
    )(proj, proj, _pad_rows(conv_w), _gdn_lane_params(a_log), _gdn_lane_params(dt_bias), norm_w.reshape(1, -1))
    return mix, (states, tmats)


def gdn_bwd(proj, conv_w, a_log, dt_bias, norm_w, saved, dmix, *, name):
    states, tmats = saved
    rows_total = proj.shape[0]
    nb = rows_total // GDN_ROWS
    hb = GDN_ROWS // SUBLANES
    cpb = GDN_ROWS // CHUNK

    def body(p_ref, halo_ref, cw_ref, alog_ref, dtb_ref, nw_ref, st_ref, tm_ref, dm_ref,
             dp_ref, dcw_ref, dalog_ref, ddtb_ref, dnw_ref,
             ext_ref, qkv_ref, beta_ref, g_ref, ds_ref, gct_ref, dext_ref, dgc_ref, dgct_ref, dbeta_ref, pre_ref):
        i = pl.program_id(0)
        blk = nb - 1 - i

        @pl.when(i == 0)
        def _():
            ds_ref[...] = jnp.zeros_like(ds_ref)
            dext_ref[GDN_ROWS:, :] = jnp.zeros((SUBLANES, GDN_CONV_DIM), F32)
            for r in (dcw_ref, dalog_ref, ddtb_ref, dnw_ref):
                r[...] = jnp.zeros_like(r)

        _gdn_prologue(blk, p_ref, halo_ref, cw_ref, alog_ref, dtb_ref, ext_ref, qkv_ref, beta_ref, g_ref, pre_ref)
        ltri, utri, eye_l, eye_c = _tri(CHUNK), _tri(CHUNK, lower=False), _eye(LANES), _eye(CHUNK)
        causal = _iota((CHUNK, CHUNK), 1) <= _iota((CHUNK, CHUNK), 0)
        strict = _iota((CHUNK, CHUNK), 1) < _iota((CHUNK, CHUNK), 0)
        lane = _iota((CHUNK, LANES), 1)
        is_last = _iota((CHUNK, 1), 0) == CHUNK - 1
        nw = nw_ref[...]

        def chunk(cc, carry):
            c = cpb - 1 - cc
            rows = pl.ds(pl.multiple_of(c * CHUNK, CHUNK), CHUNK)
            g_c = g_ref[rows, :]
            gc = _sel(ltri,g_c)
            gct_ref[...] = _sel_nt(eye_l,gc)
            glast_row = _row(gc, CHUNK - 1)
            beta_c = beta_ref[rows, :]
            dgc_ref[...] = jnp.zeros_like(dgc_ref)
            dgct_ref[...] = jnp.zeros_like(dgct_ref)
            dbeta_ref[...] = jnp.zeros_like(dbeta_ref)
            for h0 in range(0, GDN_VH, GDN_GROUP):
                hs = list(range(h0, h0 + GDN_GROUP))
                hqs = list(range(h0 // 2, (h0 + GDN_GROUP) // 2))
                qs = {hq: qkv_ref[rows, hq * GDN_HEAD:(hq + 1) * GDN_HEAD] for hq in hqs}
                ks = {hq: qkv_ref[rows, GDN_K0 + hq * GDN_HEAD:GDN_K0 + (hq + 1) * GDN_HEAD] for hq in hqs}
                kks = {hq: _dot_nt(ks[hq], ks[hq]) for hq in hqs}
                qks = {hq: _dot_nt(qs[hq], ks[hq]) for hq in hqs}
                q = [qs[h // 2] for h in hs]
                k = [ks[h // 2] for h in hs]
                v = [qkv_ref[rows, GDN_V0 + h * GDN_HEAD:GDN_V0 + (h + 1) * GDN_HEAD] for h in hs]
                s = [st_ref[c, h] for h in hs]
                bcol = [_col(beta_c, h) for h in hs]
                f = _gdn_heads_fwd(q, k, v, [kks[h // 2] for h in hs], [qks[h // 2] for h in hs],
                                   [_col(gc, GDN_GL + h) for h in hs], [gct_ref[GDN_GL + h:GDN_GL + h + 1, :] for h in hs],
                                   [_col(glast_row, GDN_GL + h) for h in hs], bcol, s, causal, strict, eye_c,
                                   t=[tm_ref[c, h] for h in hs])
                do = []
                for i_h, h in enumerate(hs):
                    zc = slice(GDN_Z0 + h * GDN_HEAD, GDN_Z0 + (h + 1) * GDN_HEAD)
                    o = f["out"][i_h]
                    z = p_ref[rows, zc]
                    sz = _silu(z)
                    r = lax.rsqrt(jnp.mean(o * o, axis=-1, keepdims=True) + RMS_EPS)
                    on = o * r
                    dm = dm_ref[rows, h * GDN_HEAD:(h + 1) * GDN_HEAD]
                    dnw_ref[...] += jnp.sum(dm * on * sz, axis=0, keepdims=True)
                    d_on = dm * nw * sz
                    dp_ref[rows, zc] = (dm * on * nw * _dsilu(z)).astype(dp_ref.dtype)
                    do.append(r * (d_on - on * jnp.mean(d_on * on, axis=-1, keepdims=True)))
                ds_n = [ds_ref[h] for h in hs]
                dv1 = _each(_dot_tn, f["attn"], do)
                dv2 = _each(_dot, f["kt"], ds_n)
                d_vnew = _each(lambda a_, b_: a_ + b_, dv1, dv2)
                d_attn = _each(lambda do_, vn_: jnp.where(causal, _dot_nt(do_, vn_), 0.0), do, f["v_new"])
                d_qd = _each(_dot_nt, do, s)
                t1 = _each(_dot_tn, f["qd"], do)
                t2 = _each(_dot_tn, f["w"], d_vnew)
                for h, a_, cd_, dsn_, b_ in zip(hs, t1, f["cd"], ds_n, t2):
                    ds_ref[h] = a_ + cd_ * dsn_ - b_
                d_cd = _each(lambda s_, dsn_: jnp.sum(jnp.sum(s_ * dsn_, axis=1, keepdims=True), axis=0, keepdims=True), s, ds_n)
                d_kt = _each(_dot_nt, f["v_new"], ds_n)
                d_w = _each(lambda dv_, s_: -_dot_nt(dv_, s_), d_vnew, s)
                d_rhs_u = _each(_dot_tn, f["t"], d_vnew)
                d_rhs_w = _each(_dot_tn, f["t"], d_w)
                m1 = _each(_dot_nt, d_rhs_u, f["u"])
                m2 = _each(_dot_nt, d_rhs_w, f["w"])
                da = _each(lambda a_, b_: -jnp.where(strict, a_ + b_, 0.0), m1, m2)
                dmm = _each(lambda a_, b_: a_ * b_, da, f["decay"])
                em = _each(lambda da_, a_, dat_, at_: da_ * a_ + dat_ * at_, da, f["a"], d_attn, f["attn"])
                x1 = _each(_dot, dmm, k)
                d_kb = _each(lambda x_, drw_, e_: x_ + drw_ * e_, x1, d_rhs_w, f["egc"])
                dk1 = _each(_dot_tn, dmm, f["kb"])
                dpm = _each(lambda a_, b_: a_ * b_, d_attn, f["decay"])
                dq1 = _each(_dot, dpm, k)
                dq = _each(lambda x_, dqd_, e_: x_ + dqd_ * e_, dq1, d_qd, f["egc"])
                dk2 = _each(_dot_tn, dpm, q)
                dk = _each(lambda a_, b_, dkb_, bc_, dkt_, et_: a_ + b_ + dkb_ * bc_ + dkt_ * et_,
                           dk1, dk2, d_kb, bcol, d_kt, f["etail"])
                for i_h, h in enumerate(hs):
                    tmp = jnp.sum(d_kt[i_h] * f["kt"][i_h], axis=1, keepdims=True)
                    d_gcol = (jnp.sum(em[i_h], axis=1, keepdims=True)
                              + jnp.sum(d_rhs_w[i_h] * f["rhs_w"][i_h], axis=1, keepdims=True)
                              + jnp.sum(d_qd[i_h] * f["qd"][i_h], axis=1, keepdims=True) - tmp)
                    d_glast = jnp.sum(tmp, axis=0, keepdims=True) + d_cd[i_h] * f["cd"][i_h]
                    d_gcol = jnp.where(is_last, d_gcol + d_glast, d_gcol)
                    d_beta = (jnp.sum(d_rhs_u[i_h] * v[i_h], axis=1, keepdims=True)
                              + jnp.sum(d_kb[i_h] * k[i_h], axis=1, keepdims=True))
                    dgc_ref[...] += jnp.where(lane == GDN_GL + h, d_gcol, 0.0)
                    dgct_ref[GDN_GL + h:GDN_GL + h + 1, :] = jnp.sum(em[i_h], axis=0, keepdims=True)
                    dbeta_ref[...] += jnp.where(lane == h, d_beta, 0.0)
                    dext_ref[rows, GDN_V0 + h * GDN_HEAD:GDN_V0 + (h + 1) * GDN_HEAD] = d_rhs_u[i_h] * bcol[i_h]
                for hq in hqs:
                    i0 = 2 * hq - h0
                    dext_ref[rows, hq * GDN_HEAD:(hq + 1) * GDN_HEAD] = dq[i0] + dq[i0 + 1]
                    dext_ref[rows, GDN_K0 + hq * GDN_HEAD:GDN_K0 + (hq + 1) * GDN_HEAD] = dk[i0] + dk[i0 + 1]
            d_gc = dgc_ref[...] - _sel_nt(eye_c,dgct_ref[...])
            dg = _sel(utri,d_gc)
            ba = p_ref[rows, GDN_BA0:GDN_BA0 + LANES]
            d_sp = dg * -jnp.exp(alog_ref[...])
            d_araw = d_sp * _sigmoid(ba + dtb_ref[...])
            d_araw = jnp.where((lane >= GDN_GL) & (lane < GDN_GL + GDN_VH), d_araw, 0.0)
            dalog_ref[...] += jnp.sum(dg * g_c, axis=0, keepdims=True)
            ddtb_ref[...] += jnp.sum(d_araw, axis=0, keepdims=True)
            d_braw = jnp.where(lane < GDN_VH, dbeta_ref[...] * beta_c * (1.0 - beta_c), 0.0)
            dp_ref[rows, GDN_BA0:GDN_BA0 + LANES] = (d_braw + d_araw).astype(dp_ref.dtype)
            return carry

        lax.fori_loop(0, cpb, chunk, 0)
        w = cw_ref[...]
        for hh in range(GDN_CONV_DIM // GDN_HEAD):
            cols = slice(hh * GDN_HEAD, (hh + 1) * GDN_HEAD)
            pre = pre_ref[:, cols]
            d_act = dext_ref[0:GDN_ROWS, cols]
            if hh < 2 * GDN_QKH:
                a = _silu(pre)
                r = lax.rsqrt(jnp.sum(a * a, axis=-1, keepdims=True) + L2_EPS)
                ah = a * r
                if hh < GDN_QKH:
                    d_act = d_act * GDN_SCALE
                d_act = r * (d_act - ah * jnp.sum(d_act * ah, axis=-1, keepdims=True))
            d_pre = d_act * _dsilu(pre)
            dext_ref[0:GDN_ROWS, cols] = d_pre
            du, dws = _conv_bwd_from_ext(dext_ref, ext_ref, w, GDN_CONV, GDN_ROWS, cols)
            for j in range(GDN_CONV):
                dcw_ref[j:j + 1, cols] += dws[j]
            dp_ref[:, cols] = du.astype(dp_ref.dtype)
            dext_ref[GDN_ROWS:, cols] = d_pre[0:SUBLANES, :]

    vec = lambda n: pl.BlockSpec((1, n), lambda i: (0, 0))
    outs = pl.pallas_call(
        body, name=name, grid=(nb,),
        in_specs=[pl.BlockSpec((GDN_ROWS, GDN_IN_PAD), lambda i: (nb - 1 - i, 0)),
                  pl.BlockSpec((SUBLANES, GDN_IN_PAD), lambda i: (jnp.maximum((nb - 1 - i) * hb - 1, 0), 0)),
                  pl.BlockSpec((SUBLANES, GDN_CONV_DIM), lambda i: (0, 0)),
                  vec(LANES), vec(LANES), vec(GDN_HEAD),
                  pl.BlockSpec((cpb, GDN_VH, GDN_HEAD, GDN_HEAD), lambda i: (nb - 1 - i, 0, 0, 0)),
                  pl.BlockSpec((cpb, GDN_VH, CHUNK, CHUNK), lambda i: (nb - 1 - i, 0, 0, 0)),
                  pl.BlockSpec((GDN_ROWS, GDN_V), lambda i: (nb - 1 - i, 0))],
        out_specs=[pl.BlockSpec((GDN_ROWS, GDN_IN_PAD), lambda i: (nb - 1 - i, 0)),
                   pl.BlockSpec((SUBLANES, GDN_CONV_DIM), lambda i: (0, 0)),
                   vec(LANES), vec(LANES), vec(GDN_HEAD)],
        out_shape=[jax.ShapeDtypeStruct((rows_total, GDN_IN_PAD), MXU_DTYPE),
                   jax.ShapeDtypeStruct((SUBLANES, GDN_CONV_DIM), F32),
                   jax.ShapeDtypeStruct((1, LANES), F32), jax.ShapeDtypeStruct((1, LANES), F32),
                   jax.ShapeDtypeStruct((1, GDN_HEAD), F32)],
        scratch_shapes=[pltpu.VMEM((GDN_ROWS + SUBLANES, GDN_CONV_DIM), F32),
                        pltpu.VMEM((GDN_ROWS, GDN_CONV_DIM), F32),
                        pltpu.VMEM((GDN_ROWS, LANES), F32),
                        pltpu.VMEM((GDN_ROWS, LANES), F32),
                        pltpu.VMEM((GDN_VH, GDN_HEAD, GDN_HEAD), F32),
                        pltpu.VMEM((LANES, CHUNK), F32),
                        pltpu.VMEM((GDN_ROWS + SUBLANES, GDN_CONV_DIM), F32),
                        pltpu.VMEM((CHUNK, LANES), F32),
                        pltpu.VMEM((LANES, CHUNK), F32),
                        pltpu.VMEM((CHUNK, LANES), F32),
                        pltpu.VMEM((GDN_ROWS, GDN_CONV_DIM), F32)],
        compiler_params=_cparams(("arbitrary",)),
    )(proj, proj, _pad_rows(conv_w), _gdn_lane_params(a_log), _gdn_lane_params(dt_bias), norm_w.reshape(1, -1),
      states, tmats, dmix)
    dproj, dcw, dalog, ddtb, dnw = outs
    return dproj, [dcw[:GDN_CONV], dalog[0, GDN_GL:GDN_GL + GDN_VH], ddtb[0, GDN_GL:GDN_GL + GDN_VH], dnw[0]]


def chip_exchange(src, *, scatter, name):
    piece_shape = src.shape[1:]

    def body(src_ref, out_ref, send_sems, recv_sems, local_sem):
        x, y, c = (lax.axis_index(a) for a in MESH_AXES)
        me = 2 * x + y

        def piece(j):
            return src_ref.at[j] if scatter else src_ref.at[c]

        local = pltpu.make_async_copy(piece(me), out_ref.at[me], local_sem)
        local.start()
        copies = []
        for k in range(1, N_SHARDS):
            px = 1 - x if k & 2 else x
            py = 1 - y if k & 1 else y
            cp = pltpu.make_async_remote_copy(
                src_ref=piece(2 * px + py), dst_ref=out_ref.at[me], send_sem=send_sems.at[k - 1],
                recv_sem=recv_sems.at[k - 1], device_id=(px, py, c), device_id_type=pl.DeviceIdType.MESH)
            cp.start()
            copies.append(cp)
        for cp in copies:
            cp.wait()
        local.wait()

    hbm = pl.BlockSpec(memory_space=pl.ANY)
    return pl.pallas_call(
        body, name=name, in_specs=[hbm], out_specs=hbm,
        out_shape=jax.ShapeDtypeStruct((N_SHARDS,) + tuple(piece_shape), src.dtype),
        scratch_shapes=[pltpu.SemaphoreType.DMA((N_SHARDS - 1,)), pltpu.SemaphoreType.DMA((N_SHARDS - 1,)),
                        pltpu.SemaphoreType.DMA],
    )(src)


def pair_exchange(src, *, add, name):
    lead, rows, cols = src.shape
    tr = _pick(rows, (512, 256))
    nblk = rows // tr
    n_steps = nblk if add else lead * nblk

    def body(c_ref, *refs):
        if add:
            mine_ref, send_ref, o_ref, recv_ref, send_sems, recv_sems, credit = refs
        else:
            send_ref, o_ref, recv_ref, send_sems, recv_sems, credit = refs
        step = pl.program_id(0) * nblk + pl.program_id(1)
        slot = step % 2
        sibling = (lax.axis_index("x"), lax.axis_index("y"), 1 - lax.axis_index("c"))

        @pl.when(step >= 2)
        def _():
            pl.semaphore_wait(credit, 1)

        cp = pltpu.make_async_remote_copy(
            src_ref=send_ref, dst_ref=recv_ref.at[slot], send_sem=send_sems.at[slot], recv_sem=recv_sems.at[slot],
            device_id=sibling, device_id_type=pl.DeviceIdType.MESH)
        cp.start()
        cp.wait_recv()
        if add:
            o_ref[...] = mine_ref[...] + recv_ref[slot]
        else:
            o_ref[c_ref[0]] = send_ref[...]
            o_ref[1 - c_ref[0]] = recv_ref[slot]
        cp.wait_send()

        @pl.when(step + 2 < n_steps)
        def _():
            pl.semaphore_signal(credit, 1, device_id=sibling, device_id_type=pl.DeviceIdType.MESH)

    flat = src.reshape(lead * rows, cols)
    if add:
        in_specs = [pl.BlockSpec((tr, cols), lambda s, i, c_ref: (c_ref[0] * nblk + i, 0)),
                    pl.BlockSpec((tr, cols), lambda s, i, c_ref: ((1 - c_ref[0]) * nblk + i, 0))]
        out_specs = pl.BlockSpec((tr, cols), lambda s, i, c_ref: (i, 0))
        out_shape = jax.ShapeDtypeStruct((rows, cols), src.dtype)
        grid, args = (1, nblk), (flat, flat)
    else:
        in_specs = [pl.BlockSpec((tr, cols), lambda s, i, c_ref: (s * nblk + i, 0))]
        out_specs = pl.BlockSpec((2, tr, cols), lambda s, i, c_ref: (s, i, 0))
        out_shape = jax.ShapeDtypeStruct((lead * 2, rows, cols), src.dtype)
        grid, args = (lead, nblk), (flat,)
    out = pl.pallas_call(
        body, name=name, out_shape=out_shape,
        grid_spec=pltpu.PrefetchScalarGridSpec(
            num_scalar_prefetch=1, grid=grid, in_specs=in_specs, out_specs=out_specs,
            scratch_shapes=[pltpu.VMEM((2, tr, cols), src.dtype), pltpu.SemaphoreType.DMA((2,)),
                            pltpu.SemaphoreType.DMA((2,)), pltpu.SemaphoreType.REGULAR]),
        compiler_params=_cparams(("arbitrary", "arbitrary")),
    )(lax.axis_index("c").astype(jnp.int32).reshape(1), *args)
    return out if add else out.reshape(lead, 2, rows, cols)


def sum_slots(buf, *, name):
    n, rows, cols = buf.shape
    tr = _pick(rows, (512, 256, 128))

    def body(b_ref, o_ref):
        acc = b_ref[0]
        for j in range(1, n):
            acc = acc + b_ref[j]
        o_ref[...] = acc

    return pl.pallas_call(
        body, name=name, grid=(rows // tr,), in_specs=[pl.BlockSpec((n, tr, cols), lambda i: (0, i, 0))],
        out_specs=pl.BlockSpec((tr, cols), lambda i: (i, 0)), out_shape=jax.ShapeDtypeStruct((rows, cols), F32),
        compiler_params=_cparams(("parallel",)),
    )(buf)


def adamw(w, g, m, v, *, name):
    shape = w.shape
    cols = shape[-1]
    rows = _size(shape) // cols
    w, g, m, v = (t.reshape(rows, cols) for t in (w, g, m, v))
    tr = 256 if rows % 256 == 0 else rows

    def body(w_ref, g_ref, m_ref, v_ref, d_ref, mo_ref, vo_ref):
        gv = g_ref[...]
        mn = ADAM_B1 * m_ref[...] + (1.0 - ADAM_B1) * gv
        vn = ADAM_B2 * v_ref[...] + (1.0 - ADAM_B2) * (gv * gv)
        m_hat = mn / (1.0 - ADAM_B1 ** ADAM_STEP)
        v_hat = vn / (1.0 - ADAM_B2 ** ADAM_STEP)
        d_ref[...] = -ADAM_LR * (m_hat / (jnp.sqrt(v_hat) + ADAM_EPS) + ADAM_WD * w_ref[...])
        mo_ref[...] = mn
        vo_ref[...] = vn

    blk = pl.BlockSpec((tr, cols), lambda i: (i, 0))
    shp = jax.ShapeDtypeStruct((rows, cols), F32)
    outs = pl.pallas_call(
        body, name=name, grid=(rows // tr,), in_specs=[blk] * 4, out_specs=[blk] * 3, out_shape=[shp] * 3,
        compiler_params=_cparams(("parallel",)),
    )(w, g, m, v)
    return [o.reshape(shape) for o in outs]


N_SHARDS = 4
FLAT_COLS = 1024
W_SPECS = (
    ("gdn_w_in", (2, 1024, 6176), 2), ("gdn_conv_w", (2, 4, 4096), 2), ("gdn_a_log", (2, 16), None),
    ("gdn_dt_bias", (2, 16), None), ("gdn_norm_w", (2, 128), None), ("gdn_w_out", (2, 2048, 1024), 1),
    ("sc_w_in", (1, 1024, 8192), 2), ("sc_conv_w", (1, 3, 2048), 2), ("sc_w_out", (1, 2048, 1024), 1),
    ("ssd_w_in", (1, 1024, 5152), 2), ("ssd_conv_w", (1, 4, 3072), 2), ("ssd_conv_b", (1, 3072), 1),
    ("ssd_a_log", (1, 32), None), ("ssd_dt_bias", (1, 32), None), ("ssd_d_skip", (1, 32), None),
    ("ssd_norm_w", (1, 2048), 1), ("ssd_w_out", (1, 2048, 1024), 1), ("ln_g", (4, 1024), None), ("ln_b", (4, 1024), None),
)


def _local_shape(shape, axis):
    return shape if axis is None else tuple(d // N_SHARDS if i == axis else d for i, d in enumerate(shape))


def _size(shape):
    n = 1
    for d in shape:
        n *= d
    return n


PIECE_ROWS = 16


def _piece_rows(shape, axis):
    return -(-_size(_local_shape(shape, axis)) // (FLAT_COLS * PIECE_ROWS)) * PIECE_ROWS


def _flat_rows(specs):
    return -(-sum(_piece_rows(s, a) for _, s, a in specs) // 512) * 512


FLAT_ROWS = _flat_rows(W_SPECS)
FLAT_HALF = FLAT_ROWS // 2


def _pack(pieces, specs=W_SPECS, dtype=F32):
    blocks, used = [], 0
    for p, (_, shape, axis) in zip(pieces, specs):
        rows = _piece_rows(shape, axis)
        flat = p.reshape(-1).astype(dtype)
        if flat.shape[0] < rows * FLAT_COLS:
            flat = jnp.pad(flat, (0, rows * FLAT_COLS - flat.shape[0]))
        blocks.append(flat.reshape(rows, FLAT_COLS))
        used += rows
    blocks.append(jnp.zeros((_flat_rows(specs) - used, FLAT_COLS), dtype))
    return jnp.concatenate(blocks, axis=0)


def _unpack(flat, specs=W_SPECS):
    out, off = [], 0
    for _, shape, axis in specs:
        ls = _local_shape(shape, axis)
        rows = _piece_rows(shape, axis)
        out.append(flat[off:off + rows].reshape(-1)[:_size(ls)].reshape(ls))
        off += rows
    return out


def _shard_of(full, axis, s):
    if axis is None:
        return full
    n = full.shape[axis] // N_SHARDS
    return lax.slice_in_dim(full, s * n, (s + 1) * n, axis=axis)


def _adamw_all(weights, grads_flat, moms, vels):
    grads = _unpack(grads_flat)
    steps = [adamw(w, g, m, v, name="adamw") for w, g, m, v in zip(weights, grads, moms, vels)]
    return grads, [s[0] for s in steps], [s[1] for s in steps], [s[2] for s in steps]


SPLIT_ROWS = 128


def shard_split(w, n_real, *, name):
    rows, n_pad = w.shape
    ns = n_real // N_SHARDS

    def body(w_ref, o_ref):
        for s in range(N_SHARDS):
            o_ref[s] = w_ref[:, s * ns:(s + 1) * ns]

    return pl.pallas_call(
        body, name=name, grid=(rows // SPLIT_ROWS,),
        in_specs=[pl.BlockSpec((SPLIT_ROWS, n_pad), lambda i: (i, 0))],
        out_specs=pl.BlockSpec((N_SHARDS, SPLIT_ROWS, ns), lambda i: (0, i, 0)),
        out_shape=jax.ShapeDtypeStruct((N_SHARDS, rows, ns), F32), compiler_params=_cparams(("parallel",)),
    )(w)


def shard_merge(pieces, n_pad, *, name):
    _, rows, ns = pieces.shape
    n_real = ns * N_SHARDS

    def body(p_ref, o_ref):
        for s in range(N_SHARDS):
            o_ref[:, s * ns:(s + 1) * ns] = p_ref[s].astype(o_ref.dtype)
        if n_pad > n_real:
            o_ref[:, n_real:] = jnp.zeros((SPLIT_ROWS, n_pad - n_real), o_ref.dtype)

    return pl.pallas_call(
        body, name=name, grid=(rows // SPLIT_ROWS,),
        in_specs=[pl.BlockSpec((N_SHARDS, SPLIT_ROWS, ns), lambda i: (0, i, 0))],
        out_specs=pl.BlockSpec((SPLIT_ROWS, n_pad), lambda i: (i, 0)),
        out_shape=jax.ShapeDtypeStruct((rows, n_pad), MXU_DTYPE), compiler_params=_cparams(("parallel",)),
    )(pieces)


def _reduce_scatter(full_grads):
    def shard(g, spec, s):
        _, shape, axis = spec
        return g[:, s] if g.ndim == len(shape) + 1 else _shard_of(g, axis, s)

    by_shard = jnp.stack([_pack([shard(g, spec, s) for g, spec in zip(full_grads, W_SPECS)])
                          for s in range(N_SHARDS)])
    by_half = by_shard.reshape(N_SHARDS, 2, FLAT_HALF, FLAT_COLS).transpose(1, 0, 2, 3)
    by_half = by_half.reshape(2, N_SHARDS * FLAT_HALF, FLAT_COLS)
    pair_sum = pair_exchange(by_half, add=True, name="rs_pair")
    chips = chip_exchange(pair_sum.reshape(N_SHARDS, FLAT_HALF, FLAT_COLS), scatter=True, name="rs_chips")
    half = sum_slots(chips, name="rs_chip_sum")
    return pair_exchange(half[None], add=False, name="rs_halves").reshape(FLAT_ROWS, FLAT_COLS)


def _gather_weights(local_weights):
    def gather(idx, dtype, tag):
        specs = [W_SPECS[i] for i in idx]
        rows = _flat_rows(specs)
        flat = _pack([local_weights[i] for i in idx], specs, dtype)
        halves = chip_exchange(flat.reshape(2, rows // 2, FLAT_COLS), scatter=False, name="gather_chips_" + tag)
        both = pair_exchange(halves, add=False, name="gather_pair_" + tag).reshape(N_SHARDS, rows, FLAT_COLS)
        return [dict(zip(idx, _unpack(both[s], specs))) for s in range(N_SHARDS)]

    matrices = [i for i, (n, _, _) in enumerate(W_SPECS) if n in MXU_WEIGHTS]
    vectors = [i for i, (n, _, a) in enumerate(W_SPECS) if n not in MXU_WEIGHTS and a is not None]
    per_shard = [{**m, **v} for m, v in zip(gather(matrices, MXU_DTYPE, "mxu"), gather(vectors, F32, "f32"))]
    full = []
    for i, (wname, shape, axis) in enumerate(W_SPECS):
        if axis is None:
            full.append(local_weights[i])
        elif wname in W_IN_PAD:
            pieces = jnp.stack([per_shard[s][i] for s in range(N_SHARDS)], axis=1)
            full.append([shard_merge(pieces[j], W_IN_PAD[wname], name="merge_" + wname) for j in range(shape[0])])
        else:
            full.append(jnp.concatenate([per_shard[s][i] for s in range(N_SHARDS)], axis=axis))
    return full


W_IN_PAD = {"gdn_w_in": GDN_IN_PAD, "sc_w_in": SC_IN, "ssd_w_in": SSD_IN_PAD}
MXU_WEIGHTS = ("gdn_w_in", "gdn_w_out", "sc_w_in", "sc_w_out", "ssd_w_in", "ssd_w_out")


def kernel(x, gdn_w_in, gdn_conv_w, gdn_a_log, gdn_dt_bias, gdn_norm_w, gdn_w_out, sc_w_in, sc_conv_w, sc_w_out, ssd_w_in, ssd_conv_w, ssd_conv_b, ssd_a_log, ssd_dt_bias, ssd_d_skip, ssd_norm_w, ssd_w_out, ln_g, ln_b, loss_target, m_gdn_w_in, m_gdn_conv_w, m_gdn_a_log, m_gdn_dt_bias, m_gdn_norm_w, m_gdn_w_out, m_sc_w_in, m_sc_conv_w, m_sc_w_out, m_ssd_w_in, m_ssd_conv_w, m_ssd_conv_b, m_ssd_a_log, m_ssd_dt_bias, m_ssd_d_skip, m_ssd_norm_w, m_ssd_w_out, m_ln_g, m_ln_b, v_gdn_w_in, v_gdn_conv_w, v_gdn_a_log, v_gdn_dt_bias, v_gdn_norm_w, v_gdn_w_out, v_sc_w_in, v_sc_conv_w, v_sc_w_out, v_ssd_w_in, v_ssd_conv_w, v_ssd_conv_b, v_ssd_a_log, v_ssd_dt_bias, v_ssd_d_skip, v_ssd_norm_w, v_ssd_w_out, v_ln_g, v_ln_b):
    weights = [gdn_w_in, gdn_conv_w, gdn_a_log, gdn_dt_bias, gdn_norm_w, gdn_w_out, sc_w_in, sc_conv_w, sc_w_out,
               ssd_w_in, ssd_conv_w, ssd_conv_b, ssd_a_log, ssd_dt_bias, ssd_d_skip, ssd_norm_w, ssd_w_out, ln_g, ln_b]
    moms = [m_gdn_w_in, m_gdn_conv_w, m_gdn_a_log, m_gdn_dt_bias, m_gdn_norm_w, m_gdn_w_out, m_sc_w_in, m_sc_conv_w,
            m_sc_w_out, m_ssd_w_in, m_ssd_conv_w, m_ssd_conv_b, m_ssd_a_log, m_ssd_dt_bias, m_ssd_d_skip, m_ssd_norm_w,
            m_ssd_w_out, m_ln_g, m_ln_b]
    vels = [v_gdn_w_in, v_gdn_conv_w, v_gdn_a_log, v_gdn_dt_bias, v_gdn_norm_w, v_gdn_w_out, v_sc_w_in, v_sc_conv_w,
            v_sc_w_out, v_ssd_w_in, v_ssd_conv_w, v_ssd_conv_b, v_ssd_a_log, v_ssd_dt_bias, v_ssd_d_skip, v_ssd_norm_w,
            v_ssd_w_out, v_ln_g, v_ln_b]
    full = dict(zip([n for n, _, _ in W_SPECS], _gather_weights(weights)))
    x0 = x[0]
    target = loss_target[0]

    layers = (("gdn", 0, GDN_IN_PAD, GDN_IN), ("sc", 0, SC_IN, SC_IN), ("ssd", 0, SSD_IN_PAD, SSD_IN), ("gdn", 1, GDN_IN_PAD, GDN_IN))

    def params(kind, j):
        if kind == "gdn":
            return [full["gdn_conv_w"][j], full["gdn_a_log"][j], full["gdn_dt_bias"][j], full["gdn_norm_w"][j]]
        if kind == "sc":
            return [full["sc_conv_w"][j]]
        return [full["ssd_conv_w"][j], full["ssd_conv_b"][j], full["ssd_a_log"][j], full["ssd_dt_bias"][j],
                full["ssd_d_skip"][j], full["ssd_norm_w"][j]]

    xs, saved = [x0], []
    for i, (kind, j, n_pad, _) in enumerate(layers):
        w_in = full[kind + "_w_in"][j]
        w_out = full[kind + "_w_out"][j].astype(MXU_DTYPE)
        proj = matmul(xs[i], w_in, name=kind + "_proj")
        if kind == "gdn":
            mix, states = gdn_fwd(proj, *params(kind, j), name="gdn_fwd")
        elif kind == "sc":
            mix, states = sc_fwd(proj, *params(kind, j), name="sc_fwd"), None
        else:
            mix, states = ssd_fwd(proj, *params(kind, j), name="ssd_fwd")
        y = matmul(mix, w_out, name=kind + "_out")
        saved.append((w_in, w_out, proj, mix, states, y))
        if i + 1 < DEPTH:
            xs.append(ln_fwd(xs[i], y, full["ln_g"][i], full["ln_b"][i], name="ln_fwd"))

    grads = {n: [None] * s[0] for n, s, _ in W_SPECS}
    dr, dg, db, loss_rows = ln_bwd(xs[DEPTH - 1], saved[DEPTH - 1][5], full["ln_g"][DEPTH - 1], b=full["ln_b"][DEPTH - 1],
                                   target=target, name="ln_bwd_loss")
    dx = None
    for i in reversed(range(DEPTH)):
        kind, j, _, n_in = layers[i]
        w_in, w_out, proj, mix, states, _ = saved[i]
        grads["ln_g"][i], grads["ln_b"][i] = dg[0], db[0]
        dmix = matmul(dr, w_out, tb=True, name=kind + "_dmix")
        grads[kind + "_w_out"][j] = matmul(mix, dr, ta=True, name=kind + "_dw_out")
        if kind == "gdn":
            dproj, (dcw, dalog, ddtb, dnw) = gdn_bwd(proj, *params(kind, j), states, dmix, name="gdn_bwd")
            grads["gdn_conv_w"][j], grads["gdn_a_log"][j], grads["gdn_dt_bias"][j], grads["gdn_norm_w"][j] = dcw, dalog, ddtb, dnw
        elif kind == "sc":
            dproj, dcw = sc_bwd(proj, *params(kind, j), dmix, name="sc_bwd")
            grads["sc_conv_w"][j] = dcw[:SC_CONV]
        else:
            dproj, (dcw, dcb, dalog, ddtb, ddsk, dnw) = ssd_bwd(proj, *params(kind, j), states, dmix, name="ssd_bwd")
            grads["ssd_conv_w"][j], grads["ssd_conv_b"][j], grads["ssd_a_log"][j] = dcw, dcb, dalog
            grads["ssd_dt_bias"][j], grads["ssd_d_skip"][j], grads["ssd_norm_w"][j] = ddtb, ddsk, dnw
        grads[kind + "_w_in"][j] = shard_split(matmul(xs[i], dproj, ta=True, name=kind + "_dw_in"), n_in, name="split_" + kind)
        dx = matmul(dproj, w_in, tb=True, add=dr, add_scale=ALPHA, name=kind + "_dx")
        if i > 0:
            dr, dg, db = ln_bwd(xs[i - 1], saved[i - 1][5], full["ln_g"][i - 1], dx, name="ln_bwd")

    full_grads = [jnp.stack(grads[n]) for n, _, _ in W_SPECS]
    grads_flat = _reduce_scatter(full_grads)
    g_out, d_out, m_out, v_out = _adamw_all(weights, grads_flat, moms, vels)
    loss = lax.psum(loss_rows[0, 0], MESH_AXES)
    return (loss, dx[None], *g_out, *d_out, *m_out, *v_out)
```

```python
import functools

import jax
import jax.numpy as jnp
from jax import lax
from jax.experimental import pallas as pl
from jax.experimental.pallas import tpu as pltpu

F32 = jnp.float32
MXU_DTYPE = jnp.bfloat16

D_MODEL = 1024
DEPTH = 4
D_INNER = 2048
CHUNK = 64
LANES = 128
SUBLANES = 8
VMEM_LIMIT = 56 * 1024 * 1024

GDN_HEAD = 128
GDN_VH = 16
GDN_QKH = 8
GDN_QK = 1024
GDN_V = 2048
GDN_CONV = 4
GDN_CONV_DIM = 4096
GDN_IN = 6176
GDN_IN_PAD = 6272

SC_W = 2048
SC_CONV = 3
SC_IN = 8192

SSD_P = 64
SSD_H = 32
SSD_G = 4
SSD_S = 128
SSD_CONV = 4
SSD_CONV_DIM = 3072
SSD_IN = 5152
SSD_IN_PAD = 5376

ALPHA = (2 * DEPTH) ** 0.25
RMS_EPS = 1e-6
LN_EPS = 1e-5
L2_EPS = 1e-6

ADAM_LR = 0.001
ADAM_B1 = 0.9
ADAM_B2 = 0.999
ADAM_EPS = 1e-08
ADAM_WD = 0.01
ADAM_STEP = 10

MESH_AXES = ("x", "y", "c")


def _cparams(sem):
    return pltpu.CompilerParams(dimension_semantics=sem, vmem_limit_bytes=VMEM_LIMIT)


def _pick(n, prefs):
    for p in prefs:
        if n % p == 0:
            return p
    return n


def _dot(a, b, dims=(((1,), (0,)), ((), ()))):
    return lax.dot_general(a.astype(MXU_DTYPE), b.astype(MXU_DTYPE), dims, preferred_element_type=F32)


def _dot_nt(a, b):
    return _dot(a, b, (((1,), (1,)), ((), ())))


def _dot_tn(a, b):
    return _dot(a, b, (((0,), (0,)), ((), ())))


NN = (((1,), (0,)), ((), ()))
NT = (((1,), (1,)), ((), ()))
TN = (((0,), (0,)), ((), ()))


def _mxu(a, b, dims):
    return lax.dot_general(a, b, dims, preferred_element_type=F32)


def _split(x, pieces):
    out, r = [], x
    for i in range(pieces):
        p = r.astype(jnp.bfloat16)
        out.append(p)
        if i + 1 < pieces:
            r = r - p.astype(F32)
    return out


def _sel(m, x, dims=NN):
    mb = m.astype(jnp.bfloat16)
    x1, x2, x3 = _split(x, 3)
    return (_mxu(mb, x3, dims) + _mxu(mb, x2, dims)) + _mxu(mb, x1, dims)


def _sel_nt(m, x):
    return _sel(m, x, NT)


def _xsel(x, m, dims=NN):
    mb = m.astype(jnp.bfloat16)
    x1, x2, x3 = _split(x, 3)
    return (_mxu(x3, mb, dims) + _mxu(x2, mb, dims)) + _mxu(x1, mb, dims)


def _xsel_nt(x, m):
    return _xsel(x, m, NT)


def _iota(shape, dim):
    return lax.broadcasted_iota(jnp.int32, shape, dim)


def _sigmoid(x):
    return 0.5 * jnp.tanh(0.5 * x) + 0.5


def _silu(x):
    return x * _sigmoid(x)


def _dsilu(x):
    s = _sigmoid(x)
    return s * (1.0 + x * (1.0 - s))


def _softplus(x):
    return jnp.maximum(x, 0.0) + jnp.log(1.0 + jnp.exp(-jnp.abs(x)))


def matmul(a, b, *, ta=False, tb=False, add=None, add_scale=1.0, name):
    if ta:
        kdim, m = a.shape
    else:
        m, kdim = a.shape
    n = b.shape[0] if tb else b.shape[1]
    assert (b.shape[1] if tb else b.shape[0]) == kdim
    tm = _pick(m, (1024, 896, 768, 512)) if ta else _pick(m, (2048, 1024, 512, 256, 128))
    tn = _pick(n, (1024, 896, 768, 512, 256, 128))
    tk = _pick(kdim, (1024, 512, 256)) if ta else _pick(kdim, (1024, 896, 768, 512))
    nk = kdim // tk
    dims = (((0 if ta else 1,), (1 if tb else 0,)), ((), ()))

    def body(a_ref, b_ref, *rest):
        o_ref = rest[-1]
        k = pl.program_id(2)
        part = _dot(a_ref[...], b_ref[...], dims)

        @pl.when(k == 0)
        def _():
            o_ref[...] = part if add is None else part + add_scale * rest[0][...]

        @pl.when(k > 0)
        def _():
            o_ref[...] += part

    a_spec = pl.BlockSpec((tk, tm), lambda i, j, k: (k, i)) if ta else pl.BlockSpec((tm, tk), lambda i, j, k: (i, k))
    b_spec = pl.BlockSpec((tn, tk), lambda i, j, k: (j, k)) if tb else pl.BlockSpec((tk, tn), lambda i, j, k: (k, j))
    o_spec = pl.BlockSpec((tm, tn), lambda i, j, k: (i, j))
    in_specs = [a_spec, b_spec] + ([] if add is None else [o_spec])
    args = (a, b) + (() if add is None else (add,))
    return pl.pallas_call(
        body, name=name, grid=(m // tm, n // tn, nk), in_specs=in_specs, out_specs=o_spec,
        out_shape=jax.ShapeDtypeStruct((m, n), F32),
        compiler_params=_cparams(("parallel", "parallel", "arbitrary")),
    )(*args)


LN_ROWS = 512


def _ln_stats(x, y):
    r = ALPHA * x + y
    mu = jnp.mean(r, axis=-1, keepdims=True)
    rc = r - mu
    var = jnp.mean(rc * rc, axis=-1, keepdims=True)
    rstd = lax.rsqrt(var + LN_EPS)
    return rc * rstd, rstd


def ln_fwd(x, y, g, b, *, name):
    rows, d = x.shape

    def body(x_ref, y_ref, g_ref, b_ref, o_ref):
        xhat, _ = _ln_stats(x_ref[...], y_ref[...])
        o_ref[...] = xhat * g_ref[...] + b_ref[...]

    blk = pl.BlockSpec((LN_ROWS, d), lambda i: (i, 0))
    vec = pl.BlockSpec((1, d), lambda i: (0, 0))
    return pl.pallas_call(
        body, name=name, grid=(rows // LN_ROWS,), in_specs=[blk, blk, vec, vec], out_specs=blk,
        out_shape=jax.ShapeDtypeStruct((rows, d), F32), compiler_params=_cparams(("parallel",)),
    )(x, y, g.reshape(1, d), b.reshape(1, d))


def ln_bwd(x, y, g, dxn=None, *, b=None, target=None, name):
    rows, d = x.shape
    final = target is not None

    def body(x_ref, y_ref, g_ref, *rest):
        if final:
            b_ref, t_ref, dr_ref, dg_ref, db_ref, loss_ref = rest
        else:
            dxn_ref, dr_ref, dg_ref, db_ref = rest
        i = pl.program_id(0)
        xhat, rstd = _ln_stats(x_ref[...], y_ref[...])
        gv = g_ref[...]
        if final:
            err = xhat * gv + b_ref[...] - t_ref[...]
            dxn_v = err * (1.0 / d)
            part = 0.5 * jnp.sum(jnp.mean(err * err, axis=-1, keepdims=True), axis=0, keepdims=True)
        else:
            dxn_v = dxn_ref[...]
        dxh = dxn_v * gv
        m1 = jnp.mean(dxh, axis=-1, keepdims=True)
        m2 = jnp.mean(dxh * xhat, axis=-1, keepdims=True)
        dr_ref[...] = rstd * (dxh - m1 - xhat * m2)

        @pl.when(i == 0)
        def _():
            dg_ref[...] = jnp.zeros_like(dg_ref)
            db_ref[...] = jnp.zeros_like(db_ref)
            if final:
                loss_ref[...] = jnp.zeros_like(loss_ref)

        dg_ref[...] += jnp.sum(dxn_v * xhat, axis=0, keepdims=True)
        db_ref[...] += jnp.sum(dxn_v, axis=0, keepdims=True)
        if final:
            loss_ref[...] += jnp.broadcast_to(part, loss_ref.shape)

    blk = pl.BlockSpec((LN_ROWS, d), lambda i: (i, 0))
    vec = pl.BlockSpec((1, d), lambda i: (0, 0))
    lvec = pl.BlockSpec((1, LANES), lambda i: (0, 0))
    out_shape = [jax.ShapeDtypeStruct((rows, d), F32), jax.ShapeDtypeStruct((1, d), F32), jax.ShapeDtypeStruct((1, d), F32)]
    out_specs = [blk, vec, vec]
    if final:
        in_specs = [blk, blk, vec, vec, blk]
        args = (x, y, g.reshape(1, d), b.reshape(1, d), target)
        out_shape.append(jax.ShapeDtypeStruct((1, LANES), F32))
        out_specs.append(lvec)
    else:
        in_specs = [blk, blk, vec, blk]
        args = (x, y, g.reshape(1, d), dxn)
    return pl.pallas_call(
        body, name=name, grid=(rows // LN_ROWS,), in_specs=in_specs, out_specs=out_specs, out_shape=out_shape,
        compiler_params=_cparams(("arbitrary",)),
    )(*args)


def _rows_from(ref, off, rows, cols=slice(None)):
    r = off % SUBLANES
    if r == 0:
        return ref[off:off + rows, cols]
    window = ref[off - r:off - r + rows + SUBLANES, cols]
    return pltpu.roll(window, rows + SUBLANES - r, axis=0)[:rows]


def _conv_from_ext(ext_ref, w, width, rows, cols=slice(None)):
    out = None
    for j in range(width):
        term = _rows_from(ext_ref, SUBLANES - (width - 1) + j, rows, cols) * w[j:j + 1, cols]
        out = term if out is None else out + term
    return out


def _conv_dgrad_from_ext(dext_ref, w, width, rows, cols):
    out = None
    for j in range(width):
        term = _rows_from(dext_ref, (width - 1) - j, rows, cols) * w[j:j + 1, cols]
        out = term if out is None else out + term
    return out


def _conv_bwd_from_ext(dext_ref, ext_ref, w, width, rows, cols):
    u = ext_ref[SUBLANES:, cols]
    du, dws = None, []
    for j in range(width):
        shifted = _rows_from(dext_ref, (width - 1) - j, rows, cols)
        term = shifted * w[j:j + 1, cols]
        du = term if du is None else du + term
        dws.append(jnp.sum(shifted * u, axis=0, keepdims=True))
    return du, dws


CONV_COLS = 256


SC_ROWS = 128


def sc_fwd(proj, conv_w, *, name):
    rows = proj.shape[0]
    nb = rows // SC_ROWS
    hb = SC_ROWS // SUBLANES

    def body(p_ref, halo_ref, w_ref, o_ref, ext_ref):
        i = pl.program_id(0)
        w = w_ref[...]
        for c0 in range(0, SC_W, CONV_COLS):
            cols, bc, cc, zc = (slice(k * SC_W + c0, k * SC_W + c0 + CONV_COLS) for k in range(4))
            ext_ref[0:SUBLANES, cols] = jnp.where(i == 0, 0.0, halo_ref[:, cc] * halo_ref[:, cols])
            ext_ref[SUBLANES:, cols] = p_ref[:, cc] * p_ref[:, cols]
            cv = _conv_from_ext(ext_ref, w, SC_CONV, SC_ROWS, cols)
            o_ref[:, cols] = (p_ref[:, bc] * cv * _silu(p_ref[:, zc])).astype(o_ref.dtype)

    return pl.pallas_call(
        body, name=name, grid=(nb,),
        in_specs=[pl.BlockSpec((SC_ROWS, SC_IN), lambda i: (i, 0)),
                  pl.BlockSpec((SUBLANES, SC_IN), lambda i: (jnp.maximum(i * hb - 1, 0), 0)),
                  pl.BlockSpec((SUBLANES, SC_W), lambda i: (0, 0))],
        out_specs=pl.BlockSpec((SC_ROWS, SC_W), lambda i: (i, 0)),
        out_shape=jax.ShapeDtypeStruct((rows, SC_W), MXU_DTYPE),
        scratch_shapes=[pltpu.VMEM((SC_ROWS + SUBLANES, SC_W), F32)],
        compiler_params=_cparams(("parallel",)),
    )(proj, proj, _pad_rows(conv_w))


def sc_bwd(proj, conv_w, dmix, *, name):
    rows = proj.shape[0]
    nb = rows // SC_ROWS
    hb = SC_ROWS // SUBLANES

    def body(p_ref, halo_ref, w_ref, dm_ref, dp_ref, dw_ref, ext_ref, dext_ref):
        i = pl.program_id(0)
        blk = nb - 1 - i
        w = w_ref[...]

        @pl.when(i == 0)
        def _():
            dext_ref[SC_ROWS:, :] = jnp.zeros((SUBLANES, SC_W), F32)
            dw_ref[...] = jnp.zeros_like(dw_ref)

        for c0 in range(0, SC_W, CONV_COLS):
            cols, bc, cc, zc = (slice(k * SC_W + c0, k * SC_W + c0 + CONV_COLS) for k in range(4))
            h, bg, cg, z = p_ref[:, cols], p_ref[:, bc], p_ref[:, cc], p_ref[:, zc]
            ext_ref[0:SUBLANES, cols] = jnp.where(blk == 0, 0.0, halo_ref[:, cc] * halo_ref[:, cols])
            ext_ref[SUBLANES:, cols] = cg * h
            taps = [_rows_from(ext_ref, SUBLANES - (SC_CONV - 1) + j, SC_ROWS, cols) for j in range(SC_CONV)]
            cv = None
            for j in range(SC_CONV):
                term = taps[j] * w[j:j + 1, cols]
                cv = term if cv is None else cv + term
            dm = dm_ref[:, cols]
            dy = dm * _silu(z)
            dp_ref[:, zc] = (dm * bg * cv * _dsilu(z)).astype(dp_ref.dtype)
            dp_ref[:, bc] = (dy * cv).astype(dp_ref.dtype)
            dcv = dy * bg
            dext_ref[0:SC_ROWS, cols] = dcv
            du = _conv_dgrad_from_ext(dext_ref, w, SC_CONV, SC_ROWS, cols)
            dp_ref[:, cols] = (du * cg).astype(dp_ref.dtype)
            dp_ref[:, cc] = (du * h).astype(dp_ref.dtype)
            for j in range(SC_CONV):
                dw_ref[j:j + 1, cols] += jnp.sum(taps[j] * dcv, axis=0, keepdims=True)
            dext_ref[SC_ROWS:, cols] = dcv[0:SUBLANES, :]

    return pl.pallas_call(
        body, name=name, grid=(nb,),
        in_specs=[pl.BlockSpec((SC_ROWS, SC_IN), lambda i: (nb - 1 - i, 0)),
                  pl.BlockSpec((SUBLANES, SC_IN), lambda i: (jnp.maximum((nb - 1 - i) * hb - 1, 0), 0)),
                  pl.BlockSpec((SUBLANES, SC_W), lambda i: (0, 0)),
                  pl.BlockSpec((SC_ROWS, SC_W), lambda i: (nb - 1 - i, 0))],
        out_specs=[pl.BlockSpec((SC_ROWS, SC_IN), lambda i: (nb - 1 - i, 0)),
                   pl.BlockSpec((SUBLANES, SC_W), lambda i: (0, 0))],
        out_shape=[jax.ShapeDtypeStruct((rows, SC_IN), MXU_DTYPE), jax.ShapeDtypeStruct((SUBLANES, SC_W), F32)],
        scratch_shapes=[pltpu.VMEM((SC_ROWS + SUBLANES, SC_W), F32), pltpu.VMEM((SC_ROWS + SUBLANES, SC_W), F32)],
        compiler_params=_cparams(("arbitrary",)),
    )(proj, proj, _pad_rows(conv_w), dmix)


def _pad_rows(w, rows=SUBLANES):
    return jnp.pad(w, ((0, rows - w.shape[0]), (0, 0)))


def _pad_lanes(v, lanes=LANES):
    v = v.reshape(1, -1)
    return jnp.pad(v, ((0, 0), (0, lanes - v.shape[1])))


def _tri(n, lower=True):
    r, c = _iota((n, n), 0), _iota((n, n), 1)
    return jnp.where((c <= r) if lower else (c >= r), 1.0, 0.0)


def _eye(n):
    return jnp.where(_iota((n, n), 0) == _iota((n, n), 1), 1.0, 0.0)


def _head_expand(n, width):
    return jnp.where(_iota((LANES, n), 1) // width == _iota((LANES, n), 0), 1.0, 0.0)


def _col(v, h):
    return jnp.sum(jnp.where(_iota(v.shape, 1) == h, v, 0.0), axis=1, keepdims=True)


def _row(v, r):
    return jnp.sum(jnp.where(_iota(v.shape, 0) == r, v, 0.0), axis=0, keepdims=True)


def _expand_row(v, e):
    return jnp.max(_xsel(jnp.broadcast_to(v, (SUBLANES, LANES)), e), axis=0, keepdims=True)


SSD_ROWS = 128
SSD_X0 = D_INNER
SSD_DT0 = D_INNER + SSD_CONV_DIM
SSD_B0 = D_INNER
SSD_C0 = D_INNER + SSD_G * SSD_S
SSD_GW = D_INNER // SSD_G
SSD_HG = SSD_H // SSD_G


def _ssd_prologue(blk, p_ref, halo_ref, cw_ref, cb_ref, dtb_ref, ext_ref, xbc_ref, dt_ref, pre_ref=None):
    ext_ref[0:SUBLANES, :] = jnp.where(blk == 0, 0.0, halo_ref[:, SSD_X0:SSD_DT0])
    ext_ref[SUBLANES:, :] = p_ref[:, SSD_X0:SSD_DT0]
    w = cw_ref[...]
    for c0 in range(0, SSD_CONV_DIM, CONV_COLS):
        cols = slice(c0, c0 + CONV_COLS)
        pre = _conv_from_ext(ext_ref, w, SSD_CONV, SSD_ROWS, cols) + cb_ref[:, cols]
        if pre_ref is not None:
            pre_ref[:, cols] = pre
        xbc_ref[:, cols] = _silu(pre)
    dt_ref[...] = _softplus(p_ref[:, SSD_DT0:SSD_DT0 + LANES] + dtb_ref[...])


def _ssd_chunk_decays(dt_c, a_row, ltri, eye_l, act_ref):
    da = dt_c * a_row
    ac = _sel(ltri,da)
    act_ref[...] = _sel_nt(eye_l,ac)
    ac_last = _row(ac, CHUNK - 1)
    return ac, jnp.exp(ac_last - ac), jnp.exp(ac), jnp.exp(ac_last)


def _ssd_seg(ac, act_ref, h, causal):
    return jnp.where(causal, jnp.exp(jnp.minimum(_col(ac, h) - act_ref[pl.ds(h, 1), :], 0.0)), 0.0)


def _ssd_half(pair, e):
    upper = _iota(pair.shape, 1) >= SSD_P
    return jnp.where(upper if e % 2 else jnp.logical_not(upper), pair, 0.0)


def _ssd_groups_fwd(xbc_ref, rows, dt_exp, tail_exp, cdec_exp, ac, act_ref, states, causal):
    gs, heads = range(SSD_G), range(SSD_HG)
    gls = [slice(g * SSD_GW, (g + 1) * SSD_GW) for g in gs]
    bg = [_mx(xbc_ref[rows, SSD_B0 + g * SSD_S:SSD_B0 + (g + 1) * SSD_S]) for g in gs]
    cg = [_mx(xbc_ref[rows, SSD_C0 + g * SSD_S:SSD_C0 + (g + 1) * SSD_S]) for g in gs]
    s_c = [_mx(s) for s in states]
    xdt = [xbc_ref[rows, gl] * dt_exp[:, gl] for gl in gls]
    cb = [_dot_nt(cg[g], bg[g]) for g in gs]
    cs = [_dot(cg[g], s_c[g]) for g in gs]
    segs = [[_ssd_seg(ac, act_ref, g * SSD_HG + e, causal) for e in heads] for g in gs]
    gms = [[seg * cb[g] for seg in segs[g]] for g in gs]
    gms_c = [[_mx(gm) for gm in gms[g]] for g in gs]
    halves = [[_ssd_half(xdt[g][:, (e // 2) * LANES:(e // 2 + 1) * LANES], e) for e in heads] for g in gs]
    parts = [[_dot(gms_c[g][e], halves[g][e]) for e in heads] for g in gs]
    yd = [jnp.concatenate([parts[g][2 * p] + parts[g][2 * p + 1] for p in range(SSD_HG // 2)], axis=1) for g in gs]
    st = [_dot_tn(bg[g], xdt[g] * tail_exp[:, gls[g]]) for g in gs]
    return [(yd[g] + cs[g] * cdec_exp[:, gls[g]], st[g], bg[g], cg[g], s_c[g], cb[g], xdt[g], cs[g],
             segs[g], gms[g], gms_c[g]) for g in gs]


def ssd_fwd(proj, conv_w, conv_b, a_log, dt_bias, d_skip, norm_w, *, name):
    rows_total = proj.shape[0]
    nb = rows_total // SSD_ROWS
    hb = SSD_ROWS // SUBLANES
    cpb = SSD_ROWS // CHUNK

    def body(p_ref, halo_ref, cw_ref, cb_ref, alog_ref, dtb_ref, dsk_ref, nw_ref, mix_ref, st_ref,
             ext_ref, xbc_ref, dt_ref, s_ref, act_ref):
        i = pl.program_id(0)

        @pl.when(i == 0)
        def _():
            s_ref[...] = jnp.zeros_like(s_ref)

        _ssd_prologue(i, p_ref, halo_ref, cw_ref, cb_ref, dtb_ref, ext_ref, xbc_ref, dt_ref)
        a_row = -jnp.exp(alog_ref[...])
        expand = _head_expand(D_INNER, SSD_P)
        dsk_exp = _expand_row(dsk_ref[...], expand)
        ltri, eye_l = _tri(CHUNK), _eye(LANES)
        causal = _iota((CHUNK, CHUNK), 1) <= _iota((CHUNK, CHUNK), 0)

        def chunk(c, carry):
            rows = pl.ds(pl.multiple_of(c * CHUNK, CHUNK), CHUNK)
            dt_c = dt_ref[rows, :]
            ac, tail, cdec, tot = _ssd_chunk_decays(dt_c, a_row, ltri, eye_l, act_ref)
            dt_exp = _xsel(dt_c, expand)
            tail_exp = _xsel(tail, expand)
            cdec_exp = _xsel(cdec, expand)
            tot_exp = _expand_row(tot, expand)
            states = [s_ref[g] for g in range(SSD_G)]
            fwd = _ssd_groups_fwd(xbc_ref, rows, dt_exp, tail_exp, cdec_exp, ac, act_ref, states, causal)
            for g in range(SSD_G):
                gl = slice(g * SSD_GW, (g + 1) * SSD_GW)
                st_ref[c, g] = states[g]
                y, st = fwd[g][:2]
                s_ref[g] = states[g] * tot_exp[:, gl] + st
                y = (y + dsk_exp[:, gl] * xbc_ref[rows, gl]) * _silu(p_ref[rows, gl])
                r = lax.rsqrt(jnp.mean(y * y, axis=-1, keepdims=True) + RMS_EPS)
                mix_ref[rows, gl] = (y * r * nw_ref[:, gl]).astype(mix_ref.dtype)
            return carry

        lax.fori_loop(0, cpb, chunk, 0)

    vec = lambda n: pl.BlockSpec((1, n), lambda i: (0, 0))
    return pl.pallas_call(
        body, name=name, grid=(nb,),
        in_specs=[pl.BlockSpec((SSD_ROWS, SSD_IN_PAD), lambda i: (i, 0)),
                  pl.BlockSpec((SUBLANES, SSD_IN_PAD), lambda i: (jnp.maximum(i * hb - 1, 0), 0)),
                  pl.BlockSpec((SUBLANES, SSD_CONV_DIM), lambda i: (0, 0)),
                  vec(SSD_CONV_DIM), vec(LANES), vec(LANES), vec(LANES), vec(D_INNER)],
        out_specs=[pl.BlockSpec((SSD_ROWS, D_INNER), lambda i: (i, 0)),
                   pl.BlockSpec((cpb, SSD_G, SSD_S, SSD_GW), lambda i: (i, 0, 0, 0))],
        out_shape=[jax.ShapeDtypeStruct((rows_total, D_INNER), MXU_DTYPE),
                   jax.ShapeDtypeStruct((rows_total // CHUNK, SSD_G, SSD_S, SSD_GW), F32)],
        scratch_shapes=[pltpu.VMEM((SSD_ROWS + SUBLANES, SSD_CONV_DIM), F32),
                        pltpu.VMEM((SSD_ROWS, SSD_CONV_DIM), F32),
                        pltpu.VMEM((SSD_ROWS, LANES), F32),
                        pltpu.VMEM((SSD_G, SSD_S, SSD_GW), F32),
                        pltpu.VMEM((LANES, CHUNK), F32)],
        compiler_params=_cparams(("arbitrary",)),
    )(proj, proj, _pad_rows(conv_w), conv_b.reshape(1, -1), _pad_lanes(a_log), _pad_lanes(dt_bias),
      _pad_lanes(d_skip), norm_w.reshape(1, -1))


def ssd_bwd(proj, conv_w, conv_b, a_log, dt_bias, d_skip, norm_w, states, dmix, *, name):
    rows_total = proj.shape[0]
    nb = rows_total // SSD_ROWS
    hb = SSD_ROWS // SUBLANES
    cpb = SSD_ROWS // CHUNK

    def body(p_ref, halo_ref, cw_ref, cb_ref, alog_ref, dtb_ref, dsk_ref, nw_ref, st_ref, dm_ref,
             dp_ref, dcw_ref, dcb_ref, dalog_ref, ddtb_ref, ddsk_ref, dnw_ref,
             ext_ref, xbc_ref, dt_ref, ds_ref, act_ref, dext_ref, dac_ref, dact_ref, ddskw_ref, pre_ref):
        i = pl.program_id(0)
        blk = nb - 1 - i

        @pl.when(i == 0)
        def _():
            ds_ref[...] = jnp.zeros_like(ds_ref)
            dext_ref[SSD_ROWS:, :] = jnp.zeros((SUBLANES, SSD_CONV_DIM), F32)
            ddskw_ref[...] = jnp.zeros_like(ddskw_ref)
            for r in (dcw_ref, dcb_ref, dalog_ref, ddtb_ref, ddsk_ref, dnw_ref):
                r[...] = jnp.zeros_like(r)

        _ssd_prologue(blk, p_ref, halo_ref, cw_ref, cb_ref, dtb_ref, ext_ref, xbc_ref, dt_ref, pre_ref)
        a_row = -jnp.exp(alog_ref[...])
        expand = _head_expand(D_INNER, SSD_P)
        dsk_exp = _expand_row(dsk_ref[...], expand)
        ltri, utri, eye_l, eye_c = _tri(CHUNK), _tri(CHUNK, lower=False), _eye(LANES), _eye(CHUNK)
        causal = _iota((CHUNK, CHUNK), 1) <= _iota((CHUNK, CHUNK), 0)
        dp_ref[:, SSD_DT0 + LANES:] = jnp.zeros((SSD_ROWS, SSD_IN_PAD - SSD_DT0 - LANES), dp_ref.dtype)

        def chunk(cc, carry):
            c = cpb - 1 - cc
            rows = pl.ds(pl.multiple_of(c * CHUNK, CHUNK), CHUNK)
            dt_c = dt_ref[rows, :]
            ac, tail, cdec, tot = _ssd_chunk_decays(dt_c, a_row, ltri, eye_l, act_ref)
            dt_exp = _xsel(dt_c, expand)
            tail_exp = _xsel(tail, expand)
            cdec_exp = _xsel(cdec, expand)
            tot_exp = _expand_row(tot, expand)
            dac_ref[...] = jnp.zeros_like(dac_ref)
            dact_ref[...] = jnp.zeros_like(dact_ref)
            d_cdec = jnp.zeros((CHUNK, LANES), F32)
            d_tail = jnp.zeros((CHUNK, LANES), F32)
            d_dt = jnp.zeros((CHUNK, LANES), F32)
            d_tot = jnp.zeros((1, LANES), F32)
            gs = range(SSD_G)
            gls = [slice(g * SSD_GW, (g + 1) * SSD_GW) for g in gs]
            exs = [expand[:, gl] for gl in gls]
            states = [st_ref[c, g] for g in gs]
            fwd = _ssd_groups_fwd(xbc_ref, rows, dt_exp, tail_exp, cdec_exp, ac, act_ref, states, causal)
            ys, _, bgs, cgs, s_cs, cbs, xdts, css, segss, gmss, gms_cs = (list(t) for t in zip(*fwd))
            xss = [xbc_ref[rows, gl] for gl in gls]
            dys = []
            for g, gl in enumerate(gls):
                z = p_ref[rows, gl]
                sz = _silu(z)
                y2 = ys[g] + dsk_exp[:, gl] * xss[g]
                yg = y2 * sz
                r = lax.rsqrt(jnp.mean(yg * yg, axis=-1, keepdims=True) + RMS_EPS)
                yn = yg * r
                dm = dm_ref[rows, gl]
                dnw_ref[:, gl] += jnp.sum(dm * yn, axis=0, keepdims=True)
                dyn = dm * nw_ref[:, gl]
                dyg = r * (dyn - yn * jnp.mean(dyn * yn, axis=-1, keepdims=True))
                dp_ref[rows, gl] = (dyg * y2 * _dsilu(z)).astype(dp_ref.dtype)
                dys.append(dyg * sz)
                ddskw_ref[:, gl] += jnp.sum(dys[g] * xss[g], axis=0, keepdims=True)
            ds_gs = [ds_ref[g] for g in gs]
            ds_cs = [_mx(d) for d in ds_gs]
            dycs = [_mx(dys[g] * cdec_exp[:, gls[g]]) for g in gs]
            ds_new = [_dot_tn(cgs[g], dycs[g]) for g in gs]
            dcgs = [_dot_nt(dycs[g], s_cs[g]) for g in gs]
            d_xdtds = [_dot(bgs[g], ds_cs[g]) for g in gs]
            dbgs = [_dot_nt(xdts[g] * tail_exp[:, gls[g]], ds_cs[g]) for g in gs]
            for g in gs:
                ds_ref[g] = ds_gs[g] * tot_exp[:, gls[g]] + ds_new[g]
                sds = jnp.broadcast_to(jnp.sum(states[g] * ds_gs[g], axis=0, keepdims=True), (SUBLANES, SSD_GW))
                d_tot = d_tot + jnp.max(_xsel_nt(sds, exs[g]), axis=0, keepdims=True)
                d_cdec = d_cdec + _xsel_nt(dys[g] * css[g], exs[g])
                d_tail = d_tail + _xsel_nt(d_xdtds[g] * xdts[g], exs[g])
            heads = range(SSD_HG)
            dy_hs = [[_mx(_ssd_half(dys[g][:, (e // 2) * LANES:(e // 2 + 1) * LANES], e)) for e in heads] for g in gs]
            xps_cs = [[_mx(xdts[g][:, p * LANES:(p + 1) * LANES]) for p in range(SSD_HG // 2)] for g in gs]
            backs = [[_dot_tn(gms_cs[g][e], dy_hs[g][e]) for e in heads] for g in gs]
            dg_ms = [[jnp.where(causal, _dot_nt(dy_hs[g][e], xps_cs[g][e // 2]), 0.0) for e in heads] for g in gs]
            d_cbs = []
            for g in gs:
                d_cb = None
                for e in heads:
                    h = g * SSD_HG + e
                    term = dg_ms[g][e] * segss[g][e]
                    d_cb = term if d_cb is None else d_cb + term
                    em = dg_ms[g][e] * gmss[g][e]
                    dac_ref[...] += jnp.where(_iota((CHUNK, LANES), 1) == h, jnp.sum(em, axis=1, keepdims=True), 0.0)
                    dact_ref[h:h + 1, :] = jnp.sum(em, axis=0, keepdims=True)
                d_cbs.append(_mx(d_cb))
            dcg2 = [_dot(d_cbs[g], bgs[g]) for g in gs]
            dbg2 = [_dot_tn(d_cbs[g], cgs[g]) for g in gs]
            for g, gl in enumerate(gls):
                d_xdt = d_xdtds[g] * tail_exp[:, gl] + jnp.concatenate(
                    [backs[g][2 * p] + backs[g][2 * p + 1] for p in range(SSD_HG // 2)], axis=1)
                d_dt = d_dt + _xsel_nt(d_xdt * xss[g], exs[g])
                dext_ref[rows, gl] = d_xdt * dt_exp[:, gl] + dys[g] * dsk_exp[:, gl]
                dext_ref[rows, SSD_B0 + g * SSD_S:SSD_B0 + (g + 1) * SSD_S] = dbgs[g] + dbg2[g]
                dext_ref[rows, SSD_C0 + g * SSD_S:SSD_C0 + (g + 1) * SSD_S] = dcgs[g] + dcg2[g]
            d_ac = dac_ref[...] - _sel_nt(eye_c,dact_ref[...]) + d_cdec * cdec - d_tail * tail
            d_last = jnp.sum(d_tail * tail, axis=0, keepdims=True) + d_tot * tot
            d_ac = jnp.where(_iota((CHUNK, LANES), 0) == CHUNK - 1, d_ac + d_last, d_ac)
            d_da = _sel(utri,d_ac)
            d_dt = d_dt + d_da * a_row
            dalog_ref[...] += jnp.sum(d_da * dt_c, axis=0, keepdims=True) * a_row
            d_raw = d_dt * _sigmoid(p_ref[rows, SSD_DT0:SSD_DT0 + LANES] + dtb_ref[...])
            d_raw = jnp.where(_iota((CHUNK, LANES), 1) < SSD_H, d_raw, 0.0)
            ddtb_ref[...] += jnp.sum(d_raw, axis=0, keepdims=True)
            dp_ref[rows, SSD_DT0:SSD_DT0 + LANES] = d_raw.astype(dp_ref.dtype)
            return carry

        lax.fori_loop(0, cpb, chunk, 0)
        w = cw_ref[...]
        for c0 in range(0, SSD_CONV_DIM, CONV_COLS):
            cols = slice(c0, c0 + CONV_COLS)
            d_pre = dext_ref[0:SSD_ROWS, cols] * _dsilu(pre_ref[:, cols])
            dext_ref[0:SSD_ROWS, cols] = d_pre
            dcb_ref[:, cols] += jnp.sum(d_pre, axis=0, keepdims=True)
            du, dws = _conv_bwd_from_ext(dext_ref, ext_ref, w, SSD_CONV, SSD_ROWS, cols)
            for j in range(SSD_CONV):
                dcw_ref[j:j + 1, cols] += dws[j]
            dp_ref[:, SSD_X0 + c0:SSD_X0 + c0 + CONV_COLS] = du.astype(dp_ref.dtype)
            dext_ref[SSD_ROWS:, cols] = d_pre[0:SUBLANES, :]

        @pl.when(i == nb - 1)
        def _():
            ddsk_ref[...] = jnp.max(_xsel_nt(jnp.broadcast_to(ddskw_ref[...], (SUBLANES, D_INNER)), expand), axis=0, keepdims=True)

    vec = lambda n: pl.BlockSpec((1, n), lambda i: (0, 0))
    outs = pl.pallas_call(
        body, name=name, grid=(nb,),
        in_specs=[pl.BlockSpec((SSD_ROWS, SSD_IN_PAD), lambda i: (nb - 1 - i, 0)),
                  pl.BlockSpec((SUBLANES, SSD_IN_PAD), lambda i: (jnp.maximum((nb - 1 - i) * hb - 1, 0), 0)),
                  pl.BlockSpec((SUBLANES, SSD_CONV_DIM), lambda i: (0, 0)),
                  vec(SSD_CONV_DIM), vec(LANES), vec(LANES), vec(LANES), vec(D_INNER),
                  pl.BlockSpec((cpb, SSD_G, SSD_S, SSD_GW), lambda i: (nb - 1 - i, 0, 0, 0)),
                  pl.BlockSpec((SSD_ROWS, D_INNER), lambda i: (nb - 1 - i, 0))],
        out_specs=[pl.BlockSpec((SSD_ROWS, SSD_IN_PAD), lambda i: (nb - 1 - i, 0)),
                   pl.BlockSpec((SUBLANES, SSD_CONV_DIM), lambda i: (0, 0)),
                   vec(SSD_CONV_DIM), vec(LANES), vec(LANES), vec(LANES), vec(D_INNER)],
        out_shape=[jax.ShapeDtypeStruct((rows_total, SSD_IN_PAD), MXU_DTYPE),
                   jax.ShapeDtypeStruct((SUBLANES, SSD_CONV_DIM), F32),
                   jax.ShapeDtypeStruct((1, SSD_CONV_DIM), F32), jax.ShapeDtypeStruct((1, LANES), F32),
                   jax.ShapeDtypeStruct((1, LANES), F32), jax.ShapeDtypeStruct((1, LANES), F32),
                   jax.ShapeDtypeStruct((1, D_INNER), F32)],
        scratch_shapes=[pltpu.VMEM((SSD_ROWS + SUBLANES, SSD_CONV_DIM), F32),
                        pltpu.VMEM((SSD_ROWS, SSD_CONV_DIM), F32),
                        pltpu.VMEM((SSD_ROWS, LANES), F32),
                        pltpu.VMEM((SSD_G, SSD_S, SSD_GW), F32),
                        pltpu.VMEM((LANES, CHUNK), F32),
                        pltpu.VMEM((SSD_ROWS + SUBLANES, SSD_CONV_DIM), F32),
                        pltpu.VMEM((CHUNK, LANES), F32),
                        pltpu.VMEM((LANES, CHUNK), F32),
                        pltpu.VMEM((1, D_INNER), F32),
                        pltpu.VMEM((SSD_ROWS, SSD_CONV_DIM), F32)],
        compiler_params=_cparams(("arbitrary",)),
    )(proj, proj, _pad_rows(conv_w), conv_b.reshape(1, -1), _pad_lanes(a_log), _pad_lanes(dt_bias),
      _pad_lanes(d_skip), norm_w.reshape(1, -1), states, dmix)
    dproj, dcw, dcb, dalog, ddtb, ddsk, dnw = outs
    return dproj, [dcw[:SSD_CONV], dcb[0], dalog[0, :SSD_H], ddtb[0, :SSD_H], ddsk[0, :SSD_H], dnw[0]]


GDN_ROWS = 128
GDN_K0 = GDN_QK
GDN_V0 = 2 * GDN_QK
GDN_Z0 = GDN_CONV_DIM
GDN_BA0 = GDN_CONV_DIM + GDN_V
GDN_GL = GDN_VH
GDN_SCALE = GDN_HEAD ** -0.5
GDN_GROUP = 16


def _gdn_lane_params(v):
    return jnp.pad(v.reshape(1, GDN_VH), ((0, 0), (GDN_GL, LANES - GDN_GL - GDN_VH)))


def _gdn_prologue(blk, p_ref, halo_ref, cw_ref, alog_ref, dtb_ref, ext_ref, qkv_ref, beta_ref, g_ref, pre_ref=None):
    ext_ref[0:SUBLANES, :] = jnp.where(blk == 0, 0.0, halo_ref[:, 0:GDN_CONV_DIM])
    ext_ref[SUBLANES:, :] = p_ref[:, 0:GDN_CONV_DIM]
    w = cw_ref[...]
    for hh in range(GDN_CONV_DIM // GDN_HEAD):
        cols = slice(hh * GDN_HEAD, (hh + 1) * GDN_HEAD)
        pre = _conv_from_ext(ext_ref, w, GDN_CONV, GDN_ROWS, cols)
        if pre_ref is not None:
            pre_ref[:, cols] = pre
        a = _silu(pre)
        if hh < 2 * GDN_QKH:
            r = lax.rsqrt(jnp.sum(a * a, axis=-1, keepdims=True) + L2_EPS)
            a = a * (r * (GDN_SCALE if hh < GDN_QKH else 1.0))
        qkv_ref[:, cols] = a
    ba = p_ref[:, GDN_BA0:GDN_BA0 + LANES]
    beta_ref[...] = _sigmoid(ba)
    g_ref[...] = -jnp.exp(alog_ref[...]) * _softplus(ba + dtb_ref[...])


def _each(f, *lists):
    return [f(*z) for z in zip(*lists)]


def _inv_unit_lower_each(a_list, eye_c):
    xs = [eye_c - a for a in a_list]
    ps = [_mx(a) for a in a_list]
    n = 2
    while n < CHUNK:
        ps = [_mx(_dot(p, p)) for p in ps]
        xs = [x + _dot(x, p) for x, p in zip(xs, ps)]
        n *= 2
    return xs


def _mx(x):
    return x.astype(MXU_DTYPE)


def _gdn_heads_fwd(q, k, v, kk, qk, gcol, grow, glast, bcol, s, causal, strict, eye_c, t=None):
    decay = _each(lambda gc_, gr_: jnp.where(causal, jnp.exp(jnp.minimum(gc_ - gr_, 0.0)), 0.0), gcol, grow)
    egc = _each(jnp.exp, gcol)
    etail = _each(lambda gl_, gc_: jnp.exp(gl_ - gc_), glast, gcol)
    cd = _each(jnp.exp, glast)
    a = _each(lambda b_, kk_, d_: jnp.where(strict, b_ * kk_ * d_, 0.0), bcol, kk, decay)
    if t is None:
        t = _inv_unit_lower_each(a, eye_c)
    t_c, s_c = _each(_mx, t), _each(_mx, s)
    kb = _each(lambda k_, b_: k_ * b_, k, bcol)
    rhs_w = _each(lambda kb_, e_: kb_ * e_, kb, egc)
    u = _each(lambda t_, v_, b_: _dot(t_, v_ * b_), t_c, v, bcol)
    w = _each(_dot, t_c, rhs_w)
    w_c = _each(_mx, w)
    attn = _each(lambda qk_, d_: qk_ * d_, qk, decay)
    attn_c = _each(_mx, attn)
    ws = _each(_dot, w_c, s_c)
    v_new = _each(lambda u_, ws_: u_ - ws_, u, ws)
    vn_c = _each(_mx, v_new)
    qd = _each(lambda q_, e_: q_ * e_, q, egc)
    kt = _each(lambda k_, e_: k_ * e_, k, etail)
    qd_c, kt_c = _each(_mx, qd), _each(_mx, kt)
    o1 = _each(_dot, qd_c, s_c)
    o2 = _each(_dot, attn_c, vn_c)
    out = _each(lambda a_, b_: a_ + b_, o1, o2)
    upd = _each(_dot_tn, kt_c, vn_c)
    s_new = _each(lambda s_, c_, u_: s_ * c_ + u_, s, cd, upd)
    return dict(decay=decay, egc=egc, etail=etail, cd=cd, a=a, t=t, kb=kb, rhs_w=rhs_w, u=u, w=w, attn=attn,
                v_new=v_new, qd=qd, kt=kt, out=out, s_new=s_new,
                t_c=t_c, s_c=s_c, w_c=w_c, attn_c=attn_c, vn_c=vn_c, qd_c=qd_c, kt_c=kt_c)


def gdn_fwd(proj, conv_w, a_log, dt_bias, norm_w, *, name):
    rows_total = proj.shape[0]
    nb = rows_total // GDN_ROWS
    hb = GDN_ROWS // SUBLANES
    cpb = GDN_ROWS // CHUNK

    def body(p_ref, halo_ref, cw_ref, alog_ref, dtb_ref, nw_ref, mix_ref, st_ref, tm_ref,
             ext_ref, qkv_ref, beta_ref, g_ref, s_ref, gct_ref):
        i = pl.program_id(0)

        @pl.when(i == 0)
        def _():
            s_ref[...] = jnp.zeros_like(s_ref)

        _gdn_prologue(i, p_ref, halo_ref, cw_ref, alog_ref, dtb_ref, ext_ref, qkv_ref, beta_ref, g_ref)
        ltri, eye_l, eye_c = _tri(CHUNK), _eye(LANES), _eye(CHUNK)
        causal = _iota((CHUNK, CHUNK), 1) <= _iota((CHUNK, CHUNK), 0)
        strict = _iota((CHUNK, CHUNK), 1) < _iota((CHUNK, CHUNK), 0)
        nw = nw_ref[...]

        def chunk(c, carry):
            rows = pl.ds(pl.multiple_of(c * CHUNK, CHUNK), CHUNK)
            gc = _sel(ltri,g_ref[rows, :])
            gct_ref[...] = _sel_nt(eye_l,gc)
            glast_row = _row(gc, CHUNK - 1)
            beta_c = beta_ref[rows, :]
            for h0 in range(0, GDN_VH, GDN_GROUP):
                hs = list(range(h0, h0 + GDN_GROUP))
                qs = {hq: qkv_ref[rows, hq * GDN_HEAD:(hq + 1) * GDN_HEAD] for hq in range(h0 // 2, (h0 + GDN_GROUP) // 2)}
                ks = {hq: qkv_ref[rows, GDN_K0 + hq * GDN_HEAD:GDN_K0 + (hq + 1) * GDN_HEAD] for hq in qs}
                ks_c = {hq: _mx(ks[hq]) for hq in qs}
                kks = {hq: _dot_nt(ks_c[hq], ks_c[hq]) for hq in qs}
                qks = {hq: _dot_nt(qs[hq], ks_c[hq]) for hq in qs}
                ss = [s_ref[h] for h in hs]
                for h, s in zip(hs, ss):
                    st_ref[c, h] = s
                f = _gdn_heads_fwd(
                    [qs[h // 2] for h in hs], [ks[h // 2] for h in hs],
                    [qkv_ref[rows, GDN_V0 + h * GDN_HEAD:GDN_V0 + (h + 1) * GDN_HEAD] for h in hs],
                    [kks[h // 2] for h in hs], [qks[h // 2] for h in hs],
                    [_col(gc, GDN_GL + h) for h in hs], [gct_ref[GDN_GL + h:GDN_GL + h + 1, :] for h in hs],
                    [_col(glast_row, GDN_GL + h) for h in hs], [_col(beta_c, h) for h in hs], ss, causal, strict, eye_c)
                for i_h, h in enumerate(hs):
                    hc = slice(h * GDN_HEAD, (h + 1) * GDN_HEAD)
                    s_ref[h] = f["s_new"][i_h]
                    tm_ref[c, h] = f["t"][i_h].astype(tm_ref.dtype)
                    o = f["out"][i_h]
                    r = lax.rsqrt(jnp.mean(o * o, axis=-1, keepdims=True) + RMS_EPS)
                    z = p_ref[rows, GDN_Z0 + h * GDN_HEAD:GDN_Z0 + (h + 1) * GDN_HEAD]
                    mix_ref[rows, hc] = (o * r * nw * _silu(z)).astype(mix_ref.dtype)
            return carry

        lax.fori_loop(0, cpb, chunk, 0)

    vec = lambda n: pl.BlockSpec((1, n), lambda i: (0, 0))
    mix, states, tmats = pl.pallas_call(
        body, name=name, grid=(nb,),
        in_specs=[pl.BlockSpec((GDN_ROWS, GDN_IN_PAD), lambda i: (i, 0)),
                  pl.BlockSpec((SUBLANES, GDN_IN_PAD), lambda i: (jnp.maximum(i * hb - 1, 0), 0)),
                  pl.BlockSpec((SUBLANES, GDN_CONV_DIM), lambda i: (0, 0)),
                  vec(LANES), vec(LANES), vec(GDN_HEAD)],
        out_specs=[pl.BlockSpec((GDN_ROWS, GDN_V), lambda i: (i, 0)),
                   pl.BlockSpec((cpb, GDN_VH, GDN_HEAD, GDN_HEAD), lambda i: (i, 0, 0, 0)),
                   pl.BlockSpec((cpb, GDN_VH, CHUNK, CHUNK), lambda i: (i, 0, 0, 0))],
        out_shape=[jax.ShapeDtypeStruct((rows_total, GDN_V), MXU_DTYPE),
                   jax.ShapeDtypeStruct((rows_total // CHUNK, GDN_VH, GDN_HEAD, GDN_HEAD), F32),
                   jax.ShapeDtypeStruct((rows_total // CHUNK, GDN_VH, CHUNK, CHUNK), MXU_DTYPE)],
        scratch_shapes=[pltpu.VMEM((GDN_ROWS + SUBLANES, GDN_CONV_DIM), F32),
                        pltpu.VMEM((GDN_ROWS, GDN_CONV_DIM), F32),
                        pltpu.VMEM((GDN_ROWS, LANES), F32),
                        pltpu.VMEM((GDN_ROWS, LANES), F32),
                        pltpu.VMEM((GDN_VH, GDN_HEAD, GDN_HEAD), F32),
                        pltpu.VMEM((LANES, CHUNK), F32)],
        compiler_params=_cparams(("arbitrary",)),
    )(proj, proj, _pad_rows(conv_w), _gdn_lane_params(a_log), _gdn_lane_params(dt_bias), norm_w.reshape(1, -1))
    return mix, (states, tmats)


def gdn_bwd(proj, conv_w, a_log, dt_bias, norm_w, saved, dmix, *, name):
    states, tmats = saved
    rows_total = proj.shape[0]
    nb = rows_total // GDN_ROWS
    hb = GDN_ROWS // SUBLANES
    cpb = GDN_ROWS // CHUNK

    def body(p_ref, halo_ref, cw_ref, alog_ref, dtb_ref, nw_ref, st_ref, tm_ref, dm_ref,
             dp_ref, dcw_ref, dalog_ref, ddtb_ref, dnw_ref,
             ext_ref, qkv_ref, beta_ref, g_ref, ds_ref, gct_ref, dext_ref, dgc_ref, dgct_ref, dbeta_ref, pre_ref):
        i = pl.program_id(0)
        blk = nb - 1 - i

        @pl.when(i == 0)
        def _():
            ds_ref[...] = jnp.zeros_like(ds_ref)
            dext_ref[GDN_ROWS:, :] = jnp.zeros((SUBLANES, GDN_CONV_DIM), F32)
            for r in (dcw_ref, dalog_ref, ddtb_ref, dnw_ref):
                r[...] = jnp.zeros_like(r)

        _gdn_prologue(blk, p_ref, halo_ref, cw_ref, alog_ref, dtb_ref, ext_ref, qkv_ref, beta_ref, g_ref, pre_ref)
        ltri, utri, eye_l, eye_c = _tri(CHUNK), _tri(CHUNK, lower=False), _eye(LANES), _eye(CHUNK)
        causal = _iota((CHUNK, CHUNK), 1) <= _iota((CHUNK, CHUNK), 0)
        strict = _iota((CHUNK, CHUNK), 1) < _iota((CHUNK, CHUNK), 0)
        lane = _iota((CHUNK, LANES), 1)
        is_last = _iota((CHUNK, 1), 0) == CHUNK - 1
        nw = nw_ref[...]

        def chunk(cc, carry):
            c = cpb - 1 - cc
            rows = pl.ds(pl.multiple_of(c * CHUNK, CHUNK), CHUNK)
            g_c = g_ref[rows, :]
            gc = _sel(ltri,g_c)
            gct_ref[...] = _sel_nt(eye_l,gc)
            glast_row = _row(gc, CHUNK - 1)
            beta_c = beta_ref[rows, :]
            dgc_ref[...] = jnp.zeros_like(dgc_ref)
            dgct_ref[...] = jnp.zeros_like(dgct_ref)
            dbeta_ref[...] = jnp.zeros_like(dbeta_ref)
            for h0 in range(0, GDN_VH, GDN_GROUP):
                hs = list(range(h0, h0 + GDN_GROUP))
                hqs = list(range(h0 // 2, (h0 + GDN_GROUP) // 2))
                qs = {hq: qkv_ref[rows, hq * GDN_HEAD:(hq + 1) * GDN_HEAD] for hq in hqs}
                ks = {hq: qkv_ref[rows, GDN_K0 + hq * GDN_HEAD:GDN_K0 + (hq + 1) * GDN_HEAD] for hq in hqs}
                qs_c = {hq: _mx(qs[hq]) for hq in hqs}
                ks_c = {hq: _mx(ks[hq]) for hq in hqs}
                kks = {hq: _dot_nt(ks_c[hq], ks_c[hq]) for hq in hqs}
                qks = {hq: _dot_nt(qs_c[hq], ks_c[hq]) for hq in hqs}
                q = [qs[h // 2] for h in hs]
                k = [ks[h // 2] for h in hs]
                v = [qkv_ref[rows, GDN_V0 + h * GDN_HEAD:GDN_V0 + (h + 1) * GDN_HEAD] for h in hs]
                s = [st_ref[c, h] for h in hs]
                bcol = [_col(beta_c, h) for h in hs]
                f = _gdn_heads_fwd(q, k, v, [kks[h // 2] for h in hs], [qks[h // 2] for h in hs],
                                   [_col(gc, GDN_GL + h) for h in hs], [gct_ref[GDN_GL + h:GDN_GL + h + 1, :] for h in hs],
                                   [_col(glast_row, GDN_GL + h) for h in hs], bcol, s, causal, strict, eye_c,
                                   t=[tm_ref[c, h] for h in hs])
                do = []
                for i_h, h in enumerate(hs):
                    zc = slice(GDN_Z0 + h * GDN_HEAD, GDN_Z0 + (h + 1) * GDN_HEAD)
                    o = f["out"][i_h]
                    z = p_ref[rows, zc]
                    sz = _silu(z)
                    r = lax.rsqrt(jnp.mean(o * o, axis=-1, keepdims=True) + RMS_EPS)
                    on = o * r
                    dm = dm_ref[rows, h * GDN_HEAD:(h + 1) * GDN_HEAD]
                    dnw_ref[...] += jnp.sum(dm * on * sz, axis=0, keepdims=True)
                    d_on = dm * nw * sz
                    dp_ref[rows, zc] = (dm * on * nw * _dsilu(z)).astype(dp_ref.dtype)
                    do.append(r * (d_on - on * jnp.mean(d_on * on, axis=-1, keepdims=True)))
                ds_n = [ds_ref[h] for h in hs]
                do_c, dsn_c = _each(_mx, do), _each(_mx, ds_n)
                k_c = [ks_c[h // 2] for h in hs]
                dv1 = _each(_dot_tn, f["attn_c"], do_c)
                dv2 = _each(_dot, f["kt_c"], dsn_c)
                d_vnew = _each(lambda a_, b_: a_ + b_, dv1, dv2)
                dvn_c = _each(_mx, d_vnew)
                d_attn = _each(lambda do_, vn_: jnp.where(causal, _dot_nt(do_, vn_), 0.0), do_c, f["vn_c"])
                d_qd = _each(_dot_nt, do_c, f["s_c"])
                t1 = _each(_dot_tn, f["qd_c"], do_c)
                t2 = _each(_dot_tn, f["w_c"], dvn_c)
                for h, a_, cd_, dsn_, b_ in zip(hs, t1, f["cd"], ds_n, t2):
                    ds_ref[h] = a_ + cd_ * dsn_ - b_
                d_cd = _each(lambda s_, dsn_: jnp.sum(jnp.sum(s_ * dsn_, axis=1, keepdims=True), axis=0, keepdims=True), s, ds_n)
                d_kt = _each(_dot_nt, f["vn_c"], dsn_c)
                d_w = _each(lambda dv_, s_: -_dot_nt(dv_, s_), dvn_c, f["s_c"])
                d_rhs_u = _each(_dot_tn, f["t_c"], dvn_c)
                d_rhs_w = _each(_dot_tn, f["t_c"], d_w)
                m1 = _each(_dot_nt, d_rhs_u, f["u"])
                m2 = _each(_dot_nt, d_rhs_w, f["w_c"])
                da = _each(lambda a_, b_: -jnp.where(strict, a_ + b_, 0.0), m1, m2)
                dmm = _each(lambda a_, b_: a_ * b_, da, f["decay"])
                em = _each(lambda da_, a_, dat_, at_: da_ * a_ + dat_ * at_, da, f["a"], d_attn, f["attn"])
                dmm_c = _each(_mx, dmm)
                x1 = _each(_dot, dmm_c, k_c)
                d_kb = _each(lambda x_, drw_, e_: x_ + drw_ * e_, x1, d_rhs_w, f["egc"])
                dk1 = _each(_dot_tn, dmm_c, f["kb"])
                dpm = _each(lambda a_, b_: _mx(a_ * b_), d_attn, f["decay"])
                dq1 = _each(_dot, dpm, k_c)
                dq = _each(lambda x_, dqd_, e_: x_ + dqd_ * e_, dq1, d_qd, f["egc"])
                dk2 = _each(_dot_tn, dpm, [qs_c[h // 2] for h in hs])
                dk = _each(lambda a_, b_, dkb_, bc_, dkt_, et_: a_ + b_ + dkb_ * bc_ + dkt_ * et_,
                           dk1, dk2, d_kb, bcol, d_kt, f["etail"])
                for i_h, h in enumerate(hs):
                    tmp = jnp.sum(d_kt[i_h] * f["kt"][i_h], axis=1, keepdims=True)
                    d_gcol = (jnp.sum(em[i_h], axis=1, keepdims=True)
                              + jnp.sum(d_rhs_w[i_h] * f["rhs_w"][i_h], axis=1, keepdims=True)
                              + jnp.sum(d_qd[i_h] * f["qd"][i_h], axis=1, keepdims=True) - tmp)
                    d_glast = jnp.sum(tmp, axis=0, keepdims=True) + d_cd[i_h] * f["cd"][i_h]
                    d_gcol = jnp.where(is_last, d_gcol + d_glast, d_gcol)
                    d_beta = (jnp.sum(d_rhs_u[i_h] * v[i_h], axis=1, keepdims=True)
                              + jnp.sum(d_kb[i_h] * k[i_h], axis=1, keepdims=True))
                    dgc_ref[...] += jnp.where(lane == GDN_GL + h, d_gcol, 0.0)
                    dgct_ref[GDN_GL + h:GDN_GL + h + 1, :] = jnp.sum(em[i_h], axis=0, keepdims=True)
                    dbeta_ref[...] += jnp.where(lane == h, d_beta, 0.0)
                    dext_ref[rows, GDN_V0 + h * GDN_HEAD:GDN_V0 + (h + 1) * GDN_HEAD] = d_rhs_u[i_h] * bcol[i_h]
                for hq in hqs:
                    i0 = 2 * hq - h0
                    dext_ref[rows, hq * GDN_HEAD:(hq + 1) * GDN_HEAD] = dq[i0] + dq[i0 + 1]
                    dext_ref[rows, GDN_K0 + hq * GDN_HEAD:GDN_K0 + (hq + 1) * GDN_HEAD] = dk[i0] + dk[i0 + 1]
            d_gc = dgc_ref[...] - _sel_nt(eye_c,dgct_ref[...])
            dg = _sel(utri,d_gc)
            ba = p_ref[rows, GDN_BA0:GDN_BA0 + LANES]
            d_sp = dg * -jnp.exp(alog_ref[...])
            d_araw = d_sp * _sigmoid(ba + dtb_ref[...])
            d_araw = jnp.where((lane >= GDN_GL) & (lane < GDN_GL + GDN_VH), d_araw, 0.0)
            dalog_ref[...] += jnp.sum(dg * g_c, axis=0, keepdims=True)
            ddtb_ref[...] += jnp.sum(d_araw, axis=0, keepdims=True)
            d_braw = jnp.where(lane < GDN_VH, dbeta_ref[...] * beta_c * (1.0 - beta_c), 0.0)
            dp_ref[rows, GDN_BA0:GDN_BA0 + LANES] = (d_braw + d_araw).astype(dp_ref.dtype)
            return carry

        lax.fori_loop(0, cpb, chunk, 0)
        w = cw_ref[...]
        for hh in range(GDN_CONV_DIM // GDN_HEAD):
            cols = slice(hh * GDN_HEAD, (hh + 1) * GDN_HEAD)
            pre = pre_ref[:, cols]
            d_act = dext_ref[0:GDN_ROWS, cols]
            if hh < 2 * GDN_QKH:
                a = _silu(pre)
                r = lax.rsqrt(jnp.sum(a * a, axis=-1, keepdims=True) + L2_EPS)
                ah = a * r
                if hh < GDN_QKH:
                    d_act = d_act * GDN_SCALE
                d_act = r * (d_act - ah * jnp.sum(d_act * ah, axis=-1, keepdims=True))
            d_pre = d_act * _dsilu(pre)
            dext_ref[0:GDN_ROWS, cols] = d_pre
            du, dws = _conv_bwd_from_ext(dext_ref, ext_ref, w, GDN_CONV, GDN_ROWS, cols)
            for j in range(GDN_CONV):
                dcw_ref[j:j + 1, cols] += dws[j]
            dp_ref[:, cols] = du.astype(dp_ref.dtype)
            dext_ref[GDN_ROWS:, cols] = d_pre[0:SUBLANES, :]

    vec = lambda n: pl.BlockSpec((1, n), lambda i: (0, 0))
    outs = pl.pallas_call(
        body, name=name, grid=(nb,),
        in_specs=[pl.BlockSpec((GDN_ROWS, GDN_IN_PAD), lambda i: (nb - 1 - i, 0)),
                  pl.BlockSpec((SUBLANES, GDN_IN_PAD), lambda i: (jnp.maximum((nb - 1 - i) * hb - 1, 0), 0)),
                  pl.BlockSpec((SUBLANES, GDN_CONV_DIM), lambda i: (0, 0)),
                  vec(LANES), vec(LANES), vec(GDN_HEAD),
                  pl.BlockSpec((cpb, GDN_VH, GDN_HEAD, GDN_HEAD), lambda i: (nb - 1 - i, 0, 0, 0)),
                  pl.BlockSpec((cpb, GDN_VH, CHUNK, CHUNK), lambda i: (nb - 1 - i, 0, 0, 0)),
                  pl.BlockSpec((GDN_ROWS, GDN_V), lambda i: (nb - 1 - i, 0))],
        out_specs=[pl.BlockSpec((GDN_ROWS, GDN_IN_PAD), lambda i: (nb - 1 - i, 0)),
                   pl.BlockSpec((SUBLANES, GDN_CONV_DIM), lambda i: (0, 0)),
                   vec(LANES), vec(LANES), vec(GDN_HEAD)],
        out_shape=[jax.ShapeDtypeStruct((rows_total, GDN_IN_PAD), MXU_DTYPE),
                   jax.ShapeDtypeStruct((SUBLANES, GDN_CONV_DIM), F32),
                   jax.ShapeDtypeStruct((1, LANES), F32), jax.ShapeDtypeStruct((1, LANES), F32),
                   jax.ShapeDtypeStruct((1, GDN_HEAD), F32)],
        scratch_shapes=[pltpu.VMEM((GDN_ROWS + SUBLANES, GDN_CONV_DIM), F32),
                        pltpu.VMEM((GDN_ROWS, GDN_CONV_DIM), F32),
                        pltpu.VMEM((GDN_ROWS, LANES), F32),
                        pltpu.VMEM((GDN_ROWS, LANES), F32),
                        pltpu.VMEM((GDN_VH, GDN_HEAD, GDN_HEAD), F32),
                        pltpu.VMEM((LANES, CHUNK), F32),
                        pltpu.VMEM((GDN_ROWS + SUBLANES, GDN_CONV_DIM), F32),
                        pltpu.VMEM((CHUNK, LANES), F32),
                        pltpu.VMEM((LANES, CHUNK), F32),
                        pltpu.VMEM((CHUNK, LANES), F32),
                        pltpu.VMEM((GDN_ROWS, GDN_CONV_DIM), F32)],
        compiler_params=_cparams(("arbitrary",)),
    )(proj, proj, _pad_rows(conv_w), _gdn_lane_params(a_log), _gdn_lane_params(dt_bias), norm_w.reshape(1, -1),
      states, tmats, dmix)
    dproj, dcw, dalog, ddtb, dnw = outs
    return dproj, [dcw[:GDN_CONV], dalog[0, GDN_GL:GDN_GL + GDN_VH], ddtb[0, GDN_GL:GDN_GL + GDN_VH], dnw[0]]


def chip_exchange(src, *, scatter, name):
    piece_shape = src.shape[1:]

    def body(src_ref, out_ref, send_sems, recv_sems, local_sem):
        x, y, c = (lax.axis_index(a) for a in MESH_AXES)
        me = 2 * x + y

        def piece(j):
            return src_ref.at[j] if scatter else src_ref.at[c]

        local = pltpu.make_async_copy(piece(me), out_ref.at[me], local_sem)
        local.start()
        copies = []
        for k in range(1, N_SHARDS):
            px = 1 - x if k & 2 else x
            py = 1 - y if k & 1 else y
            cp = pltpu.make_async_remote_copy(
                src_ref=piece(2 * px + py), dst_ref=out_ref.at[me], send_sem=send_sems.at[k - 1],
                recv_sem=recv_sems.at[k - 1], device_id=(px, py, c), device_id_type=pl.DeviceIdType.MESH)
            cp.start()
            copies.append(cp)
        for cp in copies:
            cp.wait()
        local.wait()

    hbm = pl.BlockSpec(memory_space=pl.ANY)
    return pl.pallas_call(
        body, name=name, in_specs=[hbm], out_specs=hbm,
        out_shape=jax.ShapeDtypeStruct((N_SHARDS,) + tuple(piece_shape), src.dtype),
        scratch_shapes=[pltpu.SemaphoreType.DMA((N_SHARDS - 1,)), pltpu.SemaphoreType.DMA((N_SHARDS - 1,)),
                        pltpu.SemaphoreType.DMA],
    )(src)


def pair_exchange(src, *, add, name):
    lead, rows, cols = src.shape
    tr = _pick(rows, (512, 256))
    nblk = rows // tr
    n_steps = nblk if add else lead * nblk

    def body(c_ref, *refs):
        if add:
            mine_ref, send_ref, o_ref, recv_ref, send_sems, recv_sems, credit = refs
        else:
            send_ref, o_ref, recv_ref, send_sems, recv_sems, credit = refs
        step = pl.program_id(0) * nblk + pl.program_id(1)
        slot = step % 2
        sibling = (lax.axis_index("x"), lax.axis_index("y"), 1 - lax.axis_index("c"))

        @pl.when(step >= 2)
        def _():
            pl.semaphore_wait(credit, 1)

        cp = pltpu.make_async_remote_copy(
            src_ref=send_ref, dst_ref=recv_ref.at[slot], send_sem=send_sems.at[slot], recv_sem=recv_sems.at[slot],
            device_id=sibling, device_id_type=pl.DeviceIdType.MESH)
        cp.start()
        cp.wait_recv()
        if add:
            o_ref[...] = mine_ref[...] + recv_ref[slot]
        else:
            o_ref[c_ref[0]] = send_ref[...]
            o_ref[1 - c_ref[0]] = recv_ref[slot]
        cp.wait_send()

        @pl.when(step + 2 < n_steps)
        def _():
            pl.semaphore_signal(credit, 1, device_id=sibling, device_id_type=pl.DeviceIdType.MESH)

    flat = src.reshape(lead * rows, cols)
    if add:
        in_specs = [pl.BlockSpec((tr, cols), lambda s, i, c_ref: (c_ref[0] * nblk + i, 0)),
                    pl.BlockSpec((tr, cols), lambda s, i, c_ref: ((1 - c_ref[0]) * nblk + i, 0))]
        out_specs = pl.BlockSpec((tr, cols), lambda s, i, c_ref: (i, 0))
        out_shape = jax.ShapeDtypeStruct((rows, cols), src.dtype)
        grid, args = (1, nblk), (flat, flat)
    else:
        in_specs = [pl.BlockSpec((tr, cols), lambda s, i, c_ref: (s * nblk + i, 0))]
        out_specs = pl.BlockSpec((2, tr, cols), lambda s, i, c_ref: (s, i, 0))
        out_shape = jax.ShapeDtypeStruct((lead * 2, rows, cols), src.dtype)
        grid, args = (lead, nblk), (flat,)
    out = pl.pallas_call(
        body, name=name, out_shape=out_shape,
        grid_spec=pltpu.PrefetchScalarGridSpec(
            num_scalar_prefetch=1, grid=grid, in_specs=in_specs, out_specs=out_specs,
            scratch_shapes=[pltpu.VMEM((2, tr, cols), src.dtype), pltpu.SemaphoreType.DMA((2,)),
                            pltpu.SemaphoreType.DMA((2,)), pltpu.SemaphoreType.REGULAR]),
        compiler_params=_cparams(("arbitrary", "arbitrary")),
    )(lax.axis_index("c").astype(jnp.int32).reshape(1), *args)
    return out if add else out.reshape(lead, 2, rows, cols)


def sum_slots(buf, *, name):
    n, rows, cols = buf.shape
    tr = _pick(rows, (512, 256, 128))

    def body(b_ref, o_ref):
        acc = b_ref[0]
        for j in range(1, n):
            acc = acc + b_ref[j]
        o_ref[...] = acc

    return pl.pallas_call(
        body, name=name, grid=(rows // tr,), in_specs=[pl.BlockSpec((n, tr, cols), lambda i: (0, i, 0))],
        out_specs=pl.BlockSpec((tr, cols), lambda i: (i, 0)), out_shape=jax.ShapeDtypeStruct((rows, cols), F32),
        compiler_params=_cparams(("parallel",)),
    )(buf)


def adamw(w, g, m, v, *, name):
    shape = w.shape
    cols = shape[-1]
    rows = _size(shape) // cols
    w, g, m, v = (t.reshape(rows, cols) for t in (w, g, m, v))
    tr = 256 if rows % 256 == 0 else rows

    def body(w_ref, g_ref, m_ref, v_ref, d_ref, mo_ref, vo_ref):
        gv = g_ref[...]
        mn = ADAM_B1 * m_ref[...] + (1.0 - ADAM_B1) * gv
        vn = ADAM_B2 * v_ref[...] + (1.0 - ADAM_B2) * (gv * gv)
        m_hat = mn / (1.0 - ADAM_B1 ** ADAM_STEP)
        v_hat = vn / (1.0 - ADAM_B2 ** ADAM_STEP)
        d_ref[...] = -ADAM_LR * (m_hat / (jnp.sqrt(v_hat) + ADAM_EPS) + ADAM_WD * w_ref[...])
        mo_ref[...] = mn
        vo_ref[...] = vn

    blk = pl.BlockSpec((tr, cols), lambda i: (i, 0))
    shp = jax.ShapeDtypeStruct((rows, cols), F32)
    outs = pl.pallas_call(
        body, name=name, grid=(rows // tr,), in_specs=[blk] * 4, out_specs=[blk] * 3, out_shape=[shp] * 3,
        compiler_params=_cparams(("parallel",)),
    )(w, g, m, v)
    return [o.reshape(shape) for o in outs]


N_SHARDS = 4
FLAT_COLS = 1024
W_SPECS = (
    ("gdn_w_in", (2, 1024, 6176), 2), ("gdn_conv_w", (2, 4, 4096), 2), ("gdn_a_log", (2, 16), None),
    ("gdn_dt_bias", (2, 16), None), ("gdn_norm_w", (2, 128), None), ("gdn_w_out", (2, 2048, 1024), 1),
    ("sc_w_in", (1, 1024, 8192), 2), ("sc_conv_w", (1, 3, 2048), 2), ("sc_w_out", (1, 2048, 1024), 1),
    ("ssd_w_in", (1, 1024, 5152), 2), ("ssd_conv_w", (1, 4, 3072), 2), ("ssd_conv_b", (1, 3072), 1),
    ("ssd_a_log", (1, 32), None), ("ssd_dt_bias", (1, 32), None), ("ssd_d_skip", (1, 32), None),
    ("ssd_norm_w", (1, 2048), 1), ("ssd_w_out", (1, 2048, 1024), 1), ("ln_g", (4, 1024), None), ("ln_b", (4, 1024), None),
)


def _local_shape(shape, axis):
    return shape if axis is None else tuple(d // N_SHARDS if i == axis else d for i, d in enumerate(shape))


def _size(shape):
    n = 1
    for d in shape:
        n *= d
    return n


PIECE_ROWS = 16


def _piece_rows(shape, axis):
    return -(-_size(_local_shape(shape, axis)) // (FLAT_COLS * PIECE_ROWS)) * PIECE_ROWS


def _flat_rows(specs):
    return -(-sum(_piece_rows(s, a) for _, s, a in specs) // 512) * 512


FLAT_ROWS = _flat_rows(W_SPECS)
FLAT_HALF = FLAT_ROWS // 2


def _pack(pieces, specs=W_SPECS, dtype=F32):
    blocks, used = [], 0
    for p, (_, shape, axis) in zip(pieces, specs):
        rows = _piece_rows(shape, axis)
        flat = p.reshape(-1).astype(dtype)
        if flat.shape[0] < rows * FLAT_COLS:
            flat = jnp.pad(flat, (0, rows * FLAT_COLS - flat.shape[0]))
        blocks.append(flat.reshape(rows, FLAT_COLS))
        used += rows
    blocks.append(jnp.zeros((_flat_rows(specs) - used, FLAT_COLS), dtype))
    return jnp.concatenate(blocks, axis=0)


def _unpack(flat, specs=W_SPECS):
    out, off = [], 0
    for _, shape, axis in specs:
        ls = _local_shape(shape, axis)
        rows = _piece_rows(shape, axis)
        out.append(flat[off:off + rows].reshape(-1)[:_size(ls)].reshape(ls))
        off += rows
    return out


def _shard_of(full, axis, s):
    if axis is None:
        return full
    n = full.shape[axis] // N_SHARDS
    return lax.slice_in_dim(full, s * n, (s + 1) * n, axis=axis)


def _adamw_all(weights, grads_flat, moms, vels):
    grads = _unpack(grads_flat)
    steps = [adamw(w, g, m, v, name="adamw") for w, g, m, v in zip(weights, grads, moms, vels)]
    return grads, [s[0] for s in steps], [s[1] for s in steps], [s[2] for s in steps]


SPLIT_ROWS = 128


def shard_split(w, n_real, *, name):
    rows, n_pad = w.shape
    ns = n_real // N_SHARDS

    def body(w_ref, o_ref):
        for s in range(N_SHARDS):
            o_ref[s] = w_ref[:, s * ns:(s + 1) * ns]

    return pl.pallas_call(
        body, name=name, grid=(rows // SPLIT_ROWS,),
        in_specs=[pl.BlockSpec((SPLIT_ROWS, n_pad), lambda i: (i, 0))],
        out_specs=pl.BlockSpec((N_SHARDS, SPLIT_ROWS, ns), lambda i: (0, i, 0)),
        out_shape=jax.ShapeDtypeStruct((N_SHARDS, rows, ns), F32), compiler_params=_cparams(("parallel",)),
    )(w)


def shard_merge(pieces, n_pad, *, name):
    _, rows, ns = pieces.shape
    n_real = ns * N_SHARDS

    def body(p_ref, o_ref):
        for s in range(N_SHARDS):
            o_ref[:, s * ns:(s + 1) * ns] = p_ref[s].astype(o_ref.dtype)
        if n_pad > n_real:
            o_ref[:, n_real:] = jnp.zeros((SPLIT_ROWS, n_pad - n_real), o_ref.dtype)

    return pl.pallas_call(
        body, name=name, grid=(rows // SPLIT_ROWS,),
        in_specs=[pl.BlockSpec((N_SHARDS, SPLIT_ROWS, ns), lambda i: (0, i, 0))],
        out_specs=pl.BlockSpec((SPLIT_ROWS, n_pad), lambda i: (i, 0)),
        out_shape=jax.ShapeDtypeStruct((rows, n_pad), MXU_DTYPE), compiler_params=_cparams(("parallel",)),
    )(pieces)


def _reduce_scatter(full_grads):
    def shard(g, spec, s):
        _, shape, axis = spec
        return g[:, s] if g.ndim == len(shape) + 1 else _shard_of(g, axis, s)

    by_shard = jnp.stack([_pack([shard(g, spec, s) for g, spec in zip(full_grads, W_SPECS)])
                          for s in range(N_SHARDS)])
    by_half = by_shard.reshape(N_SHARDS, 2, FLAT_HALF, FLAT_COLS).transpose(1, 0, 2, 3)
    by_half = by_half.reshape(2, N_SHARDS * FLAT_HALF, FLAT_COLS)
    pair_sum = pair_exchange(by_half, add=True, name="rs_pair")
    chips = chip_exchange(pair_sum.reshape(N_SHARDS, FLAT_HALF, FLAT_COLS), scatter=True, name="rs_chips")
    half = sum_slots(chips, name="rs_chip_sum")
    return pair_exchange(half[None], add=False, name="rs_halves").reshape(FLAT_ROWS, FLAT_COLS)


def _gather_weights(local_weights):
    def gather(idx, dtype, tag):
        specs = [W_SPECS[i] for i in idx]
        rows = _flat_rows(specs)
        flat = _pack([local_weights[i] for i in idx], specs, dtype)
        halves = chip_exchange(flat.reshape(2, rows // 2, FLAT_COLS), scatter=False, name="gather_chips_" + tag)
        both = pair_exchange(halves, add=False, name="gather_pair_" + tag).reshape(N_SHARDS, rows, FLAT_COLS)
        return [dict(zip(idx, _unpack(both[s], specs))) for s in range(N_SHARDS)]

    matrices = [i for i, (n, _, _) in enumerate(W_SPECS) if n in MXU_WEIGHTS]
    vectors = [i for i, (n, _, a) in enumerate(W_SPECS) if n not in MXU_WEIGHTS and a is not None]
    per_shard = [{**m, **v} for m, v in zip(gather(matrices, MXU_DTYPE, "mxu"), gather(vectors, F32, "f32"))]
    full = []
    for i, (wname, shape, axis) in enumerate(W_SPECS):
        if axis is None:
            full.append(local_weights[i])
        elif wname in W_IN_PAD:
            pieces = jnp.stack([per_shard[s][i] for s in range(N_SHARDS)], axis=1)
            full.append([shard_merge(pieces[j], W_IN_PAD[wname], name="merge_" + wname) for j in range(shape[0])])
        else:
            full.append(jnp.concatenate([per_shard[s][i] for s in range(N_SHARDS)], axis=axis))
    return full


W_IN_PAD = {"gdn_w_in": GDN_IN_PAD, "sc_w_in": SC_IN, "ssd_w_in": SSD_IN_PAD}
MXU_WEIGHTS = ("gdn_w_in", "gdn_w_out", "sc_w_in", "sc_w_out", "ssd_w_in", "ssd_w_out")


def kernel(x, gdn_w_in, gdn_conv_w, gdn_a_log, gdn_dt_bias, gdn_norm_w, gdn_w_out, sc_w_in, sc_conv_w, sc_w_out, ssd_w_in, ssd_conv_w, ssd_conv_b, ssd_a_log, ssd_dt_bias, ssd_d_skip, ssd_norm_w, ssd_w_out, ln_g, ln_b, loss_target, m_gdn_w_in, m_gdn_conv_w, m_gdn_a_log, m_gdn_dt_bias, m_gdn_norm_w, m_gdn_w_out, m_sc_w_in, m_sc_conv_w, m_sc_w_out, m_ssd_w_in, m_ssd_conv_w, m_ssd_conv_b, m_ssd_a_log, m_ssd_dt_bias, m_ssd_d_skip, m_ssd_norm_w, m_ssd_w_out, m_ln_g, m_ln_b, v_gdn_w_in, v_gdn_conv_w, v_gdn_a_log, v_gdn_dt_bias, v_gdn_norm_w, v_gdn_w_out, v_sc_w_in, v_sc_conv_w, v_sc_w_out, v_ssd_w_in, v_ssd_conv_w, v_ssd_conv_b, v_ssd_a_log, v_ssd_dt_bias, v_ssd_d_skip, v_ssd_norm_w, v_ssd_w_out, v_ln_g, v_ln_b):
    weights = [gdn_w_in, gdn_conv_w, gdn_a_log, gdn_dt_bias, gdn_norm_w, gdn_w_out, sc_w_in, sc_conv_w, sc_w_out,
               ssd_w_in, ssd_conv_w, ssd_conv_b, ssd_a_log, ssd_dt_bias, ssd_d_skip, ssd_norm_w, ssd_w_out, ln_g, ln_b]
    moms = [m_gdn_w_in, m_gdn_conv_w, m_gdn_a_log, m_gdn_dt_bias, m_gdn_norm_w, m_gdn_w_out, m_sc_w_in, m_sc_conv_w,
            m_sc_w_out, m_ssd_w_in, m_ssd_conv_w, m_ssd_conv_b, m_ssd_a_log, m_ssd_dt_bias, m_ssd_d_skip, m_ssd_norm_w,
            m_ssd_w_out, m_ln_g, m_ln_b]
    vels = [v_gdn_w_in, v_gdn_conv_w, v_gdn_a_log, v_gdn_dt_bias, v_gdn_norm_w, v_gdn_w_out, v_sc_w_in, v_sc_conv_w,
            v_sc_w_out, v_ssd_w_in, v_ssd_conv_w, v_ssd_conv_b, v_ssd_a_log, v_ssd_dt_bias, v_ssd_d_skip, v_ssd_norm_w,
            v_ssd_w_out, v_ln_g, v_ln_b]
    full = dict(zip([n for n, _, _ in W_SPECS], _gather_weights(weights)))
    x0 = x[0]
    target = loss_target[0]

    layers = (("gdn", 0, GDN_IN_PAD, GDN_IN), ("sc", 0, SC_IN, SC_IN), ("ssd", 0, SSD_IN_PAD, SSD_IN), ("gdn", 1, GDN_IN_PAD, GDN_IN))

    def params(kind, j):
        if kind == "gdn":
            return [full["gdn_conv_w"][j], full["gdn_a_log"][j], full["gdn_dt_bias"][j], full["gdn_norm_w"][j]]
        if kind == "sc":
            return [full["sc_conv_w"][j]]
        return [full["ssd_conv_w"][j], full["ssd_conv_b"][j], full["ssd_a_log"][j], full["ssd_dt_bias"][j],
                full["ssd_d_skip"][j], full["ssd_norm_w"][j]]

    xs, saved = [x0], []
    for i, (kind, j, n_pad, _) in enumerate(layers):
        w_in = full[kind + "_w_in"][j]
        w_out = full[kind + "_w_out"][j].astype(MXU_DTYPE)
        proj = matmul(xs[i], w_in, name=kind + "_proj")
        if kind == "gdn":
            mix, states = gdn_fwd(proj, *params(kind, j), name="gdn_fwd")
        elif kind == "sc":
            mix, states = sc_fwd(proj, *params(kind, j), name="sc_fwd"), None
        else:
            mix, states = ssd_fwd(proj, *params(kind, j), name="ssd_fwd")
        y = matmul(mix, w_out, name=kind + "_out")
        saved.append((w_in, w_out, proj, mix, states, y))
        if i + 1 < DEPTH:
            xs.append(ln_fwd(xs[i], y, full["ln_g"][i], full["ln_b"][i], name="ln_fwd"))

    grads = {n: [None] * s[0] for n, s, _ in W_SPECS}
    dr, dg, db, loss_rows = ln_bwd(xs[DEPTH - 1], saved[DEPTH - 1][5], full["ln_g"][DEPTH - 1], b=full["ln_b"][DEPTH - 1],
                                   target=target, name="ln_bwd_loss")
    dx = None
    for i in reversed(range(DEPTH)):
        kind, j, _, n_in = layers[i]
        w_in, w_out, proj, mix, states, _ = saved[i]
        grads["ln_g"][i], grads["ln_b"][i] = dg[0], db[0]
        dmix = matmul(dr, w_out, tb=True, name=kind + "_dmix")
        grads[kind + "_w_out"][j] = matmul(mix, dr, ta=True, name=kind + "_dw_out")
        if kind == "gdn":
            dproj, (dcw, dalog, ddtb, dnw) = gdn_bwd(proj, *params(kind, j), states, dmix, name="gdn_bwd")
            grads["gdn_conv_w"][j], grads["gdn_a_log"][j], grads["gdn_dt_bias"][j], grads["gdn_norm_w"][j] = dcw, dalog, ddtb, dnw
        elif kind == "sc":
            dproj, dcw = sc_bwd(proj, *params(kind, j), dmix, name="sc_bwd")
            grads["sc_conv_w"][j] = dcw[:SC_CONV]
        else:
            dproj, (dcw, dcb, dalog, ddtb, ddsk, dnw) = ssd_bwd(proj, *params(kind, j), states, dmix, name="ssd_bwd")
            grads["ssd_conv_w"][j], grads["ssd_conv_b"][j], grads["ssd_a_log"][j] = dcw, dcb, dalog
            grads["ssd_dt_bias"][j], grads["ssd_d_skip"][j], grads["ssd_norm_w"][j] = ddtb, ddsk, dnw
        grads[kind + "_w_in"][j] = shard_split(matmul(xs[i], dproj, ta=True, name=kind + "_dw_in"), n_in, name="split_" + kind)
        dx = matmul(dproj, w_in, tb=True, add=dr, add_scale=ALPHA, name=kind + "_dx")
        if i > 0:
            dr, dg, db = ln_bwd(xs[i - 1], saved[i - 1][5], full["ln_g"][i - 1], dx, name="ln_bwd")

    full_grads = [jnp.stack(grads[n]) for n, _, _ in W_SPECS]
    grads_flat = _reduce_scatter(full_grads)
    g_out, d_out, m_out, v_out = _adamw_all(weights, grads_flat, moms, vels)
    loss = lax.psum(loss_rows[0, 0], MESH_AXES)
    return (loss, dx[None], *g_out, *d_out, *m_out, *v_out)
```

```python
import functools

import jax
import jax.numpy as jnp
from jax import lax
from jax.experimental import pallas as pl
from jax.experimental.pallas import tpu as pltpu

F32 = jnp.float32
MXU_DTYPE = jnp.bfloat16

D_MODEL = 1024
DEPTH = 4
D_INNER = 2048
CHUNK = 64
LANES = 128
SUBLANES = 8
VMEM_LIMIT = 56 * 1024 * 1024

GDN_HEAD = 128
GDN_VH = 16
GDN_QKH = 8
GDN_QK = 1024
GDN_V = 2048
GDN_CONV = 4
GDN_CONV_DIM = 4096
GDN_IN = 6176
GDN_IN_PAD = 6272

SC_W = 2048
SC_CONV = 3
SC_IN = 8192

SSD_P = 64
SSD_H = 32
SSD_G = 4
SSD_S = 128
SSD_CONV = 4
SSD_CONV_DIM = 3072
SSD_IN = 5152
SSD_IN_PAD = 5376

ALPHA = (2 * DEPTH) ** 0.25
RMS_EPS = 1e-6
LN_EPS = 1e-5
L2_EPS = 1e-6

ADAM_LR = 0.001
ADAM_B1 = 0.9
ADAM_B2 = 0.999
ADAM_EPS = 1e-08
ADAM_WD = 0.01
ADAM_STEP = 10

MESH_AXES = ("x", "y", "c")


def _cparams(sem):
    return pltpu.CompilerParams(dimension_semantics=sem, vmem_limit_bytes=VMEM_LIMIT)


def _pick(n, prefs):
    for p in prefs:
        if n % p == 0:
            return p
    return n


def _dot(a, b, dims=(((1,), (0,)), ((), ()))):
    return lax.dot_general(a.astype(MXU_DTYPE), b.astype(MXU_DTYPE), dims, preferred_element_type=F32)


def _dot_nt(a, b):
    return _dot(a, b, (((1,), (1,)), ((), ())))


def _dot_tn(a, b):
    return _dot(a, b, (((0,), (0,)), ((), ())))


NN = (((1,), (0,)), ((), ()))
NT = (((1,), (1,)), ((), ()))
TN = (((0,), (0,)), ((), ()))


def _mxu(a, b, dims):
    return lax.dot_general(a, b, dims, preferred_element_type=F32)


def _split(x, pieces):
    out, r = [], x
    for i in range(pieces):
        p = r.astype(jnp.bfloat16)
        out.append(p)
        if i + 1 < pieces:
            r = r - p.astype(F32)
    return out


def _sel(m, x, dims=NN):
    mb = m.astype(jnp.bfloat16)
    x1, x2, x3 = _split(x, 3)
    return (_mxu(mb, x3, dims) + _mxu(mb, x2, dims)) + _mxu(mb, x1, dims)


def _sel_nt(m, x):
    return _sel(m, x, NT)


def _xsel(x, m, dims=NN):
    mb = m.astype(jnp.bfloat16)
    x1, x2, x3 = _split(x, 3)
    return (_mxu(x3, mb, dims) + _mxu(x2, mb, dims)) + _mxu(x1, mb, dims)


def _xsel_nt(x, m):
    return _xsel(x, m, NT)


def _iota(shape, dim):
    return lax.broadcasted_iota(jnp.int32, shape, dim)


def _sigmoid(x):
    return 0.5 * jnp.tanh(0.5 * x) + 0.5


def _silu(x):
    return x * _sigmoid(x)


def _dsilu(x):
    s = _sigmoid(x)
    return s * (1.0 + x * (1.0 - s))


def _softplus(x):
    return jnp.maximum(x, 0.0) + jnp.log(1.0 + jnp.exp(-jnp.abs(x)))


def matmul(a, b, *, ta=False, tb=False, add=None, add_scale=1.0, name):
    if ta:
        kdim, m = a.shape
    else:
        m, kdim = a.shape
    n = b.shape[0] if tb else b.shape[1]
    assert (b.shape[1] if tb else b.shape[0]) == kdim
    tm = _pick(m, (1024, 896, 768, 512)) if ta else _pick(m, (2048, 1024, 512, 256, 128))
    tn = _pick(n, (1024, 896, 768, 512, 256, 128))
    tk = _pick(kdim, (1024, 512, 256)) if ta else _pick(kdim, (1024, 896, 768, 512))
    nk = kdim // tk
    dims = (((0 if ta else 1,), (1 if tb else 0,)), ((), ()))

    def body(a_ref, b_ref, *rest):
        o_ref = rest[-1]
        k = pl.program_id(2)
        part = _dot(a_ref[...], b_ref[...], dims)

        @pl.when(k == 0)
        def _():
            o_ref[...] = part if add is None else part + add_scale * rest[0][...]

        @pl.when(k > 0)
        def _():
            o_ref[...] += part

    a_spec = pl.BlockSpec((tk, tm), lambda i, j, k: (k, i)) if ta else pl.BlockSpec((tm, tk), lambda i, j, k: (i, k))
    b_spec = pl.BlockSpec((tn, tk), lambda i, j, k: (j, k)) if tb else pl.BlockSpec((tk, tn), lambda i, j, k: (k, j))
    o_spec = pl.BlockSpec((tm, tn), lambda i, j, k: (i, j))
    in_specs = [a_spec, b_spec] + ([] if add is None else [o_spec])
    args = (a, b) + (() if add is None else (add,))
    return pl.pallas_call(
        body, name=name, grid=(m // tm, n // tn, nk), in_specs=in_specs, out_specs=o_spec,
        out_shape=jax.ShapeDtypeStruct((m, n), F32),
        compiler_params=_cparams(("parallel", "parallel", "arbitrary")),
    )(*args)


LN_ROWS = 512


def _ln_stats(x, y):
    r = ALPHA * x + y
    mu = jnp.mean(r, axis=-1, keepdims=True)
    rc = r - mu
    var = jnp.mean(rc * rc, axis=-1, keepdims=True)
    rstd = lax.rsqrt(var + LN_EPS)
    return rc * rstd, rstd


def ln_fwd(x, y, g, b, *, name):
    rows, d = x.shape

    def body(x_ref, y_ref, g_ref, b_ref, o_ref):
        xhat, _ = _ln_stats(x_ref[...], y_ref[...])
        o_ref[...] = xhat * g_ref[...] + b_ref[...]

    blk = pl.BlockSpec((LN_ROWS, d), lambda i: (i, 0))
    vec = pl.BlockSpec((1, d), lambda i: (0, 0))
    return pl.pallas_call(
        body, name=name, grid=(rows // LN_ROWS,), in_specs=[blk, blk, vec, vec], out_specs=blk,
        out_shape=jax.ShapeDtypeStruct((rows, d), F32), compiler_params=_cparams(("parallel",)),
    )(x, y, g.reshape(1, d), b.reshape(1, d))


def ln_bwd(x, y, g, dxn=None, *, b=None, target=None, name):
    rows, d = x.shape
    final = target is not None

    def body(x_ref, y_ref, g_ref, *rest):
        if final:
            b_ref, t_ref, dr_ref, dg_ref, db_ref, loss_ref = rest
        else:
            dxn_ref, dr_ref, dg_ref, db_ref = rest
        i = pl.program_id(0)
        xhat, rstd = _ln_stats(x_ref[...], y_ref[...])
        gv = g_ref[...]
        if final:
            err = xhat * gv + b_ref[...] - t_ref[...]
            dxn_v = err * (1.0 / d)
            part = 0.5 * jnp.sum(jnp.mean(err * err, axis=-1, keepdims=True), axis=0, keepdims=True)
        else:
            dxn_v = dxn_ref[...]
        dxh = dxn_v * gv
        m1 = jnp.mean(dxh, axis=-1, keepdims=True)
        m2 = jnp.mean(dxh * xhat, axis=-1, keepdims=True)
        dr_ref[...] = rstd * (dxh - m1 - xhat * m2)

        @pl.when(i == 0)
        def _():
            dg_ref[...] = jnp.zeros_like(dg_ref)
            db_ref[...] = jnp.zeros_like(db_ref)
            if final:
                loss_ref[...] = jnp.zeros_like(loss_ref)

        dg_ref[...] += jnp.sum(dxn_v * xhat, axis=0, keepdims=True)
        db_ref[...] += jnp.sum(dxn_v, axis=0, keepdims=True)
        if final:
            loss_ref[...] += jnp.broadcast_to(part, loss_ref.shape)

    blk = pl.BlockSpec((LN_ROWS, d), lambda i: (i, 0))
    vec = pl.BlockSpec((1, d), lambda i: (0, 0))
    lvec = pl.BlockSpec((1, LANES), lambda i: (0, 0))
    out_shape = [jax.ShapeDtypeStruct((rows, d), F32), jax.ShapeDtypeStruct((1, d), F32), jax.ShapeDtypeStruct((1, d), F32)]
    out_specs = [blk, vec, vec]
    if final:
        in_specs = [blk, blk, vec, vec, blk]
        args = (x, y, g.reshape(1, d), b.reshape(1, d), target)
        out_shape.append(jax.ShapeDtypeStruct((1, LANES), F32))
        out_specs.append(lvec)
    else:
        in_specs = [blk, blk, vec, blk]
        args = (x, y, g.reshape(1, d), dxn)
    return pl.pallas_call(
        body, name=name, grid=(rows // LN_ROWS,), in_specs=in_specs, out_specs=out_specs, out_shape=out_shape,
        compiler_params=_cparams(("arbitrary",)),
    )(*args)


def _rows_from(ref, off, rows, cols=slice(None)):
    r = off % SUBLANES
    if r == 0:
        return ref[off:off + rows, cols]
    window = ref[off - r:off - r + rows + SUBLANES, cols]
    return pltpu.roll(window, rows + SUBLANES - r, axis=0)[:rows]


def _conv_from_ext(ext_ref, w, width, rows, cols=slice(None)):
    out = None
    for j in range(width):
        term = _rows_from(ext_ref, SUBLANES - (width - 1) + j, rows, cols) * w[j:j + 1, cols]
        out = term if out is None else out + term
    return out


def _conv_dgrad_from_ext(dext_ref, w, width, rows, cols):
    out = None
    for j in range(width):
        term = _rows_from(dext_ref, (width - 1) - j, rows, cols) * w[j:j + 1, cols]
        out = term if out is None else out + term
    return out


def _conv_bwd_from_ext(dext_ref, u, w, width, rows, cols):
    du, dws = None, []
    for j in range(width):
        shifted = _rows_from(dext_ref, (width - 1) - j, rows, cols)
        term = shifted * w[j:j + 1, cols]
        du = term if du is None else du + term
        dws.append(jnp.sum(shifted * u, axis=0, keepdims=True))
    return du, dws


CONV_COLS = 256


SC_ROWS = 128


def sc_fwd(proj, conv_w, *, name):
    rows = proj.shape[0]
    nb = rows // SC_ROWS
    hb = SC_ROWS // SUBLANES

    def body(p_ref, halo_ref, w_ref, o_ref, ext_ref):
        i = pl.program_id(0)
        w = w_ref[...]
        for c0 in range(0, SC_W, CONV_COLS):
            cols, bc, cc, zc = (slice(k * SC_W + c0, k * SC_W + c0 + CONV_COLS) for k in range(4))
            ext_ref[0:SUBLANES, cols] = jnp.where(i == 0, 0.0, halo_ref[:, cc] * halo_ref[:, cols])
            ext_ref[SUBLANES:, cols] = p_ref[:, cc] * p_ref[:, cols]
            cv = _conv_from_ext(ext_ref, w, SC_CONV, SC_ROWS, cols)
            o_ref[:, cols] = (p_ref[:, bc] * cv * _silu(p_ref[:, zc])).astype(o_ref.dtype)

    return pl.pallas_call(
        body, name=name, grid=(nb,),
        in_specs=[pl.BlockSpec((SC_ROWS, SC_IN), lambda i: (i, 0)),
                  pl.BlockSpec((SUBLANES, SC_IN), lambda i: (jnp.maximum(i * hb - 1, 0), 0)),
                  pl.BlockSpec((SUBLANES, SC_W), lambda i: (0, 0))],
        out_specs=pl.BlockSpec((SC_ROWS, SC_W), lambda i: (i, 0)),
        out_shape=jax.ShapeDtypeStruct((rows, SC_W), MXU_DTYPE),
        scratch_shapes=[pltpu.VMEM((SC_ROWS + SUBLANES, SC_W), F32)],
        compiler_params=_cparams(("parallel",)),
    )(proj, proj, _pad_rows(conv_w))


def sc_bwd(proj, conv_w, dmix, *, name):
    rows = proj.shape[0]
    nb = rows // SC_ROWS
    hb = SC_ROWS // SUBLANES

    def body(p_ref, halo_ref, w_ref, dm_ref, dp_ref, dw_ref, ext_ref, dext_ref):
        i = pl.program_id(0)
        blk = nb - 1 - i
        w = w_ref[...]

        @pl.when(i == 0)
        def _():
            dext_ref[SC_ROWS:, :] = jnp.zeros((SUBLANES, SC_W), F32)
            dw_ref[...] = jnp.zeros_like(dw_ref)

        for c0 in range(0, SC_W, CONV_COLS):
            cols, bc, cc, zc = (slice(k * SC_W + c0, k * SC_W + c0 + CONV_COLS) for k in range(4))
            h, bg, cg, z = p_ref[:, cols], p_ref[:, bc], p_ref[:, cc], p_ref[:, zc]
            ext_ref[0:SUBLANES, cols] = jnp.where(blk == 0, 0.0, halo_ref[:, cc] * halo_ref[:, cols])
            ext_ref[SUBLANES:, cols] = cg * h
            taps = [_rows_from(ext_ref, SUBLANES - (SC_CONV - 1) + j, SC_ROWS, cols) for j in range(SC_CONV)]
            cv = None
            for j in range(SC_CONV):
                term = taps[j] * w[j:j + 1, cols]
                cv = term if cv is None else cv + term
            dm = dm_ref[:, cols]
            dy = dm * _silu(z)
            dp_ref[:, zc] = (dm * bg * cv * _dsilu(z)).astype(dp_ref.dtype)
            dp_ref[:, bc] = (dy * cv).astype(dp_ref.dtype)
            dcv = dy * bg
            dext_ref[0:SC_ROWS, cols] = dcv
            du = _conv_dgrad_from_ext(dext_ref, w, SC_CONV, SC_ROWS, cols)
            dp_ref[:, cols] = (du * cg).astype(dp_ref.dtype)
            dp_ref[:, cc] = (du * h).astype(dp_ref.dtype)
            for j in range(SC_CONV):
                dw_ref[j:j + 1, cols] += jnp.sum(taps[j] * dcv, axis=0, keepdims=True)
            dext_ref[SC_ROWS:, cols] = dcv[0:SUBLANES, :]

    return pl.pallas_call(
        body, name=name, grid=(nb,),
        in_specs=[pl.BlockSpec((SC_ROWS, SC_IN), lambda i: (nb - 1 - i, 0)),
                  pl.BlockSpec((SUBLANES, SC_IN), lambda i: (jnp.maximum((nb - 1 - i) * hb - 1, 0), 0)),
                  pl.BlockSpec((SUBLANES, SC_W), lambda i: (0, 0)),
                  pl.BlockSpec((SC_ROWS, SC_W), lambda i: (nb - 1 - i, 0))],
        out_specs=[pl.BlockSpec((SC_ROWS, SC_IN), lambda i: (nb - 1 - i, 0)),
                   pl.BlockSpec((SUBLANES, SC_W), lambda i: (0, 0))],
        out_shape=[jax.ShapeDtypeStruct((rows, SC_IN), MXU_DTYPE), jax.ShapeDtypeStruct((SUBLANES, SC_W), F32)],
        scratch_shapes=[pltpu.VMEM((SC_ROWS + SUBLANES, SC_W), F32), pltpu.VMEM((SC_ROWS + SUBLANES, SC_W), F32)],
        compiler_params=_cparams(("arbitrary",)),
    )(proj, proj, _pad_rows(conv_w), dmix)


def _pad_rows(w, rows=SUBLANES):
    return jnp.pad(w, ((0, rows - w.shape[0]), (0, 0)))


def _pad_lanes(v, lanes=LANES):
    v = v.reshape(1, -1)
    return jnp.pad(v, ((0, 0), (0, lanes - v.shape[1])))


def _tri(n, lower=True):
    r, c = _iota((n, n), 0), _iota((n, n), 1)
    return jnp.where((c <= r) if lower else (c >= r), 1.0, 0.0)


def _eye(n):
    return jnp.where(_iota((n, n), 0) == _iota((n, n), 1), 1.0, 0.0)


def _head_expand(n, width):
    return jnp.where(_iota((LANES, n), 1) // width == _iota((LANES, n), 0), 1.0, 0.0)


def _col(v, h):
    return jnp.sum(jnp.where(_iota(v.shape, 1) == h, v, 0.0), axis=1, keepdims=True)


def _row(v, r):
    return jnp.sum(jnp.where(_iota(v.shape, 0) == r, v, 0.0), axis=0, keepdims=True)


def _expand_row(v, e):
    return jnp.max(_xsel(jnp.broadcast_to(v, (SUBLANES, LANES)), e), axis=0, keepdims=True)


SSD_ROWS = 128
SSD_X0 = D_INNER
SSD_DT0 = D_INNER + SSD_CONV_DIM
SSD_B0 = D_INNER
SSD_C0 = D_INNER + SSD_G * SSD_S
SSD_GW = D_INNER // SSD_G
SSD_HG = SSD_H // SSD_G


def _ssd_prologue(blk, p_ref, halo_ref, cw_ref, cb_ref, dtb_ref, ext_ref, xbc_ref, dt_ref, pre_ref=None):
    ext_ref[0:SUBLANES, :] = jnp.where(blk == 0, 0.0, halo_ref[:, SSD_X0:SSD_DT0])
    ext_ref[SUBLANES:, :] = p_ref[:, SSD_X0:SSD_DT0]
    w = cw_ref[...]
    for c0 in range(0, SSD_CONV_DIM, CONV_COLS):
        cols = slice(c0, c0 + CONV_COLS)
        pre = _conv_from_ext(ext_ref, w, SSD_CONV, SSD_ROWS, cols) + cb_ref[:, cols]
        if pre_ref is not None:
            pre_ref[:, cols] = pre
        xbc_ref[:, cols] = _silu(pre)
    dt_ref[...] = _softplus(p_ref[:, SSD_DT0:SSD_DT0 + LANES] + dtb_ref[...])


def _ssd_chunk_decays(dt_c, a_row, ltri, eye_l, act_ref):
    da = dt_c * a_row
    ac = _sel(ltri,da)
    act_ref[...] = _sel_nt(eye_l,ac)
    ac_last = _row(ac, CHUNK - 1)
    return ac, jnp.exp(ac_last - ac), jnp.exp(ac), jnp.exp(ac_last)


def _ssd_seg(ac, act_ref, h, causal):
    return jnp.where(causal, jnp.exp(jnp.minimum(_col(ac, h) - act_ref[pl.ds(h, 1), :], 0.0)), 0.0)


def _ssd_half(pair, e):
    upper = _iota(pair.shape, 1) >= SSD_P
    return jnp.where(upper if e % 2 else jnp.logical_not(upper), pair, 0.0)


def _ssd_groups_fwd(xbc_ref, rows, dt_exp, tail_exp, cdec_exp, ac, act_ref, states, causal):
    gs, heads = range(SSD_G), range(SSD_HG)
    gls = [slice(g * SSD_GW, (g + 1) * SSD_GW) for g in gs]
    bg = [_mx(xbc_ref[rows, SSD_B0 + g * SSD_S:SSD_B0 + (g + 1) * SSD_S]) for g in gs]
    cg = [_mx(xbc_ref[rows, SSD_C0 + g * SSD_S:SSD_C0 + (g + 1) * SSD_S]) for g in gs]
    s_c = [_mx(s) for s in states]
    xdt = [xbc_ref[rows, gl] * dt_exp[:, gl] for gl in gls]
    cb = [_dot_nt(cg[g], bg[g]) for g in gs]
    cs = [_dot(cg[g], s_c[g]) for g in gs]
    segs = [[_ssd_seg(ac, act_ref, g * SSD_HG + e, causal) for e in heads] for g in gs]
    gms = [[seg * cb[g] for seg in segs[g]] for g in gs]
    gms_c = [[_mx(gm) for gm in gms[g]] for g in gs]
    halves = [[_ssd_half(xdt[g][:, (e // 2) * LANES:(e // 2 + 1) * LANES], e) for e in heads] for g in gs]
    parts = [[_dot(gms_c[g][e], halves[g][e]) for e in heads] for g in gs]
    yd = [jnp.concatenate([parts[g][2 * p] + parts[g][2 * p + 1] for p in range(SSD_HG // 2)], axis=1) for g in gs]
    st = [_dot_tn(bg[g], xdt[g] * tail_exp[:, gls[g]]) for g in gs]
    return [(yd[g] + cs[g] * cdec_exp[:, gls[g]], st[g], bg[g], cg[g], s_c[g], cb[g], xdt[g], cs[g],
             segs[g], gms[g], gms_c[g]) for g in gs]


def ssd_fwd(proj, conv_w, conv_b, a_log, dt_bias, d_skip, norm_w, *, name):
    rows_total = proj.shape[0]
    nb = rows_total // SSD_ROWS
    hb = SSD_ROWS // SUBLANES
    cpb = SSD_ROWS // CHUNK

    def body(p_ref, halo_ref, cw_ref, cb_ref, alog_ref, dtb_ref, dsk_ref, nw_ref, mix_ref, st_ref, xbc_ref, pre_ref,
             ext_ref, dt_ref, s_ref, act_ref):
        i = pl.program_id(0)

        @pl.when(i == 0)
        def _():
            s_ref[...] = jnp.zeros_like(s_ref)

        _ssd_prologue(i, p_ref, halo_ref, cw_ref, cb_ref, dtb_ref, ext_ref, xbc_ref, dt_ref, pre_ref)
        a_row = -jnp.exp(alog_ref[...])
        expand = _head_expand(D_INNER, SSD_P)
        dsk_exp = _expand_row(dsk_ref[...], expand)
        ltri, eye_l = _tri(CHUNK), _eye(LANES)
        causal = _iota((CHUNK, CHUNK), 1) <= _iota((CHUNK, CHUNK), 0)

        def chunk(c, carry):
            rows = pl.ds(pl.multiple_of(c * CHUNK, CHUNK), CHUNK)
            dt_c = dt_ref[rows, :]
            ac, tail, cdec, tot = _ssd_chunk_decays(dt_c, a_row, ltri, eye_l, act_ref)
            dt_exp = _xsel(dt_c, expand)
            tail_exp = _xsel(tail, expand)
            cdec_exp = _xsel(cdec, expand)
            tot_exp = _expand_row(tot, expand)
            states = [s_ref[g] for g in range(SSD_G)]
            fwd = _ssd_groups_fwd(xbc_ref, rows, dt_exp, tail_exp, cdec_exp, ac, act_ref, states, causal)
            for g in range(SSD_G):
                gl = slice(g * SSD_GW, (g + 1) * SSD_GW)
                st_ref[c, g] = states[g]
                y, st = fwd[g][:2]
                s_ref[g] = states[g] * tot_exp[:, gl] + st
                y = (y + dsk_exp[:, gl] * xbc_ref[rows, gl]) * _silu(p_ref[rows, gl])
                r = lax.rsqrt(jnp.mean(y * y, axis=-1, keepdims=True) + RMS_EPS)
                mix_ref[rows, gl] = (y * r * nw_ref[:, gl]).astype(mix_ref.dtype)
            return carry

        lax.fori_loop(0, cpb, chunk, 0)

    vec = lambda n: pl.BlockSpec((1, n), lambda i: (0, 0))
    mix, states, xbc, pre = pl.pallas_call(
        body, name=name, grid=(nb,),
        in_specs=[pl.BlockSpec((SSD_ROWS, SSD_IN_PAD), lambda i: (i, 0)),
                  pl.BlockSpec((SUBLANES, SSD_IN_PAD), lambda i: (jnp.maximum(i * hb - 1, 0), 0)),
                  pl.BlockSpec((SUBLANES, SSD_CONV_DIM), lambda i: (0, 0)),
                  vec(SSD_CONV_DIM), vec(LANES), vec(LANES), vec(LANES), vec(D_INNER)],
        out_specs=[pl.BlockSpec((SSD_ROWS, D_INNER), lambda i: (i, 0)),
                   pl.BlockSpec((cpb, SSD_G, SSD_S, SSD_GW), lambda i: (i, 0, 0, 0)),
                   pl.BlockSpec((SSD_ROWS, SSD_CONV_DIM), lambda i: (i, 0)),
                   pl.BlockSpec((SSD_ROWS, SSD_CONV_DIM), lambda i: (i, 0))],
        out_shape=[jax.ShapeDtypeStruct((rows_total, D_INNER), MXU_DTYPE),
                   jax.ShapeDtypeStruct((rows_total // CHUNK, SSD_G, SSD_S, SSD_GW), F32),
                   jax.ShapeDtypeStruct((rows_total, SSD_CONV_DIM), F32),
                   jax.ShapeDtypeStruct((rows_total, SSD_CONV_DIM), F32)],
        scratch_shapes=[pltpu.VMEM((SSD_ROWS + SUBLANES, SSD_CONV_DIM), F32),
                        pltpu.VMEM((SSD_ROWS, LANES), F32),
                        pltpu.VMEM((SSD_G, SSD_S, SSD_GW), F32),
                        pltpu.VMEM((LANES, CHUNK), F32)],
        compiler_params=_cparams(("arbitrary",)),
    )(proj, proj, _pad_rows(conv_w), conv_b.reshape(1, -1), _pad_lanes(a_log), _pad_lanes(dt_bias),
      _pad_lanes(d_skip), norm_w.reshape(1, -1))
    return mix, (states, xbc, pre)


def ssd_bwd(proj, conv_w, a_log, dt_bias, d_skip, norm_w, saved, dmix, *, name):
    states, xbc, pre = saved
    rows_total = proj.shape[0]
    nb = rows_total // SSD_ROWS
    cpb = SSD_ROWS // CHUNK

    def body(p_ref, cw_ref, alog_ref, dtb_ref, dsk_ref, nw_ref, st_ref, dm_ref, xbc_ref, pre_ref,
             dp_ref, dcw_ref, dcb_ref, dalog_ref, ddtb_ref, ddsk_ref, dnw_ref,
             dt_ref, ds_ref, act_ref, dext_ref, dac_ref, dact_ref, ddskw_ref):
        i = pl.program_id(0)

        @pl.when(i == 0)
        def _():
            ds_ref[...] = jnp.zeros_like(ds_ref)
            dext_ref[SSD_ROWS:, :] = jnp.zeros((SUBLANES, SSD_CONV_DIM), F32)
            ddskw_ref[...] = jnp.zeros_like(ddskw_ref)
            for r in (dcw_ref, dcb_ref, dalog_ref, ddtb_ref, ddsk_ref, dnw_ref):
                r[...] = jnp.zeros_like(r)

        dt_ref[...] = _softplus(p_ref[:, SSD_DT0:SSD_DT0 + LANES] + dtb_ref[...])
        a_row = -jnp.exp(alog_ref[...])
        expand = _head_expand(D_INNER, SSD_P)
        dsk_exp = _expand_row(dsk_ref[...], expand)
        ltri, utri, eye_l, eye_c = _tri(CHUNK), _tri(CHUNK, lower=False), _eye(LANES), _eye(CHUNK)
        causal = _iota((CHUNK, CHUNK), 1) <= _iota((CHUNK, CHUNK), 0)
        dp_ref[:, SSD_DT0 + LANES:] = jnp.zeros((SSD_ROWS, SSD_IN_PAD - SSD_DT0 - LANES), dp_ref.dtype)

        def chunk(cc, carry):
            c = cpb - 1 - cc
            rows = pl.ds(pl.multiple_of(c * CHUNK, CHUNK), CHUNK)
            dt_c = dt_ref[rows, :]
            ac, tail, cdec, tot = _ssd_chunk_decays(dt_c, a_row, ltri, eye_l, act_ref)
            dt_exp = _xsel(dt_c, expand)
            tail_exp = _xsel(tail, expand)
            cdec_exp = _xsel(cdec, expand)
            tot_exp = _expand_row(tot, expand)
            dac_ref[...] = jnp.zeros_like(dac_ref)
            dact_ref[...] = jnp.zeros_like(dact_ref)
            d_cdec = jnp.zeros((CHUNK, LANES), F32)
            d_tail = jnp.zeros((CHUNK, LANES), F32)
            d_dt = jnp.zeros((CHUNK, LANES), F32)
            d_tot = jnp.zeros((1, LANES), F32)
            gs = range(SSD_G)
            gls = [slice(g * SSD_GW, (g + 1) * SSD_GW) for g in gs]
            exs = [expand[:, gl] for gl in gls]
            states = [st_ref[c, g] for g in gs]
            fwd = _ssd_groups_fwd(xbc_ref, rows, dt_exp, tail_exp, cdec_exp, ac, act_ref, states, causal)
            ys, _, bgs, cgs, s_cs, cbs, xdts, css, segss, gmss, gms_cs = (list(t) for t in zip(*fwd))
            xss = [xbc_ref[rows, gl] for gl in gls]
            dys = []
            for g, gl in enumerate(gls):
                z = p_ref[rows, gl]
                sz = _silu(z)
                y2 = ys[g] + dsk_exp[:, gl] * xss[g]
                yg = y2 * sz
                r = lax.rsqrt(jnp.mean(yg * yg, axis=-1, keepdims=True) + RMS_EPS)
                yn = yg * r
                dm = dm_ref[rows, gl]
                dnw_ref[:, gl] += jnp.sum(dm * yn, axis=0, keepdims=True)
                dyn = dm * nw_ref[:, gl]
                dyg = r * (dyn - yn * jnp.mean(dyn * yn, axis=-1, keepdims=True))
                dp_ref[rows, gl] = (dyg * y2 * _dsilu(z)).astype(dp_ref.dtype)
                dys.append(dyg * sz)
                ddskw_ref[:, gl] += jnp.sum(dys[g] * xss[g], axis=0, keepdims=True)
            ds_gs = [ds_ref[g] for g in gs]
            ds_cs = [_mx(d) for d in ds_gs]
            dycs = [_mx(dys[g] * cdec_exp[:, gls[g]]) for g in gs]
            ds_new = [_dot_tn(cgs[g], dycs[g]) for g in gs]
            dcgs = [_dot_nt(dycs[g], s_cs[g]) for g in gs]
            d_xdtds = [_dot(bgs[g], ds_cs[g]) for g in gs]
            dbgs = [_dot_nt(xdts[g] * tail_exp[:, gls[g]], ds_cs[g]) for g in gs]
            for g in gs:
                ds_ref[g] = ds_gs[g] * tot_exp[:, gls[g]] + ds_new[g]
                sds = jnp.broadcast_to(jnp.sum(states[g] * ds_gs[g], axis=0, keepdims=True), (SUBLANES, SSD_GW))
                d_tot = d_tot + jnp.max(_xsel_nt(sds, exs[g]), axis=0, keepdims=True)
                d_cdec = d_cdec + _xsel_nt(dys[g] * css[g], exs[g])
                d_tail = d_tail + _xsel_nt(d_xdtds[g] * xdts[g], exs[g])
            heads = range(SSD_HG)
            dy_hs = [[_mx(_ssd_half(dys[g][:, (e // 2) * LANES:(e // 2 + 1) * LANES], e)) for e in heads] for g in gs]
            xps_cs = [[_mx(xdts[g][:, p * LANES:(p + 1) * LANES]) for p in range(SSD_HG // 2)] for g in gs]
            backs = [[_dot_tn(gms_cs[g][e], dy_hs[g][e]) for e in heads] for g in gs]
            dg_ms = [[jnp.where(causal, _dot_nt(dy_hs[g][e], xps_cs[g][e // 2]), 0.0) for e in heads] for g in gs]
            d_cbs = []
            for g in gs:
                d_cb = None
                for e in heads:
                    h = g * SSD_HG + e
                    term = dg_ms[g][e] * segss[g][e]
                    d_cb = term if d_cb is None else d_cb + term
                    em = dg_ms[g][e] * gmss[g][e]
                    dac_ref[...] += jnp.where(_iota((CHUNK, LANES), 1) == h, jnp.sum(em, axis=1, keepdims=True), 0.0)
                    dact_ref[h:h + 1, :] = jnp.sum(em, axis=0, keepdims=True)
                d_cbs.append(_mx(d_cb))
            dcg2 = [_dot(d_cbs[g], bgs[g]) for g in gs]
            dbg2 = [_dot_tn(d_cbs[g], cgs[g]) for g in gs]
            for g, gl in enumerate(gls):
                d_xdt = d_xdtds[g] * tail_exp[:, gl] + jnp.concatenate(
                    [backs[g][2 * p] + backs[g][2 * p + 1] for p in range(SSD_HG // 2)], axis=1)
                d_dt = d_dt + _xsel_nt(d_xdt * xss[g], exs[g])
                dext_ref[rows, gl] = d_xdt * dt_exp[:, gl] + dys[g] * dsk_exp[:, gl]
                dext_ref[rows, SSD_B0 + g * SSD_S:SSD_B0 + (g + 1) * SSD_S] = dbgs[g] + dbg2[g]
                dext_ref[rows, SSD_C0 + g * SSD_S:SSD_C0 + (g + 1) * SSD_S] = dcgs[g] + dcg2[g]
            d_ac = dac_ref[...] - _sel_nt(eye_c,dact_ref[...]) + d_cdec * cdec - d_tail * tail
            d_last = jnp.sum(d_tail * tail, axis=0, keepdims=True) + d_tot * tot
            d_ac = jnp.where(_iota((CHUNK, LANES), 0) == CHUNK - 1, d_ac + d_last, d_ac)
            d_da = _sel(utri,d_ac)
            d_dt = d_dt + d_da * a_row
            dalog_ref[...] += jnp.sum(d_da * dt_c, axis=0, keepdims=True) * a_row
            d_raw = d_dt * _sigmoid(p_ref[rows, SSD_DT0:SSD_DT0 + LANES] + dtb_ref[...])
            d_raw = jnp.where(_iota((CHUNK, LANES), 1) < SSD_H, d_raw, 0.0)
            ddtb_ref[...] += jnp.sum(d_raw, axis=0, keepdims=True)
            dp_ref[rows, SSD_DT0:SSD_DT0 + LANES] = d_raw.astype(dp_ref.dtype)
            return carry

        lax.fori_loop(0, cpb, chunk, 0)
        w = cw_ref[...]
        for c0 in range(0, SSD_CONV_DIM, CONV_COLS):
            cols = slice(c0, c0 + CONV_COLS)
            d_pre = dext_ref[0:SSD_ROWS, cols] * _dsilu(pre_ref[:, cols])
            dext_ref[0:SSD_ROWS, cols] = d_pre
            dcb_ref[:, cols] += jnp.sum(d_pre, axis=0, keepdims=True)
            du, dws = _conv_bwd_from_ext(dext_ref, p_ref[:, SSD_X0 + c0:SSD_X0 + c0 + CONV_COLS], w, SSD_CONV, SSD_ROWS, cols)
            for j in range(SSD_CONV):
                dcw_ref[j:j + 1, cols] += dws[j]
            dp_ref[:, SSD_X0 + c0:SSD_X0 + c0 + CONV_COLS] = du.astype(dp_ref.dtype)
            dext_ref[SSD_ROWS:, cols] = d_pre[0:SUBLANES, :]

        @pl.when(i == nb - 1)
        def _():
            ddsk_ref[...] = jnp.max(_xsel_nt(jnp.broadcast_to(ddskw_ref[...], (SUBLANES, D_INNER)), expand), axis=0, keepdims=True)

    vec = lambda n: pl.BlockSpec((1, n), lambda i: (0, 0))
    outs = pl.pallas_call(
        body, name=name, grid=(nb,),
        in_specs=[pl.BlockSpec((SSD_ROWS, SSD_IN_PAD), lambda i: (nb - 1 - i, 0)),
                  pl.BlockSpec((SUBLANES, SSD_CONV_DIM), lambda i: (0, 0)),
                  vec(LANES), vec(LANES), vec(LANES), vec(D_INNER),
                  pl.BlockSpec((cpb, SSD_G, SSD_S, SSD_GW), lambda i: (nb - 1 - i, 0, 0, 0)),
                  pl.BlockSpec((SSD_ROWS, D_INNER), lambda i: (nb - 1 - i, 0)),
                  pl.BlockSpec((SSD_ROWS, SSD_CONV_DIM), lambda i: (nb - 1 - i, 0)),
                  pl.BlockSpec((SSD_ROWS, SSD_CONV_DIM), lambda i: (nb - 1 - i, 0))],
        out_specs=[pl.BlockSpec((SSD_ROWS, SSD_IN_PAD), lambda i: (nb - 1 - i, 0)),
                   pl.BlockSpec((SUBLANES, SSD_CONV_DIM), lambda i: (0, 0)),
                   vec(SSD_CONV_DIM), vec(LANES), vec(LANES), vec(LANES), vec(D_INNER)],
        out_shape=[jax.ShapeDtypeStruct((rows_total, SSD_IN_PAD), MXU_DTYPE),
                   jax.ShapeDtypeStruct((SUBLANES, SSD_CONV_DIM), F32),
                   jax.ShapeDtypeStruct((1, SSD_CONV_DIM), F32), jax.ShapeDtypeStruct((1, LANES), F32),
                   jax.ShapeDtypeStruct((1, LANES), F32), jax.ShapeDtypeStruct((1, LANES), F32),
                   jax.ShapeDtypeStruct((1, D_INNER), F32)],
        scratch_shapes=[pltpu.VMEM((SSD_ROWS, LANES), F32),
                        pltpu.VMEM((SSD_G, SSD_S, SSD_GW), F32),
                        pltpu.VMEM((LANES, CHUNK), F32),
                        pltpu.VMEM((SSD_ROWS + SUBLANES, SSD_CONV_DIM), F32),
                        pltpu.VMEM((CHUNK, LANES), F32),
                        pltpu.VMEM((LANES, CHUNK), F32),
                        pltpu.VMEM((1, D_INNER), F32)],
        compiler_params=_cparams(("arbitrary",)),
    )(proj, _pad_rows(conv_w), _pad_lanes(a_log), _pad_lanes(dt_bias),
      _pad_lanes(d_skip), norm_w.reshape(1, -1), states, dmix, xbc, pre)
    dproj, dcw, dcb, dalog, ddtb, ddsk, dnw = outs
    return dproj, [dcw[:SSD_CONV], dcb[0], dalog[0, :SSD_H], ddtb[0, :SSD_H], ddsk[0, :SSD_H], dnw[0]]


GDN_ROWS = 128
GDN_K0 = GDN_QK
GDN_V0 = 2 * GDN_QK
GDN_Z0 = GDN_CONV_DIM
GDN_BA0 = GDN_CONV_DIM + GDN_V
GDN_GL = GDN_VH
GDN_SCALE = GDN_HEAD ** -0.5
GDN_GROUP = 16


def _gdn_lane_params(v):
    return jnp.pad(v.reshape(1, GDN_VH), ((0, 0), (GDN_GL, LANES - GDN_GL - GDN_VH)))


def _gdn_prologue(blk, p_ref, halo_ref, cw_ref, alog_ref, dtb_ref, ext_ref, qkv_ref, beta_ref, g_ref, pre_ref=None):
    ext_ref[0:SUBLANES, :] = jnp.where(blk == 0, 0.0, halo_ref[:, 0:GDN_CONV_DIM])
    ext_ref[SUBLANES:, :] = p_ref[:, 0:GDN_CONV_DIM]
    w = cw_ref[...]
    for hh in range(GDN_CONV_DIM // GDN_HEAD):
        cols = slice(hh * GDN_HEAD, (hh + 1) * GDN_HEAD)
        pre = _conv_from_ext(ext_ref, w, GDN_CONV, GDN_ROWS, cols)
        if pre_ref is not None:
            pre_ref[:, cols] = pre
        a = _silu(pre)
        if hh < 2 * GDN_QKH:
            r = lax.rsqrt(jnp.sum(a * a, axis=-1, keepdims=True) + L2_EPS)
            a = a * (r * (GDN_SCALE if hh < GDN_QKH else 1.0))
        qkv_ref[:, cols] = a
    _gdn_gates(p_ref, alog_ref, dtb_ref, beta_ref, g_ref)


def _gdn_gates(p_ref, alog_ref, dtb_ref, beta_ref, g_ref):
    ba = p_ref[:, GDN_BA0:GDN_BA0 + LANES]
    beta_ref[...] = _sigmoid(ba)
    g_ref[...] = -jnp.exp(alog_ref[...]) * _softplus(ba + dtb_ref[...])


def _each(f, *lists):
    return [f(*z) for z in zip(*lists)]


def _inv_unit_lower_each(a_list, eye_c):
    xs = [eye_c - a for a in a_list]
    ps = [_mx(a) for a in a_list]
    n = 2
    while n < CHUNK:
        ps = [_mx(_dot(p, p)) for p in ps]
        xs = [x + _dot(x, p) for x, p in zip(xs, ps)]
        n *= 2
    return xs


def _mx(x):
    return x.astype(MXU_DTYPE)


def _gdn_heads_fwd(q, k, v, kk, qk, gcol, grow, glast, bcol, s, causal, strict, eye_c, t=None):
    decay = _each(lambda gc_, gr_: jnp.where(causal, jnp.exp(jnp.minimum(gc_ - gr_, 0.0)), 0.0), gcol, grow)
    egc = _each(jnp.exp, gcol)
    etail = _each(lambda gl_, gc_: jnp.exp(gl_ - gc_), glast, gcol)
    cd = _each(jnp.exp, glast)
    a = _each(lambda b_, kk_, d_: jnp.where(strict, b_ * kk_ * d_, 0.0), bcol, kk, decay)
    if t is None:
        t = _inv_unit_lower_each(a, eye_c)
    t_c, s_c = _each(_mx, t), _each(_mx, s)
    kb = _each(lambda k_, b_: k_ * b_, k, bcol)
    rhs_w = _each(lambda kb_, e_: kb_ * e_, kb, egc)
    u = _each(lambda t_, v_, b_: _dot(t_, v_ * b_), t_c, v, bcol)
    w = _each(_dot, t_c, rhs_w)
    w_c = _each(_mx, w)
    attn = _each(lambda qk_, d_: qk_ * d_, qk, decay)
    attn_c = _each(_mx, attn)
    ws = _each(_dot, w_c, s_c)
    v_new = _each(lambda u_, ws_: u_ - ws_, u, ws)
    vn_c = _each(_mx, v_new)
    qd = _each(lambda q_, e_: q_ * e_, q, egc)
    kt = _each(lambda k_, e_: k_ * e_, k, etail)
    qd_c, kt_c = _each(_mx, qd), _each(_mx, kt)
    o1 = _each(_dot, qd_c, s_c)
    o2 = _each(_dot, attn_c, vn_c)
    out = _each(lambda a_, b_: a_ + b_, o1, o2)
    upd = _each(_dot_tn, kt_c, vn_c)
    s_new = _each(lambda s_, c_, u_: s_ * c_ + u_, s, cd, upd)
    return dict(decay=decay, egc=egc, etail=etail, cd=cd, a=a, t=t, kb=kb, rhs_w=rhs_w, u=u, w=w, attn=attn,
                v_new=v_new, qd=qd, kt=kt, out=out, s_new=s_new,
                t_c=t_c, s_c=s_c, w_c=w_c, attn_c=attn_c, vn_c=vn_c, qd_c=qd_c, kt_c=kt_c)


def gdn_fwd(proj, conv_w, a_log, dt_bias, norm_w, *, name):
    rows_total = proj.shape[0]
    nb = rows_total // GDN_ROWS
    hb = GDN_ROWS // SUBLANES
    cpb = GDN_ROWS // CHUNK

    def body(p_ref, halo_ref, cw_ref, alog_ref, dtb_ref, nw_ref, mix_ref, st_ref, tm_ref, qkv_ref, pre_ref,
             ext_ref, beta_ref, g_ref, s_ref, gct_ref):
        i = pl.program_id(0)

        @pl.when(i == 0)
        def _():
            s_ref[...] = jnp.zeros_like(s_ref)

        _gdn_prologue(i, p_ref, halo_ref, cw_ref, alog_ref, dtb_ref, ext_ref, qkv_ref, beta_ref, g_ref, pre_ref)
        ltri, eye_l, eye_c = _tri(CHUNK), _eye(LANES), _eye(CHUNK)
        causal = _iota((CHUNK, CHUNK), 1) <= _iota((CHUNK, CHUNK), 0)
        strict = _iota((CHUNK, CHUNK), 1) < _iota((CHUNK, CHUNK), 0)
        nw = nw_ref[...]

        def chunk(c, carry):
            rows = pl.ds(pl.multiple_of(c * CHUNK, CHUNK), CHUNK)
            gc = _sel(ltri,g_ref[rows, :])
            gct_ref[...] = _sel_nt(eye_l,gc)
            glast_row = _row(gc, CHUNK - 1)
            beta_c = beta_ref[rows, :]
            for h0 in range(0, GDN_VH, GDN_GROUP):
                hs = list(range(h0, h0 + GDN_GROUP))
                qs = {hq: qkv_ref[rows, hq * GDN_HEAD:(hq + 1) * GDN_HEAD] for hq in range(h0 // 2, (h0 + GDN_GROUP) // 2)}
                ks = {hq: qkv_ref[rows, GDN_K0 + hq * GDN_HEAD:GDN_K0 + (hq + 1) * GDN_HEAD] for hq in qs}
                ks_c = {hq: _mx(ks[hq]) for hq in qs}
                kks = {hq: _dot_nt(ks_c[hq], ks_c[hq]) for hq in qs}
                qks = {hq: _dot_nt(qs[hq], ks_c[hq]) for hq in qs}
                ss = [s_ref[h] for h in hs]
                for h, s in zip(hs, ss):
                    st_ref[c, h] = s
                f = _gdn_heads_fwd(
                    [qs[h // 2] for h in hs], [ks[h // 2] for h in hs],
                    [qkv_ref[rows, GDN_V0 + h * GDN_HEAD:GDN_V0 + (h + 1) * GDN_HEAD] for h in hs],
                    [kks[h // 2] for h in hs], [qks[h // 2] for h in hs],
                    [_col(gc, GDN_GL + h) for h in hs], [gct_ref[GDN_GL + h:GDN_GL + h + 1, :] for h in hs],
                    [_col(glast_row, GDN_GL + h) for h in hs], [_col(beta_c, h) for h in hs], ss, causal, strict, eye_c)
                for i_h, h in enumerate(hs):
                    hc = slice(h * GDN_HEAD, (h + 1) * GDN_HEAD)
                    s_ref[h] = f["s_new"][i_h]
                    tm_ref[c, h] = f["t"][i_h].astype(tm_ref.dtype)
                    o = f["out"][i_h]
                    r = lax.rsqrt(jnp.mean(o * o, axis=-1, keepdims=True) + RMS_EPS)
                    z = p_ref[rows, GDN_Z0 + h * GDN_HEAD:GDN_Z0 + (h + 1) * GDN_HEAD]
                    mix_ref[rows, hc] = (o * r * nw * _silu(z)).astype(mix_ref.dtype)
            return carry

        lax.fori_loop(0, cpb, chunk, 0)

    vec = lambda n: pl.BlockSpec((1, n), lambda i: (0, 0))
    act = pl.BlockSpec((GDN_ROWS, GDN_CONV_DIM), lambda i: (i, 0))
    mix, states, tmats, qkv, pre = pl.pallas_call(
        body, name=name, grid=(nb,),
        in_specs=[pl.BlockSpec((GDN_ROWS, GDN_IN_PAD), lambda i: (i, 0)),
                  pl.BlockSpec((SUBLANES, GDN_IN_PAD), lambda i: (jnp.maximum(i * hb - 1, 0), 0)),
                  pl.BlockSpec((SUBLANES, GDN_CONV_DIM), lambda i: (0, 0)),
                  vec(LANES), vec(LANES), vec(GDN_HEAD)],
        out_specs=[pl.BlockSpec((GDN_ROWS, GDN_V), lambda i: (i, 0)),
                   pl.BlockSpec((cpb, GDN_VH, GDN_HEAD, GDN_HEAD), lambda i: (i, 0, 0, 0)),
                   pl.BlockSpec((cpb, GDN_VH, CHUNK, CHUNK), lambda i: (i, 0, 0, 0)), act, act],
        out_shape=[jax.ShapeDtypeStruct((rows_total, GDN_V), MXU_DTYPE),
                   jax.ShapeDtypeStruct((rows_total // CHUNK, GDN_VH, GDN_HEAD, GDN_HEAD), F32),
                   jax.ShapeDtypeStruct((rows_total // CHUNK, GDN_VH, CHUNK, CHUNK), MXU_DTYPE),
                   jax.ShapeDtypeStruct((rows_total, GDN_CONV_DIM), F32),
                   jax.ShapeDtypeStruct((rows_total, GDN_CONV_DIM), F32)],
        scratch_shapes=[pltpu.VMEM((GDN_ROWS + SUBLANES, GDN_CONV_DIM), F32),
                        pltpu.VMEM((GDN_ROWS, LANES), F32),
                        pltpu.VMEM((GDN_ROWS, LANES), F32),
                        pltpu.VMEM((GDN_VH, GDN_HEAD, GDN_HEAD), F32),
                        pltpu.VMEM((LANES, CHUNK), F32)],
        compiler_params=_cparams(("arbitrary",)),
    )(proj, proj, _pad_rows(conv_w), _gdn_lane_params(a_log), _gdn_lane_params(dt_bias), norm_w.reshape(1, -1))
    return mix, (states, tmats, qkv, pre)


def gdn_bwd(proj, conv_w, a_log, dt_bias, norm_w, saved, dmix, *, name):
    states, tmats, qkv, pre = saved
    rows_total = proj.shape[0]
    nb = rows_total // GDN_ROWS
    cpb = GDN_ROWS // CHUNK

    def body(p_ref, cw_ref, alog_ref, dtb_ref, nw_ref, st_ref, tm_ref, dm_ref, qkv_ref, pre_ref,
             dp_ref, dcw_ref, dalog_ref, ddtb_ref, dnw_ref,
             beta_ref, g_ref, ds_ref, gct_ref, dext_ref, dgc_ref, dgct_ref, dbeta_ref):
        i = pl.program_id(0)

        @pl.when(i == 0)
        def _():
            ds_ref[...] = jnp.zeros_like(ds_ref)
            dext_ref[GDN_ROWS:, :] = jnp.zeros((SUBLANES, GDN_CONV_DIM), F32)
            for r in (dcw_ref, dalog_ref, ddtb_ref, dnw_ref):
                r[...] = jnp.zeros_like(r)

        _gdn_gates(p_ref, alog_ref, dtb_ref, beta_ref, g_ref)
        ltri, utri, eye_l, eye_c = _tri(CHUNK), _tri(CHUNK, lower=False), _eye(LANES), _eye(CHUNK)
        causal = _iota((CHUNK, CHUNK), 1) <= _iota((CHUNK, CHUNK), 0)
        strict = _iota((CHUNK, CHUNK), 1) < _iota((CHUNK, CHUNK), 0)
        lane = _iota((CHUNK, LANES), 1)
        is_last = _iota((CHUNK, 1), 0) == CHUNK - 1
        nw = nw_ref[...]

        def chunk(cc, carry):
            c = cpb - 1 - cc
            rows = pl.ds(pl.multiple_of(c * CHUNK, CHUNK), CHUNK)
            g_c = g_ref[rows, :]
            gc = _sel(ltri,g_c)
            gct_ref[...] = _sel_nt(eye_l,gc)
            glast_row = _row(gc, CHUNK - 1)
            beta_c = beta_ref[rows, :]
            dgc_ref[...] = jnp.zeros_like(dgc_ref)
            dgct_ref[...] = jnp.zeros_like(dgct_ref)
            dbeta_ref[...] = jnp.zeros_like(dbeta_ref)
            for h0 in range(0, GDN_VH, GDN_GROUP):
                hs = list(range(h0, h0 + GDN_GROUP))
                hqs = list(range(h0 // 2, (h0 + GDN_GROUP) // 2))
                qs = {hq: qkv_ref[rows, hq * GDN_HEAD:(hq + 1) * GDN_HEAD] for hq in hqs}
                ks = {hq: qkv_ref[rows, GDN_K0 + hq * GDN_HEAD:GDN_K0 + (hq + 1) * GDN_HEAD] for hq in hqs}
                qs_c = {hq: _mx(qs[hq]) for hq in hqs}
                ks_c = {hq: _mx(ks[hq]) for hq in hqs}
                kks = {hq: _dot_nt(ks_c[hq], ks_c[hq]) for hq in hqs}
                qks = {hq: _dot_nt(qs_c[hq], ks_c[hq]) for hq in hqs}
                q = [qs[h // 2] for h in hs]
                k = [ks[h // 2] for h in hs]
                v = [qkv_ref[rows, GDN_V0 + h * GDN_HEAD:GDN_V0 + (h + 1) * GDN_HEAD] for h in hs]
                s = [st_ref[c, h] for h in hs]
                bcol = [_col(beta_c, h) for h in hs]
                f = _gdn_heads_fwd(q, k, v, [kks[h // 2] for h in hs], [qks[h // 2] for h in hs],
                                   [_col(gc, GDN_GL + h) for h in hs], [gct_ref[GDN_GL + h:GDN_GL + h + 1, :] for h in hs],
                                   [_col(glast_row, GDN_GL + h) for h in hs], bcol, s, causal, strict, eye_c,
                                   t=[tm_ref[c, h] for h in hs])
                do = []
                for i_h, h in enumerate(hs):
                    zc = slice(GDN_Z0 + h * GDN_HEAD, GDN_Z0 + (h + 1) * GDN_HEAD)
                    o = f["out"][i_h]
                    z = p_ref[rows, zc]
                    sz = _silu(z)
                    r = lax.rsqrt(jnp.mean(o * o, axis=-1, keepdims=True) + RMS_EPS)
                    on = o * r
                    dm = dm_ref[rows, h * GDN_HEAD:(h + 1) * GDN_HEAD]
                    dnw_ref[...] += jnp.sum(dm * on * sz, axis=0, keepdims=True)
                    d_on = dm * nw * sz
                    dp_ref[rows, zc] = (dm * on * nw * _dsilu(z)).astype(dp_ref.dtype)
                    do.append(r * (d_on - on * jnp.mean(d_on * on, axis=-1, keepdims=True)))
                ds_n = [ds_ref[h] for h in hs]
                do_c, dsn_c = _each(_mx, do), _each(_mx, ds_n)
                k_c = [ks_c[h // 2] for h in hs]
                dv1 = _each(_dot_tn, f["attn_c"], do_c)
                dv2 = _each(_dot, f["kt_c"], dsn_c)
                d_vnew = _each(lambda a_, b_: a_ + b_, dv1, dv2)
                dvn_c = _each(_mx, d_vnew)
                d_attn = _each(lambda do_, vn_: jnp.where(causal, _dot_nt(do_, vn_), 0.0), do_c, f["vn_c"])
                d_qd = _each(_dot_nt, do_c, f["s_c"])
                t1 = _each(_dot_tn, f["qd_c"], do_c)
                t2 = _each(_dot_tn, f["w_c"], dvn_c)
                for h, a_, cd_, dsn_, b_ in zip(hs, t1, f["cd"], ds_n, t2):
                    ds_ref[h] = a_ + cd_ * dsn_ - b_
                d_cd = _each(lambda s_, dsn_: jnp.sum(jnp.sum(s_ * dsn_, axis=1, keepdims=True), axis=0, keepdims=True), s, ds_n)
                d_kt = _each(_dot_nt, f["vn_c"], dsn_c)
                d_w = _each(lambda dv_, s_: -_dot_nt(dv_, s_), dvn_c, f["s_c"])
                d_rhs_u = _each(_dot_tn, f["t_c"], dvn_c)
                d_rhs_w = _each(_dot_tn, f["t_c"], d_w)
                m1 = _each(_dot_nt, d_rhs_u, f["u"])
                m2 = _each(_dot_nt, d_rhs_w, f["w_c"])
                da = _each(lambda a_, b_: -jnp.where(strict, a_ + b_, 0.0), m1, m2)
                dmm = _each(lambda a_, b_: a_ * b_, da, f["decay"])
                em = _each(lambda da_, a_, dat_, at_: da_ * a_ + dat_ * at_, da, f["a"], d_attn, f["attn"])
                dmm_c = _each(_mx, dmm)
                x1 = _each(_dot, dmm_c, k_c)
                d_kb = _each(lambda x_, drw_, e_: x_ + drw_ * e_, x1, d_rhs_w, f["egc"])
                dk1 = _each(_dot_tn, dmm_c, f["kb"])
                dpm = _each(lambda a_, b_: _mx(a_ * b_), d_attn, f["decay"])
                dq1 = _each(_dot, dpm, k_c)
                dq = _each(lambda x_, dqd_, e_: x_ + dqd_ * e_, dq1, d_qd, f["egc"])
                dk2 = _each(_dot_tn, dpm, [qs_c[h // 2] for h in hs])
                dk = _each(lambda a_, b_, dkb_, bc_, dkt_, et_: a_ + b_ + dkb_ * bc_ + dkt_ * et_,
                           dk1, dk2, d_kb, bcol, d_kt, f["etail"])
                for i_h, h in enumerate(hs):
                    tmp = jnp.sum(d_kt[i_h] * f["kt"][i_h], axis=1, keepdims=True)
                    d_gcol = (jnp.sum(em[i_h], axis=1, keepdims=True)
                              + jnp.sum(d_rhs_w[i_h] * f["rhs_w"][i_h], axis=1, keepdims=True)
                              + jnp.sum(d_qd[i_h] * f["qd"][i_h], axis=1, keepdims=True) - tmp)
                    d_glast = jnp.sum(tmp, axis=0, keepdims=True) + d_cd[i_h] * f["cd"][i_h]
                    d_gcol = jnp.where(is_last, d_gcol + d_glast, d_gcol)
                    d_beta = (jnp.sum(d_rhs_u[i_h] * v[i_h], axis=1, keepdims=True)
                              + jnp.sum(d_kb[i_h] * k[i_h], axis=1, keepdims=True))
                    dgc_ref[...] += jnp.where(lane == GDN_GL + h, d_gcol, 0.0)
                    dgct_ref[GDN_GL + h:GDN_GL + h + 1, :] = jnp.sum(em[i_h], axis=0, keepdims=True)
                    dbeta_ref[...] += jnp.where(lane == h, d_beta, 0.0)
                    dext_ref[rows, GDN_V0 + h * GDN_HEAD:GDN_V0 + (h + 1) * GDN_HEAD] = d_rhs_u[i_h] * bcol[i_h]
                for hq in hqs:
                    i0 = 2 * hq - h0
                    dext_ref[rows, hq * GDN_HEAD:(hq + 1) * GDN_HEAD] = dq[i0] + dq[i0 + 1]
                    dext_ref[rows, GDN_K0 + hq * GDN_HEAD:GDN_K0 + (hq + 1) * GDN_HEAD] = dk[i0] + dk[i0 + 1]
            d_gc = dgc_ref[...] - _sel_nt(eye_c,dgct_ref[...])
            dg = _sel(utri,d_gc)
            ba = p_ref[rows, GDN_BA0:GDN_BA0 + LANES]
            d_sp = dg * -jnp.exp(alog_ref[...])
            d_araw = d_sp * _sigmoid(ba + dtb_ref[...])
            d_araw = jnp.where((lane >= GDN_GL) & (lane < GDN_GL + GDN_VH), d_araw, 0.0)
            dalog_ref[...] += jnp.sum(dg * g_c, axis=0, keepdims=True)
            ddtb_ref[...] += jnp.sum(d_araw, axis=0, keepdims=True)
            d_braw = jnp.where(lane < GDN_VH, dbeta_ref[...] * beta_c * (1.0 - beta_c), 0.0)
            dp_ref[rows, GDN_BA0:GDN_BA0 + LANES] = (d_braw + d_araw).astype(dp_ref.dtype)
            return carry

        lax.fori_loop(0, cpb, chunk, 0)
        w = cw_ref[...]
        for hh in range(GDN_CONV_DIM // GDN_HEAD):
            cols = slice(hh * GDN_HEAD, (hh + 1) * GDN_HEAD)
            pre = pre_ref[:, cols]
            d_act = dext_ref[0:GDN_ROWS, cols]
            if hh < 2 * GDN_QKH:
                a = _silu(pre)
                r = lax.rsqrt(jnp.sum(a * a, axis=-1, keepdims=True) + L2_EPS)
                ah = a * r
                if hh < GDN_QKH:
                    d_act = d_act * GDN_SCALE
                d_act = r * (d_act - ah * jnp.sum(d_act * ah, axis=-1, keepdims=True))
            d_pre = d_act * _dsilu(pre)
            dext_ref[0:GDN_ROWS, cols] = d_pre
            du, dws = _conv_bwd_from_ext(dext_ref, p_ref[:, cols], w, GDN_CONV, GDN_ROWS, cols)
            for j in range(GDN_CONV):
                dcw_ref[j:j + 1, cols] += dws[j]
            dp_ref[:, cols] = du.astype(dp_ref.dtype)
            dext_ref[GDN_ROWS:, cols] = d_pre[0:SUBLANES, :]

    vec = lambda n: pl.BlockSpec((1, n), lambda i: (0, 0))
    outs = pl.pallas_call(
        body, name=name, grid=(nb,),
        in_specs=[pl.BlockSpec((GDN_ROWS, GDN_IN_PAD), lambda i: (nb - 1 - i, 0)),
                  pl.BlockSpec((SUBLANES, GDN_CONV_DIM), lambda i: (0, 0)),
                  vec(LANES), vec(LANES), vec(GDN_HEAD),
                  pl.BlockSpec((cpb, GDN_VH, GDN_HEAD, GDN_HEAD), lambda i: (nb - 1 - i, 0, 0, 0)),
                  pl.BlockSpec((cpb, GDN_VH, CHUNK, CHUNK), lambda i: (nb - 1 - i, 0, 0, 0)),
                  pl.BlockSpec((GDN_ROWS, GDN_V), lambda i: (nb - 1 - i, 0)),
                  pl.BlockSpec((GDN_ROWS, GDN_CONV_DIM), lambda i: (nb - 1 - i, 0)),
                  pl.BlockSpec((GDN_ROWS, GDN_CONV_DIM), lambda i: (nb - 1 - i, 0))],
        out_specs=[pl.BlockSpec((GDN_ROWS, GDN_IN_PAD), lambda i: (nb - 1 - i, 0)),
                   pl.BlockSpec((SUBLANES, GDN_CONV_DIM), lambda i: (0, 0)),
                   vec(LANES), vec(LANES), vec(GDN_HEAD)],
        out_shape=[jax.ShapeDtypeStruct((rows_total, GDN_IN_PAD), MXU_DTYPE),
                   jax.ShapeDtypeStruct((SUBLANES, GDN_CONV_DIM), F32),
                   jax.ShapeDtypeStruct((1, LANES), F32), jax.ShapeDtypeStruct((1, LANES), F32),
                   jax.ShapeDtypeStruct((1, GDN_HEAD), F32)],
        scratch_shapes=[pltpu.VMEM((GDN_ROWS, LANES), F32),
                        pltpu.VMEM((GDN_ROWS, LANES), F32),
                        pltpu.VMEM((GDN_VH, GDN_HEAD, GDN_HEAD), F32),
                        pltpu.VMEM((LANES, CHUNK), F32),
                        pltpu.VMEM((GDN_ROWS + SUBLANES, GDN_CONV_DIM), F32),
                        pltpu.VMEM((CHUNK, LANES), F32),
                        pltpu.VMEM((LANES, CHUNK), F32),
                        pltpu.VMEM((CHUNK, LANES), F32)],
        compiler_params=_cparams(("arbitrary",)),
    )(proj, _pad_rows(conv_w), _gdn_lane_params(a_log), _gdn_lane_params(dt_bias), norm_w.reshape(1, -1),
      states, tmats, dmix, qkv, pre)
    dproj, dcw, dalog, ddtb, dnw = outs
    return dproj, [dcw[:GDN_CONV], dalog[0, GDN_GL:GDN_GL + GDN_VH], ddtb[0, GDN_GL:GDN_GL + GDN_VH], dnw[0]]


def chip_exchange(src, *, scatter, name):
    piece_shape = src.shape[1:]

    def body(src_ref, out_ref, send_sems, recv_sems, local_sem):
        x, y, c = (lax.axis_index(a) for a in MESH_AXES)
        me = 2 * x + y

        def piece(j):
            return src_ref.at[j] if scatter else src_ref.at[c]

        local = pltpu.make_async_copy(piece(me), out_ref.at[me], local_sem)
        local.start()
        copies = []
        for k in range(1, N_SHARDS):
            px = 1 - x if k & 2 else x
            py = 1 - y if k & 1 else y
            cp = pltpu.make_async_remote_copy(
                src_ref=piece(2 * px + py), dst_ref=out_ref.at[me], send_sem=send_sems.at[k - 1],
                recv_sem=recv_sems.at[k - 1], device_id=(px, py, c), device_id_type=pl.DeviceIdType.MESH)
            cp.start()
            copies.append(cp)
        for cp in copies:
            cp.wait()
        local.wait()

    hbm = pl.BlockSpec(memory_space=pl.ANY)
    return pl.pallas_call(
        body, name=name, in_specs=[hbm], out_specs=hbm,
        out_shape=jax.ShapeDtypeStruct((N_SHARDS,) + tuple(piece_shape), src.dtype),
        scratch_shapes=[pltpu.SemaphoreType.DMA((N_SHARDS - 1,)), pltpu.SemaphoreType.DMA((N_SHARDS - 1,)),
                        pltpu.SemaphoreType.DMA],
    )(src)


def pair_exchange(src, *, add, name):
    lead, rows, cols = src.shape
    tr = _pick(rows, (512, 256))
    nblk = rows // tr
    n_steps = nblk if add else lead * nblk

    def body(c_ref, *refs):
        if add:
            mine_ref, send_ref, o_ref, recv_ref, send_sems, recv_sems, credit = refs
        else:
            send_ref, o_ref, recv_ref, send_sems, recv_sems, credit = refs
        step = pl.program_id(0) * nblk + pl.program_id(1)
        slot = step % 2
        sibling = (lax.axis_index("x"), lax.axis_index("y"), 1 - lax.axis_index("c"))

        @pl.when(step >= 2)
        def _():
            pl.semaphore_wait(credit, 1)

        cp = pltpu.make_async_remote_copy(
            src_ref=send_ref, dst_ref=recv_ref.at[slot], send_sem=send_sems.at[slot], recv_sem=recv_sems.at[slot],
            device_id=sibling, device_id_type=pl.DeviceIdType.MESH)
        cp.start()
        cp.wait_recv()
        if add:
            o_ref[...] = mine_ref[...] + recv_ref[slot]
        else:
            o_ref[c_ref[0]] = send_ref[...]
            o_ref[1 - c_ref[0]] = recv_ref[slot]
        cp.wait_send()

        @pl.when(step + 2 < n_steps)
        def _():
            pl.semaphore_signal(credit, 1, device_id=sibling, device_id_type=pl.DeviceIdType.MESH)

    flat = src.reshape(lead * rows, cols)
    if add:
        in_specs = [pl.BlockSpec((tr, cols), lambda s, i, c_ref: (c_ref[0] * nblk + i, 0)),
                    pl.BlockSpec((tr, cols), lambda s, i, c_ref: ((1 - c_ref[0]) * nblk + i, 0))]
        out_specs = pl.BlockSpec((tr, cols), lambda s, i, c_ref: (i, 0))
        out_shape = jax.ShapeDtypeStruct((rows, cols), src.dtype)
        grid, args = (1, nblk), (flat, flat)
    else:
        in_specs = [pl.BlockSpec((tr, cols), lambda s, i, c_ref: (s * nblk + i, 0))]
        out_specs = pl.BlockSpec((2, tr, cols), lambda s, i, c_ref: (s, i, 0))
        out_shape = jax.ShapeDtypeStruct((lead * 2, rows, cols), src.dtype)
        grid, args = (lead, nblk), (flat,)
    out = pl.pallas_call(
        body, name=name, out_shape=out_shape,
        grid_spec=pltpu.PrefetchScalarGridSpec(
            num_scalar_prefetch=1, grid=grid, in_specs=in_specs, out_specs=out_specs,
            scratch_shapes=[pltpu.VMEM((2, tr, cols), src.dtype), pltpu.SemaphoreType.DMA((2,)),
                            pltpu.SemaphoreType.DMA((2,)), pltpu.SemaphoreType.REGULAR]),
        compiler_params=_cparams(("arbitrary", "arbitrary")),
    )(lax.axis_index("c").astype(jnp.int32).reshape(1), *args)
    return out if add else out.reshape(lead, 2, rows, cols)


def sum_slots(buf, *, name):
    n, rows, cols = buf.shape
    tr = _pick(rows, (512, 256, 128))

    def body(b_ref, o_ref):
        acc = b_ref[0]
        for j in range(1, n):
            acc = acc + b_ref[j]
        o_ref[...] = acc

    return pl.pallas_call(
        body, name=name, grid=(rows // tr,), in_specs=[pl.BlockSpec((n, tr, cols), lambda i: (0, i, 0))],
        out_specs=pl.BlockSpec((tr, cols), lambda i: (i, 0)), out_shape=jax.ShapeDtypeStruct((rows, cols), F32),
        compiler_params=_cparams(("parallel",)),
    )(buf)


def adamw(w, g, m, v, *, name):
    shape = w.shape
    cols = shape[-1]
    rows = _size(shape) // cols
    w, g, m, v = (t.reshape(rows, cols) for t in (w, g, m, v))
    tr = 256 if rows % 256 == 0 else rows

    def body(w_ref, g_ref, m_ref, v_ref, d_ref, mo_ref, vo_ref):
        gv = g_ref[...]
        mn = ADAM_B1 * m_ref[...] + (1.0 - ADAM_B1) * gv
        vn = ADAM_B2 * v_ref[...] + (1.0 - ADAM_B2) * (gv * gv)
        m_hat = mn / (1.0 - ADAM_B1 ** ADAM_STEP)
        v_hat = vn / (1.0 - ADAM_B2 ** ADAM_STEP)
        d_ref[...] = -ADAM_LR * (m_hat / (jnp.sqrt(v_hat) + ADAM_EPS) + ADAM_WD * w_ref[...])
        mo_ref[...] = mn
        vo_ref[...] = vn

    blk = pl.BlockSpec((tr, cols), lambda i: (i, 0))
    shp = jax.ShapeDtypeStruct((rows, cols), F32)
    outs = pl.pallas_call(
        body, name=name, grid=(rows // tr,), in_specs=[blk] * 4, out_specs=[blk] * 3, out_shape=[shp] * 3,
        compiler_params=_cparams(("parallel",)),
    )(w, g, m, v)
    return [o.reshape(shape) for o in outs]


N_SHARDS = 4
FLAT_COLS = 1024
W_SPECS = (
    ("gdn_w_in", (2, 1024, 6176), 2), ("gdn_conv_w", (2, 4, 4096), 2), ("gdn_a_log", (2, 16), None),
    ("gdn_dt_bias", (2, 16), None), ("gdn_norm_w", (2, 128), None), ("gdn_w_out", (2, 2048, 1024), 1),
    ("sc_w_in", (1, 1024, 8192), 2), ("sc_conv_w", (1, 3, 2048), 2), ("sc_w_out", (1, 2048, 1024), 1),
    ("ssd_w_in", (1, 1024, 5152), 2), ("ssd_conv_w", (1, 4, 3072), 2), ("ssd_conv_b", (1, 3072), 1),
    ("ssd_a_log", (1, 32), None), ("ssd_dt_bias", (1, 32), None), ("ssd_d_skip", (1, 32), None),
    ("ssd_norm_w", (1, 2048), 1), ("ssd_w_out", (1, 2048, 1024), 1), ("ln_g", (4, 1024), None), ("ln_b", (4, 1024), None),
)


def _local_shape(shape, axis):
    return shape if axis is None else tuple(d // N_SHARDS if i == axis else d for i, d in enumerate(shape))


def _size(shape):
    n = 1
    for d in shape:
        n *= d
    return n


PIECE_ROWS = 16


def _piece_rows(shape, axis):
    return -(-_size(_local_shape(shape, axis)) // (FLAT_COLS * PIECE_ROWS)) * PIECE_ROWS


def _flat_rows(specs):
    return -(-sum(_piece_rows(s, a) for _, s, a in specs) // 512) * 512


FLAT_ROWS = _flat_rows(W_SPECS)
FLAT_HALF = FLAT_ROWS // 2


def _pack(pieces, specs=W_SPECS, dtype=F32):
    blocks, used = [], 0
    for p, (_, shape, axis) in zip(pieces, specs):
        rows = _piece_rows(shape, axis)
        flat = p.reshape(-1).astype(dtype)
        if flat.shape[0] < rows * FLAT_COLS:
            flat = jnp.pad(flat, (0, rows * FLAT_COLS - flat.shape[0]))
        blocks.append(flat.reshape(rows, FLAT_COLS))
        used += rows
    blocks.append(jnp.zeros((_flat_rows(specs) - used, FLAT_COLS), dtype))
    return jnp.concatenate(blocks, axis=0)


def _unpack(flat, specs=W_SPECS):
    out, off = [], 0
    for _, shape, axis in specs:
        ls = _local_shape(shape, axis)
        rows = _piece_rows(shape, axis)
        out.append(flat[off:off + rows].reshape(-1)[:_size(ls)].reshape(ls))
        off += rows
    return out


def _shard_of(full, axis, s):
    if axis is None:
        return full
    n = full.shape[axis] // N_SHARDS
    return lax.slice_in_dim(full, s * n, (s + 1) * n, axis=axis)


def _adamw_all(weights, grads_flat, moms, vels):
    grads = _unpack(grads_flat)
    steps = [adamw(w, g, m, v, name="adamw") for w, g, m, v in zip(weights, grads, moms, vels)]
    return grads, [s[0] for s in steps], [s[1] for s in steps], [s[2] for s in steps]


SPLIT_ROWS = 128


def shard_split(w, n_real, *, name):
    rows, n_pad = w.shape
    ns = n_real // N_SHARDS

    def body(w_ref, o_ref):
        for s in range(N_SHARDS):
            o_ref[s] = w_ref[:, s * ns:(s + 1) * ns]

    return pl.pallas_call(
        body, name=name, grid=(rows // SPLIT_ROWS,),
        in_specs=[pl.BlockSpec((SPLIT_ROWS, n_pad), lambda i: (i, 0))],
        out_specs=pl.BlockSpec((N_SHARDS, SPLIT_ROWS, ns), lambda i: (0, i, 0)),
        out_shape=jax.ShapeDtypeStruct((N_SHARDS, rows, ns), F32), compiler_params=_cparams(("parallel",)),
    )(w)


def shard_merge(pieces, n_pad, *, name):
    _, rows, ns = pieces.shape
    n_real = ns * N_SHARDS

    def body(p_ref, o_ref):
        for s in range(N_SHARDS):
            o_ref[:, s * ns:(s + 1) * ns] = p_ref[s].astype(o_ref.dtype)
        if n_pad > n_real:
            o_ref[:, n_real:] = jnp.zeros((SPLIT_ROWS, n_pad - n_real), o_ref.dtype)

    return pl.pallas_call(
        body, name=name, grid=(rows // SPLIT_ROWS,),
        in_specs=[pl.BlockSpec((N_SHARDS, SPLIT_ROWS, ns), lambda i: (0, i, 0))],
        out_specs=pl.BlockSpec((SPLIT_ROWS, n_pad), lambda i: (i, 0)),
        out_shape=jax.ShapeDtypeStruct((rows, n_pad), MXU_DTYPE), compiler_params=_cparams(("parallel",)),
    )(pieces)


def _reduce_scatter(full_grads):
    def shard(g, spec, s):
        _, shape, axis = spec
        return g[:, s] if g.ndim == len(shape) + 1 else _shard_of(g, axis, s)

    by_shard = jnp.stack([_pack([shard(g, spec, s) for g, spec in zip(full_grads, W_SPECS)])
                          for s in range(N_SHARDS)])
    by_half = by_shard.reshape(N_SHARDS, 2, FLAT_HALF, FLAT_COLS).transpose(1, 0, 2, 3)
    by_half = by_half.reshape(2, N_SHARDS * FLAT_HALF, FLAT_COLS)
    pair_sum = pair_exchange(by_half, add=True, name="rs_pair")
    chips = chip_exchange(pair_sum.reshape(N_SHARDS, FLAT_HALF, FLAT_COLS), scatter=True, name="rs_chips")
    half = sum_slots(chips, name="rs_chip_sum")
    return pair_exchange(half[None], add=False, name="rs_halves").reshape(FLAT_ROWS, FLAT_COLS)


def _gather_weights(local_weights):
    def gather(idx, dtype, tag):
        specs = [W_SPECS[i] for i in idx]
        rows = _flat_rows(specs)
        flat = _pack([local_weights[i] for i in idx], specs, dtype)
        halves = chip_exchange(flat.reshape(2, rows // 2, FLAT_COLS), scatter=False, name="gather_chips_" + tag)
        both = pair_exchange(halves, add=False, name="gather_pair_" + tag).reshape(N_SHARDS, rows, FLAT_COLS)
        return [dict(zip(idx, _unpack(both[s], specs))) for s in range(N_SHARDS)]

    matrices = [i for i, (n, _, _) in enumerate(W_SPECS) if n in MXU_WEIGHTS]
    vectors = [i for i, (n, _, a) in enumerate(W_SPECS) if n not in MXU_WEIGHTS and a is not None]
    per_shard = [{**m, **v} for m, v in zip(gather(matrices, MXU_DTYPE, "mxu"), gather(vectors, F32, "f32"))]
    full = []
    for i, (wname, shape, axis) in enumerate(W_SPECS):
        if axis is None:
            full.append(local_weights[i])
        elif wname in W_IN_PAD:
            pieces = jnp.stack([per_shard[s][i] for s in range(N_SHARDS)], axis=1)
            full.append([shard_merge(pieces[j], W_IN_PAD[wname], name="merge_" + wname) for j in range(shape[0])])
        else:
            full.append(jnp.concatenate([per_shard[s][i] for s in range(N_SHARDS)], axis=axis))
    return full


W_IN_PAD = {"gdn_w_in": GDN_IN_PAD, "sc_w_in": SC_IN, "ssd_w_in": SSD_IN_PAD}
MXU_WEIGHTS = ("gdn_w_in", "gdn_w_out", "sc_w_in", "sc_w_out", "ssd_w_in", "ssd_w_out")


def kernel(x, gdn_w_in, gdn_conv_w, gdn_a_log, gdn_dt_bias, gdn_norm_w, gdn_w_out, sc_w_in, sc_conv_w, sc_w_out, ssd_w_in, ssd_conv_w, ssd_conv_b, ssd_a_log, ssd_dt_bias, ssd_d_skip, ssd_norm_w, ssd_w_out, ln_g, ln_b, loss_target, m_gdn_w_in, m_gdn_conv_w, m_gdn_a_log, m_gdn_dt_bias, m_gdn_norm_w, m_gdn_w_out, m_sc_w_in, m_sc_conv_w, m_sc_w_out, m_ssd_w_in, m_ssd_conv_w, m_ssd_conv_b, m_ssd_a_log, m_ssd_dt_bias, m_ssd_d_skip, m_ssd_norm_w, m_ssd_w_out, m_ln_g, m_ln_b, v_gdn_w_in, v_gdn_conv_w, v_gdn_a_log, v_gdn_dt_bias, v_gdn_norm_w, v_gdn_w_out, v_sc_w_in, v_sc_conv_w, v_sc_w_out, v_ssd_w_in, v_ssd_conv_w, v_ssd_conv_b, v_ssd_a_log, v_ssd_dt_bias, v_ssd_d_skip, v_ssd_norm_w, v_ssd_w_out, v_ln_g, v_ln_b):
    weights = [gdn_w_in, gdn_conv_w, gdn_a_log, gdn_dt_bias, gdn_norm_w, gdn_w_out, sc_w_in, sc_conv_w, sc_w_out,
               ssd_w_in, ssd_conv_w, ssd_conv_b, ssd_a_log, ssd_dt_bias, ssd_d_skip, ssd_norm_w, ssd_w_out, ln_g, ln_b]
    moms = [m_gdn_w_in, m_gdn_conv_w, m_gdn_a_log, m_gdn_dt_bias, m_gdn_norm_w, m_gdn_w_out, m_sc_w_in, m_sc_conv_w,
            m_sc_w_out, m_ssd_w_in, m_ssd_conv_w, m_ssd_conv_b, m_ssd_a_log, m_ssd_dt_bias, m_ssd_d_skip, m_ssd_norm_w,
            m_ssd_w_out, m_ln_g, m_ln_b]
    vels = [v_gdn_w_in, v_gdn_conv_w, v_gdn_a_log, v_gdn_dt_bias, v_gdn_norm_w, v_gdn_w_out, v_sc_w_in, v_sc_conv_w,
            v_sc_w_out, v_ssd_w_in, v_ssd_conv_w, v_ssd_conv_b, v_ssd_a_log, v_ssd_dt_bias, v_ssd_d_skip, v_ssd_norm_w,
            v_ssd_w_out, v_ln_g, v_ln_b]
    full = dict(zip([n for n, _, _ in W_SPECS], _gather_weights(weights)))
    x0 = x[0]
    target = loss_target[0]

    layers = (("gdn", 0, GDN_IN_PAD, GDN_IN), ("sc", 0, SC_IN, SC_IN), ("ssd", 0, SSD_IN_PAD, SSD_IN), ("gdn", 1, GDN_IN_PAD, GDN_IN))

    def params(kind, j):
        if kind == "gdn":
            return [full["gdn_conv_w"][j], full["gdn_a_log"][j], full["gdn_dt_bias"][j], full["gdn_norm_w"][j]]
        if kind == "sc":
            return [full["sc_conv_w"][j]]
        return [full["ssd_conv_w"][j], full["ssd_conv_b"][j], full["ssd_a_log"][j], full["ssd_dt_bias"][j],
                full["ssd_d_skip"][j], full["ssd_norm_w"][j]]

    xs, saved = [x0], []
    for i, (kind, j, n_pad, _) in enumerate(layers):
        w_in = full[kind + "_w_in"][j]
        w_out = full[kind + "_w_out"][j].astype(MXU_DTYPE)
        proj = matmul(xs[i], w_in, name=kind + "_proj")
        if kind == "gdn":
            mix, states = gdn_fwd(proj, *params(kind, j), name="gdn_fwd")
        elif kind == "sc":
            mix, states = sc_fwd(proj, *params(kind, j), name="sc_fwd"), None
        else:
            mix, states = ssd_fwd(proj, *params(kind, j), name="ssd_fwd")
        y = matmul(mix, w_out, name=kind + "_out")
        saved.append((w_in, w_out, proj, mix, states, y))
        if i + 1 < DEPTH:
            xs.append(ln_fwd(xs[i], y, full["ln_g"][i], full["ln_b"][i], name="ln_fwd"))

    grads = {n: [None] * s[0] for n, s, _ in W_SPECS}
    dr, dg, db, loss_rows = ln_bwd(xs[DEPTH - 1], saved[DEPTH - 1][5], full["ln_g"][DEPTH - 1], b=full["ln_b"][DEPTH - 1],
                                   target=target, name="ln_bwd_loss")
    dx = None
    for i in reversed(range(DEPTH)):
        kind, j, _, n_in = layers[i]
        w_in, w_out, proj, mix, states, _ = saved[i]
        grads["ln_g"][i], grads["ln_b"][i] = dg[0], db[0]
        dmix = matmul(dr, w_out, tb=True, name=kind + "_dmix")
        grads[kind + "_w_out"][j] = matmul(mix, dr, ta=True, name=kind + "_dw_out")
        if kind == "gdn":
            dproj, (dcw, dalog, ddtb, dnw) = gdn_bwd(proj, *params(kind, j), states, dmix, name="gdn_bwd")
            grads["gdn_conv_w"][j], grads["gdn_a_log"][j], grads["gdn_dt_bias"][j], grads["gdn_norm_w"][j] = dcw, dalog, ddtb, dnw
        elif kind == "sc":
            dproj, dcw = sc_bwd(proj, *params(kind, j), dmix, name="sc_bwd")
            grads["sc_conv_w"][j] = dcw[:SC_CONV]
        else:
            conv_w, _, *rest = params(kind, j)
            dproj, (dcw, dcb, dalog, ddtb, ddsk, dnw) = ssd_bwd(proj, conv_w, *rest, states, dmix, name="ssd_bwd")
            grads["ssd_conv_w"][j], grads["ssd_conv_b"][j], grads["ssd_a_log"][j] = dcw, dcb, dalog
            grads["ssd_dt_bias"][j], grads["ssd_d_skip"][j], grads["ssd_norm_w"][j] = ddtb, ddsk, dnw
        grads[kind + "_w_in"][j] = shard_split(matmul(xs[i], dproj, ta=True, name=kind + "_dw_in"), n_in, name="split_" + kind)
        dx = matmul(dproj, w_in, tb=True, add=dr, add_scale=ALPHA, name=kind + "_dx")
        if i > 0:
            dr, dg, db = ln_bwd(xs[i - 1], saved[i - 1][5], full["ln_g"][i - 1], dx, name="ln_bwd")

    full_grads = [jnp.stack(grads[n]) for n, _, _ in W_SPECS]
    grads_flat = _reduce_scatter(full_grads)
    g_out, d_out, m_out, v_out = _adamw_all(weights, grads_flat, moms, vels)
    loss = lax.psum(loss_rows[0, 0], MESH_AXES)
    return (loss, dx[None], *g_out, *d_out, *m_out, *v_out)
```

```python
import functools

import jax
import jax.numpy as jnp
from jax import lax
from jax.experimental import pallas as pl
from jax.experimental.pallas import tpu as pltpu

F32 = jnp.float32
MXU_DTYPE = jnp.bfloat16

D_MODEL = 1024
DEPTH = 4
D_INNER = 2048
CHUNK = 64
LANES = 128
SUBLANES = 8
VMEM_LIMIT = 56 * 1024 * 1024

GDN_HEAD = 128
GDN_VH = 16
GDN_QKH = 8
GDN_QK = 1024
GDN_V = 2048
GDN_CONV = 4
GDN_CONV_DIM = 4096
GDN_IN = 6176
GDN_IN_PAD = 6272

SC_W = 2048
SC_CONV = 3
SC_IN = 8192

SSD_P = 64
SSD_H = 32
SSD_G = 4
SSD_S = 128
SSD_CONV = 4
SSD_CONV_DIM = 3072
SSD_IN = 5152
SSD_IN_PAD = 5376

ALPHA = (2 * DEPTH) ** 0.25
RMS_EPS = 1e-6
LN_EPS = 1e-5
L2_EPS = 1e-6

ADAM_LR = 0.001
ADAM_B1 = 0.9
ADAM_B2 = 0.999
ADAM_EPS = 1e-08
ADAM_WD = 0.01
ADAM_STEP = 10

MESH_AXES = ("x", "y", "c")


def _cparams(sem):
    return pltpu.CompilerParams(dimension_semantics=sem, vmem_limit_bytes=VMEM_LIMIT)


def _pick(n, prefs):
    for p in prefs:
        if n % p == 0:
            return p
    return n


def _dot(a, b, dims=(((1,), (0,)), ((), ()))):
    return lax.dot_general(a.astype(MXU_DTYPE), b.astype(MXU_DTYPE), dims, preferred_element_type=F32)


def _dot_nt(a, b):
    return _dot(a, b, (((1,), (1,)), ((), ())))


def _dot_tn(a, b):
    return _dot(a, b, (((0,), (0,)), ((), ())))


NN = (((1,), (0,)), ((), ()))
NT = (((1,), (1,)), ((), ()))
TN = (((0,), (0,)), ((), ()))


def _mxu(a, b, dims):
    return lax.dot_general(a, b, dims, preferred_element_type=F32)


def _split(x, pieces):
    out, r = [], x
    for i in range(pieces):
        p = r.astype(jnp.bfloat16)
        out.append(p)
        if i + 1 < pieces:
            r = r - p.astype(F32)
    return out


def _sel(m, x, dims=NN):
    mb = m.astype(jnp.bfloat16)
    x1, x2, x3 = _split(x, 3)
    return (_mxu(mb, x3, dims) + _mxu(mb, x2, dims)) + _mxu(mb, x1, dims)


def _sel_nt(m, x):
    return _sel(m, x, NT)


def _xsel(x, m, dims=NN):
    mb = m.astype(jnp.bfloat16)
    x1, x2, x3 = _split(x, 3)
    return (_mxu(x3, mb, dims) + _mxu(x2, mb, dims)) + _mxu(x1, mb, dims)


def _xsel_nt(x, m):
    return _xsel(x, m, NT)


def _iota(shape, dim):
    return lax.broadcasted_iota(jnp.int32, shape, dim)


def _sigmoid(x):
    return 0.5 * jnp.tanh(0.5 * x) + 0.5


def _silu(x):
    return x * _sigmoid(x)


def _dsilu(x):
    s = _sigmoid(x)
    return s * (1.0 + x * (1.0 - s))


def _softplus(x):
    return jnp.maximum(x, 0.0) + jnp.log(1.0 + jnp.exp(-jnp.abs(x)))


def matmul(a, b, *, ta=False, tb=False, add=None, add_scale=1.0, name):
    if ta:
        kdim, m = a.shape
    else:
        m, kdim = a.shape
    n = b.shape[0] if tb else b.shape[1]
    assert (b.shape[1] if tb else b.shape[0]) == kdim
    tm = _pick(m, (1024, 896, 768, 512)) if ta else _pick(m, (2048, 1024, 512, 256, 128))
    tn = _pick(n, (1024, 896, 768, 512, 256, 128))
    tk = _pick(kdim, (1024, 512, 256)) if ta else _pick(kdim, (1024, 896, 768, 512))
    nk = kdim // tk
    dims = (((0 if ta else 1,), (1 if tb else 0,)), ((), ()))

    def body(a_ref, b_ref, *rest):
        o_ref = rest[-1]
        k = pl.program_id(2)
        part = _dot(a_ref[...], b_ref[...], dims)

        @pl.when(k == 0)
        def _():
            o_ref[...] = part if add is None else part + add_scale * rest[0][...]

        @pl.when(k > 0)
        def _():
            o_ref[...] += part

    a_spec = pl.BlockSpec((tk, tm), lambda i, j, k: (k, i)) if ta else pl.BlockSpec((tm, tk), lambda i, j, k: (i, k))
    b_spec = pl.BlockSpec((tn, tk), lambda i, j, k: (j, k)) if tb else pl.BlockSpec((tk, tn), lambda i, j, k: (k, j))
    o_spec = pl.BlockSpec((tm, tn), lambda i, j, k: (i, j))
    in_specs = [a_spec, b_spec] + ([] if add is None else [o_spec])
    args = (a, b) + (() if add is None else (add,))
    return pl.pallas_call(
        body, name=name, grid=(m // tm, n // tn, nk), in_specs=in_specs, out_specs=o_spec,
        out_shape=jax.ShapeDtypeStruct((m, n), F32),
        compiler_params=_cparams(("parallel", "parallel", "arbitrary")),
    )(*args)


LN_ROWS = 512


def _ln_stats(x, y):
    r = ALPHA * x + y
    mu = jnp.mean(r, axis=-1, keepdims=True)
    rc = r - mu
    var = jnp.mean(rc * rc, axis=-1, keepdims=True)
    rstd = lax.rsqrt(var + LN_EPS)
    return rc * rstd, rstd


def ln_fwd(x, y, g, b, *, name):
    rows, d = x.shape

    def body(x_ref, y_ref, g_ref, b_ref, o_ref):
        xhat, _ = _ln_stats(x_ref[...], y_ref[...])
        o_ref[...] = xhat * g_ref[...] + b_ref[...]

    blk = pl.BlockSpec((LN_ROWS, d), lambda i: (i, 0))
    vec = pl.BlockSpec((1, d), lambda i: (0, 0))
    return pl.pallas_call(
        body, name=name, grid=(rows // LN_ROWS,), in_specs=[blk, blk, vec, vec], out_specs=blk,
        out_shape=jax.ShapeDtypeStruct((rows, d), F32), compiler_params=_cparams(("parallel",)),
    )(x, y, g.reshape(1, d), b.reshape(1, d))


def ln_bwd(x, y, g, dxn=None, *, b=None, target=None, name):
    rows, d = x.shape
    final = target is not None

    def body(x_ref, y_ref, g_ref, *rest):
        if final:
            b_ref, t_ref, dr_ref, dg_ref, db_ref, loss_ref = rest
        else:
            dxn_ref, dr_ref, dg_ref, db_ref = rest
        i = pl.program_id(0)
        xhat, rstd = _ln_stats(x_ref[...], y_ref[...])
        gv = g_ref[...]
        if final:
            err = xhat * gv + b_ref[...] - t_ref[...]
            dxn_v = err * (1.0 / d)
            part = 0.5 * jnp.sum(jnp.mean(err * err, axis=-1, keepdims=True), axis=0, keepdims=True)
        else:
            dxn_v = dxn_ref[...]
        dxh = dxn_v * gv
        m1 = jnp.mean(dxh, axis=-1, keepdims=True)
        m2 = jnp.mean(dxh * xhat, axis=-1, keepdims=True)
        dr_ref[...] = rstd * (dxh - m1 - xhat * m2)

        @pl.when(i == 0)
        def _():
            dg_ref[...] = jnp.zeros_like(dg_ref)
            db_ref[...] = jnp.zeros_like(db_ref)
            if final:
                loss_ref[...] = jnp.zeros_like(loss_ref)

        dg_ref[...] += jnp.sum(dxn_v * xhat, axis=0, keepdims=True)
        db_ref[...] += jnp.sum(dxn_v, axis=0, keepdims=True)
        if final:
            loss_ref[...] += jnp.broadcast_to(part, loss_ref.shape)

    blk = pl.BlockSpec((LN_ROWS, d), lambda i: (i, 0))
    vec = pl.BlockSpec((1, d), lambda i: (0, 0))
    lvec = pl.BlockSpec((1, LANES), lambda i: (0, 0))
    out_shape = [jax.ShapeDtypeStruct((rows, d), F32), jax.ShapeDtypeStruct((1, d), F32), jax.ShapeDtypeStruct((1, d), F32)]
    out_specs = [blk, vec, vec]
    if final:
        in_specs = [blk, blk, vec, vec, blk]
        args = (x, y, g.reshape(1, d), b.reshape(1, d), target)
        out_shape.append(jax.ShapeDtypeStruct((1, LANES), F32))
        out_specs.append(lvec)
    else:
        in_specs = [blk, blk, vec, blk]
        args = (x, y, g.reshape(1, d), dxn)
    return pl.pallas_call(
        body, name=name, grid=(rows // LN_ROWS,), in_specs=in_specs, out_specs=out_specs, out_shape=out_shape,
        compiler_params=_cparams(("arbitrary",)),
    )(*args)


def _rows_from(ref, off, rows, cols=slice(None)):
    r = off % SUBLANES
    if r == 0:
        return ref[off:off + rows, cols]
    window = ref[off - r:off - r + rows + SUBLANES, cols]
    return pltpu.roll(window, rows + SUBLANES - r, axis=0)[:rows]


def _conv_from_ext(ext_ref, w, width, rows, cols=slice(None)):
    out = None
    for j in range(width):
        term = _rows_from(ext_ref, SUBLANES - (width - 1) + j, rows, cols) * w[j:j + 1, cols]
        out = term if out is None else out + term
    return out


def _conv_dgrad_from_ext(dext_ref, w, width, rows, cols):
    out = None
    for j in range(width):
        term = _rows_from(dext_ref, (width - 1) - j, rows, cols) * w[j:j + 1, cols]
        out = term if out is None else out + term
    return out


def _conv_bwd_from_ext(dext_ref, u, w, width, rows, cols):
    du, dws = None, []
    for j in range(width):
        shifted = _rows_from(dext_ref, (width - 1) - j, rows, cols)
        term = shifted * w[j:j + 1, cols]
        du = term if du is None else du + term
        dws.append(jnp.sum(shifted * u, axis=0, keepdims=True))
    return du, dws


CONV_COLS = 256


SC_ROWS = 128


def sc_fwd(proj, conv_w, *, name):
    rows = proj.shape[0]
    nb = rows // SC_ROWS
    hb = SC_ROWS // SUBLANES

    def body(p_ref, halo_ref, w_ref, o_ref, ext_ref):
        i = pl.program_id(0)
        w = w_ref[...]
        for c0 in range(0, SC_W, CONV_COLS):
            cols, bc, cc, zc = (slice(k * SC_W + c0, k * SC_W + c0 + CONV_COLS) for k in range(4))
            ext_ref[0:SUBLANES, cols] = jnp.where(i == 0, 0.0, halo_ref[:, cc] * halo_ref[:, cols])
            ext_ref[SUBLANES:, cols] = p_ref[:, cc] * p_ref[:, cols]
            cv = _conv_from_ext(ext_ref, w, SC_CONV, SC_ROWS, cols)
            o_ref[:, cols] = (p_ref[:, bc] * cv * _silu(p_ref[:, zc])).astype(o_ref.dtype)

    return pl.pallas_call(
        body, name=name, grid=(nb,),
        in_specs=[pl.BlockSpec((SC_ROWS, SC_IN), lambda i: (i, 0)),
                  pl.BlockSpec((SUBLANES, SC_IN), lambda i: (jnp.maximum(i * hb - 1, 0), 0)),
                  pl.BlockSpec((SUBLANES, SC_W), lambda i: (0, 0))],
        out_specs=pl.BlockSpec((SC_ROWS, SC_W), lambda i: (i, 0)),
        out_shape=jax.ShapeDtypeStruct((rows, SC_W), MXU_DTYPE),
        scratch_shapes=[pltpu.VMEM((SC_ROWS + SUBLANES, SC_W), F32)],
        compiler_params=_cparams(("parallel",)),
    )(proj, proj, _pad_rows(conv_w))


def sc_bwd(proj, conv_w, dmix, *, name):
    rows = proj.shape[0]
    nb = rows // SC_ROWS
    hb = SC_ROWS // SUBLANES

    def body(p_ref, halo_ref, w_ref, dm_ref, dp_ref, dw_ref, ext_ref, dext_ref):
        i = pl.program_id(0)
        blk = nb - 1 - i
        w = w_ref[...]

        @pl.when(i == 0)
        def _():
            dext_ref[SC_ROWS:, :] = jnp.zeros((SUBLANES, SC_W), F32)
            dw_ref[...] = jnp.zeros_like(dw_ref)

        for c0 in range(0, SC_W, CONV_COLS):
            cols, bc, cc, zc = (slice(k * SC_W + c0, k * SC_W + c0 + CONV_COLS) for k in range(4))
            h, bg, cg, z = p_ref[:, cols], p_ref[:, bc], p_ref[:, cc], p_ref[:, zc]
            ext_ref[0:SUBLANES, cols] = jnp.where(blk == 0, 0.0, halo_ref[:, cc] * halo_ref[:, cols])
            ext_ref[SUBLANES:, cols] = cg * h
            taps = [_rows_from(ext_ref, SUBLANES - (SC_CONV - 1) + j, SC_ROWS, cols) for j in range(SC_CONV)]
            cv = None
            for j in range(SC_CONV):
                term = taps[j] * w[j:j + 1, cols]
                cv = term if cv is None else cv + term
            dm = dm_ref[:, cols]
            dy = dm * _silu(z)
            dp_ref[:, zc] = (dm * bg * cv * _dsilu(z)).astype(dp_ref.dtype)
            dp_ref[:, bc] = (dy * cv).astype(dp_ref.dtype)
            dcv = dy * bg
            dext_ref[0:SC_ROWS, cols] = dcv
            du = _conv_dgrad_from_ext(dext_ref, w, SC_CONV, SC_ROWS, cols)
            dp_ref[:, cols] = (du * cg).astype(dp_ref.dtype)
            dp_ref[:, cc] = (du * h).astype(dp_ref.dtype)
            for j in range(SC_CONV):
                dw_ref[j:j + 1, cols] += jnp.sum(taps[j] * dcv, axis=0, keepdims=True)
            dext_ref[SC_ROWS:, cols] = dcv[0:SUBLANES, :]

    return pl.pallas_call(
        body, name=name, grid=(nb,),
        in_specs=[pl.BlockSpec((SC_ROWS, SC_IN), lambda i: (nb - 1 - i, 0)),
                  pl.BlockSpec((SUBLANES, SC_IN), lambda i: (jnp.maximum((nb - 1 - i) * hb - 1, 0), 0)),
                  pl.BlockSpec((SUBLANES, SC_W), lambda i: (0, 0)),
                  pl.BlockSpec((SC_ROWS, SC_W), lambda i: (nb - 1 - i, 0))],
        out_specs=[pl.BlockSpec((SC_ROWS, SC_IN), lambda i: (nb - 1 - i, 0)),
                   pl.BlockSpec((SUBLANES, SC_W), lambda i: (0, 0))],
        out_shape=[jax.ShapeDtypeStruct((rows, SC_IN), MXU_DTYPE), jax.ShapeDtypeStruct((SUBLANES, SC_W), F32)],
        scratch_shapes=[pltpu.VMEM((SC_ROWS + SUBLANES, SC_W), F32), pltpu.VMEM((SC_ROWS + SUBLANES, SC_W), F32)],
        compiler_params=_cparams(("arbitrary",)),
    )(proj, proj, _pad_rows(conv_w), dmix)


def _pad_rows(w, rows=SUBLANES):
    return jnp.pad(w, ((0, rows - w.shape[0]), (0, 0)))


def _pad_lanes(v, lanes=LANES):
    v = v.reshape(1, -1)
    return jnp.pad(v, ((0, 0), (0, lanes - v.shape[1])))


def _tri(n, lower=True):
    r, c = _iota((n, n), 0), _iota((n, n), 1)
    return jnp.where((c <= r) if lower else (c >= r), 1.0, 0.0)


def _eye(n):
    return jnp.where(_iota((n, n), 0) == _iota((n, n), 1), 1.0, 0.0)


def _head_expand(n, width):
    return jnp.where(_iota((LANES, n), 1) // width == _iota((LANES, n), 0), 1.0, 0.0)


def _col(v, h):
    return jnp.sum(jnp.where(_iota(v.shape, 1) == h, v, 0.0), axis=1, keepdims=True)


def _row(v, r):
    return jnp.sum(jnp.where(_iota(v.shape, 0) == r, v, 0.0), axis=0, keepdims=True)


def _expand_row(v, e):
    return jnp.max(_xsel(jnp.broadcast_to(v, (SUBLANES, LANES)), e), axis=0, keepdims=True)


SSD_ROWS = 128
SSD_X0 = D_INNER
SSD_DT0 = D_INNER + SSD_CONV_DIM
SSD_B0 = D_INNER
SSD_C0 = D_INNER + SSD_G * SSD_S
SSD_GW = D_INNER // SSD_G
SSD_HG = SSD_H // SSD_G


def _ssd_prologue(blk, p_ref, halo_ref, cw_ref, cb_ref, dtb_ref, ext_ref, xbc_ref, dt_ref, pre_ref=None):
    ext_ref[0:SUBLANES, :] = jnp.where(blk == 0, 0.0, halo_ref[:, SSD_X0:SSD_DT0])
    ext_ref[SUBLANES:, :] = p_ref[:, SSD_X0:SSD_DT0]
    w = cw_ref[...]
    for c0 in range(0, SSD_CONV_DIM, CONV_COLS):
        cols = slice(c0, c0 + CONV_COLS)
        pre = _conv_from_ext(ext_ref, w, SSD_CONV, SSD_ROWS, cols) + cb_ref[:, cols]
        if pre_ref is not None:
            pre_ref[:, cols] = pre
        xbc_ref[:, cols] = _silu(pre)
    dt_ref[...] = _softplus(p_ref[:, SSD_DT0:SSD_DT0 + LANES] + dtb_ref[...])


def _ssd_chunk_decays(dt_c, a_row, ltri, eye_l, act_ref):
    da = dt_c * a_row
    ac = _sel(ltri,da)
    act_ref[...] = _sel_nt(eye_l,ac)
    ac_last = _row(ac, CHUNK - 1)
    return ac, jnp.exp(ac_last - ac), jnp.exp(ac), jnp.exp(ac_last)


def _ssd_seg(ac, act_ref, h, causal):
    return jnp.where(causal, jnp.exp(jnp.minimum(_col(ac, h) - act_ref[pl.ds(h, 1), :], 0.0)), 0.0)


def _ssd_half(pair, e):
    upper = _iota(pair.shape, 1) >= SSD_P
    return jnp.where(upper if e % 2 else jnp.logical_not(upper), pair, 0.0)


def _ssd_groups_fwd(xbc_ref, rows, dt_exp, tail_exp, cdec_exp, ac, act_ref, states, causal):
    gs, heads = range(SSD_G), range(SSD_HG)
    gls = [slice(g * SSD_GW, (g + 1) * SSD_GW) for g in gs]
    bg = [_mx(xbc_ref[rows, SSD_B0 + g * SSD_S:SSD_B0 + (g + 1) * SSD_S]) for g in gs]
    cg = [_mx(xbc_ref[rows, SSD_C0 + g * SSD_S:SSD_C0 + (g + 1) * SSD_S]) for g in gs]
    s_c = [_mx(s) for s in states]
    xdt = [xbc_ref[rows, gl] * dt_exp[:, gl] for gl in gls]
    cb = [_dot_nt(cg[g], bg[g]) for g in gs]
    cs = [_dot(cg[g], s_c[g]) for g in gs]
    segs = [[_ssd_seg(ac, act_ref, g * SSD_HG + e, causal) for e in heads] for g in gs]
    gms = [[seg * cb[g] for seg in segs[g]] for g in gs]
    gms_c = [[_mx(gm) for gm in gms[g]] for g in gs]
    halves = [[_ssd_half(xdt[g][:, (e // 2) * LANES:(e // 2 + 1) * LANES], e) for e in heads] for g in gs]
    parts = [[_dot(gms_c[g][e], halves[g][e]) for e in heads] for g in gs]
    yd = [jnp.concatenate([parts[g][2 * p] + parts[g][2 * p + 1] for p in range(SSD_HG // 2)], axis=1) for g in gs]
    st = [_dot_tn(bg[g], xdt[g] * tail_exp[:, gls[g]]) for g in gs]
    return [(yd[g] + cs[g] * cdec_exp[:, gls[g]], st[g], bg[g], cg[g], s_c[g], cb[g], xdt[g], cs[g],
             segs[g], gms[g], gms_c[g]) for g in gs]


def ssd_fwd(proj, conv_w, conv_b, a_log, dt_bias, d_skip, norm_w, *, name):
    rows_total = proj.shape[0]
    nb = rows_total // SSD_ROWS
    hb = SSD_ROWS // SUBLANES
    cpb = SSD_ROWS // CHUNK

    def body(p_ref, halo_ref, cw_ref, cb_ref, alog_ref, dtb_ref, dsk_ref, nw_ref, mix_ref, st_ref, xbc_ref, pre_ref,
             ext_ref, dt_ref, s_ref, act_ref):
        i = pl.program_id(0)

        @pl.when(i == 0)
        def _():
            s_ref[...] = jnp.zeros_like(s_ref)

        _ssd_prologue(i, p_ref, halo_ref, cw_ref, cb_ref, dtb_ref, ext_ref, xbc_ref, dt_ref, pre_ref)
        a_row = -jnp.exp(alog_ref[...])
        expand = _head_expand(D_INNER, SSD_P)
        dsk_exp = _expand_row(dsk_ref[...], expand)
        ltri, eye_l = _tri(CHUNK), _eye(LANES)
        causal = _iota((CHUNK, CHUNK), 1) <= _iota((CHUNK, CHUNK), 0)

        def chunk(c, carry):
            rows = pl.ds(pl.multiple_of(c * CHUNK, CHUNK), CHUNK)
            dt_c = dt_ref[rows, :]
            ac, tail, cdec, tot = _ssd_chunk_decays(dt_c, a_row, ltri, eye_l, act_ref)
            dt_exp = _xsel(dt_c, expand)
            tail_exp = _xsel(tail, expand)
            cdec_exp = _xsel(cdec, expand)
            tot_exp = _expand_row(tot, expand)
            states = [s_ref[g] for g in range(SSD_G)]
            fwd = _ssd_groups_fwd(xbc_ref, rows, dt_exp, tail_exp, cdec_exp, ac, act_ref, states, causal)
            for g in range(SSD_G):
                gl = slice(g * SSD_GW, (g + 1) * SSD_GW)
                st_ref[c, g] = states[g]
                y, st = fwd[g][:2]
                s_ref[g] = states[g] * tot_exp[:, gl] + st
                y = (y + dsk_exp[:, gl] * xbc_ref[rows, gl]) * _silu(p_ref[rows, gl])
                r = lax.rsqrt(jnp.mean(y * y, axis=-1, keepdims=True) + RMS_EPS)
                mix_ref[rows, gl] = (y * r * nw_ref[:, gl]).astype(mix_ref.dtype)
            return carry

        lax.fori_loop(0, cpb, chunk, 0)

    vec = lambda n: pl.BlockSpec((1, n), lambda i: (0, 0))
    mix, states, xbc, pre = pl.pallas_call(
        body, name=name, grid=(nb,),
        in_specs=[pl.BlockSpec((SSD_ROWS, SSD_IN_PAD), lambda i: (i, 0)),
                  pl.BlockSpec((SUBLANES, SSD_IN_PAD), lambda i: (jnp.maximum(i * hb - 1, 0), 0)),
                  pl.BlockSpec((SUBLANES, SSD_CONV_DIM), lambda i: (0, 0)),
                  vec(SSD_CONV_DIM), vec(LANES), vec(LANES), vec(LANES), vec(D_INNER)],
        out_specs=[pl.BlockSpec((SSD_ROWS, D_INNER), lambda i: (i, 0)),
                   pl.BlockSpec((cpb, SSD_G, SSD_S, SSD_GW), lambda i: (i, 0, 0, 0)),
                   pl.BlockSpec((SSD_ROWS, SSD_CONV_DIM), lambda i: (i, 0)),
                   pl.BlockSpec((SSD_ROWS, SSD_CONV_DIM), lambda i: (i, 0))],
        out_shape=[jax.ShapeDtypeStruct((rows_total, D_INNER), MXU_DTYPE),
                   jax.ShapeDtypeStruct((rows_total // CHUNK, SSD_G, SSD_S, SSD_GW), F32),
                   jax.ShapeDtypeStruct((rows_total, SSD_CONV_DIM), F32),
                   jax.ShapeDtypeStruct((rows_total, SSD_CONV_DIM), F32)],
        scratch_shapes=[pltpu.VMEM((SSD_ROWS + SUBLANES, SSD_CONV_DIM), F32),
                        pltpu.VMEM((SSD_ROWS, LANES), F32),
                        pltpu.VMEM((SSD_G, SSD_S, SSD_GW), F32),
                        pltpu.VMEM((LANES, CHUNK), F32)],
        compiler_params=_cparams(("arbitrary",)),
    )(proj, proj, _pad_rows(conv_w), conv_b.reshape(1, -1), _pad_lanes(a_log), _pad_lanes(dt_bias),
      _pad_lanes(d_skip), norm_w.reshape(1, -1))
    return mix, (states, xbc, pre)


def ssd_bwd(proj, conv_w, a_log, dt_bias, d_skip, norm_w, saved, dmix, *, name):
    states, xbc, pre = saved
    rows_total = proj.shape[0]
    nb = rows_total // SSD_ROWS
    cpb = SSD_ROWS // CHUNK

    def body(p_ref, cw_ref, alog_ref, dtb_ref, dsk_ref, nw_ref, st_ref, dm_ref, xbc_ref, pre_ref,
             dp_ref, dcw_ref, dcb_ref, dalog_ref, ddtb_ref, ddsk_ref, dnw_ref,
             dt_ref, ds_ref, act_ref, dext_ref, dac_ref, dact_ref, ddskw_ref):
        i = pl.program_id(0)

        @pl.when(i == 0)
        def _():
            ds_ref[...] = jnp.zeros_like(ds_ref)
            dext_ref[SSD_ROWS:, :] = jnp.zeros((SUBLANES, SSD_CONV_DIM), F32)
            ddskw_ref[...] = jnp.zeros_like(ddskw_ref)
            for r in (dcw_ref, dcb_ref, dalog_ref, ddtb_ref, ddsk_ref, dnw_ref):
                r[...] = jnp.zeros_like(r)

        dt_ref[...] = _softplus(p_ref[:, SSD_DT0:SSD_DT0 + LANES] + dtb_ref[...])
        a_row = -jnp.exp(alog_ref[...])
        expand = _head_expand(D_INNER, SSD_P)
        dsk_exp = _expand_row(dsk_ref[...], expand)
        ltri, utri, eye_l, eye_c = _tri(CHUNK), _tri(CHUNK, lower=False), _eye(LANES), _eye(CHUNK)
        causal = _iota((CHUNK, CHUNK), 1) <= _iota((CHUNK, CHUNK), 0)
        dp_ref[:, SSD_DT0 + LANES:] = jnp.zeros((SSD_ROWS, SSD_IN_PAD - SSD_DT0 - LANES), dp_ref.dtype)

        def chunk(cc, carry):
            c = cpb - 1 - cc
            rows = pl.ds(pl.multiple_of(c * CHUNK, CHUNK), CHUNK)
            dt_c = dt_ref[rows, :]
            ac, tail, cdec, tot = _ssd_chunk_decays(dt_c, a_row, ltri, eye_l, act_ref)
            dt_exp = _xsel(dt_c, expand)
            tail_exp = _xsel(tail, expand)
            cdec_exp = _xsel(cdec, expand)
            tot_exp = _expand_row(tot, expand)
            dac_ref[...] = jnp.zeros_like(dac_ref)
            dact_ref[...] = jnp.zeros_like(dact_ref)
            d_cdec = jnp.zeros((CHUNK, LANES), F32)
            d_tail = jnp.zeros((CHUNK, LANES), F32)
            d_dt = jnp.zeros((CHUNK, LANES), F32)
            d_tot = jnp.zeros((1, LANES), F32)
            gs = range(SSD_G)
            gls = [slice(g * SSD_GW, (g + 1) * SSD_GW) for g in gs]
            exs = [expand[:, gl] for gl in gls]
            states = [st_ref[c, g] for g in gs]
            fwd = _ssd_groups_fwd(xbc_ref, rows, dt_exp, tail_exp, cdec_exp, ac, act_ref, states, causal)
            ys, _, bgs, cgs, s_cs, cbs, xdts, css, segss, gmss, gms_cs = (list(t) for t in zip(*fwd))
            xss = [xbc_ref[rows, gl] for gl in gls]
            dys = []
            for g, gl in enumerate(gls):
                z = p_ref[rows, gl]
                sz = _silu(z)
                y2 = ys[g] + dsk_exp[:, gl] * xss[g]
                yg = y2 * sz
                r = lax.rsqrt(jnp.mean(yg * yg, axis=-1, keepdims=True) + RMS_EPS)
                yn = yg * r
                dm = dm_ref[rows, gl]
                dnw_ref[:, gl] += jnp.sum(dm * yn, axis=0, keepdims=True)
                dyn = dm * nw_ref[:, gl]
                dyg = r * (dyn - yn * jnp.mean(dyn * yn, axis=-1, keepdims=True))
                dp_ref[rows, gl] = (dyg * y2 * _dsilu(z)).astype(dp_ref.dtype)
                dys.append(dyg * sz)
                ddskw_ref[:, gl] += jnp.sum(dys[g] * xss[g], axis=0, keepdims=True)
            ds_gs = [ds_ref[g] for g in gs]
            ds_cs = [_mx(d) for d in ds_gs]
            dycs = [_mx(dys[g] * cdec_exp[:, gls[g]]) for g in gs]
            ds_new = [_dot_tn(cgs[g], dycs[g]) for g in gs]
            dcgs = [_dot_nt(dycs[g], s_cs[g]) for g in gs]
            d_xdtds = [_dot(bgs[g], ds_cs[g]) for g in gs]
            dbgs = [_dot_nt(xdts[g] * tail_exp[:, gls[g]], ds_cs[g]) for g in gs]
            for g in gs:
                ds_ref[g] = ds_gs[g] * tot_exp[:, gls[g]] + ds_new[g]
                sds = jnp.broadcast_to(jnp.sum(states[g] * ds_gs[g], axis=0, keepdims=True), (SUBLANES, SSD_GW))
                d_tot = d_tot + jnp.max(_xsel_nt(sds, exs[g]), axis=0, keepdims=True)
                d_cdec = d_cdec + _xsel_nt(dys[g] * css[g], exs[g])
                d_tail = d_tail + _xsel_nt(d_xdtds[g] * xdts[g], exs[g])
            heads = range(SSD_HG)
            dy_hs = [[_mx(_ssd_half(dys[g][:, (e // 2) * LANES:(e // 2 + 1) * LANES], e)) for e in heads] for g in gs]
            xps_cs = [[_mx(xdts[g][:, p * LANES:(p + 1) * LANES]) for p in range(SSD_HG // 2)] for g in gs]
            backs = [[_dot_tn(gms_cs[g][e], dy_hs[g][e]) for e in heads] for g in gs]
            dg_ms = [[jnp.where(causal, _dot_nt(dy_hs[g][e], xps_cs[g][e // 2]), 0.0) for e in heads] for g in gs]
            d_cbs = []
            for g in gs:
                d_cb = None
                for e in heads:
                    h = g * SSD_HG + e
                    term = dg_ms[g][e] * segss[g][e]
                    d_cb = term if d_cb is None else d_cb + term
                    em = dg_ms[g][e] * gmss[g][e]
                    dac_ref[...] += jnp.where(_iota((CHUNK, LANES), 1) == h, jnp.sum(em, axis=1, keepdims=True), 0.0)
                    dact_ref[h:h + 1, :] = jnp.sum(em, axis=0, keepdims=True)
                d_cbs.append(_mx(d_cb))
            dcg2 = [_dot(d_cbs[g], bgs[g]) for g in gs]
            dbg2 = [_dot_tn(d_cbs[g], cgs[g]) for g in gs]
            for g, gl in enumerate(gls):
                d_xdt = d_xdtds[g] * tail_exp[:, gl] + jnp.concatenate(
                    [backs[g][2 * p] + backs[g][2 * p + 1] for p in range(SSD_HG // 2)], axis=1)
                d_dt = d_dt + _xsel_nt(d_xdt * xss[g], exs[g])
                dext_ref[rows, gl] = d_xdt * dt_exp[:, gl] + dys[g] * dsk_exp[:, gl]
                dext_ref[rows, SSD_B0 + g * SSD_S:SSD_B0 + (g + 1) * SSD_S] = dbgs[g] + dbg2[g]
                dext_ref[rows, SSD_C0 + g * SSD_S:SSD_C0 + (g + 1) * SSD_S] = dcgs[g] + dcg2[g]
            d_ac = dac_ref[...] - _sel_nt(eye_c,dact_ref[...]) + d_cdec * cdec - d_tail * tail
            d_last = jnp.sum(d_tail * tail, axis=0, keepdims=True) + d_tot * tot
            d_ac = jnp.where(_iota((CHUNK, LANES), 0) == CHUNK - 1, d_ac + d_last, d_ac)
            d_da = _sel(utri,d_ac)
            d_dt = d_dt + d_da * a_row
            dalog_ref[...] += jnp.sum(d_da * dt_c, axis=0, keepdims=True) * a_row
            d_raw = d_dt * _sigmoid(p_ref[rows, SSD_DT0:SSD_DT0 + LANES] + dtb_ref[...])
            d_raw = jnp.where(_iota((CHUNK, LANES), 1) < SSD_H, d_raw, 0.0)
            ddtb_ref[...] += jnp.sum(d_raw, axis=0, keepdims=True)
            dp_ref[rows, SSD_DT0:SSD_DT0 + LANES] = d_raw.astype(dp_ref.dtype)
            return carry

        lax.fori_loop(0, cpb, chunk, 0)
        w = cw_ref[...]
        for c0 in range(0, SSD_CONV_DIM, CONV_COLS):
            cols = slice(c0, c0 + CONV_COLS)
            d_pre = dext_ref[0:SSD_ROWS, cols] * _dsilu(pre_ref[:, cols])
            dext_ref[0:SSD_ROWS, cols] = d_pre
            dcb_ref[:, cols] += jnp.sum(d_pre, axis=0, keepdims=True)
            du, dws = _conv_bwd_from_ext(dext_ref, p_ref[:, SSD_X0 + c0:SSD_X0 + c0 + CONV_COLS], w, SSD_CONV, SSD_ROWS, cols)
            for j in range(SSD_CONV):
                dcw_ref[j:j + 1, cols] += dws[j]
            dp_ref[:, SSD_X0 + c0:SSD_X0 + c0 + CONV_COLS] = du.astype(dp_ref.dtype)
            dext_ref[SSD_ROWS:, cols] = d_pre[0:SUBLANES, :]

        @pl.when(i == nb - 1)
        def _():
            ddsk_ref[...] = jnp.max(_xsel_nt(jnp.broadcast_to(ddskw_ref[...], (SUBLANES, D_INNER)), expand), axis=0, keepdims=True)

    vec = lambda n: pl.BlockSpec((1, n), lambda i: (0, 0))
    outs = pl.pallas_call(
        body, name=name, grid=(nb,),
        in_specs=[pl.BlockSpec((SSD_ROWS, SSD_IN_PAD), lambda i: (nb - 1 - i, 0)),
                  pl.BlockSpec((SUBLANES, SSD_CONV_DIM), lambda i: (0, 0)),
                  vec(LANES), vec(LANES), vec(LANES), vec(D_INNER),
                  pl.BlockSpec((cpb, SSD_G, SSD_S, SSD_GW), lambda i: (nb - 1 - i, 0, 0, 0)),
                  pl.BlockSpec((SSD_ROWS, D_INNER), lambda i: (nb - 1 - i, 0)),
                  pl.BlockSpec((SSD_ROWS, SSD_CONV_DIM), lambda i: (nb - 1 - i, 0)),
                  pl.BlockSpec((SSD_ROWS, SSD_CONV_DIM), lambda i: (nb - 1 - i, 0))],
        out_specs=[pl.BlockSpec((SSD_ROWS, SSD_IN_PAD), lambda i: (nb - 1 - i, 0)),
                   pl.BlockSpec((SUBLANES, SSD_CONV_DIM), lambda i: (0, 0)),
                   vec(SSD_CONV_DIM), vec(LANES), vec(LANES), vec(LANES), vec(D_INNER)],
        out_shape=[jax.ShapeDtypeStruct((rows_total, SSD_IN_PAD), MXU_DTYPE),
                   jax.ShapeDtypeStruct((SUBLANES, SSD_CONV_DIM), F32),
                   jax.ShapeDtypeStruct((1, SSD_CONV_DIM), F32), jax.ShapeDtypeStruct((1, LANES), F32),
                   jax.ShapeDtypeStruct((1, LANES), F32), jax.ShapeDtypeStruct((1, LANES), F32),
                   jax.ShapeDtypeStruct((1, D_INNER), F32)],
        scratch_shapes=[pltpu.VMEM((SSD_ROWS, LANES), F32),
                        pltpu.VMEM((SSD_G, SSD_S, SSD_GW), F32),
                        pltpu.VMEM((LANES, CHUNK), F32),
                        pltpu.VMEM((SSD_ROWS + SUBLANES, SSD_CONV_DIM), F32),
                        pltpu.VMEM((CHUNK, LANES), F32),
                        pltpu.VMEM((LANES, CHUNK), F32),
                        pltpu.VMEM((1, D_INNER), F32)],
        compiler_params=_cparams(("arbitrary",)),
    )(proj, _pad_rows(conv_w), _pad_lanes(a_log), _pad_lanes(dt_bias),
      _pad_lanes(d_skip), norm_w.reshape(1, -1), states, dmix, xbc, pre)
    dproj, dcw, dcb, dalog, ddtb, ddsk, dnw = outs
    return dproj, [dcw[:SSD_CONV], dcb[0], dalog[0, :SSD_H], ddtb[0, :SSD_H], ddsk[0, :SSD_H], dnw[0]]


GDN_ROWS = 128
GDN_K0 = GDN_QK
GDN_V0 = 2 * GDN_QK
GDN_Z0 = GDN_CONV_DIM
GDN_BA0 = GDN_CONV_DIM + GDN_V
GDN_GL = GDN_VH
GDN_SCALE = GDN_HEAD ** -0.5
GDN_GROUP = 16


def _gdn_lane_params(v):
    return jnp.pad(v.reshape(1, GDN_VH), ((0, 0), (GDN_GL, LANES - GDN_GL - GDN_VH)))


def _gdn_prologue(blk, p_ref, halo_ref, cw_ref, alog_ref, dtb_ref, ext_ref, qkv_ref, beta_ref, g_ref, pre_ref=None):
    ext_ref[0:SUBLANES, :] = jnp.where(blk == 0, 0.0, halo_ref[:, 0:GDN_CONV_DIM])
    ext_ref[SUBLANES:, :] = p_ref[:, 0:GDN_CONV_DIM]
    w = cw_ref[...]
    for hh in range(GDN_CONV_DIM // GDN_HEAD):
        cols = slice(hh * GDN_HEAD, (hh + 1) * GDN_HEAD)
        pre = _conv_from_ext(ext_ref, w, GDN_CONV, GDN_ROWS, cols)
        if pre_ref is not None:
            pre_ref[:, cols] = pre
        a = _silu(pre)
        if hh < 2 * GDN_QKH:
            r = lax.rsqrt(jnp.sum(a * a, axis=-1, keepdims=True) + L2_EPS)
            a = a * (r * (GDN_SCALE if hh < GDN_QKH else 1.0))
        qkv_ref[:, cols] = a
    _gdn_gates(p_ref, alog_ref, dtb_ref, beta_ref, g_ref)


def _gdn_gates(p_ref, alog_ref, dtb_ref, beta_ref, g_ref):
    ba = p_ref[:, GDN_BA0:GDN_BA0 + LANES]
    beta_ref[...] = _sigmoid(ba)
    g_ref[...] = -jnp.exp(alog_ref[...]) * _softplus(ba + dtb_ref[...])


def _each(f, *lists):
    return [f(*z) for z in zip(*lists)]


def _inv_unit_lower_each(a_list, eye_c):
    xs = [eye_c - a for a in a_list]
    ps = [_mx(a) for a in a_list]
    n = 2
    while n < CHUNK:
        ps = [_mx(_dot(p, p)) for p in ps]
        xs = [x + _dot(x, p) for x, p in zip(xs, ps)]
        n *= 2
    return xs


def _mx(x):
    return x.astype(MXU_DTYPE)


def _gdn_heads_fwd(q, k, v, kk, qk, gcol, grow, glast, bcol, s, causal, strict, eye_c, t=None):
    decay = _each(lambda gc_, gr_: jnp.where(causal, jnp.exp(jnp.minimum(gc_ - gr_, 0.0)), 0.0), gcol, grow)
    egc = _each(jnp.exp, gcol)
    etail = _each(lambda gl_, gc_: jnp.exp(gl_ - gc_), glast, gcol)
    cd = _each(jnp.exp, glast)
    a = _each(lambda b_, kk_, d_: jnp.where(strict, b_ * kk_ * d_, 0.0), bcol, kk, decay)
    if t is None:
        t = _inv_unit_lower_each(a, eye_c)
    t_c, s_c = _each(_mx, t), _each(_mx, s)
    kb = _each(lambda k_, b_: k_ * b_, k, bcol)
    rhs_w = _each(lambda kb_, e_: kb_ * e_, kb, egc)
    u = _each(lambda t_, v_, b_: _dot(t_, v_ * b_), t_c, v, bcol)
    w = _each(_dot, t_c, rhs_w)
    w_c = _each(_mx, w)
    attn = _each(lambda qk_, d_: qk_ * d_, qk, decay)
    attn_c = _each(_mx, attn)
    ws = _each(_dot, w_c, s_c)
    v_new = _each(lambda u_, ws_: u_ - ws_, u, ws)
    vn_c = _each(_mx, v_new)
    qd = _each(lambda q_, e_: q_ * e_, q, egc)
    kt = _each(lambda k_, e_: k_ * e_, k, etail)
    qd_c, kt_c = _each(_mx, qd), _each(_mx, kt)
    o1 = _each(_dot, qd_c, s_c)
    o2 = _each(_dot, attn_c, vn_c)
    out = _each(lambda a_, b_: a_ + b_, o1, o2)
    upd = _each(_dot_tn, kt_c, vn_c)
    s_new = _each(lambda s_, c_, u_: s_ * c_ + u_, s, cd, upd)
    return dict(decay=decay, egc=egc, etail=etail, cd=cd, a=a, t=t, kb=kb, rhs_w=rhs_w, u=u, w=w, attn=attn,
                v_new=v_new, qd=qd, kt=kt, out=out, s_new=s_new,
                t_c=t_c, s_c=s_c, w_c=w_c, attn_c=attn_c, vn_c=vn_c, qd_c=qd_c, kt_c=kt_c)


def gdn_fwd(proj, conv_w, a_log, dt_bias, norm_w, *, name):
    rows_total = proj.shape[0]
    nb = rows_total // GDN_ROWS
    hb = GDN_ROWS // SUBLANES
    cpb = GDN_ROWS // CHUNK

    def body(p_ref, halo_ref, cw_ref, alog_ref, dtb_ref, nw_ref, mix_ref, st_ref, tm_ref, qkv_ref, pre_ref,
             ext_ref, beta_ref, g_ref, s_ref, gct_ref):
        i = pl.program_id(0)

        @pl.when(i == 0)
        def _():
            s_ref[...] = jnp.zeros_like(s_ref)

        _gdn_prologue(i, p_ref, halo_ref, cw_ref, alog_ref, dtb_ref, ext_ref, qkv_ref, beta_ref, g_ref, pre_ref)
        ltri, eye_l, eye_c = _tri(CHUNK), _eye(LANES), _eye(CHUNK)
        causal = _iota((CHUNK, CHUNK), 1) <= _iota((CHUNK, CHUNK), 0)
        strict = _iota((CHUNK, CHUNK), 1) < _iota((CHUNK, CHUNK), 0)
        nw = nw_ref[...]

        def chunk(c, carry):
            rows = pl.ds(pl.multiple_of(c * CHUNK, CHUNK), CHUNK)
            gc = _sel(ltri,g_ref[rows, :])
            gct_ref[...] = _sel_nt(eye_l,gc)
            glast_row = _row(gc, CHUNK - 1)
            beta_c = beta_ref[rows, :]
            for h0 in range(0, GDN_VH, GDN_GROUP):
                hs = list(range(h0, h0 + GDN_GROUP))
                qs = {hq: qkv_ref[rows, hq * GDN_HEAD:(hq + 1) * GDN_HEAD] for hq in range(h0 // 2, (h0 + GDN_GROUP) // 2)}
                ks = {hq: qkv_ref[rows, GDN_K0 + hq * GDN_HEAD:GDN_K0 + (hq + 1) * GDN_HEAD] for hq in qs}
                ks_c = {hq: _mx(ks[hq]) for hq in qs}
                kks = {hq: _dot_nt(ks_c[hq], ks_c[hq]) for hq in qs}
                qks = {hq: _dot_nt(qs[hq], ks_c[hq]) for hq in qs}
                ss = [s_ref[h] for h in hs]
                for h, s in zip(hs, ss):
                    st_ref[c, h] = s
                f = _gdn_heads_fwd(
                    [qs[h // 2] for h in hs], [ks[h // 2] for h in hs],
                    [qkv_ref[rows, GDN_V0 + h * GDN_HEAD:GDN_V0 + (h + 1) * GDN_HEAD] for h in hs],
                    [kks[h // 2] for h in hs], [qks[h // 2] for h in hs],
                    [_col(gc, GDN_GL + h) for h in hs], [gct_ref[GDN_GL + h:GDN_GL + h + 1, :] for h in hs],
                    [_col(glast_row, GDN_GL + h) for h in hs], [_col(beta_c, h) for h in hs], ss, causal, strict, eye_c)
                for i_h, h in enumerate(hs):
                    hc = slice(h * GDN_HEAD, (h + 1) * GDN_HEAD)
                    s_ref[h] = f["s_new"][i_h]
                    tm_ref[c, h] = f["t"][i_h].astype(tm_ref.dtype)
                    o = f["out"][i_h]
                    r = lax.rsqrt(jnp.mean(o * o, axis=-1, keepdims=True) + RMS_EPS)
                    z = p_ref[rows, GDN_Z0 + h * GDN_HEAD:GDN_Z0 + (h + 1) * GDN_HEAD]
                    mix_ref[rows, hc] = (o * r * nw * _silu(z)).astype(mix_ref.dtype)
            return carry

        lax.fori_loop(0, cpb, chunk, 0)

    vec = lambda n: pl.BlockSpec((1, n), lambda i: (0, 0))
    act = pl.BlockSpec((GDN_ROWS, GDN_CONV_DIM), lambda i: (i, 0))
    mix, states, tmats, qkv, pre = pl.pallas_call(
        body, name=name, grid=(nb,),
        in_specs=[pl.BlockSpec((GDN_ROWS, GDN_IN_PAD), lambda i: (i, 0)),
                  pl.BlockSpec((SUBLANES, GDN_IN_PAD), lambda i: (jnp.maximum(i * hb - 1, 0), 0)),
                  pl.BlockSpec((SUBLANES, GDN_CONV_DIM), lambda i: (0, 0)),
                  vec(LANES), vec(LANES), vec(GDN_HEAD)],
        out_specs=[pl.BlockSpec((GDN_ROWS, GDN_V), lambda i: (i, 0)),
                   pl.BlockSpec((cpb, GDN_VH, GDN_HEAD, GDN_HEAD), lambda i: (i, 0, 0, 0)),
                   pl.BlockSpec((cpb, GDN_VH, CHUNK, CHUNK), lambda i: (i, 0, 0, 0)), act, act],
        out_shape=[jax.ShapeDtypeStruct((rows_total, GDN_V), MXU_DTYPE),
                   jax.ShapeDtypeStruct((rows_total // CHUNK, GDN_VH, GDN_HEAD, GDN_HEAD), F32),
                   jax.ShapeDtypeStruct((rows_total // CHUNK, GDN_VH, CHUNK, CHUNK), MXU_DTYPE),
                   jax.ShapeDtypeStruct((rows_total, GDN_CONV_DIM), F32),
                   jax.ShapeDtypeStruct((rows_total, GDN_CONV_DIM), F32)],
        scratch_shapes=[pltpu.VMEM((GDN_ROWS + SUBLANES, GDN_CONV_DIM), F32),
                        pltpu.VMEM((GDN_ROWS, LANES), F32),
                        pltpu.VMEM((GDN_ROWS, LANES), F32),
                        pltpu.VMEM((GDN_VH, GDN_HEAD, GDN_HEAD), F32),
                        pltpu.VMEM((LANES, CHUNK), F32)],
        compiler_params=_cparams(("arbitrary",)),
    )(proj, proj, _pad_rows(conv_w), _gdn_lane_params(a_log), _gdn_lane_params(dt_bias), norm_w.reshape(1, -1))
    return mix, (states, tmats, qkv, pre)


def gdn_bwd(proj, conv_w, a_log, dt_bias, norm_w, saved, dmix, *, name):
    states, tmats, qkv, pre = saved
    rows_total = proj.shape[0]
    nb = rows_total // GDN_ROWS
    cpb = GDN_ROWS // CHUNK

    def body(p_ref, cw_ref, alog_ref, dtb_ref, nw_ref, st_ref, tm_ref, dm_ref, qkv_ref, pre_ref,
             dp_ref, dcw_ref, dalog_ref, ddtb_ref, dnw_ref,
             beta_ref, g_ref, ds_ref, gct_ref, dext_ref, dgc_ref, dgct_ref, dbeta_ref):
        i = pl.program_id(0)

        @pl.when(i == 0)
        def _():
            ds_ref[...] = jnp.zeros_like(ds_ref)
            dext_ref[GDN_ROWS:, :] = jnp.zeros((SUBLANES, GDN_CONV_DIM), F32)
            for r in (dcw_ref, dalog_ref, ddtb_ref, dnw_ref):
                r[...] = jnp.zeros_like(r)

        _gdn_gates(p_ref, alog_ref, dtb_ref, beta_ref, g_ref)
        ltri, utri, eye_l, eye_c = _tri(CHUNK), _tri(CHUNK, lower=False), _eye(LANES), _eye(CHUNK)
        causal = _iota((CHUNK, CHUNK), 1) <= _iota((CHUNK, CHUNK), 0)
        strict = _iota((CHUNK, CHUNK), 1) < _iota((CHUNK, CHUNK), 0)
        lane = _iota((CHUNK, LANES), 1)
        is_last = _iota((CHUNK, 1), 0) == CHUNK - 1
        nw = nw_ref[...]

        def chunk(cc, carry):
            c = cpb - 1 - cc
            rows = pl.ds(pl.multiple_of(c * CHUNK, CHUNK), CHUNK)
            g_c = g_ref[rows, :]
            gc = _sel(ltri,g_c)
            gct_ref[...] = _sel_nt(eye_l,gc)
            glast_row = _row(gc, CHUNK - 1)
            beta_c = beta_ref[rows, :]
            dgc_ref[...] = jnp.zeros_like(dgc_ref)
            dgct_ref[...] = jnp.zeros_like(dgct_ref)
            dbeta_ref[...] = jnp.zeros_like(dbeta_ref)
            for h0 in range(0, GDN_VH, GDN_GROUP):
                hs = list(range(h0, h0 + GDN_GROUP))
                hqs = list(range(h0 // 2, (h0 + GDN_GROUP) // 2))
                qs = {hq: qkv_ref[rows, hq * GDN_HEAD:(hq + 1) * GDN_HEAD] for hq in hqs}
                ks = {hq: qkv_ref[rows, GDN_K0 + hq * GDN_HEAD:GDN_K0 + (hq + 1) * GDN_HEAD] for hq in hqs}
                qs_c = {hq: _mx(qs[hq]) for hq in hqs}
                ks_c = {hq: _mx(ks[hq]) for hq in hqs}
                kks = {hq: _dot_nt(ks_c[hq], ks_c[hq]) for hq in hqs}
                qks = {hq: _dot_nt(qs_c[hq], ks_c[hq]) for hq in hqs}
                q = [qs[h // 2] for h in hs]
                k = [ks[h // 2] for h in hs]
                v = [qkv_ref[rows, GDN_V0 + h * GDN_HEAD:GDN_V0 + (h + 1) * GDN_HEAD] for h in hs]
                s = [st_ref[c, h] for h in hs]
                bcol = [_col(beta_c, h) for h in hs]
                f = _gdn_heads_fwd(q, k, v, [kks[h // 2] for h in hs], [qks[h // 2] for h in hs],
                                   [_col(gc, GDN_GL + h) for h in hs], [gct_ref[GDN_GL + h:GDN_GL + h + 1, :] for h in hs],
                                   [_col(glast_row, GDN_GL + h) for h in hs], bcol, s, causal, strict, eye_c,
                                   t=[tm_ref[c, h] for h in hs])
                do = []
                for i_h, h in enumerate(hs):
                    zc = slice(GDN_Z0 + h * GDN_HEAD, GDN_Z0 + (h + 1) * GDN_HEAD)
                    o = f["out"][i_h]
                    z = p_ref[rows, zc]
                    sz = _silu(z)
                    r = lax.rsqrt(jnp.mean(o * o, axis=-1, keepdims=True) + RMS_EPS)
                    on = o * r
                    dm = dm_ref[rows, h * GDN_HEAD:(h + 1) * GDN_HEAD]
                    dnw_ref[...] += jnp.sum(dm * on * sz, axis=0, keepdims=True)
                    d_on = dm * nw * sz
                    dp_ref[rows, zc] = (dm * on * nw * _dsilu(z)).astype(dp_ref.dtype)
                    do.append(r * (d_on - on * jnp.mean(d_on * on, axis=-1, keepdims=True)))
                ds_n = [ds_ref[h] for h in hs]
                do_c, dsn_c = _each(_mx, do), _each(_mx, ds_n)
                k_c = [ks_c[h // 2] for h in hs]
                dv1 = _each(_dot_tn, f["attn_c"], do_c)
                dv2 = _each(_dot, f["kt_c"], dsn_c)
                d_vnew = _each(lambda a_, b_: a_ + b_, dv1, dv2)
                dvn_c = _each(_mx, d_vnew)
                d_attn = _each(lambda do_, vn_: jnp.where(causal, _dot_nt(do_, vn_), 0.0), do_c, f["vn_c"])
                d_qd = _each(_dot_nt, do_c, f["s_c"])
                t1 = _each(_dot_tn, f["qd_c"], do_c)
                t2 = _each(_dot_tn, f["w_c"], dvn_c)
                for h, a_, cd_, dsn_, b_ in zip(hs, t1, f["cd"], ds_n, t2):
                    ds_ref[h] = a_ + cd_ * dsn_ - b_
                d_cd = _each(lambda s_, dsn_: jnp.sum(jnp.sum(s_ * dsn_, axis=1, keepdims=True), axis=0, keepdims=True), s, ds_n)
                d_kt = _each(_dot_nt, f["vn_c"], dsn_c)
                d_w = _each(lambda dv_, s_: -_dot_nt(dv_, s_), dvn_c, f["s_c"])
                d_rhs_u = _each(_dot_tn, f["t_c"], dvn_c)
                d_rhs_w = _each(_dot_tn, f["t_c"], d_w)
                m1 = _each(_dot_nt, d_rhs_u, f["u"])
                m2 = _each(_dot_nt, d_rhs_w, f["w_c"])
                da = _each(lambda a_, b_: -jnp.where(strict, a_ + b_, 0.0), m1, m2)
                dmm = _each(lambda a_, b_: a_ * b_, da, f["decay"])
                em = _each(lambda da_, a_, dat_, at_: da_ * a_ + dat_ * at_, da, f["a"], d_attn, f["attn"])
                dmm_c = _each(_mx, dmm)
                x1 = _each(_dot, dmm_c, k_c)
                d_kb = _each(lambda x_, drw_, e_: x_ + drw_ * e_, x1, d_rhs_w, f["egc"])
                dk1 = _each(_dot_tn, dmm_c, f["kb"])
                dpm = _each(lambda a_, b_: _mx(a_ * b_), d_attn, f["decay"])
                dq1 = _each(_dot, dpm, k_c)
                dq = _each(lambda x_, dqd_, e_: x_ + dqd_ * e_, dq1, d_qd, f["egc"])
                dk2 = _each(_dot_tn, dpm, [qs_c[h // 2] for h in hs])
                dk = _each(lambda a_, b_, dkb_, bc_, dkt_, et_: a_ + b_ + dkb_ * bc_ + dkt_ * et_,
                           dk1, dk2, d_kb, bcol, d_kt, f["etail"])
                for i_h, h in enumerate(hs):
                    tmp = jnp.sum(d_kt[i_h] * f["kt"][i_h], axis=1, keepdims=True)
                    d_gcol = (jnp.sum(em[i_h], axis=1, keepdims=True)
                              + jnp.sum(d_rhs_w[i_h] * f["rhs_w"][i_h], axis=1, keepdims=True)
                              + jnp.sum(d_qd[i_h] * f["qd"][i_h], axis=1, keepdims=True) - tmp)
                    d_glast = jnp.sum(tmp, axis=0, keepdims=True) + d_cd[i_h] * f["cd"][i_h]
                    d_gcol = jnp.where(is_last, d_gcol + d_glast, d_gcol)
                    d_beta = (jnp.sum(d_rhs_u[i_h] * v[i_h], axis=1, keepdims=True)
                              + jnp.sum(d_kb[i_h] * k[i_h], axis=1, keepdims=True))
                    dgc_ref[...] += jnp.where(lane == GDN_GL + h, d_gcol, 0.0)
                    dgct_ref[GDN_GL + h:GDN_GL + h + 1, :] = jnp.sum(em[i_h], axis=0, keepdims=True)
                    dbeta_ref[...] += jnp.where(lane == h, d_beta, 0.0)
                    dext_ref[rows, GDN_V0 + h * GDN_HEAD:GDN_V0 + (h + 1) * GDN_HEAD] = d_rhs_u[i_h] * bcol[i_h]
                for hq in hqs:
                    i0 = 2 * hq - h0
                    dext_ref[rows, hq * GDN_HEAD:(hq + 1) * GDN_HEAD] = dq[i0] + dq[i0 + 1]
                    dext_ref[rows, GDN_K0 + hq * GDN_HEAD:GDN_K0 + (hq + 1) * GDN_HEAD] = dk[i0] + dk[i0 + 1]
            d_gc = dgc_ref[...] - _sel_nt(eye_c,dgct_ref[...])
            dg = _sel(utri,d_gc)
            ba = p_ref[rows, GDN_BA0:GDN_BA0 + LANES]
            d_sp = dg * -jnp.exp(alog_ref[...])
            d_araw = d_sp * _sigmoid(ba + dtb_ref[...])
            d_araw = jnp.where((lane >= GDN_GL) & (lane < GDN_GL + GDN_VH), d_araw, 0.0)
            dalog_ref[...] += jnp.sum(dg * g_c, axis=0, keepdims=True)
            ddtb_ref[...] += jnp.sum(d_araw, axis=0, keepdims=True)
            d_braw = jnp.where(lane < GDN_VH, dbeta_ref[...] * beta_c * (1.0 - beta_c), 0.0)
            dp_ref[rows, GDN_BA0:GDN_BA0 + LANES] = (d_braw + d_araw).astype(dp_ref.dtype)
            return carry

        lax.fori_loop(0, cpb, chunk, 0)
        w = cw_ref[...]
        for hh in range(GDN_CONV_DIM // GDN_HEAD):
            cols = slice(hh * GDN_HEAD, (hh + 1) * GDN_HEAD)
            pre = pre_ref[:, cols]
            d_act = dext_ref[0:GDN_ROWS, cols]
            if hh < 2 * GDN_QKH:
                a = _silu(pre)
                r = lax.rsqrt(jnp.sum(a * a, axis=-1, keepdims=True) + L2_EPS)
                ah = a * r
                if hh < GDN_QKH:
                    d_act = d_act * GDN_SCALE
                d_act = r * (d_act - ah * jnp.sum(d_act * ah, axis=-1, keepdims=True))
            d_pre = d_act * _dsilu(pre)
            dext_ref[0:GDN_ROWS, cols] = d_pre
            du, dws = _conv_bwd_from_ext(dext_ref, p_ref[:, cols], w, GDN_CONV, GDN_ROWS, cols)
            for j in range(GDN_CONV):
                dcw_ref[j:j + 1, cols] += dws[j]
            dp_ref[:, cols] = du.astype(dp_ref.dtype)
            dext_ref[GDN_ROWS:, cols] = d_pre[0:SUBLANES, :]

    vec = lambda n: pl.BlockSpec((1, n), lambda i: (0, 0))
    outs = pl.pallas_call(
        body, name=name, grid=(nb,),
        in_specs=[pl.BlockSpec((GDN_ROWS, GDN_IN_PAD), lambda i: (nb - 1 - i, 0)),
                  pl.BlockSpec((SUBLANES, GDN_CONV_DIM), lambda i: (0, 0)),
                  vec(LANES), vec(LANES), vec(GDN_HEAD),
                  pl.BlockSpec((cpb, GDN_VH, GDN_HEAD, GDN_HEAD), lambda i: (nb - 1 - i, 0, 0, 0)),
                  pl.BlockSpec((cpb, GDN_VH, CHUNK, CHUNK), lambda i: (nb - 1 - i, 0, 0, 0)),
                  pl.BlockSpec((GDN_ROWS, GDN_V), lambda i: (nb - 1 - i, 0)),
                  pl.BlockSpec((GDN_ROWS, GDN_CONV_DIM), lambda i: (nb - 1 - i, 0)),
                  pl.BlockSpec((GDN_ROWS, GDN_CONV_DIM), lambda i: (nb - 1 - i, 0))],
        out_specs=[pl.BlockSpec((GDN_ROWS, GDN_IN_PAD), lambda i: (nb - 1 - i, 0)),
                   pl.BlockSpec((SUBLANES, GDN_CONV_DIM), lambda i: (0, 0)),
                   vec(LANES), vec(LANES), vec(GDN_HEAD)],
        out_shape=[jax.ShapeDtypeStruct((rows_total, GDN_IN_PAD), MXU_DTYPE),
                   jax.ShapeDtypeStruct((SUBLANES, GDN_CONV_DIM), F32),
                   jax.ShapeDtypeStruct((1, LANES), F32), jax.ShapeDtypeStruct((1, LANES), F32),
                   jax.ShapeDtypeStruct((1, GDN_HEAD), F32)],
        scratch_shapes=[pltpu.VMEM((GDN_ROWS, LANES), F32),
                        pltpu.VMEM((GDN_ROWS, LANES), F32),
                        pltpu.VMEM((GDN_VH, GDN_HEAD, GDN_HEAD), F32),
                        pltpu.VMEM((LANES, CHUNK), F32),
                        pltpu.VMEM((GDN_ROWS + SUBLANES, GDN_CONV_DIM), F32),
                        pltpu.VMEM((CHUNK, LANES), F32),
                        pltpu.VMEM((LANES, CHUNK), F32),
                        pltpu.VMEM((CHUNK, LANES), F32)],
        compiler_params=_cparams(("arbitrary",)),
    )(proj, _pad_rows(conv_w), _gdn_lane_params(a_log), _gdn_lane_params(dt_bias), norm_w.reshape(1, -1),
      states, tmats, dmix, qkv, pre)
    dproj, dcw, dalog, ddtb, dnw = outs
    return dproj, [dcw[:GDN_CONV], dalog[0, GDN_GL:GDN_GL + GDN_VH], ddtb[0, GDN_GL:GDN_GL + GDN_VH], dnw[0]]


def chip_exchange(src, *, scatter, name):
    piece_shape = src.shape[1:]

    def body(src_ref, out_ref, send_sems, recv_sems, local_sem):
        x, y, c = (lax.axis_index(a) for a in MESH_AXES)
        me = 2 * x + y

        def piece(j):
            return src_ref.at[j] if scatter else src_ref.at[c]

        local = pltpu.make_async_copy(piece(me), out_ref.at[me], local_sem)
        local.start()
        copies = []
        for k in range(1, N_SHARDS):
            px = 1 - x if k & 2 else x
            py = 1 - y if k & 1 else y
            cp = pltpu.make_async_remote_copy(
                src_ref=piece(2 * px + py), dst_ref=out_ref.at[me], send_sem=send_sems.at[k - 1],
                recv_sem=recv_sems.at[k - 1], device_id=(px, py, c), device_id_type=pl.DeviceIdType.MESH)
            cp.start()
            copies.append(cp)
        for cp in copies:
            cp.wait()
        local.wait()

    hbm = pl.BlockSpec(memory_space=pl.ANY)
    return pl.pallas_call(
        body, name=name, in_specs=[hbm], out_specs=hbm,
        out_shape=jax.ShapeDtypeStruct((N_SHARDS,) + tuple(piece_shape), src.dtype),
        scratch_shapes=[pltpu.SemaphoreType.DMA((N_SHARDS - 1,)), pltpu.SemaphoreType.DMA((N_SHARDS - 1,)),
                        pltpu.SemaphoreType.DMA],
    )(src)


def pair_exchange(src, *, add, name, out_dtype=None):
    lead, rows, cols = src.shape
    tr = _pick(rows, (512, 256))
    nblk = rows // tr
    n_steps = nblk if add else lead * nblk

    def body(c_ref, *refs):
        if add:
            mine_ref, send_ref, o_ref, recv_ref, send_sems, recv_sems, credit = refs
        else:
            send_ref, o_ref, recv_ref, send_sems, recv_sems, credit = refs
        step = pl.program_id(0) * nblk + pl.program_id(1)
        slot = step % 2
        sibling = (lax.axis_index("x"), lax.axis_index("y"), 1 - lax.axis_index("c"))

        @pl.when(step >= 2)
        def _():
            pl.semaphore_wait(credit, 1)

        cp = pltpu.make_async_remote_copy(
            src_ref=send_ref, dst_ref=recv_ref.at[slot], send_sem=send_sems.at[slot], recv_sem=recv_sems.at[slot],
            device_id=sibling, device_id_type=pl.DeviceIdType.MESH)
        cp.start()
        cp.wait_recv()
        if add:
            o_ref[...] = (mine_ref[...] + recv_ref[slot]).astype(o_ref.dtype)
        else:
            o_ref[c_ref[0]] = send_ref[...]
            o_ref[1 - c_ref[0]] = recv_ref[slot]
        cp.wait_send()

        @pl.when(step + 2 < n_steps)
        def _():
            pl.semaphore_signal(credit, 1, device_id=sibling, device_id_type=pl.DeviceIdType.MESH)

    flat = src.reshape(lead * rows, cols)
    if add:
        in_specs = [pl.BlockSpec((tr, cols), lambda s, i, c_ref: (c_ref[0] * nblk + i, 0)),
                    pl.BlockSpec((tr, cols), lambda s, i, c_ref: ((1 - c_ref[0]) * nblk + i, 0))]
        out_specs = pl.BlockSpec((tr, cols), lambda s, i, c_ref: (i, 0))
        out_shape = jax.ShapeDtypeStruct((rows, cols), src.dtype if out_dtype is None else out_dtype)
        grid, args = (1, nblk), (flat, flat)
    else:
        in_specs = [pl.BlockSpec((tr, cols), lambda s, i, c_ref: (s * nblk + i, 0))]
        out_specs = pl.BlockSpec((2, tr, cols), lambda s, i, c_ref: (s, i, 0))
        out_shape = jax.ShapeDtypeStruct((lead * 2, rows, cols), src.dtype)
        grid, args = (lead, nblk), (flat,)
    out = pl.pallas_call(
        body, name=name, out_shape=out_shape,
        grid_spec=pltpu.PrefetchScalarGridSpec(
            num_scalar_prefetch=1, grid=grid, in_specs=in_specs, out_specs=out_specs,
            scratch_shapes=[pltpu.VMEM((2, tr, cols), src.dtype), pltpu.SemaphoreType.DMA((2,)),
                            pltpu.SemaphoreType.DMA((2,)), pltpu.SemaphoreType.REGULAR]),
        compiler_params=_cparams(("arbitrary", "arbitrary")),
    )(lax.axis_index("c").astype(jnp.int32).reshape(1), *args)
    return out if add else out.reshape(lead, 2, rows, cols)


def sum_slots(buf, *, name):
    n, rows, cols = buf.shape
    tr = _pick(rows, (512, 256, 128))

    def body(b_ref, o_ref):
        acc = b_ref[0].astype(F32)
        for j in range(1, n):
            acc = acc + b_ref[j].astype(F32)
        o_ref[...] = acc

    return pl.pallas_call(
        body, name=name, grid=(rows // tr,), in_specs=[pl.BlockSpec((n, tr, cols), lambda i: (0, i, 0))],
        out_specs=pl.BlockSpec((tr, cols), lambda i: (i, 0)), out_shape=jax.ShapeDtypeStruct((rows, cols), F32),
        compiler_params=_cparams(("parallel",)),
    )(buf)


def adamw(w, g, m, v, *, name):
    shape = w.shape
    cols = shape[-1]
    rows = _size(shape) // cols
    w, g, m, v = (t.reshape(rows, cols) for t in (w, g, m, v))
    tr = 256 if rows % 256 == 0 else rows

    def body(w_ref, g_ref, m_ref, v_ref, d_ref, mo_ref, vo_ref):
        gv = g_ref[...]
        mn = ADAM_B1 * m_ref[...] + (1.0 - ADAM_B1) * gv
        vn = ADAM_B2 * v_ref[...] + (1.0 - ADAM_B2) * (gv * gv)
        m_hat = mn / (1.0 - ADAM_B1 ** ADAM_STEP)
        v_hat = vn / (1.0 - ADAM_B2 ** ADAM_STEP)
        d_ref[...] = -ADAM_LR * (m_hat / (jnp.sqrt(v_hat) + ADAM_EPS) + ADAM_WD * w_ref[...])
        mo_ref[...] = mn
        vo_ref[...] = vn

    blk = pl.BlockSpec((tr, cols), lambda i: (i, 0))
    shp = jax.ShapeDtypeStruct((rows, cols), F32)
    outs = pl.pallas_call(
        body, name=name, grid=(rows // tr,), in_specs=[blk] * 4, out_specs=[blk] * 3, out_shape=[shp] * 3,
        compiler_params=_cparams(("parallel",)),
    )(w, g, m, v)
    return [o.reshape(shape) for o in outs]


N_SHARDS = 4
FLAT_COLS = 1024
W_SPECS = (
    ("gdn_w_in", (2, 1024, 6176), 2), ("gdn_conv_w", (2, 4, 4096), 2), ("gdn_a_log", (2, 16), None),
    ("gdn_dt_bias", (2, 16), None), ("gdn_norm_w", (2, 128), None), ("gdn_w_out", (2, 2048, 1024), 1),
    ("sc_w_in", (1, 1024, 8192), 2), ("sc_conv_w", (1, 3, 2048), 2), ("sc_w_out", (1, 2048, 1024), 1),
    ("ssd_w_in", (1, 1024, 5152), 2), ("ssd_conv_w", (1, 4, 3072), 2), ("ssd_conv_b", (1, 3072), 1),
    ("ssd_a_log", (1, 32), None), ("ssd_dt_bias", (1, 32), None), ("ssd_d_skip", (1, 32), None),
    ("ssd_norm_w", (1, 2048), 1), ("ssd_w_out", (1, 2048, 1024), 1), ("ln_g", (4, 1024), None), ("ln_b", (4, 1024), None),
)


def _local_shape(shape, axis):
    return shape if axis is None else tuple(d // N_SHARDS if i == axis else d for i, d in enumerate(shape))


def _size(shape):
    n = 1
    for d in shape:
        n *= d
    return n


PIECE_ROWS = 16


def _piece_rows(shape, axis):
    return -(-_size(_local_shape(shape, axis)) // (FLAT_COLS * PIECE_ROWS)) * PIECE_ROWS


def _flat_rows(specs):
    return -(-sum(_piece_rows(s, a) for _, s, a in specs) // 512) * 512


def _pack(pieces, specs, dtype=F32):
    blocks, used = [], 0
    for p, (_, shape, axis) in zip(pieces, specs):
        rows = _piece_rows(shape, axis)
        flat = p.reshape(-1).astype(dtype)
        if flat.shape[0] < rows * FLAT_COLS:
            flat = jnp.pad(flat, (0, rows * FLAT_COLS - flat.shape[0]))
        blocks.append(flat.reshape(rows, FLAT_COLS))
        used += rows
    blocks.append(jnp.zeros((_flat_rows(specs) - used, FLAT_COLS), dtype))
    return jnp.concatenate(blocks, axis=0)


def _unpack(flat, specs):
    out, off = [], 0
    for _, shape, axis in specs:
        ls = _local_shape(shape, axis)
        rows = _piece_rows(shape, axis)
        out.append(flat[off:off + rows].reshape(-1)[:_size(ls)].reshape(ls))
        off += rows
    return out


def _shard_of(full, axis, s):
    if axis is None:
        return full
    n = full.shape[axis] // N_SHARDS
    return lax.slice_in_dim(full, s * n, (s + 1) * n, axis=axis)


def _adamw_all(weights, grads, moms, vels):
    steps = [adamw(w, g, m, v, name="adamw") for w, g, m, v in zip(weights, grads, moms, vels)]
    return grads, [s[0] for s in steps], [s[1] for s in steps], [s[2] for s in steps]


SPLIT_ROWS = 128


def shard_split(w, n_real, *, name):
    rows, n_pad = w.shape
    ns = n_real // N_SHARDS

    def body(w_ref, o_ref):
        for s in range(N_SHARDS):
            o_ref[s] = w_ref[:, s * ns:(s + 1) * ns]

    return pl.pallas_call(
        body, name=name, grid=(rows // SPLIT_ROWS,),
        in_specs=[pl.BlockSpec((SPLIT_ROWS, n_pad), lambda i: (i, 0))],
        out_specs=pl.BlockSpec((N_SHARDS, SPLIT_ROWS, ns), lambda i: (0, i, 0)),
        out_shape=jax.ShapeDtypeStruct((N_SHARDS, rows, ns), F32), compiler_params=_cparams(("parallel",)),
    )(w)


def shard_merge(pieces, n_pad, *, name):
    _, rows, ns = pieces.shape
    n_real = ns * N_SHARDS

    def body(p_ref, o_ref):
        for s in range(N_SHARDS):
            o_ref[:, s * ns:(s + 1) * ns] = p_ref[s].astype(o_ref.dtype)
        if n_pad > n_real:
            o_ref[:, n_real:] = jnp.zeros((SPLIT_ROWS, n_pad - n_real), o_ref.dtype)

    return pl.pallas_call(
        body, name=name, grid=(rows // SPLIT_ROWS,),
        in_specs=[pl.BlockSpec((N_SHARDS, SPLIT_ROWS, ns), lambda i: (0, i, 0))],
        out_specs=pl.BlockSpec((SPLIT_ROWS, n_pad), lambda i: (i, 0)),
        out_shape=jax.ShapeDtypeStruct((rows, n_pad), MXU_DTYPE), compiler_params=_cparams(("parallel",)),
    )(pieces)


def _reduce_scatter(full_grads):
    def shard(g, spec, s):
        _, shape, axis = spec
        return g[:, s] if g.ndim == len(shape) + 1 else _shard_of(g, axis, s)

    def reduce(idx, wire_dtype, tag):
        specs = [W_SPECS[i] for i in idx]
        half = _flat_rows(specs) // 2
        by_shard = jnp.stack([_pack([shard(full_grads[i], W_SPECS[i], s) for i in idx], specs)
                              for s in range(N_SHARDS)])
        by_half = by_shard.reshape(N_SHARDS, 2, half, FLAT_COLS).transpose(1, 0, 2, 3)
        by_half = by_half.reshape(2, N_SHARDS * half, FLAT_COLS)
        pair_sum = pair_exchange(by_half, add=True, out_dtype=wire_dtype, name="rs_pair_" + tag)
        chips = chip_exchange(pair_sum.reshape(N_SHARDS, half, FLAT_COLS), scatter=True, name="rs_chips_" + tag)
        summed = sum_slots(chips, name="rs_chip_sum_" + tag)
        both = pair_exchange(summed[None], add=False, name="rs_halves_" + tag).reshape(2 * half, FLAT_COLS)
        return dict(zip(idx, _unpack(both, specs)))

    matrices = [i for i, (n, _, _) in enumerate(W_SPECS) if n in MXU_WEIGHTS]
    others = [i for i, (n, _, _) in enumerate(W_SPECS) if n not in MXU_WEIGHTS]
    grads = {**reduce(matrices, MXU_DTYPE, "mxu"), **reduce(others, F32, "f32")}
    return [grads[i] for i in range(len(W_SPECS))]


def _gather_weights(local_weights):
    def gather(idx, dtype, tag):
        specs = [W_SPECS[i] for i in idx]
        rows = _flat_rows(specs)
        flat = _pack([local_weights[i] for i in idx], specs, dtype)
        halves = chip_exchange(flat.reshape(2, rows // 2, FLAT_COLS), scatter=False, name="gather_chips_" + tag)
        both = pair_exchange(halves, add=False, name="gather_pair_" + tag).reshape(N_SHARDS, rows, FLAT_COLS)
        return [dict(zip(idx, _unpack(both[s], specs))) for s in range(N_SHARDS)]

    matrices = [i for i, (n, _, _) in enumerate(W_SPECS) if n in MXU_WEIGHTS]
    vectors = [i for i, (n, _, a) in enumerate(W_SPECS) if n not in MXU_WEIGHTS and a is not None]
    per_shard = [{**m, **v} for m, v in zip(gather(matrices, MXU_DTYPE, "mxu"), gather(vectors, F32, "f32"))]
    full = []
    for i, (wname, shape, axis) in enumerate(W_SPECS):
        if axis is None:
            full.append(local_weights[i])
        elif wname in W_IN_PAD:
            pieces = jnp.stack([per_shard[s][i] for s in range(N_SHARDS)], axis=1)
            full.append([shard_merge(pieces[j], W_IN_PAD[wname], name="merge_" + wname) for j in range(shape[0])])
        else:
            full.append(jnp.concatenate([per_shard[s][i] for s in range(N_SHARDS)], axis=axis))
    return full


W_IN_PAD = {"gdn_w_in": GDN_IN_PAD, "sc_w_in": SC_IN, "ssd_w_in": SSD_IN_PAD}
MXU_WEIGHTS = ("gdn_w_in", "gdn_w_out", "sc_w_in", "sc_w_out", "ssd_w_in", "ssd_w_out")


def kernel(x, gdn_w_in, gdn_conv_w, gdn_a_log, gdn_dt_bias, gdn_norm_w, gdn_w_out, sc_w_in, sc_conv_w, sc_w_out, ssd_w_in, ssd_conv_w, ssd_conv_b, ssd_a_log, ssd_dt_bias, ssd_d_skip, ssd_norm_w, ssd_w_out, ln_g, ln_b, loss_target, m_gdn_w_in, m_gdn_conv_w, m_gdn_a_log, m_gdn_dt_bias, m_gdn_norm_w, m_gdn_w_out, m_sc_w_in, m_sc_conv_w, m_sc_w_out, m_ssd_w_in, m_ssd_conv_w, m_ssd_conv_b, m_ssd_a_log, m_ssd_dt_bias, m_ssd_d_skip, m_ssd_norm_w, m_ssd_w_out, m_ln_g, m_ln_b, v_gdn_w_in, v_gdn_conv_w, v_gdn_a_log, v_gdn_dt_bias, v_gdn_norm_w, v_gdn_w_out, v_sc_w_in, v_sc_conv_w, v_sc_w_out, v_ssd_w_in, v_ssd_conv_w, v_ssd_conv_b, v_ssd_a_log, v_ssd_dt_bias, v_ssd_d_skip, v_ssd_norm_w, v_ssd_w_out, v_ln_g, v_ln_b):
    weights = [gdn_w_in, gdn_conv_w, gdn_a_log, gdn_dt_bias, gdn_norm_w, gdn_w_out, sc_w_in, sc_conv_w, sc_w_out,
               ssd_w_in, ssd_conv_w, ssd_conv_b, ssd_a_log, ssd_dt_bias, ssd_d_skip, ssd_norm_w, ssd_w_out, ln_g, ln_b]
    moms = [m_gdn_w_in, m_gdn_conv_w, m_gdn_a_log, m_gdn_dt_bias, m_gdn_norm_w, m_gdn_w_out, m_sc_w_in, m_sc_conv_w,
            m_sc_w_out, m_ssd_w_in, m_ssd_conv_w, m_ssd_conv_b, m_ssd_a_log, m_ssd_dt_bias, m_ssd_d_skip, m_ssd_norm_w,
            m_ssd_w_out, m_ln_g, m_ln_b]
    vels = [v_gdn_w_in, v_gdn_conv_w, v_gdn_a_log, v_gdn_dt_bias, v_gdn_norm_w, v_gdn_w_out, v_sc_w_in, v_sc_conv_w,
            v_sc_w_out, v_ssd_w_in, v_ssd_conv_w, v_ssd_conv_b, v_ssd_a_log, v_ssd_dt_bias, v_ssd_d_skip, v_ssd_norm_w,
            v_ssd_w_out, v_ln_g, v_ln_b]
    full = dict(zip([n for n, _, _ in W_SPECS], _gather_weights(weights)))
    x0 = x[0]
    target = loss_target[0]

    layers = (("gdn", 0, GDN_IN_PAD, GDN_IN), ("sc", 0, SC_IN, SC_IN), ("ssd", 0, SSD_IN_PAD, SSD_IN), ("gdn", 1, GDN_IN_PAD, GDN_IN))

    def params(kind, j):
        if kind == "gdn":
            return [full["gdn_conv_w"][j], full["gdn_a_log"][j], full["gdn_dt_bias"][j], full["gdn_norm_w"][j]]
        if kind == "sc":
            return [full["sc_conv_w"][j]]
        return [full["ssd_conv_w"][j], full["ssd_conv_b"][j], full["ssd_a_log"][j], full["ssd_dt_bias"][j],
                full["ssd_d_skip"][j], full["ssd_norm_w"][j]]

    xs, saved = [x0], []
    for i, (kind, j, n_pad, _) in enumerate(layers):
        w_in = full[kind + "_w_in"][j]
        w_out = full[kind + "_w_out"][j].astype(MXU_DTYPE)
        proj = matmul(xs[i], w_in, name=kind + "_proj")
        if kind == "gdn":
            mix, states = gdn_fwd(proj, *params(kind, j), name="gdn_fwd")
        elif kind == "sc":
            mix, states = sc_fwd(proj, *params(kind, j), name="sc_fwd"), None
        else:
            mix, states = ssd_fwd(proj, *params(kind, j), name="ssd_fwd")
        y = matmul(mix, w_out, name=kind + "_out")
        saved.append((w_in, w_out, proj, mix, states, y))
        if i + 1 < DEPTH:
            xs.append(ln_fwd(xs[i], y, full["ln_g"][i], full["ln_b"][i], name="ln_fwd"))

    grads = {n: [None] * s[0] for n, s, _ in W_SPECS}
    dr, dg, db, loss_rows = ln_bwd(xs[DEPTH - 1], saved[DEPTH - 1][5], full["ln_g"][DEPTH - 1], b=full["ln_b"][DEPTH - 1],
                                   target=target, name="ln_bwd_loss")
    dx = None
    for i in reversed(range(DEPTH)):
        kind, j, _, n_in = layers[i]
        w_in, w_out, proj, mix, states, _ = saved[i]
        grads["ln_g"][i], grads["ln_b"][i] = dg[0], db[0]
        dmix = matmul(dr, w_out, tb=True, name=kind + "_dmix")
        grads[kind + "_w_out"][j] = matmul(mix, dr, ta=True, name=kind + "_dw_out")
        if kind == "gdn":
            dproj, (dcw, dalog, ddtb, dnw) = gdn_bwd(proj, *params(kind, j), states, dmix, name="gdn_bwd")
            grads["gdn_conv_w"][j], grads["gdn_a_log"][j], grads["gdn_dt_bias"][j], grads["gdn_norm_w"][j] = dcw, dalog, ddtb, dnw
        elif kind == "sc":
            dproj, dcw = sc_bwd(proj, *params(kind, j), dmix, name="sc_bwd")
            grads["sc_conv_w"][j] = dcw[:SC_CONV]
        else:
            conv_w, _, *rest = params(kind, j)
            dproj, (dcw, dcb, dalog, ddtb, ddsk, dnw) = ssd_bwd(proj, conv_w, *rest, states, dmix, name="ssd_bwd")
            grads["ssd_conv_w"][j], grads["ssd_conv_b"][j], grads["ssd_a_log"][j] = dcw, dcb, dalog
            grads["ssd_dt_bias"][j], grads["ssd_d_skip"][j], grads["ssd_norm_w"][j] = ddtb, ddsk, dnw
        grads[kind + "_w_in"][j] = shard_split(matmul(xs[i], dproj, ta=True, name=kind + "_dw_in"), n_in, name="split_" + kind)
        dx = matmul(dproj, w_in, tb=True, add=dr, add_scale=ALPHA, name=kind + "_dx")
        if i > 0:
            dr, dg, db = ln_bwd(xs[i - 1], saved[i - 1][5], full["ln_g"][i - 1], dx, name="ln_bwd")

    full_grads = [jnp.stack(grads[n]) for n, _, _ in W_SPECS]
    g_out, d_out, m_out, v_out = _adamw_all(weights, _reduce_scatter(full_grads), moms, vels)
    loss = lax.psum(loss_rows[0, 0], MESH_AXES)
    return (loss, dx[None], *g_out, *d_out, *m_out, *v_out)
```

```python
import functools

import jax
import jax.numpy as jnp
from jax import lax
from jax.experimental import pallas as pl
from jax.experimental.pallas import tpu as pltpu

F32 = jnp.float32
MXU_DTYPE = jnp.bfloat16

D_MODEL = 1024
DEPTH = 4
D_INNER = 2048
CHUNK = 64
LANES = 128
SUBLANES = 8
VMEM_LIMIT = 56 * 1024 * 1024

GDN_HEAD = 128
GDN_VH = 16
GDN_QKH = 8
GDN_QK = 1024
GDN_V = 2048
GDN_CONV = 4
GDN_CONV_DIM = 4096
GDN_IN = 6176
GDN_IN_PAD = 6272

SC_W = 2048
SC_CONV = 3
SC_IN = 8192

SSD_P = 64
SSD_H = 32
SSD_G = 4
SSD_S = 128
SSD_CONV = 4
SSD_CONV_DIM = 3072
SSD_IN = 5152
SSD_IN_PAD = 5376

ALPHA = (2 * DEPTH) ** 0.25
RMS_EPS = 1e-6
LN_EPS = 1e-5
L2_EPS = 1e-6

ADAM_LR = 0.001
ADAM_B1 = 0.9
ADAM_B2 = 0.999
ADAM_EPS = 1e-08
ADAM_WD = 0.01
ADAM_STEP = 10

MESH_AXES = ("x", "y", "c")


def _cparams(sem):
    return pltpu.CompilerParams(dimension_semantics=sem, vmem_limit_bytes=VMEM_LIMIT)


def _pick(n, prefs):
    for p in prefs:
        if n % p == 0:
            return p
    return n


def _dot(a, b, dims=(((1,), (0,)), ((), ()))):
    return lax.dot_general(a.astype(MXU_DTYPE), b.astype(MXU_DTYPE), dims, preferred_element_type=F32)


def _dot_nt(a, b):
    return _dot(a, b, (((1,), (1,)), ((), ())))


def _dot_tn(a, b):
    return _dot(a, b, (((0,), (0,)), ((), ())))


NN = (((1,), (0,)), ((), ()))
NT = (((1,), (1,)), ((), ()))
TN = (((0,), (0,)), ((), ()))


def _mxu(a, b, dims):
    return lax.dot_general(a, b, dims, preferred_element_type=F32)


def _split(x, pieces):
    out, r = [], x
    for i in range(pieces):
        p = r.astype(jnp.bfloat16)
        out.append(p)
        if i + 1 < pieces:
            r = r - p.astype(F32)
    return out


def _sel(m, x, dims=NN):
    mb = m.astype(jnp.bfloat16)
    x1, x2, x3 = _split(x, 3)
    return (_mxu(mb, x3, dims) + _mxu(mb, x2, dims)) + _mxu(mb, x1, dims)


def _sel_nt(m, x):
    return _sel(m, x, NT)


def _xsel(x, m, dims=NN):
    mb = m.astype(jnp.bfloat16)
    x1, x2, x3 = _split(x, 3)
    return (_mxu(x3, mb, dims) + _mxu(x2, mb, dims)) + _mxu(x1, mb, dims)


def _xsel_nt(x, m):
    return _xsel(x, m, NT)


def _iota(shape, dim):
    return lax.broadcasted_iota(jnp.int32, shape, dim)


def _sigmoid(x):
    return 0.5 * jnp.tanh(0.5 * x) + 0.5


def _silu(x):
    return x * _sigmoid(x)


def _dsilu(x):
    s = _sigmoid(x)
    return s * (1.0 + x * (1.0 - s))


def _softplus(x):
    return jnp.maximum(x, 0.0) + jnp.log(1.0 + jnp.exp(-jnp.abs(x)))


def matmul(a, b, *, ta=False, tb=False, add=None, add_scale=1.0, name):
    if ta:
        kdim, m = a.shape
    else:
        m, kdim = a.shape
    n = b.shape[0] if tb else b.shape[1]
    assert (b.shape[1] if tb else b.shape[0]) == kdim
    tm = _pick(m, (1024, 896, 768, 512)) if ta else _pick(m, (2048, 1024, 512, 256, 128))
    tn = _pick(n, (1024, 896, 768, 512, 256, 128))
    tk = _pick(kdim, (1024, 512, 256)) if ta else _pick(kdim, (1024, 896, 768, 512))
    nk = kdim // tk
    dims = (((0 if ta else 1,), (1 if tb else 0,)), ((), ()))

    def body(a_ref, b_ref, *rest):
        o_ref = rest[-1]
        k = pl.program_id(2)
        part = _dot(a_ref[...], b_ref[...], dims)

        @pl.when(k == 0)
        def _():
            o_ref[...] = part if add is None else part + add_scale * rest[0][...]

        @pl.when(k > 0)
        def _():
            o_ref[...] += part

    a_spec = pl.BlockSpec((tk, tm), lambda i, j, k: (k, i)) if ta else pl.BlockSpec((tm, tk), lambda i, j, k: (i, k))
    b_spec = pl.BlockSpec((tn, tk), lambda i, j, k: (j, k)) if tb else pl.BlockSpec((tk, tn), lambda i, j, k: (k, j))
    o_spec = pl.BlockSpec((tm, tn), lambda i, j, k: (i, j))
    in_specs = [a_spec, b_spec] + ([] if add is None else [o_spec])
    args = (a, b) + (() if add is None else (add,))
    return pl.pallas_call(
        body, name=name, grid=(m // tm, n // tn, nk), in_specs=in_specs, out_specs=o_spec,
        out_shape=jax.ShapeDtypeStruct((m, n), F32),
        compiler_params=_cparams(("parallel", "parallel", "arbitrary")),
    )(*args)


LN_ROWS = 512


def _ln_stats(x, y):
    r = ALPHA * x + y
    mu = jnp.mean(r, axis=-1, keepdims=True)
    rc = r - mu
    var = jnp.mean(rc * rc, axis=-1, keepdims=True)
    rstd = lax.rsqrt(var + LN_EPS)
    return rc * rstd, rstd


def ln_fwd(x, y, g, b, *, name):
    rows, d = x.shape

    def body(x_ref, y_ref, g_ref, b_ref, o_ref, oc_ref):
        xhat, _ = _ln_stats(x_ref[...], y_ref[...])
        xn = xhat * g_ref[...] + b_ref[...]
        o_ref[...] = xn
        oc_ref[...] = xn.astype(oc_ref.dtype)

    blk = pl.BlockSpec((LN_ROWS, d), lambda i: (i, 0))
    vec = pl.BlockSpec((1, d), lambda i: (0, 0))
    return pl.pallas_call(
        body, name=name, grid=(rows // LN_ROWS,), in_specs=[blk, blk, vec, vec], out_specs=[blk, blk],
        out_shape=[jax.ShapeDtypeStruct((rows, d), F32), jax.ShapeDtypeStruct((rows, d), MXU_DTYPE)],
        compiler_params=_cparams(("parallel",)),
    )(x, y, g.reshape(1, d), b.reshape(1, d))


def ln_bwd(x, y, g, dxn=None, *, b=None, target=None, name):
    rows, d = x.shape
    final = target is not None

    def body(x_ref, y_ref, g_ref, *rest):
        if final:
            b_ref, t_ref, dr_ref, dg_ref, db_ref, loss_ref = rest
        else:
            dxn_ref, dr_ref, dg_ref, db_ref = rest
        i = pl.program_id(0)
        xhat, rstd = _ln_stats(x_ref[...], y_ref[...])
        gv = g_ref[...]
        if final:
            err = xhat * gv + b_ref[...] - t_ref[...]
            dxn_v = err * (1.0 / d)
            part = 0.5 * jnp.sum(jnp.mean(err * err, axis=-1, keepdims=True), axis=0, keepdims=True)
        else:
            dxn_v = dxn_ref[...]
        dxh = dxn_v * gv
        m1 = jnp.mean(dxh, axis=-1, keepdims=True)
        m2 = jnp.mean(dxh * xhat, axis=-1, keepdims=True)
        dr_ref[...] = rstd * (dxh - m1 - xhat * m2)

        @pl.when(i == 0)
        def _():
            dg_ref[...] = jnp.zeros_like(dg_ref)
            db_ref[...] = jnp.zeros_like(db_ref)
            if final:
                loss_ref[...] = jnp.zeros_like(loss_ref)

        dg_ref[...] += jnp.sum(dxn_v * xhat, axis=0, keepdims=True)
        db_ref[...] += jnp.sum(dxn_v, axis=0, keepdims=True)
        if final:
            loss_ref[...] += jnp.broadcast_to(part, loss_ref.shape)

    blk = pl.BlockSpec((LN_ROWS, d), lambda i: (i, 0))
    vec = pl.BlockSpec((1, d), lambda i: (0, 0))
    lvec = pl.BlockSpec((1, LANES), lambda i: (0, 0))
    out_shape = [jax.ShapeDtypeStruct((rows, d), F32), jax.ShapeDtypeStruct((1, d), F32), jax.ShapeDtypeStruct((1, d), F32)]
    out_specs = [blk, vec, vec]
    if final:
        in_specs = [blk, blk, vec, vec, blk]
        args = (x, y, g.reshape(1, d), b.reshape(1, d), target)
        out_shape.append(jax.ShapeDtypeStruct((1, LANES), F32))
        out_specs.append(lvec)
    else:
        in_specs = [blk, blk, vec, blk]
        args = (x, y, g.reshape(1, d), dxn)
    return pl.pallas_call(
        body, name=name, grid=(rows // LN_ROWS,), in_specs=in_specs, out_specs=out_specs, out_shape=out_shape,
        compiler_params=_cparams(("arbitrary",)),
    )(*args)


def _rows_from(ref, off, rows, cols=slice(None)):
    r = off % SUBLANES
    if r == 0:
        return ref[off:off + rows, cols]
    window = ref[off - r:off - r + rows + SUBLANES, cols]
    return pltpu.roll(window, rows + SUBLANES - r, axis=0)[:rows]


def _conv_from_ext(ext_ref, w, width, rows, cols=slice(None)):
    out = None
    for j in range(width):
        term = _rows_from(ext_ref, SUBLANES - (width - 1) + j, rows, cols) * w[j:j + 1, cols]
        out = term if out is None else out + term
    return out


def _conv_dgrad_from_ext(dext_ref, w, width, rows, cols):
    out = None
    for j in range(width):
        term = _rows_from(dext_ref, (width - 1) - j, rows, cols) * w[j:j + 1, cols]
        out = term if out is None else out + term
    return out


def _conv_bwd_from_ext(dext_ref, u, w, width, rows, cols):
    du, dws = None, []
    for j in range(width):
        shifted = _rows_from(dext_ref, (width - 1) - j, rows, cols)
        term = shifted * w[j:j + 1, cols]
        du = term if du is None else du + term
        dws.append(jnp.sum(shifted * u, axis=0, keepdims=True))
    return du, dws


CONV_COLS = 256


SC_ROWS = 128


def sc_fwd(proj, conv_w, *, name):
    rows = proj.shape[0]
    nb = rows // SC_ROWS
    hb = SC_ROWS // SUBLANES

    def body(p_ref, halo_ref, w_ref, o_ref, ext_ref):
        i = pl.program_id(0)
        w = w_ref[...]
        for c0 in range(0, SC_W, CONV_COLS):
            cols, bc, cc, zc = (slice(k * SC_W + c0, k * SC_W + c0 + CONV_COLS) for k in range(4))
            ext_ref[0:SUBLANES, cols] = jnp.where(i == 0, 0.0, halo_ref[:, cc] * halo_ref[:, cols])
            ext_ref[SUBLANES:, cols] = p_ref[:, cc] * p_ref[:, cols]
            cv = _conv_from_ext(ext_ref, w, SC_CONV, SC_ROWS, cols)
            o_ref[:, cols] = (p_ref[:, bc] * cv * _silu(p_ref[:, zc])).astype(o_ref.dtype)

    return pl.pallas_call(
        body, name=name, grid=(nb,),
        in_specs=[pl.BlockSpec((SC_ROWS, SC_IN), lambda i: (i, 0)),
                  pl.BlockSpec((SUBLANES, SC_IN), lambda i: (jnp.maximum(i * hb - 1, 0), 0)),
                  pl.BlockSpec((SUBLANES, SC_W), lambda i: (0, 0))],
        out_specs=pl.BlockSpec((SC_ROWS, SC_W), lambda i: (i, 0)),
        out_shape=jax.ShapeDtypeStruct((rows, SC_W), MXU_DTYPE),
        scratch_shapes=[pltpu.VMEM((SC_ROWS + SUBLANES, SC_W), F32)],
        compiler_params=_cparams(("parallel",)),
    )(proj, proj, _pad_rows(conv_w))


def sc_bwd(proj, conv_w, dmix, *, name):
    rows = proj.shape[0]
    nb = rows // SC_ROWS
    hb = SC_ROWS // SUBLANES

    def body(p_ref, halo_ref, w_ref, dm_ref, dp_ref, dw_ref, ext_ref, dext_ref):
        i = pl.program_id(0)
        blk = nb - 1 - i
        w = w_ref[...]

        @pl.when(i == 0)
        def _():
            dext_ref[SC_ROWS:, :] = jnp.zeros((SUBLANES, SC_W), F32)
            dw_ref[...] = jnp.zeros_like(dw_ref)

        for c0 in range(0, SC_W, CONV_COLS):
            cols, bc, cc, zc = (slice(k * SC_W + c0, k * SC_W + c0 + CONV_COLS) for k in range(4))
            h, bg, cg, z = p_ref[:, cols], p_ref[:, bc], p_ref[:, cc], p_ref[:, zc]
            ext_ref[0:SUBLANES, cols] = jnp.where(blk == 0, 0.0, halo_ref[:, cc] * halo_ref[:, cols])
            ext_ref[SUBLANES:, cols] = cg * h
            taps = [_rows_from(ext_ref, SUBLANES - (SC_CONV - 1) + j, SC_ROWS, cols) for j in range(SC_CONV)]
            cv = None
            for j in range(SC_CONV):
                term = taps[j] * w[j:j + 1, cols]
                cv = term if cv is None else cv + term
            dm = dm_ref[:, cols]
            dy = dm * _silu(z)
            dp_ref[:, zc] = (dm * bg * cv * _dsilu(z)).astype(dp_ref.dtype)
            dp_ref[:, bc] = (dy * cv).astype(dp_ref.dtype)
            dcv = dy * bg
            dext_ref[0:SC_ROWS, cols] = dcv
            du = _conv_dgrad_from_ext(dext_ref, w, SC_CONV, SC_ROWS, cols)
            dp_ref[:, cols] = (du * cg).astype(dp_ref.dtype)
            dp_ref[:, cc] = (du * h).astype(dp_ref.dtype)
            for j in range(SC_CONV):
                dw_ref[j:j + 1, cols] += jnp.sum(taps[j] * dcv, axis=0, keepdims=True)
            dext_ref[SC_ROWS:, cols] = dcv[0:SUBLANES, :]

    return pl.pallas_call(
        body, name=name, grid=(nb,),
        in_specs=[pl.BlockSpec((SC_ROWS, SC_IN), lambda i: (nb - 1 - i, 0)),
                  pl.BlockSpec((SUBLANES, SC_IN), lambda i: (jnp.maximum((nb - 1 - i) * hb - 1, 0), 0)),
                  pl.BlockSpec((SUBLANES, SC_W), lambda i: (0, 0)),
                  pl.BlockSpec((SC_ROWS, SC_W), lambda i: (nb - 1 - i, 0))],
        out_specs=[pl.BlockSpec((SC_ROWS, SC_IN), lambda i: (nb - 1 - i, 0)),
                   pl.BlockSpec((SUBLANES, SC_W), lambda i: (0, 0))],
        out_shape=[jax.ShapeDtypeStruct((rows, SC_IN), MXU_DTYPE), jax.ShapeDtypeStruct((SUBLANES, SC_W), F32)],
        scratch_shapes=[pltpu.VMEM((SC_ROWS + SUBLANES, SC_W), F32), pltpu.VMEM((SC_ROWS + SUBLANES, SC_W), F32)],
        compiler_params=_cparams(("arbitrary",)),
    )(proj, proj, _pad_rows(conv_w), dmix)


def _pad_rows(w, rows=SUBLANES):
    return jnp.pad(w, ((0, rows - w.shape[0]), (0, 0)))


def _pad_lanes(v, lanes=LANES):
    v = v.reshape(1, -1)
    return jnp.pad(v, ((0, 0), (0, lanes - v.shape[1])))


def _tri(n, lower=True):
    r, c = _iota((n, n), 0), _iota((n, n), 1)
    return jnp.where((c <= r) if lower else (c >= r), 1.0, 0.0)


def _eye(n):
    return jnp.where(_iota((n, n), 0) == _iota((n, n), 1), 1.0, 0.0)


def _head_expand(n, width):
    return jnp.where(_iota((LANES, n), 1) // width == _iota((LANES, n), 0), 1.0, 0.0)


def _col(v, h):
    return jnp.sum(jnp.where(_iota(v.shape, 1) == h, v, 0.0), axis=1, keepdims=True)


def _row(v, r):
    return jnp.sum(jnp.where(_iota(v.shape, 0) == r, v, 0.0), axis=0, keepdims=True)


def _expand_row(v, e):
    return jnp.max(_xsel(jnp.broadcast_to(v, (SUBLANES, LANES)), e), axis=0, keepdims=True)


SSD_ROWS = 128
SSD_X0 = D_INNER
SSD_DT0 = D_INNER + SSD_CONV_DIM
SSD_B0 = D_INNER
SSD_C0 = D_INNER + SSD_G * SSD_S
SSD_GW = D_INNER // SSD_G
SSD_HG = SSD_H // SSD_G


def _ssd_prologue(blk, p_ref, halo_ref, cw_ref, cb_ref, dtb_ref, ext_ref, xbc_ref, dt_ref, pre_ref=None):
    ext_ref[0:SUBLANES, :] = jnp.where(blk == 0, 0.0, halo_ref[:, SSD_X0:SSD_DT0])
    ext_ref[SUBLANES:, :] = p_ref[:, SSD_X0:SSD_DT0]
    w = cw_ref[...]
    for c0 in range(0, SSD_CONV_DIM, CONV_COLS):
        cols = slice(c0, c0 + CONV_COLS)
        pre = _conv_from_ext(ext_ref, w, SSD_CONV, SSD_ROWS, cols) + cb_ref[:, cols]
        if pre_ref is not None:
            pre_ref[:, cols] = pre
        xbc_ref[:, cols] = _silu(pre)
    dt_ref[...] = _softplus(p_ref[:, SSD_DT0:SSD_DT0 + LANES] + dtb_ref[...])


def _ssd_chunk_decays(dt_c, a_row, ltri, eye_l, act_ref):
    da = dt_c * a_row
    ac = _sel(ltri,da)
    act_ref[...] = _sel_nt(eye_l,ac)
    ac_last = _row(ac, CHUNK - 1)
    return ac, jnp.exp(ac_last - ac), jnp.exp(ac), jnp.exp(ac_last)


def _ssd_seg(ac, act_ref, h, causal):
    return jnp.where(causal, jnp.exp(jnp.minimum(_col(ac, h) - act_ref[pl.ds(h, 1), :], 0.0)), 0.0)


def _ssd_half(pair, e):
    upper = _iota(pair.shape, 1) >= SSD_P
    return jnp.where(upper if e % 2 else jnp.logical_not(upper), pair, 0.0)


def _ssd_groups_fwd(xbc_ref, rows, dt_exp, tail_exp, cdec_exp, ac, act_ref, states, causal):
    gs, heads = range(SSD_G), range(SSD_HG)
    gls = [slice(g * SSD_GW, (g + 1) * SSD_GW) for g in gs]
    bg = [_mx(xbc_ref[rows, SSD_B0 + g * SSD_S:SSD_B0 + (g + 1) * SSD_S]) for g in gs]
    cg = [_mx(xbc_ref[rows, SSD_C0 + g * SSD_S:SSD_C0 + (g + 1) * SSD_S]) for g in gs]
    s_c = [_mx(s) for s in states]
    xdt = [xbc_ref[rows, gl] * dt_exp[:, gl] for gl in gls]
    cb = [_dot_nt(cg[g], bg[g]) for g in gs]
    cs = [_dot(cg[g], s_c[g]) for g in gs]
    segs = [[_ssd_seg(ac, act_ref, g * SSD_HG + e, causal) for e in heads] for g in gs]
    gms = [[seg * cb[g] for seg in segs[g]] for g in gs]
    gms_c = [[_mx(gm) for gm in gms[g]] for g in gs]
    halves = [[_ssd_half(xdt[g][:, (e // 2) * LANES:(e // 2 + 1) * LANES], e) for e in heads] for g in gs]
    parts = [[_dot(gms_c[g][e], halves[g][e]) for e in heads] for g in gs]
    yd = [jnp.concatenate([parts[g][2 * p] + parts[g][2 * p + 1] for p in range(SSD_HG // 2)], axis=1) for g in gs]
    st = [_dot_tn(bg[g], xdt[g] * tail_exp[:, gls[g]]) for g in gs]
    return [(yd[g] + cs[g] * cdec_exp[:, gls[g]], st[g], bg[g], cg[g], s_c[g], cb[g], xdt[g], cs[g],
             segs[g], gms[g], gms_c[g]) for g in gs]


def ssd_fwd(proj, conv_w, conv_b, a_log, dt_bias, d_skip, norm_w, *, name):
    rows_total = proj.shape[0]
    nb = rows_total // SSD_ROWS
    hb = SSD_ROWS // SUBLANES
    cpb = SSD_ROWS // CHUNK

    def body(p_ref, halo_ref, cw_ref, cb_ref, alog_ref, dtb_ref, dsk_ref, nw_ref, mix_ref, st_ref, xbc_ref, pre_ref,
             ext_ref, dt_ref, s_ref, act_ref):
        i = pl.program_id(0)

        @pl.when(i == 0)
        def _():
            s_ref[...] = jnp.zeros_like(s_ref)

        _ssd_prologue(i, p_ref, halo_ref, cw_ref, cb_ref, dtb_ref, ext_ref, xbc_ref, dt_ref, pre_ref)
        a_row = -jnp.exp(alog_ref[...])
        expand = _head_expand(D_INNER, SSD_P)
        dsk_exp = _expand_row(dsk_ref[...], expand)
        ltri, eye_l = _tri(CHUNK), _eye(LANES)
        causal = _iota((CHUNK, CHUNK), 1) <= _iota((CHUNK, CHUNK), 0)

        def chunk(c, carry):
            rows = pl.ds(pl.multiple_of(c * CHUNK, CHUNK), CHUNK)
            dt_c = dt_ref[rows, :]
            ac, tail, cdec, tot = _ssd_chunk_decays(dt_c, a_row, ltri, eye_l, act_ref)
            dt_exp = _xsel(dt_c, expand)
            tail_exp = _xsel(tail, expand)
            cdec_exp = _xsel(cdec, expand)
            tot_exp = _expand_row(tot, expand)
            states = [s_ref[g] for g in range(SSD_G)]
            fwd = _ssd_groups_fwd(xbc_ref, rows, dt_exp, tail_exp, cdec_exp, ac, act_ref, states, causal)
            for g in range(SSD_G):
                gl = slice(g * SSD_GW, (g + 1) * SSD_GW)
                st_ref[c, g] = states[g]
                y, st = fwd[g][:2]
                s_ref[g] = states[g] * tot_exp[:, gl] + st
                y = (y + dsk_exp[:, gl] * xbc_ref[rows, gl]) * _silu(p_ref[rows, gl])
                r = lax.rsqrt(jnp.mean(y * y, axis=-1, keepdims=True) + RMS_EPS)
                mix_ref[rows, gl] = (y * r * nw_ref[:, gl]).astype(mix_ref.dtype)
            return carry

        lax.fori_loop(0, cpb, chunk, 0)

    vec = lambda n: pl.BlockSpec((1, n), lambda i: (0, 0))
    mix, states, xbc, pre = pl.pallas_call(
        body, name=name, grid=(nb,),
        in_specs=[pl.BlockSpec((SSD_ROWS, SSD_IN_PAD), lambda i: (i, 0)),
                  pl.BlockSpec((SUBLANES, SSD_IN_PAD), lambda i: (jnp.maximum(i * hb - 1, 0), 0)),
                  pl.BlockSpec((SUBLANES, SSD_CONV_DIM), lambda i: (0, 0)),
                  vec(SSD_CONV_DIM), vec(LANES), vec(LANES), vec(LANES), vec(D_INNER)],
        out_specs=[pl.BlockSpec((SSD_ROWS, D_INNER), lambda i: (i, 0)),
                   pl.BlockSpec((cpb, SSD_G, SSD_S, SSD_GW), lambda i: (i, 0, 0, 0)),
                   pl.BlockSpec((SSD_ROWS, SSD_CONV_DIM), lambda i: (i, 0)),
                   pl.BlockSpec((SSD_ROWS, SSD_CONV_DIM), lambda i: (i, 0))],
        out_shape=[jax.ShapeDtypeStruct((rows_total, D_INNER), MXU_DTYPE),
                   jax.ShapeDtypeStruct((rows_total // CHUNK, SSD_G, SSD_S, SSD_GW), F32),
                   jax.ShapeDtypeStruct((rows_total, SSD_CONV_DIM), F32),
                   jax.ShapeDtypeStruct((rows_total, SSD_CONV_DIM), F32)],
        scratch_shapes=[pltpu.VMEM((SSD_ROWS + SUBLANES, SSD_CONV_DIM), F32),
                        pltpu.VMEM((SSD_ROWS, LANES), F32),
                        pltpu.VMEM((SSD_G, SSD_S, SSD_GW), F32),
                        pltpu.VMEM((LANES, CHUNK), F32)],
        compiler_params=_cparams(("arbitrary",)),
    )(proj, proj, _pad_rows(conv_w), conv_b.reshape(1, -1), _pad_lanes(a_log), _pad_lanes(dt_bias),
      _pad_lanes(d_skip), norm_w.reshape(1, -1))
    return mix, (states, xbc, pre)


def ssd_bwd(proj, conv_w, a_log, dt_bias, d_skip, norm_w, saved, dmix, *, name):
    states, xbc, pre = saved
    rows_total = proj.shape[0]
    nb = rows_total // SSD_ROWS
    cpb = SSD_ROWS // CHUNK

    def body(p_ref, cw_ref, alog_ref, dtb_ref, dsk_ref, nw_ref, st_ref, dm_ref, xbc_ref, pre_ref,
             dp_ref, dcw_ref, dcb_ref, dalog_ref, ddtb_ref, ddsk_ref, dnw_ref,
             dt_ref, ds_ref, act_ref, dext_ref, dac_ref, dact_ref, ddskw_ref):
        i = pl.program_id(0)

        @pl.when(i == 0)
        def _():
            ds_ref[...] = jnp.zeros_like(ds_ref)
            dext_ref[SSD_ROWS:, :] = jnp.zeros((SUBLANES, SSD_CONV_DIM), F32)
            ddskw_ref[...] = jnp.zeros_like(ddskw_ref)
            for r in (dcw_ref, dcb_ref, dalog_ref, ddtb_ref, ddsk_ref, dnw_ref):
                r[...] = jnp.zeros_like(r)

        dt_ref[...] = _softplus(p_ref[:, SSD_DT0:SSD_DT0 + LANES] + dtb_ref[...])
        a_row = -jnp.exp(alog_ref[...])
        expand = _head_expand(D_INNER, SSD_P)
        dsk_exp = _expand_row(dsk_ref[...], expand)
        ltri, utri, eye_l, eye_c = _tri(CHUNK), _tri(CHUNK, lower=False), _eye(LANES), _eye(CHUNK)
        causal = _iota((CHUNK, CHUNK), 1) <= _iota((CHUNK, CHUNK), 0)
        dp_ref[:, SSD_DT0 + LANES:] = jnp.zeros((SSD_ROWS, SSD_IN_PAD - SSD_DT0 - LANES), dp_ref.dtype)

        def chunk(cc, carry):
            c = cpb - 1 - cc
            rows = pl.ds(pl.multiple_of(c * CHUNK, CHUNK), CHUNK)
            dt_c = dt_ref[rows, :]
            ac, tail, cdec, tot = _ssd_chunk_decays(dt_c, a_row, ltri, eye_l, act_ref)
            dt_exp = _xsel(dt_c, expand)
            tail_exp = _xsel(tail, expand)
            cdec_exp = _xsel(cdec, expand)
            tot_exp = _expand_row(tot, expand)
            dac_ref[...] = jnp.zeros_like(dac_ref)
            dact_ref[...] = jnp.zeros_like(dact_ref)
            d_cdec = jnp.zeros((CHUNK, LANES), F32)
            d_tail = jnp.zeros((CHUNK, LANES), F32)
            d_dt = jnp.zeros((CHUNK, LANES), F32)
            d_tot = jnp.zeros((1, LANES), F32)
            gs = range(SSD_G)
            gls = [slice(g * SSD_GW, (g + 1) * SSD_GW) for g in gs]
            exs = [expand[:, gl] for gl in gls]
            states = [st_ref[c, g] for g in gs]
            fwd = _ssd_groups_fwd(xbc_ref, rows, dt_exp, tail_exp, cdec_exp, ac, act_ref, states, causal)
            ys, _, bgs, cgs, s_cs, cbs, xdts, css, segss, gmss, gms_cs = (list(t) for t in zip(*fwd))
            xss = [xbc_ref[rows, gl] for gl in gls]
            dys = []
            for g, gl in enumerate(gls):
                z = p_ref[rows, gl]
                sz = _silu(z)
                y2 = ys[g] + dsk_exp[:, gl] * xss[g]
                yg = y2 * sz
                r = lax.rsqrt(jnp.mean(yg * yg, axis=-1, keepdims=True) + RMS_EPS)
                yn = yg * r
                dm = dm_ref[rows, gl]
                dnw_ref[:, gl] += jnp.sum(dm * yn, axis=0, keepdims=True)
                dyn = dm * nw_ref[:, gl]
                dyg = r * (dyn - yn * jnp.mean(dyn * yn, axis=-1, keepdims=True))
                dp_ref[rows, gl] = (dyg * y2 * _dsilu(z)).astype(dp_ref.dtype)
                dys.append(dyg * sz)
                ddskw_ref[:, gl] += jnp.sum(dys[g] * xss[g], axis=0, keepdims=True)
            ds_gs = [ds_ref[g] for g in gs]
            ds_cs = [_mx(d) for d in ds_gs]
            dycs = [_mx(dys[g] * cdec_exp[:, gls[g]]) for g in gs]
            ds_new = [_dot_tn(cgs[g], dycs[g]) for g in gs]
            dcgs = [_dot_nt(dycs[g], s_cs[g]) for g in gs]
            d_xdtds = [_dot(bgs[g], ds_cs[g]) for g in gs]
            dbgs = [_dot_nt(xdts[g] * tail_exp[:, gls[g]], ds_cs[g]) for g in gs]
            for g in gs:
                ds_ref[g] = ds_gs[g] * tot_exp[:, gls[g]] + ds_new[g]
                sds = jnp.broadcast_to(jnp.sum(states[g] * ds_gs[g], axis=0, keepdims=True), (SUBLANES, SSD_GW))
                d_tot = d_tot + jnp.max(_xsel_nt(sds, exs[g]), axis=0, keepdims=True)
                d_cdec = d_cdec + _xsel_nt(dys[g] * css[g], exs[g])
                d_tail = d_tail + _xsel_nt(d_xdtds[g] * xdts[g], exs[g])
            heads = range(SSD_HG)
            dy_hs = [[_mx(_ssd_half(dys[g][:, (e // 2) * LANES:(e // 2 + 1) * LANES], e)) for e in heads] for g in gs]
            xps_cs = [[_mx(xdts[g][:, p * LANES:(p + 1) * LANES]) for p in range(SSD_HG // 2)] for g in gs]
            backs = [[_dot_tn(gms_cs[g][e], dy_hs[g][e]) for e in heads] for g in gs]
            dg_ms = [[jnp.where(causal, _dot_nt(dy_hs[g][e], xps_cs[g][e // 2]), 0.0) for e in heads] for g in gs]
            d_cbs = []
            for g in gs:
                d_cb = None
                for e in heads:
                    h = g * SSD_HG + e
                    term = dg_ms[g][e] * segss[g][e]
                    d_cb = term if d_cb is None else d_cb + term
                    em = dg_ms[g][e] * gmss[g][e]
                    dac_ref[...] += jnp.where(_iota((CHUNK, LANES), 1) == h, jnp.sum(em, axis=1, keepdims=True), 0.0)
                    dact_ref[h:h + 1, :] = jnp.sum(em, axis=0, keepdims=True)
                d_cbs.append(_mx(d_cb))
            dcg2 = [_dot(d_cbs[g], bgs[g]) for g in gs]
            dbg2 = [_dot_tn(d_cbs[g], cgs[g]) for g in gs]
            for g, gl in enumerate(gls):
                d_xdt = d_xdtds[g] * tail_exp[:, gl] + jnp.concatenate(
                    [backs[g][2 * p] + backs[g][2 * p + 1] for p in range(SSD_HG // 2)], axis=1)
                d_dt = d_dt + _xsel_nt(d_xdt * xss[g], exs[g])
                dext_ref[rows, gl] = d_xdt * dt_exp[:, gl] + dys[g] * dsk_exp[:, gl]
                dext_ref[rows, SSD_B0 + g * SSD_S:SSD_B0 + (g + 1) * SSD_S] = dbgs[g] + dbg2[g]
                dext_ref[rows, SSD_C0 + g * SSD_S:SSD_C0 + (g + 1) * SSD_S] = dcgs[g] + dcg2[g]
            d_ac = dac_ref[...] - _sel_nt(eye_c,dact_ref[...]) + d_cdec * cdec - d_tail * tail
            d_last = jnp.sum(d_tail * tail, axis=0, keepdims=True) + d_tot * tot
            d_ac = jnp.where(_iota((CHUNK, LANES), 0) == CHUNK - 1, d_ac + d_last, d_ac)
            d_da = _sel(utri,d_ac)
            d_dt = d_dt + d_da * a_row
            dalog_ref[...] += jnp.sum(d_da * dt_c, axis=0, keepdims=True) * a_row
            d_raw = d_dt * _sigmoid(p_ref[rows, SSD_DT0:SSD_DT0 + LANES] + dtb_ref[...])
            d_raw = jnp.where(_iota((CHUNK, LANES), 1) < SSD_H, d_raw, 0.0)
            ddtb_ref[...] += jnp.sum(d_raw, axis=0, keepdims=True)
            dp_ref[rows, SSD_DT0:SSD_DT0 + LANES] = d_raw.astype(dp_ref.dtype)
            return carry

        lax.fori_loop(0, cpb, chunk, 0)
        w = cw_ref[...]
        for c0 in range(0, SSD_CONV_DIM, CONV_COLS):
            cols = slice(c0, c0 + CONV_COLS)
            d_pre = dext_ref[0:SSD_ROWS, cols] * _dsilu(pre_ref[:, cols])
            dext_ref[0:SSD_ROWS, cols] = d_pre
            dcb_ref[:, cols] += jnp.sum(d_pre, axis=0, keepdims=True)
            du, dws = _conv_bwd_from_ext(dext_ref, p_ref[:, SSD_X0 + c0:SSD_X0 + c0 + CONV_COLS], w, SSD_CONV, SSD_ROWS, cols)
            for j in range(SSD_CONV):
                dcw_ref[j:j + 1, cols] += dws[j]
            dp_ref[:, SSD_X0 + c0:SSD_X0 + c0 + CONV_COLS] = du.astype(dp_ref.dtype)
            dext_ref[SSD_ROWS:, cols] = d_pre[0:SUBLANES, :]

        @pl.when(i == nb - 1)
        def _():
            ddsk_ref[...] = jnp.max(_xsel_nt(jnp.broadcast_to(ddskw_ref[...], (SUBLANES, D_INNER)), expand), axis=0, keepdims=True)

    vec = lambda n: pl.BlockSpec((1, n), lambda i: (0, 0))
    outs = pl.pallas_call(
        body, name=name, grid=(nb,),
        in_specs=[pl.BlockSpec((SSD_ROWS, SSD_IN_PAD), lambda i: (nb - 1 - i, 0)),
                  pl.BlockSpec((SUBLANES, SSD_CONV_DIM), lambda i: (0, 0)),
                  vec(LANES), vec(LANES), vec(LANES), vec(D_INNER),
                  pl.BlockSpec((cpb, SSD_G, SSD_S, SSD_GW), lambda i: (nb - 1 - i, 0, 0, 0)),
                  pl.BlockSpec((SSD_ROWS, D_INNER), lambda i: (nb - 1 - i, 0)),
                  pl.BlockSpec((SSD_ROWS, SSD_CONV_DIM), lambda i: (nb - 1 - i, 0)),
                  pl.BlockSpec((SSD_ROWS, SSD_CONV_DIM), lambda i: (nb - 1 - i, 0))],
        out_specs=[pl.BlockSpec((SSD_ROWS, SSD_IN_PAD), lambda i: (nb - 1 - i, 0)),
                   pl.BlockSpec((SUBLANES, SSD_CONV_DIM), lambda i: (0, 0)),
                   vec(SSD_CONV_DIM), vec(LANES), vec(LANES), vec(LANES), vec(D_INNER)],
        out_shape=[jax.ShapeDtypeStruct((rows_total, SSD_IN_PAD), MXU_DTYPE),
                   jax.ShapeDtypeStruct((SUBLANES, SSD_CONV_DIM), F32),
                   jax.ShapeDtypeStruct((1, SSD_CONV_DIM), F32), jax.ShapeDtypeStruct((1, LANES), F32),
                   jax.ShapeDtypeStruct((1, LANES), F32), jax.ShapeDtypeStruct((1, LANES), F32),
                   jax.ShapeDtypeStruct((1, D_INNER), F32)],
        scratch_shapes=[pltpu.VMEM((SSD_ROWS, LANES), F32),
                        pltpu.VMEM((SSD_G, SSD_S, SSD_GW), F32),
                        pltpu.VMEM((LANES, CHUNK), F32),
                        pltpu.VMEM((SSD_ROWS + SUBLANES, SSD_CONV_DIM), F32),
                        pltpu.VMEM((CHUNK, LANES), F32),
                        pltpu.VMEM((LANES, CHUNK), F32),
                        pltpu.VMEM((1, D_INNER), F32)],
        compiler_params=_cparams(("arbitrary",)),
    )(proj, _pad_rows(conv_w), _pad_lanes(a_log), _pad_lanes(dt_bias),
      _pad_lanes(d_skip), norm_w.reshape(1, -1), states, dmix, xbc, pre)
    dproj, dcw, dcb, dalog, ddtb, ddsk, dnw = outs
    return dproj, [dcw[:SSD_CONV], dcb[0], dalog[0, :SSD_H], ddtb[0, :SSD_H], ddsk[0, :SSD_H], dnw[0]]


GDN_ROWS = 128
GDN_K0 = GDN_QK
GDN_V0 = 2 * GDN_QK
GDN_Z0 = GDN_CONV_DIM
GDN_BA0 = GDN_CONV_DIM + GDN_V
GDN_GL = GDN_VH
GDN_SCALE = GDN_HEAD ** -0.5
GDN_GROUP = 16


def _gdn_lane_params(v):
    return jnp.pad(v.reshape(1, GDN_VH), ((0, 0), (GDN_GL, LANES - GDN_GL - GDN_VH)))


def _gdn_prologue(blk, p_ref, halo_ref, cw_ref, alog_ref, dtb_ref, ext_ref, qkv_ref, beta_ref, g_ref, pre_ref=None):
    ext_ref[0:SUBLANES, :] = jnp.where(blk == 0, 0.0, halo_ref[:, 0:GDN_CONV_DIM])
    ext_ref[SUBLANES:, :] = p_ref[:, 0:GDN_CONV_DIM]
    w = cw_ref[...]
    for hh in range(GDN_CONV_DIM // GDN_HEAD):
        cols = slice(hh * GDN_HEAD, (hh + 1) * GDN_HEAD)
        pre = _conv_from_ext(ext_ref, w, GDN_CONV, GDN_ROWS, cols)
        if pre_ref is not None:
            pre_ref[:, cols] = pre
        a = _silu(pre)
        if hh < 2 * GDN_QKH:
            r = lax.rsqrt(jnp.sum(a * a, axis=-1, keepdims=True) + L2_EPS)
            a = a * (r * (GDN_SCALE if hh < GDN_QKH else 1.0))
        qkv_ref[:, cols] = a
    _gdn_gates(p_ref, alog_ref, dtb_ref, beta_ref, g_ref)


def _gdn_gates(p_ref, alog_ref, dtb_ref, beta_ref, g_ref):
    ba = p_ref[:, GDN_BA0:GDN_BA0 + LANES]
    beta_ref[...] = _sigmoid(ba)
    g_ref[...] = -jnp.exp(alog_ref[...]) * _softplus(ba + dtb_ref[...])


def _each(f, *lists):
    return [f(*z) for z in zip(*lists)]


def _inv_unit_lower_each(a_list, eye_c):
    xs = [eye_c - a for a in a_list]
    ps = [_mx(a) for a in a_list]
    n = 2
    while n < CHUNK:
        ps = [_mx(_dot(p, p)) for p in ps]
        xs = [x + _dot(x, p) for x, p in zip(xs, ps)]
        n *= 2
    return xs


def _mx(x):
    return x.astype(MXU_DTYPE)


def _gdn_heads_fwd(q, k, v, kk, qk, gcol, grow, glast, bcol, s, causal, strict, eye_c, t=None):
    decay = _each(lambda gc_, gr_: jnp.where(causal, jnp.exp(jnp.minimum(gc_ - gr_, 0.0)), 0.0), gcol, grow)
    egc = _each(jnp.exp, gcol)
    etail = _each(lambda gl_, gc_: jnp.exp(gl_ - gc_), glast, gcol)
    cd = _each(jnp.exp, glast)
    a = _each(lambda b_, kk_, d_: jnp.where(strict, b_ * kk_ * d_, 0.0), bcol, kk, decay)
    if t is None:
        t = _inv_unit_lower_each(a, eye_c)
    t_c, s_c = _each(_mx, t), _each(_mx, s)
    kb = _each(lambda k_, b_: k_ * b_, k, bcol)
    rhs_w = _each(lambda kb_, e_: kb_ * e_, kb, egc)
    u = _each(lambda t_, v_, b_: _dot(t_, v_ * b_), t_c, v, bcol)
    w = _each(_dot, t_c, rhs_w)
    w_c = _each(_mx, w)
    attn = _each(lambda qk_, d_: qk_ * d_, qk, decay)
    attn_c = _each(_mx, attn)
    ws = _each(_dot, w_c, s_c)
    v_new = _each(lambda u_, ws_: u_ - ws_, u, ws)
    vn_c = _each(_mx, v_new)
    qd = _each(lambda q_, e_: q_ * e_, q, egc)
    kt = _each(lambda k_, e_: k_ * e_, k, etail)
    qd_c, kt_c = _each(_mx, qd), _each(_mx, kt)
    o1 = _each(_dot, qd_c, s_c)
    o2 = _each(_dot, attn_c, vn_c)
    out = _each(lambda a_, b_: a_ + b_, o1, o2)
    upd = _each(_dot_tn, kt_c, vn_c)
    s_new = _each(lambda s_, c_, u_: s_ * c_ + u_, s, cd, upd)
    return dict(decay=decay, egc=egc, etail=etail, cd=cd, a=a, t=t, kb=kb, rhs_w=rhs_w, u=u, w=w, attn=attn,
                v_new=v_new, qd=qd, kt=kt, out=out, s_new=s_new,
                t_c=t_c, s_c=s_c, w_c=w_c, attn_c=attn_c, vn_c=vn_c, qd_c=qd_c, kt_c=kt_c)


def gdn_fwd(proj, conv_w, a_log, dt_bias, norm_w, *, name):
    rows_total = proj.shape[0]
    nb = rows_total // GDN_ROWS
    hb = GDN_ROWS // SUBLANES
    cpb = GDN_ROWS // CHUNK

    def body(p_ref, halo_ref, cw_ref, alog_ref, dtb_ref, nw_ref, mix_ref, st_ref, tm_ref, qkv_ref, pre_ref,
             ext_ref, beta_ref, g_ref, s_ref, gct_ref):
        i = pl.program_id(0)

        @pl.when(i == 0)
        def _():
            s_ref[...] = jnp.zeros_like(s_ref)

        _gdn_prologue(i, p_ref, halo_ref, cw_ref, alog_ref, dtb_ref, ext_ref, qkv_ref, beta_ref, g_ref, pre_ref)
        ltri, eye_l, eye_c = _tri(CHUNK), _eye(LANES), _eye(CHUNK)
        causal = _iota((CHUNK, CHUNK), 1) <= _iota((CHUNK, CHUNK), 0)
        strict = _iota((CHUNK, CHUNK), 1) < _iota((CHUNK, CHUNK), 0)
        nw = nw_ref[...]

        def chunk(c, carry):
            rows = pl.ds(pl.multiple_of(c * CHUNK, CHUNK), CHUNK)
            gc = _sel(ltri,g_ref[rows, :])
            gct_ref[...] = _sel_nt(eye_l,gc)
            glast_row = _row(gc, CHUNK - 1)
            beta_c = beta_ref[rows, :]
            for h0 in range(0, GDN_VH, GDN_GROUP):
                hs = list(range(h0, h0 + GDN_GROUP))
                qs = {hq: qkv_ref[rows, hq * GDN_HEAD:(hq + 1) * GDN_HEAD] for hq in range(h0 // 2, (h0 + GDN_GROUP) // 2)}
                ks = {hq: qkv_ref[rows, GDN_K0 + hq * GDN_HEAD:GDN_K0 + (hq + 1) * GDN_HEAD] for hq in qs}
                ks_c = {hq: _mx(ks[hq]) for hq in qs}
                kks = {hq: _dot_nt(ks_c[hq], ks_c[hq]) for hq in qs}
                qks = {hq: _dot_nt(qs[hq], ks_c[hq]) for hq in qs}
                ss = [s_ref[h] for h in hs]
                for h, s in zip(hs, ss):
                    st_ref[c, h] = s
                f = _gdn_heads_fwd(
                    [qs[h // 2] for h in hs], [ks[h // 2] for h in hs],
                    [qkv_ref[rows, GDN_V0 + h * GDN_HEAD:GDN_V0 + (h + 1) * GDN_HEAD] for h in hs],
                    [kks[h // 2] for h in hs], [qks[h // 2] for h in hs],
                    [_col(gc, GDN_GL + h) for h in hs], [gct_ref[GDN_GL + h:GDN_GL + h + 1, :] for h in hs],
                    [_col(glast_row, GDN_GL + h) for h in hs], [_col(beta_c, h) for h in hs], ss, causal, strict, eye_c)
                for i_h, h in enumerate(hs):
                    hc = slice(h * GDN_HEAD, (h + 1) * GDN_HEAD)
                    s_ref[h] = f["s_new"][i_h]
                    tm_ref[c, h] = f["t"][i_h].astype(tm_ref.dtype)
                    o = f["out"][i_h]
                    r = lax.rsqrt(jnp.mean(o * o, axis=-1, keepdims=True) + RMS_EPS)
                    z = p_ref[rows, GDN_Z0 + h * GDN_HEAD:GDN_Z0 + (h + 1) * GDN_HEAD]
                    mix_ref[rows, hc] = (o * r * nw * _silu(z)).astype(mix_ref.dtype)
            return carry

        lax.fori_loop(0, cpb, chunk, 0)

    vec = lambda n: pl.BlockSpec((1, n), lambda i: (0, 0))
    act = pl.BlockSpec((GDN_ROWS, GDN_CONV_DIM), lambda i: (i, 0))
    mix, states, tmats, qkv, pre = pl.pallas_call(
        body, name=name, grid=(nb,),
        in_specs=[pl.BlockSpec((GDN_ROWS, GDN_IN_PAD), lambda i: (i, 0)),
                  pl.BlockSpec((SUBLANES, GDN_IN_PAD), lambda i: (jnp.maximum(i * hb - 1, 0), 0)),
                  pl.BlockSpec((SUBLANES, GDN_CONV_DIM), lambda i: (0, 0)),
                  vec(LANES), vec(LANES), vec(GDN_HEAD)],
        out_specs=[pl.BlockSpec((GDN_ROWS, GDN_V), lambda i: (i, 0)),
                   pl.BlockSpec((cpb, GDN_VH, GDN_HEAD, GDN_HEAD), lambda i: (i, 0, 0, 0)),
                   pl.BlockSpec((cpb, GDN_VH, CHUNK, CHUNK), lambda i: (i, 0, 0, 0)), act, act],
        out_shape=[jax.ShapeDtypeStruct((rows_total, GDN_V), MXU_DTYPE),
                   jax.ShapeDtypeStruct((rows_total // CHUNK, GDN_VH, GDN_HEAD, GDN_HEAD), F32),
                   jax.ShapeDtypeStruct((rows_total // CHUNK, GDN_VH, CHUNK, CHUNK), MXU_DTYPE),
                   jax.ShapeDtypeStruct((rows_total, GDN_CONV_DIM), F32),
                   jax.ShapeDtypeStruct((rows_total, GDN_CONV_DIM), F32)],
        scratch_shapes=[pltpu.VMEM((GDN_ROWS + SUBLANES, GDN_CONV_DIM), F32),
                        pltpu.VMEM((GDN_ROWS, LANES), F32),
                        pltpu.VMEM((GDN_ROWS, LANES), F32),
                        pltpu.VMEM((GDN_VH, GDN_HEAD, GDN_HEAD), F32),
                        pltpu.VMEM((LANES, CHUNK), F32)],
        compiler_params=_cparams(("arbitrary",)),
    )(proj, proj, _pad_rows(conv_w), _gdn_lane_params(a_log), _gdn_lane_params(dt_bias), norm_w.reshape(1, -1))
    return mix, (states, tmats, qkv, pre)


def gdn_bwd(proj, conv_w, a_log, dt_bias, norm_w, saved, dmix, *, name):
    states, tmats, qkv, pre = saved
    rows_total = proj.shape[0]
    nb = rows_total // GDN_ROWS
    cpb = GDN_ROWS // CHUNK

    def body(p_ref, cw_ref, alog_ref, dtb_ref, nw_ref, st_ref, tm_ref, dm_ref, qkv_ref, pre_ref,
             dp_ref, dcw_ref, dalog_ref, ddtb_ref, dnw_ref,
             beta_ref, g_ref, ds_ref, gct_ref, dext_ref, dgc_ref, dgct_ref, dbeta_ref):
        i = pl.program_id(0)

        @pl.when(i == 0)
        def _():
            ds_ref[...] = jnp.zeros_like(ds_ref)
            dext_ref[GDN_ROWS:, :] = jnp.zeros((SUBLANES, GDN_CONV_DIM), F32)
            for r in (dcw_ref, dalog_ref, ddtb_ref, dnw_ref):
                r[...] = jnp.zeros_like(r)

        _gdn_gates(p_ref, alog_ref, dtb_ref, beta_ref, g_ref)
        ltri, utri, eye_l, eye_c = _tri(CHUNK), _tri(CHUNK, lower=False), _eye(LANES), _eye(CHUNK)
        causal = _iota((CHUNK, CHUNK), 1) <= _iota((CHUNK, CHUNK), 0)
        strict = _iota((CHUNK, CHUNK), 1) < _iota((CHUNK, CHUNK), 0)
        lane = _iota((CHUNK, LANES), 1)
        is_last = _iota((CHUNK, 1), 0) == CHUNK - 1
        nw = nw_ref[...]

        def chunk(cc, carry):
            c = cpb - 1 - cc
            rows = pl.ds(pl.multiple_of(c * CHUNK, CHUNK), CHUNK)
            g_c = g_ref[rows, :]
            gc = _sel(ltri,g_c)
            gct_ref[...] = _sel_nt(eye_l,gc)
            glast_row = _row(gc, CHUNK - 1)
            beta_c = beta_ref[rows, :]
            dgc_ref[...] = jnp.zeros_like(dgc_ref)
            dgct_ref[...] = jnp.zeros_like(dgct_ref)
            dbeta_ref[...] = jnp.zeros_like(dbeta_ref)
            for h0 in range(0, GDN_VH, GDN_GROUP):
                hs = list(range(h0, h0 + GDN_GROUP))
                hqs = list(range(h0 // 2, (h0 + GDN_GROUP) // 2))
                qs = {hq: qkv_ref[rows, hq * GDN_HEAD:(hq + 1) * GDN_HEAD] for hq in hqs}
                ks = {hq: qkv_ref[rows, GDN_K0 + hq * GDN_HEAD:GDN_K0 + (hq + 1) * GDN_HEAD] for hq in hqs}
                qs_c = {hq: _mx(qs[hq]) for hq in hqs}
                ks_c = {hq: _mx(ks[hq]) for hq in hqs}
                kks = {hq: _dot_nt(ks_c[hq], ks_c[hq]) for hq in hqs}
                qks = {hq: _dot_nt(qs_c[hq], ks_c[hq]) for hq in hqs}
                q = [qs[h // 2] for h in hs]
                k = [ks[h // 2] for h in hs]
                v = [qkv_ref[rows, GDN_V0 + h * GDN_HEAD:GDN_V0 + (h + 1) * GDN_HEAD] for h in hs]
                s = [st_ref[c, h] for h in hs]
                bcol = [_col(beta_c, h) for h in hs]
                f = _gdn_heads_fwd(q, k, v, [kks[h // 2] for h in hs], [qks[h // 2] for h in hs],
                                   [_col(gc, GDN_GL + h) for h in hs], [gct_ref[GDN_GL + h:GDN_GL + h + 1, :] for h in hs],
                                   [_col(glast_row, GDN_GL + h) for h in hs], bcol, s, causal, strict, eye_c,
                                   t=[tm_ref[c, h] for h in hs])
                do = []
                for i_h, h in enumerate(hs):
                    zc = slice(GDN_Z0 + h * GDN_HEAD, GDN_Z0 + (h + 1) * GDN_HEAD)
                    o = f["out"][i_h]
                    z = p_ref[rows, zc]
                    sz = _silu(z)
                    r = lax.rsqrt(jnp.mean(o * o, axis=-1, keepdims=True) + RMS_EPS)
                    on = o * r
                    dm = dm_ref[rows, h * GDN_HEAD:(h + 1) * GDN_HEAD]
                    dnw_ref[...] += jnp.sum(dm * on * sz, axis=0, keepdims=True)
                    d_on = dm * nw * sz
                    dp_ref[rows, zc] = (dm * on * nw * _dsilu(z)).astype(dp_ref.dtype)
                    do.append(r * (d_on - on * jnp.mean(d_on * on, axis=-1, keepdims=True)))
                ds_n = [ds_ref[h] for h in hs]
                do_c, dsn_c = _each(_mx, do), _each(_mx, ds_n)
                k_c = [ks_c[h // 2] for h in hs]
                dv1 = _each(_dot_tn, f["attn_c"], do_c)
                dv2 = _each(_dot, f["kt_c"], dsn_c)
                d_vnew = _each(lambda a_, b_: a_ + b_, dv1, dv2)
                dvn_c = _each(_mx, d_vnew)
                d_attn = _each(lambda do_, vn_: jnp.where(causal, _dot_nt(do_, vn_), 0.0), do_c, f["vn_c"])
                d_qd = _each(_dot_nt, do_c, f["s_c"])
                t1 = _each(_dot_tn, f["qd_c"], do_c)
                t2 = _each(_dot_tn, f["w_c"], dvn_c)
                for h, a_, cd_, dsn_, b_ in zip(hs, t1, f["cd"], ds_n, t2):
                    ds_ref[h] = a_ + cd_ * dsn_ - b_
                d_cd = _each(lambda s_, dsn_: jnp.sum(jnp.sum(s_ * dsn_, axis=1, keepdims=True), axis=0, keepdims=True), s, ds_n)
                d_kt = _each(_dot_nt, f["vn_c"], dsn_c)
                d_w = _each(lambda dv_, s_: -_dot_nt(dv_, s_), dvn_c, f["s_c"])
                d_rhs_u = _each(_dot_tn, f["t_c"], dvn_c)
                d_rhs_w = _each(_dot_tn, f["t_c"], d_w)
                m1 = _each(_dot_nt, d_rhs_u, f["u"])
                m2 = _each(_dot_nt, d_rhs_w, f["w_c"])
                da = _each(lambda a_, b_: -jnp.where(strict, a_ + b_, 0.0), m1, m2)
                dmm = _each(lambda a_, b_: a_ * b_, da, f["decay"])
                em = _each(lambda da_, a_, dat_, at_: da_ * a_ + dat_ * at_, da, f["a"], d_attn, f["attn"])
                dmm_c = _each(_mx, dmm)
                x1 = _each(_dot, dmm_c, k_c)
                d_kb = _each(lambda x_, drw_, e_: x_ + drw_ * e_, x1, d_rhs_w, f["egc"])
                dk1 = _each(_dot_tn, dmm_c, f["kb"])
                dpm = _each(lambda a_, b_: _mx(a_ * b_), d_attn, f["decay"])
                dq1 = _each(_dot, dpm, k_c)
                dq = _each(lambda x_, dqd_, e_: x_ + dqd_ * e_, dq1, d_qd, f["egc"])
                dk2 = _each(_dot_tn, dpm, [qs_c[h // 2] for h in hs])
                dk = _each(lambda a_, b_, dkb_, bc_, dkt_, et_: a_ + b_ + dkb_ * bc_ + dkt_ * et_,
                           dk1, dk2, d_kb, bcol, d_kt, f["etail"])
                for i_h, h in enumerate(hs):
                    tmp = jnp.sum(d_kt[i_h] * f["kt"][i_h], axis=1, keepdims=True)
                    d_gcol = (jnp.sum(em[i_h], axis=1, keepdims=True)
                              + jnp.sum(d_rhs_w[i_h] * f["rhs_w"][i_h], axis=1, keepdims=True)
                              + jnp.sum(d_qd[i_h] * f["qd"][i_h], axis=1, keepdims=True) - tmp)
                    d_glast = jnp.sum(tmp, axis=0, keepdims=True) + d_cd[i_h] * f["cd"][i_h]
                    d_gcol = jnp.where(is_last, d_gcol + d_glast, d_gcol)
                    d_beta = (jnp.sum(d_rhs_u[i_h] * v[i_h], axis=1, keepdims=True)
                              + jnp.sum(d_kb[i_h] * k[i_h], axis=1, keepdims=True))
                    dgc_ref[...] += jnp.where(lane == GDN_GL + h, d_gcol, 0.0)
                    dgct_ref[GDN_GL + h:GDN_GL + h + 1, :] = jnp.sum(em[i_h], axis=0, keepdims=True)
                    dbeta_ref[...] += jnp.where(lane == h, d_beta, 0.0)
                    dext_ref[rows, GDN_V0 + h * GDN_HEAD:GDN_V0 + (h + 1) * GDN_HEAD] = d_rhs_u[i_h] * bcol[i_h]
                for hq in hqs:
                    i0 = 2 * hq - h0
                    dext_ref[rows, hq * GDN_HEAD:(hq + 1) * GDN_HEAD] = dq[i0] + dq[i0 + 1]
                    dext_ref[rows, GDN_K0 + hq * GDN_HEAD:GDN_K0 + (hq + 1) * GDN_HEAD] = dk[i0] + dk[i0 + 1]
            d_gc = dgc_ref[...] - _sel_nt(eye_c,dgct_ref[...])
            dg = _sel(utri,d_gc)
            ba = p_ref[rows, GDN_BA0:GDN_BA0 + LANES]
            d_sp = dg * -jnp.exp(alog_ref[...])
            d_araw = d_sp * _sigmoid(ba + dtb_ref[...])
            d_araw = jnp.where((lane >= GDN_GL) & (lane < GDN_GL + GDN_VH), d_araw, 0.0)
            dalog_ref[...] += jnp.sum(dg * g_c, axis=0, keepdims=True)
            ddtb_ref[...] += jnp.sum(d_araw, axis=0, keepdims=True)
            d_braw = jnp.where(lane < GDN_VH, dbeta_ref[...] * beta_c * (1.0 - beta_c), 0.0)
            dp_ref[rows, GDN_BA0:GDN_BA0 + LANES] = (d_braw + d_araw).astype(dp_ref.dtype)
            return carry

        lax.fori_loop(0, cpb, chunk, 0)
        w = cw_ref[...]
        for hh in range(GDN_CONV_DIM // GDN_HEAD):
            cols = slice(hh * GDN_HEAD, (hh + 1) * GDN_HEAD)
            pre = pre_ref[:, cols]
            d_act = dext_ref[0:GDN_ROWS, cols]
            if hh < 2 * GDN_QKH:
                a = _silu(pre)
                r = lax.rsqrt(jnp.sum(a * a, axis=-1, keepdims=True) + L2_EPS)
                ah = a * r
                if hh < GDN_QKH:
                    d_act = d_act * GDN_SCALE
                d_act = r * (d_act - ah * jnp.sum(d_act * ah, axis=-1, keepdims=True))
            d_pre = d_act * _dsilu(pre)
            dext_ref[0:GDN_ROWS, cols] = d_pre
            du, dws = _conv_bwd_from_ext(dext_ref, p_ref[:, cols], w, GDN_CONV, GDN_ROWS, cols)
            for j in range(GDN_CONV):
                dcw_ref[j:j + 1, cols] += dws[j]
            dp_ref[:, cols] = du.astype(dp_ref.dtype)
            dext_ref[GDN_ROWS:, cols] = d_pre[0:SUBLANES, :]

    vec = lambda n: pl.BlockSpec((1, n), lambda i: (0, 0))
    outs = pl.pallas_call(
        body, name=name, grid=(nb,),
        in_specs=[pl.BlockSpec((GDN_ROWS, GDN_IN_PAD), lambda i: (nb - 1 - i, 0)),
                  pl.BlockSpec((SUBLANES, GDN_CONV_DIM), lambda i: (0, 0)),
                  vec(LANES), vec(LANES), vec(GDN_HEAD),
                  pl.BlockSpec((cpb, GDN_VH, GDN_HEAD, GDN_HEAD), lambda i: (nb - 1 - i, 0, 0, 0)),
                  pl.BlockSpec((cpb, GDN_VH, CHUNK, CHUNK), lambda i: (nb - 1 - i, 0, 0, 0)),
                  pl.BlockSpec((GDN_ROWS, GDN_V), lambda i: (nb - 1 - i, 0)),
                  pl.BlockSpec((GDN_ROWS, GDN_CONV_DIM), lambda i: (nb - 1 - i, 0)),
                  pl.BlockSpec((GDN_ROWS, GDN_CONV_DIM), lambda i: (nb - 1 - i, 0))],
        out_specs=[pl.BlockSpec((GDN_ROWS, GDN_IN_PAD), lambda i: (nb - 1 - i, 0)),
                   pl.BlockSpec((SUBLANES, GDN_CONV_DIM), lambda i: (0, 0)),
                   vec(LANES), vec(LANES), vec(GDN_HEAD)],
        out_shape=[jax.ShapeDtypeStruct((rows_total, GDN_IN_PAD), MXU_DTYPE),
                   jax.ShapeDtypeStruct((SUBLANES, GDN_CONV_DIM), F32),
                   jax.ShapeDtypeStruct((1, LANES), F32), jax.ShapeDtypeStruct((1, LANES), F32),
                   jax.ShapeDtypeStruct((1, GDN_HEAD), F32)],
        scratch_shapes=[pltpu.VMEM((GDN_ROWS, LANES), F32),
                        pltpu.VMEM((GDN_ROWS, LANES), F32),
                        pltpu.VMEM((GDN_VH, GDN_HEAD, GDN_HEAD), F32),
                        pltpu.VMEM((LANES, CHUNK), F32),
                        pltpu.VMEM((GDN_ROWS + SUBLANES, GDN_CONV_DIM), F32),
                        pltpu.VMEM((CHUNK, LANES), F32),
                        pltpu.VMEM((LANES, CHUNK), F32),
                        pltpu.VMEM((CHUNK, LANES), F32)],
        compiler_params=_cparams(("arbitrary",)),
    )(proj, _pad_rows(conv_w), _gdn_lane_params(a_log), _gdn_lane_params(dt_bias), norm_w.reshape(1, -1),
      states, tmats, dmix, qkv, pre)
    dproj, dcw, dalog, ddtb, dnw = outs
    return dproj, [dcw[:GDN_CONV], dalog[0, GDN_GL:GDN_GL + GDN_VH], ddtb[0, GDN_GL:GDN_GL + GDN_VH], dnw[0]]


def chip_exchange(src, *, scatter, name):
    piece_shape = src.shape[1:]

    def body(src_ref, out_ref, send_sems, recv_sems, local_sem):
        x, y, c = (lax.axis_index(a) for a in MESH_AXES)
        me = 2 * x + y

        def piece(j):
            return src_ref.at[j] if scatter else src_ref.at[c]

        local = pltpu.make_async_copy(piece(me), out_ref.at[me], local_sem)
        local.start()
        copies = []
        for k in range(1, N_SHARDS):
            px = 1 - x if k & 2 else x
            py = 1 - y if k & 1 else y
            cp = pltpu.make_async_remote_copy(
                src_ref=piece(2 * px + py), dst_ref=out_ref.at[me], send_sem=send_sems.at[k - 1],
                recv_sem=recv_sems.at[k - 1], device_id=(px, py, c), device_id_type=pl.DeviceIdType.MESH)
            cp.start()
            copies.append(cp)
        for cp in copies:
            cp.wait()
        local.wait()

    hbm = pl.BlockSpec(memory_space=pl.ANY)
    return pl.pallas_call(
        body, name=name, in_specs=[hbm], out_specs=hbm,
        out_shape=jax.ShapeDtypeStruct((N_SHARDS,) + tuple(piece_shape), src.dtype),
        scratch_shapes=[pltpu.SemaphoreType.DMA((N_SHARDS - 1,)), pltpu.SemaphoreType.DMA((N_SHARDS - 1,)),
                        pltpu.SemaphoreType.DMA],
    )(src)


def pair_exchange(src, *, add, name, out_dtype=None):
    lead, rows, cols = src.shape
    tr = _pick(rows, (512, 256))
    nblk = rows // tr
    n_steps = nblk if add else lead * nblk

    def body(c_ref, *refs):
        if add:
            mine_ref, send_ref, o_ref, recv_ref, send_sems, recv_sems, credit = refs
        else:
            send_ref, o_ref, recv_ref, send_sems, recv_sems, credit = refs
        step = pl.program_id(0) * nblk + pl.program_id(1)
        slot = step % 2
        sibling = (lax.axis_index("x"), lax.axis_index("y"), 1 - lax.axis_index("c"))

        @pl.when(step >= 2)
        def _():
            pl.semaphore_wait(credit, 1)

        cp = pltpu.make_async_remote_copy(
            src_ref=send_ref, dst_ref=recv_ref.at[slot], send_sem=send_sems.at[slot], recv_sem=recv_sems.at[slot],
            device_id=sibling, device_id_type=pl.DeviceIdType.MESH)
        cp.start()
        cp.wait_recv()
        if add:
            o_ref[...] = (mine_ref[...] + recv_ref[slot]).astype(o_ref.dtype)
        else:
            o_ref[c_ref[0]] = send_ref[...]
            o_ref[1 - c_ref[0]] = recv_ref[slot]
        cp.wait_send()

        @pl.when(step + 2 < n_steps)
        def _():
            pl.semaphore_signal(credit, 1, device_id=sibling, device_id_type=pl.DeviceIdType.MESH)

    flat = src.reshape(lead * rows, cols)
    if add:
        in_specs = [pl.BlockSpec((tr, cols), lambda s, i, c_ref: (c_ref[0] * nblk + i, 0)),
                    pl.BlockSpec((tr, cols), lambda s, i, c_ref: ((1 - c_ref[0]) * nblk + i, 0))]
        out_specs = pl.BlockSpec((tr, cols), lambda s, i, c_ref: (i, 0))
        out_shape = jax.ShapeDtypeStruct((rows, cols), src.dtype if out_dtype is None else out_dtype)
        grid, args = (1, nblk), (flat, flat)
    else:
        in_specs = [pl.BlockSpec((tr, cols), lambda s, i, c_ref: (s * nblk + i, 0))]
        out_specs = pl.BlockSpec((2, tr, cols), lambda s, i, c_ref: (s, i, 0))
        out_shape = jax.ShapeDtypeStruct((lead * 2, rows, cols), src.dtype)
        grid, args = (lead, nblk), (flat,)
    out = pl.pallas_call(
        body, name=name, out_shape=out_shape,
        grid_spec=pltpu.PrefetchScalarGridSpec(
            num_scalar_prefetch=1, grid=grid, in_specs=in_specs, out_specs=out_specs,
            scratch_shapes=[pltpu.VMEM((2, tr, cols), src.dtype), pltpu.SemaphoreType.DMA((2,)),
                            pltpu.SemaphoreType.DMA((2,)), pltpu.SemaphoreType.REGULAR]),
        compiler_params=_cparams(("arbitrary", "arbitrary")),
    )(lax.axis_index("c").astype(jnp.int32).reshape(1), *args)
    return out if add else out.reshape(lead, 2, rows, cols)


def sum_slots(buf, *, name):
    n, rows, cols = buf.shape
    tr = _pick(rows, (512, 256, 128))

    def body(b_ref, o_ref):
        acc = b_ref[0].astype(F32)
        for j in range(1, n):
            acc = acc + b_ref[j].astype(F32)
        o_ref[...] = acc

    return pl.pallas_call(
        body, name=name, grid=(rows // tr,), in_specs=[pl.BlockSpec((n, tr, cols), lambda i: (0, i, 0))],
        out_specs=pl.BlockSpec((tr, cols), lambda i: (i, 0)), out_shape=jax.ShapeDtypeStruct((rows, cols), F32),
        compiler_params=_cparams(("parallel",)),
    )(buf)


def adamw(w, g, m, v, *, name):
    shape = w.shape
    cols = shape[-1]
    rows = _size(shape) // cols
    w, g, m, v = (t.reshape(rows, cols) for t in (w, g, m, v))
    tr = 256 if rows % 256 == 0 else rows

    def body(w_ref, g_ref, m_ref, v_ref, d_ref, mo_ref, vo_ref):
        gv = g_ref[...]
        mn = ADAM_B1 * m_ref[...] + (1.0 - ADAM_B1) * gv
        vn = ADAM_B2 * v_ref[...] + (1.0 - ADAM_B2) * (gv * gv)
        m_hat = mn / (1.0 - ADAM_B1 ** ADAM_STEP)
        v_hat = vn / (1.0 - ADAM_B2 ** ADAM_STEP)
        d_ref[...] = -ADAM_LR * (m_hat / (jnp.sqrt(v_hat) + ADAM_EPS) + ADAM_WD * w_ref[...])
        mo_ref[...] = mn
        vo_ref[...] = vn

    blk = pl.BlockSpec((tr, cols), lambda i: (i, 0))
    shp = jax.ShapeDtypeStruct((rows, cols), F32)
    outs = pl.pallas_call(
        body, name=name, grid=(rows // tr,), in_specs=[blk] * 4, out_specs=[blk] * 3, out_shape=[shp] * 3,
        compiler_params=_cparams(("parallel",)),
    )(w, g, m, v)
    return [o.reshape(shape) for o in outs]


N_SHARDS = 4
FLAT_COLS = 1024
W_SPECS = (
    ("gdn_w_in", (2, 1024, 6176), 2), ("gdn_conv_w", (2, 4, 4096), 2), ("gdn_a_log", (2, 16), None),
    ("gdn_dt_bias", (2, 16), None), ("gdn_norm_w", (2, 128), None), ("gdn_w_out", (2, 2048, 1024), 1),
    ("sc_w_in", (1, 1024, 8192), 2), ("sc_conv_w", (1, 3, 2048), 2), ("sc_w_out", (1, 2048, 1024), 1),
    ("ssd_w_in", (1, 1024, 5152), 2), ("ssd_conv_w", (1, 4, 3072), 2), ("ssd_conv_b", (1, 3072), 1),
    ("ssd_a_log", (1, 32), None), ("ssd_dt_bias", (1, 32), None), ("ssd_d_skip", (1, 32), None),
    ("ssd_norm_w", (1, 2048), 1), ("ssd_w_out", (1, 2048, 1024), 1), ("ln_g", (4, 1024), None), ("ln_b", (4, 1024), None),
)


def _local_shape(shape, axis):
    return shape if axis is None else tuple(d // N_SHARDS if i == axis else d for i, d in enumerate(shape))


def _size(shape):
    n = 1
    for d in shape:
        n *= d
    return n


PIECE_ROWS = 16


def _piece_rows(shape, axis):
    return -(-_size(_local_shape(shape, axis)) // (FLAT_COLS * PIECE_ROWS)) * PIECE_ROWS


def _flat_rows(specs):
    return -(-sum(_piece_rows(s, a) for _, s, a in specs) // 512) * 512


def _pack(pieces, specs, dtype=F32):
    blocks, used = [], 0
    for p, (_, shape, axis) in zip(pieces, specs):
        rows = _piece_rows(shape, axis)
        flat = p.reshape(-1).astype(dtype)
        if flat.shape[0] < rows * FLAT_COLS:
            flat = jnp.pad(flat, (0, rows * FLAT_COLS - flat.shape[0]))
        blocks.append(flat.reshape(rows, FLAT_COLS))
        used += rows
    blocks.append(jnp.zeros((_flat_rows(specs) - used, FLAT_COLS), dtype))
    return jnp.concatenate(blocks, axis=0)


def _unpack(flat, specs):
    out, off = [], 0
    for _, shape, axis in specs:
        ls = _local_shape(shape, axis)
        rows = _piece_rows(shape, axis)
        out.append(flat[off:off + rows].reshape(-1)[:_size(ls)].reshape(ls))
        off += rows
    return out


def _shard_of(full, axis, s):
    if axis is None:
        return full
    n = full.shape[axis] // N_SHARDS
    return lax.slice_in_dim(full, s * n, (s + 1) * n, axis=axis)


def _adamw_all(weights, grads, moms, vels):
    steps = [adamw(w, g, m, v, name="adamw") for w, g, m, v in zip(weights, grads, moms, vels)]
    return grads, [s[0] for s in steps], [s[1] for s in steps], [s[2] for s in steps]


SPLIT_ROWS = 128


def shard_split(w, n_real, *, name):
    rows, n_pad = w.shape
    ns = n_real // N_SHARDS

    def body(w_ref, o_ref):
        for s in range(N_SHARDS):
            o_ref[s] = w_ref[:, s * ns:(s + 1) * ns]

    return pl.pallas_call(
        body, name=name, grid=(rows // SPLIT_ROWS,),
        in_specs=[pl.BlockSpec((SPLIT_ROWS, n_pad), lambda i: (i, 0))],
        out_specs=pl.BlockSpec((N_SHARDS, SPLIT_ROWS, ns), lambda i: (0, i, 0)),
        out_shape=jax.ShapeDtypeStruct((N_SHARDS, rows, ns), F32), compiler_params=_cparams(("parallel",)),
    )(w)


def shard_merge(pieces, n_pad, *, name):
    _, rows, ns = pieces.shape
    n_real = ns * N_SHARDS

    def body(p_ref, o_ref):
        for s in range(N_SHARDS):
            o_ref[:, s * ns:(s + 1) * ns] = p_ref[s].astype(o_ref.dtype)
        if n_pad > n_real:
            o_ref[:, n_real:] = jnp.zeros((SPLIT_ROWS, n_pad - n_real), o_ref.dtype)

    return pl.pallas_call(
        body, name=name, grid=(rows // SPLIT_ROWS,),
        in_specs=[pl.BlockSpec((N_SHARDS, SPLIT_ROWS, ns), lambda i: (0, i, 0))],
        out_specs=pl.BlockSpec((SPLIT_ROWS, n_pad), lambda i: (i, 0)),
        out_shape=jax.ShapeDtypeStruct((rows, n_pad), MXU_DTYPE), compiler_params=_cparams(("parallel",)),
    )(pieces)


def _reduce_scatter(full_grads):
    def shard(g, spec, s):
        _, shape, axis = spec
        return g[:, s] if g.ndim == len(shape) + 1 else _shard_of(g, axis, s)

    def reduce(idx, wire_dtype, tag):
        specs = [W_SPECS[i] for i in idx]
        half = _flat_rows(specs) // 2
        by_shard = jnp.stack([_pack([shard(full_grads[i], W_SPECS[i], s) for i in idx], specs)
                              for s in range(N_SHARDS)])
        by_half = by_shard.reshape(N_SHARDS, 2, half, FLAT_COLS).transpose(1, 0, 2, 3)
        by_half = by_half.reshape(2, N_SHARDS * half, FLAT_COLS)
        pair_sum = pair_exchange(by_half, add=True, out_dtype=wire_dtype, name="rs_pair_" + tag)
        chips = chip_exchange(pair_sum.reshape(N_SHARDS, half, FLAT_COLS), scatter=True, name="rs_chips_" + tag)
        summed = sum_slots(chips, name="rs_chip_sum_" + tag)
        both = pair_exchange(summed[None], add=False, name="rs_halves_" + tag).reshape(2 * half, FLAT_COLS)
        return dict(zip(idx, _unpack(both, specs)))

    matrices = [i for i, (n, _, _) in enumerate(W_SPECS) if n in MXU_WEIGHTS]
    others = [i for i, (n, _, _) in enumerate(W_SPECS) if n not in MXU_WEIGHTS]
    grads = {**reduce(matrices, MXU_DTYPE, "mxu"), **reduce(others, F32, "f32")}
    return [grads[i] for i in range(len(W_SPECS))]


def _gather_weights(local_weights):
    def gather(idx, dtype, tag):
        specs = [W_SPECS[i] for i in idx]
        rows = _flat_rows(specs)
        flat = _pack([local_weights[i] for i in idx], specs, dtype)
        halves = chip_exchange(flat.reshape(2, rows // 2, FLAT_COLS), scatter=False, name="gather_chips_" + tag)
        both = pair_exchange(halves, add=False, name="gather_pair_" + tag).reshape(N_SHARDS, rows, FLAT_COLS)
        return [dict(zip(idx, _unpack(both[s], specs))) for s in range(N_SHARDS)]

    matrices = [i for i, (n, _, _) in enumerate(W_SPECS) if n in MXU_WEIGHTS]
    vectors = [i for i, (n, _, a) in enumerate(W_SPECS) if n not in MXU_WEIGHTS and a is not None]
    per_shard = [{**m, **v} for m, v in zip(gather(matrices, MXU_DTYPE, "mxu"), gather(vectors, F32, "f32"))]
    full = []
    for i, (wname, shape, axis) in enumerate(W_SPECS):
        if axis is None:
            full.append(local_weights[i])
        elif wname in W_IN_PAD:
            pieces = jnp.stack([per_shard[s][i] for s in range(N_SHARDS)], axis=1)
            full.append([shard_merge(pieces[j], W_IN_PAD[wname], name="merge_" + wname) for j in range(shape[0])])
        else:
            full.append(jnp.concatenate([per_shard[s][i] for s in range(N_SHARDS)], axis=axis))
    return full


W_IN_PAD = {"gdn_w_in": GDN_IN_PAD, "sc_w_in": SC_IN, "ssd_w_in": SSD_IN_PAD}
MXU_WEIGHTS = ("gdn_w_in", "gdn_w_out", "sc_w_in", "sc_w_out", "ssd_w_in", "ssd_w_out")


def kernel(x, gdn_w_in, gdn_conv_w, gdn_a_log, gdn_dt_bias, gdn_norm_w, gdn_w_out, sc_w_in, sc_conv_w, sc_w_out, ssd_w_in, ssd_conv_w, ssd_conv_b, ssd_a_log, ssd_dt_bias, ssd_d_skip, ssd_norm_w, ssd_w_out, ln_g, ln_b, loss_target, m_gdn_w_in, m_gdn_conv_w, m_gdn_a_log, m_gdn_dt_bias, m_gdn_norm_w, m_gdn_w_out, m_sc_w_in, m_sc_conv_w, m_sc_w_out, m_ssd_w_in, m_ssd_conv_w, m_ssd_conv_b, m_ssd_a_log, m_ssd_dt_bias, m_ssd_d_skip, m_ssd_norm_w, m_ssd_w_out, m_ln_g, m_ln_b, v_gdn_w_in, v_gdn_conv_w, v_gdn_a_log, v_gdn_dt_bias, v_gdn_norm_w, v_gdn_w_out, v_sc_w_in, v_sc_conv_w, v_sc_w_out, v_ssd_w_in, v_ssd_conv_w, v_ssd_conv_b, v_ssd_a_log, v_ssd_dt_bias, v_ssd_d_skip, v_ssd_norm_w, v_ssd_w_out, v_ln_g, v_ln_b):
    weights = [gdn_w_in, gdn_conv_w, gdn_a_log, gdn_dt_bias, gdn_norm_w, gdn_w_out, sc_w_in, sc_conv_w, sc_w_out,
               ssd_w_in, ssd_conv_w, ssd_conv_b, ssd_a_log, ssd_dt_bias, ssd_d_skip, ssd_norm_w, ssd_w_out, ln_g, ln_b]
    moms = [m_gdn_w_in, m_gdn_conv_w, m_gdn_a_log, m_gdn_dt_bias, m_gdn_norm_w, m_gdn_w_out, m_sc_w_in, m_sc_conv_w,
            m_sc_w_out, m_ssd_w_in, m_ssd_conv_w, m_ssd_conv_b, m_ssd_a_log, m_ssd_dt_bias, m_ssd_d_skip, m_ssd_norm_w,
            m_ssd_w_out, m_ln_g, m_ln_b]
    vels = [v_gdn_w_in, v_gdn_conv_w, v_gdn_a_log, v_gdn_dt_bias, v_gdn_norm_w, v_gdn_w_out, v_sc_w_in, v_sc_conv_w,
            v_sc_w_out, v_ssd_w_in, v_ssd_conv_w, v_ssd_conv_b, v_ssd_a_log, v_ssd_dt_bias, v_ssd_d_skip, v_ssd_norm_w,
            v_ssd_w_out, v_ln_g, v_ln_b]
    full = dict(zip([n for n, _, _ in W_SPECS], _gather_weights(weights)))
    x0 = x[0]
    target = loss_target[0]

    layers = (("gdn", 0, GDN_IN_PAD, GDN_IN), ("sc", 0, SC_IN, SC_IN), ("ssd", 0, SSD_IN_PAD, SSD_IN), ("gdn", 1, GDN_IN_PAD, GDN_IN))

    def params(kind, j):
        if kind == "gdn":
            return [full["gdn_conv_w"][j], full["gdn_a_log"][j], full["gdn_dt_bias"][j], full["gdn_norm_w"][j]]
        if kind == "sc":
            return [full["sc_conv_w"][j]]
        return [full["ssd_conv_w"][j], full["ssd_conv_b"][j], full["ssd_a_log"][j], full["ssd_dt_bias"][j],
                full["ssd_d_skip"][j], full["ssd_norm_w"][j]]

    xs, xs_c, saved = [x0], [x0.astype(MXU_DTYPE)], []
    for i, (kind, j, n_pad, _) in enumerate(layers):
        w_in = full[kind + "_w_in"][j]
        w_out = full[kind + "_w_out"][j].astype(MXU_DTYPE)
        proj = matmul(xs_c[i], w_in, name=kind + "_proj")
        if kind == "gdn":
            mix, states = gdn_fwd(proj, *params(kind, j), name="gdn_fwd")
        elif kind == "sc":
            mix, states = sc_fwd(proj, *params(kind, j), name="sc_fwd"), None
        else:
            mix, states = ssd_fwd(proj, *params(kind, j), name="ssd_fwd")
        y = matmul(mix, w_out, name=kind + "_out")
        saved.append((w_in, w_out, proj, mix, states, y))
        if i + 1 < DEPTH:
            xn, xn_c = ln_fwd(xs[i], y, full["ln_g"][i], full["ln_b"][i], name="ln_fwd")
            xs.append(xn)
            xs_c.append(xn_c)

    grads = {n: [None] * s[0] for n, s, _ in W_SPECS}
    dr, dg, db, loss_rows = ln_bwd(xs[DEPTH - 1], saved[DEPTH - 1][5], full["ln_g"][DEPTH - 1], b=full["ln_b"][DEPTH - 1],
                                   target=target, name="ln_bwd_loss")
    dx = None
    for i in reversed(range(DEPTH)):
        kind, j, _, n_in = layers[i]
        w_in, w_out, proj, mix, states, _ = saved[i]
        grads["ln_g"][i], grads["ln_b"][i] = dg[0], db[0]
        dmix = matmul(dr, w_out, tb=True, name=kind + "_dmix")
        grads[kind + "_w_out"][j] = matmul(mix, dr, ta=True, name=kind + "_dw_out")
        if kind == "gdn":
            dproj, (dcw, dalog, ddtb, dnw) = gdn_bwd(proj, *params(kind, j), states, dmix, name="gdn_bwd")
            grads["gdn_conv_w"][j], grads["gdn_a_log"][j], grads["gdn_dt_bias"][j], grads["gdn_norm_w"][j] = dcw, dalog, ddtb, dnw
        elif kind == "sc":
            dproj, dcw = sc_bwd(proj, *params(kind, j), dmix, name="sc_bwd")
            grads["sc_conv_w"][j] = dcw[:SC_CONV]
        else:
            conv_w, _, *rest = params(kind, j)
            dproj, (dcw, dcb, dalog, ddtb, ddsk, dnw) = ssd_bwd(proj, conv_w, *rest, states, dmix, name="ssd_bwd")
            grads["ssd_conv_w"][j], grads["ssd_conv_b"][j], grads["ssd_a_log"][j] = dcw, dcb, dalog
            grads["ssd_dt_bias"][j], grads["ssd_d_skip"][j], grads["ssd_norm_w"][j] = ddtb, ddsk, dnw
        grads[kind + "_w_in"][j] = shard_split(matmul(xs_c[i], dproj, ta=True, name=kind + "_dw_in"), n_in, name="split_" + kind)
        dx = matmul(dproj, w_in, tb=True, add=dr, add_scale=ALPHA, name=kind + "_dx")
        if i > 0:
            dr, dg, db = ln_bwd(xs[i - 1], saved[i - 1][5], full["ln_g"][i - 1], dx, name="ln_bwd")

    full_grads = [jnp.stack(grads[n]) for n, _, _ in W_SPECS]
    g_out, d_out, m_out, v_out = _adamw_all(weights, _reduce_scatter(full_grads), moms, vels)
    loss = lax.psum(loss_rows[0, 0], MESH_AXES)
    return (loss, dx[None], *g_out, *d_out, *m_out, *v_out)
```

```python
import functools

import jax
import jax.numpy as jnp
from jax import lax
from jax.experimental import pallas as pl
from jax.experimental.pallas import tpu as pltpu

F32 = jnp.float32
MXU_DTYPE = jnp.bfloat16

D_MODEL = 1024
DEPTH = 4
D_INNER = 2048
CHUNK = 64
LANES = 128
SUBLANES = 8
VMEM_LIMIT = 56 * 1024 * 1024

GDN_HEAD = 128
GDN_VH = 16
GDN_QKH = 8
GDN_QK = 1024
GDN_V = 2048
GDN_CONV = 4
GDN_CONV_DIM = 4096
GDN_IN = 6176
GDN_IN_PAD = 6272

SC_W = 2048
SC_CONV = 3
SC_IN = 8192

SSD_P = 64
SSD_H = 32
SSD_G = 4
SSD_S = 128
SSD_CONV = 4
SSD_CONV_DIM = 3072
SSD_IN = 5152
SSD_IN_PAD = 5376

ALPHA = (2 * DEPTH) ** 0.25
RMS_EPS = 1e-6
LN_EPS = 1e-5
L2_EPS = 1e-6

ADAM_LR = 0.001
ADAM_B1 = 0.9
ADAM_B2 = 0.999
ADAM_EPS = 1e-08
ADAM_WD = 0.01
ADAM_STEP = 10

MESH_AXES = ("x", "y", "c")


def _cparams(sem):
    return pltpu.CompilerParams(dimension_semantics=sem, vmem_limit_bytes=VMEM_LIMIT)


def _pick(n, prefs):
    for p in prefs:
        if n % p == 0:
            return p
    return n


def _dot(a, b, dims=(((1,), (0,)), ((), ()))):
    return lax.dot_general(a.astype(MXU_DTYPE), b.astype(MXU_DTYPE), dims, preferred_element_type=F32)


def _dot_nt(a, b):
    return _dot(a, b, (((1,), (1,)), ((), ())))


def _dot_tn(a, b):
    return _dot(a, b, (((0,), (0,)), ((), ())))


NN = (((1,), (0,)), ((), ()))
NT = (((1,), (1,)), ((), ()))
TN = (((0,), (0,)), ((), ()))


def _mxu(a, b, dims):
    return lax.dot_general(a, b, dims, preferred_element_type=F32)


def _split(x, pieces):
    out, r = [], x
    for i in range(pieces):
        p = r.astype(jnp.bfloat16)
        out.append(p)
        if i + 1 < pieces:
            r = r - p.astype(F32)
    return out


def _sel(m, x, dims=NN):
    mb = m.astype(jnp.bfloat16)
    x1, x2, x3 = _split(x, 3)
    return (_mxu(mb, x3, dims) + _mxu(mb, x2, dims)) + _mxu(mb, x1, dims)


def _sel_nt(m, x):
    return _sel(m, x, NT)


def _xsel(x, m, dims=NN):
    mb = m.astype(jnp.bfloat16)
    x1, x2, x3 = _split(x, 3)
    return (_mxu(x3, mb, dims) + _mxu(x2, mb, dims)) + _mxu(x1, mb, dims)


def _xsel_nt(x, m):
    return _xsel(x, m, NT)


def _iota(shape, dim):
    return lax.broadcasted_iota(jnp.int32, shape, dim)


def _sigmoid(x):
    return 0.5 * jnp.tanh(0.5 * x) + 0.5


def _silu(x):
    return x * _sigmoid(x)


def _dsilu(x):
    s = _sigmoid(x)
    return s * (1.0 + x * (1.0 - s))


def _softplus(x):
    return jnp.maximum(x, 0.0) + jnp.log(1.0 + jnp.exp(-jnp.abs(x)))


def matmul(a, b, *, ta=False, tb=False, add=None, add_scale=1.0, name):
    if ta:
        kdim, m = a.shape
    else:
        m, kdim = a.shape
    n = b.shape[0] if tb else b.shape[1]
    assert (b.shape[1] if tb else b.shape[0]) == kdim
    tm = _pick(m, (1024, 896, 768, 512)) if ta else _pick(m, (2048, 1024, 512, 256, 128))
    tn = _pick(n, (1024, 896, 768, 512, 256, 128))
    tk = _pick(kdim, (1024, 512, 256)) if ta else _pick(kdim, (1024, 896, 768, 512))
    nk = kdim // tk
    dims = (((0 if ta else 1,), (1 if tb else 0,)), ((), ()))

    def body(a_ref, b_ref, *rest):
        o_ref = rest[-1]
        k = pl.program_id(2)
        part = _dot(a_ref[...], b_ref[...], dims)

        @pl.when(k == 0)
        def _():
            o_ref[...] = part if add is None else part + add_scale * rest[0][...]

        @pl.when(k > 0)
        def _():
            o_ref[...] += part

    a_spec = pl.BlockSpec((tk, tm), lambda i, j, k: (k, i)) if ta else pl.BlockSpec((tm, tk), lambda i, j, k: (i, k))
    b_spec = pl.BlockSpec((tn, tk), lambda i, j, k: (j, k)) if tb else pl.BlockSpec((tk, tn), lambda i, j, k: (k, j))
    o_spec = pl.BlockSpec((tm, tn), lambda i, j, k: (i, j))
    in_specs = [a_spec, b_spec] + ([] if add is None else [o_spec])
    args = (a, b) + (() if add is None else (add,))
    return pl.pallas_call(
        body, name=name, grid=(m // tm, n // tn, nk), in_specs=in_specs, out_specs=o_spec,
        out_shape=jax.ShapeDtypeStruct((m, n), F32),
        compiler_params=_cparams(("parallel", "parallel", "arbitrary")),
    )(*args)


LN_ROWS = 512


def _ln_stats(x, y):
    r = ALPHA * x + y
    mu = jnp.mean(r, axis=-1, keepdims=True)
    rc = r - mu
    var = jnp.mean(rc * rc, axis=-1, keepdims=True)
    rstd = lax.rsqrt(var + LN_EPS)
    return rc * rstd, rstd


def ln_fwd(x, y, g, b, *, name):
    rows, d = x.shape

    def body(x_ref, y_ref, g_ref, b_ref, o_ref, oc_ref):
        xhat, _ = _ln_stats(x_ref[...], y_ref[...])
        xn = xhat * g_ref[...] + b_ref[...]
        o_ref[...] = xn
        oc_ref[...] = xn.astype(oc_ref.dtype)

    blk = pl.BlockSpec((LN_ROWS, d), lambda i: (i, 0))
    vec = pl.BlockSpec((1, d), lambda i: (0, 0))
    return pl.pallas_call(
        body, name=name, grid=(rows // LN_ROWS,), in_specs=[blk, blk, vec, vec], out_specs=[blk, blk],
        out_shape=[jax.ShapeDtypeStruct((rows, d), F32), jax.ShapeDtypeStruct((rows, d), MXU_DTYPE)],
        compiler_params=_cparams(("parallel",)),
    )(x, y, g.reshape(1, d), b.reshape(1, d))


def ln_bwd(x, y, g, dxn=None, *, b=None, target=None, name):
    rows, d = x.shape
    final = target is not None

    def body(x_ref, y_ref, g_ref, *rest):
        if final:
            b_ref, t_ref, dr_ref, dg_ref, db_ref, loss_ref = rest
        else:
            dxn_ref, dr_ref, dg_ref, db_ref = rest
        i = pl.program_id(0)
        xhat, rstd = _ln_stats(x_ref[...], y_ref[...])
        gv = g_ref[...]
        if final:
            err = xhat * gv + b_ref[...] - t_ref[...]
            dxn_v = err * (1.0 / d)
            part = 0.5 * jnp.sum(jnp.mean(err * err, axis=-1, keepdims=True), axis=0, keepdims=True)
        else:
            dxn_v = dxn_ref[...]
        dxh = dxn_v * gv
        m1 = jnp.mean(dxh, axis=-1, keepdims=True)
        m2 = jnp.mean(dxh * xhat, axis=-1, keepdims=True)
        dr_ref[...] = rstd * (dxh - m1 - xhat * m2)

        @pl.when(i == 0)
        def _():
            dg_ref[...] = jnp.zeros_like(dg_ref)
            db_ref[...] = jnp.zeros_like(db_ref)
            if final:
                loss_ref[...] = jnp.zeros_like(loss_ref)

        dg_ref[...] += jnp.sum(dxn_v * xhat, axis=0, keepdims=True)
        db_ref[...] += jnp.sum(dxn_v, axis=0, keepdims=True)
        if final:
            loss_ref[...] += jnp.broadcast_to(part, loss_ref.shape)

    blk = pl.BlockSpec((LN_ROWS, d), lambda i: (i, 0))
    vec = pl.BlockSpec((1, d), lambda i: (0, 0))
    lvec = pl.BlockSpec((1, LANES), lambda i: (0, 0))
    out_shape = [jax.ShapeDtypeStruct((rows, d), F32), jax.ShapeDtypeStruct((1, d), F32), jax.ShapeDtypeStruct((1, d), F32)]
    out_specs = [blk, vec, vec]
    if final:
        in_specs = [blk, blk, vec, vec, blk]
        args = (x, y, g.reshape(1, d), b.reshape(1, d), target)
        out_shape.append(jax.ShapeDtypeStruct((1, LANES), F32))
        out_specs.append(lvec)
    else:
        in_specs = [blk, blk, vec, blk]
        args = (x, y, g.reshape(1, d), dxn)
    return pl.pallas_call(
        body, name=name, grid=(rows // LN_ROWS,), in_specs=in_specs, out_specs=out_specs, out_shape=out_shape,
        compiler_params=_cparams(("arbitrary",)),
    )(*args)


def _rows_from(ref, off, rows, cols=slice(None)):
    r = off % SUBLANES
    if r == 0:
        return ref[off:off + rows, cols]
    window = ref[off - r:off - r + rows + SUBLANES, cols]
    return pltpu.roll(window, rows + SUBLANES - r, axis=0)[:rows]


def _conv_from_ext(ext_ref, w, width, rows, cols=slice(None)):
    out = None
    for j in range(width):
        term = _rows_from(ext_ref, SUBLANES - (width - 1) + j, rows, cols) * w[j:j + 1, cols]
        out = term if out is None else out + term
    return out


def _conv_dgrad_from_ext(dext_ref, w, width, rows, cols):
    out = None
    for j in range(width):
        term = _rows_from(dext_ref, (width - 1) - j, rows, cols) * w[j:j + 1, cols]
        out = term if out is None else out + term
    return out


def _conv_bwd_from_ext(dext_ref, u, w, width, rows, cols):
    du, dws = None, []
    for j in range(width):
        shifted = _rows_from(dext_ref, (width - 1) - j, rows, cols)
        term = shifted * w[j:j + 1, cols]
        du = term if du is None else du + term
        dws.append(jnp.sum(shifted * u, axis=0, keepdims=True))
    return du, dws


CONV_COLS = 256


SC_ROWS = 128


def sc_fwd(proj, conv_w, *, name):
    rows = proj.shape[0]
    nb = rows // SC_ROWS
    hb = SC_ROWS // SUBLANES

    def body(p_ref, halo_ref, w_ref, o_ref, ext_ref):
        i = pl.program_id(0)
        w = w_ref[...]
        for c0 in range(0, SC_W, CONV_COLS):
            cols, bc, cc, zc = (slice(k * SC_W + c0, k * SC_W + c0 + CONV_COLS) for k in range(4))
            ext_ref[0:SUBLANES, cols] = jnp.where(i == 0, 0.0, halo_ref[:, cc] * halo_ref[:, cols])
            ext_ref[SUBLANES:, cols] = p_ref[:, cc] * p_ref[:, cols]
            cv = _conv_from_ext(ext_ref, w, SC_CONV, SC_ROWS, cols)
            o_ref[:, cols] = (p_ref[:, bc] * cv * _silu(p_ref[:, zc])).astype(o_ref.dtype)

    return pl.pallas_call(
        body, name=name, grid=(nb,),
        in_specs=[pl.BlockSpec((SC_ROWS, SC_IN), lambda i: (i, 0)),
                  pl.BlockSpec((SUBLANES, SC_IN), lambda i: (jnp.maximum(i * hb - 1, 0), 0)),
                  pl.BlockSpec((SUBLANES, SC_W), lambda i: (0, 0))],
        out_specs=pl.BlockSpec((SC_ROWS, SC_W), lambda i: (i, 0)),
        out_shape=jax.ShapeDtypeStruct((rows, SC_W), MXU_DTYPE),
        scratch_shapes=[pltpu.VMEM((SC_ROWS + SUBLANES, SC_W), F32)],
        compiler_params=_cparams(("parallel",)),
    )(proj, proj, _pad_rows(conv_w))


def sc_bwd(proj, conv_w, dmix, *, name):
    rows = proj.shape[0]
    nb = rows // SC_ROWS
    hb = SC_ROWS // SUBLANES

    def body(p_ref, halo_ref, w_ref, dm_ref, dp_ref, dw_ref, ext_ref, dext_ref):
        i = pl.program_id(0)
        blk = nb - 1 - i
        w = w_ref[...]

        @pl.when(i == 0)
        def _():
            dext_ref[SC_ROWS:, :] = jnp.zeros((SUBLANES, SC_W), F32)
            dw_ref[...] = jnp.zeros_like(dw_ref)

        for c0 in range(0, SC_W, CONV_COLS):
            cols, bc, cc, zc = (slice(k * SC_W + c0, k * SC_W + c0 + CONV_COLS) for k in range(4))
            h, bg, cg, z = p_ref[:, cols], p_ref[:, bc], p_ref[:, cc], p_ref[:, zc]
            ext_ref[0:SUBLANES, cols] = jnp.where(blk == 0, 0.0, halo_ref[:, cc] * halo_ref[:, cols])
            ext_ref[SUBLANES:, cols] = cg * h
            taps = [_rows_from(ext_ref, SUBLANES - (SC_CONV - 1) + j, SC_ROWS, cols) for j in range(SC_CONV)]
            cv = None
            for j in range(SC_CONV):
                term = taps[j] * w[j:j + 1, cols]
                cv = term if cv is None else cv + term
            dm = dm_ref[:, cols]
            dy = dm * _silu(z)
            dp_ref[:, zc] = (dm * bg * cv * _dsilu(z)).astype(dp_ref.dtype)
            dp_ref[:, bc] = (dy * cv).astype(dp_ref.dtype)
            dcv = dy * bg
            dext_ref[0:SC_ROWS, cols] = dcv
            du = _conv_dgrad_from_ext(dext_ref, w, SC_CONV, SC_ROWS, cols)
            dp_ref[:, cols] = (du * cg).astype(dp_ref.dtype)
            dp_ref[:, cc] = (du * h).astype(dp_ref.dtype)
            for j in range(SC_CONV):
                dw_ref[j:j + 1, cols] += jnp.sum(taps[j] * dcv, axis=0, keepdims=True)
            dext_ref[SC_ROWS:, cols] = dcv[0:SUBLANES, :]

    return pl.pallas_call(
        body, name=name, grid=(nb,),
        in_specs=[pl.BlockSpec((SC_ROWS, SC_IN), lambda i: (nb - 1 - i, 0)),
                  pl.BlockSpec((SUBLANES, SC_IN), lambda i: (jnp.maximum((nb - 1 - i) * hb - 1, 0), 0)),
                  pl.BlockSpec((SUBLANES, SC_W), lambda i: (0, 0)),
                  pl.BlockSpec((SC_ROWS, SC_W), lambda i: (nb - 1 - i, 0))],
        out_specs=[pl.BlockSpec((SC_ROWS, SC_IN), lambda i: (nb - 1 - i, 0)),
                   pl.BlockSpec((SUBLANES, SC_W), lambda i: (0, 0))],
        out_shape=[jax.ShapeDtypeStruct((rows, SC_IN), MXU_DTYPE), jax.ShapeDtypeStruct((SUBLANES, SC_W), F32)],
        scratch_shapes=[pltpu.VMEM((SC_ROWS + SUBLANES, SC_W), F32), pltpu.VMEM((SC_ROWS + SUBLANES, SC_W), F32)],
        compiler_params=_cparams(("arbitrary",)),
    )(proj, proj, _pad_rows(conv_w), dmix)


def _pad_rows(w, rows=SUBLANES):
    return jnp.pad(w, ((0, rows - w.shape[0]), (0, 0)))


def _pad_lanes(v, lanes=LANES):
    v = v.reshape(1, -1)
    return jnp.pad(v, ((0, 0), (0, lanes - v.shape[1])))


def _tri(n, lower=True):
    r, c = _iota((n, n), 0), _iota((n, n), 1)
    return jnp.where((c <= r) if lower else (c >= r), 1.0, 0.0)


def _eye(n):
    return jnp.where(_iota((n, n), 0) == _iota((n, n), 1), 1.0, 0.0)


def _head_expand(n, width):
    return jnp.where(_iota((LANES, n), 1) // width == _iota((LANES, n), 0), 1.0, 0.0)


def _col(v, h):
    return jnp.sum(jnp.where(_iota(v.shape, 1) == h, v, 0.0), axis=1, keepdims=True)


def _row(v, r):
    return jnp.sum(jnp.where(_iota(v.shape, 0) == r, v, 0.0), axis=0, keepdims=True)


def _expand_row(v, e):
    return jnp.max(_xsel(jnp.broadcast_to(v, (SUBLANES, LANES)), e), axis=0, keepdims=True)


SSD_ROWS = 128
SSD_X0 = D_INNER
SSD_DT0 = D_INNER + SSD_CONV_DIM
SSD_B0 = D_INNER
SSD_C0 = D_INNER + SSD_G * SSD_S
SSD_GW = D_INNER // SSD_G
SSD_HG = SSD_H // SSD_G


def _ssd_prologue(blk, p_ref, halo_ref, cw_ref, cb_ref, dtb_ref, ext_ref, xbc_ref, dt_ref, pre_ref=None):
    ext_ref[0:SUBLANES, :] = jnp.where(blk == 0, 0.0, halo_ref[:, SSD_X0:SSD_DT0])
    ext_ref[SUBLANES:, :] = p_ref[:, SSD_X0:SSD_DT0]
    w = cw_ref[...]
    for c0 in range(0, SSD_CONV_DIM, CONV_COLS):
        cols = slice(c0, c0 + CONV_COLS)
        pre = _conv_from_ext(ext_ref, w, SSD_CONV, SSD_ROWS, cols) + cb_ref[:, cols]
        if pre_ref is not None:
            pre_ref[:, cols] = pre
        xbc_ref[:, cols] = _silu(pre)
    dt_ref[...] = _softplus(p_ref[:, SSD_DT0:SSD_DT0 + LANES] + dtb_ref[...])


def _ssd_chunk_decays(dt_c, a_row, ltri, eye_l, act_ref):
    da = dt_c * a_row
    ac = _sel(ltri,da)
    act_ref[...] = _sel_nt(eye_l,ac)
    ac_last = _row(ac, CHUNK - 1)
    return ac, jnp.exp(ac_last - ac), jnp.exp(ac), jnp.exp(ac_last)


def _ssd_seg(ac, act_ref, h, causal):
    return jnp.where(causal, jnp.exp(jnp.minimum(_col(ac, h) - act_ref[pl.ds(h, 1), :], 0.0)), 0.0)


def _ssd_half(pair, e):
    upper = _iota(pair.shape, 1) >= SSD_P
    return jnp.where(upper if e % 2 else jnp.logical_not(upper), pair, 0.0)


def _ssd_groups_fwd(xbc_ref, rows, dt_exp, tail_exp, cdec_exp, ac, act_ref, states, causal):
    gs, heads = range(SSD_G), range(SSD_HG)
    gls = [slice(g * SSD_GW, (g + 1) * SSD_GW) for g in gs]
    bg = [_mx(xbc_ref[rows, SSD_B0 + g * SSD_S:SSD_B0 + (g + 1) * SSD_S]) for g in gs]
    cg = [_mx(xbc_ref[rows, SSD_C0 + g * SSD_S:SSD_C0 + (g + 1) * SSD_S]) for g in gs]
    s_c = [_mx(s) for s in states]
    xdt = [xbc_ref[rows, gl] * dt_exp[:, gl] for gl in gls]
    cb = [_dot_nt(cg[g], bg[g]) for g in gs]
    cs = [_dot(cg[g], s_c[g]) for g in gs]
    segs = [[_ssd_seg(ac, act_ref, g * SSD_HG + e, causal) for e in heads] for g in gs]
    gms = [[seg * cb[g] for seg in segs[g]] for g in gs]
    gms_c = [[_mx(gm) for gm in gms[g]] for g in gs]
    halves = [[_ssd_half(xdt[g][:, (e // 2) * LANES:(e // 2 + 1) * LANES], e) for e in heads] for g in gs]
    parts = [[_dot(gms_c[g][e], halves[g][e]) for e in heads] for g in gs]
    yd = [jnp.concatenate([parts[g][2 * p] + parts[g][2 * p + 1] for p in range(SSD_HG // 2)], axis=1) for g in gs]
    st = [_dot_tn(bg[g], xdt[g] * tail_exp[:, gls[g]]) for g in gs]
    return [(yd[g] + cs[g] * cdec_exp[:, gls[g]], st[g], bg[g], cg[g], s_c[g], cb[g], xdt[g], cs[g],
             segs[g], gms[g], gms_c[g]) for g in gs]


def ssd_fwd(proj, conv_w, conv_b, a_log, dt_bias, d_skip, norm_w, *, name):
    rows_total = proj.shape[0]
    nb = rows_total // SSD_ROWS
    hb = SSD_ROWS // SUBLANES
    cpb = SSD_ROWS // CHUNK

    def body(p_ref, halo_ref, cw_ref, cb_ref, alog_ref, dtb_ref, dsk_ref, nw_ref, mix_ref, st_ref, xbc_ref, pre_ref,
             ext_ref, dt_ref, s_ref, act_ref):
        i = pl.program_id(0)

        @pl.when(i == 0)
        def _():
            s_ref[...] = jnp.zeros_like(s_ref)

        _ssd_prologue(i, p_ref, halo_ref, cw_ref, cb_ref, dtb_ref, ext_ref, xbc_ref, dt_ref, pre_ref)
        a_row = -jnp.exp(alog_ref[...])
        expand = _head_expand(D_INNER, SSD_P)
        dsk_exp = _expand_row(dsk_ref[...], expand)
        ltri, eye_l = _tri(CHUNK), _eye(LANES)
        causal = _iota((CHUNK, CHUNK), 1) <= _iota((CHUNK, CHUNK), 0)

        def chunk(c, carry):
            rows = pl.ds(pl.multiple_of(c * CHUNK, CHUNK), CHUNK)
            dt_c = dt_ref[rows, :]
            ac, tail, cdec, tot = _ssd_chunk_decays(dt_c, a_row, ltri, eye_l, act_ref)
            dt_exp = _xsel(dt_c, expand)
            tail_exp = _xsel(tail, expand)
            cdec_exp = _xsel(cdec, expand)
            tot_exp = _expand_row(tot, expand)
            states = [s_ref[g] for g in range(SSD_G)]
            fwd = _ssd_groups_fwd(xbc_ref, rows, dt_exp, tail_exp, cdec_exp, ac, act_ref, states, causal)
            for g in range(SSD_G):
                gl = slice(g * SSD_GW, (g + 1) * SSD_GW)
                st_ref[c, g] = states[g]
                y, st = fwd[g][:2]
                s_ref[g] = states[g] * tot_exp[:, gl] + st
                y = (y + dsk_exp[:, gl] * xbc_ref[rows, gl]) * _silu(p_ref[rows, gl])
                r = lax.rsqrt(jnp.mean(y * y, axis=-1, keepdims=True) + RMS_EPS)
                mix_ref[rows, gl] = (y * r * nw_ref[:, gl]).astype(mix_ref.dtype)
            return carry

        lax.fori_loop(0, cpb, chunk, 0)

    vec = lambda n: pl.BlockSpec((1, n), lambda i: (0, 0))
    mix, states, xbc, pre = pl.pallas_call(
        body, name=name, grid=(nb,),
        in_specs=[pl.BlockSpec((SSD_ROWS, SSD_IN_PAD), lambda i: (i, 0)),
                  pl.BlockSpec((SUBLANES, SSD_IN_PAD), lambda i: (jnp.maximum(i * hb - 1, 0), 0)),
                  pl.BlockSpec((SUBLANES, SSD_CONV_DIM), lambda i: (0, 0)),
                  vec(SSD_CONV_DIM), vec(LANES), vec(LANES), vec(LANES), vec(D_INNER)],
        out_specs=[pl.BlockSpec((SSD_ROWS, D_INNER), lambda i: (i, 0)),
                   pl.BlockSpec((cpb, SSD_G, SSD_S, SSD_GW), lambda i: (i, 0, 0, 0)),
                   pl.BlockSpec((SSD_ROWS, SSD_CONV_DIM), lambda i: (i, 0)),
                   pl.BlockSpec((SSD_ROWS, SSD_CONV_DIM), lambda i: (i, 0))],
        out_shape=[jax.ShapeDtypeStruct((rows_total, D_INNER), MXU_DTYPE),
                   jax.ShapeDtypeStruct((rows_total // CHUNK, SSD_G, SSD_S, SSD_GW), F32),
                   jax.ShapeDtypeStruct((rows_total, SSD_CONV_DIM), F32),
                   jax.ShapeDtypeStruct((rows_total, SSD_CONV_DIM), F32)],
        scratch_shapes=[pltpu.VMEM((SSD_ROWS + SUBLANES, SSD_CONV_DIM), F32),
                        pltpu.VMEM((SSD_ROWS, LANES), F32),
                        pltpu.VMEM((SSD_G, SSD_S, SSD_GW), F32),
                        pltpu.VMEM((LANES, CHUNK), F32)],
        compiler_params=_cparams(("arbitrary",)),
    )(proj, proj, _pad_rows(conv_w), conv_b.reshape(1, -1), _pad_lanes(a_log), _pad_lanes(dt_bias),
      _pad_lanes(d_skip), norm_w.reshape(1, -1))
    return mix, (states, xbc, pre)


def ssd_bwd(proj, conv_w, a_log, dt_bias, d_skip, norm_w, saved, dmix, *, name):
    states, xbc, pre = saved
    rows_total = proj.shape[0]
    nb = rows_total // SSD_ROWS
    cpb = SSD_ROWS // CHUNK

    def body(p_ref, cw_ref, alog_ref, dtb_ref, dsk_ref, nw_ref, st_ref, dm_ref, xbc_ref, pre_ref,
             dp_ref, dcw_ref, dcb_ref, dalog_ref, ddtb_ref, ddsk_ref, dnw_ref,
             dt_ref, ds_ref, act_ref, dext_ref, dac_ref, dact_ref, ddskw_ref):
        i = pl.program_id(0)

        @pl.when(i == 0)
        def _():
            ds_ref[...] = jnp.zeros_like(ds_ref)
            dext_ref[SSD_ROWS:, :] = jnp.zeros((SUBLANES, SSD_CONV_DIM), F32)
            ddskw_ref[...] = jnp.zeros_like(ddskw_ref)
            for r in (dcw_ref, dcb_ref, dalog_ref, ddtb_ref, ddsk_ref, dnw_ref):
                r[...] = jnp.zeros_like(r)

        dt_ref[...] = _softplus(p_ref[:, SSD_DT0:SSD_DT0 + LANES] + dtb_ref[...])
        a_row = -jnp.exp(alog_ref[...])
        expand = _head_expand(D_INNER, SSD_P)
        dsk_exp = _expand_row(dsk_ref[...], expand)
        ltri, utri, eye_l, eye_c = _tri(CHUNK), _tri(CHUNK, lower=False), _eye(LANES), _eye(CHUNK)
        causal = _iota((CHUNK, CHUNK), 1) <= _iota((CHUNK, CHUNK), 0)
        dp_ref[:, SSD_DT0 + LANES:] = jnp.zeros((SSD_ROWS, SSD_IN_PAD - SSD_DT0 - LANES), dp_ref.dtype)

        def chunk(cc, carry):
            c = cpb - 1 - cc
            rows = pl.ds(pl.multiple_of(c * CHUNK, CHUNK), CHUNK)
            dt_c = dt_ref[rows, :]
            ac, tail, cdec, tot = _ssd_chunk_decays(dt_c, a_row, ltri, eye_l, act_ref)
            dt_exp = _xsel(dt_c, expand)
            tail_exp = _xsel(tail, expand)
            cdec_exp = _xsel(cdec, expand)
            tot_exp = _expand_row(tot, expand)
            dac_ref[...] = jnp.zeros_like(dac_ref)
            dact_ref[...] = jnp.zeros_like(dact_ref)
            d_cdec = jnp.zeros((CHUNK, LANES), F32)
            d_tail = jnp.zeros((CHUNK, LANES), F32)
            d_dt = jnp.zeros((CHUNK, LANES), F32)
            d_tot = jnp.zeros((1, LANES), F32)
            gs = range(SSD_G)
            gls = [slice(g * SSD_GW, (g + 1) * SSD_GW) for g in gs]
            exs = [expand[:, gl] for gl in gls]
            states = [st_ref[c, g] for g in gs]
            fwd = _ssd_groups_fwd(xbc_ref, rows, dt_exp, tail_exp, cdec_exp, ac, act_ref, states, causal)
            ys, _, bgs, cgs, s_cs, cbs, xdts, css, segss, gmss, gms_cs = (list(t) for t in zip(*fwd))
            xss = [xbc_ref[rows, gl] for gl in gls]
            dys = []
            for g, gl in enumerate(gls):
                z = p_ref[rows, gl]
                sz = _silu(z)
                y2 = ys[g] + dsk_exp[:, gl] * xss[g]
                yg = y2 * sz
                r = lax.rsqrt(jnp.mean(yg * yg, axis=-1, keepdims=True) + RMS_EPS)
                yn = yg * r
                dm = dm_ref[rows, gl]
                dnw_ref[:, gl] += jnp.sum(dm * yn, axis=0, keepdims=True)
                dyn = dm * nw_ref[:, gl]
                dyg = r * (dyn - yn * jnp.mean(dyn * yn, axis=-1, keepdims=True))
                dp_ref[rows, gl] = (dyg * y2 * _dsilu(z)).astype(dp_ref.dtype)
                dys.append(dyg * sz)
                ddskw_ref[:, gl] += jnp.sum(dys[g] * xss[g], axis=0, keepdims=True)
            ds_gs = [ds_ref[g] for g in gs]
            ds_cs = [_mx(d) for d in ds_gs]
            dycs = [_mx(dys[g] * cdec_exp[:, gls[g]]) for g in gs]
            ds_new = [_dot_tn(cgs[g], dycs[g]) for g in gs]
            dcgs = [_dot_nt(dycs[g], s_cs[g]) for g in gs]
            d_xdtds = [_dot(bgs[g], ds_cs[g]) for g in gs]
            dbgs = [_dot_nt(xdts[g] * tail_exp[:, gls[g]], ds_cs[g]) for g in gs]
            for g in gs:
                ds_ref[g] = ds_gs[g] * tot_exp[:, gls[g]] + ds_new[g]
                sds = jnp.broadcast_to(jnp.sum(states[g] * ds_gs[g], axis=0, keepdims=True), (SUBLANES, SSD_GW))
                d_tot = d_tot + jnp.max(_xsel_nt(sds, exs[g]), axis=0, keepdims=True)
                d_cdec = d_cdec + _xsel_nt(dys[g] * css[g], exs[g])
                d_tail = d_tail + _xsel_nt(d_xdtds[g] * xdts[g], exs[g])
            heads = range(SSD_HG)
            dy_hs = [[_mx(_ssd_half(dys[g][:, (e // 2) * LANES:(e // 2 + 1) * LANES], e)) for e in heads] for g in gs]
            xps_cs = [[_mx(xdts[g][:, p * LANES:(p + 1) * LANES]) for p in range(SSD_HG // 2)] for g in gs]
            backs = [[_dot_tn(gms_cs[g][e], dy_hs[g][e]) for e in heads] for g in gs]
            dg_ms = [[jnp.where(causal, _dot_nt(dy_hs[g][e], xps_cs[g][e // 2]), 0.0) for e in heads] for g in gs]
            d_cbs = []
            for g in gs:
                d_cb = None
                for e in heads:
                    h = g * SSD_HG + e
                    term = dg_ms[g][e] * segss[g][e]
                    d_cb = term if d_cb is None else d_cb + term
                    em = dg_ms[g][e] * gmss[g][e]
                    dac_ref[...] += jnp.where(_iota((CHUNK, LANES), 1) == h, jnp.sum(em, axis=1, keepdims=True), 0.0)
                    dact_ref[h:h + 1, :] = jnp.sum(em, axis=0, keepdims=True)
                d_cbs.append(_mx(d_cb))
            dcg2 = [_dot(d_cbs[g], bgs[g]) for g in gs]
            dbg2 = [_dot_tn(d_cbs[g], cgs[g]) for g in gs]
            for g, gl in enumerate(gls):
                d_xdt = d_xdtds[g] * tail_exp[:, gl] + jnp.concatenate(
                    [backs[g][2 * p] + backs[g][2 * p + 1] for p in range(SSD_HG // 2)], axis=1)
                d_dt = d_dt + _xsel_nt(d_xdt * xss[g], exs[g])
                dext_ref[rows, gl] = d_xdt * dt_exp[:, gl] + dys[g] * dsk_exp[:, gl]
                dext_ref[rows, SSD_B0 + g * SSD_S:SSD_B0 + (g + 1) * SSD_S] = dbgs[g] + dbg2[g]
                dext_ref[rows, SSD_C0 + g * SSD_S:SSD_C0 + (g + 1) * SSD_S] = dcgs[g] + dcg2[g]
            d_ac = dac_ref[...] - _sel_nt(eye_c,dact_ref[...]) + d_cdec * cdec - d_tail * tail
            d_last = jnp.sum(d_tail * tail, axis=0, keepdims=True) + d_tot * tot
            d_ac = jnp.where(_iota((CHUNK, LANES), 0) == CHUNK - 1, d_ac + d_last, d_ac)
            d_da = _sel(utri,d_ac)
            d_dt = d_dt + d_da * a_row
            dalog_ref[...] += jnp.sum(d_da * dt_c, axis=0, keepdims=True) * a_row
            d_raw = d_dt * _sigmoid(p_ref[rows, SSD_DT0:SSD_DT0 + LANES] + dtb_ref[...])
            d_raw = jnp.where(_iota((CHUNK, LANES), 1) < SSD_H, d_raw, 0.0)
            ddtb_ref[...] += jnp.sum(d_raw, axis=0, keepdims=True)
            dp_ref[rows, SSD_DT0:SSD_DT0 + LANES] = d_raw.astype(dp_ref.dtype)
            return carry

        lax.fori_loop(0, cpb, chunk, 0)
        w = cw_ref[...]
        for c0 in range(0, SSD_CONV_DIM, CONV_COLS):
            cols = slice(c0, c0 + CONV_COLS)
            d_pre = dext_ref[0:SSD_ROWS, cols] * _dsilu(pre_ref[:, cols])
            dext_ref[0:SSD_ROWS, cols] = d_pre
            dcb_ref[:, cols] += jnp.sum(d_pre, axis=0, keepdims=True)
            du, dws = _conv_bwd_from_ext(dext_ref, p_ref[:, SSD_X0 + c0:SSD_X0 + c0 + CONV_COLS], w, SSD_CONV, SSD_ROWS, cols)
            for j in range(SSD_CONV):
                dcw_ref[j:j + 1, cols] += dws[j]
            dp_ref[:, SSD_X0 + c0:SSD_X0 + c0 + CONV_COLS] = du.astype(dp_ref.dtype)
            dext_ref[SSD_ROWS:, cols] = d_pre[0:SUBLANES, :]

        @pl.when(i == nb - 1)
        def _():
            ddsk_ref[...] = jnp.max(_xsel_nt(jnp.broadcast_to(ddskw_ref[...], (SUBLANES, D_INNER)), expand), axis=0, keepdims=True)

    vec = lambda n: pl.BlockSpec((1, n), lambda i: (0, 0))
    outs = pl.pallas_call(
        body, name=name, grid=(nb,),
        in_specs=[pl.BlockSpec((SSD_ROWS, SSD_IN_PAD), lambda i: (nb - 1 - i, 0)),
                  pl.BlockSpec((SUBLANES, SSD_CONV_DIM), lambda i: (0, 0)),
                  vec(LANES), vec(LANES), vec(LANES), vec(D_INNER),
                  pl.BlockSpec((cpb, SSD_G, SSD_S, SSD_GW), lambda i: (nb - 1 - i, 0, 0, 0)),
                  pl.BlockSpec((SSD_ROWS, D_INNER), lambda i: (nb - 1 - i, 0)),
                  pl.BlockSpec((SSD_ROWS, SSD_CONV_DIM), lambda i: (nb - 1 - i, 0)),
                  pl.BlockSpec((SSD_ROWS, SSD_CONV_DIM), lambda i: (nb - 1 - i, 0))],
        out_specs=[pl.BlockSpec((SSD_ROWS, SSD_IN_PAD), lambda i: (nb - 1 - i, 0)),
                   pl.BlockSpec((SUBLANES, SSD_CONV_DIM), lambda i: (0, 0)),
                   vec(SSD_CONV_DIM), vec(LANES), vec(LANES), vec(LANES), vec(D_INNER)],
        out_shape=[jax.ShapeDtypeStruct((rows_total, SSD_IN_PAD), MXU_DTYPE),
                   jax.ShapeDtypeStruct((SUBLANES, SSD_CONV_DIM), F32),
                   jax.ShapeDtypeStruct((1, SSD_CONV_DIM), F32), jax.ShapeDtypeStruct((1, LANES), F32),
                   jax.ShapeDtypeStruct((1, LANES), F32), jax.ShapeDtypeStruct((1, LANES), F32),
                   jax.ShapeDtypeStruct((1, D_INNER), F32)],
        scratch_shapes=[pltpu.VMEM((SSD_ROWS, LANES), F32),
                        pltpu.VMEM((SSD_G, SSD_S, SSD_GW), F32),
                        pltpu.VMEM((LANES, CHUNK), F32),
                        pltpu.VMEM((SSD_ROWS + SUBLANES, SSD_CONV_DIM), F32),
                        pltpu.VMEM((CHUNK, LANES), F32),
                        pltpu.VMEM((LANES, CHUNK), F32),
                        pltpu.VMEM((1, D_INNER), F32)],
        compiler_params=_cparams(("arbitrary",)),
    )(proj, _pad_rows(conv_w), _pad_lanes(a_log), _pad_lanes(dt_bias),
      _pad_lanes(d_skip), norm_w.reshape(1, -1), states, dmix, xbc, pre)
    dproj, dcw, dcb, dalog, ddtb, ddsk, dnw = outs
    return dproj, [dcw[:SSD_CONV], dcb[0], dalog[0, :SSD_H], ddtb[0, :SSD_H], ddsk[0, :SSD_H], dnw[0]]


GDN_ROWS = 128
GDN_K0 = GDN_QK
GDN_V0 = 2 * GDN_QK
GDN_Z0 = GDN_CONV_DIM
GDN_BA0 = GDN_CONV_DIM + GDN_V
GDN_GL = GDN_VH
GDN_SCALE = GDN_HEAD ** -0.5
GDN_GROUP = 16


def _gdn_lane_params(v):
    return jnp.pad(v.reshape(1, GDN_VH), ((0, 0), (GDN_GL, LANES - GDN_GL - GDN_VH)))


def _gdn_prologue(blk, p_ref, halo_ref, cw_ref, alog_ref, dtb_ref, ext_ref, qkv_ref, beta_ref, g_ref, pre_ref=None):
    ext_ref[0:SUBLANES, :] = jnp.where(blk == 0, 0.0, halo_ref[:, 0:GDN_CONV_DIM])
    ext_ref[SUBLANES:, :] = p_ref[:, 0:GDN_CONV_DIM]
    w = cw_ref[...]
    for hh in range(GDN_CONV_DIM // GDN_HEAD):
        cols = slice(hh * GDN_HEAD, (hh + 1) * GDN_HEAD)
        pre = _conv_from_ext(ext_ref, w, GDN_CONV, GDN_ROWS, cols)
        if pre_ref is not None:
            pre_ref[:, cols] = pre
        a = _silu(pre)
        if hh < 2 * GDN_QKH:
            r = lax.rsqrt(jnp.sum(a * a, axis=-1, keepdims=True) + L2_EPS)
            a = a * (r * (GDN_SCALE if hh < GDN_QKH else 1.0))
        qkv_ref[:, cols] = a
    _gdn_gates(p_ref, alog_ref, dtb_ref, beta_ref, g_ref)


def _gdn_gates(p_ref, alog_ref, dtb_ref, beta_ref, g_ref):
    ba = p_ref[:, GDN_BA0:GDN_BA0 + LANES]
    beta_ref[...] = _sigmoid(ba)
    g_ref[...] = -jnp.exp(alog_ref[...]) * _softplus(ba + dtb_ref[...])


def _each(f, *lists):
    return [f(*z) for z in zip(*lists)]


def _inv_unit_lower_each(a_list, eye_c):
    xs = [eye_c - a for a in a_list]
    ps = [_mx(a) for a in a_list]
    n = 2
    while n < CHUNK:
        ps = [_mx(_dot(p, p)) for p in ps]
        xs = [x + _dot(x, p) for x, p in zip(xs, ps)]
        n *= 2
    return xs


def _mx(x):
    return x.astype(MXU_DTYPE)


def _gdn_heads_fwd(q, k, v, kk, qk, gcol, grow, glast, bcol, s, causal, strict, eye_c, t=None):
    decay = _each(lambda gc_, gr_: jnp.where(causal, jnp.exp(jnp.minimum(gc_ - gr_, 0.0)), 0.0), gcol, grow)
    egc = _each(jnp.exp, gcol)
    etail = _each(lambda gl_, gc_: jnp.exp(gl_ - gc_), glast, gcol)
    cd = _each(jnp.exp, glast)
    a = _each(lambda b_, kk_, d_: jnp.where(strict, b_ * kk_ * d_, 0.0), bcol, kk, decay)
    if t is None:
        t = _inv_unit_lower_each(a, eye_c)
    t_c, s_c = _each(_mx, t), _each(_mx, s)
    kb = _each(lambda k_, b_: k_ * b_, k, bcol)
    rhs_w = _each(lambda kb_, e_: kb_ * e_, kb, egc)
    u = _each(lambda t_, v_, b_: _dot(t_, v_ * b_), t_c, v, bcol)
    w = _each(_dot, t_c, rhs_w)
    w_c = _each(_mx, w)
    attn = _each(lambda qk_, d_: qk_ * d_, qk, decay)
    attn_c = _each(_mx, attn)
    ws = _each(_dot, w_c, s_c)
    v_new = _each(lambda u_, ws_: u_ - ws_, u, ws)
    vn_c = _each(_mx, v_new)
    qd = _each(lambda q_, e_: q_ * e_, q, egc)
    kt = _each(lambda k_, e_: k_ * e_, k, etail)
    qd_c, kt_c = _each(_mx, qd), _each(_mx, kt)
    o1 = _each(_dot, qd_c, s_c)
    o2 = _each(_dot, attn_c, vn_c)
    out = _each(lambda a_, b_: a_ + b_, o1, o2)
    upd = _each(_dot_tn, kt_c, vn_c)
    s_new = _each(lambda s_, c_, u_: s_ * c_ + u_, s, cd, upd)
    return dict(decay=decay, egc=egc, etail=etail, cd=cd, a=a, t=t, kb=kb, rhs_w=rhs_w, u=u, w=w, attn=attn,
                v_new=v_new, qd=qd, kt=kt, out=out, s_new=s_new,
                t_c=t_c, s_c=s_c, w_c=w_c, attn_c=attn_c, vn_c=vn_c, qd_c=qd_c, kt_c=kt_c)


def gdn_fwd(proj, conv_w, a_log, dt_bias, norm_w, *, name):
    rows_total = proj.shape[0]
    nb = rows_total // GDN_ROWS
    hb = GDN_ROWS // SUBLANES
    cpb = GDN_ROWS // CHUNK

    def body(p_ref, halo_ref, cw_ref, alog_ref, dtb_ref, nw_ref, mix_ref, st_ref, tm_ref, qkv_ref, pre_ref,
             ext_ref, beta_ref, g_ref, s_ref, gct_ref):
        i = pl.program_id(0)

        @pl.when(i == 0)
        def _():
            s_ref[...] = jnp.zeros_like(s_ref)

        _gdn_prologue(i, p_ref, halo_ref, cw_ref, alog_ref, dtb_ref, ext_ref, qkv_ref, beta_ref, g_ref, pre_ref)
        ltri, eye_l, eye_c = _tri(CHUNK), _eye(LANES), _eye(CHUNK)
        causal = _iota((CHUNK, CHUNK), 1) <= _iota((CHUNK, CHUNK), 0)
        strict = _iota((CHUNK, CHUNK), 1) < _iota((CHUNK, CHUNK), 0)
        nw = nw_ref[...]

        def chunk(c, carry):
            rows = pl.ds(pl.multiple_of(c * CHUNK, CHUNK), CHUNK)
            gc = _sel(ltri,g_ref[rows, :])
            gct_ref[...] = _sel_nt(eye_l,gc)
            glast_row = _row(gc, CHUNK - 1)
            beta_c = beta_ref[rows, :]
            for h0 in range(0, GDN_VH, GDN_GROUP):
                hs = list(range(h0, h0 + GDN_GROUP))
                qs = {hq: qkv_ref[rows, hq * GDN_HEAD:(hq + 1) * GDN_HEAD] for hq in range(h0 // 2, (h0 + GDN_GROUP) // 2)}
                ks = {hq: qkv_ref[rows, GDN_K0 + hq * GDN_HEAD:GDN_K0 + (hq + 1) * GDN_HEAD] for hq in qs}
                ks_c = {hq: _mx(ks[hq]) for hq in qs}
                kks = {hq: _dot_nt(ks_c[hq], ks_c[hq]) for hq in qs}
                qks = {hq: _dot_nt(qs[hq], ks_c[hq]) for hq in qs}
                ss = [s_ref[h] for h in hs]
                for h, s in zip(hs, ss):
                    st_ref[c, h] = s
                f = _gdn_heads_fwd(
                    [qs[h // 2] for h in hs], [ks[h // 2] for h in hs],
                    [qkv_ref[rows, GDN_V0 + h * GDN_HEAD:GDN_V0 + (h + 1) * GDN_HEAD] for h in hs],
                    [kks[h // 2] for h in hs], [qks[h // 2] for h in hs],
                    [_col(gc, GDN_GL + h) for h in hs], [gct_ref[GDN_GL + h:GDN_GL + h + 1, :] for h in hs],
                    [_col(glast_row, GDN_GL + h) for h in hs], [_col(beta_c, h) for h in hs], ss, causal, strict, eye_c)
                for i_h, h in enumerate(hs):
                    hc = slice(h * GDN_HEAD, (h + 1) * GDN_HEAD)
                    s_ref[h] = f["s_new"][i_h]
                    tm_ref[c, h] = f["t"][i_h].astype(tm_ref.dtype)
                    o = f["out"][i_h]
                    r = lax.rsqrt(jnp.mean(o * o, axis=-1, keepdims=True) + RMS_EPS)
                    z = p_ref[rows, GDN_Z0 + h * GDN_HEAD:GDN_Z0 + (h + 1) * GDN_HEAD]
                    mix_ref[rows, hc] = (o * r * nw * _silu(z)).astype(mix_ref.dtype)
            return carry

        lax.fori_loop(0, cpb, chunk, 0)

    vec = lambda n: pl.BlockSpec((1, n), lambda i: (0, 0))
    act = pl.BlockSpec((GDN_ROWS, GDN_CONV_DIM), lambda i: (i, 0))
    mix, states, tmats, qkv, pre = pl.pallas_call(
        body, name=name, grid=(nb,),
        in_specs=[pl.BlockSpec((GDN_ROWS, GDN_IN_PAD), lambda i: (i, 0)),
                  pl.BlockSpec((SUBLANES, GDN_IN_PAD), lambda i: (jnp.maximum(i * hb - 1, 0), 0)),
                  pl.BlockSpec((SUBLANES, GDN_CONV_DIM), lambda i: (0, 0)),
                  vec(LANES), vec(LANES), vec(GDN_HEAD)],
        out_specs=[pl.BlockSpec((GDN_ROWS, GDN_V), lambda i: (i, 0)),
                   pl.BlockSpec((cpb, GDN_VH, GDN_HEAD, GDN_HEAD), lambda i: (i, 0, 0, 0)),
                   pl.BlockSpec((cpb, GDN_VH, CHUNK, CHUNK), lambda i: (i, 0, 0, 0)), act, act],
        out_shape=[jax.ShapeDtypeStruct((rows_total, GDN_V), MXU_DTYPE),
                   jax.ShapeDtypeStruct((rows_total // CHUNK, GDN_VH, GDN_HEAD, GDN_HEAD), F32),
                   jax.ShapeDtypeStruct((rows_total // CHUNK, GDN_VH, CHUNK, CHUNK), MXU_DTYPE),
                   jax.ShapeDtypeStruct((rows_total, GDN_CONV_DIM), F32),
                   jax.ShapeDtypeStruct((rows_total, GDN_CONV_DIM), F32)],
        scratch_shapes=[pltpu.VMEM((GDN_ROWS + SUBLANES, GDN_CONV_DIM), F32),
                        pltpu.VMEM((GDN_ROWS, LANES), F32),
                        pltpu.VMEM((GDN_ROWS, LANES), F32),
                        pltpu.VMEM((GDN_VH, GDN_HEAD, GDN_HEAD), F32),
                        pltpu.VMEM((LANES, CHUNK), F32)],
        compiler_params=_cparams(("arbitrary",)),
    )(proj, proj, _pad_rows(conv_w), _gdn_lane_params(a_log), _gdn_lane_params(dt_bias), norm_w.reshape(1, -1))
    return mix, (states, tmats, qkv, pre)


def gdn_bwd(proj, conv_w, a_log, dt_bias, norm_w, saved, dmix, *, name):
    states, tmats, qkv, pre = saved
    rows_total = proj.shape[0]
    nb = rows_total // GDN_ROWS
    cpb = GDN_ROWS // CHUNK

    def body(p_ref, cw_ref, alog_ref, dtb_ref, nw_ref, st_ref, tm_ref, dm_ref, qkv_ref, pre_ref,
             dp_ref, dcw_ref, dalog_ref, ddtb_ref, dnw_ref,
             beta_ref, g_ref, ds_ref, gct_ref, dext_ref, dgc_ref, dgct_ref, dbeta_ref):
        i = pl.program_id(0)

        @pl.when(i == 0)
        def _():
            ds_ref[...] = jnp.zeros_like(ds_ref)
            dext_ref[GDN_ROWS:, :] = jnp.zeros((SUBLANES, GDN_CONV_DIM), F32)
            for r in (dcw_ref, dalog_ref, ddtb_ref, dnw_ref):
                r[...] = jnp.zeros_like(r)

        _gdn_gates(p_ref, alog_ref, dtb_ref, beta_ref, g_ref)
        ltri, utri, eye_l, eye_c = _tri(CHUNK), _tri(CHUNK, lower=False), _eye(LANES), _eye(CHUNK)
        causal = _iota((CHUNK, CHUNK), 1) <= _iota((CHUNK, CHUNK), 0)
        strict = _iota((CHUNK, CHUNK), 1) < _iota((CHUNK, CHUNK), 0)
        lane = _iota((CHUNK, LANES), 1)
        is_last = _iota((CHUNK, 1), 0) == CHUNK - 1
        nw = nw_ref[...]

        def chunk(cc, carry):
            c = cpb - 1 - cc
            rows = pl.ds(pl.multiple_of(c * CHUNK, CHUNK), CHUNK)
            g_c = g_ref[rows, :]
            gc = _sel(ltri,g_c)
            gct_ref[...] = _sel_nt(eye_l,gc)
            glast_row = _row(gc, CHUNK - 1)
            beta_c = beta_ref[rows, :]
            dgc_ref[...] = jnp.zeros_like(dgc_ref)
            dgct_ref[...] = jnp.zeros_like(dgct_ref)
            dbeta_ref[...] = jnp.zeros_like(dbeta_ref)
            for h0 in range(0, GDN_VH, GDN_GROUP):
                hs = list(range(h0, h0 + GDN_GROUP))
                hqs = list(range(h0 // 2, (h0 + GDN_GROUP) // 2))
                qs = {hq: qkv_ref[rows, hq * GDN_HEAD:(hq + 1) * GDN_HEAD] for hq in hqs}
                ks = {hq: qkv_ref[rows, GDN_K0 + hq * GDN_HEAD:GDN_K0 + (hq + 1) * GDN_HEAD] for hq in hqs}
                qs_c = {hq: _mx(qs[hq]) for hq in hqs}
                ks_c = {hq: _mx(ks[hq]) for hq in hqs}
                kks = {hq: _dot_nt(ks_c[hq], ks_c[hq]) for hq in hqs}
                qks = {hq: _dot_nt(qs_c[hq], ks_c[hq]) for hq in hqs}
                q = [qs[h // 2] for h in hs]
                k = [ks[h // 2] for h in hs]
                v = [qkv_ref[rows, GDN_V0 + h * GDN_HEAD:GDN_V0 + (h + 1) * GDN_HEAD] for h in hs]
                s = [st_ref[c, h] for h in hs]
                bcol = [_col(beta_c, h) for h in hs]
                f = _gdn_heads_fwd(q, k, v, [kks[h // 2] for h in hs], [qks[h // 2] for h in hs],
                                   [_col(gc, GDN_GL + h) for h in hs], [gct_ref[GDN_GL + h:GDN_GL + h + 1, :] for h in hs],
                                   [_col(glast_row, GDN_GL + h) for h in hs], bcol, s, causal, strict, eye_c,
                                   t=[tm_ref[c, h] for h in hs])
                do = []
                for i_h, h in enumerate(hs):
                    zc = slice(GDN_Z0 + h * GDN_HEAD, GDN_Z0 + (h + 1) * GDN_HEAD)
                    o = f["out"][i_h]
                    z = p_ref[rows, zc]
                    sz = _silu(z)
                    r = lax.rsqrt(jnp.mean(o * o, axis=-1, keepdims=True) + RMS_EPS)
                    on = o * r
                    dm = dm_ref[rows, h * GDN_HEAD:(h + 1) * GDN_HEAD]
                    dnw_ref[...] += jnp.sum(dm * on * sz, axis=0, keepdims=True)
                    d_on = dm * nw * sz
                    dp_ref[rows, zc] = (dm * on * nw * _dsilu(z)).astype(dp_ref.dtype)
                    do.append(r * (d_on - on * jnp.mean(d_on * on, axis=-1, keepdims=True)))
                ds_n = [ds_ref[h] for h in hs]
                do_c, dsn_c = _each(_mx, do), _each(_mx, ds_n)
                k_c = [ks_c[h // 2] for h in hs]
                dv1 = _each(_dot_tn, f["attn_c"], do_c)
                dv2 = _each(_dot, f["kt_c"], dsn_c)
                d_vnew = _each(lambda a_, b_: a_ + b_, dv1, dv2)
                dvn_c = _each(_mx, d_vnew)
                d_attn = _each(lambda do_, vn_: jnp.where(causal, _dot_nt(do_, vn_), 0.0), do_c, f["vn_c"])
                d_qd = _each(_dot_nt, do_c, f["s_c"])
                t1 = _each(_dot_tn, f["qd_c"], do_c)
                t2 = _each(_dot_tn, f["w_c"], dvn_c)
                for h, a_, cd_, dsn_, b_ in zip(hs, t1, f["cd"], ds_n, t2):
                    ds_ref[h] = a_ + cd_ * dsn_ - b_
                d_cd = _each(lambda s_, dsn_: jnp.sum(jnp.sum(s_ * dsn_, axis=1, keepdims=True), axis=0, keepdims=True), s, ds_n)
                d_kt = _each(_dot_nt, f["vn_c"], dsn_c)
                d_w = _each(lambda dv_, s_: -_dot_nt(dv_, s_), dvn_c, f["s_c"])
                d_rhs_u = _each(_dot_tn, f["t_c"], dvn_c)
                d_rhs_w = _each(_dot_tn, f["t_c"], d_w)
                m1 = _each(_dot_nt, d_rhs_u, f["u"])
                m2 = _each(_dot_nt, d_rhs_w, f["w_c"])
                da = _each(lambda a_, b_: -jnp.where(strict, a_ + b_, 0.0), m1, m2)
                dmm = _each(lambda a_, b_: a_ * b_, da, f["decay"])
                em = _each(lambda da_, a_, dat_, at_: da_ * a_ + dat_ * at_, da, f["a"], d_attn, f["attn"])
                dmm_c = _each(_mx, dmm)
                x1 = _each(_dot, dmm_c, k_c)
                d_kb = _each(lambda x_, drw_, e_: x_ + drw_ * e_, x1, d_rhs_w, f["egc"])
                dk1 = _each(_dot_tn, dmm_c, f["kb"])
                dpm = _each(lambda a_, b_: _mx(a_ * b_), d_attn, f["decay"])
                dq1 = _each(_dot, dpm, k_c)
                dq = _each(lambda x_, dqd_, e_: x_ + dqd_ * e_, dq1, d_qd, f["egc"])
                dk2 = _each(_dot_tn, dpm, [qs_c[h // 2] for h in hs])
                dk = _each(lambda a_, b_, dkb_, bc_, dkt_, et_: a_ + b_ + dkb_ * bc_ + dkt_ * et_,
                           dk1, dk2, d_kb, bcol, d_kt, f["etail"])
                for i_h, h in enumerate(hs):
                    tmp = jnp.sum(d_kt[i_h] * f["kt"][i_h], axis=1, keepdims=True)
                    d_gcol = (jnp.sum(em[i_h], axis=1, keepdims=True)
                              + jnp.sum(d_rhs_w[i_h] * f["rhs_w"][i_h], axis=1, keepdims=True)
                              + jnp.sum(d_qd[i_h] * f["qd"][i_h], axis=1, keepdims=True) - tmp)
                    d_glast = jnp.sum(tmp, axis=0, keepdims=True) + d_cd[i_h] * f["cd"][i_h]
                    d_gcol = jnp.where(is_last, d_gcol + d_glast, d_gcol)
                    d_beta = (jnp.sum(d_rhs_u[i_h] * v[i_h], axis=1, keepdims=True)
                              + jnp.sum(d_kb[i_h] * k[i_h], axis=1, keepdims=True))
                    dgc_ref[...] += jnp.where(lane == GDN_GL + h, d_gcol, 0.0)
                    dgct_ref[GDN_GL + h:GDN_GL + h + 1, :] = jnp.sum(em[i_h], axis=0, keepdims=True)
                    dbeta_ref[...] += jnp.where(lane == h, d_beta, 0.0)
                    dext_ref[rows, GDN_V0 + h * GDN_HEAD:GDN_V0 + (h + 1) * GDN_HEAD] = d_rhs_u[i_h] * bcol[i_h]
                for hq in hqs:
                    i0 = 2 * hq - h0
                    dext_ref[rows, hq * GDN_HEAD:(hq + 1) * GDN_HEAD] = dq[i0] + dq[i0 + 1]
                    dext_ref[rows, GDN_K0 + hq * GDN_HEAD:GDN_K0 + (hq + 1) * GDN_HEAD] = dk[i0] + dk[i0 + 1]
            d_gc = dgc_ref[...] - _sel_nt(eye_c,dgct_ref[...])
            dg = _sel(utri,d_gc)
            ba = p_ref[rows, GDN_BA0:GDN_BA0 + LANES]
            d_sp = dg * -jnp.exp(alog_ref[...])
            d_araw = d_sp * _sigmoid(ba + dtb_ref[...])
            d_araw = jnp.where((lane >= GDN_GL) & (lane < GDN_GL + GDN_VH), d_araw, 0.0)
            dalog_ref[...] += jnp.sum(dg * g_c, axis=0, keepdims=True)
            ddtb_ref[...] += jnp.sum(d_araw, axis=0, keepdims=True)
            d_braw = jnp.where(lane < GDN_VH, dbeta_ref[...] * beta_c * (1.0 - beta_c), 0.0)
            dp_ref[rows, GDN_BA0:GDN_BA0 + LANES] = (d_braw + d_araw).astype(dp_ref.dtype)
            return carry

        lax.fori_loop(0, cpb, chunk, 0)
        w = cw_ref[...]
        for hh in range(GDN_CONV_DIM // GDN_HEAD):
            cols = slice(hh * GDN_HEAD, (hh + 1) * GDN_HEAD)
            pre = pre_ref[:, cols]
            d_act = dext_ref[0:GDN_ROWS, cols]
            if hh < 2 * GDN_QKH:
                a = _silu(pre)
                r = lax.rsqrt(jnp.sum(a * a, axis=-1, keepdims=True) + L2_EPS)
                ah = a * r
                if hh < GDN_QKH:
                    d_act = d_act * GDN_SCALE
                d_act = r * (d_act - ah * jnp.sum(d_act * ah, axis=-1, keepdims=True))
            d_pre = d_act * _dsilu(pre)
            dext_ref[0:GDN_ROWS, cols] = d_pre
            du, dws = _conv_bwd_from_ext(dext_ref, p_ref[:, cols], w, GDN_CONV, GDN_ROWS, cols)
            for j in range(GDN_CONV):
                dcw_ref[j:j + 1, cols] += dws[j]
            dp_ref[:, cols] = du.astype(dp_ref.dtype)
            dext_ref[GDN_ROWS:, cols] = d_pre[0:SUBLANES, :]

    vec = lambda n: pl.BlockSpec((1, n), lambda i: (0, 0))
    outs = pl.pallas_call(
        body, name=name, grid=(nb,),
        in_specs=[pl.BlockSpec((GDN_ROWS, GDN_IN_PAD), lambda i: (nb - 1 - i, 0)),
                  pl.BlockSpec((SUBLANES, GDN_CONV_DIM), lambda i: (0, 0)),
                  vec(LANES), vec(LANES), vec(GDN_HEAD),
                  pl.BlockSpec((cpb, GDN_VH, GDN_HEAD, GDN_HEAD), lambda i: (nb - 1 - i, 0, 0, 0)),
                  pl.BlockSpec((cpb, GDN_VH, CHUNK, CHUNK), lambda i: (nb - 1 - i, 0, 0, 0)),
                  pl.BlockSpec((GDN_ROWS, GDN_V), lambda i: (nb - 1 - i, 0)),
                  pl.BlockSpec((GDN_ROWS, GDN_CONV_DIM), lambda i: (nb - 1 - i, 0)),
                  pl.BlockSpec((GDN_ROWS, GDN_CONV_DIM), lambda i: (nb - 1 - i, 0))],
        out_specs=[pl.BlockSpec((GDN_ROWS, GDN_IN_PAD), lambda i: (nb - 1 - i, 0)),
                   pl.BlockSpec((SUBLANES, GDN_CONV_DIM), lambda i: (0, 0)),
                   vec(LANES), vec(LANES), vec(GDN_HEAD)],
        out_shape=[jax.ShapeDtypeStruct((rows_total, GDN_IN_PAD), MXU_DTYPE),
                   jax.ShapeDtypeStruct((SUBLANES, GDN_CONV_DIM), F32),
                   jax.ShapeDtypeStruct((1, LANES), F32), jax.ShapeDtypeStruct((1, LANES), F32),
                   jax.ShapeDtypeStruct((1, GDN_HEAD), F32)],
        scratch_shapes=[pltpu.VMEM((GDN_ROWS, LANES), F32),
                        pltpu.VMEM((GDN_ROWS, LANES), F32),
                        pltpu.VMEM((GDN_VH, GDN_HEAD, GDN_HEAD), F32),
                        pltpu.VMEM((LANES, CHUNK), F32),
                        pltpu.VMEM((GDN_ROWS + SUBLANES, GDN_CONV_DIM), F32),
                        pltpu.VMEM((CHUNK, LANES), F32),
                        pltpu.VMEM((LANES, CHUNK), F32),
                        pltpu.VMEM((CHUNK, LANES), F32)],
        compiler_params=_cparams(("arbitrary",)),
    )(proj, _pad_rows(conv_w), _gdn_lane_params(a_log), _gdn_lane_params(dt_bias), norm_w.reshape(1, -1),
      states, tmats, dmix, qkv, pre)
    dproj, dcw, dalog, ddtb, dnw = outs
    return dproj, [dcw[:GDN_CONV], dalog[0, GDN_GL:GDN_GL + GDN_VH], ddtb[0, GDN_GL:GDN_GL + GDN_VH], dnw[0]]


def chip_exchange(srcs, *, scatter, name):
    n = len(srcs)
    peers = N_SHARDS - 1

    def body(*refs):
        src_refs, out_refs = refs[:n], refs[n:2 * n]
        send_sems, recv_sems, local_sems = refs[2 * n:]
        x, y, c = (lax.axis_index(a) for a in MESH_AXES)
        me = 2 * x + y
        copies = []
        for t, (src_ref, out_ref) in enumerate(zip(src_refs, out_refs)):
            def piece(j, src_ref=src_ref):
                return src_ref.at[j] if scatter else src_ref.at[c]

            local = pltpu.make_async_copy(piece(me), out_ref.at[me], local_sems.at[t])
            local.start()
            copies.append(local)
            for k in range(1, N_SHARDS):
                px = 1 - x if k & 2 else x
                py = 1 - y if k & 1 else y
                cp = pltpu.make_async_remote_copy(
                    src_ref=piece(2 * px + py), dst_ref=out_ref.at[me], send_sem=send_sems.at[t * peers + k - 1],
                    recv_sem=recv_sems.at[t * peers + k - 1], device_id=(px, py, c),
                    device_id_type=pl.DeviceIdType.MESH)
                cp.start()
                copies.append(cp)
        for cp in copies:
            cp.wait()

    hbm = pl.BlockSpec(memory_space=pl.ANY)
    return pl.pallas_call(
        body, name=name, in_specs=[hbm] * n, out_specs=[hbm] * n,
        out_shape=[jax.ShapeDtypeStruct((N_SHARDS,) + tuple(s.shape[1:]), s.dtype) for s in srcs],
        scratch_shapes=[pltpu.SemaphoreType.DMA((n * peers,)), pltpu.SemaphoreType.DMA((n * peers,)),
                        pltpu.SemaphoreType.DMA((n,))],
    )(*srcs)


def pair_exchange(src, *, add, name, out_dtype=None):
    lead, rows, cols = src.shape
    tr = _pick(rows, (512, 256))
    nblk = rows // tr
    n_steps = nblk if add else lead * nblk

    def body(c_ref, *refs):
        if add:
            mine_ref, send_ref, o_ref, recv_ref, send_sems, recv_sems, credit = refs
        else:
            send_ref, o_ref, recv_ref, send_sems, recv_sems, credit = refs
        step = pl.program_id(0) * nblk + pl.program_id(1)
        slot = step % 2
        sibling = (lax.axis_index("x"), lax.axis_index("y"), 1 - lax.axis_index("c"))

        @pl.when(step >= 2)
        def _():
            pl.semaphore_wait(credit, 1)

        cp = pltpu.make_async_remote_copy(
            src_ref=send_ref, dst_ref=recv_ref.at[slot], send_sem=send_sems.at[slot], recv_sem=recv_sems.at[slot],
            device_id=sibling, device_id_type=pl.DeviceIdType.MESH)
        cp.start()
        cp.wait_recv()
        if add:
            o_ref[...] = (mine_ref[...] + recv_ref[slot]).astype(o_ref.dtype)
        else:
            o_ref[c_ref[0]] = send_ref[...]
            o_ref[1 - c_ref[0]] = recv_ref[slot]
        cp.wait_send()

        @pl.when(step + 2 < n_steps)
        def _():
            pl.semaphore_signal(credit, 1, device_id=sibling, device_id_type=pl.DeviceIdType.MESH)

    flat = src.reshape(lead * rows, cols)
    if add:
        in_specs = [pl.BlockSpec((tr, cols), lambda s, i, c_ref: (c_ref[0] * nblk + i, 0)),
                    pl.BlockSpec((tr, cols), lambda s, i, c_ref: ((1 - c_ref[0]) * nblk + i, 0))]
        out_specs = pl.BlockSpec((tr, cols), lambda s, i, c_ref: (i, 0))
        out_shape = jax.ShapeDtypeStruct((rows, cols), src.dtype if out_dtype is None else out_dtype)
        grid, args = (1, nblk), (flat, flat)
    else:
        in_specs = [pl.BlockSpec((tr, cols), lambda s, i, c_ref: (s * nblk + i, 0))]
        out_specs = pl.BlockSpec((2, tr, cols), lambda s, i, c_ref: (s, i, 0))
        out_shape = jax.ShapeDtypeStruct((lead * 2, rows, cols), src.dtype)
        grid, args = (lead, nblk), (flat,)
    out = pl.pallas_call(
        body, name=name, out_shape=out_shape,
        grid_spec=pltpu.PrefetchScalarGridSpec(
            num_scalar_prefetch=1, grid=grid, in_specs=in_specs, out_specs=out_specs,
            scratch_shapes=[pltpu.VMEM((2, tr, cols), src.dtype), pltpu.SemaphoreType.DMA((2,)),
                            pltpu.SemaphoreType.DMA((2,)), pltpu.SemaphoreType.REGULAR]),
        compiler_params=_cparams(("arbitrary", "arbitrary")),
    )(lax.axis_index("c").astype(jnp.int32).reshape(1), *args)
    return out if add else out.reshape(lead, 2, rows, cols)


def sum_slots(buf, *, name):
    n, rows, cols = buf.shape
    tr = _pick(rows, (512, 256, 128))

    def body(b_ref, o_ref):
        acc = b_ref[0].astype(F32)
        for j in range(1, n):
            acc = acc + b_ref[j].astype(F32)
        o_ref[...] = acc

    return pl.pallas_call(
        body, name=name, grid=(rows // tr,), in_specs=[pl.BlockSpec((n, tr, cols), lambda i: (0, i, 0))],
        out_specs=pl.BlockSpec((tr, cols), lambda i: (i, 0)), out_shape=jax.ShapeDtypeStruct((rows, cols), F32),
        compiler_params=_cparams(("parallel",)),
    )(buf)


def adamw(w, g, m, v, *, name):
    shape = w.shape
    cols = shape[-1]
    rows = _size(shape) // cols
    w, g, m, v = (t.reshape(rows, cols) for t in (w, g, m, v))
    tr = 256 if rows % 256 == 0 else rows

    def body(w_ref, g_ref, m_ref, v_ref, d_ref, mo_ref, vo_ref):
        gv = g_ref[...]
        mn = ADAM_B1 * m_ref[...] + (1.0 - ADAM_B1) * gv
        vn = ADAM_B2 * v_ref[...] + (1.0 - ADAM_B2) * (gv * gv)
        m_hat = mn / (1.0 - ADAM_B1 ** ADAM_STEP)
        v_hat = vn / (1.0 - ADAM_B2 ** ADAM_STEP)
        d_ref[...] = -ADAM_LR * (m_hat / (jnp.sqrt(v_hat) + ADAM_EPS) + ADAM_WD * w_ref[...])
        mo_ref[...] = mn
        vo_ref[...] = vn

    blk = pl.BlockSpec((tr, cols), lambda i: (i, 0))
    shp = jax.ShapeDtypeStruct((rows, cols), F32)
    outs = pl.pallas_call(
        body, name=name, grid=(rows // tr,), in_specs=[blk] * 4, out_specs=[blk] * 3, out_shape=[shp] * 3,
        compiler_params=_cparams(("parallel",)),
    )(w, g, m, v)
    return [o.reshape(shape) for o in outs]


N_SHARDS = 4
FLAT_COLS = 1024
W_SPECS = (
    ("gdn_w_in", (2, 1024, 6176), 2), ("gdn_conv_w", (2, 4, 4096), 2), ("gdn_a_log", (2, 16), None),
    ("gdn_dt_bias", (2, 16), None), ("gdn_norm_w", (2, 128), None), ("gdn_w_out", (2, 2048, 1024), 1),
    ("sc_w_in", (1, 1024, 8192), 2), ("sc_conv_w", (1, 3, 2048), 2), ("sc_w_out", (1, 2048, 1024), 1),
    ("ssd_w_in", (1, 1024, 5152), 2), ("ssd_conv_w", (1, 4, 3072), 2), ("ssd_conv_b", (1, 3072), 1),
    ("ssd_a_log", (1, 32), None), ("ssd_dt_bias", (1, 32), None), ("ssd_d_skip", (1, 32), None),
    ("ssd_norm_w", (1, 2048), 1), ("ssd_w_out", (1, 2048, 1024), 1), ("ln_g", (4, 1024), None), ("ln_b", (4, 1024), None),
)


def _local_shape(shape, axis):
    return shape if axis is None else tuple(d // N_SHARDS if i == axis else d for i, d in enumerate(shape))


def _size(shape):
    n = 1
    for d in shape:
        n *= d
    return n


PIECE_ROWS = 16


def _piece_rows(shape, axis):
    return -(-_size(_local_shape(shape, axis)) // (FLAT_COLS * PIECE_ROWS)) * PIECE_ROWS


def _flat_rows(specs):
    return -(-sum(_piece_rows(s, a) for _, s, a in specs) // 512) * 512


def _pack(pieces, specs, dtype=F32):
    blocks, used = [], 0
    for p, (_, shape, axis) in zip(pieces, specs):
        rows = _piece_rows(shape, axis)
        flat = p.reshape(-1).astype(dtype)
        if flat.shape[0] < rows * FLAT_COLS:
            flat = jnp.pad(flat, (0, rows * FLAT_COLS - flat.shape[0]))
        blocks.append(flat.reshape(rows, FLAT_COLS))
        used += rows
    blocks.append(jnp.zeros((_flat_rows(specs) - used, FLAT_COLS), dtype))
    return jnp.concatenate(blocks, axis=0)


def _unpack(flat, specs):
    out, off = [], 0
    for _, shape, axis in specs:
        ls = _local_shape(shape, axis)
        rows = _piece_rows(shape, axis)
        out.append(flat[off:off + rows].reshape(-1)[:_size(ls)].reshape(ls))
        off += rows
    return out


def _shard_of(full, axis, s):
    if axis is None:
        return full
    n = full.shape[axis] // N_SHARDS
    return lax.slice_in_dim(full, s * n, (s + 1) * n, axis=axis)


def _adamw_all(weights, grads, moms, vels):
    steps = [adamw(w, g, m, v, name="adamw") for w, g, m, v in zip(weights, grads, moms, vels)]
    return grads, [s[0] for s in steps], [s[1] for s in steps], [s[2] for s in steps]


SPLIT_ROWS = 128


def shard_split(w, n_real, *, name):
    rows, n_pad = w.shape
    ns = n_real // N_SHARDS

    def body(w_ref, o_ref):
        for s in range(N_SHARDS):
            o_ref[s] = w_ref[:, s * ns:(s + 1) * ns]

    return pl.pallas_call(
        body, name=name, grid=(rows // SPLIT_ROWS,),
        in_specs=[pl.BlockSpec((SPLIT_ROWS, n_pad), lambda i: (i, 0))],
        out_specs=pl.BlockSpec((N_SHARDS, SPLIT_ROWS, ns), lambda i: (0, i, 0)),
        out_shape=jax.ShapeDtypeStruct((N_SHARDS, rows, ns), F32), compiler_params=_cparams(("parallel",)),
    )(w)


def shard_merge(pieces, n_pad, *, name):
    _, rows, ns = pieces.shape
    n_real = ns * N_SHARDS

    def body(p_ref, o_ref):
        for s in range(N_SHARDS):
            o_ref[:, s * ns:(s + 1) * ns] = p_ref[s].astype(o_ref.dtype)
        if n_pad > n_real:
            o_ref[:, n_real:] = jnp.zeros((SPLIT_ROWS, n_pad - n_real), o_ref.dtype)

    return pl.pallas_call(
        body, name=name, grid=(rows // SPLIT_ROWS,),
        in_specs=[pl.BlockSpec((N_SHARDS, SPLIT_ROWS, ns), lambda i: (0, i, 0))],
        out_specs=pl.BlockSpec((SPLIT_ROWS, n_pad), lambda i: (i, 0)),
        out_shape=jax.ShapeDtypeStruct((rows, n_pad), MXU_DTYPE), compiler_params=_cparams(("parallel",)),
    )(pieces)


def _reduce_scatter(full_grads):
    def shard(g, spec, s):
        _, shape, axis = spec
        return g[:, s] if g.ndim == len(shape) + 1 else _shard_of(g, axis, s)

    def pair_sums(idx, wire_dtype, tag):
        specs = [W_SPECS[i] for i in idx]
        half = _flat_rows(specs) // 2
        by_shard = jnp.stack([_pack([shard(full_grads[i], W_SPECS[i], s) for i in idx], specs)
                              for s in range(N_SHARDS)])
        by_half = by_shard.reshape(N_SHARDS, 2, half, FLAT_COLS).transpose(1, 0, 2, 3)
        by_half = by_half.reshape(2, N_SHARDS * half, FLAT_COLS)
        pair_sum = pair_exchange(by_half, add=True, out_dtype=wire_dtype, name="rs_pair_" + tag)
        return pair_sum.reshape(N_SHARDS, half, FLAT_COLS)

    def finish(idx, chips, tag):
        specs = [W_SPECS[i] for i in idx]
        summed = sum_slots(chips, name="rs_chip_sum_" + tag)
        both = pair_exchange(summed[None], add=False, name="rs_halves_" + tag)
        return dict(zip(idx, _unpack(both.reshape(_flat_rows(specs), FLAT_COLS), specs)))

    matrices = [i for i, (n, _, _) in enumerate(W_SPECS) if n in MXU_WEIGHTS]
    others = [i for i, (n, _, _) in enumerate(W_SPECS) if n not in MXU_WEIGHTS]
    chips_m, chips_o = chip_exchange([pair_sums(matrices, MXU_DTYPE, "mxu"), pair_sums(others, F32, "f32")],
                                     scatter=True, name="rs_chips")
    grads = {**finish(matrices, chips_m, "mxu"), **finish(others, chips_o, "f32")}
    return [grads[i] for i in range(len(W_SPECS))]


def _gather_weights(local_weights):
    def halves(idx, dtype):
        specs = [W_SPECS[i] for i in idx]
        return _pack([local_weights[i] for i in idx], specs, dtype).reshape(2, _flat_rows(specs) // 2, FLAT_COLS)

    def finish(idx, from_chips, tag):
        specs = [W_SPECS[i] for i in idx]
        both = pair_exchange(from_chips, add=False, name="gather_pair_" + tag)
        both = both.reshape(N_SHARDS, _flat_rows(specs), FLAT_COLS)
        return [dict(zip(idx, _unpack(both[s], specs))) for s in range(N_SHARDS)]

    matrices = [i for i, (n, _, _) in enumerate(W_SPECS) if n in MXU_WEIGHTS]
    vectors = [i for i, (n, _, a) in enumerate(W_SPECS) if n not in MXU_WEIGHTS and a is not None]
    chips_m, chips_v = chip_exchange([halves(matrices, MXU_DTYPE), halves(vectors, F32)], scatter=False,
                                     name="gather_chips")
    per_shard = [{**m, **v} for m, v in zip(finish(matrices, chips_m, "mxu"), finish(vectors, chips_v, "f32"))]
    full = []
    for i, (wname, shape, axis) in enumerate(W_SPECS):
        if axis is None:
            full.append(local_weights[i])
        elif wname in W_IN_PAD:
            pieces = jnp.stack([per_shard[s][i] for s in range(N_SHARDS)], axis=1)
            full.append([shard_merge(pieces[j], W_IN_PAD[wname], name="merge_" + wname) for j in range(shape[0])])
        else:
            full.append(jnp.concatenate([per_shard[s][i] for s in range(N_SHARDS)], axis=axis))
    return full


W_IN_PAD = {"gdn_w_in": GDN_IN_PAD, "sc_w_in": SC_IN, "ssd_w_in": SSD_IN_PAD}
MXU_WEIGHTS = ("gdn_w_in", "gdn_w_out", "sc_w_in", "sc_w_out", "ssd_w_in", "ssd_w_out")


def kernel(x, gdn_w_in, gdn_conv_w, gdn_a_log, gdn_dt_bias, gdn_norm_w, gdn_w_out, sc_w_in, sc_conv_w, sc_w_out, ssd_w_in, ssd_conv_w, ssd_conv_b, ssd_a_log, ssd_dt_bias, ssd_d_skip, ssd_norm_w, ssd_w_out, ln_g, ln_b, loss_target, m_gdn_w_in, m_gdn_conv_w, m_gdn_a_log, m_gdn_dt_bias, m_gdn_norm_w, m_gdn_w_out, m_sc_w_in, m_sc_conv_w, m_sc_w_out, m_ssd_w_in, m_ssd_conv_w, m_ssd_conv_b, m_ssd_a_log, m_ssd_dt_bias, m_ssd_d_skip, m_ssd_norm_w, m_ssd_w_out, m_ln_g, m_ln_b, v_gdn_w_in, v_gdn_conv_w, v_gdn_a_log, v_gdn_dt_bias, v_gdn_norm_w, v_gdn_w_out, v_sc_w_in, v_sc_conv_w, v_sc_w_out, v_ssd_w_in, v_ssd_conv_w, v_ssd_conv_b, v_ssd_a_log, v_ssd_dt_bias, v_ssd_d_skip, v_ssd_norm_w, v_ssd_w_out, v_ln_g, v_ln_b):
    weights = [gdn_w_in, gdn_conv_w, gdn_a_log, gdn_dt_bias, gdn_norm_w, gdn_w_out, sc_w_in, sc_conv_w, sc_w_out,
               ssd_w_in, ssd_conv_w, ssd_conv_b, ssd_a_log, ssd_dt_bias, ssd_d_skip, ssd_norm_w, ssd_w_out, ln_g, ln_b]
    moms = [m_gdn_w_in, m_gdn_conv_w, m_gdn_a_log, m_gdn_dt_bias, m_gdn_norm_w, m_gdn_w_out, m_sc_w_in, m_sc_conv_w,
            m_sc_w_out, m_ssd_w_in, m_ssd_conv_w, m_ssd_conv_b, m_ssd_a_log, m_ssd_dt_bias, m_ssd_d_skip, m_ssd_norm_w,
            m_ssd_w_out, m_ln_g, m_ln_b]
    vels = [v_gdn_w_in, v_gdn_conv_w, v_gdn_a_log, v_gdn_dt_bias, v_gdn_norm_w, v_gdn_w_out, v_sc_w_in, v_sc_conv_w,
            v_sc_w_out, v_ssd_w_in, v_ssd_conv_w, v_ssd_conv_b, v_ssd_a_log, v_ssd_dt_bias, v_ssd_d_skip, v_ssd_norm_w,
            v_ssd_w_out, v_ln_g, v_ln_b]
    full = dict(zip([n for n, _, _ in W_SPECS], _gather_weights(weights)))
    x0 = x[0]
    target = loss_target[0]

    layers = (("gdn", 0, GDN_IN_PAD, GDN_IN), ("sc", 0, SC_IN, SC_IN), ("ssd", 0, SSD_IN_PAD, SSD_IN), ("gdn", 1, GDN_IN_PAD, GDN_IN))

    def params(kind, j):
        if kind == "gdn":
            return [full["gdn_conv_w"][j], full["gdn_a_log"][j], full["gdn_dt_bias"][j], full["gdn_norm_w"][j]]
        if kind == "sc":
            return [full["sc_conv_w"][j]]
        return [full["ssd_conv_w"][j], full["ssd_conv_b"][j], full["ssd_a_log"][j], full["ssd_dt_bias"][j],
                full["ssd_d_skip"][j], full["ssd_norm_w"][j]]

    xs, xs_c, saved = [x0], [x0.astype(MXU_DTYPE)], []
    for i, (kind, j, n_pad, _) in enumerate(layers):
        w_in = full[kind + "_w_in"][j]
        w_out = full[kind + "_w_out"][j].astype(MXU_DTYPE)
        proj = matmul(xs_c[i], w_in, name=kind + "_proj")
        if kind == "gdn":
            mix, states = gdn_fwd(proj, *params(kind, j), name="gdn_fwd")
        elif kind == "sc":
            mix, states = sc_fwd(proj, *params(kind, j), name="sc_fwd"), None
        else:
            mix, states = ssd_fwd(proj, *params(kind, j), name="ssd_fwd")
        y = matmul(mix, w_out, name=kind + "_out")
        saved.append((w_in, w_out, proj, mix, states, y))
        if i + 1 < DEPTH:
            xn, xn_c = ln_fwd(xs[i], y, full["ln_g"][i], full["ln_b"][i], name="ln_fwd")
            xs.append(xn)
            xs_c.append(xn_c)

    grads = {n: [None] * s[0] for n, s, _ in W_SPECS}
    dr, dg, db, loss_rows = ln_bwd(xs[DEPTH - 1], saved[DEPTH - 1][5], full["ln_g"][DEPTH - 1], b=full["ln_b"][DEPTH - 1],
                                   target=target, name="ln_bwd_loss")
    dx = None
    for i in reversed(range(DEPTH)):
        kind, j, _, n_in = layers[i]
        w_in, w_out, proj, mix, states, _ = saved[i]
        grads["ln_g"][i], grads["ln_b"][i] = dg[0], db[0]
        dmix = matmul(dr, w_out, tb=True, name=kind + "_dmix")
        grads[kind + "_w_out"][j] = matmul(mix, dr, ta=True, name=kind + "_dw_out")
        if kind == "gdn":
            dproj, (dcw, dalog, ddtb, dnw) = gdn_bwd(proj, *params(kind, j), states, dmix, name="gdn_bwd")
            grads["gdn_conv_w"][j], grads["gdn_a_log"][j], grads["gdn_dt_bias"][j], grads["gdn_norm_w"][j] = dcw, dalog, ddtb, dnw
        elif kind == "sc":
            dproj, dcw = sc_bwd(proj, *params(kind, j), dmix, name="sc_bwd")
            grads["sc_conv_w"][j] = dcw[:SC_CONV]
        else:
            conv_w, _, *rest = params(kind, j)
            dproj, (dcw, dcb, dalog, ddtb, ddsk, dnw) = ssd_bwd(proj, conv_w, *rest, states, dmix, name="ssd_bwd")
            grads["ssd_conv_w"][j], grads["ssd_conv_b"][j], grads["ssd_a_log"][j] = dcw, dcb, dalog
            grads["ssd_dt_bias"][j], grads["ssd_d_skip"][j], grads["ssd_norm_w"][j] = ddtb, ddsk, dnw
        grads[kind + "_w_in"][j] = shard_split(matmul(xs_c[i], dproj, ta=True, name=kind + "_dw_in"), n_in, name="split_" + kind)
        dx = matmul(dproj, w_in, tb=True, add=dr, add_scale=ALPHA, name=kind + "_dx")
        if i > 0:
            dr, dg, db = ln_bwd(xs[i - 1], saved[i - 1][5], full["ln_g"][i - 1], dx, name="ln_bwd")

    full_grads = [jnp.stack(grads[n]) for n, _, _ in W_SPECS]
    g_out, d_out, m_out, v_out = _adamw_all(weights, _reduce_scatter(full_grads), moms, vels)
    loss = lax.psum(loss_rows[0, 0], MESH_AXES)
    return (loss, dx[None], *g_out, *d_out, *m_out, *v_out)
```
